```python
import math
import jax, jax.numpy as jnp
from jax import lax
import numpy as np

D_MODEL = 2048
BATCH = 4
SEQ = 4096
DEPTH = 4

HEAD_DIM = 128
N_BRANCH = 4
RET_HEADS = 4
RET_WIDTH = RET_HEADS * HEAD_DIM
RET_CHUNK = 128
ROPE_BASE = 10000.0
DSA_HEADS = 4
DSA_WIDTH = DSA_HEADS * HEAD_DIM
DSA_Q_RANK = 384
IDX_HEADS = 16
IDX_DIM = 64
TOPK_MAX = 256
FOX_HEADS = 4
FOX_WIDTH = FOX_HEADS * HEAD_DIM
GDN_HEADS = 4
GDN_WIDTH = GDN_HEADS * HEAD_DIM
GDN_CONV = 4
GDN_CHUNK = 64
Q_BLOCK = 128
REL_BUCKETS = 32
REL_MAX_DIST = 128
D_FF = 5632
FFN_CONV = 3
EPS = 1e-6

IN_SIZES = (RET_WIDTH, RET_WIDTH, RET_WIDTH, RET_WIDTH,
            DSA_Q_RANK, HEAD_DIM, HEAD_DIM, IDX_DIM, IDX_HEADS,
            FOX_WIDTH, FOX_WIDTH, FOX_WIDTH, FOX_HEADS,
            GDN_WIDTH, GDN_WIDTH, GDN_WIDTH, GDN_WIDTH, GDN_HEADS, GDN_HEADS,
            N_BRANCH * D_MODEL)
IN_COLS = sum(IN_SIZES)

kernel_name = "hybrid_gated_parallel_mixers"

F32 = jnp.float32


def rmsnorm(x, g):
    xf = x.astype(F32)
    y = xf * lax.rsqrt(jnp.mean(xf * xf, axis=-1, keepdims=True) + EPS)
    return (y * g.astype(F32)).astype(x.dtype)


def l2norm(x):
    return x * lax.rsqrt(jnp.sum(x * x, axis=-1, keepdims=True) + EPS)


def causal_dwconv(x, w):
    k, s = w.shape[0], x.shape[1]
    xp = jnp.pad(x, ((0, 0), (k - 1, 0), (0, 0)))
    return sum(xp[:, i:i + s] * w[i] for i in range(k))


def rotary(x, pos):
    half = x.shape[-1] // 2
    inv = 1.0 / (ROPE_BASE ** (jnp.arange(half, dtype=F32) / half))
    ang = pos.astype(F32)[:, None] * inv[None, :]
    cos, sin = jnp.cos(ang)[None, :, None, :], jnp.sin(ang)[None, :, None, :]
    xf = x.astype(F32)
    x1, x2 = xf[..., :half], xf[..., half:]
    return jnp.concatenate([x1 * cos - x2 * sin, x1 * sin + x2 * cos], axis=-1).astype(x.dtype)


def t5_bucket(rel):
    rel = jnp.maximum(rel, 0)
    max_exact = REL_BUCKETS // 2
    relf = jnp.maximum(rel, max_exact).astype(F32)
    large = max_exact + (jnp.log(relf / max_exact) / math.log(REL_MAX_DIST / max_exact)
                         * (REL_BUCKETS - max_exact)).astype(jnp.int32)
    large = jnp.minimum(large, REL_BUCKETS - 1)
    return jnp.where(rel < max_exact, rel, large)


def query_blocks(t):
    b, s = t.shape[0], t.shape[1]
    return jnp.moveaxis(t.reshape(b, s // Q_BLOCK, Q_BLOCK, *t.shape[2:]), 1, 0)


def retention(q, k, v, gate):
    b, s, _ = q.shape
    c = RET_CHUNK
    nc = s // c
    pos = jnp.arange(s)
    heads = lambda t: t.reshape(b, s, RET_HEADS, HEAD_DIM)
    chunks = lambda t: t.reshape(b, nc, c, RET_HEADS, HEAD_DIM)
    qf = chunks(rotary(heads(q), pos).astype(F32))
    kf = chunks(rotary(heads(k), pos).astype(F32) * HEAD_DIM ** -0.5)
    vf = chunks(heads(v).astype(F32))
    log_gamma = jnp.log1p(-jnp.exp2(-5.0 - jnp.arange(RET_HEADS, dtype=F32)))
    n = jnp.arange(c, dtype=F32)
    diff = n[:, None] - n[None, :]
    decay = jnp.where(diff >= 0, jnp.exp(log_gamma[:, None, None] * jnp.maximum(diff, 0.0)), 0.0)
    inner = jnp.einsum('bnqhd,bnkhd->bnhqk', qf, kf) * decay
    o_inner = jnp.einsum('bnhqk,bnkhd->bnqhd', inner, vf)
    zeta = jnp.exp(log_gamma[None, :] * (c - 1 - n)[:, None])
    chunk_kv = jnp.einsum('bnchd,bnche->bnhde', kf * zeta[:, :, None], vf)
    gamma_c = jnp.exp(log_gamma * c)[:, None, None]

    def step(r, kv):
        return r * gamma_c + kv, r

    _, r_prev = lax.scan(step, jnp.zeros_like(chunk_kv[:, 0]), jnp.moveaxis(chunk_kv, 1, 0))
    r_prev = jnp.moveaxis(r_prev, 0, 1)
    xi = jnp.exp(log_gamma[None, :] * (n + 1)[:, None])
    o_cross = jnp.einsum('bnchd,bnhde->bnche', qf, r_prev) * xi[:, :, None]
    o = (o_inner + o_cross).reshape(b, s, RET_HEADS, HEAD_DIM)
    mu = jnp.mean(o, axis=-1, keepdims=True)
    var = jnp.mean((o - mu) ** 2, axis=-1, keepdims=True)
    o = ((o - mu) * lax.rsqrt(var + EPS)).reshape(b, s, RET_WIDTH)
    return (jax.nn.silu(gate.astype(F32)) * o).astype(gate.dtype)


def dsa_attention(c_q, k, v, k_idx, w_idx, cq_norm, w_uq, w_qidx, rel_bias):
    b, s, _ = c_q.shape
    c_q = rmsnorm(c_q, cq_norm)
    q = (c_q @ w_uq).reshape(b, s, DSA_HEADS, HEAD_DIM)
    q_idx = (c_q @ w_qidx).reshape(b, s, IDX_HEADS, IDX_DIM)
    w_idx = w_idx * (IDX_HEADS ** -0.5 * IDX_DIM ** -0.5)
    topk = min(TOPK_MAX, s // 4)
    nb = s // Q_BLOCK
    key_pos = jnp.arange(s)
    gather = jax.vmap(lambda table, ids: table[ids])

    def block(args):
        q_b, qi_b, wi_b, start = args
        t = start + jnp.arange(Q_BLOCK)
        causal = key_pos[None, :] <= t[:, None]
        score = jnp.einsum('bqhs,bqh->bqs',
                           jax.nn.relu(jnp.einsum('bqhd,bsd->bqhs', qi_b, k_idx)), wi_b).astype(F32)
        score = jnp.where(causal[None], score, -jnp.inf)
        _, idx = lax.top_k(score, topk)
        valid = idx <= t[None, :, None]
        k_sel = gather(k, idx)
        v_sel = gather(v, idx)
        bias = rel_bias[t5_bucket(t[None, :, None] - idx)]
        logits = (jnp.einsum('bqhd,bqkd->bhqk', q_b, k_sel).astype(F32) * HEAD_DIM ** -0.5
                  + jnp.transpose(bias, (0, 3, 1, 2)).astype(F32))
        logits = jnp.where(valid[:, None], logits, -jnp.inf)
        p = jax.nn.softmax(logits, axis=-1)
        return jnp.einsum('bhqk,bqkd->bqhd', p.astype(v.dtype), v_sel)

    out = lax.map(block, (query_blocks(q), query_blocks(q_idx), query_blocks(w_idx),
                          jnp.arange(nb, dtype=jnp.int32) * Q_BLOCK))
    return jnp.moveaxis(out, 0, 1).reshape(b, s, DSA_WIDTH)


def forgetting_attention(q, k, v, f_logit):
    b, s, _ = q.shape
    heads = lambda t: t.reshape(b, s, FOX_HEADS, HEAD_DIM)
    q, k, v = heads(q), heads(k), heads(v)
    c = jnp.cumsum(jax.nn.log_sigmoid(f_logit.astype(F32)), axis=1)
    c_keys = jnp.moveaxis(c, 1, 2)
    key_pos = jnp.arange(s)
    nb = s // Q_BLOCK

    def block(args):
        q_b, c_b, start = args
        t = start + jnp.arange(Q_BLOCK)
        causal = key_pos[None, :] <= t[:, None]
        forget_bias = jnp.moveaxis(c_b, 1, 2)[..., :, None] - c_keys[:, :, None, :]
        logits = jnp.einsum('bqhd,bshd->bhqs', q_b, k).astype(F32) * HEAD_DIM ** -0.5 + forget_bias
        logits = jnp.where(causal[None, None], logits, -jnp.inf)
        p = jax.nn.softmax(logits, axis=-1)
        return jnp.einsum('bhqs,bshd->bqhd', p.astype(v.dtype), v)

    out = lax.map(block, (query_blocks(q), query_blocks(c), jnp.arange(nb, dtype=jnp.int32) * Q_BLOCK))
    return jnp.moveaxis(out, 0, 1).reshape(b, s, FOX_WIDTH)


def gated_deltanet(q, k, v, z, beta_logit, a_logit, conv_w, a_log, dt_bias, norm_g):
    b, s, _ = q.shape
    c = GDN_CHUNK
    nc = s // c
    qkv = jax.nn.silu(causal_dwconv(jnp.concatenate([q, k, v], axis=-1), conv_w))
    q, k, v = jnp.split(qkv, 3, axis=-1)
    chunks = lambda t: t.reshape(b, nc, c, GDN_HEADS, HEAD_DIM).transpose(0, 3, 1, 2, 4).astype(F32)
    chunks_h = lambda t: t.reshape(b, nc, c, GDN_HEADS).transpose(0, 3, 1, 2).astype(F32)
    qf = l2norm(chunks(q)) * HEAD_DIM ** -0.5
    kf = l2norm(chunks(k))
    vf = chunks(v)
    beta = chunks_h(jax.nn.sigmoid(beta_logit.astype(F32)))
    g = chunks_h(-jnp.exp(a_log.astype(F32)) * jax.nn.softplus(a_logit.astype(F32) + dt_bias.astype(F32)))
    g = jnp.cumsum(g, axis=-1)
    tril = jnp.tril(jnp.ones((c, c), dtype=bool))
    strict = jnp.tril(jnp.ones((c, c), dtype=bool), -1)
    decay = jnp.exp(jnp.where(tril, g[..., :, None] - g[..., None, :], -jnp.inf))
    kk = jnp.einsum('bhncd,bhnsd->bhncs', kf, kf)
    a_mat = jnp.eye(c, dtype=F32) + jnp.where(strict, beta[..., :, None] * kk * decay, 0.0)
    rhs = jnp.concatenate([vf * beta[..., None], kf * (beta * jnp.exp(g))[..., None]], axis=-1)
    sol = lax.linalg.triangular_solve(a_mat, rhs, left_side=True, lower=True)
    u0, kcum = jnp.split(sol, 2, axis=-1)
    qk = jnp.einsum('bhncd,bhnsd->bhncs', qf, kf) * decay
    q_dec = qf * jnp.exp(g)[..., None]
    g_last = g[..., -1]
    k_dec = kf * jnp.exp(g_last[..., None] - g)[..., None]

    def step(state, xs):
        u0_c, kcum_c, qk_c, qdec_c, kdec_c, glast_c = xs
        v_new = u0_c - jnp.einsum('bhcd,bhde->bhce', kcum_c, state)
        o = jnp.einsum('bhcd,bhde->bhce', qdec_c, state) + jnp.einsum('bhcs,bhse->bhce', qk_c, v_new)
        state = state * jnp.exp(glast_c)[..., None, None] + jnp.einsum('bhcd,bhce->bhde', kdec_c, v_new)
        return state, o

    xs = tuple(jnp.moveaxis(t, 2, 0) for t in (u0, kcum, qk, q_dec, k_dec, g_last))
    _, o = lax.scan(step, jnp.zeros((b, GDN_HEADS, HEAD_DIM, HEAD_DIM), F32), xs)
    o = o.transpose(1, 0, 3, 2, 4).reshape(b, s, GDN_HEADS, HEAD_DIM)
    o = rmsnorm(o, norm_g).reshape(b, s, GDN_WIDTH)
    return (o * jax.nn.silu(z.astype(F32))).astype(z.dtype)


def hybrid_mixer(h, w_in, dsa_cq_norm, dsa_w_uq, dsa_w_qidx, fox_f_bias,
                 gdn_conv, gdn_a_log, gdn_dt_bias, gdn_norm, w_branch, w_out, rel_bias):
    b, s, _ = h.shape
    proj = h @ w_in
    split_at = np.cumsum(IN_SIZES)[:-1].tolist()
    (r_q, r_k, r_v, r_g,
     d_cq, d_k, d_v, i_k, i_w,
     f_q, f_k, f_v, f_f,
     g_q, g_k, g_v, g_z, g_b, g_a,
     mix_gates) = jnp.split(proj, split_at, axis=-1)
    o_ret = retention(r_q, r_k, r_v, r_g)
    o_dsa = dsa_attention(d_cq, d_k, d_v, i_k, i_w, dsa_cq_norm, dsa_w_uq, dsa_w_qidx, rel_bias)
    o_fox = forgetting_attention(f_q, f_k, f_v, f_f + fox_f_bias)
    o_gdn = gated_deltanet(g_q, g_k, g_v, g_z, g_b, g_a, gdn_conv, gdn_a_log, gdn_dt_bias, gdn_norm)
    branches = jnp.stack([o_ret, o_dsa, o_fox, o_gdn], axis=2)
    wide = jnp.einsum('bsnw,nwd->bsnd', branches, w_branch)
    gates = jax.nn.sigmoid(mix_gates.reshape(b, s, N_BRANCH, D_MODEL))
    return jnp.einsum('bsnd,bsnd->bsd', gates, wide) @ w_out


def conv_ffn(h, w_gate, w_up, conv_w, conv_b, w_down):
    a = jax.nn.silu(causal_dwconv(h @ w_gate, conv_w) + conv_b)
    return (a * (h @ w_up)) @ w_down


def setup_inputs(seed: int = 0) -> dict:
    key = jax.random.key(seed)
    ks = jax.random.split(key, 24)
    L, D = DEPTH, D_MODEL
    nrm = lambda k, shape, scale: jax.random.normal(k, shape, F32) * scale
    gain = lambda k, shape: 1.0 + 0.02 * jax.random.normal(k, shape, F32)
    dt = jax.random.uniform(ks[10], (L, GDN_HEADS), F32, 1e-3, 1e-1)
    return {
        "x": nrm(ks[0], (BATCH, SEQ, D), 1.0),
        "norm_mix": gain(ks[1], (L, D)),
        "w_in": nrm(ks[2], (L, D, IN_COLS), D ** -0.5),
        "dsa_cq_norm": gain(ks[3], (L, DSA_Q_RANK)),
        "dsa_w_uq": nrm(ks[4], (L, DSA_Q_RANK, DSA_WIDTH), DSA_Q_RANK ** -0.5),
        "dsa_w_qidx": nrm(ks[5], (L, DSA_Q_RANK, IDX_HEADS * IDX_DIM), DSA_Q_RANK ** -0.5),
        "fox_f_bias": 3.0 + 0.5 * jax.random.normal(ks[6], (L, FOX_HEADS), F32),
        "gdn_conv": nrm(ks[7], (L, GDN_CONV, 3 * GDN_WIDTH), GDN_CONV ** -0.5),
        "gdn_a_log": jnp.log(jax.random.uniform(ks[8], (L, GDN_HEADS), F32, 1.0, 16.0)),
        "gdn_dt_bias": jnp.log(jnp.expm1(dt)),
        "gdn_norm": gain(ks[9], (L, HEAD_DIM)),
        "w_branch": nrm(ks[11], (L, N_BRANCH, 512, D), 512 ** -0.5),
        "w_out": nrm(ks[12], (L, D, D), D ** -0.5),
        "rel_bias": nrm(ks[13], (REL_BUCKETS, DSA_HEADS), 0.5),
        "norm_ffn": gain(ks[14], (L, D)),
        "ffn_w_gate": nrm(ks[15], (L, D, D_FF), D ** -0.5),
        "ffn_w_up": nrm(ks[16], (L, D, D_FF), D ** -0.5),
        "ffn_conv": nrm(ks[17], (L, FFN_CONV, D_FF), FFN_CONV ** -0.5),
        "ffn_conv_b": nrm(ks[18], (L, D_FF), 0.02),
        "ffn_w_down": nrm(ks[19], (L, D_FF, D), D_FF ** -0.5),
        "final_norm": gain(ks[20], (D,)),
    }


def reference(x, norm_mix, w_in, dsa_cq_norm, dsa_w_uq, dsa_w_qidx, fox_f_bias,
              gdn_conv, gdn_a_log, gdn_dt_bias, gdn_norm, w_branch, w_out, rel_bias,
              norm_ffn, ffn_w_gate, ffn_w_up, ffn_conv, ffn_conv_b, ffn_w_down, final_norm):
    for l in range(DEPTH):
        x = x + hybrid_mixer(rmsnorm(x, norm_mix[l]), w_in[l], dsa_cq_norm[l], dsa_w_uq[l],
                             dsa_w_qidx[l], fox_f_bias[l], gdn_conv[l], gdn_a_log[l],
                             gdn_dt_bias[l], gdn_norm[l], w_branch[l], w_out[l], rel_bias)
        x = x + conv_ffn(rmsnorm(x, norm_ffn[l]), ffn_w_gate[l], ffn_w_up[l], ffn_conv[l],
                         ffn_conv_b[l], ffn_w_down[l])
    return rmsnorm(x, final_norm)
```

```python
import functools
import math

import jax
import jax.numpy as jnp
from jax import lax
from jax.experimental import pallas as pl
from jax.experimental.pallas import tpu as pltpu

F32 = jnp.float32
BF16 = jnp.bfloat16
I32 = jnp.int32

HEAD_DIM = 128
N_HEADS = 4
WIDTH = N_HEADS * HEAD_DIM
N_BRANCH = 4
RET_CHUNK = 128
ROPE_BASE = 10000.0
DSA_Q_RANK = 384
IDX_HEADS = 16
IDX_DIM = 64
TOPK_MAX = 256
GDN_CONV = 4
GDN_CHUNK = 64
REL_BUCKETS = 32
REL_MAX_DIST = 128
FFN_CONV = 3
EPS = 1e-6

LANE = 128
VMEM_LIMIT = 56 * 1024 * 1024

C_RQ, C_RK, C_RV, C_RG = 0, 512, 1024, 1536
C_FQ, C_FK, C_FV = 2048, 2560, 3072
C_GQ, C_GK, C_GV, C_GZ = 3584, 4096, 4608, 5120
C_DK, C_DCQ, C_DV, C_KA, C_KB, C_SM = 5632, 5760, 6144, 6272, 6400, 6528
C_TOT = 6656
L_IW, L_FF, L_GB, L_GA = 0, 16, 20, 24

INT_MIN = -(2 ** 31)
HIGHEST = lax.Precision.HIGHEST


def _cparams(sem, vmem=VMEM_LIMIT):
    return pltpu.CompilerParams(dimension_semantics=sem, vmem_limit_bytes=vmem)


def _dot(a, b):
    return jnp.dot(a, b, preferred_element_type=F32)


def _dot_nt(a, b):
    return lax.dot_general(a, b, (((1,), (1,)), ((), ())), preferred_element_type=F32)


def _silu(x):
    return x * jax.nn.sigmoid(x)


NORM_ROWS = 32


def _rmsnorm_rows(x_ref, g_ref, rows):
    x = x_ref[rows, :]
    ms = jnp.mean(x * x, axis=-1, keepdims=True)
    return x * lax.rsqrt(ms + EPS) * g_ref[...]


def _norm_proj_kernel(x_ref, g_ref, w_ref, o_ref, h_ref, h_scr, *, tm):
    @pl.when(pl.program_id(1) == 0)
    def _():
        def body(r, carry):
            rows = pl.ds(pl.multiple_of(r * NORM_ROWS, NORM_ROWS), NORM_ROWS)
            hb = _rmsnorm_rows(x_ref, g_ref, rows).astype(BF16)
            h_scr[rows, :] = hb
            h_ref[rows, :] = hb
            return carry
        lax.fori_loop(0, tm // NORM_ROWS, body, 0)

    o_ref[...] = _dot(h_scr[...], w_ref[...])


def norm_proj(x, gain, w, *, tm=1024, tn=512):
    n, d = x.shape
    c = w.shape[1]
    return pl.pallas_call(
        functools.partial(_norm_proj_kernel, tm=tm),
        grid=(n // tm, c // tn),
        in_specs=[pl.BlockSpec((tm, d), lambda i, j: (i, 0)),
                  pl.BlockSpec((1, d), lambda i, j: (0, 0)),
                  pl.BlockSpec((d, tn), lambda i, j: (0, j))],
        out_specs=[pl.BlockSpec((tm, tn), lambda i, j: (i, j)),
                   pl.BlockSpec((tm, d), lambda i, j: (i, 0))],
        out_shape=[jax.ShapeDtypeStruct((n, c), F32),
                   jax.ShapeDtypeStruct((n, d), BF16)],
        scratch_shapes=[pltpu.VMEM((tm, d), BF16)],
        compiler_params=_cparams(("arbitrary", "arbitrary")),
        name="norm_proj",
    )(x, gain.reshape(1, d), w)


def _rmsnorm_kernel(x_ref, g_ref, o_ref, *, tm):
    def body(r, carry):
        rows = pl.ds(pl.multiple_of(r * NORM_ROWS, NORM_ROWS), NORM_ROWS)
        o_ref[rows, :] = _rmsnorm_rows(x_ref, g_ref, rows)
        return carry
    lax.fori_loop(0, tm // NORM_ROWS, body, 0)


def rmsnorm(x, gain, *, tm=512):
    n, d = x.shape
    return pl.pallas_call(
        functools.partial(_rmsnorm_kernel, tm=tm),
        grid=(n // tm,),
        in_specs=[pl.BlockSpec((tm, d), lambda i: (i, 0)),
                  pl.BlockSpec((1, d), lambda i: (0, 0))],
        out_specs=pl.BlockSpec((tm, d), lambda i: (i, 0)),
        out_shape=jax.ShapeDtypeStruct((n, d), F32),
        compiler_params=_cparams(("arbitrary",)),
        name="final_rmsnorm",
    )(x, gain.reshape(1, d))


def _prep_kernel(s_ref, par_ref, tok_ref, tr_ref, carry_scr):
    @pl.when(pl.program_id(1) == 0)
    def _():
        carry_scr[...] = jnp.zeros_like(carry_scr)

    s = s_ref[...]
    lane = lax.broadcasted_iota(I32, (LANE, LANE), 1)
    row = lax.broadcasted_iota(I32, (LANE, LANE), 0)
    z = s + par_ref[0:1, :]
    soft = jnp.maximum(z, 0.0) + jnp.log1p(jnp.exp(-jnp.abs(z)))
    log_sig = z - soft
    sig = jax.nn.sigmoid(z)
    g_val = -jnp.exp(par_ref[1:2, :]) * soft
    is_f = (lane >= L_FF) & (lane < L_FF + N_HEADS)
    is_b = (lane >= L_GB) & (lane < L_GB + N_HEADS)
    is_a = (lane >= L_GA) & (lane < L_GA + N_HEADS)
    pre = jnp.where(is_f, log_sig, jnp.where(is_a, g_val, 0.0))
    tri = (row >= lane).astype(F32)
    tri_blk = ((row >= lane) & ((row // GDN_CHUNK) == (lane // GDN_CHUNK))).astype(F32)
    cum_full = jnp.dot(tri, pre, precision=HIGHEST, preferred_element_type=F32)
    cum_blk = jnp.dot(tri_blk, pre, precision=HIGHEST, preferred_element_type=F32)
    c_fox = cum_full + carry_scr[0:1, :]
    carry_scr[0:1, :] = c_fox[LANE - 1:LANE, :]
    scale_iw = IDX_HEADS ** -0.5 * IDX_DIM ** -0.5
    out = jnp.where(is_f, c_fox,
                    jnp.where(is_a, cum_blk,
                              jnp.where(is_b, sig,
                                        jnp.where(lane < IDX_HEADS, s * scale_iw, 0.0))))
    tok_ref[...] = out
    tr_ref[0] = out.T[0:32, :]


def prep_small(proj, par, batch, seq):
    n = proj.shape[0]
    nc = seq // LANE
    return pl.pallas_call(
        _prep_kernel,
        grid=(batch, nc),
        in_specs=[pl.BlockSpec((LANE, LANE), lambda b, c: (b * nc + c, C_SM // LANE)),
                  pl.BlockSpec((8, LANE), lambda b, c: (0, 0))],
        out_specs=[pl.BlockSpec((LANE, LANE), lambda b, c: (b * nc + c, 0)),
                   pl.BlockSpec((1, 32, LANE), lambda b, c: (b, 0, c))],
        out_shape=[jax.ShapeDtypeStruct((n, LANE), F32),
                   jax.ShapeDtypeStruct((batch, 32, seq), F32)],
        scratch_shapes=[pltpu.VMEM((8, LANE), F32)],
        compiler_params=_cparams(("arbitrary", "arbitrary")),
        name="prep_small",
    )(proj, par)


def _ret_gamma():
    return [math.log1p(-(2.0 ** (-5.0 - h))) for h in range(N_HEADS)]


def _retention_kernel(q_ref, k_ref, v_ref, g_ref, cos_ref, sin_ref, dec_ref, zeta_ref, xi_ref,
                      o_ref, state_scr):
    @pl.when(pl.program_id(1) == 0)
    def _():
        state_scr[...] = jnp.zeros_like(state_scr)

    cos_t = cos_ref[...]
    sin_t = sin_ref[...]
    log_gamma = _ret_gamma()
    for h in range(N_HEADS):
        sl = slice(h * HEAD_DIM, (h + 1) * HEAD_DIM)
        q = q_ref[:, sl]
        k = k_ref[:, sl]
        qr = q * cos_t + pltpu.roll(q, HEAD_DIM // 2, 1) * sin_t
        kr = (k * cos_t + pltpu.roll(k, HEAD_DIM // 2, 1) * sin_t) * (HEAD_DIM ** -0.5)
        qb = qr.astype(BF16)
        kb = kr.astype(BF16)
        vb = v_ref[:, sl].astype(BF16)
        inner = _dot_nt(qb, kb) * dec_ref[h]
        st = state_scr[h]
        o = _dot(inner.astype(BF16), vb) + _dot(qb, st.astype(BF16)) * xi_ref[h]
        kz_t = (kr * zeta_ref[h]).T.astype(BF16)
        state_scr[h] = st * math.exp(log_gamma[h] * RET_CHUNK) + _dot(kz_t, vb)
        mu = jnp.mean(o, axis=-1, keepdims=True)
        oc = o - mu
        var = jnp.mean(oc * oc, axis=-1, keepdims=True)
        o_ref[:, sl] = (_silu(g_ref[:, sl]) * (oc * lax.rsqrt(var + EPS))).astype(BF16)


def _retention_tables(seq):
    half = HEAD_DIM // 2
    inv = 1.0 / (ROPE_BASE ** (jnp.arange(half, dtype=F32) / half))
    ang = jnp.arange(seq).astype(F32)[:, None] * inv[None, :]
    cos, sin = jnp.cos(ang), jnp.sin(ang)
    cos_t = jnp.concatenate([cos, cos], axis=-1)
    sin_t = jnp.concatenate([-sin, sin], axis=-1)
    c = RET_CHUNK
    log_gamma = jnp.log1p(-jnp.exp2(-5.0 - jnp.arange(N_HEADS, dtype=F32)))
    n = jnp.arange(c, dtype=F32)
    diff = n[:, None] - n[None, :]
    decay = jnp.where(diff >= 0, jnp.exp(log_gamma[:, None, None] * jnp.maximum(diff, 0.0)), 0.0)
    zeta = jnp.exp(log_gamma[:, None] * (c - 1 - n)[None, :])
    xi = jnp.exp(log_gamma[:, None] * (n + 1)[None, :])
    ones = jnp.ones((1, 1, HEAD_DIM), F32)
    return cos_t, sin_t, decay, zeta[:, :, None] * ones, xi[:, :, None] * ones


def retention(proj, tables, batch, seq):
    n = proj.shape[0]
    c = RET_CHUNK
    nc = seq // c
    cos_t, sin_t, decay, zeta, xi = tables
    col = lambda off: pl.BlockSpec((c, WIDTH), lambda b, i: (b * nc + i, off // WIDTH))
    full3 = pl.BlockSpec((N_HEADS, c, HEAD_DIM), lambda b, i: (0, 0, 0))
    return pl.pallas_call(
        _retention_kernel,
        grid=(batch, nc),
        in_specs=[col(C_RQ), col(C_RK), col(C_RV), col(C_RG),
                  pl.BlockSpec((c, HEAD_DIM), lambda b, i: (i, 0)),
                  pl.BlockSpec((c, HEAD_DIM), lambda b, i: (i, 0)),
                  full3, full3, full3],
        out_specs=pl.BlockSpec((c, WIDTH), lambda b, i: (b * nc + i, 0)),
        out_shape=jax.ShapeDtypeStruct((n, WIDTH), BF16),
        scratch_shapes=[pltpu.VMEM((N_HEADS, HEAD_DIM, HEAD_DIM), F32)],
        compiler_params=_cparams(("arbitrary", "arbitrary")),
        name="retention",
    )(proj, proj, proj, proj, cos_t, sin_t, decay, zeta, xi)


def _fox_kernel(q_ref, k_ref, v_ref, ctok_ref, ctr_ref, o_ref, m_scr, l_scr, acc_scr, *, t):
    qi = pl.program_id(1)
    ki = pl.program_id(2)

    @pl.when(ki == 0)
    def _():
        m_scr[...] = jnp.full_like(m_scr, -jnp.inf)
        l_scr[...] = jnp.zeros_like(l_scr)
        acc_scr[...] = jnp.zeros_like(acc_scr)

    def step(masked):
        if masked:
            row = lax.broadcasted_iota(I32, (t, t), 0)
            colm = lax.broadcasted_iota(I32, (t, t), 1)
            keep = row >= colm
        for h in range(N_HEADS):
            sl = slice(h * HEAD_DIM, (h + 1) * HEAD_DIM)
            qb = q_ref[:, sl].astype(BF16)
            kb = k_ref[:, sl].astype(BF16)
            c_q = ctok_ref[:, L_FF + h:L_FF + h + 1]
            c_k = ctr_ref[0, L_FF + h:L_FF + h + 1, :]
            s = _dot_nt(qb, kb) * (HEAD_DIM ** -0.5) + (c_q - c_k)
            if masked:
                s = jnp.where(keep, s, -jnp.inf)
            m_old = m_scr[h]
            m_new = jnp.maximum(m_old, jnp.max(s, axis=-1, keepdims=True))
            alpha = jnp.exp(m_old - m_new)
            p = jnp.exp(s - m_new)
            l_scr[h] = alpha * l_scr[h] + jnp.sum(p, axis=-1, keepdims=True)
            acc_scr[h] = alpha * acc_scr[h] + _dot(p.astype(BF16), v_ref[:, sl].astype(BF16))
            m_scr[h] = m_new

    @pl.when(ki < qi)
    def _():
        step(False)

    @pl.when(ki == qi)
    def _():
        step(True)
        for h in range(N_HEADS):
            sl = slice(h * HEAD_DIM, (h + 1) * HEAD_DIM)
            o_ref[:, sl] = (acc_scr[h] / l_scr[h]).astype(BF16)


def fox_attention(proj, prep_tok, prep_tr, batch, seq, *, t=512):
    n = proj.shape[0]
    nt = seq // t
    qspec = pl.BlockSpec((t, WIDTH), lambda b, qi, ki: (b * nt + qi, C_FQ // WIDTH))
    kspec = lambda off: pl.BlockSpec(
        (t, WIDTH), lambda b, qi, ki: (b * nt + jnp.minimum(ki, qi), off // WIDTH))
    return pl.pallas_call(
        functools.partial(_fox_kernel, t=t),
        grid=(batch, nt, nt),
        in_specs=[qspec, kspec(C_FK), kspec(C_FV),
                  pl.BlockSpec((t, LANE), lambda b, qi, ki: (b * nt + qi, 0)),
                  pl.BlockSpec((1, 32, t), lambda b, qi, ki: (b, 0, jnp.minimum(ki, qi)))],
        out_specs=pl.BlockSpec((t, WIDTH), lambda b, qi, ki: (b * nt + qi, 0)),
        out_shape=jax.ShapeDtypeStruct((n, WIDTH), BF16),
        scratch_shapes=[pltpu.VMEM((N_HEADS, t, 1), F32),
                        pltpu.VMEM((N_HEADS, t, 1), F32),
                        pltpu.VMEM((N_HEADS, t, HEAD_DIM), F32)],
        compiler_params=_cparams(("arbitrary", "arbitrary", "arbitrary")),
        name="fox_attention",
    )(proj, proj, proj, prep_tok, prep_tr)


def _dsa_proj_kernel(cq_ref, g_ref, wq_ref, wi_ref, q_ref, qi_ref):
    x = cq_ref[...]
    ms = jnp.mean(x * x, axis=-1, keepdims=True)
    cb = (x * lax.rsqrt(ms + EPS) * g_ref[...]).astype(BF16)
    q_ref[...] = _dot(cb, wq_ref[...]).astype(BF16)
    qi_ref[...] = _dot(cb, wi_ref[...]).astype(BF16)


def dsa_proj(proj, cq_norm, w_uq, w_qidx, *, tm=512):
    n = proj.shape[0]
    r = DSA_Q_RANK
    wi = IDX_HEADS * IDX_DIM
    return pl.pallas_call(
        _dsa_proj_kernel,
        grid=(n // tm,),
        in_specs=[pl.BlockSpec((tm, r), lambda i: (i, C_DCQ // r)),
                  pl.BlockSpec((1, r), lambda i: (0, 0)),
                  pl.BlockSpec((r, WIDTH), lambda i: (0, 0)),
                  pl.BlockSpec((r, wi), lambda i: (0, 0))],
        out_specs=[pl.BlockSpec((tm, WIDTH), lambda i: (i, 0)),
                   pl.BlockSpec((tm, wi), lambda i: (i, 0))],
        out_shape=[jax.ShapeDtypeStruct((n, WIDTH), BF16),
                   jax.ShapeDtypeStruct((n, wi), BF16)],
        compiler_params=_cparams(("arbitrary",)),
        name="dsa_proj",
    )(proj, cq_norm.reshape(1, r), w_uq, w_qidx)


DSA_QB = 128
DSA_KC = 512


def _t5_bucket(rel):
    max_exact = REL_BUCKETS // 2
    relf = jnp.maximum(rel, max_exact).astype(F32)
    large = max_exact + (jnp.log(relf / max_exact) / math.log(REL_MAX_DIST / max_exact)
                         * (REL_BUCKETS - max_exact)).astype(I32)
    large = jnp.minimum(large, REL_BUCKETS - 1)
    return jnp.where(rel < max_exact, rel, large)


def _dsa_kernel(rb_ref, q_ref, qi_ref, tok_ref, k_ref, v_ref, ka_ref, kb_ref, o_ref,
                key_scr, lg_scr, band_scr, kb16_scr, vb16_scr, ka16_scr, kb16i_scr, *, seq, topk):
    qb_idx = pl.program_id(1)
    n_kc = qb_idx // (DSA_KC // DSA_QB) + 1
    t0 = qb_idx * DSA_QB

    @pl.when(qb_idx == 0)
    def _():
        kb16_scr[...] = k_ref[...].astype(BF16)
        vb16_scr[...] = v_ref[...].astype(BF16)
        ka16_scr[...] = ka_ref[...].astype(BF16)
        kb16i_scr[...] = kb_ref[...].astype(BF16)

    @pl.when((pl.program_id(0) == 0) & (qb_idx == 0))
    def _():
        i_ = lax.broadcasted_iota(I32, (DSA_QB, 2 * DSA_QB), 0)
        j_ = lax.broadcasted_iota(I32, (DSA_QB, 2 * DSA_QB), 1)
        rel = i_ + DSA_QB - j_
        bucket = _t5_bucket(rel)
        for h in range(N_HEADS):
            far = rb_ref[REL_BUCKETS - 1, h]
            band = jnp.zeros((DSA_QB, 2 * DSA_QB), F32)
            for bk in range(REL_BUCKETS - 1):
                band = jnp.where(bucket == bk, rb_ref[bk, h] - far, band)
            band_scr[h] = jnp.where(rel >= 0, band, 0.0)

    w_all = tok_ref[...]
    row_t = t0 + lax.broadcasted_iota(I32, (DSA_QB, DSA_KC), 0)

    def score_chunk(c, carry):
        ks = pl.ds(pl.multiple_of(c * DSA_KC, DSA_KC), DSA_KC)
        ka = ka16_scr[ks, :]
        kb = kb16i_scr[ks, :]
        acc = jnp.zeros((DSA_QB, DSA_KC), F32)
        for p in range(IDX_HEADS // 2):
            qp = qi_ref[:, p * LANE:(p + 1) * LANE]
            acc = acc + jnp.maximum(_dot_nt(qp, ka), 0.0) * w_all[:, 2 * p:2 * p + 1]
            acc = acc + jnp.maximum(_dot_nt(qp, kb), 0.0) * w_all[:, 2 * p + 1:2 * p + 2]
        bits = pltpu.bitcast(acc, I32)
        key = bits ^ ((bits >> 31) & 0x7FFFFFFF)
        col_s = c * DSA_KC + lax.broadcasted_iota(I32, (DSA_QB, DSA_KC), 1)
        key_scr[:, ks] = jnp.where(col_s <= row_t, key, INT_MIN)
        return carry

    lax.fori_loop(0, n_kc, score_chunk, 0)

    def count_ge(cand):
        def body(c, cnt):
            ks = pl.ds(pl.multiple_of(c * DSA_KC, DSA_KC), DSA_KC)
            ge = (key_scr[:, ks] >= cand).astype(I32)
            for u in range(DSA_KC // LANE):
                cnt = cnt + ge[:, u * LANE:(u + 1) * LANE]
            return cnt
        cnt = lax.fori_loop(0, n_kc, body, jnp.zeros((DSA_QB, LANE), I32))
        return jnp.sum(cnt, axis=-1, keepdims=True)

    def bisect(b, lo):
        cand = lo + (jnp.int32(1) << (31 - b))
        return jnp.where(count_ge(cand) >= topk, cand, lo)

    thr = lax.fori_loop(0, 32, bisect, jnp.full((DSA_QB, 1), INT_MIN, I32))
    n_ge = count_ge(thr)
    thr_eff = jnp.maximum(thr, INT_MIN + 1)

    has_tie = jnp.max(jnp.where((n_ge > topk) & (thr > INT_MIN), 1, 0)) > 0

    @pl.when(has_tie)
    def _():
        n_gt = count_ge(thr + 1)
        room = (topk - n_gt).astype(F32)
        ii = lax.broadcasted_iota(I32, (LANE, LANE), 0)
        jj = lax.broadcasted_iota(I32, (LANE, LANE), 1)
        upper = (ii <= jj).astype(BF16)

        def body(c, seen):
            ks = pl.ds(pl.multiple_of(c * LANE, LANE), LANE)
            kk = key_scr[:, ks]
            eq = kk == thr
            rank = seen + _dot(eq.astype(BF16), upper)
            drop = eq & (rank > room) & (thr > INT_MIN)
            key_scr[:, ks] = jnp.where(drop, INT_MIN, kk)
            return seen + jnp.sum(eq.astype(F32), axis=-1, keepdims=True)
        lax.fori_loop(0, n_kc * (DSA_KC // LANE), body, jnp.zeros((DSA_QB, 1), F32))

    band_off = pl.multiple_of(jnp.maximum(qb_idx - 1, 0) * DSA_QB, DSA_QB)
    for h in range(N_HEADS):
        sl = slice(h * HEAD_DIM, (h + 1) * HEAD_DIM)
        qh = q_ref[:, sl]
        far = rb_ref[REL_BUCKETS - 1, h]

        def logit_chunk(c, m):
            ks = pl.ds(pl.multiple_of(c * DSA_KC, DSA_KC), DSA_KC)
            s = _dot_nt(qh, kb16_scr[ks, :]) * (HEAD_DIM ** -0.5) + far
            s = jnp.where(key_scr[:, ks] >= thr_eff, s, -jnp.inf)
            lg_scr[:, ks] = s
            return m
        lax.fori_loop(0, n_kc, logit_chunk, 0)

        @pl.when(qb_idx > 0)
        def _():
            ws = pl.ds(band_off, 2 * DSA_QB)
            lg_scr[:, ws] = lg_scr[:, ws] + band_scr[h]

        @pl.when(qb_idx == 0)
        def _():
            lg_scr[:, 0:DSA_QB] = lg_scr[:, 0:DSA_QB] + band_scr[h][:, DSA_QB:]

        def max_chunk(c, m):
            ks = pl.ds(pl.multiple_of(c * DSA_KC, DSA_KC), DSA_KC)
            return jnp.maximum(m, jnp.max(lg_scr[:, ks], axis=-1, keepdims=True))
        m = lax.fori_loop(0, n_kc, max_chunk, jnp.full((DSA_QB, 1), -jnp.inf, F32))

        def pv_chunk(c, carry):
            l, acc = carry
            ks = pl.ds(pl.multiple_of(c * DSA_KC, DSA_KC), DSA_KC)
            p = jnp.exp(lg_scr[:, ks] - m)
            l = l + jnp.sum(p, axis=-1, keepdims=True)
            acc = acc + _dot(p.astype(BF16), vb16_scr[ks, :])
            return l, acc
        l, acc = lax.fori_loop(0, n_kc, pv_chunk,
                               (jnp.zeros((DSA_QB, 1), F32), jnp.zeros((DSA_QB, HEAD_DIM), F32)))
        o_ref[:, sl] = (acc / l).astype(BF16)


def dsa_attention(proj, q, q_idx, prep_tok, rel_bias, batch, seq):
    n = proj.shape[0]
    nq = seq // DSA_QB
    topk = min(TOPK_MAX, seq // 4)
    wi = IDX_HEADS * IDX_DIM
    rowblk = lambda w, cb: pl.BlockSpec((DSA_QB, w), lambda b, i: (b * nq + i, cb))
    seqblk = lambda off: pl.BlockSpec((seq, LANE), lambda b, i: (b, off // LANE))
    return pl.pallas_call(
        functools.partial(_dsa_kernel, seq=seq, topk=topk),
        grid=(batch, nq),
        in_specs=[pl.BlockSpec(memory_space=pltpu.SMEM),
                  rowblk(WIDTH, 0), rowblk(wi, 0), rowblk(LANE, 0),
                  seqblk(C_DK), seqblk(C_DV), seqblk(C_KA), seqblk(C_KB)],
        out_specs=pl.BlockSpec((DSA_QB, WIDTH), lambda b, i: (b * nq + i, 0)),
        out_shape=jax.ShapeDtypeStruct((n, WIDTH), BF16),
        scratch_shapes=[pltpu.VMEM((DSA_QB, seq), I32),
                        pltpu.VMEM((DSA_QB, seq), F32),
                        pltpu.VMEM((N_HEADS, DSA_QB, 2 * DSA_QB), F32),
                        pltpu.VMEM((seq, LANE), BF16),
                        pltpu.VMEM((seq, LANE), BF16),
                        pltpu.VMEM((seq, LANE), BF16),
                        pltpu.VMEM((seq, LANE), BF16)],
        compiler_params=_cparams(("arbitrary", "arbitrary")),
        name="dsa_attention",
    )(rel_bias, q, q_idx, prep_tok, proj, proj, proj, proj)


GDN_T = 128
GDN_HALO = 8


def _gdn_kernel(q_ref, k_ref, v_ref, z_ref, cw_ref, ng_ref, tok_ref, tr_ref, o_ref,
                xq_scr, xk_scr, xv_scr, state_scr):
    first = pl.program_id(1) == 0

    @pl.when(first)
    def _():
        state_scr[...] = jnp.zeros_like(state_scr)
        for scr in (xq_scr, xk_scr, xv_scr):
            scr[0:GDN_HALO, :] = jnp.zeros((GDN_HALO, WIDTH), F32)

    def conv(x_ref, scr, w_off):
        scr[GDN_HALO:, :] = x_ref[...]
        y = jnp.zeros((GDN_T, WIDTH), F32)
        for i in range(GDN_CONV):
            st = GDN_HALO - (GDN_CONV - 1) + i
            y = y + scr[st:st + GDN_T, :] * cw_ref[i:i + 1, w_off:w_off + WIDTH]
        scr[0:GDN_HALO, :] = scr[GDN_T:GDN_T + GDN_HALO, :]
        return _silu(y)

    qc = conv(q_ref, xq_scr, 0)
    kc = conv(k_ref, xk_scr, WIDTH)
    vc = conv(v_ref, xv_scr, 2 * WIDTH)
    tok = tok_ref[...]
    c = GDN_CHUNK
    ri = lax.broadcasted_iota(I32, (c, c), 0)
    ci = lax.broadcasted_iota(I32, (c, c), 1)
    tril = ri >= ci
    strict = ri > ci
    eye = (ri == ci).astype(F32)

    for h in range(N_HEADS):
        sl = slice(h * HEAD_DIM, (h + 1) * HEAD_DIM)
        qh = qc[:, sl]
        kh = kc[:, sl]
        qf = qh * lax.rsqrt(jnp.sum(qh * qh, axis=-1, keepdims=True) + EPS) * (HEAD_DIM ** -0.5)
        kf = kh * lax.rsqrt(jnp.sum(kh * kh, axis=-1, keepdims=True) + EPS)
        vf = vc[:, sl]
        beta = tok[:, L_GB + h:L_GB + h + 1]
        g_col = tok[:, L_GA + h:L_GA + h + 1]
        g_row = tr_ref[0, L_GA + h:L_GA + h + 1, :]
        outs = []
        for j in range(GDN_T // c):
            rs = slice(j * c, (j + 1) * c)
            qj, kj, vj = qf[rs], kf[rs], vf[rs]
            bj, gj = beta[rs], g_col[rs]
            gr = g_row[:, rs]
            kjb = kj.astype(BF16)
            decay = jnp.exp(jnp.where(tril, gj - gr, -jnp.inf))
            kk = _dot_nt(kjb, kjb)
            neg_l = -jnp.where(strict, bj * kk * decay, 0.0)
            t_inv = eye + neg_l
            pw = neg_l
            for _ in range(5):
                pw = jnp.dot(pw, pw, precision=HIGHEST, preferred_element_type=F32)
                t_inv = t_inv + jnp.dot(t_inv, pw, precision=HIGHEST, preferred_element_type=F32)
            eg = jnp.exp(gj)
            rhs = jnp.concatenate([vj * bj, kj * (bj * eg)], axis=-1)
            sol = jnp.dot(t_inv, rhs, precision=HIGHEST, preferred_element_type=F32)
            u0 = sol[:, :HEAD_DIM]
            kcum = sol[:, HEAD_DIM:]
            qk = _dot_nt(qj.astype(BF16), kjb) * decay
            q_dec = qj * eg
            g_last = gj[c - 1:c, :]
            k_dec = kj * jnp.exp(g_last - gj)
            st = state_scr[h]
            stb = st.astype(BF16)
            v_new = u0 - _dot(kcum.astype(BF16), stb)
            v_new_b = v_new.astype(BF16)
            outs.append(_dot(q_dec.astype(BF16), stb) + _dot(qk.astype(BF16), v_new_b))
            state_scr[h] = st * jnp.exp(g_last) + _dot(k_dec.T.astype(BF16), v_new_b)
        o = jnp.concatenate(outs, axis=0)
        ms = jnp.mean(o * o, axis=-1, keepdims=True)
        on = o * lax.rsqrt(ms + EPS) * ng_ref[...]
        o_ref[:, sl] = (on * _silu(z_ref[:, sl])).astype(BF16)


def gated_deltanet(proj, prep_tok, prep_tr, conv_w, norm_g, batch, seq):
    n = proj.shape[0]
    t = GDN_T
    nt = seq // t
    col = lambda off: pl.BlockSpec((t, WIDTH), lambda b, i: (b * nt + i, off // WIDTH))
    return pl.pallas_call(
        _gdn_kernel,
        grid=(batch, nt),
        in_specs=[col(C_GQ), col(C_GK), col(C_GV), col(C_GZ),
                  pl.BlockSpec((GDN_CONV, 3 * WIDTH), lambda b, i: (0, 0)),
                  pl.BlockSpec((1, HEAD_DIM), lambda b, i: (0, 0)),
                  pl.BlockSpec((t, LANE), lambda b, i: (b * nt + i, 0)),
                  pl.BlockSpec((1, 32, t), lambda b, i: (b, 0, i))],
        out_specs=pl.BlockSpec((t, WIDTH), lambda b, i: (b * nt + i, 0)),
        out_shape=jax.ShapeDtypeStruct((n, WIDTH), BF16),
        scratch_shapes=[pltpu.VMEM((t + GDN_HALO, WIDTH), F32),
                        pltpu.VMEM((t + GDN_HALO, WIDTH), F32),
                        pltpu.VMEM((t + GDN_HALO, WIDTH), F32),
                        pltpu.VMEM((N_HEADS, HEAD_DIM, HEAD_DIM), F32)],
        compiler_params=_cparams(("arbitrary", "arbitrary")),
        name="gated_deltanet",
    )(proj, proj, proj, proj, conv_w, norm_g.reshape(1, HEAD_DIM), prep_tok, prep_tr)


def _merge_kernel(h_ref, b0_ref, b1_ref, b2_ref, b3_ref, g0_ref, g1_ref, g2_ref, g3_ref,
                  wb_ref, o_ref):
    h = h_ref[...]
    acc = None
    for n, (b_ref, g_ref) in enumerate(zip((b0_ref, b1_ref, b2_ref, b3_ref),
                                           (g0_ref, g1_ref, g2_ref, g3_ref))):
        gate = jax.nn.sigmoid(_dot(h, g_ref[...]))
        term = gate * _dot(b_ref[...], wb_ref[n])
        acc = term if acc is None else acc + term
    o_ref[...] = acc.astype(BF16)


def merge_branches(h, branches, w_gate, w_branch, *, tm=512, tn=512):
    n, d = h.shape
    nj = d // tn
    bspec = pl.BlockSpec((tm, WIDTH), lambda j, i: (i, 0))
    gspec = lambda k: pl.BlockSpec((d, tn), lambda j, i: (0, k * nj + j))
    return pl.pallas_call(
        _merge_kernel,
        grid=(nj, n // tm),
        in_specs=[pl.BlockSpec((tm, d), lambda j, i: (i, 0)),
                  bspec, bspec, bspec, bspec,
                  gspec(0), gspec(1), gspec(2), gspec(3),
                  pl.BlockSpec((N_BRANCH, WIDTH, tn), lambda j, i: (0, 0, j))],
        out_specs=pl.BlockSpec((tm, tn), lambda j, i: (i, j)),
        out_shape=jax.ShapeDtypeStruct((n, d), BF16),
        compiler_params=_cparams(("arbitrary", "arbitrary")),
        name="merge_branches",
    )(h, *branches, w_gate, w_gate, w_gate, w_gate, w_branch)


def _resid_mm_kernel(a_ref, w_ref, x_ref, o_ref):
    o_ref[...] = x_ref[...] + _dot(a_ref[...], w_ref[...])


def resid_matmul(a, w, x, *, tm=512, tn=1024, name="resid_matmul"):
    n, k = a.shape
    d = w.shape[1]
    return pl.pallas_call(
        _resid_mm_kernel,
        grid=(d // tn, n // tm),
        in_specs=[pl.BlockSpec((tm, k), lambda j, i: (i, 0)),
                  pl.BlockSpec((k, tn), lambda j, i: (0, j)),
                  pl.BlockSpec((tm, tn), lambda j, i: (i, j))],
        out_specs=pl.BlockSpec((tm, tn), lambda j, i: (i, j)),
        out_shape=jax.ShapeDtypeStruct((n, d), F32),
        compiler_params=_cparams(("arbitrary", "arbitrary")),
        name=name,
    )(a, w, x)


FFN_HALO = 8


def _ffn1_kernel(x_ref, g_ref, wg_ref, wu_ref, cw_ref, cb_ref, o_ref, h_scr, gt_scr, halo_scr,
                 *, tm, tiles_per_seq):
    i = pl.program_id(0)
    j = pl.program_id(1)

    @pl.when(j == 0)
    def _():
        def body(r, carry):
            rows = pl.ds(pl.multiple_of(r * NORM_ROWS, NORM_ROWS), NORM_ROWS)
            h_scr[rows, :] = _rmsnorm_rows(x_ref, g_ref, rows).astype(BF16)
            return carry
        lax.fori_loop(0, tm // NORM_ROWS, body, 0)

    h = h_scr[...]
    gt_scr[FFN_HALO:, :] = _dot(h, wg_ref[...])
    seq_start = (i % tiles_per_seq) == 0

    @pl.when(seq_start)
    def _():
        gt_scr[0:FFN_HALO, :] = jnp.zeros((FFN_HALO, gt_scr.shape[1]), F32)

    @pl.when(jnp.logical_not(seq_start))
    def _():
        gt_scr[0:FFN_HALO, :] = halo_scr[j]

    halo_scr[j] = gt_scr[tm:tm + FFN_HALO, :]
    y = cb_ref[...]
    for t in range(FFN_CONV):
        st = FFN_HALO - (FFN_CONV - 1) + t
        y = y + gt_scr[st:st + tm, :] * cw_ref[t:t + 1, :]
    o_ref[...] = (_silu(y) * _dot(h, wu_ref[...])).astype(BF16)


def conv_ffn_up(x, gain, w_gate, w_up, conv_w, conv_b, seq, *, tm=512, tn=512):
    n, d = x.shape
    f = w_gate.shape[1]
    nj = f // tn
    return pl.pallas_call(
        functools.partial(_ffn1_kernel, tm=tm, tiles_per_seq=seq // tm),
        grid=(n // tm, nj),
        in_specs=[pl.BlockSpec((tm, d), lambda i, j: (i, 0)),
                  pl.BlockSpec((1, d), lambda i, j: (0, 0)),
                  pl.BlockSpec((d, tn), lambda i, j: (0, j)),
                  pl.BlockSpec((d, tn), lambda i, j: (0, j)),
                  pl.BlockSpec((FFN_CONV, tn), lambda i, j: (0, j)),
                  pl.BlockSpec((1, tn), lambda i, j: (0, j))],
        out_specs=pl.BlockSpec((tm, tn), lambda i, j: (i, j)),
        out_shape=jax.ShapeDtypeStruct((n, f), BF16),
        scratch_shapes=[pltpu.VMEM((tm, d), BF16),
                        pltpu.VMEM((tm + FFN_HALO, tn), F32),
                        pltpu.VMEM((nj, FFN_HALO, tn), F32)],
        compiler_params=_cparams(("arbitrary", "arbitrary")),
        name="conv_ffn_up",
    )(x, gain.reshape(1, d), w_gate, w_up, conv_w, conv_b.reshape(1, f))


def _permute_w_in(w_in):
    d = w_in.shape[0]
    sizes = (WIDTH, WIDTH, WIDTH, WIDTH,
             DSA_Q_RANK, HEAD_DIM, HEAD_DIM, IDX_DIM, IDX_HEADS,
             WIDTH, WIDTH, WIDTH, N_HEADS,
             WIDTH, WIDTH, WIDTH, WIDTH, N_HEADS, N_HEADS,
             N_BRANCH * d)
    offs = [0]
    for s in sizes:
        offs.append(offs[-1] + s)
    seg = [w_in[:, offs[i]:offs[i + 1]] for i in range(len(sizes))]
    (r_q, r_k, r_v, r_g, d_cq, d_k, d_v, i_k, i_w, f_q, f_k, f_v, f_f,
     g_q, g_k, g_v, g_z, g_b, g_a, gates) = seg
    z = lambda w: jnp.zeros((d, w), w_in.dtype)
    small = jnp.concatenate([i_w, f_f, g_b, g_a, z(LANE - IDX_HEADS - 3 * N_HEADS)], axis=1)
    main = jnp.concatenate([r_q, r_k, r_v, r_g, f_q, f_k, f_v, g_q, g_k, g_v, g_z,
                            d_k, d_cq, d_v,
                            i_k, z(IDX_DIM), z(IDX_DIM), i_k, small], axis=1)
    return main.astype(BF16), gates.astype(BF16)


def kernel(x, norm_mix, w_in, dsa_cq_norm, dsa_w_uq, dsa_w_qidx, fox_f_bias, gdn_conv, gdn_a_log,
           gdn_dt_bias, gdn_norm, w_branch, w_out, rel_bias, norm_ffn, ffn_w_gate, ffn_w_up,
           ffn_conv, ffn_conv_b, ffn_w_down, final_norm):
    batch, seq, d = x.shape
    depth = w_in.shape[0]
    xf = x.reshape(batch * seq, d)
    ret_tables = _retention_tables(seq)
    for l in range(depth):
        w_main, w_gate = _permute_w_in(w_in[l])
        proj, h = norm_proj(xf, norm_mix[l], w_main)
        par = jnp.zeros((8, LANE), F32)
        par = par.at[0, L_FF:L_FF + N_HEADS].set(fox_f_bias[l])
        par = par.at[0, L_GA:L_GA + N_HEADS].set(gdn_dt_bias[l])
        par = par.at[1, L_GA:L_GA + N_HEADS].set(gdn_a_log[l])
        prep_tok, prep_tr = prep_small(proj, par, batch, seq)
        o_ret = retention(proj, ret_tables, batch, seq)
        q_dsa, q_idx = dsa_proj(proj, dsa_cq_norm[l], dsa_w_uq[l].astype(BF16),
                                dsa_w_qidx[l].astype(BF16))
        o_dsa = dsa_attention(proj, q_dsa, q_idx, prep_tok, rel_bias, batch, seq)
        o_fox = fox_attention(proj, prep_tok, prep_tr, batch, seq)
        o_gdn = gated_deltanet(proj, prep_tok, prep_tr, gdn_conv[l], gdn_norm[l], batch, seq)
        merged = merge_branches(h, (o_ret, o_dsa, o_fox, o_gdn), w_gate, w_branch[l].astype(BF16))
        xf = resid_matmul(merged, w_out[l].astype(BF16), xf, name="out_proj")
        act = conv_ffn_up(xf, norm_ffn[l], ffn_w_gate[l].astype(BF16), ffn_w_up[l].astype(BF16),
                          ffn_conv[l], ffn_conv_b[l], seq)
        xf = resid_matmul(act, ffn_w_down[l].astype(BF16), xf, name="ffn_down")
    return rmsnorm(xf, final_norm).reshape(batch, seq, d)
```

```python
import functools
import math

import jax
import jax.numpy as jnp
from jax import lax
from jax.experimental import pallas as pl
from jax.experimental.pallas import tpu as pltpu

F32 = jnp.float32
BF16 = jnp.bfloat16
I32 = jnp.int32

HEAD_DIM = 128
N_HEADS = 4
WIDTH = N_HEADS * HEAD_DIM
N_BRANCH = 4
RET_CHUNK = 128
ROPE_BASE = 10000.0
DSA_Q_RANK = 384
IDX_HEADS = 16
IDX_DIM = 64
TOPK_MAX = 256
GDN_CONV = 4
GDN_CHUNK = 64
REL_BUCKETS = 32
REL_MAX_DIST = 128
FFN_CONV = 3
EPS = 1e-6

LANE = 128
VMEM_LIMIT = 56 * 1024 * 1024

C_RQ, C_RK, C_RV, C_RG = 0, 512, 1024, 1536
C_FQ, C_FK, C_FV = 2048, 2560, 3072
C_GQ, C_GK, C_GV, C_GZ = 3584, 4096, 4608, 5120
C_DK, C_DCQ, C_DV, C_KA, C_KB, C_SM = 5632, 5760, 6144, 6272, 6400, 6528
C_TOT = 6656
L_IW, L_FF, L_GB, L_GA = 0, 16, 20, 24

INT_MIN = -(2 ** 31)
HIGHEST = lax.Precision.HIGHEST


def _cparams(sem, vmem=VMEM_LIMIT):
    return pltpu.CompilerParams(dimension_semantics=sem, vmem_limit_bytes=vmem)


def _dot(a, b):
    return jnp.dot(a, b, preferred_element_type=F32)


def _dot_nt(a, b):
    return lax.dot_general(a, b, (((1,), (1,)), ((), ())), preferred_element_type=F32)


def _silu(x):
    return x * jax.nn.sigmoid(x)


NORM_ROWS = 32


def _rmsnorm_rows(x_ref, g_ref, rows):
    x = x_ref[rows, :]
    ms = jnp.mean(x * x, axis=-1, keepdims=True)
    return x * lax.rsqrt(ms + EPS) * g_ref[...]


def _norm_proj_kernel(x_ref, g_ref, w_ref, o_ref, h_ref, h_scr, *, tm):
    @pl.when(pl.program_id(1) == 0)
    def _():
        def body(r, carry):
            rows = pl.ds(pl.multiple_of(r * NORM_ROWS, NORM_ROWS), NORM_ROWS)
            hb = _rmsnorm_rows(x_ref, g_ref, rows).astype(BF16)
            h_scr[rows, :] = hb
            h_ref[rows, :] = hb
            return carry
        lax.fori_loop(0, tm // NORM_ROWS, body, 0)

    o_ref[...] = _dot(h_scr[...], w_ref[...])


def norm_proj(x, gain, w, *, tm=1024, tn=512):
    n, d = x.shape
    c = w.shape[1]
    return pl.pallas_call(
        functools.partial(_norm_proj_kernel, tm=tm),
        grid=(n // tm, c // tn),
        in_specs=[pl.BlockSpec((tm, d), lambda i, j: (i, 0)),
                  pl.BlockSpec((1, d), lambda i, j: (0, 0)),
                  pl.BlockSpec((d, tn), lambda i, j: (0, j))],
        out_specs=[pl.BlockSpec((tm, tn), lambda i, j: (i, j)),
                   pl.BlockSpec((tm, d), lambda i, j: (i, 0))],
        out_shape=[jax.ShapeDtypeStruct((n, c), F32),
                   jax.ShapeDtypeStruct((n, d), BF16)],
        scratch_shapes=[pltpu.VMEM((tm, d), BF16)],
        compiler_params=_cparams(("arbitrary", "arbitrary")),
        name="norm_proj",
    )(x, gain.reshape(1, d), w)


def _rmsnorm_kernel(x_ref, g_ref, o_ref, *, tm):
    def body(r, carry):
        rows = pl.ds(pl.multiple_of(r * NORM_ROWS, NORM_ROWS), NORM_ROWS)
        o_ref[rows, :] = _rmsnorm_rows(x_ref, g_ref, rows)
        return carry
    lax.fori_loop(0, tm // NORM_ROWS, body, 0)


def rmsnorm(x, gain, *, tm=512):
    n, d = x.shape
    return pl.pallas_call(
        functools.partial(_rmsnorm_kernel, tm=tm),
        grid=(n // tm,),
        in_specs=[pl.BlockSpec((tm, d), lambda i: (i, 0)),
                  pl.BlockSpec((1, d), lambda i: (0, 0))],
        out_specs=pl.BlockSpec((tm, d), lambda i: (i, 0)),
        out_shape=jax.ShapeDtypeStruct((n, d), F32),
        compiler_params=_cparams(("arbitrary",)),
        name="final_rmsnorm",
    )(x, gain.reshape(1, d))


def _prep_kernel(s_ref, par_ref, tok_ref, tr_ref, carry_scr):
    @pl.when(pl.program_id(1) == 0)
    def _():
        carry_scr[...] = jnp.zeros_like(carry_scr)

    s = s_ref[...]
    lane = lax.broadcasted_iota(I32, (LANE, LANE), 1)
    row = lax.broadcasted_iota(I32, (LANE, LANE), 0)
    z = s + par_ref[0:1, :]
    soft = jnp.maximum(z, 0.0) + jnp.log1p(jnp.exp(-jnp.abs(z)))
    log_sig = z - soft
    sig = jax.nn.sigmoid(z)
    g_val = -jnp.exp(par_ref[1:2, :]) * soft
    is_f = (lane >= L_FF) & (lane < L_FF + N_HEADS)
    is_b = (lane >= L_GB) & (lane < L_GB + N_HEADS)
    is_a = (lane >= L_GA) & (lane < L_GA + N_HEADS)
    pre = jnp.where(is_f, log_sig, jnp.where(is_a, g_val, 0.0))
    tri = (row >= lane).astype(F32)
    tri_blk = ((row >= lane) & ((row // GDN_CHUNK) == (lane // GDN_CHUNK))).astype(F32)
    cum_full = jnp.dot(tri, pre, precision=HIGHEST, preferred_element_type=F32)
    cum_blk = jnp.dot(tri_blk, pre, precision=HIGHEST, preferred_element_type=F32)
    c_fox = cum_full + carry_scr[0:1, :]
    carry_scr[0:1, :] = c_fox[LANE - 1:LANE, :]
    scale_iw = IDX_HEADS ** -0.5 * IDX_DIM ** -0.5
    out = jnp.where(is_f, c_fox,
                    jnp.where(is_a, cum_blk,
                              jnp.where(is_b, sig,
                                        jnp.where(lane < IDX_HEADS, s * scale_iw, 0.0))))
    tok_ref[...] = out
    tr_ref[0] = out.T[0:32, :]


def prep_small(proj, par, batch, seq):
    n = proj.shape[0]
    nc = seq // LANE
    return pl.pallas_call(
        _prep_kernel,
        grid=(batch, nc),
        in_specs=[pl.BlockSpec((LANE, LANE), lambda b, c: (b * nc + c, C_SM // LANE)),
                  pl.BlockSpec((8, LANE), lambda b, c: (0, 0))],
        out_specs=[pl.BlockSpec((LANE, LANE), lambda b, c: (b * nc + c, 0)),
                   pl.BlockSpec((1, 32, LANE), lambda b, c: (b, 0, c))],
        out_shape=[jax.ShapeDtypeStruct((n, LANE), F32),
                   jax.ShapeDtypeStruct((batch, 32, seq), F32)],
        scratch_shapes=[pltpu.VMEM((8, LANE), F32)],
        compiler_params=_cparams(("arbitrary", "arbitrary")),
        name="prep_small",
    )(proj, par)


def _ret_gamma():
    return [math.log1p(-(2.0 ** (-5.0 - h))) for h in range(N_HEADS)]


def _retention_kernel(q_ref, k_ref, v_ref, g_ref, cos_ref, sin_ref, dec_ref, zeta_ref, xi_ref,
                      o_ref, state_scr):
    @pl.when(pl.program_id(1) == 0)
    def _():
        state_scr[...] = jnp.zeros_like(state_scr)

    cos_t = cos_ref[...]
    sin_t = sin_ref[...]
    log_gamma = _ret_gamma()
    for h in range(N_HEADS):
        sl = slice(h * HEAD_DIM, (h + 1) * HEAD_DIM)
        q = q_ref[:, sl]
        k = k_ref[:, sl]
        qr = q * cos_t + pltpu.roll(q, HEAD_DIM // 2, 1) * sin_t
        kr = (k * cos_t + pltpu.roll(k, HEAD_DIM // 2, 1) * sin_t) * (HEAD_DIM ** -0.5)
        qb = qr.astype(BF16)
        kb = kr.astype(BF16)
        vb = v_ref[:, sl].astype(BF16)
        inner = _dot_nt(qb, kb) * dec_ref[h]
        st = state_scr[h]
        o = _dot(inner.astype(BF16), vb) + _dot(qb, st.astype(BF16)) * xi_ref[h]
        kz_t = (kr * zeta_ref[h]).T.astype(BF16)
        state_scr[h] = st * math.exp(log_gamma[h] * RET_CHUNK) + _dot(kz_t, vb)
        mu = jnp.mean(o, axis=-1, keepdims=True)
        oc = o - mu
        var = jnp.mean(oc * oc, axis=-1, keepdims=True)
        o_ref[:, sl] = (_silu(g_ref[:, sl]) * (oc * lax.rsqrt(var + EPS))).astype(BF16)


def _retention_tables(seq):
    half = HEAD_DIM // 2
    inv = 1.0 / (ROPE_BASE ** (jnp.arange(half, dtype=F32) / half))
    ang = jnp.arange(seq).astype(F32)[:, None] * inv[None, :]
    cos, sin = jnp.cos(ang), jnp.sin(ang)
    cos_t = jnp.concatenate([cos, cos], axis=-1)
    sin_t = jnp.concatenate([-sin, sin], axis=-1)
    c = RET_CHUNK
    log_gamma = jnp.log1p(-jnp.exp2(-5.0 - jnp.arange(N_HEADS, dtype=F32)))
    n = jnp.arange(c, dtype=F32)
    diff = n[:, None] - n[None, :]
    decay = jnp.where(diff >= 0, jnp.exp(log_gamma[:, None, None] * jnp.maximum(diff, 0.0)), 0.0)
    zeta = jnp.exp(log_gamma[:, None] * (c - 1 - n)[None, :])
    xi = jnp.exp(log_gamma[:, None] * (n + 1)[None, :])
    ones = jnp.ones((1, 1, HEAD_DIM), F32)
    return cos_t, sin_t, decay, zeta[:, :, None] * ones, xi[:, :, None] * ones


def retention(proj, tables, batch, seq):
    n = proj.shape[0]
    c = RET_CHUNK
    nc = seq // c
    cos_t, sin_t, decay, zeta, xi = tables
    col = lambda off: pl.BlockSpec((c, WIDTH), lambda b, i: (b * nc + i, off // WIDTH))
    full3 = pl.BlockSpec((N_HEADS, c, HEAD_DIM), lambda b, i: (0, 0, 0))
    return pl.pallas_call(
        _retention_kernel,
        grid=(batch, nc),
        in_specs=[col(C_RQ), col(C_RK), col(C_RV), col(C_RG),
                  pl.BlockSpec((c, HEAD_DIM), lambda b, i: (i, 0)),
                  pl.BlockSpec((c, HEAD_DIM), lambda b, i: (i, 0)),
                  full3, full3, full3],
        out_specs=pl.BlockSpec((c, WIDTH), lambda b, i: (b * nc + i, 0)),
        out_shape=jax.ShapeDtypeStruct((n, WIDTH), BF16),
        scratch_shapes=[pltpu.VMEM((N_HEADS, HEAD_DIM, HEAD_DIM), F32)],
        compiler_params=_cparams(("arbitrary", "arbitrary")),
        name="retention",
    )(proj, proj, proj, proj, cos_t, sin_t, decay, zeta, xi)


def _fox_kernel(q_ref, k_ref, v_ref, ctok_ref, ctr_ref, o_ref, m_scr, l_scr, acc_scr, *, t):
    qi = pl.program_id(1)
    ki = pl.program_id(2)

    @pl.when(ki == 0)
    def _():
        m_scr[...] = jnp.full_like(m_scr, -jnp.inf)
        l_scr[...] = jnp.zeros_like(l_scr)
        acc_scr[...] = jnp.zeros_like(acc_scr)

    def step(masked):
        if masked:
            row = lax.broadcasted_iota(I32, (t, t), 0)
            colm = lax.broadcasted_iota(I32, (t, t), 1)
            keep = row >= colm
        for h in range(N_HEADS):
            sl = slice(h * HEAD_DIM, (h + 1) * HEAD_DIM)
            qb = q_ref[:, sl].astype(BF16)
            kb = k_ref[:, sl].astype(BF16)
            c_q = ctok_ref[:, L_FF + h:L_FF + h + 1]
            c_k = ctr_ref[0, L_FF + h:L_FF + h + 1, :]
            s = _dot_nt(qb, kb) * (HEAD_DIM ** -0.5) + (c_q - c_k)
            if masked:
                s = jnp.where(keep, s, -jnp.inf)
            m_old = m_scr[h]
            m_new = jnp.maximum(m_old, jnp.max(s, axis=-1, keepdims=True))
            alpha = jnp.exp(m_old - m_new)
            p = jnp.exp(s - m_new)
            l_scr[h] = alpha * l_scr[h] + jnp.sum(p, axis=-1, keepdims=True)
            acc_scr[h] = alpha * acc_scr[h] + _dot(p.astype(BF16), v_ref[:, sl].astype(BF16))
            m_scr[h] = m_new

    @pl.when(ki < qi)
    def _():
        step(False)

    @pl.when(ki == qi)
    def _():
        step(True)
        for h in range(N_HEADS):
            sl = slice(h * HEAD_DIM, (h + 1) * HEAD_DIM)
            o_ref[:, sl] = (acc_scr[h] / l_scr[h]).astype(BF16)


def fox_attention(proj, prep_tok, prep_tr, batch, seq, *, t=512):
    n = proj.shape[0]
    nt = seq // t
    qspec = pl.BlockSpec((t, WIDTH), lambda b, qi, ki: (b * nt + qi, C_FQ // WIDTH))
    kspec = lambda off: pl.BlockSpec(
        (t, WIDTH), lambda b, qi, ki: (b * nt + jnp.minimum(ki, qi), off // WIDTH))
    return pl.pallas_call(
        functools.partial(_fox_kernel, t=t),
        grid=(batch, nt, nt),
        in_specs=[qspec, kspec(C_FK), kspec(C_FV),
                  pl.BlockSpec((t, LANE), lambda b, qi, ki: (b * nt + qi, 0)),
                  pl.BlockSpec((1, 32, t), lambda b, qi, ki: (b, 0, jnp.minimum(ki, qi)))],
        out_specs=pl.BlockSpec((t, WIDTH), lambda b, qi, ki: (b * nt + qi, 0)),
        out_shape=jax.ShapeDtypeStruct((n, WIDTH), BF16),
        scratch_shapes=[pltpu.VMEM((N_HEADS, t, 1), F32),
                        pltpu.VMEM((N_HEADS, t, 1), F32),
                        pltpu.VMEM((N_HEADS, t, HEAD_DIM), F32)],
        compiler_params=_cparams(("arbitrary", "arbitrary", "arbitrary")),
        name="fox_attention",
    )(proj, proj, proj, prep_tok, prep_tr)


def _dsa_proj_kernel(cq_ref, g_ref, wq_ref, wi_ref, q_ref, qi_ref):
    x = cq_ref[...]
    ms = jnp.mean(x * x, axis=-1, keepdims=True)
    cb = (x * lax.rsqrt(ms + EPS) * g_ref[...]).astype(BF16)
    q_ref[...] = _dot(cb, wq_ref[...]).astype(BF16)
    qi_ref[...] = _dot(cb, wi_ref[...]).astype(BF16)


def dsa_proj(proj, cq_norm, w_uq, w_qidx, *, tm=512):
    n = proj.shape[0]
    r = DSA_Q_RANK
    wi = IDX_HEADS * IDX_DIM
    return pl.pallas_call(
        _dsa_proj_kernel,
        grid=(n // tm,),
        in_specs=[pl.BlockSpec((tm, r), lambda i: (i, C_DCQ // r)),
                  pl.BlockSpec((1, r), lambda i: (0, 0)),
                  pl.BlockSpec((r, WIDTH), lambda i: (0, 0)),
                  pl.BlockSpec((r, wi), lambda i: (0, 0))],
        out_specs=[pl.BlockSpec((tm, WIDTH), lambda i: (i, 0)),
                   pl.BlockSpec((tm, wi), lambda i: (i, 0))],
        out_shape=[jax.ShapeDtypeStruct((n, WIDTH), BF16),
                   jax.ShapeDtypeStruct((n, wi), BF16)],
        compiler_params=_cparams(("arbitrary",)),
        name="dsa_proj",
    )(proj, cq_norm.reshape(1, r), w_uq, w_qidx)


DSA_QB = 128
DSA_KC = 512


def _t5_bucket(rel):
    max_exact = REL_BUCKETS // 2
    relf = jnp.maximum(rel, max_exact).astype(F32)
    large = max_exact + (jnp.log(relf / max_exact) / math.log(REL_MAX_DIST / max_exact)
                         * (REL_BUCKETS - max_exact)).astype(I32)
    large = jnp.minimum(large, REL_BUCKETS - 1)
    return jnp.where(rel < max_exact, rel, large)


def _dsa_kernel(rb_ref, q_ref, qi_ref, tok_ref, k_ref, v_ref, ka_ref, kb_ref, o_ref,
                key_scr, lg_scr, band_scr, kb16_scr, vb16_scr, ka16_scr, kb16i_scr, *, seq, topk):
    qb_idx = pl.program_id(1)
    n_kc = qb_idx // (DSA_KC // DSA_QB) + 1
    t0 = qb_idx * DSA_QB

    @pl.when(qb_idx == 0)
    def _():
        kb16_scr[...] = k_ref[...].astype(BF16)
        vb16_scr[...] = v_ref[...].astype(BF16)
        ka16_scr[...] = ka_ref[...].astype(BF16)
        kb16i_scr[...] = kb_ref[...].astype(BF16)

    @pl.when((pl.program_id(0) == 0) & (qb_idx == 0))
    def _():
        i_ = lax.broadcasted_iota(I32, (DSA_QB, 2 * DSA_QB), 0)
        j_ = lax.broadcasted_iota(I32, (DSA_QB, 2 * DSA_QB), 1)
        rel = i_ + DSA_QB - j_
        bucket = _t5_bucket(rel)
        for h in range(N_HEADS):
            far = rb_ref[REL_BUCKETS - 1, h]
            band = jnp.zeros((DSA_QB, 2 * DSA_QB), F32)
            for bk in range(REL_BUCKETS - 1):
                band = jnp.where(bucket == bk, rb_ref[bk, h] - far, band)
            band_scr[h] = jnp.where(rel >= 0, band, 0.0)

    w_all = tok_ref[...]
    row_t = t0 + lax.broadcasted_iota(I32, (DSA_QB, DSA_KC), 0)

    def score_chunk(c, carry):
        ks = pl.ds(pl.multiple_of(c * DSA_KC, DSA_KC), DSA_KC)
        ka = ka16_scr[ks, :]
        kb = kb16i_scr[ks, :]
        acc = jnp.zeros((DSA_QB, DSA_KC), F32)
        for p in range(IDX_HEADS // 2):
            qp = qi_ref[:, p * LANE:(p + 1) * LANE]
            acc = acc + jnp.maximum(_dot_nt(qp, ka), 0.0) * w_all[:, 2 * p:2 * p + 1]
            acc = acc + jnp.maximum(_dot_nt(qp, kb), 0.0) * w_all[:, 2 * p + 1:2 * p + 2]
        bits = pltpu.bitcast(acc, I32)
        key = bits ^ ((bits >> 31) & 0x7FFFFFFF)
        col_s = c * DSA_KC + lax.broadcasted_iota(I32, (DSA_QB, DSA_KC), 1)
        key_scr[:, ks] = jnp.where(col_s <= row_t, key, INT_MIN)
        return carry

    lax.fori_loop(0, n_kc, score_chunk, 0)

    def count_ge(cand):
        def body(c, cnt):
            ks = pl.ds(pl.multiple_of(c * DSA_KC, DSA_KC), DSA_KC)
            ge = (key_scr[:, ks] >= cand).astype(I32)
            for u in range(DSA_KC // LANE):
                cnt = cnt + ge[:, u * LANE:(u + 1) * LANE]
            return cnt
        cnt = lax.fori_loop(0, n_kc, body, jnp.zeros((DSA_QB, LANE), I32))
        return jnp.sum(cnt, axis=-1, keepdims=True)

    def bisect(b, lo):
        cand = lo + (jnp.int32(1) << (31 - b))
        return jnp.where(count_ge(cand) >= topk, cand, lo)

    thr = lax.fori_loop(0, 32, bisect, jnp.full((DSA_QB, 1), INT_MIN, I32))
    n_ge = count_ge(thr)
    thr_eff = jnp.maximum(thr, INT_MIN + 1)

    has_tie = jnp.max(jnp.where((n_ge > topk) & (thr > INT_MIN), 1, 0)) > 0

    @pl.when(has_tie)
    def _():
        n_gt = count_ge(thr + 1)
        room = (topk - n_gt).astype(F32)
        ii = lax.broadcasted_iota(I32, (LANE, LANE), 0)
        jj = lax.broadcasted_iota(I32, (LANE, LANE), 1)
        upper = (ii <= jj).astype(BF16)

        def body(c, seen):
            ks = pl.ds(pl.multiple_of(c * LANE, LANE), LANE)
            kk = key_scr[:, ks]
            eq = kk == thr
            rank = seen + _dot(eq.astype(BF16), upper)
            drop = eq & (rank > room) & (thr > INT_MIN)
            key_scr[:, ks] = jnp.where(drop, INT_MIN, kk)
            return seen + jnp.sum(eq.astype(F32), axis=-1, keepdims=True)
        lax.fori_loop(0, n_kc * (DSA_KC // LANE), body, jnp.zeros((DSA_QB, 1), F32))

    band_off = pl.multiple_of(jnp.maximum(qb_idx - 1, 0) * DSA_QB, DSA_QB)
    for h in range(N_HEADS):
        sl = slice(h * HEAD_DIM, (h + 1) * HEAD_DIM)
        qh = q_ref[:, sl]
        far = rb_ref[REL_BUCKETS - 1, h]

        def logit_chunk(c, m):
            ks = pl.ds(pl.multiple_of(c * DSA_KC, DSA_KC), DSA_KC)
            s = _dot_nt(qh, kb16_scr[ks, :]) * (HEAD_DIM ** -0.5) + far
            s = jnp.where(key_scr[:, ks] >= thr_eff, s, -jnp.inf)
            lg_scr[:, ks] = s
            return m
        lax.fori_loop(0, n_kc, logit_chunk, 0)

        @pl.when(qb_idx > 0)
        def _():
            ws = pl.ds(band_off, 2 * DSA_QB)
            lg_scr[:, ws] = lg_scr[:, ws] + band_scr[h]

        @pl.when(qb_idx == 0)
        def _():
            lg_scr[:, 0:DSA_QB] = lg_scr[:, 0:DSA_QB] + band_scr[h][:, DSA_QB:]

        def max_chunk(c, m):
            ks = pl.ds(pl.multiple_of(c * DSA_KC, DSA_KC), DSA_KC)
            return jnp.maximum(m, jnp.max(lg_scr[:, ks], axis=-1, keepdims=True))
        m = lax.fori_loop(0, n_kc, max_chunk, jnp.full((DSA_QB, 1), -jnp.inf, F32))

        def pv_chunk(c, carry):
            l, acc = carry
            ks = pl.ds(pl.multiple_of(c * DSA_KC, DSA_KC), DSA_KC)
            p = jnp.exp(lg_scr[:, ks] - m)
            l = l + jnp.sum(p, axis=-1, keepdims=True)
            acc = acc + _dot(p.astype(BF16), vb16_scr[ks, :])
            return l, acc
        l, acc = lax.fori_loop(0, n_kc, pv_chunk,
                               (jnp.zeros((DSA_QB, 1), F32), jnp.zeros((DSA_QB, HEAD_DIM), F32)))
        o_ref[:, sl] = (acc / l).astype(BF16)


def dsa_attention(proj, q, q_idx, prep_tok, rel_bias, batch, seq):
    n = proj.shape[0]
    nq = seq // DSA_QB
    topk = min(TOPK_MAX, seq // 4)
    wi = IDX_HEADS * IDX_DIM
    rowblk = lambda w, cb: pl.BlockSpec((DSA_QB, w), lambda b, i: (b * nq + i, cb))
    seqblk = lambda off: pl.BlockSpec((seq, LANE), lambda b, i: (b, off // LANE))
    return pl.pallas_call(
        functools.partial(_dsa_kernel, seq=seq, topk=topk),
        grid=(batch, nq),
        in_specs=[pl.BlockSpec(memory_space=pltpu.SMEM),
                  rowblk(WIDTH, 0), rowblk(wi, 0), rowblk(LANE, 0),
                  seqblk(C_DK), seqblk(C_DV), seqblk(C_KA), seqblk(C_KB)],
        out_specs=pl.BlockSpec((DSA_QB, WIDTH), lambda b, i: (b * nq + i, 0)),
        out_shape=jax.ShapeDtypeStruct((n, WIDTH), BF16),
        scratch_shapes=[pltpu.VMEM((DSA_QB, seq), I32),
                        pltpu.VMEM((DSA_QB, seq), F32),
                        pltpu.VMEM((N_HEADS, DSA_QB, 2 * DSA_QB), F32),
                        pltpu.VMEM((seq, LANE), BF16),
                        pltpu.VMEM((seq, LANE), BF16),
                        pltpu.VMEM((seq, LANE), BF16),
                        pltpu.VMEM((seq, LANE), BF16)],
        compiler_params=_cparams(("arbitrary", "arbitrary")),
        name="dsa_attention",
    )(rel_bias, q, q_idx, prep_tok, proj, proj, proj, proj)


GDN_T = 128
GDN_HALO = 8


def _gdn_kernel(q_ref, k_ref, v_ref, z_ref, cw_ref, ng_ref, tok_ref, tr_ref, o_ref,
                xq_scr, xk_scr, xv_scr, state_scr):
    first = pl.program_id(1) == 0

    @pl.when(first)
    def _():
        state_scr[...] = jnp.zeros_like(state_scr)
        for scr in (xq_scr, xk_scr, xv_scr):
            scr[0:GDN_HALO, :] = jnp.zeros((GDN_HALO, WIDTH), F32)

    def conv(x_ref, scr, w_off):
        scr[GDN_HALO:, :] = x_ref[...]
        y = jnp.zeros((GDN_T, WIDTH), F32)
        for i in range(GDN_CONV):
            st = GDN_HALO - (GDN_CONV - 1) + i
            y = y + scr[st:st + GDN_T, :] * cw_ref[i:i + 1, w_off:w_off + WIDTH]
        scr[0:GDN_HALO, :] = scr[GDN_T:GDN_T + GDN_HALO, :]
        return _silu(y)

    qc = conv(q_ref, xq_scr, 0)
    kc = conv(k_ref, xk_scr, WIDTH)
    vc = conv(v_ref, xv_scr, 2 * WIDTH)
    tok = tok_ref[...]
    c = GDN_CHUNK
    ri = lax.broadcasted_iota(I32, (c, c), 0)
    ci = lax.broadcasted_iota(I32, (c, c), 1)
    tril = ri >= ci
    strict = ri > ci
    eye = (ri == ci).astype(F32)

    for h in range(N_HEADS):
        sl = slice(h * HEAD_DIM, (h + 1) * HEAD_DIM)
        qh = qc[:, sl]
        kh = kc[:, sl]
        qf = qh * lax.rsqrt(jnp.sum(qh * qh, axis=-1, keepdims=True) + EPS) * (HEAD_DIM ** -0.5)
        kf = kh * lax.rsqrt(jnp.sum(kh * kh, axis=-1, keepdims=True) + EPS)
        vf = vc[:, sl]
        beta = tok[:, L_GB + h:L_GB + h + 1]
        g_col = tok[:, L_GA + h:L_GA + h + 1]
        g_row = tr_ref[0, L_GA + h:L_GA + h + 1, :]
        outs = []
        for j in range(GDN_T // c):
            rs = slice(j * c, (j + 1) * c)
            qj, kj, vj = qf[rs], kf[rs], vf[rs]
            bj, gj = beta[rs], g_col[rs]
            gr = g_row[:, rs]
            kjb = kj.astype(BF16)
            decay = jnp.exp(jnp.where(tril, gj - gr, -jnp.inf))
            kk = _dot_nt(kjb, kjb)
            neg_l = -jnp.where(strict, bj * kk * decay, 0.0)
            t_inv = eye + neg_l
            pw = neg_l
            for _ in range(5):
                pw = jnp.dot(pw, pw, precision=HIGHEST, preferred_element_type=F32)
                t_inv = t_inv + jnp.dot(t_inv, pw, precision=HIGHEST, preferred_element_type=F32)
            eg = jnp.exp(gj)
            rhs = jnp.concatenate([vj * bj, kj * (bj * eg)], axis=-1)
            sol = jnp.dot(t_inv, rhs, precision=HIGHEST, preferred_element_type=F32)
            u0 = sol[:, :HEAD_DIM]
            kcum = sol[:, HEAD_DIM:]
            qk = _dot_nt(qj.astype(BF16), kjb) * decay
            q_dec = qj * eg
            g_last = gj[c - 1:c, :]
            k_dec = kj * jnp.exp(g_last - gj)
            st = state_scr[h]
            stb = st.astype(BF16)
            v_new = u0 - _dot(kcum.astype(BF16), stb)
            v_new_b = v_new.astype(BF16)
            outs.append(_dot(q_dec.astype(BF16), stb) + _dot(qk.astype(BF16), v_new_b))
            state_scr[h] = st * jnp.exp(g_last) + _dot(k_dec.T.astype(BF16), v_new_b)
        o = jnp.concatenate(outs, axis=0)
        ms = jnp.mean(o * o, axis=-1, keepdims=True)
        on = o * lax.rsqrt(ms + EPS) * ng_ref[...]
        o_ref[:, sl] = (on * _silu(z_ref[:, sl])).astype(BF16)


def gated_deltanet(proj, prep_tok, prep_tr, conv_w, norm_g, batch, seq):
    n = proj.shape[0]
    t = GDN_T
    nt = seq // t
    col = lambda off: pl.BlockSpec((t, WIDTH), lambda b, i: (b * nt + i, off // WIDTH))
    return pl.pallas_call(
        _gdn_kernel,
        grid=(batch, nt),
        in_specs=[col(C_GQ), col(C_GK), col(C_GV), col(C_GZ),
                  pl.BlockSpec((GDN_CONV, 3 * WIDTH), lambda b, i: (0, 0)),
                  pl.BlockSpec((1, HEAD_DIM), lambda b, i: (0, 0)),
                  pl.BlockSpec((t, LANE), lambda b, i: (b * nt + i, 0)),
                  pl.BlockSpec((1, 32, t), lambda b, i: (b, 0, i))],
        out_specs=pl.BlockSpec((t, WIDTH), lambda b, i: (b * nt + i, 0)),
        out_shape=jax.ShapeDtypeStruct((n, WIDTH), BF16),
        scratch_shapes=[pltpu.VMEM((t + GDN_HALO, WIDTH), F32),
                        pltpu.VMEM((t + GDN_HALO, WIDTH), F32),
                        pltpu.VMEM((t + GDN_HALO, WIDTH), F32),
                        pltpu.VMEM((N_HEADS, HEAD_DIM, HEAD_DIM), F32)],
        compiler_params=_cparams(("arbitrary", "arbitrary")),
        name="gated_deltanet",
    )(proj, proj, proj, proj, conv_w, norm_g.reshape(1, HEAD_DIM), prep_tok, prep_tr)


def _merge_kernel(h_ref, b0_ref, b1_ref, b2_ref, b3_ref, g0_ref, g1_ref, g2_ref, g3_ref,
                  wb_ref, o_ref):
    h = h_ref[...]
    acc = None
    for n, (b_ref, g_ref) in enumerate(zip((b0_ref, b1_ref, b2_ref, b3_ref),
                                           (g0_ref, g1_ref, g2_ref, g3_ref))):
        gate = jax.nn.sigmoid(_dot(h, g_ref[...]))
        term = gate * _dot(b_ref[...], wb_ref[n])
        acc = term if acc is None else acc + term
    o_ref[...] = acc.astype(BF16)


def merge_branches(h, branches, w_gate, w_branch, *, tm=512, tn=512):
    n, d = h.shape
    nj = d // tn
    bspec = pl.BlockSpec((tm, WIDTH), lambda j, i: (i, 0))
    gspec = lambda k: pl.BlockSpec((d, tn), lambda j, i: (0, k * nj + j))
    return pl.pallas_call(
        _merge_kernel,
        grid=(nj, n // tm),
        in_specs=[pl.BlockSpec((tm, d), lambda j, i: (i, 0)),
                  bspec, bspec, bspec, bspec,
                  gspec(0), gspec(1), gspec(2), gspec(3),
                  pl.BlockSpec((N_BRANCH, WIDTH, tn), lambda j, i: (0, 0, j))],
        out_specs=pl.BlockSpec((tm, tn), lambda j, i: (i, j)),
        out_shape=jax.ShapeDtypeStruct((n, d), BF16),
        compiler_params=_cparams(("arbitrary", "arbitrary")),
        name="merge_branches",
    )(h, *branches, w_gate, w_gate, w_gate, w_gate, w_branch)


def _resid_mm_kernel(a_ref, w_ref, x_ref, o_ref):
    o_ref[...] = x_ref[...] + _dot(a_ref[...], w_ref[...])


def resid_matmul(a, w, x, *, tm=512, tn=1024, name="resid_matmul"):
    n, k = a.shape
    d = w.shape[1]
    return pl.pallas_call(
        _resid_mm_kernel,
        grid=(d // tn, n // tm),
        in_specs=[pl.BlockSpec((tm, k), lambda j, i: (i, 0)),
                  pl.BlockSpec((k, tn), lambda j, i: (0, j)),
                  pl.BlockSpec((tm, tn), lambda j, i: (i, j))],
        out_specs=pl.BlockSpec((tm, tn), lambda j, i: (i, j)),
        out_shape=jax.ShapeDtypeStruct((n, d), F32),
        compiler_params=_cparams(("arbitrary", "arbitrary")),
        name=name,
    )(a, w, x)


FFN_HALO = 8


def _ffn1_kernel(x_ref, g_ref, wg_ref, wu_ref, cw_ref, cb_ref, o_ref, h_scr, gt_scr, halo_scr,
                 *, tm, tiles_per_seq):
    i = pl.program_id(0)
    j = pl.program_id(1)

    @pl.when(j == 0)
    def _():
        def body(r, carry):
            rows = pl.ds(pl.multiple_of(r * NORM_ROWS, NORM_ROWS), NORM_ROWS)
            h_scr[rows, :] = _rmsnorm_rows(x_ref, g_ref, rows).astype(BF16)
            return carry
        lax.fori_loop(0, tm // NORM_ROWS, body, 0)

    h = h_scr[...]
    gt_scr[FFN_HALO:, :] = _dot(h, wg_ref[...])
    seq_start = (i % tiles_per_seq) == 0

    @pl.when(seq_start)
    def _():
        gt_scr[0:FFN_HALO, :] = jnp.zeros((FFN_HALO, gt_scr.shape[1]), F32)

    @pl.when(jnp.logical_not(seq_start))
    def _():
        gt_scr[0:FFN_HALO, :] = halo_scr[j]

    halo_scr[j] = gt_scr[tm:tm + FFN_HALO, :]
    y = cb_ref[...]
    for t in range(FFN_CONV):
        st = FFN_HALO - (FFN_CONV - 1) + t
        y = y + gt_scr[st:st + tm, :] * cw_ref[t:t + 1, :]
    o_ref[...] = (_silu(y) * _dot(h, wu_ref[...])).astype(BF16)


def conv_ffn_up(x, gain, w_gate, w_up, conv_w, conv_b, seq, *, tm=512, tn=512):
    n, d = x.shape
    f = w_gate.shape[1]
    nj = f // tn
    return pl.pallas_call(
        functools.partial(_ffn1_kernel, tm=tm, tiles_per_seq=seq // tm),
        grid=(n // tm, nj),
        in_specs=[pl.BlockSpec((tm, d), lambda i, j: (i, 0)),
                  pl.BlockSpec((1, d), lambda i, j: (0, 0)),
                  pl.BlockSpec((d, tn), lambda i, j: (0, j)),
                  pl.BlockSpec((d, tn), lambda i, j: (0, j)),
                  pl.BlockSpec((FFN_CONV, tn), lambda i, j: (0, j)),
                  pl.BlockSpec((1, tn), lambda i, j: (0, j))],
        out_specs=pl.BlockSpec((tm, tn), lambda i, j: (i, j)),
        out_shape=jax.ShapeDtypeStruct((n, f), BF16),
        scratch_shapes=[pltpu.VMEM((tm, d), BF16),
                        pltpu.VMEM((tm + FFN_HALO, tn), F32),
                        pltpu.VMEM((nj, FFN_HALO, tn), F32)],
        compiler_params=_cparams(("arbitrary", "arbitrary")),
        name="conv_ffn_up",
    )(x, gain.reshape(1, d), w_gate, w_up, conv_w, conv_b.reshape(1, f))


IN_SIZES = (WIDTH, WIDTH, WIDTH, WIDTH,
            DSA_Q_RANK, HEAD_DIM, HEAD_DIM, IDX_DIM, IDX_HEADS,
            WIDTH, WIDTH, WIDTH, N_HEADS,
            WIDTH, WIDTH, WIDTH, WIDTH, N_HEADS, N_HEADS)
IN_NAMES = ("r_q", "r_k", "r_v", "r_g", "d_cq", "d_k", "d_v", "i_k", "i_w",
            "f_q", "f_k", "f_v", "f_f", "g_q", "g_k", "g_v", "g_z", "g_b", "g_a")
IN_PLAN = (("r_q", C_RQ), ("r_k", C_RK), ("r_v", C_RV), ("r_g", C_RG),
           ("f_q", C_FQ), ("f_k", C_FK), ("f_v", C_FV),
           ("g_q", C_GQ), ("g_k", C_GK), ("g_v", C_GV), ("g_z", C_GZ),
           ("d_k", C_DK), ("d_cq", C_DCQ), ("d_v", C_DV),
           ("i_k", C_KA), ("i_k", C_KB + IDX_DIM),
           ("i_w", C_SM + L_IW), ("f_f", C_SM + L_FF), ("g_b", C_SM + L_GB), ("g_a", C_SM + L_GA))


def _prep_w_in_kernel(w_ref, m_ref, g_ref):
    src = {}
    off = 0
    for name, size in zip(IN_NAMES, IN_SIZES):
        src[name] = (off, size)
        off += size
    m_ref[...] = jnp.zeros_like(m_ref)
    for name, dst in IN_PLAN:
        so, w = src[name]
        m_ref[:, dst:dst + w] = w_ref[:, so:so + w].astype(BF16)
    g_ref[...] = w_ref[:, off:off + g_ref.shape[1]].astype(BF16)


def prep_w_in(w_in, layer, *, tr=128):
    _, d, c = w_in.shape
    return pl.pallas_call(
        _prep_w_in_kernel,
        grid=(d // tr,),
        in_specs=[pl.BlockSpec((None, tr, c), lambda i: (layer, i, 0))],
        out_specs=[pl.BlockSpec((tr, C_TOT), lambda i: (i, 0)),
                   pl.BlockSpec((tr, N_BRANCH * d), lambda i: (i, 0))],
        out_shape=[jax.ShapeDtypeStruct((d, C_TOT), BF16),
                   jax.ShapeDtypeStruct((d, N_BRANCH * d), BF16)],
        compiler_params=_cparams(("arbitrary",)),
        name="prep_w_in",
    )(w_in)


def _cast_kernel(w_ref, o_ref):
    o_ref[...] = w_ref[...].astype(BF16)


def cast_layer(w, layer, *, tr=256):
    _, r, c = w.shape
    if r % tr:
        tr = r
    return pl.pallas_call(
        _cast_kernel,
        grid=(r // tr,),
        in_specs=[pl.BlockSpec((None, tr, c), lambda i: (layer, i, 0))],
        out_specs=pl.BlockSpec((tr, c), lambda i: (i, 0)),
        out_shape=jax.ShapeDtypeStruct((r, c), BF16),
        compiler_params=_cparams(("arbitrary",)),
        name="cast_bf16",
    )(w)


def kernel(x, norm_mix, w_in, dsa_cq_norm, dsa_w_uq, dsa_w_qidx, fox_f_bias, gdn_conv, gdn_a_log,
           gdn_dt_bias, gdn_norm, w_branch, w_out, rel_bias, norm_ffn, ffn_w_gate, ffn_w_up,
           ffn_conv, ffn_conv_b, ffn_w_down, final_norm):
    batch, seq, d = x.shape
    depth = w_in.shape[0]
    xf = x.reshape(batch * seq, d)
    ret_tables = _retention_tables(seq)
    w_branch2 = w_branch.reshape(depth, N_BRANCH * WIDTH, d)
    for l in range(depth):
        w_main, w_gate = prep_w_in(w_in, l)
        proj, h = norm_proj(xf, norm_mix[l], w_main)
        par = jnp.zeros((8, LANE), F32)
        par = par.at[0, L_FF:L_FF + N_HEADS].set(fox_f_bias[l])
        par = par.at[0, L_GA:L_GA + N_HEADS].set(gdn_dt_bias[l])
        par = par.at[1, L_GA:L_GA + N_HEADS].set(gdn_a_log[l])
        prep_tok, prep_tr = prep_small(proj, par, batch, seq)
        o_ret = retention(proj, ret_tables, batch, seq)
        q_dsa, q_idx = dsa_proj(proj, dsa_cq_norm[l], cast_layer(dsa_w_uq, l), cast_layer(dsa_w_qidx, l))
        o_dsa = dsa_attention(proj, q_dsa, q_idx, prep_tok, rel_bias, batch, seq)
        o_fox = fox_attention(proj, prep_tok, prep_tr, batch, seq)
        o_gdn = gated_deltanet(proj, prep_tok, prep_tr, gdn_conv[l], gdn_norm[l], batch, seq)
        wb = cast_layer(w_branch2, l).reshape(N_BRANCH, WIDTH, d)
        merged = merge_branches(h, (o_ret, o_dsa, o_fox, o_gdn), w_gate, wb)
        xf = resid_matmul(merged, cast_layer(w_out, l), xf, name="out_proj")
        act = conv_ffn_up(xf, norm_ffn[l], cast_layer(ffn_w_gate, l), cast_layer(ffn_w_up, l),
                          ffn_conv[l], ffn_conv_b[l], seq)
        xf = resid_matmul(act, cast_layer(ffn_w_down, l), xf, name="ffn_down")
    return rmsnorm(xf, final_norm).reshape(batch, seq, d)
```

```python
import functools
import math

import jax
import jax.numpy as jnp
from jax import lax
from jax.experimental import pallas as pl
from jax.experimental.pallas import tpu as pltpu

F32 = jnp.float32
BF16 = jnp.bfloat16
I32 = jnp.int32

HEAD_DIM = 128
N_HEADS = 4
WIDTH = N_HEADS * HEAD_DIM
N_BRANCH = 4
RET_CHUNK = 128
ROPE_BASE = 10000.0
DSA_Q_RANK = 384
IDX_HEADS = 16
IDX_DIM = 64
TOPK_MAX = 256
GDN_CONV = 4
GDN_CHUNK = 64
REL_BUCKETS = 32
REL_MAX_DIST = 128
FFN_CONV = 3
EPS = 1e-6

LANE = 128
VMEM_LIMIT = 56 * 1024 * 1024

C_RQ, C_RK, C_RV, C_RG = 0, 512, 1024, 1536
C_FQ, C_FK, C_FV = 2048, 2560, 3072
C_GQ, C_GK, C_GV, C_GZ = 3584, 4096, 4608, 5120
C_DK, C_DCQ, C_DV, C_KA, C_KB, C_SM = 5632, 5760, 6144, 6272, 6400, 6528
C_TOT = 6656
L_IW, L_FF, L_GB, L_GA = 0, 16, 20, 24

INT_MIN = -(2 ** 31)
INT_MAX = 2 ** 31 - 1
HIGHEST = lax.Precision.HIGHEST


def _cparams(sem, vmem=VMEM_LIMIT):
    return pltpu.CompilerParams(dimension_semantics=sem, vmem_limit_bytes=vmem)


def _dot(a, b):
    return jnp.dot(a, b, preferred_element_type=F32)


def _dot_nt(a, b):
    return lax.dot_general(a, b, (((1,), (1,)), ((), ())), preferred_element_type=F32)


def _silu(x):
    return x * jax.nn.sigmoid(x)


NORM_ROWS = 32


def _rmsnorm_rows(x_ref, g_ref, rows):
    x = x_ref[rows, :]
    ms = jnp.mean(x * x, axis=-1, keepdims=True)
    return x * lax.rsqrt(ms + EPS) * g_ref[...]


def _norm_proj_kernel(x_ref, g_ref, w_ref, o_ref, h_ref, h_scr, *, tm):
    @pl.when(pl.program_id(1) == 0)
    def _():
        def body(r, carry):
            rows = pl.ds(pl.multiple_of(r * NORM_ROWS, NORM_ROWS), NORM_ROWS)
            hb = _rmsnorm_rows(x_ref, g_ref, rows).astype(BF16)
            h_scr[rows, :] = hb
            h_ref[rows, :] = hb
            return carry
        lax.fori_loop(0, tm // NORM_ROWS, body, 0)

    o_ref[...] = _dot(h_scr[...], w_ref[...])


def norm_proj(x, gain, w, *, tm=1024, tn=512):
    n, d = x.shape
    c = w.shape[1]
    return pl.pallas_call(
        functools.partial(_norm_proj_kernel, tm=tm),
        grid=(n // tm, c // tn),
        in_specs=[pl.BlockSpec((tm, d), lambda i, j: (i, 0)),
                  pl.BlockSpec((1, d), lambda i, j: (0, 0)),
                  pl.BlockSpec((d, tn), lambda i, j: (0, j))],
        out_specs=[pl.BlockSpec((tm, tn), lambda i, j: (i, j)),
                   pl.BlockSpec((tm, d), lambda i, j: (i, 0))],
        out_shape=[jax.ShapeDtypeStruct((n, c), F32),
                   jax.ShapeDtypeStruct((n, d), BF16)],
        scratch_shapes=[pltpu.VMEM((tm, d), BF16)],
        compiler_params=_cparams(("arbitrary", "arbitrary")),
        name="norm_proj",
    )(x, gain.reshape(1, d), w)


def _rmsnorm_kernel(x_ref, g_ref, o_ref, *, tm):
    def body(r, carry):
        rows = pl.ds(pl.multiple_of(r * NORM_ROWS, NORM_ROWS), NORM_ROWS)
        o_ref[rows, :] = _rmsnorm_rows(x_ref, g_ref, rows)
        return carry
    lax.fori_loop(0, tm // NORM_ROWS, body, 0)


def rmsnorm(x, gain, *, tm=512):
    n, d = x.shape
    return pl.pallas_call(
        functools.partial(_rmsnorm_kernel, tm=tm),
        grid=(n // tm,),
        in_specs=[pl.BlockSpec((tm, d), lambda i: (i, 0)),
                  pl.BlockSpec((1, d), lambda i: (0, 0))],
        out_specs=pl.BlockSpec((tm, d), lambda i: (i, 0)),
        out_shape=jax.ShapeDtypeStruct((n, d), F32),
        compiler_params=_cparams(("arbitrary",)),
        name="final_rmsnorm",
    )(x, gain.reshape(1, d))


def _prep_kernel(s_ref, par_ref, tok_ref, tr_ref, carry_scr):
    @pl.when(pl.program_id(1) == 0)
    def _():
        carry_scr[...] = jnp.zeros_like(carry_scr)

    s = s_ref[...]
    lane = lax.broadcasted_iota(I32, (LANE, LANE), 1)
    row = lax.broadcasted_iota(I32, (LANE, LANE), 0)
    z = s + par_ref[0:1, :]
    soft = jnp.maximum(z, 0.0) + jnp.log1p(jnp.exp(-jnp.abs(z)))
    log_sig = z - soft
    sig = jax.nn.sigmoid(z)
    g_val = -jnp.exp(par_ref[1:2, :]) * soft
    is_f = (lane >= L_FF) & (lane < L_FF + N_HEADS)
    is_b = (lane >= L_GB) & (lane < L_GB + N_HEADS)
    is_a = (lane >= L_GA) & (lane < L_GA + N_HEADS)
    pre = jnp.where(is_f, log_sig, jnp.where(is_a, g_val, 0.0))
    tri = (row >= lane).astype(F32)
    tri_blk = ((row >= lane) & ((row // GDN_CHUNK) == (lane // GDN_CHUNK))).astype(F32)
    cum_full = jnp.dot(tri, pre, precision=HIGHEST, preferred_element_type=F32)
    cum_blk = jnp.dot(tri_blk, pre, precision=HIGHEST, preferred_element_type=F32)
    c_fox = cum_full + carry_scr[0:1, :]
    carry_scr[0:1, :] = c_fox[LANE - 1:LANE, :]
    scale_iw = IDX_HEADS ** -0.5 * IDX_DIM ** -0.5
    out = jnp.where(is_f, c_fox,
                    jnp.where(is_a, cum_blk,
                              jnp.where(is_b, sig,
                                        jnp.where(lane < IDX_HEADS, s * scale_iw, 0.0))))
    tok_ref[...] = out
    tr_ref[0] = out.T[0:32, :]


def prep_small(proj, par, batch, seq):
    n = proj.shape[0]
    nc = seq // LANE
    return pl.pallas_call(
        _prep_kernel,
        grid=(batch, nc),
        in_specs=[pl.BlockSpec((LANE, LANE), lambda b, c: (b * nc + c, C_SM // LANE)),
                  pl.BlockSpec((8, LANE), lambda b, c: (0, 0))],
        out_specs=[pl.BlockSpec((LANE, LANE), lambda b, c: (b * nc + c, 0)),
                   pl.BlockSpec((1, 32, LANE), lambda b, c: (b, 0, c))],
        out_shape=[jax.ShapeDtypeStruct((n, LANE), F32),
                   jax.ShapeDtypeStruct((batch, 32, seq), F32)],
        scratch_shapes=[pltpu.VMEM((8, LANE), F32)],
        compiler_params=_cparams(("arbitrary", "arbitrary")),
        name="prep_small",
    )(proj, par)


def _ret_gamma():
    return [math.log1p(-(2.0 ** (-5.0 - h))) for h in range(N_HEADS)]


def _retention_kernel(q_ref, k_ref, v_ref, g_ref, cos_ref, sin_ref, dec_ref, zeta_ref, xi_ref,
                      o_ref, state_scr):
    @pl.when(pl.program_id(1) == 0)
    def _():
        state_scr[...] = jnp.zeros_like(state_scr)

    cos_t = cos_ref[...]
    sin_t = sin_ref[...]
    log_gamma = _ret_gamma()
    for h in range(N_HEADS):
        sl = slice(h * HEAD_DIM, (h + 1) * HEAD_DIM)
        q = q_ref[:, sl]
        k = k_ref[:, sl]
        qr = q * cos_t + pltpu.roll(q, HEAD_DIM // 2, 1) * sin_t
        kr = (k * cos_t + pltpu.roll(k, HEAD_DIM // 2, 1) * sin_t) * (HEAD_DIM ** -0.5)
        qb = qr.astype(BF16)
        kb = kr.astype(BF16)
        vb = v_ref[:, sl].astype(BF16)
        inner = _dot_nt(qb, kb) * dec_ref[h]
        st = state_scr[h]
        o = _dot(inner.astype(BF16), vb) + _dot(qb, st.astype(BF16)) * xi_ref[h]
        kz_t = (kr * zeta_ref[h]).T.astype(BF16)
        state_scr[h] = st * math.exp(log_gamma[h] * RET_CHUNK) + _dot(kz_t, vb)
        mu = jnp.mean(o, axis=-1, keepdims=True)
        oc = o - mu
        var = jnp.mean(oc * oc, axis=-1, keepdims=True)
        o_ref[:, sl] = (_silu(g_ref[:, sl]) * (oc * lax.rsqrt(var + EPS))).astype(BF16)


def _retention_tables(seq):
    half = HEAD_DIM // 2
    inv = 1.0 / (ROPE_BASE ** (jnp.arange(half, dtype=F32) / half))
    ang = jnp.arange(seq).astype(F32)[:, None] * inv[None, :]
    cos, sin = jnp.cos(ang), jnp.sin(ang)
    cos_t = jnp.concatenate([cos, cos], axis=-1)
    sin_t = jnp.concatenate([-sin, sin], axis=-1)
    c = RET_CHUNK
    log_gamma = jnp.log1p(-jnp.exp2(-5.0 - jnp.arange(N_HEADS, dtype=F32)))
    n = jnp.arange(c, dtype=F32)
    diff = n[:, None] - n[None, :]
    decay = jnp.where(diff >= 0, jnp.exp(log_gamma[:, None, None] * jnp.maximum(diff, 0.0)), 0.0)
    zeta = jnp.exp(log_gamma[:, None] * (c - 1 - n)[None, :])
    xi = jnp.exp(log_gamma[:, None] * (n + 1)[None, :])
    ones = jnp.ones((1, 1, HEAD_DIM), F32)
    return cos_t, sin_t, decay, zeta[:, :, None] * ones, xi[:, :, None] * ones


def retention(proj, tables, batch, seq):
    n = proj.shape[0]
    c = RET_CHUNK
    nc = seq // c
    cos_t, sin_t, decay, zeta, xi = tables
    col = lambda off: pl.BlockSpec((c, WIDTH), lambda b, i: (b * nc + i, off // WIDTH))
    full3 = pl.BlockSpec((N_HEADS, c, HEAD_DIM), lambda b, i: (0, 0, 0))
    return pl.pallas_call(
        _retention_kernel,
        grid=(batch, nc),
        in_specs=[col(C_RQ), col(C_RK), col(C_RV), col(C_RG),
                  pl.BlockSpec((c, HEAD_DIM), lambda b, i: (i, 0)),
                  pl.BlockSpec((c, HEAD_DIM), lambda b, i: (i, 0)),
                  full3, full3, full3],
        out_specs=pl.BlockSpec((c, WIDTH), lambda b, i: (b * nc + i, 0)),
        out_shape=jax.ShapeDtypeStruct((n, WIDTH), BF16),
        scratch_shapes=[pltpu.VMEM((N_HEADS, HEAD_DIM, HEAD_DIM), F32)],
        compiler_params=_cparams(("arbitrary", "arbitrary")),
        name="retention",
    )(proj, proj, proj, proj, cos_t, sin_t, decay, zeta, xi)


def _fox_kernel(q_ref, k_ref, v_ref, ctok_ref, ctr_ref, o_ref, m_scr, l_scr, acc_scr, *, t):
    qi = pl.program_id(1)
    ki = pl.program_id(2)

    @pl.when(ki == 0)
    def _():
        m_scr[...] = jnp.full_like(m_scr, -jnp.inf)
        l_scr[...] = jnp.zeros_like(l_scr)
        acc_scr[...] = jnp.zeros_like(acc_scr)

    def step(masked):
        if masked:
            row = lax.broadcasted_iota(I32, (t, t), 0)
            colm = lax.broadcasted_iota(I32, (t, t), 1)
            keep = row >= colm
        for h in range(N_HEADS):
            sl = slice(h * HEAD_DIM, (h + 1) * HEAD_DIM)
            qb = q_ref[:, sl].astype(BF16)
            kb = k_ref[:, sl].astype(BF16)
            c_q = ctok_ref[:, L_FF + h:L_FF + h + 1]
            c_k = ctr_ref[0, L_FF + h:L_FF + h + 1, :]
            s = _dot_nt(qb, kb) * (HEAD_DIM ** -0.5) + (c_q - c_k)
            if masked:
                s = jnp.where(keep, s, -jnp.inf)
            m_old = m_scr[h]
            m_new = jnp.maximum(m_old, jnp.max(s, axis=-1, keepdims=True))
            alpha = jnp.exp(m_old - m_new)
            p = jnp.exp(s - m_new)
            l_scr[h] = alpha * l_scr[h] + jnp.sum(p, axis=-1, keepdims=True)
            acc_scr[h] = alpha * acc_scr[h] + _dot(p.astype(BF16), v_ref[:, sl].astype(BF16))
            m_scr[h] = m_new

    @pl.when(ki < qi)
    def _():
        step(False)

    @pl.when(ki == qi)
    def _():
        step(True)
        for h in range(N_HEADS):
            sl = slice(h * HEAD_DIM, (h + 1) * HEAD_DIM)
            o_ref[:, sl] = (acc_scr[h] / l_scr[h]).astype(BF16)


def fox_attention(proj, prep_tok, prep_tr, batch, seq, *, t=512):
    n = proj.shape[0]
    nt = seq // t
    qspec = pl.BlockSpec((t, WIDTH), lambda b, qi, ki: (b * nt + qi, C_FQ // WIDTH))
    kspec = lambda off: pl.BlockSpec(
        (t, WIDTH), lambda b, qi, ki: (b * nt + jnp.minimum(ki, qi), off // WIDTH))
    return pl.pallas_call(
        functools.partial(_fox_kernel, t=t),
        grid=(batch, nt, nt),
        in_specs=[qspec, kspec(C_FK), kspec(C_FV),
                  pl.BlockSpec((t, LANE), lambda b, qi, ki: (b * nt + qi, 0)),
                  pl.BlockSpec((1, 32, t), lambda b, qi, ki: (b, 0, jnp.minimum(ki, qi)))],
        out_specs=pl.BlockSpec((t, WIDTH), lambda b, qi, ki: (b * nt + qi, 0)),
        out_shape=jax.ShapeDtypeStruct((n, WIDTH), BF16),
        scratch_shapes=[pltpu.VMEM((N_HEADS, t, 1), F32),
                        pltpu.VMEM((N_HEADS, t, 1), F32),
                        pltpu.VMEM((N_HEADS, t, HEAD_DIM), F32)],
        compiler_params=_cparams(("arbitrary", "arbitrary", "arbitrary")),
        name="fox_attention",
    )(proj, proj, proj, prep_tok, prep_tr)


def _dsa_proj_kernel(cq_ref, g_ref, wq_ref, wi_ref, q_ref, qi_ref):
    x = cq_ref[...]
    ms = jnp.mean(x * x, axis=-1, keepdims=True)
    cb = (x * lax.rsqrt(ms + EPS) * g_ref[...]).astype(BF16)
    q_ref[...] = _dot(cb, wq_ref[...]).astype(BF16)
    qi_ref[...] = _dot(cb, wi_ref[...]).astype(BF16)


def dsa_proj(proj, cq_norm, w_uq, w_qidx, *, tm=512):
    n = proj.shape[0]
    r = DSA_Q_RANK
    wi = IDX_HEADS * IDX_DIM
    return pl.pallas_call(
        _dsa_proj_kernel,
        grid=(n // tm,),
        in_specs=[pl.BlockSpec((tm, r), lambda i: (i, C_DCQ // r)),
                  pl.BlockSpec((1, r), lambda i: (0, 0)),
                  pl.BlockSpec((r, WIDTH), lambda i: (0, 0)),
                  pl.BlockSpec((r, wi), lambda i: (0, 0))],
        out_specs=[pl.BlockSpec((tm, WIDTH), lambda i: (i, 0)),
                   pl.BlockSpec((tm, wi), lambda i: (i, 0))],
        out_shape=[jax.ShapeDtypeStruct((n, WIDTH), BF16),
                   jax.ShapeDtypeStruct((n, wi), BF16)],
        compiler_params=_cparams(("arbitrary",)),
        name="dsa_proj",
    )(proj, cq_norm.reshape(1, r), w_uq, w_qidx)


DSA_QB = 256
DSA_KC = 512
DSA_SCORE_MID_STEPS = 20
DSA_FEW_KEYS = 4


def _t5_bucket(rel):
    max_exact = REL_BUCKETS // 2
    relf = jnp.maximum(rel, max_exact).astype(F32)
    large = max_exact + (jnp.log(relf / max_exact) / math.log(REL_MAX_DIST / max_exact)
                         * (REL_BUCKETS - max_exact)).astype(I32)
    large = jnp.minimum(large, REL_BUCKETS - 1)
    return jnp.where(rel < max_exact, rel, large)


def _dsa_kernel(rb_ref, q_ref, qi_ref, tok_ref, k_ref, v_ref, ka_ref, kb_ref, o_ref,
                key_scr, lg_scr, band_scr, kb16_scr, vb16_scr, ka16_scr, kb16i_scr, *, seq, topk):
    qb_idx = pl.program_id(1)
    t0 = qb_idx * DSA_QB
    n_kc = (t0 + DSA_QB - 1) // DSA_KC + 1

    @pl.when(qb_idx == 0)
    def _():
        kb16_scr[...] = k_ref[...].astype(BF16)
        vb16_scr[...] = v_ref[...].astype(BF16)
        ka16_scr[...] = ka_ref[...].astype(BF16)
        kb16i_scr[...] = kb_ref[...].astype(BF16)

    @pl.when((pl.program_id(0) == 0) & (qb_idx == 0))
    def _():
        i_ = lax.broadcasted_iota(I32, (DSA_QB, 2 * DSA_QB), 0)
        j_ = lax.broadcasted_iota(I32, (DSA_QB, 2 * DSA_QB), 1)
        rel = i_ + DSA_QB - j_
        bucket = _t5_bucket(rel)
        for h in range(N_HEADS):
            far = rb_ref[REL_BUCKETS - 1, h]
            band = jnp.zeros((DSA_QB, 2 * DSA_QB), F32)
            for bk in range(REL_BUCKETS - 1):
                band = jnp.where(bucket == bk, rb_ref[bk, h] - far, band)
            band_scr[h] = jnp.where(rel >= 0, band, 0.0)

    w_all = tok_ref[...]
    row_t = t0 + lax.broadcasted_iota(I32, (DSA_QB, DSA_KC), 0)

    def score_chunk(c, carry):
        kmax, kmin = carry
        ks = pl.ds(pl.multiple_of(c * DSA_KC, DSA_KC), DSA_KC)
        ka = ka16_scr[ks, :]
        kb = kb16i_scr[ks, :]
        acc = jnp.zeros((DSA_QB, DSA_KC), F32)
        for p in range(IDX_HEADS // 2):
            qp = qi_ref[:, p * LANE:(p + 1) * LANE]
            acc = acc + jnp.maximum(_dot_nt(qp, ka), 0.0) * w_all[:, 2 * p:2 * p + 1]
            acc = acc + jnp.maximum(_dot_nt(qp, kb), 0.0) * w_all[:, 2 * p + 1:2 * p + 2]
        bits = pltpu.bitcast(acc, I32)
        key = bits ^ ((bits >> 31) & 0x7FFFFFFF)
        col_s = c * DSA_KC + lax.broadcasted_iota(I32, (DSA_QB, DSA_KC), 1)
        valid = col_s <= row_t
        key_scr[:, ks] = jnp.where(valid, key, INT_MIN)
        kmax = jnp.maximum(kmax, jnp.max(jnp.where(valid, key, INT_MIN), axis=-1, keepdims=True))
        kmin = jnp.minimum(kmin, jnp.min(jnp.where(valid, key, INT_MAX), axis=-1, keepdims=True))
        return kmax, kmin

    kmax, kmin = lax.fori_loop(0, n_kc, score_chunk, (jnp.full((DSA_QB, 1), INT_MIN, I32),
                                                     jnp.full((DSA_QB, 1), INT_MAX, I32)))

    def scan_keys(cand, with_below):
        def body(c, carry):
            cnt, below = carry
            ks = pl.ds(pl.multiple_of(c * DSA_KC, DSA_KC), DSA_KC)
            keys = key_scr[:, ks]
            ge = keys >= cand
            ones = ge.astype(I32)
            low = jnp.where(ge, INT_MIN, keys)
            for u in range(DSA_KC // LANE):
                cnt = cnt + ones[:, u * LANE:(u + 1) * LANE]
                if with_below:
                    below = jnp.maximum(below, low[:, u * LANE:(u + 1) * LANE])
            return cnt, below
        cnt, below = lax.fori_loop(0, n_kc, body, (jnp.zeros((DSA_QB, LANE), I32),
                                                   jnp.full((DSA_QB, LANE), INT_MIN, I32)))
        cnt = jnp.sum(cnt, axis=-1, keepdims=True)
        if with_below:
            return cnt, jnp.max(below, axis=-1, keepdims=True)
        return cnt

    def key_to_score(k):
        return pltpu.bitcast(k ^ ((k >> 31) & 0x7FFFFFFF), F32)

    def score_to_key(s):
        b = pltpu.bitcast(s, I32)
        return b ^ ((b >> 31) & 0x7FFFFFFF)

    def open_rows(lo, hi, c_lo):
        return (c_lo > topk) & (hi - 1 > lo)

    def any_row(flag):
        return jnp.max(jnp.where(flag, 1, 0))

    def update(cand, cnt, lo, hi, c_lo, c_hi):
        ge = cnt >= topk
        return (jnp.where(ge, cand, lo), jnp.where(ge, hi, cand),
                jnp.where(ge, cnt, c_lo), jnp.where(ge, c_hi, cnt))

    def crowded(lo, hi, c_lo, c_hi):
        return any_row(open_rows(lo, hi, c_lo) & (c_lo - c_hi > DSA_FEW_KEYS))

    def halve_body(st):
        it, _, lo, hi, c_lo, c_hi = st
        key_mid = (lo >> 1) + (hi >> 1) + (lo & hi & 1)
        score_mid = score_to_key(0.5 * key_to_score(lo) + 0.5 * key_to_score(hi - 1))
        cand = jnp.where(it < DSA_SCORE_MID_STEPS, score_mid, key_mid)
        cand = jnp.minimum(jnp.maximum(cand, lo + 1), hi - 1)
        cand = jnp.where(hi - 1 > lo, cand, lo)
        lo, hi, c_lo, c_hi = update(cand, scan_keys(cand, False), lo, hi, c_lo, c_hi)
        return it + 1, crowded(lo, hi, c_lo, c_hi), lo, hi, c_lo, c_hi

    n_valid = jnp.minimum(t0 + lax.broadcasted_iota(I32, (DSA_QB, 1), 0) + 1, seq)
    lo0 = kmin
    hi0 = kmax + 1
    c_hi0 = jnp.zeros((DSA_QB, 1), I32)
    _, _, lo, hi, c_lo, c_hi = lax.while_loop(
        lambda st: st[1] > 0, halve_body,
        (jnp.int32(0), crowded(lo0, hi0, n_valid, c_hi0), lo0, hi0, n_valid, c_hi0))

    def walk_body(st):
        _, lo, hi, c_lo, c_hi, nxt = st
        is_open = open_rows(lo, hi, c_lo)
        cand = jnp.where(is_open, nxt, lo)
        cnt, below = scan_keys(cand, True)
        ge = cnt >= topk
        hi = jnp.where(is_open, jnp.where(ge, cand + 1, cand), hi)
        c_hi = jnp.where(is_open & jnp.logical_not(ge), cnt, c_hi)
        lo = jnp.where(is_open & ge, cand, lo)
        c_lo = jnp.where(is_open & ge, cnt, c_lo)
        nxt = jnp.where(ge, nxt, below)
        return any_row(open_rows(lo, hi, c_lo)), lo, hi, c_lo, c_hi, nxt

    _, nxt0 = scan_keys(hi, True)
    _, thr, hi, n_ge, n_gt, _ = lax.while_loop(
        lambda st: st[0] > 0, walk_body,
        (any_row(open_rows(lo, hi, c_lo)), lo, hi, c_lo, c_hi, nxt0))

    tied = n_ge > topk
    has_tie = jnp.max(jnp.where(tied, 1, 0)) > 0

    @pl.when(has_tie)
    def _():
        room = (topk - n_gt).astype(F32)
        ii = lax.broadcasted_iota(I32, (LANE, LANE), 0)
        jj = lax.broadcasted_iota(I32, (LANE, LANE), 1)
        upper = (ii <= jj).astype(BF16)

        def body(c, seen):
            ks = pl.ds(pl.multiple_of(c * LANE, LANE), LANE)
            kk = key_scr[:, ks]
            eq = kk == thr
            rank = seen + _dot(eq.astype(BF16), upper)
            drop = eq & (rank > room) & tied
            key_scr[:, ks] = jnp.where(drop, INT_MIN, kk)
            return seen + jnp.sum(eq.astype(F32), axis=-1, keepdims=True)
        lax.fori_loop(0, n_kc * (DSA_KC // LANE), body, jnp.zeros((DSA_QB, 1), F32))

    def mask_chunk(c, carry):
        ks = pl.ds(pl.multiple_of(c * DSA_KC, DSA_KC), DSA_KC)
        sel = jnp.where(key_scr[:, ks] >= thr, 0.0, -jnp.inf).astype(F32)
        key_scr[:, ks] = pltpu.bitcast(sel, I32)
        return carry
    lax.fori_loop(0, n_kc, mask_chunk, 0)

    band_off = pl.multiple_of(jnp.maximum(qb_idx - 1, 0) * DSA_QB, DSA_QB)
    for h in range(N_HEADS):
        sl = slice(h * HEAD_DIM, (h + 1) * HEAD_DIM)
        qh = q_ref[:, sl]

        def logit_chunk(c, m):
            ks = pl.ds(pl.multiple_of(c * DSA_KC, DSA_KC), DSA_KC)
            s = (_dot_nt(qh, kb16_scr[ks, :]) * (HEAD_DIM ** -0.5)
                 + pltpu.bitcast(key_scr[:, ks], F32))
            lg_scr[:, ks] = s
            return jnp.maximum(m, jnp.max(s, axis=-1, keepdims=True))
        m = lax.fori_loop(0, n_kc, logit_chunk, jnp.full((DSA_QB, 1), -jnp.inf, F32))

        band_h = band_scr[h]
        band_first = jnp.concatenate([band_h[:, DSA_QB:], jnp.zeros((DSA_QB, DSA_QB), F32)], axis=1)
        ws = pl.ds(band_off, 2 * DSA_QB)
        win = lg_scr[:, ws] + jnp.where(qb_idx == 0, band_first, band_h)
        lg_scr[:, ws] = win
        m = jnp.maximum(m, jnp.max(win, axis=-1, keepdims=True))

        def pv_chunk(c, carry):
            l, acc = carry
            ks = pl.ds(pl.multiple_of(c * DSA_KC, DSA_KC), DSA_KC)
            p = jnp.exp(lg_scr[:, ks] - m)
            l = l + jnp.sum(p, axis=-1, keepdims=True)
            acc = acc + _dot(p.astype(BF16), vb16_scr[ks, :])
            return l, acc
        l, acc = lax.fori_loop(0, n_kc, pv_chunk,
                               (jnp.zeros((DSA_QB, 1), F32), jnp.zeros((DSA_QB, HEAD_DIM), F32)))
        o_ref[:, sl] = (acc / l).astype(BF16)


def dsa_attention(proj, q, q_idx, prep_tok, rel_bias, batch, seq):
    n = proj.shape[0]
    nq = seq // DSA_QB
    topk = min(TOPK_MAX, seq // 4)
    wi = IDX_HEADS * IDX_DIM
    rowblk = lambda w, cb: pl.BlockSpec((DSA_QB, w), lambda b, i: (b * nq + i, cb))
    seqblk = lambda off: pl.BlockSpec((seq, LANE), lambda b, i: (b, off // LANE))
    return pl.pallas_call(
        functools.partial(_dsa_kernel, seq=seq, topk=topk),
        grid=(batch, nq),
        in_specs=[pl.BlockSpec(memory_space=pltpu.SMEM),
                  rowblk(WIDTH, 0), rowblk(wi, 0), rowblk(LANE, 0),
                  seqblk(C_DK), seqblk(C_DV), seqblk(C_KA), seqblk(C_KB)],
        out_specs=pl.BlockSpec((DSA_QB, WIDTH), lambda b, i: (b * nq + i, 0)),
        out_shape=jax.ShapeDtypeStruct((n, WIDTH), BF16),
        scratch_shapes=[pltpu.VMEM((DSA_QB, seq), I32),
                        pltpu.VMEM((DSA_QB, seq), F32),
                        pltpu.VMEM((N_HEADS, DSA_QB, 2 * DSA_QB), F32),
                        pltpu.VMEM((seq, LANE), BF16),
                        pltpu.VMEM((seq, LANE), BF16),
                        pltpu.VMEM((seq, LANE), BF16),
                        pltpu.VMEM((seq, LANE), BF16)],
        compiler_params=_cparams(("arbitrary", "arbitrary")),
        name="dsa_attention",
    )(rel_bias, q, q_idx, prep_tok, proj, proj, proj, proj)


GDN_T = 256
GDN_HALO = 8


def _gdn_kernel(q_ref, k_ref, v_ref, z_ref, cw_ref, ng_ref, tok_ref, tr_ref, o_ref,
                xq_scr, xk_scr, xv_scr, state_scr):
    first = pl.program_id(1) == 0

    @pl.when(first)
    def _():
        state_scr[...] = jnp.zeros_like(state_scr)
        for scr in (xq_scr, xk_scr, xv_scr):
            scr[0:GDN_HALO, :] = jnp.zeros((GDN_HALO, WIDTH), F32)

    def conv(x_ref, scr, w_off):
        scr[GDN_HALO:, :] = x_ref[...]
        y = jnp.zeros((GDN_T, WIDTH), F32)
        for i in range(GDN_CONV):
            st = GDN_HALO - (GDN_CONV - 1) + i
            y = y + scr[st:st + GDN_T, :] * cw_ref[i:i + 1, w_off:w_off + WIDTH]
        scr[0:GDN_HALO, :] = scr[GDN_T:GDN_T + GDN_HALO, :]
        return _silu(y)

    qc = conv(q_ref, xq_scr, 0)
    kc = conv(k_ref, xk_scr, WIDTH)
    vc = conv(v_ref, xv_scr, 2 * WIDTH)
    tok = tok_ref[...]
    c = GDN_CHUNK
    heads = range(N_HEADS)
    hsl = [slice(h * HEAD_DIM, (h + 1) * HEAD_DIM) for h in heads]

    def l2norm_heads(x, scale):
        return jnp.concatenate(
            [x[:, s] * (lax.rsqrt(jnp.sum(x[:, s] * x[:, s], axis=-1, keepdims=True) + EPS) * scale)
             for s in hsl], axis=1)

    qf = l2norm_heads(qc, HEAD_DIM ** -0.5)
    kf = l2norm_heads(kc, 1.0)

    nb = N_HEADS * c
    ri = lax.broadcasted_iota(I32, (nb, nb), 0)
    ci = lax.broadcasted_iota(I32, (nb, nb), 1)
    same_head = (ri // c) == (ci // c)
    tril = same_head & (ri >= ci)
    strict = same_head & (ri > ci)
    eye = (ri == ci).astype(F32)
    lane_head = lax.broadcasted_iota(I32, (c, WIDTH), 1) // HEAD_DIM
    row_head = lax.broadcasted_iota(I32, (nb, HEAD_DIM), 0) // c

    def spread(x):
        return jnp.concatenate([jnp.where(lane_head == h, x, 0.0) for h in heads], axis=0)

    def stack(x):
        return jnp.concatenate([x[:, s] for s in hsl], axis=0)

    def spread_lanes(x):
        return jnp.concatenate([jnp.where(row_head == h, x, 0.0) for h in heads], axis=1)

    def split(x):
        hi = x.astype(BF16)
        return hi, (x - hi.astype(F32)).astype(BF16)

    def dot_split(a, b):
        return _dot(a[0], b[0]) + (_dot(a[0], b[1]) + _dot(a[1], b[0]))

    outs = [[] for _ in heads]
    for j in range(GDN_T // c):
        rs = slice(j * c, (j + 1) * c)
        last = slice((j + 1) * c - 1, (j + 1) * c)
        qj, kj, vj = qf[rs], kf[rs], vc[rs]
        b_col = jnp.concatenate([tok[rs, L_GB + h:L_GB + h + 1] for h in heads], axis=0)
        g_col = jnp.concatenate([tok[rs, L_GA + h:L_GA + h + 1] for h in heads], axis=0)
        g_row = jnp.concatenate([tr_ref[0, L_GA + h:L_GA + h + 1, rs] for h in heads], axis=1)
        g_last = [tok[last, L_GA + h:L_GA + h + 1] for h in heads]
        g_last_col = jnp.concatenate([jnp.broadcast_to(g, (c, 1)) for g in g_last], axis=0)
        k_sp = spread(kj)
        q_sp = spread(qj)
        k_sp16 = k_sp.astype(BF16)
        decay = jnp.exp(jnp.where(tril, g_col - g_row, -jnp.inf))
        neg_l = -jnp.where(strict, b_col * _dot_nt(k_sp16, k_sp16) * decay, 0.0)
        t_inv = eye + neg_l
        pw_s = split(neg_l)
        for _ in range(5):
            pw_s = split(dot_split(pw_s, pw_s))
            t_inv = t_inv + dot_split(split(t_inv), pw_s)
        eg = jnp.exp(g_col)
        rhs = jnp.concatenate([stack(vj) * b_col, stack(kj) * (b_col * eg)], axis=1)
        sol = dot_split(split(t_inv), split(rhs))
        u0 = sol[:, :HEAD_DIM]
        kcum = sol[:, HEAD_DIM:]
        qk = _dot_nt(q_sp.astype(BF16), k_sp16) * decay
        q_dec = q_sp * eg
        k_dec = k_sp * jnp.exp(g_last_col - g_col)
        st = state_scr[...]
        stb = st.astype(BF16)
        v_new = u0 - _dot(spread_lanes(kcum).astype(BF16), stb)
        v_new_b = v_new.astype(BF16)
        o_st = _dot(q_dec.astype(BF16), stb) + _dot(qk.astype(BF16), v_new_b)
        e_last = jnp.concatenate([jnp.broadcast_to(jnp.exp(g), (HEAD_DIM, 1)) for g in g_last], axis=0)
        state_scr[...] = st * e_last + _dot(k_dec.T.astype(BF16), v_new_b)
        for h in heads:
            outs[h].append(o_st[h * c:(h + 1) * c, :])

    for h in heads:
        o = jnp.concatenate(outs[h], axis=0)
        ms = jnp.mean(o * o, axis=-1, keepdims=True)
        on = o * lax.rsqrt(ms + EPS) * ng_ref[...]
        o_ref[:, hsl[h]] = (on * _silu(z_ref[:, hsl[h]])).astype(BF16)


def gated_deltanet(proj, prep_tok, prep_tr, conv_w, norm_g, batch, seq):
    n = proj.shape[0]
    t = GDN_T
    nt = seq // t
    col = lambda off: pl.BlockSpec((t, WIDTH), lambda b, i: (b * nt + i, off // WIDTH))
    return pl.pallas_call(
        _gdn_kernel,
        grid=(batch, nt),
        in_specs=[col(C_GQ), col(C_GK), col(C_GV), col(C_GZ),
                  pl.BlockSpec((GDN_CONV, 3 * WIDTH), lambda b, i: (0, 0)),
                  pl.BlockSpec((1, HEAD_DIM), lambda b, i: (0, 0)),
                  pl.BlockSpec((t, LANE), lambda b, i: (b * nt + i, 0)),
                  pl.BlockSpec((1, 32, t), lambda b, i: (b, 0, i))],
        out_specs=pl.BlockSpec((t, WIDTH), lambda b, i: (b * nt + i, 0)),
        out_shape=jax.ShapeDtypeStruct((n, WIDTH), BF16),
        scratch_shapes=[pltpu.VMEM((t + GDN_HALO, WIDTH), F32),
                        pltpu.VMEM((t + GDN_HALO, WIDTH), F32),
                        pltpu.VMEM((t + GDN_HALO, WIDTH), F32),
                        pltpu.VMEM((N_HEADS * HEAD_DIM, HEAD_DIM), F32)],
        compiler_params=_cparams(("arbitrary", "arbitrary")),
        name="gated_deltanet",
    )(proj, proj, proj, proj, conv_w, norm_g.reshape(1, HEAD_DIM), prep_tok, prep_tr)


def _merge_kernel(h_ref, b0_ref, b1_ref, b2_ref, b3_ref, g0_ref, g1_ref, g2_ref, g3_ref,
                  wb_ref, o_ref):
    h = h_ref[...]
    acc = None
    for n, (b_ref, g_ref) in enumerate(zip((b0_ref, b1_ref, b2_ref, b3_ref),
                                           (g0_ref, g1_ref, g2_ref, g3_ref))):
        gate = jax.nn.sigmoid(_dot(h, g_ref[...]))
        term = gate * _dot(b_ref[...], wb_ref[n])
        acc = term if acc is None else acc + term
    o_ref[...] = acc.astype(BF16)


def merge_branches(h, branches, w_gate, w_branch, *, tm=512, tn=512):
    n, d = h.shape
    nj = d // tn
    bspec = pl.BlockSpec((tm, WIDTH), lambda j, i: (i, 0))
    gspec = lambda k: pl.BlockSpec((d, tn), lambda j, i: (0, k * nj + j))
    return pl.pallas_call(
        _merge_kernel,
        grid=(nj, n // tm),
        in_specs=[pl.BlockSpec((tm, d), lambda j, i: (i, 0)),
                  bspec, bspec, bspec, bspec,
                  gspec(0), gspec(1), gspec(2), gspec(3),
                  pl.BlockSpec((N_BRANCH, WIDTH, tn), lambda j, i: (0, 0, j))],
        out_specs=pl.BlockSpec((tm, tn), lambda j, i: (i, j)),
        out_shape=jax.ShapeDtypeStruct((n, d), BF16),
        compiler_params=_cparams(("arbitrary", "arbitrary")),
        name="merge_branches",
    )(h, *branches, w_gate, w_gate, w_gate, w_gate, w_branch)


def _resid_mm_kernel(a_ref, w_ref, x_ref, o_ref):
    o_ref[...] = x_ref[...] + _dot(a_ref[...], w_ref[...])


def resid_matmul(a, w, x, *, tm=512, tn=1024, name="resid_matmul"):
    n, k = a.shape
    d = w.shape[1]
    return pl.pallas_call(
        _resid_mm_kernel,
        grid=(d // tn, n // tm),
        in_specs=[pl.BlockSpec((tm, k), lambda j, i: (i, 0)),
                  pl.BlockSpec((k, tn), lambda j, i: (0, j)),
                  pl.BlockSpec((tm, tn), lambda j, i: (i, j))],
        out_specs=pl.BlockSpec((tm, tn), lambda j, i: (i, j)),
        out_shape=jax.ShapeDtypeStruct((n, d), F32),
        compiler_params=_cparams(("arbitrary", "arbitrary")),
        name=name,
    )(a, w, x)


FFN_HALO = 8


def _ffn1_kernel(x_ref, g_ref, wg_ref, wu_ref, cw_ref, cb_ref, o_ref, h_scr, gt_scr, halo_scr,
                 *, tm, tiles_per_seq):
    i = pl.program_id(0)
    j = pl.program_id(1)

    @pl.when(j == 0)
    def _():
        def body(r, carry):
            rows = pl.ds(pl.multiple_of(r * NORM_ROWS, NORM_ROWS), NORM_ROWS)
            h_scr[rows, :] = _rmsnorm_rows(x_ref, g_ref, rows).astype(BF16)
            return carry
        lax.fori_loop(0, tm // NORM_ROWS, body, 0)

    h = h_scr[...]
    gt_scr[FFN_HALO:, :] = _dot(h, wg_ref[...])
    seq_start = (i % tiles_per_seq) == 0

    @pl.when(seq_start)
    def _():
        gt_scr[0:FFN_HALO, :] = jnp.zeros((FFN_HALO, gt_scr.shape[1]), F32)

    @pl.when(jnp.logical_not(seq_start))
    def _():
        gt_scr[0:FFN_HALO, :] = halo_scr[j]

    halo_scr[j] = gt_scr[tm:tm + FFN_HALO, :]
    y = cb_ref[...]
    for t in range(FFN_CONV):
        st = FFN_HALO - (FFN_CONV - 1) + t
        y = y + gt_scr[st:st + tm, :] * cw_ref[t:t + 1, :]
    o_ref[...] = (_silu(y) * _dot(h, wu_ref[...])).astype(BF16)


def conv_ffn_up(x, gain, w_gate, w_up, conv_w, conv_b, seq, *, tm=512, tn=512):
    n, d = x.shape
    f = w_gate.shape[1]
    nj = f // tn
    return pl.pallas_call(
        functools.partial(_ffn1_kernel, tm=tm, tiles_per_seq=seq // tm),
        grid=(n // tm, nj),
        in_specs=[pl.BlockSpec((tm, d), lambda i, j: (i, 0)),
                  pl.BlockSpec((1, d), lambda i, j: (0, 0)),
                  pl.BlockSpec((d, tn), lambda i, j: (0, j)),
                  pl.BlockSpec((d, tn), lambda i, j: (0, j)),
                  pl.BlockSpec((FFN_CONV, tn), lambda i, j: (0, j)),
                  pl.BlockSpec((1, tn), lambda i, j: (0, j))],
        out_specs=pl.BlockSpec((tm, tn), lambda i, j: (i, j)),
        out_shape=jax.ShapeDtypeStruct((n, f), BF16),
        scratch_shapes=[pltpu.VMEM((tm, d), BF16),
                        pltpu.VMEM((tm + FFN_HALO, tn), F32),
                        pltpu.VMEM((nj, FFN_HALO, tn), F32)],
        compiler_params=_cparams(("arbitrary", "arbitrary")),
        name="conv_ffn_up",
    )(x, gain.reshape(1, d), w_gate, w_up, conv_w, conv_b.reshape(1, f))


IN_SIZES = (WIDTH, WIDTH, WIDTH, WIDTH,
            DSA_Q_RANK, HEAD_DIM, HEAD_DIM, IDX_DIM, IDX_HEADS,
            WIDTH, WIDTH, WIDTH, N_HEADS,
            WIDTH, WIDTH, WIDTH, WIDTH, N_HEADS, N_HEADS)
IN_NAMES = ("r_q", "r_k", "r_v", "r_g", "d_cq", "d_k", "d_v", "i_k", "i_w",
            "f_q", "f_k", "f_v", "f_f", "g_q", "g_k", "g_v", "g_z", "g_b", "g_a")
IN_PLAN = (("r_q", C_RQ), ("r_k", C_RK), ("r_v", C_RV), ("r_g", C_RG),
           ("f_q", C_FQ), ("f_k", C_FK), ("f_v", C_FV),
           ("g_q", C_GQ), ("g_k", C_GK), ("g_v", C_GV), ("g_z", C_GZ),
           ("d_k", C_DK), ("d_cq", C_DCQ), ("d_v", C_DV),
           ("i_k", C_KA), ("i_k", C_KB + IDX_DIM),
           ("i_w", C_SM + L_IW), ("f_f", C_SM + L_FF), ("g_b", C_SM + L_GB), ("g_a", C_SM + L_GA))


def _prep_w_in_kernel(w_ref, m_ref, g_ref):
    src = {}
    off = 0
    for name, size in zip(IN_NAMES, IN_SIZES):
        src[name] = (off, size)
        off += size
    m_ref[...] = jnp.zeros_like(m_ref)
    for name, dst in IN_PLAN:
        so, w = src[name]
        m_ref[:, dst:dst + w] = w_ref[:, so:so + w].astype(BF16)
    g_ref[...] = w_ref[:, off:off + g_ref.shape[1]].astype(BF16)


def prep_w_in(w_in, layer, *, tr=128):
    _, d, c = w_in.shape
    return pl.pallas_call(
        _prep_w_in_kernel,
        grid=(d // tr,),
        in_specs=[pl.BlockSpec((None, tr, c), lambda i: (layer, i, 0))],
        out_specs=[pl.BlockSpec((tr, C_TOT), lambda i: (i, 0)),
                   pl.BlockSpec((tr, N_BRANCH * d), lambda i: (i, 0))],
        out_shape=[jax.ShapeDtypeStruct((d, C_TOT), BF16),
                   jax.ShapeDtypeStruct((d, N_BRANCH * d), BF16)],
        compiler_params=_cparams(("arbitrary",)),
        name="prep_w_in",
    )(w_in)


def _cast_kernel(w_ref, o_ref):
    o_ref[...] = w_ref[...].astype(BF16)


def cast_layer(w, layer, *, tr=256):
    _, r, c = w.shape
    if r % tr:
        tr = r
    return pl.pallas_call(
        _cast_kernel,
        grid=(r // tr,),
        in_specs=[pl.BlockSpec((None, tr, c), lambda i: (layer, i, 0))],
        out_specs=pl.BlockSpec((tr, c), lambda i: (i, 0)),
        out_shape=jax.ShapeDtypeStruct((r, c), BF16),
        compiler_params=_cparams(("arbitrary",)),
        name="cast_bf16",
    )(w)


def cast_branch(w_branch, layer):
    _, nbr, r, c = w_branch.shape
    return pl.pallas_call(
        _cast_kernel,
        grid=(nbr,),
        in_specs=[pl.BlockSpec((None, None, r, c), lambda i: (layer, i, 0, 0))],
        out_specs=pl.BlockSpec((None, r, c), lambda i: (i, 0, 0)),
        out_shape=jax.ShapeDtypeStruct((nbr, r, c), BF16),
        compiler_params=_cparams(("arbitrary",)),
        name="cast_branch",
    )(w_branch)


def kernel(x, norm_mix, w_in, dsa_cq_norm, dsa_w_uq, dsa_w_qidx, fox_f_bias, gdn_conv, gdn_a_log,
           gdn_dt_bias, gdn_norm, w_branch, w_out, rel_bias, norm_ffn, ffn_w_gate, ffn_w_up,
           ffn_conv, ffn_conv_b, ffn_w_down, final_norm):
    batch, seq, d = x.shape
    depth = w_in.shape[0]
    xf = x.reshape(batch * seq, d)
    ret_tables = _retention_tables(seq)
    for l in range(depth):
        w_main, w_gate = prep_w_in(w_in, l)
        proj, h = norm_proj(xf, norm_mix[l], w_main)
        par = jnp.zeros((8, LANE), F32)
        par = par.at[0, L_FF:L_FF + N_HEADS].set(fox_f_bias[l])
        par = par.at[0, L_GA:L_GA + N_HEADS].set(gdn_dt_bias[l])
        par = par.at[1, L_GA:L_GA + N_HEADS].set(gdn_a_log[l])
        prep_tok, prep_tr = prep_small(proj, par, batch, seq)
        o_ret = retention(proj, ret_tables, batch, seq)
        q_dsa, q_idx = dsa_proj(proj, dsa_cq_norm[l], cast_layer(dsa_w_uq, l), cast_layer(dsa_w_qidx, l))
        o_dsa = dsa_attention(proj, q_dsa, q_idx, prep_tok, rel_bias, batch, seq)
        o_fox = fox_attention(proj, prep_tok, prep_tr, batch, seq)
        o_gdn = gated_deltanet(proj, prep_tok, prep_tr, gdn_conv[l], gdn_norm[l], batch, seq)
        merged = merge_branches(h, (o_ret, o_dsa, o_fox, o_gdn), w_gate, cast_branch(w_branch, l))
        xf = resid_matmul(merged, cast_layer(w_out, l), xf, name="out_proj")
        act = conv_ffn_up(xf, norm_ffn[l], cast_layer(ffn_w_gate, l), cast_layer(ffn_w_up, l),
                          ffn_conv[l], ffn_conv_b[l], seq)
        xf = resid_matmul(act, cast_layer(ffn_w_down, l), xf, name="ffn_down")
    return rmsnorm(xf, final_norm).reshape(batch, seq, d)
```

```python
import functools
import math

import jax
import jax.numpy as jnp
from jax import lax
from jax.experimental import pallas as pl
from jax.experimental.pallas import tpu as pltpu

F32 = jnp.float32
BF16 = jnp.bfloat16
I32 = jnp.int32

HEAD_DIM = 128
N_HEADS = 4
WIDTH = N_HEADS * HEAD_DIM
N_BRANCH = 4
RET_CHUNK = 128
ROPE_BASE = 10000.0
DSA_Q_RANK = 384
IDX_HEADS = 16
IDX_DIM = 64
TOPK_MAX = 256
GDN_CONV = 4
GDN_CHUNK = 64
REL_BUCKETS = 32
REL_MAX_DIST = 128
FFN_CONV = 3
EPS = 1e-6

LANE = 128
VMEM_LIMIT = 56 * 1024 * 1024

C_RQ, C_RK, C_RV, C_RG = 0, 512, 1024, 1536
C_FQ, C_FK, C_FV = 2048, 2560, 3072
C_GQ, C_GK, C_GV, C_GZ = 3584, 4096, 4608, 5120
C_DK, C_DCQ, C_DV, C_KA, C_KB, C_SM = 5632, 5760, 6144, 6272, 6400, 6528
C_TOT = 6656
L_IW, L_FF, L_GB, L_GA = 0, 16, 20, 24

LOG2E = 1.4426950408889634
INT_MIN = -(2 ** 31)
INT_MAX = 2 ** 31 - 1
HIGHEST = lax.Precision.HIGHEST


def _cparams(sem, vmem=VMEM_LIMIT):
    return pltpu.CompilerParams(dimension_semantics=sem, vmem_limit_bytes=vmem)


def _dot(a, b):
    return jnp.dot(a, b, preferred_element_type=F32)


def _dot_nt(a, b):
    return lax.dot_general(a, b, (((1,), (1,)), ((), ())), preferred_element_type=F32)


def _silu(x):
    return x * jax.nn.sigmoid(x)


NORM_ROWS = 32


def _rmsnorm_rows(x_ref, g_ref, rows):
    x = x_ref[rows, :]
    ms = jnp.mean(x * x, axis=-1, keepdims=True)
    return x * lax.rsqrt(ms + EPS) * g_ref[...]


def _norm_proj_kernel(x_ref, g_ref, w_ref, o_ref, h_ref, h_scr, *, tm):
    @pl.when(pl.program_id(1) == 0)
    def _():
        def body(r, carry):
            rows = pl.ds(pl.multiple_of(r * NORM_ROWS, NORM_ROWS), NORM_ROWS)
            hb = _rmsnorm_rows(x_ref, g_ref, rows).astype(BF16)
            h_scr[rows, :] = hb
            h_ref[rows, :] = hb
            return carry
        lax.fori_loop(0, tm // NORM_ROWS, body, 0)

    o_ref[...] = _dot(h_scr[...], w_ref[...])


def norm_proj(x, gain, w, *, tm=1024, tn=512):
    n, d = x.shape
    c = w.shape[1]
    return pl.pallas_call(
        functools.partial(_norm_proj_kernel, tm=tm),
        grid=(n // tm, c // tn),
        in_specs=[pl.BlockSpec((tm, d), lambda i, j: (i, 0)),
                  pl.BlockSpec((1, d), lambda i, j: (0, 0)),
                  pl.BlockSpec((d, tn), lambda i, j: (0, j))],
        out_specs=[pl.BlockSpec((tm, tn), lambda i, j: (i, j)),
                   pl.BlockSpec((tm, d), lambda i, j: (i, 0))],
        out_shape=[jax.ShapeDtypeStruct((n, c), F32),
                   jax.ShapeDtypeStruct((n, d), BF16)],
        scratch_shapes=[pltpu.VMEM((tm, d), BF16)],
        compiler_params=_cparams(("arbitrary", "arbitrary")),
        name="norm_proj",
    )(x, gain.reshape(1, d), w)


def _rmsnorm_kernel(x_ref, g_ref, o_ref, *, tm):
    def body(r, carry):
        rows = pl.ds(pl.multiple_of(r * NORM_ROWS, NORM_ROWS), NORM_ROWS)
        o_ref[rows, :] = _rmsnorm_rows(x_ref, g_ref, rows)
        return carry
    lax.fori_loop(0, tm // NORM_ROWS, body, 0)


def rmsnorm(x, gain, *, tm=512):
    n, d = x.shape
    return pl.pallas_call(
        functools.partial(_rmsnorm_kernel, tm=tm),
        grid=(n // tm,),
        in_specs=[pl.BlockSpec((tm, d), lambda i: (i, 0)),
                  pl.BlockSpec((1, d), lambda i: (0, 0))],
        out_specs=pl.BlockSpec((tm, d), lambda i: (i, 0)),
        out_shape=jax.ShapeDtypeStruct((n, d), F32),
        compiler_params=_cparams(("arbitrary",)),
        name="final_rmsnorm",
    )(x, gain.reshape(1, d))


def _prep_kernel(s_ref, par_ref, tok_ref, tr_ref, carry_scr):
    @pl.when(pl.program_id(1) == 0)
    def _():
        carry_scr[...] = jnp.zeros_like(carry_scr)

    s = s_ref[...]
    lane = lax.broadcasted_iota(I32, (LANE, LANE), 1)
    row = lax.broadcasted_iota(I32, (LANE, LANE), 0)
    z = s + par_ref[0:1, :]
    soft = jnp.maximum(z, 0.0) + jnp.log1p(jnp.exp(-jnp.abs(z)))
    log_sig = z - soft
    sig = jax.nn.sigmoid(z)
    g_val = -jnp.exp(par_ref[1:2, :]) * soft
    is_f = (lane >= L_FF) & (lane < L_FF + N_HEADS)
    is_b = (lane >= L_GB) & (lane < L_GB + N_HEADS)
    is_a = (lane >= L_GA) & (lane < L_GA + N_HEADS)
    pre = jnp.where(is_f, log_sig, jnp.where(is_a, g_val, 0.0))
    tri = (row >= lane).astype(F32)
    tri_blk = ((row >= lane) & ((row // GDN_CHUNK) == (lane // GDN_CHUNK))).astype(F32)
    cum_full = jnp.dot(tri, pre, precision=HIGHEST, preferred_element_type=F32)
    cum_blk = jnp.dot(tri_blk, pre, precision=HIGHEST, preferred_element_type=F32)
    c_fox = cum_full + carry_scr[0:1, :]
    carry_scr[0:1, :] = c_fox[LANE - 1:LANE, :]
    scale_iw = IDX_HEADS ** -0.5 * IDX_DIM ** -0.5
    out = jnp.where(is_f, c_fox,
                    jnp.where(is_a, cum_blk,
                              jnp.where(is_b, sig,
                                        jnp.where(lane < IDX_HEADS, s * scale_iw, 0.0))))
    tok_ref[...] = out
    tr_ref[0] = out.T[0:32, :]


def prep_small(proj, par, batch, seq):
    n = proj.shape[0]
    nc = seq // LANE
    return pl.pallas_call(
        _prep_kernel,
        grid=(batch, nc),
        in_specs=[pl.BlockSpec((LANE, LANE), lambda b, c: (b * nc + c, C_SM // LANE)),
                  pl.BlockSpec((8, LANE), lambda b, c: (0, 0))],
        out_specs=[pl.BlockSpec((LANE, LANE), lambda b, c: (b * nc + c, 0)),
                   pl.BlockSpec((1, 32, LANE), lambda b, c: (b, 0, c))],
        out_shape=[jax.ShapeDtypeStruct((n, LANE), F32),
                   jax.ShapeDtypeStruct((batch, 32, seq), F32)],
        scratch_shapes=[pltpu.VMEM((8, LANE), F32)],
        compiler_params=_cparams(("arbitrary", "arbitrary")),
        name="prep_small",
    )(proj, par)


def _ret_gamma():
    return [math.log1p(-(2.0 ** (-5.0 - h))) for h in range(N_HEADS)]


def _retention_kernel(q_ref, k_ref, v_ref, g_ref, cos_ref, sin_ref, dec_ref, zeta_ref, xi_ref,
                      o_ref, state_scr):
    @pl.when(pl.program_id(1) == 0)
    def _():
        state_scr[...] = jnp.zeros_like(state_scr)

    cos_t = cos_ref[...]
    sin_t = sin_ref[...]
    log_gamma = _ret_gamma()
    for h in range(N_HEADS):
        sl = slice(h * HEAD_DIM, (h + 1) * HEAD_DIM)
        q = q_ref[:, sl]
        k = k_ref[:, sl]
        qr = q * cos_t + pltpu.roll(q, HEAD_DIM // 2, 1) * sin_t
        kr = (k * cos_t + pltpu.roll(k, HEAD_DIM // 2, 1) * sin_t) * (HEAD_DIM ** -0.5)
        qb = qr.astype(BF16)
        kb = kr.astype(BF16)
        vb = v_ref[:, sl].astype(BF16)
        inner = _dot_nt(qb, kb) * dec_ref[h]
        st = state_scr[h]
        o = _dot(inner.astype(BF16), vb) + _dot(qb, st.astype(BF16)) * xi_ref[h]
        kz_t = (kr * zeta_ref[h]).T.astype(BF16)
        state_scr[h] = st * math.exp(log_gamma[h] * RET_CHUNK) + _dot(kz_t, vb)
        mu = jnp.mean(o, axis=-1, keepdims=True)
        oc = o - mu
        var = jnp.mean(oc * oc, axis=-1, keepdims=True)
        o_ref[:, sl] = (_silu(g_ref[:, sl]) * (oc * lax.rsqrt(var + EPS))).astype(BF16)


def _retention_tables(seq):
    half = HEAD_DIM // 2
    inv = 1.0 / (ROPE_BASE ** (jnp.arange(half, dtype=F32) / half))
    ang = jnp.arange(seq).astype(F32)[:, None] * inv[None, :]
    cos, sin = jnp.cos(ang), jnp.sin(ang)
    cos_t = jnp.concatenate([cos, cos], axis=-1)
    sin_t = jnp.concatenate([-sin, sin], axis=-1)
    c = RET_CHUNK
    log_gamma = jnp.log1p(-jnp.exp2(-5.0 - jnp.arange(N_HEADS, dtype=F32)))
    n = jnp.arange(c, dtype=F32)
    diff = n[:, None] - n[None, :]
    decay = jnp.where(diff >= 0, jnp.exp(log_gamma[:, None, None] * jnp.maximum(diff, 0.0)), 0.0)
    zeta = jnp.exp(log_gamma[:, None] * (c - 1 - n)[None, :])
    xi = jnp.exp(log_gamma[:, None] * (n + 1)[None, :])
    ones = jnp.ones((1, 1, HEAD_DIM), F32)
    return cos_t, sin_t, decay, zeta[:, :, None] * ones, xi[:, :, None] * ones


def retention(proj, tables, batch, seq):
    n = proj.shape[0]
    c = RET_CHUNK
    nc = seq // c
    cos_t, sin_t, decay, zeta, xi = tables
    col = lambda off: pl.BlockSpec((c, WIDTH), lambda b, i: (b * nc + i, off // WIDTH))
    full3 = pl.BlockSpec((N_HEADS, c, HEAD_DIM), lambda b, i: (0, 0, 0))
    return pl.pallas_call(
        _retention_kernel,
        grid=(batch, nc),
        in_specs=[col(C_RQ), col(C_RK), col(C_RV), col(C_RG),
                  pl.BlockSpec((c, HEAD_DIM), lambda b, i: (i, 0)),
                  pl.BlockSpec((c, HEAD_DIM), lambda b, i: (i, 0)),
                  full3, full3, full3],
        out_specs=pl.BlockSpec((c, WIDTH), lambda b, i: (b * nc + i, 0)),
        out_shape=jax.ShapeDtypeStruct((n, WIDTH), BF16),
        scratch_shapes=[pltpu.VMEM((N_HEADS, HEAD_DIM, HEAD_DIM), F32)],
        compiler_params=_cparams(("arbitrary", "arbitrary")),
        name="retention",
    )(proj, proj, proj, proj, cos_t, sin_t, decay, zeta, xi)


def _fox_kernel(q_ref, k_ref, v_ref, ctr_ref, o_ref, m_scr, acc_scr, *, t):
    qi = pl.program_id(1)
    ki = pl.program_id(2)

    @pl.when(ki == 0)
    def _():
        m_scr[...] = jnp.full_like(m_scr, -jnp.inf)
        acc_scr[...] = jnp.zeros_like(acc_scr)

    def step(masked):
        if masked:
            row = lax.broadcasted_iota(I32, (t, t), 0)
            colm = lax.broadcasted_iota(I32, (t, t), 1)
            keep = row >= colm
        ones = jnp.ones((t, HEAD_DIM), BF16)
        for h in range(N_HEADS):
            sl = slice(h * HEAD_DIM, (h + 1) * HEAD_DIM)
            qb = q_ref[:, sl].astype(BF16)
            kb = k_ref[:, sl].astype(BF16)
            c_k = ctr_ref[0, L_FF + h:L_FF + h + 1, :] * LOG2E
            s = _dot_nt(qb, kb) * (HEAD_DIM ** -0.5 * LOG2E) - c_k
            if masked:
                s = jnp.where(keep, s, -jnp.inf)
            m_old = m_scr[h]
            m_new = jnp.maximum(m_old, jnp.max(s, axis=-1, keepdims=True))
            alpha = jnp.exp2(m_old - m_new)
            p = jnp.exp2(s - m_new)
            v_aug = jnp.concatenate([v_ref[:, sl].astype(BF16), ones], axis=1)
            acc_scr[h] = alpha * acc_scr[h] + _dot(p.astype(BF16), v_aug)
            m_scr[h] = m_new

    @pl.when(ki < qi)
    def _():
        step(False)

    @pl.when(ki == qi)
    def _():
        step(True)
        for h in range(N_HEADS):
            sl = slice(h * HEAD_DIM, (h + 1) * HEAD_DIM)
            acc = acc_scr[h]
            o_ref[:, sl] = (acc[:, :HEAD_DIM] / acc[:, HEAD_DIM:]).astype(BF16)


def fox_attention(proj, prep_tr, batch, seq, *, t=512):
    n = proj.shape[0]
    nt = seq // t
    qspec = pl.BlockSpec((t, WIDTH), lambda b, qi, ki: (b * nt + qi, C_FQ // WIDTH))
    kspec = lambda off: pl.BlockSpec(
        (t, WIDTH), lambda b, qi, ki: (b * nt + jnp.minimum(ki, qi), off // WIDTH))
    return pl.pallas_call(
        functools.partial(_fox_kernel, t=t),
        grid=(batch, nt, nt),
        in_specs=[qspec, kspec(C_FK), kspec(C_FV),
                  pl.BlockSpec((1, 32, t), lambda b, qi, ki: (b, 0, jnp.minimum(ki, qi)))],
        out_specs=pl.BlockSpec((t, WIDTH), lambda b, qi, ki: (b * nt + qi, 0)),
        out_shape=jax.ShapeDtypeStruct((n, WIDTH), BF16),
        scratch_shapes=[pltpu.VMEM((N_HEADS, t, 1), F32),
                        pltpu.VMEM((N_HEADS, t, 2 * HEAD_DIM), F32)],
        compiler_params=_cparams(("arbitrary", "arbitrary", "arbitrary")),
        name="fox_attention",
    )(proj, proj, proj, prep_tr)


def _dsa_proj_kernel(cq_ref, g_ref, wq_ref, wi_ref, q_ref, qi_ref):
    x = cq_ref[...]
    ms = jnp.mean(x * x, axis=-1, keepdims=True)
    cb = (x * lax.rsqrt(ms + EPS) * g_ref[...]).astype(BF16)
    q_ref[...] = _dot(cb, wq_ref[...]).astype(BF16)
    qi_ref[...] = _dot(cb, wi_ref[...]).astype(BF16)


def dsa_proj(proj, cq_norm, w_uq, w_qidx, *, tm=512):
    n = proj.shape[0]
    r = DSA_Q_RANK
    wi = IDX_HEADS * IDX_DIM
    return pl.pallas_call(
        _dsa_proj_kernel,
        grid=(n // tm,),
        in_specs=[pl.BlockSpec((tm, r), lambda i: (i, C_DCQ // r)),
                  pl.BlockSpec((1, r), lambda i: (0, 0)),
                  pl.BlockSpec((r, WIDTH), lambda i: (0, 0)),
                  pl.BlockSpec((r, wi), lambda i: (0, 0))],
        out_specs=[pl.BlockSpec((tm, WIDTH), lambda i: (i, 0)),
                   pl.BlockSpec((tm, wi), lambda i: (i, 0))],
        out_shape=[jax.ShapeDtypeStruct((n, WIDTH), BF16),
                   jax.ShapeDtypeStruct((n, wi), BF16)],
        compiler_params=_cparams(("arbitrary",)),
        name="dsa_proj",
    )(proj, cq_norm.reshape(1, r), w_uq, w_qidx)


DSA_QB = 256
DSA_KC = 512
DSA_SCORE_MID_STEPS = 20
DSA_FEW_KEYS = 4


def _t5_bucket(rel):
    max_exact = REL_BUCKETS // 2
    relf = jnp.maximum(rel, max_exact).astype(F32)
    large = max_exact + (jnp.log(relf / max_exact) / math.log(REL_MAX_DIST / max_exact)
                         * (REL_BUCKETS - max_exact)).astype(I32)
    large = jnp.minimum(large, REL_BUCKETS - 1)
    return jnp.where(rel < max_exact, rel, large)


def _dsa_kernel(rb_ref, q_ref, qi_ref, tok_ref, k_ref, v_ref, ka_ref, kb_ref, o_ref,
                key_scr, lg_scr, band_scr, kb16_scr, vb16_scr, ka16_scr, kb16i_scr, *, seq, topk):
    qb_idx = pl.program_id(1)
    t0 = qb_idx * DSA_QB
    n_kc = (t0 + DSA_QB - 1) // DSA_KC + 1

    @pl.when(qb_idx == 0)
    def _():
        kb16_scr[...] = k_ref[...].astype(BF16)
        vb16_scr[...] = v_ref[...].astype(BF16)
        ka16_scr[...] = ka_ref[...].astype(BF16)
        kb16i_scr[...] = kb_ref[...].astype(BF16)

    @pl.when((pl.program_id(0) == 0) & (qb_idx == 0))
    def _():
        i_ = lax.broadcasted_iota(I32, (DSA_QB, 2 * DSA_QB), 0)
        j_ = lax.broadcasted_iota(I32, (DSA_QB, 2 * DSA_QB), 1)
        rel = i_ + DSA_QB - j_
        bucket = _t5_bucket(rel)
        for h in range(N_HEADS):
            far = rb_ref[REL_BUCKETS - 1, h]
            band = jnp.zeros((DSA_QB, 2 * DSA_QB), F32)
            for bk in range(REL_BUCKETS - 1):
                band = jnp.where(bucket == bk, rb_ref[bk, h] - far, band)
            band_scr[h] = jnp.where(rel >= 0, band, 0.0)

    w_all = tok_ref[...]
    row_t = t0 + lax.broadcasted_iota(I32, (DSA_QB, DSA_KC), 0)

    def score_chunk(c, carry):
        kmax, kmin = carry
        ks = pl.ds(pl.multiple_of(c * DSA_KC, DSA_KC), DSA_KC)
        ka = ka16_scr[ks, :]
        kb = kb16i_scr[ks, :]
        acc = jnp.zeros((DSA_QB, DSA_KC), F32)
        for p in range(IDX_HEADS // 2):
            qp = qi_ref[:, p * LANE:(p + 1) * LANE]
            acc = acc + jnp.maximum(_dot_nt(qp, ka), 0.0) * w_all[:, 2 * p:2 * p + 1]
            acc = acc + jnp.maximum(_dot_nt(qp, kb), 0.0) * w_all[:, 2 * p + 1:2 * p + 2]
        bits = pltpu.bitcast(acc, I32)
        key = bits ^ ((bits >> 31) & 0x7FFFFFFF)
        col_s = c * DSA_KC + lax.broadcasted_iota(I32, (DSA_QB, DSA_KC), 1)
        valid = col_s <= row_t
        key_scr[:, ks] = jnp.where(valid, key, INT_MIN)
        kmax = jnp.maximum(kmax, jnp.max(jnp.where(valid, key, INT_MIN), axis=-1, keepdims=True))
        kmin = jnp.minimum(kmin, jnp.min(jnp.where(valid, key, INT_MAX), axis=-1, keepdims=True))
        return kmax, kmin

    kmax, kmin = lax.fori_loop(0, n_kc, score_chunk, (jnp.full((DSA_QB, 1), INT_MIN, I32),
                                                     jnp.full((DSA_QB, 1), INT_MAX, I32)))

    def scan_keys(cand, with_below):
        def body(c, carry):
            cnt, below = carry
            ks = pl.ds(pl.multiple_of(c * DSA_KC, DSA_KC), DSA_KC)
            keys = key_scr[:, ks]
            ge = keys >= cand
            ones = ge.astype(I32)
            low = jnp.where(ge, INT_MIN, keys)
            for u in range(DSA_KC // LANE):
                cnt = cnt + ones[:, u * LANE:(u + 1) * LANE]
                if with_below:
                    below = jnp.maximum(below, low[:, u * LANE:(u + 1) * LANE])
            return cnt, below
        cnt, below = lax.fori_loop(0, n_kc, body, (jnp.zeros((DSA_QB, LANE), I32),
                                                   jnp.full((DSA_QB, LANE), INT_MIN, I32)))
        cnt = jnp.sum(cnt, axis=-1, keepdims=True)
        if with_below:
            return cnt, jnp.max(below, axis=-1, keepdims=True)
        return cnt

    def key_to_score(k):
        return pltpu.bitcast(k ^ ((k >> 31) & 0x7FFFFFFF), F32)

    def score_to_key(s):
        b = pltpu.bitcast(s, I32)
        return b ^ ((b >> 31) & 0x7FFFFFFF)

    def open_rows(lo, hi, c_lo):
        return (c_lo > topk) & (hi - 1 > lo)

    def any_row(flag):
        return jnp.max(jnp.where(flag, 1, 0))

    def update(cand, cnt, lo, hi, c_lo, c_hi):
        ge = cnt >= topk
        return (jnp.where(ge, cand, lo), jnp.where(ge, hi, cand),
                jnp.where(ge, cnt, c_lo), jnp.where(ge, c_hi, cnt))

    def crowded(lo, hi, c_lo, c_hi):
        return any_row(open_rows(lo, hi, c_lo) & (c_lo - c_hi > DSA_FEW_KEYS))

    def halve_body(st):
        it, _, lo, hi, c_lo, c_hi = st
        key_mid = (lo >> 1) + (hi >> 1) + (lo & hi & 1)
        score_mid = score_to_key(0.5 * key_to_score(lo) + 0.5 * key_to_score(hi - 1))
        cand = jnp.where(it < DSA_SCORE_MID_STEPS, score_mid, key_mid)
        cand = jnp.minimum(jnp.maximum(cand, lo + 1), hi - 1)
        cand = jnp.where(hi - 1 > lo, cand, lo)
        lo, hi, c_lo, c_hi = update(cand, scan_keys(cand, False), lo, hi, c_lo, c_hi)
        return it + 1, crowded(lo, hi, c_lo, c_hi), lo, hi, c_lo, c_hi

    n_valid = jnp.minimum(t0 + lax.broadcasted_iota(I32, (DSA_QB, 1), 0) + 1, seq)
    lo0 = kmin
    hi0 = kmax + 1
    c_hi0 = jnp.zeros((DSA_QB, 1), I32)
    _, _, lo, hi, c_lo, c_hi = lax.while_loop(
        lambda st: st[1] > 0, halve_body,
        (jnp.int32(0), crowded(lo0, hi0, n_valid, c_hi0), lo0, hi0, n_valid, c_hi0))

    def walk_body(st):
        _, lo, hi, c_lo, c_hi, nxt = st
        is_open = open_rows(lo, hi, c_lo)
        cand = jnp.where(is_open, nxt, lo)
        cnt, below = scan_keys(cand, True)
        ge = cnt >= topk
        hi = jnp.where(is_open, jnp.where(ge, cand + 1, cand), hi)
        c_hi = jnp.where(is_open & jnp.logical_not(ge), cnt, c_hi)
        lo = jnp.where(is_open & ge, cand, lo)
        c_lo = jnp.where(is_open & ge, cnt, c_lo)
        nxt = jnp.where(ge, nxt, below)
        return any_row(open_rows(lo, hi, c_lo)), lo, hi, c_lo, c_hi, nxt

    _, nxt0 = scan_keys(hi, True)
    _, thr, hi, n_ge, n_gt, _ = lax.while_loop(
        lambda st: st[0] > 0, walk_body,
        (any_row(open_rows(lo, hi, c_lo)), lo, hi, c_lo, c_hi, nxt0))

    tied = n_ge > topk
    has_tie = jnp.max(jnp.where(tied, 1, 0)) > 0

    @pl.when(has_tie)
    def _():
        room = (topk - n_gt).astype(F32)
        ii = lax.broadcasted_iota(I32, (LANE, LANE), 0)
        jj = lax.broadcasted_iota(I32, (LANE, LANE), 1)
        upper = (ii <= jj).astype(BF16)

        def body(c, seen):
            ks = pl.ds(pl.multiple_of(c * LANE, LANE), LANE)
            kk = key_scr[:, ks]
            eq = kk == thr
            rank = seen + _dot(eq.astype(BF16), upper)
            drop = eq & (rank > room) & tied
            key_scr[:, ks] = jnp.where(drop, INT_MIN, kk)
            return seen + jnp.sum(eq.astype(F32), axis=-1, keepdims=True)
        lax.fori_loop(0, n_kc * (DSA_KC // LANE), body, jnp.zeros((DSA_QB, 1), F32))

    def mask_chunk(c, carry):
        ks = pl.ds(pl.multiple_of(c * DSA_KC, DSA_KC), DSA_KC)
        sel = jnp.where(key_scr[:, ks] >= thr, 0.0, -jnp.inf).astype(F32)
        key_scr[:, ks] = pltpu.bitcast(sel, I32)
        return carry
    lax.fori_loop(0, n_kc, mask_chunk, 0)

    band_off = pl.multiple_of(jnp.maximum(qb_idx - 1, 0) * DSA_QB, DSA_QB)
    for h in range(N_HEADS):
        sl = slice(h * HEAD_DIM, (h + 1) * HEAD_DIM)
        qh = q_ref[:, sl]

        def logit_chunk(c, m):
            ks = pl.ds(pl.multiple_of(c * DSA_KC, DSA_KC), DSA_KC)
            s = (_dot_nt(qh, kb16_scr[ks, :]) * (HEAD_DIM ** -0.5)
                 + pltpu.bitcast(key_scr[:, ks], F32))
            lg_scr[:, ks] = s
            return jnp.maximum(m, jnp.max(s, axis=-1, keepdims=True))
        m = lax.fori_loop(0, n_kc, logit_chunk, jnp.full((DSA_QB, 1), -jnp.inf, F32))

        band_h = band_scr[h]
        band_first = jnp.concatenate([band_h[:, DSA_QB:], jnp.zeros((DSA_QB, DSA_QB), F32)], axis=1)
        ws = pl.ds(band_off, 2 * DSA_QB)
        win = lg_scr[:, ws] + jnp.where(qb_idx == 0, band_first, band_h)
        lg_scr[:, ws] = win
        m = jnp.maximum(m, jnp.max(win, axis=-1, keepdims=True))

        def pv_chunk(c, carry):
            l, acc = carry
            ks = pl.ds(pl.multiple_of(c * DSA_KC, DSA_KC), DSA_KC)
            p = jnp.exp(lg_scr[:, ks] - m)
            l = l + jnp.sum(p, axis=-1, keepdims=True)
            acc = acc + _dot(p.astype(BF16), vb16_scr[ks, :])
            return l, acc
        l, acc = lax.fori_loop(0, n_kc, pv_chunk,
                               (jnp.zeros((DSA_QB, 1), F32), jnp.zeros((DSA_QB, HEAD_DIM), F32)))
        o_ref[:, sl] = (acc / l).astype(BF16)


def dsa_attention(proj, q, q_idx, prep_tok, rel_bias, batch, seq):
    n = proj.shape[0]
    nq = seq // DSA_QB
    topk = min(TOPK_MAX, seq // 4)
    wi = IDX_HEADS * IDX_DIM
    rowblk = lambda w, cb: pl.BlockSpec((DSA_QB, w), lambda b, i: (b * nq + i, cb))
    seqblk = lambda off: pl.BlockSpec((seq, LANE), lambda b, i: (b, off // LANE))
    return pl.pallas_call(
        functools.partial(_dsa_kernel, seq=seq, topk=topk),
        grid=(batch, nq),
        in_specs=[pl.BlockSpec(memory_space=pltpu.SMEM),
                  rowblk(WIDTH, 0), rowblk(wi, 0), rowblk(LANE, 0),
                  seqblk(C_DK), seqblk(C_DV), seqblk(C_KA), seqblk(C_KB)],
        out_specs=pl.BlockSpec((DSA_QB, WIDTH), lambda b, i: (b * nq + i, 0)),
        out_shape=jax.ShapeDtypeStruct((n, WIDTH), BF16),
        scratch_shapes=[pltpu.VMEM((DSA_QB, seq), I32),
                        pltpu.VMEM((DSA_QB, seq), F32),
                        pltpu.VMEM((N_HEADS, DSA_QB, 2 * DSA_QB), F32),
                        pltpu.VMEM((seq, LANE), BF16),
                        pltpu.VMEM((seq, LANE), BF16),
                        pltpu.VMEM((seq, LANE), BF16),
                        pltpu.VMEM((seq, LANE), BF16)],
        compiler_params=_cparams(("arbitrary", "arbitrary")),
        name="dsa_attention",
    )(rel_bias, q, q_idx, prep_tok, proj, proj, proj, proj)


GDN_T = 256
GDN_HALO = 8


def _gdn_kernel(q_ref, k_ref, v_ref, z_ref, cw_ref, ng_ref, tok_ref, tr_ref, o_ref,
                xq_scr, xk_scr, xv_scr, state_scr):
    first = pl.program_id(1) == 0

    @pl.when(first)
    def _():
        state_scr[...] = jnp.zeros_like(state_scr)
        for scr in (xq_scr, xk_scr, xv_scr):
            scr[0:GDN_HALO, :] = jnp.zeros((GDN_HALO, WIDTH), F32)

    def conv(x_ref, scr, w_off):
        scr[GDN_HALO:, :] = x_ref[...]
        y = jnp.zeros((GDN_T, WIDTH), F32)
        for i in range(GDN_CONV):
            st = GDN_HALO - (GDN_CONV - 1) + i
            y = y + scr[st:st + GDN_T, :] * cw_ref[i:i + 1, w_off:w_off + WIDTH]
        scr[0:GDN_HALO, :] = scr[GDN_T:GDN_T + GDN_HALO, :]
        return _silu(y)

    qc = conv(q_ref, xq_scr, 0)
    kc = conv(k_ref, xk_scr, WIDTH)
    vc = conv(v_ref, xv_scr, 2 * WIDTH)
    tok = tok_ref[...]
    c = GDN_CHUNK
    heads = range(N_HEADS)
    hsl = [slice(h * HEAD_DIM, (h + 1) * HEAD_DIM) for h in heads]

    def l2norm_heads(x, scale):
        return jnp.concatenate(
            [x[:, s] * (lax.rsqrt(jnp.sum(x[:, s] * x[:, s], axis=-1, keepdims=True) + EPS) * scale)
             for s in hsl], axis=1)

    qf = l2norm_heads(qc, HEAD_DIM ** -0.5)
    kf = l2norm_heads(kc, 1.0)

    nb = N_HEADS * c
    ri = lax.broadcasted_iota(I32, (nb, nb), 0)
    ci = lax.broadcasted_iota(I32, (nb, nb), 1)
    same_head = (ri // c) == (ci // c)
    tril = same_head & (ri >= ci)
    strict = same_head & (ri > ci)
    eye = (ri == ci).astype(F32)
    lane_head = lax.broadcasted_iota(I32, (c, WIDTH), 1) // HEAD_DIM
    row_head = lax.broadcasted_iota(I32, (nb, HEAD_DIM), 0) // c

    def spread(x):
        return jnp.concatenate([jnp.where(lane_head == h, x, 0.0) for h in heads], axis=0)

    def stack(x):
        return jnp.concatenate([x[:, s] for s in hsl], axis=0)

    def spread_lanes(x):
        return jnp.concatenate([jnp.where(row_head == h, x, 0.0) for h in heads], axis=1)

    def split(x):
        hi = x.astype(BF16)
        return hi, (x - hi.astype(F32)).astype(BF16)

    def dot_split(a, b):
        return _dot(a[0], b[0]) + (_dot(a[0], b[1]) + _dot(a[1], b[0]))

    outs = [[] for _ in heads]
    for j in range(GDN_T // c):
        rs = slice(j * c, (j + 1) * c)
        last = slice((j + 1) * c - 1, (j + 1) * c)
        qj, kj, vj = qf[rs], kf[rs], vc[rs]
        b_col = jnp.concatenate([tok[rs, L_GB + h:L_GB + h + 1] for h in heads], axis=0)
        g_col = jnp.concatenate([tok[rs, L_GA + h:L_GA + h + 1] for h in heads], axis=0)
        g_row = jnp.concatenate([tr_ref[0, L_GA + h:L_GA + h + 1, rs] for h in heads], axis=1)
        g_last = [tok[last, L_GA + h:L_GA + h + 1] for h in heads]
        g_last_col = jnp.concatenate([jnp.broadcast_to(g, (c, 1)) for g in g_last], axis=0)
        k_sp = spread(kj)
        q_sp = spread(qj)
        k_sp16 = k_sp.astype(BF16)
        decay = jnp.exp(jnp.where(tril, g_col - g_row, -jnp.inf))
        neg_l = -jnp.where(strict, b_col * _dot_nt(k_sp16, k_sp16) * decay, 0.0)
        t_inv = eye + neg_l
        pw_s = split(neg_l)
        for _ in range(5):
            pw_s = split(dot_split(pw_s, pw_s))
            t_inv = t_inv + dot_split(split(t_inv), pw_s)
        eg = jnp.exp(g_col)
        rhs = jnp.concatenate([stack(vj) * b_col, stack(kj) * (b_col * eg)], axis=1)
        sol = dot_split(split(t_inv), split(rhs))
        u0 = sol[:, :HEAD_DIM]
        kcum = sol[:, HEAD_DIM:]
        qk = _dot_nt(q_sp.astype(BF16), k_sp16) * decay
        q_dec = q_sp * eg
        k_dec = k_sp * jnp.exp(g_last_col - g_col)
        st = state_scr[...]
        stb = st.astype(BF16)
        v_new = u0 - _dot(spread_lanes(kcum).astype(BF16), stb)
        v_new_b = v_new.astype(BF16)
        o_st = _dot(q_dec.astype(BF16), stb) + _dot(qk.astype(BF16), v_new_b)
        e_last = jnp.concatenate([jnp.broadcast_to(jnp.exp(g), (HEAD_DIM, 1)) for g in g_last], axis=0)
        state_scr[...] = st * e_last + _dot(k_dec.T.astype(BF16), v_new_b)
        for h in heads:
            outs[h].append(o_st[h * c:(h + 1) * c, :])

    for h in heads:
        o = jnp.concatenate(outs[h], axis=0)
        ms = jnp.mean(o * o, axis=-1, keepdims=True)
        on = o * lax.rsqrt(ms + EPS) * ng_ref[...]
        o_ref[:, hsl[h]] = (on * _silu(z_ref[:, hsl[h]])).astype(BF16)


def gated_deltanet(proj, prep_tok, prep_tr, conv_w, norm_g, batch, seq):
    n = proj.shape[0]
    t = GDN_T
    nt = seq // t
    col = lambda off: pl.BlockSpec((t, WIDTH), lambda b, i: (b * nt + i, off // WIDTH))
    return pl.pallas_call(
        _gdn_kernel,
        grid=(batch, nt),
        in_specs=[col(C_GQ), col(C_GK), col(C_GV), col(C_GZ),
                  pl.BlockSpec((GDN_CONV, 3 * WIDTH), lambda b, i: (0, 0)),
                  pl.BlockSpec((1, HEAD_DIM), lambda b, i: (0, 0)),
                  pl.BlockSpec((t, LANE), lambda b, i: (b * nt + i, 0)),
                  pl.BlockSpec((1, 32, t), lambda b, i: (b, 0, i))],
        out_specs=pl.BlockSpec((t, WIDTH), lambda b, i: (b * nt + i, 0)),
        out_shape=jax.ShapeDtypeStruct((n, WIDTH), BF16),
        scratch_shapes=[pltpu.VMEM((t + GDN_HALO, WIDTH), F32),
                        pltpu.VMEM((t + GDN_HALO, WIDTH), F32),
                        pltpu.VMEM((t + GDN_HALO, WIDTH), F32),
                        pltpu.VMEM((N_HEADS * HEAD_DIM, HEAD_DIM), F32)],
        compiler_params=_cparams(("arbitrary", "arbitrary")),
        name="gated_deltanet",
    )(proj, proj, proj, proj, conv_w, norm_g.reshape(1, HEAD_DIM), prep_tok, prep_tr)


def _merge_kernel(h_ref, b0_ref, b1_ref, b2_ref, b3_ref, g0_ref, g1_ref, g2_ref, g3_ref,
                  wb_ref, o_ref):
    h = h_ref[...]
    acc = None
    for n, (b_ref, g_ref) in enumerate(zip((b0_ref, b1_ref, b2_ref, b3_ref),
                                           (g0_ref, g1_ref, g2_ref, g3_ref))):
        gate = jax.nn.sigmoid(_dot(h, g_ref[...]))
        term = gate * _dot(b_ref[...], wb_ref[n])
        acc = term if acc is None else acc + term
    o_ref[...] = acc.astype(BF16)


def merge_branches(h, branches, w_gate, w_branch, *, tm=512, tn=512):
    n, d = h.shape
    nj = d // tn
    bspec = pl.BlockSpec((tm, WIDTH), lambda j, i: (i, 0))
    gspec = lambda k: pl.BlockSpec((d, tn), lambda j, i: (0, k * nj + j))
    return pl.pallas_call(
        _merge_kernel,
        grid=(nj, n // tm),
        in_specs=[pl.BlockSpec((tm, d), lambda j, i: (i, 0)),
                  bspec, bspec, bspec, bspec,
                  gspec(0), gspec(1), gspec(2), gspec(3),
                  pl.BlockSpec((N_BRANCH, WIDTH, tn), lambda j, i: (0, 0, j))],
        out_specs=pl.BlockSpec((tm, tn), lambda j, i: (i, j)),
        out_shape=jax.ShapeDtypeStruct((n, d), BF16),
        compiler_params=_cparams(("arbitrary", "arbitrary")),
        name="merge_branches",
    )(h, *branches, w_gate, w_gate, w_gate, w_gate, w_branch)


def _resid_mm_kernel(a_ref, w_ref, x_ref, o_ref):
    o_ref[...] = x_ref[...] + _dot(a_ref[...], w_ref[...])


def resid_matmul(a, w, x, *, tm=512, tn=1024, name="resid_matmul"):
    n, k = a.shape
    d = w.shape[1]
    return pl.pallas_call(
        _resid_mm_kernel,
        grid=(d // tn, n // tm),
        in_specs=[pl.BlockSpec((tm, k), lambda j, i: (i, 0)),
                  pl.BlockSpec((k, tn), lambda j, i: (0, j)),
                  pl.BlockSpec((tm, tn), lambda j, i: (i, j))],
        out_specs=pl.BlockSpec((tm, tn), lambda j, i: (i, j)),
        out_shape=jax.ShapeDtypeStruct((n, d), F32),
        compiler_params=_cparams(("arbitrary", "arbitrary")),
        name=name,
    )(a, w, x)


FFN_HALO = 8
FFN_SUB = 512


def _ffn1_kernel(x_ref, g_ref, wg_ref, wu_ref, cw_ref, cb_ref, o_ref, h_scr, gt_scr, halo_scr,
                 *, tm, tiles_per_seq):
    i = pl.program_id(0)
    j = pl.program_id(1)

    @pl.when((i == 0) & (j == 0))
    def _():
        halo_scr[...] = jnp.zeros_like(halo_scr)

    @pl.when(j == 0)
    def _():
        def body(r, carry):
            rows = pl.ds(pl.multiple_of(r * NORM_ROWS, NORM_ROWS), NORM_ROWS)
            h_scr[rows, :] = _rmsnorm_rows(x_ref, g_ref, rows).astype(BF16)
            return carry
        lax.fori_loop(0, tm // NORM_ROWS, body, 0)

    h = h_scr[...]
    seq_start = (i % tiles_per_seq) == 0
    tn = o_ref.shape[1]
    for off in range(0, tn, FFN_SUB):
        cs = slice(off, min(off + FFN_SUB, tn))
        g = _dot(h, wg_ref[:, cs])
        gt_scr[FFN_HALO:, cs] = g
        gt_scr[0:FFN_HALO, cs] = jnp.where(seq_start, 0.0, halo_scr[j, :, cs])
        halo_scr[j, :, cs] = g[tm - FFN_HALO:, :]
        y = cb_ref[:, cs] + g * cw_ref[FFN_CONV - 1:FFN_CONV, cs]
        for t in range(FFN_CONV - 1):
            st = FFN_HALO - (FFN_CONV - 1) + t
            y = y + gt_scr[st:st + tm, cs] * cw_ref[t:t + 1, cs]
        o_ref[:, cs] = (_silu(y) * _dot(h, wu_ref[:, cs])).astype(BF16)


def conv_ffn_up(x, gain, w_gate, w_up, conv_w, conv_b, seq, *, tm=512, tn=1408):
    n, d = x.shape
    f = w_gate.shape[1]
    nj = f // tn
    return pl.pallas_call(
        functools.partial(_ffn1_kernel, tm=tm, tiles_per_seq=seq // tm),
        grid=(n // tm, nj),
        in_specs=[pl.BlockSpec((tm, d), lambda i, j: (i, 0)),
                  pl.BlockSpec((1, d), lambda i, j: (0, 0)),
                  pl.BlockSpec((d, tn), lambda i, j: (0, j)),
                  pl.BlockSpec((d, tn), lambda i, j: (0, j)),
                  pl.BlockSpec((FFN_CONV, tn), lambda i, j: (0, j)),
                  pl.BlockSpec((1, tn), lambda i, j: (0, j))],
        out_specs=pl.BlockSpec((tm, tn), lambda i, j: (i, j)),
        out_shape=jax.ShapeDtypeStruct((n, f), BF16),
        scratch_shapes=[pltpu.VMEM((tm, d), BF16),
                        pltpu.VMEM((tm + FFN_HALO, tn), F32),
                        pltpu.VMEM((nj, FFN_HALO, tn), F32)],
        compiler_params=_cparams(("arbitrary", "arbitrary")),
        name="conv_ffn_up",
    )(x, gain.reshape(1, d), w_gate, w_up, conv_w, conv_b.reshape(1, f))


IN_SIZES = (WIDTH, WIDTH, WIDTH, WIDTH,
            DSA_Q_RANK, HEAD_DIM, HEAD_DIM, IDX_DIM, IDX_HEADS,
            WIDTH, WIDTH, WIDTH, N_HEADS,
            WIDTH, WIDTH, WIDTH, WIDTH, N_HEADS, N_HEADS)
IN_NAMES = ("r_q", "r_k", "r_v", "r_g", "d_cq", "d_k", "d_v", "i_k", "i_w",
            "f_q", "f_k", "f_v", "f_f", "g_q", "g_k", "g_v", "g_z", "g_b", "g_a")
IN_PLAN = (("r_q", C_RQ), ("r_k", C_RK), ("r_v", C_RV), ("r_g", C_RG),
           ("f_q", C_FQ), ("f_k", C_FK), ("f_v", C_FV),
           ("g_q", C_GQ), ("g_k", C_GK), ("g_v", C_GV), ("g_z", C_GZ),
           ("d_k", C_DK), ("d_cq", C_DCQ), ("d_v", C_DV),
           ("i_k", C_KA), ("i_k", C_KB + IDX_DIM),
           ("i_w", C_SM + L_IW), ("f_f", C_SM + L_FF), ("g_b", C_SM + L_GB), ("g_a", C_SM + L_GA))


def _prep_w_in_kernel(w_ref, m_ref, g_ref):
    src = {}
    off = 0
    for name, size in zip(IN_NAMES, IN_SIZES):
        src[name] = (off, size)
        off += size
    m_ref[...] = jnp.zeros_like(m_ref)
    for name, dst in IN_PLAN:
        so, w = src[name]
        m_ref[:, dst:dst + w] = w_ref[:, so:so + w].astype(BF16)
    g_ref[...] = w_ref[:, off:off + g_ref.shape[1]].astype(BF16)


def prep_w_in(w_in, layer, *, tr=128):
    _, d, c = w_in.shape
    return pl.pallas_call(
        _prep_w_in_kernel,
        grid=(d // tr,),
        in_specs=[pl.BlockSpec((None, tr, c), lambda i: (layer, i, 0))],
        out_specs=[pl.BlockSpec((tr, C_TOT), lambda i: (i, 0)),
                   pl.BlockSpec((tr, N_BRANCH * d), lambda i: (i, 0))],
        out_shape=[jax.ShapeDtypeStruct((d, C_TOT), BF16),
                   jax.ShapeDtypeStruct((d, N_BRANCH * d), BF16)],
        compiler_params=_cparams(("arbitrary",)),
        name="prep_w_in",
    )(w_in)


def _cast_kernel(w_ref, o_ref):
    o_ref[...] = w_ref[...].astype(BF16)


def cast_layer(w, layer, *, tr=256):
    _, r, c = w.shape
    if r % tr:
        tr = r
    return pl.pallas_call(
        _cast_kernel,
        grid=(r // tr,),
        in_specs=[pl.BlockSpec((None, tr, c), lambda i: (layer, i, 0))],
        out_specs=pl.BlockSpec((tr, c), lambda i: (i, 0)),
        out_shape=jax.ShapeDtypeStruct((r, c), BF16),
        compiler_params=_cparams(("arbitrary",)),
        name="cast_bf16",
    )(w)


def cast_branch(w_branch, layer):
    _, nbr, r, c = w_branch.shape
    return pl.pallas_call(
        _cast_kernel,
        grid=(nbr,),
        in_specs=[pl.BlockSpec((None, None, r, c), lambda i: (layer, i, 0, 0))],
        out_specs=pl.BlockSpec((None, r, c), lambda i: (i, 0, 0)),
        out_shape=jax.ShapeDtypeStruct((nbr, r, c), BF16),
        compiler_params=_cparams(("arbitrary",)),
        name="cast_branch",
    )(w_branch)


def kernel(x, norm_mix, w_in, dsa_cq_norm, dsa_w_uq, dsa_w_qidx, fox_f_bias, gdn_conv, gdn_a_log,
           gdn_dt_bias, gdn_norm, w_branch, w_out, rel_bias, norm_ffn, ffn_w_gate, ffn_w_up,
           ffn_conv, ffn_conv_b, ffn_w_down, final_norm):
    batch, seq, d = x.shape
    depth = w_in.shape[0]
    xf = x.reshape(batch * seq, d)
    ret_tables = _retention_tables(seq)
    for l in range(depth):
        w_main, w_gate = prep_w_in(w_in, l)
        proj, h = norm_proj(xf, norm_mix[l], w_main)
        par = jnp.zeros((8, LANE), F32)
        par = par.at[0, L_FF:L_FF + N_HEADS].set(fox_f_bias[l])
        par = par.at[0, L_GA:L_GA + N_HEADS].set(gdn_dt_bias[l])
        par = par.at[1, L_GA:L_GA + N_HEADS].set(gdn_a_log[l])
        prep_tok, prep_tr = prep_small(proj, par, batch, seq)
        o_ret = retention(proj, ret_tables, batch, seq)
        q_dsa, q_idx = dsa_proj(proj, dsa_cq_norm[l], cast_layer(dsa_w_uq, l), cast_layer(dsa_w_qidx, l))
        o_dsa = dsa_attention(proj, q_dsa, q_idx, prep_tok, rel_bias, batch, seq)
        o_fox = fox_attention(proj, prep_tr, batch, seq)
        o_gdn = gated_deltanet(proj, prep_tok, prep_tr, gdn_conv[l], gdn_norm[l], batch, seq)
        merged = merge_branches(h, (o_ret, o_dsa, o_fox, o_gdn), w_gate, cast_branch(w_branch, l))
        xf = resid_matmul(merged, cast_layer(w_out, l), xf, name="out_proj")
        act = conv_ffn_up(xf, norm_ffn[l], cast_layer(ffn_w_gate, l), cast_layer(ffn_w_up, l),
                          ffn_conv[l], ffn_conv_b[l], seq)
        xf = resid_matmul(act, cast_layer(ffn_w_down, l), xf, name="ffn_down")
    return rmsnorm(xf, final_norm).reshape(batch, seq, d)
```

```python
import functools
import math

import jax
import jax.numpy as jnp
from jax import lax
from jax.experimental import pallas as pl
from jax.experimental.pallas import tpu as pltpu

F32 = jnp.float32
BF16 = jnp.bfloat16
I32 = jnp.int32

HEAD_DIM = 128
N_HEADS = 4
WIDTH = N_HEADS * HEAD_DIM
N_BRANCH = 4
RET_CHUNK = 128
ROPE_BASE = 10000.0
DSA_Q_RANK = 384
IDX_HEADS = 16
IDX_DIM = 64
TOPK_MAX = 256
GDN_CONV = 4
GDN_CHUNK = 64
REL_BUCKETS = 32
REL_MAX_DIST = 128
FFN_CONV = 3
EPS = 1e-6

LANE = 128
SUBLANE = 8
VMEM_LIMIT = 56 * 1024 * 1024

C_RQ, C_RK, C_RV, C_RG = 0, 512, 1024, 1536
C_FQ, C_FK, C_FV = 2048, 2560, 3072
C_GQ, C_GK, C_GV, C_GZ = 3584, 4096, 4608, 5120
C_DK, C_DCQ, C_DV, C_KA, C_KB, C_SM = 5632, 5760, 6144, 6272, 6400, 6528
C_TOT = 6656
L_IW, L_FF, L_GB, L_GA = 0, 16, 20, 24

LOG2E = 1.4426950408889634
INT_MIN = -(2 ** 31)
INT_MAX = 2 ** 31 - 1
HIGHEST = lax.Precision.HIGHEST


def _cparams(sem, vmem=VMEM_LIMIT):
    return pltpu.CompilerParams(dimension_semantics=sem, vmem_limit_bytes=vmem)


def _dot(a, b):
    return jnp.dot(a, b, preferred_element_type=F32)


def _dot_nt(a, b):
    return lax.dot_general(a, b, (((1,), (1,)), ((), ())), preferred_element_type=F32)


def _silu(x):
    return x * jax.nn.sigmoid(x)


NORM_ROWS = 32


def _rmsnorm_rows(x_ref, g_ref, rows):
    x = x_ref[rows, :]
    ms = jnp.mean(x * x, axis=-1, keepdims=True)
    return x * lax.rsqrt(ms + EPS) * g_ref[...]


def _norm_proj_kernel(x_ref, g_ref, w_ref, o_ref, h_ref, h_scr, *, tm):
    @pl.when(pl.program_id(1) == 0)
    def _():
        def body(r, carry):
            rows = pl.ds(pl.multiple_of(r * NORM_ROWS, NORM_ROWS), NORM_ROWS)
            hb = _rmsnorm_rows(x_ref, g_ref, rows).astype(BF16)
            h_scr[rows, :] = hb
            h_ref[rows, :] = hb
            return carry
        lax.fori_loop(0, tm // NORM_ROWS, body, 0)

    o_ref[...] = _dot(h_scr[...], w_ref[...])


def norm_proj(x, gain, w, *, tm=1024, tn=512):
    n, d = x.shape
    c = w.shape[1]
    return pl.pallas_call(
        functools.partial(_norm_proj_kernel, tm=tm),
        grid=(n // tm, c // tn),
        in_specs=[pl.BlockSpec((tm, d), lambda i, j: (i, 0)),
                  pl.BlockSpec((1, d), lambda i, j: (0, 0)),
                  pl.BlockSpec((d, tn), lambda i, j: (0, j))],
        out_specs=[pl.BlockSpec((tm, tn), lambda i, j: (i, j)),
                   pl.BlockSpec((tm, d), lambda i, j: (i, 0))],
        out_shape=[jax.ShapeDtypeStruct((n, c), F32),
                   jax.ShapeDtypeStruct((n, d), BF16)],
        scratch_shapes=[pltpu.VMEM((tm, d), BF16)],
        compiler_params=_cparams(("arbitrary", "arbitrary")),
        name="norm_proj",
    )(x, gain.reshape(1, d), w)


def _rmsnorm_kernel(x_ref, g_ref, o_ref, *, tm):
    def body(r, carry):
        rows = pl.ds(pl.multiple_of(r * NORM_ROWS, NORM_ROWS), NORM_ROWS)
        o_ref[rows, :] = _rmsnorm_rows(x_ref, g_ref, rows)
        return carry
    lax.fori_loop(0, tm // NORM_ROWS, body, 0)


def rmsnorm(x, gain, *, tm=512):
    n, d = x.shape
    return pl.pallas_call(
        functools.partial(_rmsnorm_kernel, tm=tm),
        grid=(n // tm,),
        in_specs=[pl.BlockSpec((tm, d), lambda i: (i, 0)),
                  pl.BlockSpec((1, d), lambda i: (0, 0))],
        out_specs=pl.BlockSpec((tm, d), lambda i: (i, 0)),
        out_shape=jax.ShapeDtypeStruct((n, d), F32),
        compiler_params=_cparams(("arbitrary",)),
        name="final_rmsnorm",
    )(x, gain.reshape(1, d))


def _prep_kernel(s_ref, par_ref, tok_ref, tr_ref, carry_scr):
    @pl.when(pl.program_id(1) == 0)
    def _():
        carry_scr[...] = jnp.zeros_like(carry_scr)

    s = s_ref[...]
    lane = lax.broadcasted_iota(I32, (LANE, LANE), 1)
    row = lax.broadcasted_iota(I32, (LANE, LANE), 0)
    z = s + par_ref[0:1, :]
    soft = jnp.maximum(z, 0.0) + jnp.log1p(jnp.exp(-jnp.abs(z)))
    log_sig = z - soft
    sig = jax.nn.sigmoid(z)
    g_val = -jnp.exp(par_ref[1:2, :]) * soft
    is_f = (lane >= L_FF) & (lane < L_FF + N_HEADS)
    is_b = (lane >= L_GB) & (lane < L_GB + N_HEADS)
    is_a = (lane >= L_GA) & (lane < L_GA + N_HEADS)
    pre = jnp.where(is_f, log_sig, jnp.where(is_a, g_val, 0.0))
    tri = (row >= lane).astype(F32)
    tri_blk = ((row >= lane) & ((row // GDN_CHUNK) == (lane // GDN_CHUNK))).astype(F32)
    cum_full = jnp.dot(tri, pre, precision=HIGHEST, preferred_element_type=F32)
    cum_blk = jnp.dot(tri_blk, pre, precision=HIGHEST, preferred_element_type=F32)
    c_fox = cum_full + carry_scr[0:1, :]
    carry_scr[0:1, :] = c_fox[LANE - 1:LANE, :]
    scale_iw = IDX_HEADS ** -0.5 * IDX_DIM ** -0.5
    out = jnp.where(is_f, c_fox,
                    jnp.where(is_a, cum_blk,
                              jnp.where(is_b, sig,
                                        jnp.where(lane < IDX_HEADS, s * scale_iw, 0.0))))
    tok_ref[...] = out
    tr_ref[0] = out.T[0:32, :]


def prep_small(proj, par, batch, seq):
    n = proj.shape[0]
    nc = seq // LANE
    return pl.pallas_call(
        _prep_kernel,
        grid=(batch, nc),
        in_specs=[pl.BlockSpec((LANE, LANE), lambda b, c: (b * nc + c, C_SM // LANE)),
                  pl.BlockSpec((8, LANE), lambda b, c: (0, 0))],
        out_specs=[pl.BlockSpec((LANE, LANE), lambda b, c: (b * nc + c, 0)),
                   pl.BlockSpec((1, 32, LANE), lambda b, c: (b, 0, c))],
        out_shape=[jax.ShapeDtypeStruct((n, LANE), F32),
                   jax.ShapeDtypeStruct((batch, 32, seq), F32)],
        scratch_shapes=[pltpu.VMEM((8, LANE), F32)],
        compiler_params=_cparams(("arbitrary", "arbitrary")),
        name="prep_small",
    )(proj, par)


def _ret_gamma():
    return [math.log1p(-(2.0 ** (-5.0 - h))) for h in range(N_HEADS)]


def _retention_kernel(q_ref, k_ref, v_ref, g_ref, cos_ref, sin_ref, dec_ref, zeta_ref, xi_ref,
                      o_ref, state_scr):
    @pl.when(pl.program_id(1) == 0)
    def _():
        state_scr[...] = jnp.zeros_like(state_scr)

    cos_t = cos_ref[...]
    sin_t = sin_ref[...]
    log_gamma = _ret_gamma()
    for h in range(N_HEADS):
        sl = slice(h * HEAD_DIM, (h + 1) * HEAD_DIM)
        q = q_ref[:, sl]
        k = k_ref[:, sl]
        qr = q * cos_t + pltpu.roll(q, HEAD_DIM // 2, 1) * sin_t
        kr = (k * cos_t + pltpu.roll(k, HEAD_DIM // 2, 1) * sin_t) * (HEAD_DIM ** -0.5)
        qb = qr.astype(BF16)
        kb = kr.astype(BF16)
        vb = v_ref[:, sl].astype(BF16)
        inner = _dot_nt(qb, kb) * dec_ref[h]
        st = state_scr[h]
        o = _dot(inner.astype(BF16), vb) + _dot(qb, st.astype(BF16)) * xi_ref[h]
        kz_t = (kr * zeta_ref[h]).T.astype(BF16)
        state_scr[h] = st * math.exp(log_gamma[h] * RET_CHUNK) + _dot(kz_t, vb)
        mu = jnp.mean(o, axis=-1, keepdims=True)
        oc = o - mu
        var = jnp.mean(oc * oc, axis=-1, keepdims=True)
        o_ref[:, sl] = (_silu(g_ref[:, sl]) * (oc * lax.rsqrt(var + EPS))).astype(BF16)


def _retention_tables(seq):
    half = HEAD_DIM // 2
    inv = 1.0 / (ROPE_BASE ** (jnp.arange(half, dtype=F32) / half))
    ang = jnp.arange(seq).astype(F32)[:, None] * inv[None, :]
    cos, sin = jnp.cos(ang), jnp.sin(ang)
    cos_t = jnp.concatenate([cos, cos], axis=-1)
    sin_t = jnp.concatenate([-sin, sin], axis=-1)
    c = RET_CHUNK
    log_gamma = jnp.log1p(-jnp.exp2(-5.0 - jnp.arange(N_HEADS, dtype=F32)))
    n = jnp.arange(c, dtype=F32)
    diff = n[:, None] - n[None, :]
    decay = jnp.where(diff >= 0, jnp.exp(log_gamma[:, None, None] * jnp.maximum(diff, 0.0)), 0.0)
    zeta = jnp.exp(log_gamma[:, None] * (c - 1 - n)[None, :])
    xi = jnp.exp(log_gamma[:, None] * (n + 1)[None, :])
    ones = jnp.ones((1, 1, HEAD_DIM), F32)
    return cos_t, sin_t, decay, zeta[:, :, None] * ones, xi[:, :, None] * ones


def retention(proj, tables, batch, seq):
    n = proj.shape[0]
    c = RET_CHUNK
    nc = seq // c
    cos_t, sin_t, decay, zeta, xi = tables
    col = lambda off: pl.BlockSpec((c, WIDTH), lambda b, i: (b * nc + i, off // WIDTH))
    full3 = pl.BlockSpec((N_HEADS, c, HEAD_DIM), lambda b, i: (0, 0, 0))
    return pl.pallas_call(
        _retention_kernel,
        grid=(batch, nc),
        in_specs=[col(C_RQ), col(C_RK), col(C_RV), col(C_RG),
                  pl.BlockSpec((c, HEAD_DIM), lambda b, i: (i, 0)),
                  pl.BlockSpec((c, HEAD_DIM), lambda b, i: (i, 0)),
                  full3, full3, full3],
        out_specs=pl.BlockSpec((c, WIDTH), lambda b, i: (b * nc + i, 0)),
        out_shape=jax.ShapeDtypeStruct((n, WIDTH), BF16),
        scratch_shapes=[pltpu.VMEM((N_HEADS, HEAD_DIM, HEAD_DIM), F32)],
        compiler_params=_cparams(("arbitrary", "arbitrary")),
        name="retention",
    )(proj, proj, proj, proj, cos_t, sin_t, decay, zeta, xi)


def _fox_kernel(q_ref, k_ref, v_ref, ctr_ref, o_ref, m_scr, acc_scr, *, t):
    qi = pl.program_id(1)
    ki = pl.program_id(2)

    @pl.when(ki == 0)
    def _():
        m_scr[...] = jnp.full_like(m_scr, -jnp.inf)
        acc_scr[...] = jnp.zeros_like(acc_scr)

    def step(masked):
        if masked:
            row = lax.broadcasted_iota(I32, (t, t), 0)
            colm = lax.broadcasted_iota(I32, (t, t), 1)
            keep = row >= colm
        ones = jnp.ones((t, HEAD_DIM), BF16)
        for h in range(N_HEADS):
            sl = slice(h * HEAD_DIM, (h + 1) * HEAD_DIM)
            qb = q_ref[:, sl].astype(BF16)
            kb = k_ref[:, sl].astype(BF16)
            c_k = ctr_ref[0, L_FF + h:L_FF + h + 1, :] * LOG2E
            s = _dot_nt(qb, kb) * (HEAD_DIM ** -0.5 * LOG2E) - c_k
            if masked:
                s = jnp.where(keep, s, -jnp.inf)
            m_old = m_scr[h]
            m_new = jnp.maximum(m_old, jnp.max(s, axis=-1, keepdims=True))
            alpha = jnp.exp2(m_old - m_new)
            p = jnp.exp2(s - m_new)
            v_aug = jnp.concatenate([v_ref[:, sl].astype(BF16), ones], axis=1)
            acc_scr[h] = alpha * acc_scr[h] + _dot(p.astype(BF16), v_aug)
            m_scr[h] = m_new

    @pl.when(ki < qi)
    def _():
        step(False)

    @pl.when(ki == qi)
    def _():
        step(True)
        for h in range(N_HEADS):
            sl = slice(h * HEAD_DIM, (h + 1) * HEAD_DIM)
            acc = acc_scr[h]
            o_ref[:, sl] = (acc[:, :HEAD_DIM] / acc[:, HEAD_DIM:]).astype(BF16)


def fox_attention(proj, prep_tr, batch, seq, *, t=512):
    n = proj.shape[0]
    nt = seq // t
    qspec = pl.BlockSpec((t, WIDTH), lambda b, qi, ki: (b * nt + qi, C_FQ // WIDTH))
    kspec = lambda off: pl.BlockSpec(
        (t, WIDTH), lambda b, qi, ki: (b * nt + jnp.minimum(ki, qi), off // WIDTH))
    return pl.pallas_call(
        functools.partial(_fox_kernel, t=t),
        grid=(batch, nt, nt),
        in_specs=[qspec, kspec(C_FK), kspec(C_FV),
                  pl.BlockSpec((1, 32, t), lambda b, qi, ki: (b, 0, jnp.minimum(ki, qi)))],
        out_specs=pl.BlockSpec((t, WIDTH), lambda b, qi, ki: (b * nt + qi, 0)),
        out_shape=jax.ShapeDtypeStruct((n, WIDTH), BF16),
        scratch_shapes=[pltpu.VMEM((N_HEADS, t, 1), F32),
                        pltpu.VMEM((N_HEADS, t, 2 * HEAD_DIM), F32)],
        compiler_params=_cparams(("arbitrary", "arbitrary", "arbitrary")),
        name="fox_attention",
    )(proj, proj, proj, prep_tr)


def _dsa_proj_kernel(cq_ref, g_ref, wq_ref, wi_ref, q_ref, qi_ref):
    x = cq_ref[...]
    ms = jnp.mean(x * x, axis=-1, keepdims=True)
    cb = (x * lax.rsqrt(ms + EPS) * g_ref[...]).astype(BF16)
    q_ref[...] = _dot(cb, wq_ref[...]).astype(BF16)
    qi_ref[...] = _dot(cb, wi_ref[...]).astype(BF16)


def dsa_proj(proj, cq_norm, w_uq, w_qidx, *, tm=512):
    n = proj.shape[0]
    r = DSA_Q_RANK
    wi = IDX_HEADS * IDX_DIM
    return pl.pallas_call(
        _dsa_proj_kernel,
        grid=(n // tm,),
        in_specs=[pl.BlockSpec((tm, r), lambda i: (i, C_DCQ // r)),
                  pl.BlockSpec((1, r), lambda i: (0, 0)),
                  pl.BlockSpec((r, WIDTH), lambda i: (0, 0)),
                  pl.BlockSpec((r, wi), lambda i: (0, 0))],
        out_specs=[pl.BlockSpec((tm, WIDTH), lambda i: (i, 0)),
                   pl.BlockSpec((tm, wi), lambda i: (i, 0))],
        out_shape=[jax.ShapeDtypeStruct((n, WIDTH), BF16),
                   jax.ShapeDtypeStruct((n, wi), BF16)],
        compiler_params=_cparams(("arbitrary",)),
        name="dsa_proj",
    )(proj, cq_norm.reshape(1, r), w_uq, w_qidx)


DSA_QB = 256
DSA_KC = 512
DSA_SCORE_MID_STEPS = 20
DSA_FEW_KEYS = 4
DSA_HALVE_FIXED = 12
DSA_WALK_FIXED = 3


def _t5_bucket(rel):
    max_exact = REL_BUCKETS // 2
    relf = jnp.maximum(rel, max_exact).astype(F32)
    large = max_exact + (jnp.log(relf / max_exact) / math.log(REL_MAX_DIST / max_exact)
                         * (REL_BUCKETS - max_exact)).astype(I32)
    large = jnp.minimum(large, REL_BUCKETS - 1)
    return jnp.where(rel < max_exact, rel, large)


def _dsa_kernel(rb_ref, q_ref, qi_ref, tok_ref, k_ref, v_ref, ka_ref, kb_ref, o_ref,
                key_scr, lg_scr, band_scr, kb16_scr, vt_scr, ka16_scr, kb16i_scr, *, seq, topk):
    qb_idx = pl.program_id(1)
    t0 = qb_idx * DSA_QB
    n_kc = (t0 + DSA_QB - 1) // DSA_KC + 1
    row_vec = (1, DSA_QB)

    @pl.when(qb_idx == 0)
    def _():
        kb16_scr[...] = k_ref[...].astype(BF16)
        ka16_scr[...] = ka_ref[...].astype(BF16)
        kb16i_scr[...] = kb_ref[...].astype(BF16)
        for c in range(seq // DSA_KC):
            cs = slice(c * DSA_KC, (c + 1) * DSA_KC)
            vt_scr[0:HEAD_DIM, cs] = v_ref[cs, :].T.astype(BF16)
        vt_scr[HEAD_DIM:, :] = jnp.ones((HEAD_DIM, seq), BF16)

    @pl.when((pl.program_id(0) == 0) & (qb_idx == 0))
    def _():
        j_ = lax.broadcasted_iota(I32, (2 * DSA_QB, DSA_QB), 0)
        i_ = lax.broadcasted_iota(I32, (2 * DSA_QB, DSA_QB), 1)
        rel = i_ + DSA_QB - j_
        bucket = _t5_bucket(rel)
        for h in range(N_HEADS):
            far = rb_ref[REL_BUCKETS - 1, h]
            band = jnp.zeros((2 * DSA_QB, DSA_QB), F32)
            for bk in range(REL_BUCKETS - 1):
                band = jnp.where(bucket == bk, (rb_ref[bk, h] - far) * LOG2E, band)
            band_scr[h] = jnp.where(rel >= 0, band, 0.0)

    w_t = tok_ref[...].T
    key_s = lax.broadcasted_iota(I32, (DSA_KC, DSA_QB), 0)
    row_t = t0 + lax.broadcasted_iota(I32, (DSA_KC, DSA_QB), 1)

    def score_chunk(c, carry):
        kmax, kmin = carry
        ks = pl.ds(pl.multiple_of(c * DSA_KC, DSA_KC), DSA_KC)
        ka = ka16_scr[ks, :]
        kb = kb16i_scr[ks, :]
        acc = jnp.zeros((DSA_KC, DSA_QB), F32)
        for p in range(IDX_HEADS // 2):
            qp = qi_ref[:, p * LANE:(p + 1) * LANE]
            acc = acc + jnp.maximum(_dot_nt(ka, qp), 0.0) * w_t[2 * p:2 * p + 1, :]
            acc = acc + jnp.maximum(_dot_nt(kb, qp), 0.0) * w_t[2 * p + 1:2 * p + 2, :]
        bits = pltpu.bitcast(acc, I32)
        key = bits ^ ((bits >> 31) & 0x7FFFFFFF)
        valid = (c * DSA_KC + key_s) <= row_t
        key_scr[ks, :] = jnp.where(valid, key, INT_MIN)
        kmax = jnp.maximum(kmax, jnp.max(jnp.where(valid, key, INT_MIN), axis=0, keepdims=True))
        kmin = jnp.minimum(kmin, jnp.min(jnp.where(valid, key, INT_MAX), axis=0, keepdims=True))
        return kmax, kmin

    kmax, kmin = lax.fori_loop(0, n_kc, score_chunk, (jnp.full(row_vec, INT_MIN, I32),
                                                     jnp.full(row_vec, INT_MAX, I32)))

    def scan_keys(cand, with_below):
        def body(c, carry):
            cnt, below = carry
            ks = pl.ds(pl.multiple_of(c * DSA_KC, DSA_KC), DSA_KC)
            keys = key_scr[ks, :]
            ge = keys >= cand
            ones = ge.astype(I32)
            low = jnp.where(ge, INT_MIN, keys)
            for u in range(DSA_KC // SUBLANE):
                us = slice(u * SUBLANE, (u + 1) * SUBLANE)
                cnt = cnt + ones[us, :]
                if with_below:
                    below = jnp.maximum(below, low[us, :])
            return cnt, below
        cnt, below = lax.fori_loop(0, n_kc, body, (jnp.zeros((SUBLANE, DSA_QB), I32),
                                                   jnp.full((SUBLANE, DSA_QB), INT_MIN, I32)))
        cnt = jnp.sum(cnt, axis=0, keepdims=True)
        if with_below:
            return cnt, jnp.max(below, axis=0, keepdims=True)
        return cnt

    def key_to_score(k):
        return pltpu.bitcast(k ^ ((k >> 31) & 0x7FFFFFFF), F32)

    def score_to_key(s):
        b = pltpu.bitcast(s, I32)
        return b ^ ((b >> 31) & 0x7FFFFFFF)

    def open_rows(lo, hi, c_lo):
        return (c_lo > topk) & (hi - 1 > lo)

    def any_row(flag):
        return jnp.max(jnp.where(flag, 1, 0))

    def update(cand, cnt, lo, hi, c_lo, c_hi):
        ge = cnt >= topk
        return (jnp.where(ge, cand, lo), jnp.where(ge, hi, cand),
                jnp.where(ge, cnt, c_lo), jnp.where(ge, c_hi, cnt))

    def crowded(lo, hi, c_lo, c_hi):
        return any_row(open_rows(lo, hi, c_lo) & (c_lo - c_hi > DSA_FEW_KEYS))

    def halve_step(it, lo, hi, c_lo, c_hi):
        key_mid = (lo >> 1) + (hi >> 1) + (lo & hi & 1)
        score_mid = score_to_key(0.5 * key_to_score(lo) + 0.5 * key_to_score(hi - 1))
        cand = jnp.where(it < DSA_SCORE_MID_STEPS, score_mid, key_mid)
        cand = jnp.minimum(jnp.maximum(cand, lo + 1), hi - 1)
        cand = jnp.where(hi - 1 > lo, cand, lo)
        return update(cand, scan_keys(cand, False), lo, hi, c_lo, c_hi)

    def halve_body(st):
        it, _, lo, hi, c_lo, c_hi = st
        go = crowded(lo, hi, c_lo, c_hi)
        return (it + 1, go) + halve_step(it, lo, hi, c_lo, c_hi)

    def walk_step(lo, hi, c_lo, c_hi, nxt):
        is_open = open_rows(lo, hi, c_lo)
        cand = jnp.where(is_open, nxt, lo)
        cnt, below = scan_keys(cand, True)
        ge = cnt >= topk
        hi = jnp.where(is_open, jnp.where(ge, cand + 1, cand), hi)
        c_hi = jnp.where(is_open & jnp.logical_not(ge), cnt, c_hi)
        lo = jnp.where(is_open & ge, cand, lo)
        c_lo = jnp.where(is_open & ge, cnt, c_lo)
        nxt = jnp.where(ge, nxt, below)
        return lo, hi, c_lo, c_hi, nxt

    def walk_body(st):
        go = any_row(open_rows(st[1], st[2], st[3]))
        return (go,) + walk_step(*st[1:])

    n_valid = jnp.minimum(t0 + lax.broadcasted_iota(I32, row_vec, 1) + 1, seq)
    st = (kmin, kmax + 1, n_valid, jnp.zeros(row_vec, I32))
    st = lax.fori_loop(0, DSA_HALVE_FIXED, lambda it, s: halve_step(it, *s), st)
    st = lax.while_loop(lambda s: s[1] > 0, halve_body,
                        (jnp.int32(DSA_HALVE_FIXED), crowded(*st)) + st)[2:]
    _, nxt0 = scan_keys(st[1], True)
    st = lax.fori_loop(0, DSA_WALK_FIXED, lambda it, s: walk_step(*s), st + (nxt0,))
    _, thr, hi, n_ge, n_gt, _ = lax.while_loop(
        lambda s: s[0] > 0, walk_body, (any_row(open_rows(st[0], st[1], st[2])),) + st)

    tied = n_ge > topk
    has_tie = jnp.max(jnp.where(tied, 1, 0)) > 0

    @pl.when(has_tie)
    def _():
        room = (topk - n_gt).astype(F32)
        ii = lax.broadcasted_iota(I32, (LANE, LANE), 0)
        jj = lax.broadcasted_iota(I32, (LANE, LANE), 1)
        lower = (ii >= jj).astype(BF16)

        def body(c, seen):
            ks = pl.ds(pl.multiple_of(c * LANE, LANE), LANE)
            kk = key_scr[ks, :]
            eq = kk == thr
            rank = seen + _dot(lower, eq.astype(BF16))
            drop = eq & (rank > room) & tied
            key_scr[ks, :] = jnp.where(drop, INT_MIN, kk)
            return seen + jnp.sum(eq.astype(F32), axis=0, keepdims=True)
        lax.fori_loop(0, n_kc * (DSA_KC // LANE), body, jnp.zeros(row_vec, F32))

    def mask_chunk(c, carry):
        ks = pl.ds(pl.multiple_of(c * DSA_KC, DSA_KC), DSA_KC)
        sel = jnp.where(key_scr[ks, :] >= thr, 0.0, -jnp.inf).astype(F32)
        key_scr[ks, :] = pltpu.bitcast(sel, I32)
        return carry
    lax.fori_loop(0, n_kc, mask_chunk, 0)

    heads = range(N_HEADS)
    hsl = [slice(h * HEAD_DIM, (h + 1) * HEAD_DIM) for h in heads]

    def logit_chunk(c, ms):
        ks = pl.ds(pl.multiple_of(c * DSA_KC, DSA_KC), DSA_KC)
        k_c = kb16_scr[ks, :]
        sel = pltpu.bitcast(key_scr[ks, :], F32)
        out = []
        for h in heads:
            s = _dot_nt(k_c, q_ref[:, hsl[h]]) * (HEAD_DIM ** -0.5 * LOG2E) + sel
            lg_scr[h, ks, :] = s
            out.append(jnp.maximum(ms[h], jnp.max(s, axis=0, keepdims=True)))
        return tuple(out)
    ms = lax.fori_loop(0, n_kc, logit_chunk,
                       tuple(jnp.full(row_vec, -jnp.inf, F32) for _ in heads))

    band_off = pl.multiple_of(jnp.maximum(qb_idx - 1, 0) * DSA_QB, DSA_QB)
    ws = pl.ds(band_off, 2 * DSA_QB)
    ms = list(ms)
    for h in heads:
        band_h = band_scr[h]
        band_first = jnp.concatenate([band_h[DSA_QB:, :], jnp.zeros((DSA_QB, DSA_QB), F32)], axis=0)
        win = lg_scr[h, ws, :] + jnp.where(qb_idx == 0, band_first, band_h)
        lg_scr[h, ws, :] = win
        ms[h] = jnp.maximum(ms[h], jnp.max(win, axis=0, keepdims=True))

    def pv_chunk(c, accs):
        ks = pl.ds(pl.multiple_of(c * DSA_KC, DSA_KC), DSA_KC)
        vt_c = vt_scr[:, ks]
        return tuple(accs[h] + _dot(vt_c, jnp.exp2(lg_scr[h, ks, :] - ms[h]).astype(BF16))
                     for h in heads)
    accs = lax.fori_loop(0, n_kc, pv_chunk,
                         tuple(jnp.zeros((2 * HEAD_DIM, DSA_QB), F32) for _ in heads))
    for h in heads:
        o_ref[:, hsl[h]] = (accs[h][:HEAD_DIM, :] / accs[h][HEAD_DIM:, :]).T.astype(BF16)


def dsa_attention(proj, q, q_idx, prep_tok, rel_bias, batch, seq):
    n = proj.shape[0]
    nq = seq // DSA_QB
    topk = min(TOPK_MAX, seq // 4)
    wi = IDX_HEADS * IDX_DIM
    rowblk = lambda w, cb: pl.BlockSpec((DSA_QB, w), lambda b, i: (b * nq + i, cb))
    seqblk = lambda off: pl.BlockSpec((seq, LANE), lambda b, i: (b, off // LANE))
    return pl.pallas_call(
        functools.partial(_dsa_kernel, seq=seq, topk=topk),
        grid=(batch, nq),
        in_specs=[pl.BlockSpec(memory_space=pltpu.SMEM),
                  rowblk(WIDTH, 0), rowblk(wi, 0), rowblk(LANE, 0),
                  seqblk(C_DK), seqblk(C_DV), seqblk(C_KA), seqblk(C_KB)],
        out_specs=pl.BlockSpec((DSA_QB, WIDTH), lambda b, i: (b * nq + i, 0)),
        out_shape=jax.ShapeDtypeStruct((n, WIDTH), BF16),
        scratch_shapes=[pltpu.VMEM((seq, DSA_QB), I32),
                        pltpu.VMEM((N_HEADS, seq, DSA_QB), F32),
                        pltpu.VMEM((N_HEADS, 2 * DSA_QB, DSA_QB), F32),
                        pltpu.VMEM((seq, LANE), BF16),
                        pltpu.VMEM((2 * HEAD_DIM, seq), BF16),
                        pltpu.VMEM((seq, LANE), BF16),
                        pltpu.VMEM((seq, LANE), BF16)],
        compiler_params=_cparams(("arbitrary", "arbitrary")),
        name="dsa_attention",
    )(rel_bias, q, q_idx, prep_tok, proj, proj, proj, proj)


GDN_T = 256
GDN_HALO = 8


def _gdn_kernel(q_ref, k_ref, v_ref, z_ref, cw_ref, ng_ref, tok_ref, tr_ref, o_ref,
                xq_scr, xk_scr, xv_scr, state_scr):
    first = pl.program_id(1) == 0

    @pl.when(first)
    def _():
        state_scr[...] = jnp.zeros_like(state_scr)
        for scr in (xq_scr, xk_scr, xv_scr):
            scr[0:GDN_HALO, :] = jnp.zeros((GDN_HALO, WIDTH), F32)

    def conv(x_ref, scr, w_off):
        scr[GDN_HALO:, :] = x_ref[...]
        y = jnp.zeros((GDN_T, WIDTH), F32)
        for i in range(GDN_CONV):
            st = GDN_HALO - (GDN_CONV - 1) + i
            y = y + scr[st:st + GDN_T, :] * cw_ref[i:i + 1, w_off:w_off + WIDTH]
        scr[0:GDN_HALO, :] = scr[GDN_T:GDN_T + GDN_HALO, :]
        return _silu(y)

    qc = conv(q_ref, xq_scr, 0)
    kc = conv(k_ref, xk_scr, WIDTH)
    vc = conv(v_ref, xv_scr, 2 * WIDTH)
    tok = tok_ref[...]
    c = GDN_CHUNK
    heads = range(N_HEADS)
    hsl = [slice(h * HEAD_DIM, (h + 1) * HEAD_DIM) for h in heads]

    def l2norm_heads(x, scale):
        return jnp.concatenate(
            [x[:, s] * (lax.rsqrt(jnp.sum(x[:, s] * x[:, s], axis=-1, keepdims=True) + EPS) * scale)
             for s in hsl], axis=1)

    qf = l2norm_heads(qc, HEAD_DIM ** -0.5)
    kf = l2norm_heads(kc, 1.0)

    nb = N_HEADS * c
    ri = lax.broadcasted_iota(I32, (nb, nb), 0)
    ci = lax.broadcasted_iota(I32, (nb, nb), 1)
    same_head = (ri // c) == (ci // c)
    tril = same_head & (ri >= ci)
    strict = same_head & (ri > ci)
    eye = (ri == ci).astype(F32)
    lane_head = lax.broadcasted_iota(I32, (c, WIDTH), 1) // HEAD_DIM
    row_head = lax.broadcasted_iota(I32, (nb, HEAD_DIM), 0) // c

    def spread(x):
        return jnp.concatenate([jnp.where(lane_head == h, x, 0.0) for h in heads], axis=0)

    def stack(x):
        return jnp.concatenate([x[:, s] for s in hsl], axis=0)

    def spread_lanes(x):
        return jnp.concatenate([jnp.where(row_head == h, x, 0.0) for h in heads], axis=1)

    def split(x):
        hi = x.astype(BF16)
        return hi, (x - hi.astype(F32)).astype(BF16)

    def dot_split(a, b):
        return _dot(a[0], b[0]) + (_dot(a[0], b[1]) + _dot(a[1], b[0]))

    outs = [[] for _ in heads]
    for j in range(GDN_T // c):
        rs = slice(j * c, (j + 1) * c)
        last = slice((j + 1) * c - 1, (j + 1) * c)
        qj, kj, vj = qf[rs], kf[rs], vc[rs]
        b_col = jnp.concatenate([tok[rs, L_GB + h:L_GB + h + 1] for h in heads], axis=0)
        g_col = jnp.concatenate([tok[rs, L_GA + h:L_GA + h + 1] for h in heads], axis=0)
        g_row = jnp.concatenate([tr_ref[0, L_GA + h:L_GA + h + 1, rs] for h in heads], axis=1)
        g_last = [tok[last, L_GA + h:L_GA + h + 1] for h in heads]
        g_last_col = jnp.concatenate([jnp.broadcast_to(g, (c, 1)) for g in g_last], axis=0)
        k_sp = spread(kj)
        q_sp = spread(qj)
        k_sp16 = k_sp.astype(BF16)
        decay = jnp.exp(jnp.where(tril, g_col - g_row, -jnp.inf))
        neg_l = -jnp.where(strict, b_col * _dot_nt(k_sp16, k_sp16) * decay, 0.0)
        t_inv = eye + neg_l
        pw_s = split(neg_l)
        for _ in range(5):
            pw_s = split(dot_split(pw_s, pw_s))
            t_inv = t_inv + dot_split(split(t_inv), pw_s)
        eg = jnp.exp(g_col)
        rhs = jnp.concatenate([stack(vj) * b_col, stack(kj) * (b_col * eg)], axis=1)
        sol = dot_split(split(t_inv), split(rhs))
        u0 = sol[:, :HEAD_DIM]
        kcum = sol[:, HEAD_DIM:]
        qk = _dot_nt(q_sp.astype(BF16), k_sp16) * decay
        q_dec = q_sp * eg
        k_dec = k_sp * jnp.exp(g_last_col - g_col)
        st = state_scr[...]
        stb = st.astype(BF16)
        v_new = u0 - _dot(spread_lanes(kcum).astype(BF16), stb)
        v_new_b = v_new.astype(BF16)
        o_st = _dot(q_dec.astype(BF16), stb) + _dot(qk.astype(BF16), v_new_b)
        e_last = jnp.concatenate([jnp.broadcast_to(jnp.exp(g), (HEAD_DIM, 1)) for g in g_last], axis=0)
        state_scr[...] = st * e_last + _dot(k_dec.T.astype(BF16), v_new_b)
        for h in heads:
            outs[h].append(o_st[h * c:(h + 1) * c, :])

    for h in heads:
        o = jnp.concatenate(outs[h], axis=0)
        ms = jnp.mean(o * o, axis=-1, keepdims=True)
        on = o * lax.rsqrt(ms + EPS) * ng_ref[...]
        o_ref[:, hsl[h]] = (on * _silu(z_ref[:, hsl[h]])).astype(BF16)


def gated_deltanet(proj, prep_tok, prep_tr, conv_w, norm_g, batch, seq):
    n = proj.shape[0]
    t = GDN_T
    nt = seq // t
    col = lambda off: pl.BlockSpec((t, WIDTH), lambda b, i: (b * nt + i, off // WIDTH))
    return pl.pallas_call(
        _gdn_kernel,
        grid=(batch, nt),
        in_specs=[col(C_GQ), col(C_GK), col(C_GV), col(C_GZ),
                  pl.BlockSpec((GDN_CONV, 3 * WIDTH), lambda b, i: (0, 0)),
                  pl.BlockSpec((1, HEAD_DIM), lambda b, i: (0, 0)),
                  pl.BlockSpec((t, LANE), lambda b, i: (b * nt + i, 0)),
                  pl.BlockSpec((1, 32, t), lambda b, i: (b, 0, i))],
        out_specs=pl.BlockSpec((t, WIDTH), lambda b, i: (b * nt + i, 0)),
        out_shape=jax.ShapeDtypeStruct((n, WIDTH), BF16),
        scratch_shapes=[pltpu.VMEM((t + GDN_HALO, WIDTH), F32),
                        pltpu.VMEM((t + GDN_HALO, WIDTH), F32),
                        pltpu.VMEM((t + GDN_HALO, WIDTH), F32),
                        pltpu.VMEM((N_HEADS * HEAD_DIM, HEAD_DIM), F32)],
        compiler_params=_cparams(("arbitrary", "arbitrary")),
        name="gated_deltanet",
    )(proj, proj, proj, proj, conv_w, norm_g.reshape(1, HEAD_DIM), prep_tok, prep_tr)


def _merge_kernel(h_ref, b0_ref, b1_ref, b2_ref, b3_ref, g0_ref, g1_ref, g2_ref, g3_ref,
                  wb_ref, o_ref):
    h = h_ref[...]
    acc = None
    for n, (b_ref, g_ref) in enumerate(zip((b0_ref, b1_ref, b2_ref, b3_ref),
                                           (g0_ref, g1_ref, g2_ref, g3_ref))):
        gate = jax.nn.sigmoid(_dot(h, g_ref[...]))
        term = gate * _dot(b_ref[...], wb_ref[n])
        acc = term if acc is None else acc + term
    o_ref[...] = acc.astype(BF16)


def merge_branches(h, branches, w_gate, w_branch, *, tm=512, tn=512):
    n, d = h.shape
    nj = d // tn
    bspec = pl.BlockSpec((tm, WIDTH), lambda j, i: (i, 0))
    gspec = lambda k: pl.BlockSpec((d, tn), lambda j, i: (0, k * nj + j))
    return pl.pallas_call(
        _merge_kernel,
        grid=(nj, n // tm),
        in_specs=[pl.BlockSpec((tm, d), lambda j, i: (i, 0)),
                  bspec, bspec, bspec, bspec,
                  gspec(0), gspec(1), gspec(2), gspec(3),
                  pl.BlockSpec((N_BRANCH, WIDTH, tn), lambda j, i: (0, 0, j))],
        out_specs=pl.BlockSpec((tm, tn), lambda j, i: (i, j)),
        out_shape=jax.ShapeDtypeStruct((n, d), BF16),
        compiler_params=_cparams(("arbitrary", "arbitrary")),
        name="merge_branches",
    )(h, *branches, w_gate, w_gate, w_gate, w_gate, w_branch)


def _resid_mm_kernel(a_ref, w_ref, x_ref, o_ref):
    o_ref[...] = x_ref[...] + _dot(a_ref[...], w_ref[...])


def resid_matmul(a, w, x, *, tm=512, tn=1024, name="resid_matmul"):
    n, k = a.shape
    d = w.shape[1]
    return pl.pallas_call(
        _resid_mm_kernel,
        grid=(d // tn, n // tm),
        in_specs=[pl.BlockSpec((tm, k), lambda j, i: (i, 0)),
                  pl.BlockSpec((k, tn), lambda j, i: (0, j)),
                  pl.BlockSpec((tm, tn), lambda j, i: (i, j))],
        out_specs=pl.BlockSpec((tm, tn), lambda j, i: (i, j)),
        out_shape=jax.ShapeDtypeStruct((n, d), F32),
        compiler_params=_cparams(("arbitrary", "arbitrary")),
        name=name,
    )(a, w, x)


FFN_HALO = 8
FFN_SUB = 512


def _ffn1_kernel(x_ref, g_ref, wg_ref, wu_ref, cw_ref, cb_ref, o_ref, h_scr, gt_scr, halo_scr,
                 *, tm, tiles_per_seq):
    i = pl.program_id(0)
    j = pl.program_id(1)

    @pl.when((i == 0) & (j == 0))
    def _():
        halo_scr[...] = jnp.zeros_like(halo_scr)

    @pl.when(j == 0)
    def _():
        def body(r, carry):
            rows = pl.ds(pl.multiple_of(r * NORM_ROWS, NORM_ROWS), NORM_ROWS)
            h_scr[rows, :] = _rmsnorm_rows(x_ref, g_ref, rows).astype(BF16)
            return carry
        lax.fori_loop(0, tm // NORM_ROWS, body, 0)

    h = h_scr[...]
    seq_start = (i % tiles_per_seq) == 0
    tn = o_ref.shape[1]
    for off in range(0, tn, FFN_SUB):
        cs = slice(off, min(off + FFN_SUB, tn))
        g = _dot(h, wg_ref[:, cs])
        gt_scr[FFN_HALO:, cs] = g
        gt_scr[0:FFN_HALO, cs] = jnp.where(seq_start, 0.0, halo_scr[j, :, cs])
        halo_scr[j, :, cs] = g[tm - FFN_HALO:, :]
        y = cb_ref[:, cs] + g * cw_ref[FFN_CONV - 1:FFN_CONV, cs]
        for t in range(FFN_CONV - 1):
            st = FFN_HALO - (FFN_CONV - 1) + t
            y = y + gt_scr[st:st + tm, cs] * cw_ref[t:t + 1, cs]
        o_ref[:, cs] = (_silu(y) * _dot(h, wu_ref[:, cs])).astype(BF16)


def conv_ffn_up(x, gain, w_gate, w_up, conv_w, conv_b, seq, *, tm=512, tn=1408):
    n, d = x.shape
    f = w_gate.shape[1]
    nj = f // tn
    return pl.pallas_call(
        functools.partial(_ffn1_kernel, tm=tm, tiles_per_seq=seq // tm),
        grid=(n // tm, nj),
        in_specs=[pl.BlockSpec((tm, d), lambda i, j: (i, 0)),
                  pl.BlockSpec((1, d), lambda i, j: (0, 0)),
                  pl.BlockSpec((d, tn), lambda i, j: (0, j)),
                  pl.BlockSpec((d, tn), lambda i, j: (0, j)),
                  pl.BlockSpec((FFN_CONV, tn), lambda i, j: (0, j)),
                  pl.BlockSpec((1, tn), lambda i, j: (0, j))],
        out_specs=pl.BlockSpec((tm, tn), lambda i, j: (i, j)),
        out_shape=jax.ShapeDtypeStruct((n, f), BF16),
        scratch_shapes=[pltpu.VMEM((tm, d), BF16),
                        pltpu.VMEM((tm + FFN_HALO, tn), F32),
                        pltpu.VMEM((nj, FFN_HALO, tn), F32)],
        compiler_params=_cparams(("arbitrary", "arbitrary")),
        name="conv_ffn_up",
    )(x, gain.reshape(1, d), w_gate, w_up, conv_w, conv_b.reshape(1, f))


IN_SIZES = (WIDTH, WIDTH, WIDTH, WIDTH,
            DSA_Q_RANK, HEAD_DIM, HEAD_DIM, IDX_DIM, IDX_HEADS,
            WIDTH, WIDTH, WIDTH, N_HEADS,
            WIDTH, WIDTH, WIDTH, WIDTH, N_HEADS, N_HEADS)
IN_NAMES = ("r_q", "r_k", "r_v", "r_g", "d_cq", "d_k", "d_v", "i_k", "i_w",
            "f_q", "f_k", "f_v", "f_f", "g_q", "g_k", "g_v", "g_z", "g_b", "g_a")
IN_PLAN = (("r_q", C_RQ), ("r_k", C_RK), ("r_v", C_RV), ("r_g", C_RG),
           ("f_q", C_FQ), ("f_k", C_FK), ("f_v", C_FV),
           ("g_q", C_GQ), ("g_k", C_GK), ("g_v", C_GV), ("g_z", C_GZ),
           ("d_k", C_DK), ("d_cq", C_DCQ), ("d_v", C_DV),
           ("i_k", C_KA), ("i_k", C_KB + IDX_DIM),
           ("i_w", C_SM + L_IW), ("f_f", C_SM + L_FF), ("g_b", C_SM + L_GB), ("g_a", C_SM + L_GA))


def _prep_w_in_kernel(w_ref, m_ref, g_ref):
    src = {}
    off = 0
    for name, size in zip(IN_NAMES, IN_SIZES):
        src[name] = (off, size)
        off += size
    m_ref[...] = jnp.zeros_like(m_ref)
    for name, dst in IN_PLAN:
        so, w = src[name]
        m_ref[:, dst:dst + w] = w_ref[:, so:so + w].astype(BF16)
    g_ref[...] = w_ref[:, off:off + g_ref.shape[1]].astype(BF16)


def prep_w_in(w_in, layer, *, tr=128):
    _, d, c = w_in.shape
    return pl.pallas_call(
        _prep_w_in_kernel,
        grid=(d // tr,),
        in_specs=[pl.BlockSpec((None, tr, c), lambda i: (layer, i, 0))],
        out_specs=[pl.BlockSpec((tr, C_TOT), lambda i: (i, 0)),
                   pl.BlockSpec((tr, N_BRANCH * d), lambda i: (i, 0))],
        out_shape=[jax.ShapeDtypeStruct((d, C_TOT), BF16),
                   jax.ShapeDtypeStruct((d, N_BRANCH * d), BF16)],
        compiler_params=_cparams(("arbitrary",)),
        name="prep_w_in",
    )(w_in)


def _cast_kernel(w_ref, o_ref):
    o_ref[...] = w_ref[...].astype(BF16)


def cast_layer(w, layer, *, tr=256):
    _, r, c = w.shape
    if r % tr:
        tr = r
    return pl.pallas_call(
        _cast_kernel,
        grid=(r // tr,),
        in_specs=[pl.BlockSpec((None, tr, c), lambda i: (layer, i, 0))],
        out_specs=pl.BlockSpec((tr, c), lambda i: (i, 0)),
        out_shape=jax.ShapeDtypeStruct((r, c), BF16),
        compiler_params=_cparams(("arbitrary",)),
        name="cast_bf16",
    )(w)


def cast_branch(w_branch, layer):
    _, nbr, r, c = w_branch.shape
    return pl.pallas_call(
        _cast_kernel,
        grid=(nbr,),
        in_specs=[pl.BlockSpec((None, None, r, c), lambda i: (layer, i, 0, 0))],
        out_specs=pl.BlockSpec((None, r, c), lambda i: (i, 0, 0)),
        out_shape=jax.ShapeDtypeStruct((nbr, r, c), BF16),
        compiler_params=_cparams(("arbitrary",)),
        name="cast_branch",
    )(w_branch)


def kernel(x, norm_mix, w_in, dsa_cq_norm, dsa_w_uq, dsa_w_qidx, fox_f_bias, gdn_conv, gdn_a_log,
           gdn_dt_bias, gdn_norm, w_branch, w_out, rel_bias, norm_ffn, ffn_w_gate, ffn_w_up,
           ffn_conv, ffn_conv_b, ffn_w_down, final_norm):
    batch, seq, d = x.shape
    depth = w_in.shape[0]
    xf = x.reshape(batch * seq, d)
    ret_tables = _retention_tables(seq)
    for l in range(depth):
        w_main, w_gate = prep_w_in(w_in, l)
        proj, h = norm_proj(xf, norm_mix[l], w_main)
        par = jnp.zeros((8, LANE), F32)
        par = par.at[0, L_FF:L_FF + N_HEADS].set(fox_f_bias[l])
        par = par.at[0, L_GA:L_GA + N_HEADS].set(gdn_dt_bias[l])
        par = par.at[1, L_GA:L_GA + N_HEADS].set(gdn_a_log[l])
        prep_tok, prep_tr = prep_small(proj, par, batch, seq)
        o_ret = retention(proj, ret_tables, batch, seq)
        q_dsa, q_idx = dsa_proj(proj, dsa_cq_norm[l], cast_layer(dsa_w_uq, l), cast_layer(dsa_w_qidx, l))
        o_dsa = dsa_attention(proj, q_dsa, q_idx, prep_tok, rel_bias, batch, seq)
        o_fox = fox_attention(proj, prep_tr, batch, seq)
        o_gdn = gated_deltanet(proj, prep_tok, prep_tr, gdn_conv[l], gdn_norm[l], batch, seq)
        merged = merge_branches(h, (o_ret, o_dsa, o_fox, o_gdn), w_gate, cast_branch(w_branch, l))
        xf = resid_matmul(merged, cast_layer(w_out, l), xf, name="out_proj")
        act = conv_ffn_up(xf, norm_ffn[l], cast_layer(ffn_w_gate, l), cast_layer(ffn_w_up, l),
                          ffn_conv[l], ffn_conv_b[l], seq)
        xf = resid_matmul(act, cast_layer(ffn_w_down, l), xf, name="ffn_down")
    return rmsnorm(xf, final_norm).reshape(batch, seq, d)
```

```python
import functools
import math

import jax
import jax.numpy as jnp
from jax import lax
from jax.experimental import pallas as pl
from jax.experimental.pallas import tpu as pltpu

F32 = jnp.float32
BF16 = jnp.bfloat16
I32 = jnp.int32

HEAD_DIM = 128
N_HEADS = 4
WIDTH = N_HEADS * HEAD_DIM
N_BRANCH = 4
RET_CHUNK = 128
ROPE_BASE = 10000.0
DSA_Q_RANK = 384
IDX_HEADS = 16
IDX_DIM = 64
TOPK_MAX = 256
GDN_CONV = 4
GDN_CHUNK = 64
REL_BUCKETS = 32
REL_MAX_DIST = 128
FFN_CONV = 3
EPS = 1e-6

LANE = 128
SUBLANE = 8
VMEM_LIMIT = 56 * 1024 * 1024

C_RQ, C_RK, C_RV, C_RG = 0, 512, 1024, 1536
C_FQ, C_FK, C_FV = 2048, 2560, 3072
C_GQ, C_GK, C_GV, C_GZ = 3584, 4096, 4608, 5120
C_DK, C_DCQ, C_DV, C_KA, C_KB, C_SM = 5632, 5760, 6144, 6272, 6400, 6528
C_TOT = 6656
L_IW, L_FF, L_GB, L_GA = 0, 16, 20, 24

LOG2E = 1.4426950408889634
INT_MIN = -(2 ** 31)
INT_MAX = 2 ** 31 - 1
HIGHEST = lax.Precision.HIGHEST


def _cparams(sem, vmem=VMEM_LIMIT):
    return pltpu.CompilerParams(dimension_semantics=sem, vmem_limit_bytes=vmem)


def _dot(a, b):
    return jnp.dot(a, b, preferred_element_type=F32)


def _dot_nt(a, b):
    return lax.dot_general(a, b, (((1,), (1,)), ((), ())), preferred_element_type=F32)


def _silu(x):
    return x * jax.nn.sigmoid(x)


NORM_ROWS = 32


def _rmsnorm_rows(x_ref, g_ref, rows):
    x = x_ref[rows, :]
    ms = jnp.mean(x * x, axis=-1, keepdims=True)
    return x * lax.rsqrt(ms + EPS) * g_ref[...]


def _norm_proj_kernel(x_ref, g_ref, w_ref, o_ref, h_ref, h_scr, *, tm):
    @pl.when(pl.program_id(1) == 0)
    def _():
        def body(r, carry):
            rows = pl.ds(pl.multiple_of(r * NORM_ROWS, NORM_ROWS), NORM_ROWS)
            hb = _rmsnorm_rows(x_ref, g_ref, rows).astype(BF16)
            h_scr[rows, :] = hb
            h_ref[rows, :] = hb
            return carry
        lax.fori_loop(0, tm // NORM_ROWS, body, 0)

    o_ref[...] = _dot(h_scr[...], w_ref[...])


def norm_proj(x, gain, w, *, tm=512, tn=1664):
    n, d = x.shape
    c = w.shape[1]
    return pl.pallas_call(
        functools.partial(_norm_proj_kernel, tm=tm),
        grid=(n // tm, c // tn),
        in_specs=[pl.BlockSpec((tm, d), lambda i, j: (i, 0)),
                  pl.BlockSpec((1, d), lambda i, j: (0, 0)),
                  pl.BlockSpec((d, tn), lambda i, j: (0, j))],
        out_specs=[pl.BlockSpec((tm, tn), lambda i, j: (i, j)),
                   pl.BlockSpec((tm, d), lambda i, j: (i, 0))],
        out_shape=[jax.ShapeDtypeStruct((n, c), F32),
                   jax.ShapeDtypeStruct((n, d), BF16)],
        scratch_shapes=[pltpu.VMEM((tm, d), BF16)],
        compiler_params=_cparams(("arbitrary", "arbitrary")),
        name="norm_proj",
    )(x, gain.reshape(1, d), w)


def _rmsnorm_kernel(x_ref, g_ref, o_ref, *, tm):
    def body(r, carry):
        rows = pl.ds(pl.multiple_of(r * NORM_ROWS, NORM_ROWS), NORM_ROWS)
        o_ref[rows, :] = _rmsnorm_rows(x_ref, g_ref, rows)
        return carry
    lax.fori_loop(0, tm // NORM_ROWS, body, 0)


def rmsnorm(x, gain, *, tm=512):
    n, d = x.shape
    return pl.pallas_call(
        functools.partial(_rmsnorm_kernel, tm=tm),
        grid=(n // tm,),
        in_specs=[pl.BlockSpec((tm, d), lambda i: (i, 0)),
                  pl.BlockSpec((1, d), lambda i: (0, 0))],
        out_specs=pl.BlockSpec((tm, d), lambda i: (i, 0)),
        out_shape=jax.ShapeDtypeStruct((n, d), F32),
        compiler_params=_cparams(("arbitrary",)),
        name="final_rmsnorm",
    )(x, gain.reshape(1, d))


def _prep_kernel(s_ref, par_ref, tok_ref, tr_ref, carry_scr):
    @pl.when(pl.program_id(1) == 0)
    def _():
        carry_scr[...] = jnp.zeros_like(carry_scr)

    s = s_ref[...]
    lane = lax.broadcasted_iota(I32, (LANE, LANE), 1)
    row = lax.broadcasted_iota(I32, (LANE, LANE), 0)
    z = s + par_ref[0:1, :]
    soft = jnp.maximum(z, 0.0) + jnp.log1p(jnp.exp(-jnp.abs(z)))
    log_sig = z - soft
    sig = jax.nn.sigmoid(z)
    g_val = -jnp.exp(par_ref[1:2, :]) * soft
    is_f = (lane >= L_FF) & (lane < L_FF + N_HEADS)
    is_b = (lane >= L_GB) & (lane < L_GB + N_HEADS)
    is_a = (lane >= L_GA) & (lane < L_GA + N_HEADS)
    pre = jnp.where(is_f, log_sig, jnp.where(is_a, g_val, 0.0))
    tri = (row >= lane).astype(F32)
    tri_blk = ((row >= lane) & ((row // GDN_CHUNK) == (lane // GDN_CHUNK))).astype(F32)
    cum_full = jnp.dot(tri, pre, precision=HIGHEST, preferred_element_type=F32)
    cum_blk = jnp.dot(tri_blk, pre, precision=HIGHEST, preferred_element_type=F32)
    c_fox = cum_full + carry_scr[0:1, :]
    carry_scr[0:1, :] = c_fox[LANE - 1:LANE, :]
    scale_iw = IDX_HEADS ** -0.5 * IDX_DIM ** -0.5
    out = jnp.where(is_f, c_fox,
                    jnp.where(is_a, cum_blk,
                              jnp.where(is_b, sig,
                                        jnp.where(lane < IDX_HEADS, s * scale_iw, 0.0))))
    tok_ref[...] = out
    tr_ref[0] = out.T[0:32, :]


def prep_small(proj, par, batch, seq):
    n = proj.shape[0]
    nc = seq // LANE
    return pl.pallas_call(
        _prep_kernel,
        grid=(batch, nc),
        in_specs=[pl.BlockSpec((LANE, LANE), lambda b, c: (b * nc + c, C_SM // LANE)),
                  pl.BlockSpec((8, LANE), lambda b, c: (0, 0))],
        out_specs=[pl.BlockSpec((LANE, LANE), lambda b, c: (b * nc + c, 0)),
                   pl.BlockSpec((1, 32, LANE), lambda b, c: (b, 0, c))],
        out_shape=[jax.ShapeDtypeStruct((n, LANE), F32),
                   jax.ShapeDtypeStruct((batch, 32, seq), F32)],
        scratch_shapes=[pltpu.VMEM((8, LANE), F32)],
        compiler_params=_cparams(("arbitrary", "arbitrary")),
        name="prep_small",
    )(proj, par)


def _ret_gamma():
    return [math.log1p(-(2.0 ** (-5.0 - h))) for h in range(N_HEADS)]


def _retention_kernel(q_ref, k_ref, v_ref, g_ref, cos_ref, sin_ref, dec_ref, zeta_ref, xi_ref,
                      o_ref, state_scr):
    @pl.when(pl.program_id(1) == 0)
    def _():
        state_scr[...] = jnp.zeros_like(state_scr)

    cos_t = cos_ref[...]
    sin_t = sin_ref[...]
    log_gamma = _ret_gamma()
    heads = range(N_HEADS)
    hsl = [slice(h * HEAD_DIM, (h + 1) * HEAD_DIM) for h in heads]

    def rope(x):
        return x * cos_t + pltpu.roll(x, HEAD_DIM // 2, 1) * sin_t

    qb = [rope(q_ref[:, s]).astype(BF16) for s in hsl]
    kr = [rope(k_ref[:, s]) * (HEAD_DIM ** -0.5) for s in hsl]
    kb = [x.astype(BF16) for x in kr]
    vb = [v_ref[:, s].astype(BF16) for s in hsl]
    st = [state_scr[h] for h in heads]
    inner = [(_dot_nt(qb[h], kb[h]) * dec_ref[h]).astype(BF16) for h in heads]
    cross = [_dot(qb[h], st[h].astype(BF16)) * xi_ref[h] for h in heads]
    kv = [_dot((kr[h] * zeta_ref[h]).T.astype(BF16), vb[h]) for h in heads]
    o = [_dot(inner[h], vb[h]) + cross[h] for h in heads]
    for h in heads:
        state_scr[h] = st[h] * math.exp(log_gamma[h] * RET_CHUNK) + kv[h]
        mu = jnp.mean(o[h], axis=-1, keepdims=True)
        oc = o[h] - mu
        var = jnp.mean(oc * oc, axis=-1, keepdims=True)
        o_ref[:, hsl[h]] = (_silu(g_ref[:, hsl[h]]) * (oc * lax.rsqrt(var + EPS))).astype(BF16)


def _retention_tables(seq):
    half = HEAD_DIM // 2
    inv = 1.0 / (ROPE_BASE ** (jnp.arange(half, dtype=F32) / half))
    ang = jnp.arange(seq).astype(F32)[:, None] * inv[None, :]
    cos, sin = jnp.cos(ang), jnp.sin(ang)
    cos_t = jnp.concatenate([cos, cos], axis=-1)
    sin_t = jnp.concatenate([-sin, sin], axis=-1)
    c = RET_CHUNK
    log_gamma = jnp.log1p(-jnp.exp2(-5.0 - jnp.arange(N_HEADS, dtype=F32)))
    n = jnp.arange(c, dtype=F32)
    diff = n[:, None] - n[None, :]
    decay = jnp.where(diff >= 0, jnp.exp(log_gamma[:, None, None] * jnp.maximum(diff, 0.0)), 0.0)
    zeta = jnp.exp(log_gamma[:, None] * (c - 1 - n)[None, :])
    xi = jnp.exp(log_gamma[:, None] * (n + 1)[None, :])
    ones = jnp.ones((1, 1, HEAD_DIM), F32)
    return cos_t, sin_t, decay, zeta[:, :, None] * ones, xi[:, :, None] * ones


def retention(proj, tables, batch, seq):
    n = proj.shape[0]
    c = RET_CHUNK
    nc = seq // c
    cos_t, sin_t, decay, zeta, xi = tables
    col = lambda off: pl.BlockSpec((c, WIDTH), lambda b, i: (b * nc + i, off // WIDTH))
    full3 = pl.BlockSpec((N_HEADS, c, HEAD_DIM), lambda b, i: (0, 0, 0))
    return pl.pallas_call(
        _retention_kernel,
        grid=(batch, nc),
        in_specs=[col(C_RQ), col(C_RK), col(C_RV), col(C_RG),
                  pl.BlockSpec((c, HEAD_DIM), lambda b, i: (i, 0)),
                  pl.BlockSpec((c, HEAD_DIM), lambda b, i: (i, 0)),
                  full3, full3, full3],
        out_specs=pl.BlockSpec((c, WIDTH), lambda b, i: (b * nc + i, 0)),
        out_shape=jax.ShapeDtypeStruct((n, WIDTH), BF16),
        scratch_shapes=[pltpu.VMEM((N_HEADS, HEAD_DIM, HEAD_DIM), F32)],
        compiler_params=_cparams(("arbitrary", "arbitrary")),
        name="retention",
    )(proj, proj, proj, proj, cos_t, sin_t, decay, zeta, xi)


def _fox_kernel(q_ref, k_ref, v_ref, ctr_ref, o_ref, m_scr, acc_scr, *, t):
    qi = pl.program_id(1)
    ki = pl.program_id(2)

    @pl.when(ki == 0)
    def _():
        m_scr[...] = jnp.full_like(m_scr, -jnp.inf)
        acc_scr[...] = jnp.zeros_like(acc_scr)

    def step(masked):
        if masked:
            row = lax.broadcasted_iota(I32, (t, t), 0)
            colm = lax.broadcasted_iota(I32, (t, t), 1)
            keep = row >= colm
        ones = jnp.ones((t, HEAD_DIM), BF16)
        for h in range(N_HEADS):
            sl = slice(h * HEAD_DIM, (h + 1) * HEAD_DIM)
            qb = q_ref[:, sl].astype(BF16)
            kb = k_ref[:, sl].astype(BF16)
            c_k = ctr_ref[0, L_FF + h:L_FF + h + 1, :] * LOG2E
            s = _dot_nt(qb, kb) * (HEAD_DIM ** -0.5 * LOG2E) - c_k
            if masked:
                s = jnp.where(keep, s, -jnp.inf)
            m_old = m_scr[h]
            m_new = jnp.maximum(m_old, jnp.max(s, axis=-1, keepdims=True))
            alpha = jnp.exp2(m_old - m_new)
            p = jnp.exp2(s - m_new)
            v_aug = jnp.concatenate([v_ref[:, sl].astype(BF16), ones], axis=1)
            acc_scr[h] = alpha * acc_scr[h] + _dot(p.astype(BF16), v_aug)
            m_scr[h] = m_new

    @pl.when(ki < qi)
    def _():
        step(False)

    @pl.when(ki == qi)
    def _():
        step(True)
        for h in range(N_HEADS):
            sl = slice(h * HEAD_DIM, (h + 1) * HEAD_DIM)
            acc = acc_scr[h]
            o_ref[:, sl] = (acc[:, :HEAD_DIM] / acc[:, HEAD_DIM:]).astype(BF16)


def fox_attention(proj, prep_tr, batch, seq, *, t=512):
    n = proj.shape[0]
    nt = seq // t
    qspec = pl.BlockSpec((t, WIDTH), lambda b, qi, ki: (b * nt + qi, C_FQ // WIDTH))
    kspec = lambda off: pl.BlockSpec(
        (t, WIDTH), lambda b, qi, ki: (b * nt + jnp.minimum(ki, qi), off // WIDTH))
    return pl.pallas_call(
        functools.partial(_fox_kernel, t=t),
        grid=(batch, nt, nt),
        in_specs=[qspec, kspec(C_FK), kspec(C_FV),
                  pl.BlockSpec((1, 32, t), lambda b, qi, ki: (b, 0, jnp.minimum(ki, qi)))],
        out_specs=pl.BlockSpec((t, WIDTH), lambda b, qi, ki: (b * nt + qi, 0)),
        out_shape=jax.ShapeDtypeStruct((n, WIDTH), BF16),
        scratch_shapes=[pltpu.VMEM((N_HEADS, t, 1), F32),
                        pltpu.VMEM((N_HEADS, t, 2 * HEAD_DIM), F32)],
        compiler_params=_cparams(("arbitrary", "arbitrary", "arbitrary")),
        name="fox_attention",
    )(proj, proj, proj, prep_tr)


def _dsa_proj_kernel(cq_ref, g_ref, wq_ref, wi_ref, q_ref, qi_ref):
    x = cq_ref[...]
    ms = jnp.mean(x * x, axis=-1, keepdims=True)
    cb = (x * lax.rsqrt(ms + EPS) * g_ref[...]).astype(BF16)
    q_ref[...] = _dot(cb, wq_ref[...]).astype(BF16)
    qi_ref[...] = _dot(cb, wi_ref[...]).astype(BF16)


def dsa_proj(proj, cq_norm, w_uq, w_qidx, *, tm=512):
    n = proj.shape[0]
    r = DSA_Q_RANK
    wi = IDX_HEADS * IDX_DIM
    return pl.pallas_call(
        _dsa_proj_kernel,
        grid=(n // tm,),
        in_specs=[pl.BlockSpec((tm, r), lambda i: (i, C_DCQ // r)),
                  pl.BlockSpec((1, r), lambda i: (0, 0)),
                  pl.BlockSpec((r, WIDTH), lambda i: (0, 0)),
                  pl.BlockSpec((r, wi), lambda i: (0, 0))],
        out_specs=[pl.BlockSpec((tm, WIDTH), lambda i: (i, 0)),
                   pl.BlockSpec((tm, wi), lambda i: (i, 0))],
        out_shape=[jax.ShapeDtypeStruct((n, WIDTH), BF16),
                   jax.ShapeDtypeStruct((n, wi), BF16)],
        compiler_params=_cparams(("arbitrary",)),
        name="dsa_proj",
    )(proj, cq_norm.reshape(1, r), w_uq, w_qidx)


DSA_QB = 256
DSA_KC = 512
DSA_SCORE_MID_STEPS = 20
DSA_FEW_KEYS = 4
DSA_HALVE_FIXED = 12
DSA_WALK_FIXED = 3


def _t5_bucket(rel):
    max_exact = REL_BUCKETS // 2
    relf = jnp.maximum(rel, max_exact).astype(F32)
    large = max_exact + (jnp.log(relf / max_exact) / math.log(REL_MAX_DIST / max_exact)
                         * (REL_BUCKETS - max_exact)).astype(I32)
    large = jnp.minimum(large, REL_BUCKETS - 1)
    return jnp.where(rel < max_exact, rel, large)


def _dsa_kernel(rb_ref, q_ref, qi_ref, tok_ref, k_ref, v_ref, ka_ref, kb_ref, o_ref,
                key_scr, lg_scr, band_scr, kb16_scr, vt_scr, ka16_scr, kb16i_scr, *, seq, topk):
    qb_idx = pl.program_id(1)
    t0 = qb_idx * DSA_QB
    n_kc = (t0 + DSA_QB - 1) // DSA_KC + 1
    row_vec = (1, DSA_QB)

    @pl.when(qb_idx == 0)
    def _():
        kb16_scr[...] = k_ref[...].astype(BF16)
        ka16_scr[...] = ka_ref[...].astype(BF16)
        kb16i_scr[...] = kb_ref[...].astype(BF16)
        for c in range(seq // DSA_KC):
            cs = slice(c * DSA_KC, (c + 1) * DSA_KC)
            vt_scr[0:HEAD_DIM, cs] = v_ref[cs, :].T.astype(BF16)
        vt_scr[HEAD_DIM:, :] = jnp.ones((HEAD_DIM, seq), BF16)

    @pl.when((pl.program_id(0) == 0) & (qb_idx == 0))
    def _():
        j_ = lax.broadcasted_iota(I32, (2 * DSA_QB, DSA_QB), 0)
        i_ = lax.broadcasted_iota(I32, (2 * DSA_QB, DSA_QB), 1)
        rel = i_ + DSA_QB - j_
        bucket = _t5_bucket(rel)
        for h in range(N_HEADS):
            far = rb_ref[REL_BUCKETS - 1, h]
            band = jnp.zeros((2 * DSA_QB, DSA_QB), F32)
            for bk in range(REL_BUCKETS - 1):
                band = jnp.where(bucket == bk, (rb_ref[bk, h] - far) * LOG2E, band)
            band_scr[h] = jnp.where(rel >= 0, band, 0.0)

    w_t = tok_ref[...].T
    key_s = lax.broadcasted_iota(I32, (DSA_KC, DSA_QB), 0)
    row_t = t0 + lax.broadcasted_iota(I32, (DSA_KC, DSA_QB), 1)

    def score_chunk(c, carry):
        kmax, kmin = carry
        ks = pl.ds(pl.multiple_of(c * DSA_KC, DSA_KC), DSA_KC)
        ka = ka16_scr[ks, :]
        kb = kb16i_scr[ks, :]
        acc = jnp.zeros((DSA_KC, DSA_QB), F32)
        for p in range(IDX_HEADS // 2):
            qp = qi_ref[:, p * LANE:(p + 1) * LANE]
            acc = acc + jnp.maximum(_dot_nt(ka, qp), 0.0) * w_t[2 * p:2 * p + 1, :]
            acc = acc + jnp.maximum(_dot_nt(kb, qp), 0.0) * w_t[2 * p + 1:2 * p + 2, :]
        bits = pltpu.bitcast(acc, I32)
        key = bits ^ ((bits >> 31) & 0x7FFFFFFF)
        valid = (c * DSA_KC + key_s) <= row_t
        key_scr[ks, :] = jnp.where(valid, key, INT_MIN)
        kmax = jnp.maximum(kmax, jnp.max(jnp.where(valid, key, INT_MIN), axis=0, keepdims=True))
        kmin = jnp.minimum(kmin, jnp.min(jnp.where(valid, key, INT_MAX), axis=0, keepdims=True))
        return kmax, kmin

    kmax, kmin = lax.fori_loop(0, n_kc, score_chunk, (jnp.full(row_vec, INT_MIN, I32),
                                                     jnp.full(row_vec, INT_MAX, I32)))

    def scan_keys(cand, with_below):
        def body(c, carry):
            cnt, below = carry
            ks = pl.ds(pl.multiple_of(c * DSA_KC, DSA_KC), DSA_KC)
            keys = key_scr[ks, :]
            ge = keys >= cand
            ones = ge.astype(I32)
            low = jnp.where(ge, INT_MIN, keys)
            for u in range(DSA_KC // SUBLANE):
                us = slice(u * SUBLANE, (u + 1) * SUBLANE)
                cnt = cnt + ones[us, :]
                if with_below:
                    below = jnp.maximum(below, low[us, :])
            return cnt, below
        cnt, below = lax.fori_loop(0, n_kc, body, (jnp.zeros((SUBLANE, DSA_QB), I32),
                                                   jnp.full((SUBLANE, DSA_QB), INT_MIN, I32)))
        cnt = jnp.sum(cnt, axis=0, keepdims=True)
        if with_below:
            return cnt, jnp.max(below, axis=0, keepdims=True)
        return cnt

    def key_to_score(k):
        return pltpu.bitcast(k ^ ((k >> 31) & 0x7FFFFFFF), F32)

    def score_to_key(s):
        b = pltpu.bitcast(s, I32)
        return b ^ ((b >> 31) & 0x7FFFFFFF)

    def open_rows(lo, hi, c_lo):
        return (c_lo > topk) & (hi - 1 > lo)

    def any_row(flag):
        return jnp.max(jnp.where(flag, 1, 0))

    def update(cand, cnt, lo, hi, c_lo, c_hi):
        ge = cnt >= topk
        return (jnp.where(ge, cand, lo), jnp.where(ge, hi, cand),
                jnp.where(ge, cnt, c_lo), jnp.where(ge, c_hi, cnt))

    def crowded(lo, hi, c_lo, c_hi):
        return any_row(open_rows(lo, hi, c_lo) & (c_lo - c_hi > DSA_FEW_KEYS))

    def halve_step(it, lo, hi, c_lo, c_hi):
        key_mid = (lo >> 1) + (hi >> 1) + (lo & hi & 1)
        score_mid = score_to_key(0.5 * key_to_score(lo) + 0.5 * key_to_score(hi - 1))
        cand = jnp.where(it < DSA_SCORE_MID_STEPS, score_mid, key_mid)
        cand = jnp.minimum(jnp.maximum(cand, lo + 1), hi - 1)
        cand = jnp.where(hi - 1 > lo, cand, lo)
        return update(cand, scan_keys(cand, False), lo, hi, c_lo, c_hi)

    def halve_body(st):
        it, _, lo, hi, c_lo, c_hi = st
        go = crowded(lo, hi, c_lo, c_hi)
        return (it + 1, go) + halve_step(it, lo, hi, c_lo, c_hi)

    def walk_step(lo, hi, c_lo, c_hi, nxt):
        is_open = open_rows(lo, hi, c_lo)
        cand = jnp.where(is_open, nxt, lo)
        cnt, below = scan_keys(cand, True)
        ge = cnt >= topk
        hi = jnp.where(is_open, jnp.where(ge, cand + 1, cand), hi)
        c_hi = jnp.where(is_open & jnp.logical_not(ge), cnt, c_hi)
        lo = jnp.where(is_open & ge, cand, lo)
        c_lo = jnp.where(is_open & ge, cnt, c_lo)
        nxt = jnp.where(ge, nxt, below)
        return lo, hi, c_lo, c_hi, nxt

    def walk_body(st):
        go = any_row(open_rows(st[1], st[2], st[3]))
        return (go,) + walk_step(*st[1:])

    n_valid = jnp.minimum(t0 + lax.broadcasted_iota(I32, row_vec, 1) + 1, seq)
    st = (kmin, kmax + 1, n_valid, jnp.zeros(row_vec, I32))
    st = lax.fori_loop(0, DSA_HALVE_FIXED, lambda it, s: halve_step(it, *s), st)
    st = lax.while_loop(lambda s: s[1] > 0, halve_body,
                        (jnp.int32(DSA_HALVE_FIXED), crowded(*st)) + st)[2:]
    _, nxt0 = scan_keys(st[1], True)
    st = lax.fori_loop(0, DSA_WALK_FIXED, lambda it, s: walk_step(*s), st + (nxt0,))
    _, thr, hi, n_ge, n_gt, _ = lax.while_loop(
        lambda s: s[0] > 0, walk_body, (any_row(open_rows(st[0], st[1], st[2])),) + st)

    tied = n_ge > topk
    has_tie = jnp.max(jnp.where(tied, 1, 0)) > 0

    @pl.when(has_tie)
    def _():
        room = (topk - n_gt).astype(F32)
        ii = lax.broadcasted_iota(I32, (LANE, LANE), 0)
        jj = lax.broadcasted_iota(I32, (LANE, LANE), 1)
        lower = (ii >= jj).astype(BF16)

        def body(c, seen):
            ks = pl.ds(pl.multiple_of(c * LANE, LANE), LANE)
            kk = key_scr[ks, :]
            eq = kk == thr
            rank = seen + _dot(lower, eq.astype(BF16))
            drop = eq & (rank > room) & tied
            key_scr[ks, :] = jnp.where(drop, INT_MIN, kk)
            return seen + jnp.sum(eq.astype(F32), axis=0, keepdims=True)
        lax.fori_loop(0, n_kc * (DSA_KC // LANE), body, jnp.zeros(row_vec, F32))

    def mask_chunk(c, carry):
        ks = pl.ds(pl.multiple_of(c * DSA_KC, DSA_KC), DSA_KC)
        sel = jnp.where(key_scr[ks, :] >= thr, 0.0, -jnp.inf).astype(F32)
        key_scr[ks, :] = pltpu.bitcast(sel, I32)
        return carry
    lax.fori_loop(0, n_kc, mask_chunk, 0)

    heads = range(N_HEADS)
    hsl = [slice(h * HEAD_DIM, (h + 1) * HEAD_DIM) for h in heads]

    def logit_chunk(c, ms):
        ks = pl.ds(pl.multiple_of(c * DSA_KC, DSA_KC), DSA_KC)
        k_c = kb16_scr[ks, :]
        sel = pltpu.bitcast(key_scr[ks, :], F32)
        out = []
        for h in heads:
            s = _dot_nt(k_c, q_ref[:, hsl[h]]) * (HEAD_DIM ** -0.5 * LOG2E) + sel
            lg_scr[h, ks, :] = s
            out.append(jnp.maximum(ms[h], jnp.max(s, axis=0, keepdims=True)))
        return tuple(out)
    ms = lax.fori_loop(0, n_kc, logit_chunk,
                       tuple(jnp.full(row_vec, -jnp.inf, F32) for _ in heads))

    band_off = pl.multiple_of(jnp.maximum(qb_idx - 1, 0) * DSA_QB, DSA_QB)
    ws = pl.ds(band_off, 2 * DSA_QB)
    ms = list(ms)
    for h in heads:
        band_h = band_scr[h]
        band_first = jnp.concatenate([band_h[DSA_QB:, :], jnp.zeros((DSA_QB, DSA_QB), F32)], axis=0)
        win = lg_scr[h, ws, :] + jnp.where(qb_idx == 0, band_first, band_h)
        lg_scr[h, ws, :] = win
        ms[h] = jnp.maximum(ms[h], jnp.max(win, axis=0, keepdims=True))

    def pv_chunk(c, accs):
        ks = pl.ds(pl.multiple_of(c * DSA_KC, DSA_KC), DSA_KC)
        vt_c = vt_scr[:, ks]
        return tuple(accs[h] + _dot(vt_c, jnp.exp2(lg_scr[h, ks, :] - ms[h]).astype(BF16))
                     for h in heads)
    accs = lax.fori_loop(0, n_kc, pv_chunk,
                         tuple(jnp.zeros((2 * HEAD_DIM, DSA_QB), F32) for _ in heads))
    for h in heads:
        o_ref[:, hsl[h]] = (accs[h][:HEAD_DIM, :] / accs[h][HEAD_DIM:, :]).T.astype(BF16)


def dsa_attention(proj, q, q_idx, prep_tok, rel_bias, batch, seq):
    n = proj.shape[0]
    nq = seq // DSA_QB
    topk = min(TOPK_MAX, seq // 4)
    wi = IDX_HEADS * IDX_DIM
    rowblk = lambda w, cb: pl.BlockSpec((DSA_QB, w), lambda b, i: (b * nq + i, cb))
    seqblk = lambda off: pl.BlockSpec((seq, LANE), lambda b, i: (b, off // LANE))
    return pl.pallas_call(
        functools.partial(_dsa_kernel, seq=seq, topk=topk),
        grid=(batch, nq),
        in_specs=[pl.BlockSpec(memory_space=pltpu.SMEM),
                  rowblk(WIDTH, 0), rowblk(wi, 0), rowblk(LANE, 0),
                  seqblk(C_DK), seqblk(C_DV), seqblk(C_KA), seqblk(C_KB)],
        out_specs=pl.BlockSpec((DSA_QB, WIDTH), lambda b, i: (b * nq + i, 0)),
        out_shape=jax.ShapeDtypeStruct((n, WIDTH), BF16),
        scratch_shapes=[pltpu.VMEM((seq, DSA_QB), I32),
                        pltpu.VMEM((N_HEADS, seq, DSA_QB), F32),
                        pltpu.VMEM((N_HEADS, 2 * DSA_QB, DSA_QB), F32),
                        pltpu.VMEM((seq, LANE), BF16),
                        pltpu.VMEM((2 * HEAD_DIM, seq), BF16),
                        pltpu.VMEM((seq, LANE), BF16),
                        pltpu.VMEM((seq, LANE), BF16)],
        compiler_params=_cparams(("arbitrary", "arbitrary")),
        name="dsa_attention",
    )(rel_bias, q, q_idx, prep_tok, proj, proj, proj, proj)


GDN_T = 256
GDN_GROUP = 2
GDN_HALO = 8


def _gdn_kernel(q_ref, k_ref, v_ref, z_ref, cw_ref, ng_ref, tok_ref, tr_ref, o_ref,
                xq_scr, xk_scr, xv_scr, state_scr):
    first = pl.program_id(1) == 0

    @pl.when(first)
    def _():
        state_scr[...] = jnp.zeros_like(state_scr)
        for scr in (xq_scr, xk_scr, xv_scr):
            scr[0:GDN_HALO, :] = jnp.zeros((GDN_HALO, WIDTH), F32)

    def conv(x_ref, scr, w_off):
        scr[GDN_HALO:, :] = x_ref[...]
        y = jnp.zeros((GDN_T, WIDTH), F32)
        for i in range(GDN_CONV):
            st = GDN_HALO - (GDN_CONV - 1) + i
            y = y + scr[st:st + GDN_T, :] * cw_ref[i:i + 1, w_off:w_off + WIDTH]
        scr[0:GDN_HALO, :] = scr[GDN_T:GDN_T + GDN_HALO, :]
        return _silu(y)

    qc = conv(q_ref, xq_scr, 0)
    kc = conv(k_ref, xk_scr, WIDTH)
    vc = conv(v_ref, xv_scr, 2 * WIDTH)
    tok = tok_ref[...]
    c = GDN_CHUNK
    heads = range(N_HEADS)
    hsl = [slice(h * HEAD_DIM, (h + 1) * HEAD_DIM) for h in heads]

    def l2norm_heads(x, scale):
        return jnp.concatenate(
            [x[:, s] * (lax.rsqrt(jnp.sum(x[:, s] * x[:, s], axis=-1, keepdims=True) + EPS) * scale)
             for s in hsl], axis=1)

    qf = l2norm_heads(qc, HEAD_DIM ** -0.5)
    kf = l2norm_heads(kc, 1.0)

    grp = GDN_GROUP
    gw = grp * HEAD_DIM
    nb = grp * c
    ri = lax.broadcasted_iota(I32, (nb, nb), 0)
    ci = lax.broadcasted_iota(I32, (nb, nb), 1)
    tril = ((ri // c) == (ci // c)) & (ri >= ci)
    eye = (ri == ci).astype(F32)
    pair_masks = []
    for lg in range(c.bit_length() - 1):
        pair_masks.append(((ri >> (lg + 1)) == (ci >> (lg + 1)))
                          & (((ri >> lg) & 1) == 1) & (((ci >> lg) & 1) == 0))
    lane_head = lax.broadcasted_iota(I32, (c, gw), 1) // HEAD_DIM
    row_head = lax.broadcasted_iota(I32, (nb, HEAD_DIM), 0) // c

    def spread(x):
        return jnp.concatenate([jnp.where(lane_head == u, x, 0.0) for u in range(grp)], axis=0)

    def stack(x):
        return jnp.concatenate([x[:, hsl[u]] for u in range(grp)], axis=0)

    def spread_lanes(x):
        return jnp.concatenate([jnp.where(row_head == u, x, 0.0) for u in range(grp)], axis=1)

    def split(x):
        hi = x.astype(BF16)
        return hi, (x - hi.astype(F32)).astype(BF16)

    def dot_split(a, b):
        return _dot(a[0], b[0]) + (_dot(a[0], b[1]) + _dot(a[1], b[0]))

    items = [(j, gi) for j in range(GDN_T // c) for gi in range(N_HEADS // grp)]
    pre = []
    for j, gi in items:
        rs = slice(j * c, (j + 1) * c)
        last = slice((j + 1) * c - 1, (j + 1) * c)
        gh = [gi * grp + u for u in range(grp)]
        gsl = slice(gi * gw, (gi + 1) * gw)
        qj, kj, vj = qf[rs, gsl], kf[rs, gsl], vc[rs, gsl]
        b_col = jnp.concatenate([tok[rs, L_GB + h:L_GB + h + 1] for h in gh], axis=0)
        g_col = jnp.concatenate([tok[rs, L_GA + h:L_GA + h + 1] for h in gh], axis=0)
        g_row = jnp.concatenate([tr_ref[0, L_GA + h:L_GA + h + 1, rs] for h in gh], axis=1)
        g_last = [tok[last, L_GA + h:L_GA + h + 1] for h in gh]
        g_last_col = jnp.concatenate([jnp.broadcast_to(g, (c, 1)) for g in g_last], axis=0)
        k_sp = spread(kj)
        q_sp = spread(qj)
        k_sp16 = k_sp.astype(BF16)
        decay = jnp.exp(jnp.where(tril, g_col - g_row, -jnp.inf))
        eg = jnp.exp(g_col)
        pre.append(dict(
            gh=gh, gsl=gsl,
            l_mat=b_col * _dot_nt(k_sp16, k_sp16) * decay,
            rhs=jnp.concatenate([stack(vj) * b_col, stack(kj) * (b_col * eg)], axis=1),
            qk=_dot_nt(q_sp.astype(BF16), k_sp16) * decay,
            q_dec=q_sp * eg,
            k_dec=k_sp * jnp.exp(g_last_col - g_col),
            e_last=jnp.concatenate([jnp.broadcast_to(jnp.exp(g), (HEAD_DIM, 1)) for g in g_last], axis=0)))

    t_inv = [eye - jnp.where(pair_masks[0], p["l_mat"], 0.0) for p in pre]
    for pm in pair_masks[1:]:
        t_s = [split(t) for t in t_inv]
        m_t = [dot_split(split(jnp.where(pm, p["l_mat"], 0.0)), ts) for p, ts in zip(pre, t_s)]
        t_inv = [t - dot_split(ts, split(m)) for t, ts, m in zip(t_inv, t_s, m_t)]
    sols = [dot_split(split(t), split(p["rhs"])) for t, p in zip(t_inv, pre)]

    outs = [[] for _ in heads]
    for p, sol in zip(pre, sols):
        u0 = sol[:, :HEAD_DIM]
        kcum = sol[:, HEAD_DIM:]
        st = state_scr[p["gsl"], :]
        stb = st.astype(BF16)
        v_new = u0 - _dot(spread_lanes(kcum).astype(BF16), stb)
        v_new_b = v_new.astype(BF16)
        o_st = _dot(p["q_dec"].astype(BF16), stb) + _dot(p["qk"].astype(BF16), v_new_b)
        state_scr[p["gsl"], :] = st * p["e_last"] + _dot(p["k_dec"].T.astype(BF16), v_new_b)
        for u, h in enumerate(p["gh"]):
            outs[h].append(o_st[u * c:(u + 1) * c, :])

    for h in heads:
        o = jnp.concatenate(outs[h], axis=0)
        ms = jnp.mean(o * o, axis=-1, keepdims=True)
        on = o * lax.rsqrt(ms + EPS) * ng_ref[...]
        o_ref[:, hsl[h]] = (on * _silu(z_ref[:, hsl[h]])).astype(BF16)


def gated_deltanet(proj, prep_tok, prep_tr, conv_w, norm_g, batch, seq):
    n = proj.shape[0]
    t = GDN_T
    nt = seq // t
    col = lambda off: pl.BlockSpec((t, WIDTH), lambda b, i: (b * nt + i, off // WIDTH))
    return pl.pallas_call(
        _gdn_kernel,
        grid=(batch, nt),
        in_specs=[col(C_GQ), col(C_GK), col(C_GV), col(C_GZ),
                  pl.BlockSpec((GDN_CONV, 3 * WIDTH), lambda b, i: (0, 0)),
                  pl.BlockSpec((1, HEAD_DIM), lambda b, i: (0, 0)),
                  pl.BlockSpec((t, LANE), lambda b, i: (b * nt + i, 0)),
                  pl.BlockSpec((1, 32, t), lambda b, i: (b, 0, i))],
        out_specs=pl.BlockSpec((t, WIDTH), lambda b, i: (b * nt + i, 0)),
        out_shape=jax.ShapeDtypeStruct((n, WIDTH), BF16),
        scratch_shapes=[pltpu.VMEM((t + GDN_HALO, WIDTH), F32),
                        pltpu.VMEM((t + GDN_HALO, WIDTH), F32),
                        pltpu.VMEM((t + GDN_HALO, WIDTH), F32),
                        pltpu.VMEM((N_HEADS * HEAD_DIM, HEAD_DIM), F32)],
        compiler_params=_cparams(("arbitrary", "arbitrary")),
        name="gated_deltanet",
    )(proj, proj, proj, proj, conv_w, norm_g.reshape(1, HEAD_DIM), prep_tok, prep_tr)


def _merge_kernel(h_ref, b0_ref, b1_ref, b2_ref, b3_ref, g0_ref, g1_ref, g2_ref, g3_ref,
                  wb_ref, o_ref):
    h = h_ref[...]
    acc = None
    for n, (b_ref, g_ref) in enumerate(zip((b0_ref, b1_ref, b2_ref, b3_ref),
                                           (g0_ref, g1_ref, g2_ref, g3_ref))):
        gate = jax.nn.sigmoid(_dot(h, g_ref[...]))
        term = gate * _dot(b_ref[...], wb_ref[n])
        acc = term if acc is None else acc + term
    o_ref[...] = acc.astype(BF16)


def merge_branches(h, branches, w_gate, w_branch, *, tm=512, tn=512):
    n, d = h.shape
    nj = d // tn
    bspec = pl.BlockSpec((tm, WIDTH), lambda j, i: (i, 0))
    gspec = lambda k: pl.BlockSpec((d, tn), lambda j, i: (0, k * nj + j))
    return pl.pallas_call(
        _merge_kernel,
        grid=(nj, n // tm),
        in_specs=[pl.BlockSpec((tm, d), lambda j, i: (i, 0)),
                  bspec, bspec, bspec, bspec,
                  gspec(0), gspec(1), gspec(2), gspec(3),
                  pl.BlockSpec((N_BRANCH, WIDTH, tn), lambda j, i: (0, 0, j))],
        out_specs=pl.BlockSpec((tm, tn), lambda j, i: (i, j)),
        out_shape=jax.ShapeDtypeStruct((n, d), BF16),
        compiler_params=_cparams(("arbitrary", "arbitrary")),
        name="merge_branches",
    )(h, *branches, w_gate, w_gate, w_gate, w_gate, w_branch)


def _resid_mm_kernel(a_ref, w_ref, x_ref, o_ref):
    o_ref[...] = x_ref[...] + _dot(a_ref[...], w_ref[...])


def resid_matmul(a, w, x, *, tm=512, tn=1024, name="resid_matmul"):
    n, k = a.shape
    d = w.shape[1]
    return pl.pallas_call(
        _resid_mm_kernel,
        grid=(d // tn, n // tm),
        in_specs=[pl.BlockSpec((tm, k), lambda j, i: (i, 0)),
                  pl.BlockSpec((k, tn), lambda j, i: (0, j)),
                  pl.BlockSpec((tm, tn), lambda j, i: (i, j))],
        out_specs=pl.BlockSpec((tm, tn), lambda j, i: (i, j)),
        out_shape=jax.ShapeDtypeStruct((n, d), F32),
        compiler_params=_cparams(("arbitrary", "arbitrary")),
        name=name,
    )(a, w, x)


FFN_HALO = 8
FFN_SUB = 512


def _ffn1_kernel(x_ref, g_ref, wg_ref, wu_ref, cw_ref, cb_ref, o_ref, h_scr, gt_scr, halo_scr,
                 *, tm, tiles_per_seq):
    i = pl.program_id(0)
    j = pl.program_id(1)

    @pl.when((i == 0) & (j == 0))
    def _():
        halo_scr[...] = jnp.zeros_like(halo_scr)

    @pl.when(j == 0)
    def _():
        def body(r, carry):
            rows = pl.ds(pl.multiple_of(r * NORM_ROWS, NORM_ROWS), NORM_ROWS)
            h_scr[rows, :] = _rmsnorm_rows(x_ref, g_ref, rows).astype(BF16)
            return carry
        lax.fori_loop(0, tm // NORM_ROWS, body, 0)

    h = h_scr[...]
    seq_start = (i % tiles_per_seq) == 0
    tn = o_ref.shape[1]
    for off in range(0, tn, FFN_SUB):
        cs = slice(off, min(off + FFN_SUB, tn))
        g = _dot(h, wg_ref[:, cs])
        gt_scr[FFN_HALO:, cs] = g
        gt_scr[0:FFN_HALO, cs] = jnp.where(seq_start, 0.0, halo_scr[j, :, cs])
        halo_scr[j, :, cs] = g[tm - FFN_HALO:, :]
        y = cb_ref[:, cs] + g * cw_ref[FFN_CONV - 1:FFN_CONV, cs]
        for t in range(FFN_CONV - 1):
            st = FFN_HALO - (FFN_CONV - 1) + t
            y = y + gt_scr[st:st + tm, cs] * cw_ref[t:t + 1, cs]
        o_ref[:, cs] = (_silu(y) * _dot(h, wu_ref[:, cs])).astype(BF16)


def conv_ffn_up(x, gain, w_gate, w_up, conv_w, conv_b, seq, *, tm=512, tn=1408):
    n, d = x.shape
    f = w_gate.shape[1]
    nj = f // tn
    return pl.pallas_call(
        functools.partial(_ffn1_kernel, tm=tm, tiles_per_seq=seq // tm),
        grid=(n // tm, nj),
        in_specs=[pl.BlockSpec((tm, d), lambda i, j: (i, 0)),
                  pl.BlockSpec((1, d), lambda i, j: (0, 0)),
                  pl.BlockSpec((d, tn), lambda i, j: (0, j)),
                  pl.BlockSpec((d, tn), lambda i, j: (0, j)),
                  pl.BlockSpec((FFN_CONV, tn), lambda i, j: (0, j)),
                  pl.BlockSpec((1, tn), lambda i, j: (0, j))],
        out_specs=pl.BlockSpec((tm, tn), lambda i, j: (i, j)),
        out_shape=jax.ShapeDtypeStruct((n, f), BF16),
        scratch_shapes=[pltpu.VMEM((tm, d), BF16),
                        pltpu.VMEM((tm + FFN_HALO, tn), F32),
                        pltpu.VMEM((nj, FFN_HALO, tn), F32)],
        compiler_params=_cparams(("arbitrary", "arbitrary")),
        name="conv_ffn_up",
    )(x, gain.reshape(1, d), w_gate, w_up, conv_w, conv_b.reshape(1, f))


IN_SIZES = (WIDTH, WIDTH, WIDTH, WIDTH,
            DSA_Q_RANK, HEAD_DIM, HEAD_DIM, IDX_DIM, IDX_HEADS,
            WIDTH, WIDTH, WIDTH, N_HEADS,
            WIDTH, WIDTH, WIDTH, WIDTH, N_HEADS, N_HEADS)
IN_NAMES = ("r_q", "r_k", "r_v", "r_g", "d_cq", "d_k", "d_v", "i_k", "i_w",
            "f_q", "f_k", "f_v", "f_f", "g_q", "g_k", "g_v", "g_z", "g_b", "g_a")
IN_PLAN = (("r_q", C_RQ), ("r_k", C_RK), ("r_v", C_RV), ("r_g", C_RG),
           ("f_q", C_FQ), ("f_k", C_FK), ("f_v", C_FV),
           ("g_q", C_GQ), ("g_k", C_GK), ("g_v", C_GV), ("g_z", C_GZ),
           ("d_k", C_DK), ("d_cq", C_DCQ), ("d_v", C_DV),
           ("i_k", C_KA), ("i_k", C_KB + IDX_DIM),
           ("i_w", C_SM + L_IW), ("f_f", C_SM + L_FF), ("g_b", C_SM + L_GB), ("g_a", C_SM + L_GA))


def _prep_w_in_kernel(w_ref, m_ref, g_ref):
    src = {}
    off = 0
    for name, size in zip(IN_NAMES, IN_SIZES):
        src[name] = (off, size)
        off += size
    m_ref[...] = jnp.zeros_like(m_ref)
    for name, dst in IN_PLAN:
        so, w = src[name]
        m_ref[:, dst:dst + w] = w_ref[:, so:so + w].astype(BF16)
    g_ref[...] = w_ref[:, off:off + g_ref.shape[1]].astype(BF16)


def prep_w_in(w_in, layer, *, tr=128):
    _, d, c = w_in.shape
    return pl.pallas_call(
        _prep_w_in_kernel,
        grid=(d // tr,),
        in_specs=[pl.BlockSpec((None, tr, c), lambda i: (layer, i, 0))],
        out_specs=[pl.BlockSpec((tr, C_TOT), lambda i: (i, 0)),
                   pl.BlockSpec((tr, N_BRANCH * d), lambda i: (i, 0))],
        out_shape=[jax.ShapeDtypeStruct((d, C_TOT), BF16),
                   jax.ShapeDtypeStruct((d, N_BRANCH * d), BF16)],
        compiler_params=_cparams(("arbitrary",)),
        name="prep_w_in",
    )(w_in)


def _cast_kernel(w_ref, o_ref):
    o_ref[...] = w_ref[...].astype(BF16)


def cast_layer(w, layer, *, tr=256):
    _, r, c = w.shape
    if r % tr:
        tr = r
    return pl.pallas_call(
        _cast_kernel,
        grid=(r // tr,),
        in_specs=[pl.BlockSpec((None, tr, c), lambda i: (layer, i, 0))],
        out_specs=pl.BlockSpec((tr, c), lambda i: (i, 0)),
        out_shape=jax.ShapeDtypeStruct((r, c), BF16),
        compiler_params=_cparams(("arbitrary",)),
        name="cast_bf16",
    )(w)


def cast_branch(w_branch, layer):
    _, nbr, r, c = w_branch.shape
    return pl.pallas_call(
        _cast_kernel,
        grid=(nbr,),
        in_specs=[pl.BlockSpec((None, None, r, c), lambda i: (layer, i, 0, 0))],
        out_specs=pl.BlockSpec((None, r, c), lambda i: (i, 0, 0)),
        out_shape=jax.ShapeDtypeStruct((nbr, r, c), BF16),
        compiler_params=_cparams(("arbitrary",)),
        name="cast_branch",
    )(w_branch)


def kernel(x, norm_mix, w_in, dsa_cq_norm, dsa_w_uq, dsa_w_qidx, fox_f_bias, gdn_conv, gdn_a_log,
           gdn_dt_bias, gdn_norm, w_branch, w_out, rel_bias, norm_ffn, ffn_w_gate, ffn_w_up,
           ffn_conv, ffn_conv_b, ffn_w_down, final_norm):
    batch, seq, d = x.shape
    depth = w_in.shape[0]
    xf = x.reshape(batch * seq, d)
    ret_tables = _retention_tables(seq)
    for l in range(depth):
        w_main, w_gate = prep_w_in(w_in, l)
        proj, h = norm_proj(xf, norm_mix[l], w_main)
        par = jnp.zeros((8, LANE), F32)
        par = par.at[0, L_FF:L_FF + N_HEADS].set(fox_f_bias[l])
        par = par.at[0, L_GA:L_GA + N_HEADS].set(gdn_dt_bias[l])
        par = par.at[1, L_GA:L_GA + N_HEADS].set(gdn_a_log[l])
        prep_tok, prep_tr = prep_small(proj, par, batch, seq)
        o_ret = retention(proj, ret_tables, batch, seq)
        q_dsa, q_idx = dsa_proj(proj, dsa_cq_norm[l], cast_layer(dsa_w_uq, l), cast_layer(dsa_w_qidx, l))
        o_dsa = dsa_attention(proj, q_dsa, q_idx, prep_tok, rel_bias, batch, seq)
        o_fox = fox_attention(proj, prep_tr, batch, seq)
        o_gdn = gated_deltanet(proj, prep_tok, prep_tr, gdn_conv[l], gdn_norm[l], batch, seq)
        merged = merge_branches(h, (o_ret, o_dsa, o_fox, o_gdn), w_gate, cast_branch(w_branch, l))
        xf = resid_matmul(merged, cast_layer(w_out, l), xf, name="out_proj")
        act = conv_ffn_up(xf, norm_ffn[l], cast_layer(ffn_w_gate, l), cast_layer(ffn_w_up, l),
                          ffn_conv[l], ffn_conv_b[l], seq)
        xf = resid_matmul(act, cast_layer(ffn_w_down, l), xf, name="ffn_down")
    return rmsnorm(xf, final_norm).reshape(batch, seq, d)
```

```python
import functools
import math

import jax
import jax.numpy as jnp
from jax import lax
from jax.experimental import pallas as pl
from jax.experimental.pallas import tpu as pltpu

F32 = jnp.float32
BF16 = jnp.bfloat16
I32 = jnp.int32

HEAD_DIM = 128
N_HEADS = 4
WIDTH = N_HEADS * HEAD_DIM
N_BRANCH = 4
RET_CHUNK = 128
ROPE_BASE = 10000.0
DSA_Q_RANK = 384
IDX_HEADS = 16
IDX_DIM = 64
TOPK_MAX = 256
GDN_CONV = 4
GDN_CHUNK = 64
REL_BUCKETS = 32
REL_MAX_DIST = 128
FFN_CONV = 3
EPS = 1e-6

LANE = 128
SUBLANE = 8
VMEM_LIMIT = 56 * 1024 * 1024

C_RQ, C_RK, C_RV, C_RG = 0, 512, 1024, 1536
C_FQ, C_FK, C_FV = 2048, 2560, 3072
C_GQ, C_GK, C_GV, C_GZ = 3584, 4096, 4608, 5120
C_DK, C_DCQ, C_DV, C_KA, C_KB, C_SM = 5632, 5760, 6144, 6272, 6400, 6528
C_TOT = 6656
L_IW, L_FF, L_GB, L_GA = 0, 16, 20, 24

LOG2E = 1.4426950408889634
INT_MIN = -(2 ** 31)
INT_MAX = 2 ** 31 - 1
HIGHEST = lax.Precision.HIGHEST


def _cparams(sem, vmem=VMEM_LIMIT):
    return pltpu.CompilerParams(dimension_semantics=sem, vmem_limit_bytes=vmem)


def _dot(a, b):
    return jnp.dot(a, b, preferred_element_type=F32)


def _dot_nt(a, b):
    return lax.dot_general(a, b, (((1,), (1,)), ((), ())), preferred_element_type=F32)


def _silu(x):
    return x * jax.nn.sigmoid(x)


NORM_ROWS = 32


def _rmsnorm_rows(x_ref, g_ref, rows):
    x = x_ref[rows, :]
    ms = jnp.mean(x * x, axis=-1, keepdims=True)
    return x * lax.rsqrt(ms + EPS) * g_ref[...]


def _norm_proj_kernel(x_ref, g_ref, w_ref, o_ref, h_ref, h_scr, *, tm):
    @pl.when(pl.program_id(1) == 0)
    def _():
        def body(r, carry):
            rows = pl.ds(pl.multiple_of(r * NORM_ROWS, NORM_ROWS), NORM_ROWS)
            hb = _rmsnorm_rows(x_ref, g_ref, rows).astype(BF16)
            h_scr[rows, :] = hb
            h_ref[rows, :] = hb
            return carry
        lax.fori_loop(0, tm // NORM_ROWS, body, 0)

    o_ref[...] = _dot(h_scr[...], w_ref[...])


def norm_proj(x, gain, w, *, tm=512, tn=1664):
    n, d = x.shape
    c = w.shape[1]
    return pl.pallas_call(
        functools.partial(_norm_proj_kernel, tm=tm),
        grid=(n // tm, c // tn),
        in_specs=[pl.BlockSpec((tm, d), lambda i, j: (i, 0)),
                  pl.BlockSpec((1, d), lambda i, j: (0, 0)),
                  pl.BlockSpec((d, tn), lambda i, j: (0, j))],
        out_specs=[pl.BlockSpec((tm, tn), lambda i, j: (i, j)),
                   pl.BlockSpec((tm, d), lambda i, j: (i, 0))],
        out_shape=[jax.ShapeDtypeStruct((n, c), F32),
                   jax.ShapeDtypeStruct((n, d), BF16)],
        scratch_shapes=[pltpu.VMEM((tm, d), BF16)],
        compiler_params=_cparams(("arbitrary", "arbitrary")),
        name="norm_proj",
    )(x, gain.reshape(1, d), w)


def _rmsnorm_kernel(x_ref, g_ref, o_ref, *, tm):
    def body(r, carry):
        rows = pl.ds(pl.multiple_of(r * NORM_ROWS, NORM_ROWS), NORM_ROWS)
        o_ref[rows, :] = _rmsnorm_rows(x_ref, g_ref, rows)
        return carry
    lax.fori_loop(0, tm // NORM_ROWS, body, 0)


def rmsnorm(x, gain, *, tm=512):
    n, d = x.shape
    return pl.pallas_call(
        functools.partial(_rmsnorm_kernel, tm=tm),
        grid=(n // tm,),
        in_specs=[pl.BlockSpec((tm, d), lambda i: (i, 0)),
                  pl.BlockSpec((1, d), lambda i: (0, 0))],
        out_specs=pl.BlockSpec((tm, d), lambda i: (i, 0)),
        out_shape=jax.ShapeDtypeStruct((n, d), F32),
        compiler_params=_cparams(("arbitrary",)),
        name="final_rmsnorm",
    )(x, gain.reshape(1, d))


def _prep_kernel(s_ref, par_ref, tok_ref, tr_ref, carry_scr):
    @pl.when(pl.program_id(1) == 0)
    def _():
        carry_scr[...] = jnp.zeros_like(carry_scr)

    s = s_ref[...]
    lane = lax.broadcasted_iota(I32, (LANE, LANE), 1)
    row = lax.broadcasted_iota(I32, (LANE, LANE), 0)
    z = s + par_ref[0:1, :]
    soft = jnp.maximum(z, 0.0) + jnp.log1p(jnp.exp(-jnp.abs(z)))
    log_sig = z - soft
    sig = jax.nn.sigmoid(z)
    g_val = -jnp.exp(par_ref[1:2, :]) * soft
    is_f = (lane >= L_FF) & (lane < L_FF + N_HEADS)
    is_b = (lane >= L_GB) & (lane < L_GB + N_HEADS)
    is_a = (lane >= L_GA) & (lane < L_GA + N_HEADS)
    pre = jnp.where(is_f, log_sig, jnp.where(is_a, g_val, 0.0))
    tri = (row >= lane).astype(F32)
    tri_blk = ((row >= lane) & ((row // GDN_CHUNK) == (lane // GDN_CHUNK))).astype(F32)
    cum_full = jnp.dot(tri, pre, precision=HIGHEST, preferred_element_type=F32)
    cum_blk = jnp.dot(tri_blk, pre, precision=HIGHEST, preferred_element_type=F32)
    c_fox = cum_full + carry_scr[0:1, :]
    carry_scr[0:1, :] = c_fox[LANE - 1:LANE, :]
    scale_iw = IDX_HEADS ** -0.5 * IDX_DIM ** -0.5
    out = jnp.where(is_f, c_fox,
                    jnp.where(is_a, cum_blk,
                              jnp.where(is_b, sig,
                                        jnp.where(lane < IDX_HEADS, s * scale_iw, 0.0))))
    tok_ref[...] = out
    tr_ref[0] = out.T[0:32, :]


def prep_small(proj, par, batch, seq):
    n = proj.shape[0]
    nc = seq // LANE
    return pl.pallas_call(
        _prep_kernel,
        grid=(batch, nc),
        in_specs=[pl.BlockSpec((LANE, LANE), lambda b, c: (b * nc + c, C_SM // LANE)),
                  pl.BlockSpec((8, LANE), lambda b, c: (0, 0))],
        out_specs=[pl.BlockSpec((LANE, LANE), lambda b, c: (b * nc + c, 0)),
                   pl.BlockSpec((1, 32, LANE), lambda b, c: (b, 0, c))],
        out_shape=[jax.ShapeDtypeStruct((n, LANE), F32),
                   jax.ShapeDtypeStruct((batch, 32, seq), F32)],
        scratch_shapes=[pltpu.VMEM((8, LANE), F32)],
        compiler_params=_cparams(("arbitrary", "arbitrary")),
        name="prep_small",
    )(proj, par)


def _ret_gamma():
    return [math.log1p(-(2.0 ** (-5.0 - h))) for h in range(N_HEADS)]


def _retention_kernel(q_ref, k_ref, v_ref, g_ref, cos_ref, sin_ref, dec_ref, zeta_ref, xi_ref,
                      o_ref, state_scr):
    @pl.when(pl.program_id(1) == 0)
    def _():
        state_scr[...] = jnp.zeros_like(state_scr)

    cos_t = cos_ref[...]
    sin_t = sin_ref[...]
    log_gamma = _ret_gamma()
    heads = range(N_HEADS)
    hsl = [slice(h * HEAD_DIM, (h + 1) * HEAD_DIM) for h in heads]

    def rope(x):
        return x * cos_t + pltpu.roll(x, HEAD_DIM // 2, 1) * sin_t

    qb = [rope(q_ref[:, s]).astype(BF16) for s in hsl]
    kr = [rope(k_ref[:, s]) * (HEAD_DIM ** -0.5) for s in hsl]
    kb = [x.astype(BF16) for x in kr]
    vb = [v_ref[:, s].astype(BF16) for s in hsl]
    st = [state_scr[h] for h in heads]
    inner = [(_dot_nt(qb[h], kb[h]) * dec_ref[h]).astype(BF16) for h in heads]
    cross = [_dot(qb[h], st[h].astype(BF16)) * xi_ref[h] for h in heads]
    kv = [_dot((kr[h] * zeta_ref[h]).T.astype(BF16), vb[h]) for h in heads]
    o = [_dot(inner[h], vb[h]) + cross[h] for h in heads]
    for h in heads:
        state_scr[h] = st[h] * math.exp(log_gamma[h] * RET_CHUNK) + kv[h]
        mu = jnp.mean(o[h], axis=-1, keepdims=True)
        oc = o[h] - mu
        var = jnp.mean(oc * oc, axis=-1, keepdims=True)
        o_ref[:, hsl[h]] = (_silu(g_ref[:, hsl[h]]) * (oc * lax.rsqrt(var + EPS))).astype(BF16)


def _retention_tables(seq):
    half = HEAD_DIM // 2
    inv = 1.0 / (ROPE_BASE ** (jnp.arange(half, dtype=F32) / half))
    ang = jnp.arange(seq).astype(F32)[:, None] * inv[None, :]
    cos, sin = jnp.cos(ang), jnp.sin(ang)
    cos_t = jnp.concatenate([cos, cos], axis=-1)
    sin_t = jnp.concatenate([-sin, sin], axis=-1)
    c = RET_CHUNK
    log_gamma = jnp.log1p(-jnp.exp2(-5.0 - jnp.arange(N_HEADS, dtype=F32)))
    n = jnp.arange(c, dtype=F32)
    diff = n[:, None] - n[None, :]
    decay = jnp.where(diff >= 0, jnp.exp(log_gamma[:, None, None] * jnp.maximum(diff, 0.0)), 0.0)
    zeta = jnp.exp(log_gamma[:, None] * (c - 1 - n)[None, :])
    xi = jnp.exp(log_gamma[:, None] * (n + 1)[None, :])
    ones = jnp.ones((1, 1, HEAD_DIM), F32)
    return cos_t, sin_t, decay, zeta[:, :, None] * ones, xi[:, :, None] * ones


def retention(proj, tables, batch, seq):
    n = proj.shape[0]
    c = RET_CHUNK
    nc = seq // c
    cos_t, sin_t, decay, zeta, xi = tables
    col = lambda off: pl.BlockSpec((c, WIDTH), lambda b, i: (b * nc + i, off // WIDTH))
    full3 = pl.BlockSpec((N_HEADS, c, HEAD_DIM), lambda b, i: (0, 0, 0))
    return pl.pallas_call(
        _retention_kernel,
        grid=(batch, nc),
        in_specs=[col(C_RQ), col(C_RK), col(C_RV), col(C_RG),
                  pl.BlockSpec((c, HEAD_DIM), lambda b, i: (i, 0)),
                  pl.BlockSpec((c, HEAD_DIM), lambda b, i: (i, 0)),
                  full3, full3, full3],
        out_specs=pl.BlockSpec((c, WIDTH), lambda b, i: (b * nc + i, 0)),
        out_shape=jax.ShapeDtypeStruct((n, WIDTH), BF16),
        scratch_shapes=[pltpu.VMEM((N_HEADS, HEAD_DIM, HEAD_DIM), F32)],
        compiler_params=_cparams(("arbitrary", "arbitrary")),
        name="retention",
    )(proj, proj, proj, proj, cos_t, sin_t, decay, zeta, xi)


def _fox_kernel(qi_ref, ki_ref, q_ref, k_ref, v_ref, ctr_ref, o_ref, m_scr, acc_scr, *, t):
    qi = qi_ref[pl.program_id(1)]
    ki = ki_ref[pl.program_id(1)]

    @pl.when(ki == 0)
    def _():
        m_scr[...] = jnp.full_like(m_scr, -jnp.inf)
        acc_scr[...] = jnp.zeros_like(acc_scr)

    def step(masked):
        if masked:
            row = lax.broadcasted_iota(I32, (t, t), 0)
            colm = lax.broadcasted_iota(I32, (t, t), 1)
            keep = row >= colm
        ones = jnp.ones((t, HEAD_DIM), BF16)
        for h in range(N_HEADS):
            sl = slice(h * HEAD_DIM, (h + 1) * HEAD_DIM)
            qb = q_ref[:, sl].astype(BF16)
            kb = k_ref[:, sl].astype(BF16)
            c_k = ctr_ref[0, L_FF + h:L_FF + h + 1, :] * LOG2E
            s = _dot_nt(qb, kb) * (HEAD_DIM ** -0.5 * LOG2E) - c_k
            if masked:
                s = jnp.where(keep, s, -jnp.inf)
            m_old = m_scr[h]
            m_new = jnp.maximum(m_old, jnp.max(s, axis=-1, keepdims=True))
            alpha = jnp.exp2(m_old - m_new)
            p = jnp.exp2(s - m_new)
            v_aug = jnp.concatenate([v_ref[:, sl].astype(BF16), ones], axis=1)
            acc_scr[h] = alpha * acc_scr[h] + _dot(p.astype(BF16), v_aug)
            m_scr[h] = m_new

    @pl.when(ki < qi)
    def _():
        step(False)

    @pl.when(ki == qi)
    def _():
        step(True)
        for h in range(N_HEADS):
            sl = slice(h * HEAD_DIM, (h + 1) * HEAD_DIM)
            acc = acc_scr[h]
            o_ref[:, sl] = (acc[:, :HEAD_DIM] / acc[:, HEAD_DIM:]).astype(BF16)


def fox_attention(proj, prep_tr, batch, seq, *, t=512):
    n = proj.shape[0]
    nt = seq // t
    pairs = [(qi, ki) for qi in range(nt) for ki in range(qi + 1)]
    qi_arr = jnp.asarray([p[0] for p in pairs], I32)
    ki_arr = jnp.asarray([p[1] for p in pairs], I32)
    qspec = pl.BlockSpec((t, WIDTH), lambda b, s, qi, ki: (b * nt + qi[s], C_FQ // WIDTH))
    kspec = lambda off: pl.BlockSpec(
        (t, WIDTH), lambda b, s, qi, ki: (b * nt + ki[s], off // WIDTH))
    return pl.pallas_call(
        functools.partial(_fox_kernel, t=t),
        grid_spec=pltpu.PrefetchScalarGridSpec(
            num_scalar_prefetch=2,
            grid=(batch, len(pairs)),
            in_specs=[qspec, kspec(C_FK), kspec(C_FV),
                      pl.BlockSpec((1, 32, t), lambda b, s, qi, ki: (b, 0, ki[s]))],
            out_specs=pl.BlockSpec((t, WIDTH), lambda b, s, qi, ki: (b * nt + qi[s], 0)),
            scratch_shapes=[pltpu.VMEM((N_HEADS, t, 1), F32),
                            pltpu.VMEM((N_HEADS, t, 2 * HEAD_DIM), F32)]),
        out_shape=jax.ShapeDtypeStruct((n, WIDTH), BF16),
        compiler_params=_cparams(("arbitrary", "arbitrary")),
        name="fox_attention",
    )(qi_arr, ki_arr, proj, proj, proj, prep_tr)


def _dsa_proj_kernel(cq_ref, g_ref, wq_ref, wi_ref, q_ref, qi_ref):
    x = cq_ref[...]
    ms = jnp.mean(x * x, axis=-1, keepdims=True)
    cb = (x * lax.rsqrt(ms + EPS) * g_ref[...]).astype(BF16)
    q_ref[...] = _dot(cb, wq_ref[...]).astype(BF16)
    qi_ref[...] = _dot(cb, wi_ref[...]).astype(BF16)


def dsa_proj(proj, cq_norm, w_uq, w_qidx, *, tm=512):
    n = proj.shape[0]
    r = DSA_Q_RANK
    wi = IDX_HEADS * IDX_DIM
    return pl.pallas_call(
        _dsa_proj_kernel,
        grid=(n // tm,),
        in_specs=[pl.BlockSpec((tm, r), lambda i: (i, C_DCQ // r)),
                  pl.BlockSpec((1, r), lambda i: (0, 0)),
                  pl.BlockSpec((r, WIDTH), lambda i: (0, 0)),
                  pl.BlockSpec((r, wi), lambda i: (0, 0))],
        out_specs=[pl.BlockSpec((tm, WIDTH), lambda i: (i, 0)),
                   pl.BlockSpec((tm, wi), lambda i: (i, 0))],
        out_shape=[jax.ShapeDtypeStruct((n, WIDTH), BF16),
                   jax.ShapeDtypeStruct((n, wi), BF16)],
        compiler_params=_cparams(("arbitrary",)),
        name="dsa_proj",
    )(proj, cq_norm.reshape(1, r), w_uq, w_qidx)


DSA_QB = 256
DSA_KC = 512
DSA_SCORE_MID_STEPS = 20
DSA_FEW_KEYS = 4
DSA_HALVE_FIXED = 12
DSA_WALK_FIXED = 3


def _t5_bucket(rel):
    max_exact = REL_BUCKETS // 2
    relf = jnp.maximum(rel, max_exact).astype(F32)
    large = max_exact + (jnp.log(relf / max_exact) / math.log(REL_MAX_DIST / max_exact)
                         * (REL_BUCKETS - max_exact)).astype(I32)
    large = jnp.minimum(large, REL_BUCKETS - 1)
    return jnp.where(rel < max_exact, rel, large)


def _dsa_kernel(rb_ref, q_ref, qi_ref, tok_ref, k_ref, v_ref, ka_ref, kb_ref, o_ref,
                key_scr, lg_scr, band_scr, kb16_scr, vt_scr, ka16_scr, kb16i_scr, *, seq, topk):
    qb_idx = pl.program_id(1)
    t0 = qb_idx * DSA_QB
    n_kc = (t0 + DSA_QB - 1) // DSA_KC + 1
    row_vec = (1, DSA_QB)

    @pl.when(qb_idx == 0)
    def _():
        kb16_scr[...] = k_ref[...].astype(BF16)
        ka16_scr[...] = ka_ref[...].astype(BF16)
        kb16i_scr[...] = kb_ref[...].astype(BF16)
        for c in range(seq // DSA_KC):
            cs = slice(c * DSA_KC, (c + 1) * DSA_KC)
            vt_scr[0:HEAD_DIM, cs] = v_ref[cs, :].T.astype(BF16)
        vt_scr[HEAD_DIM:, :] = jnp.ones((HEAD_DIM, seq), BF16)

    @pl.when((pl.program_id(0) == 0) & (qb_idx == 0))
    def _():
        j_ = lax.broadcasted_iota(I32, (2 * DSA_QB, DSA_QB), 0)
        i_ = lax.broadcasted_iota(I32, (2 * DSA_QB, DSA_QB), 1)
        rel = i_ + DSA_QB - j_
        bucket = _t5_bucket(rel)
        for h in range(N_HEADS):
            far = rb_ref[REL_BUCKETS - 1, h]
            band = jnp.zeros((2 * DSA_QB, DSA_QB), F32)
            for bk in range(REL_BUCKETS - 1):
                band = jnp.where(bucket == bk, (rb_ref[bk, h] - far) * LOG2E, band)
            band_scr[h] = jnp.where(rel >= 0, band, 0.0)

    w_t = tok_ref[...].T
    key_s = lax.broadcasted_iota(I32, (DSA_KC, DSA_QB), 0)
    row_t = t0 + lax.broadcasted_iota(I32, (DSA_KC, DSA_QB), 1)

    def score_chunk(c, carry):
        kmax, kmin = carry
        ks = pl.ds(pl.multiple_of(c * DSA_KC, DSA_KC), DSA_KC)
        ka = ka16_scr[ks, :]
        kb = kb16i_scr[ks, :]
        acc = jnp.zeros((DSA_KC, DSA_QB), F32)
        for p in range(IDX_HEADS // 2):
            qp = qi_ref[:, p * LANE:(p + 1) * LANE]
            acc = acc + jnp.maximum(_dot_nt(ka, qp), 0.0) * w_t[2 * p:2 * p + 1, :]
            acc = acc + jnp.maximum(_dot_nt(kb, qp), 0.0) * w_t[2 * p + 1:2 * p + 2, :]
        bits = pltpu.bitcast(acc, I32)
        key = bits ^ ((bits >> 31) & 0x7FFFFFFF)
        valid = (c * DSA_KC + key_s) <= row_t
        key_scr[ks, :] = jnp.where(valid, key, INT_MIN)
        kmax = jnp.maximum(kmax, jnp.max(jnp.where(valid, key, INT_MIN), axis=0, keepdims=True))
        kmin = jnp.minimum(kmin, jnp.min(jnp.where(valid, key, INT_MAX), axis=0, keepdims=True))
        return kmax, kmin

    kmax, kmin = lax.fori_loop(0, n_kc, score_chunk, (jnp.full(row_vec, INT_MIN, I32),
                                                     jnp.full(row_vec, INT_MAX, I32)))

    def scan_keys(cand, with_below):
        def body(c, carry):
            cnt, below = carry
            ks = pl.ds(pl.multiple_of(c * DSA_KC, DSA_KC), DSA_KC)
            keys = key_scr[ks, :]
            ge = keys >= cand
            ones = ge.astype(I32)
            low = jnp.where(ge, INT_MIN, keys)
            for u in range(DSA_KC // SUBLANE):
                us = slice(u * SUBLANE, (u + 1) * SUBLANE)
                cnt = cnt + ones[us, :]
                if with_below:
                    below = jnp.maximum(below, low[us, :])
            return cnt, below
        cnt, below = lax.fori_loop(0, n_kc, body, (jnp.zeros((SUBLANE, DSA_QB), I32),
                                                   jnp.full((SUBLANE, DSA_QB), INT_MIN, I32)))
        cnt = jnp.sum(cnt, axis=0, keepdims=True)
        if with_below:
            return cnt, jnp.max(below, axis=0, keepdims=True)
        return cnt

    def key_to_score(k):
        return pltpu.bitcast(k ^ ((k >> 31) & 0x7FFFFFFF), F32)

    def score_to_key(s):
        b = pltpu.bitcast(s, I32)
        return b ^ ((b >> 31) & 0x7FFFFFFF)

    def open_rows(lo, hi, c_lo):
        return (c_lo > topk) & (hi - 1 > lo)

    def any_row(flag):
        return jnp.max(jnp.where(flag, 1, 0))

    def update(cand, cnt, lo, hi, c_lo, c_hi):
        ge = cnt >= topk
        return (jnp.where(ge, cand, lo), jnp.where(ge, hi, cand),
                jnp.where(ge, cnt, c_lo), jnp.where(ge, c_hi, cnt))

    def crowded(lo, hi, c_lo, c_hi):
        return any_row(open_rows(lo, hi, c_lo) & (c_lo - c_hi > DSA_FEW_KEYS))

    def halve_step(it, lo, hi, c_lo, c_hi):
        key_mid = (lo >> 1) + (hi >> 1) + (lo & hi & 1)
        score_mid = score_to_key(0.5 * key_to_score(lo) + 0.5 * key_to_score(hi - 1))
        cand = jnp.where(it < DSA_SCORE_MID_STEPS, score_mid, key_mid)
        cand = jnp.minimum(jnp.maximum(cand, lo + 1), hi - 1)
        cand = jnp.where(hi - 1 > lo, cand, lo)
        return update(cand, scan_keys(cand, False), lo, hi, c_lo, c_hi)

    def halve_body(st):
        it, _, lo, hi, c_lo, c_hi = st
        go = crowded(lo, hi, c_lo, c_hi)
        return (it + 1, go) + halve_step(it, lo, hi, c_lo, c_hi)

    def walk_step(lo, hi, c_lo, c_hi, nxt):
        is_open = open_rows(lo, hi, c_lo)
        cand = jnp.where(is_open, nxt, lo)
        cnt, below = scan_keys(cand, True)
        ge = cnt >= topk
        hi = jnp.where(is_open, jnp.where(ge, cand + 1, cand), hi)
        c_hi = jnp.where(is_open & jnp.logical_not(ge), cnt, c_hi)
        lo = jnp.where(is_open & ge, cand, lo)
        c_lo = jnp.where(is_open & ge, cnt, c_lo)
        nxt = jnp.where(ge, nxt, below)
        return lo, hi, c_lo, c_hi, nxt

    def walk_body(st):
        go = any_row(open_rows(st[1], st[2], st[3]))
        return (go,) + walk_step(*st[1:])

    n_valid = jnp.minimum(t0 + lax.broadcasted_iota(I32, row_vec, 1) + 1, seq)
    st = (kmin, kmax + 1, n_valid, jnp.zeros(row_vec, I32))
    st = lax.fori_loop(0, DSA_HALVE_FIXED, lambda it, s: halve_step(it, *s), st)
    st = lax.while_loop(lambda s: s[1] > 0, halve_body,
                        (jnp.int32(DSA_HALVE_FIXED), crowded(*st)) + st)[2:]
    _, nxt0 = scan_keys(st[1], True)
    st = lax.fori_loop(0, DSA_WALK_FIXED, lambda it, s: walk_step(*s), st + (nxt0,))
    _, thr, hi, n_ge, n_gt, _ = lax.while_loop(
        lambda s: s[0] > 0, walk_body, (any_row(open_rows(st[0], st[1], st[2])),) + st)

    tied = n_ge > topk
    has_tie = jnp.max(jnp.where(tied, 1, 0)) > 0

    @pl.when(has_tie)
    def _():
        room = (topk - n_gt).astype(F32)
        ii = lax.broadcasted_iota(I32, (LANE, LANE), 0)
        jj = lax.broadcasted_iota(I32, (LANE, LANE), 1)
        lower = (ii >= jj).astype(BF16)

        def body(c, seen):
            ks = pl.ds(pl.multiple_of(c * LANE, LANE), LANE)
            kk = key_scr[ks, :]
            eq = kk == thr
            rank = seen + _dot(lower, eq.astype(BF16))
            drop = eq & (rank > room) & tied
            key_scr[ks, :] = jnp.where(drop, INT_MIN, kk)
            return seen + jnp.sum(eq.astype(F32), axis=0, keepdims=True)
        lax.fori_loop(0, n_kc * (DSA_KC // LANE), body, jnp.zeros(row_vec, F32))

    def mask_chunk(c, carry):
        ks = pl.ds(pl.multiple_of(c * DSA_KC, DSA_KC), DSA_KC)
        sel = jnp.where(key_scr[ks, :] >= thr, 0.0, -jnp.inf).astype(F32)
        key_scr[ks, :] = pltpu.bitcast(sel, I32)
        return carry
    lax.fori_loop(0, n_kc, mask_chunk, 0)

    heads = range(N_HEADS)
    hsl = [slice(h * HEAD_DIM, (h + 1) * HEAD_DIM) for h in heads]

    def logit_chunk(c, ms):
        ks = pl.ds(pl.multiple_of(c * DSA_KC, DSA_KC), DSA_KC)
        k_c = kb16_scr[ks, :]
        sel = pltpu.bitcast(key_scr[ks, :], F32)
        out = []
        for h in heads:
            s = _dot_nt(k_c, q_ref[:, hsl[h]]) * (HEAD_DIM ** -0.5 * LOG2E) + sel
            lg_scr[h, ks, :] = s
            out.append(jnp.maximum(ms[h], jnp.max(s, axis=0, keepdims=True)))
        return tuple(out)
    ms = lax.fori_loop(0, n_kc, logit_chunk,
                       tuple(jnp.full(row_vec, -jnp.inf, F32) for _ in heads))

    band_off = pl.multiple_of(jnp.maximum(qb_idx - 1, 0) * DSA_QB, DSA_QB)
    ws = pl.ds(band_off, 2 * DSA_QB)
    ms = list(ms)
    for h in heads:
        band_h = band_scr[h]
        band_first = jnp.concatenate([band_h[DSA_QB:, :], jnp.zeros((DSA_QB, DSA_QB), F32)], axis=0)
        win = lg_scr[h, ws, :] + jnp.where(qb_idx == 0, band_first, band_h)
        lg_scr[h, ws, :] = win
        ms[h] = jnp.maximum(ms[h], jnp.max(win, axis=0, keepdims=True))

    def pv_chunk(c, accs):
        ks = pl.ds(pl.multiple_of(c * DSA_KC, DSA_KC), DSA_KC)
        vt_c = vt_scr[:, ks]
        return tuple(accs[h] + _dot(vt_c, jnp.exp2(lg_scr[h, ks, :] - ms[h]).astype(BF16))
                     for h in heads)
    accs = lax.fori_loop(0, n_kc, pv_chunk,
                         tuple(jnp.zeros((2 * HEAD_DIM, DSA_QB), F32) for _ in heads))
    for h in heads:
        o_ref[:, hsl[h]] = (accs[h][:HEAD_DIM, :] / accs[h][HEAD_DIM:, :]).T.astype(BF16)


def dsa_attention(proj, q, q_idx, prep_tok, rel_bias, batch, seq):
    n = proj.shape[0]
    nq = seq // DSA_QB
    topk = min(TOPK_MAX, seq // 4)
    wi = IDX_HEADS * IDX_DIM
    rowblk = lambda w, cb: pl.BlockSpec((DSA_QB, w), lambda b, i: (b * nq + i, cb))
    seqblk = lambda off: pl.BlockSpec((seq, LANE), lambda b, i: (b, off // LANE))
    return pl.pallas_call(
        functools.partial(_dsa_kernel, seq=seq, topk=topk),
        grid=(batch, nq),
        in_specs=[pl.BlockSpec(memory_space=pltpu.SMEM),
                  rowblk(WIDTH, 0), rowblk(wi, 0), rowblk(LANE, 0),
                  seqblk(C_DK), seqblk(C_DV), seqblk(C_KA), seqblk(C_KB)],
        out_specs=pl.BlockSpec((DSA_QB, WIDTH), lambda b, i: (b * nq + i, 0)),
        out_shape=jax.ShapeDtypeStruct((n, WIDTH), BF16),
        scratch_shapes=[pltpu.VMEM((seq, DSA_QB), I32),
                        pltpu.VMEM((N_HEADS, seq, DSA_QB), F32),
                        pltpu.VMEM((N_HEADS, 2 * DSA_QB, DSA_QB), F32),
                        pltpu.VMEM((seq, LANE), BF16),
                        pltpu.VMEM((2 * HEAD_DIM, seq), BF16),
                        pltpu.VMEM((seq, LANE), BF16),
                        pltpu.VMEM((seq, LANE), BF16)],
        compiler_params=_cparams(("arbitrary", "arbitrary")),
        name="dsa_attention",
    )(rel_bias, q, q_idx, prep_tok, proj, proj, proj, proj)


GDN_T = 256
GDN_GROUP = 2
GDN_HALO = 8


def _gdn_kernel(q_ref, k_ref, v_ref, z_ref, cw_ref, ng_ref, tok_ref, tr_ref, o_ref,
                xq_scr, xk_scr, xv_scr, state_scr):
    first = pl.program_id(1) == 0

    @pl.when(first)
    def _():
        state_scr[...] = jnp.zeros_like(state_scr)
        for scr in (xq_scr, xk_scr, xv_scr):
            scr[0:GDN_HALO, :] = jnp.zeros((GDN_HALO, WIDTH), F32)

    def conv(x_ref, scr, w_off):
        scr[GDN_HALO:, :] = x_ref[...]
        y = jnp.zeros((GDN_T, WIDTH), F32)
        for i in range(GDN_CONV):
            st = GDN_HALO - (GDN_CONV - 1) + i
            y = y + scr[st:st + GDN_T, :] * cw_ref[i:i + 1, w_off:w_off + WIDTH]
        scr[0:GDN_HALO, :] = scr[GDN_T:GDN_T + GDN_HALO, :]
        return _silu(y)

    qc = conv(q_ref, xq_scr, 0)
    kc = conv(k_ref, xk_scr, WIDTH)
    vc = conv(v_ref, xv_scr, 2 * WIDTH)
    tok = tok_ref[...]
    c = GDN_CHUNK
    heads = range(N_HEADS)
    hsl = [slice(h * HEAD_DIM, (h + 1) * HEAD_DIM) for h in heads]

    def l2norm_heads(x, scale):
        return jnp.concatenate(
            [x[:, s] * (lax.rsqrt(jnp.sum(x[:, s] * x[:, s], axis=-1, keepdims=True) + EPS) * scale)
             for s in hsl], axis=1)

    qf = l2norm_heads(qc, HEAD_DIM ** -0.5)
    kf = l2norm_heads(kc, 1.0)

    grp = GDN_GROUP
    gw = grp * HEAD_DIM
    nb = grp * c
    ri = lax.broadcasted_iota(I32, (nb, nb), 0)
    ci = lax.broadcasted_iota(I32, (nb, nb), 1)
    tril = ((ri // c) == (ci // c)) & (ri >= ci)
    eye = (ri == ci).astype(F32)
    pair_masks = []
    for lg in range(c.bit_length() - 1):
        pair_masks.append(((ri >> (lg + 1)) == (ci >> (lg + 1)))
                          & (((ri >> lg) & 1) == 1) & (((ci >> lg) & 1) == 0))
    lane_head = lax.broadcasted_iota(I32, (c, gw), 1) // HEAD_DIM
    row_head = lax.broadcasted_iota(I32, (nb, HEAD_DIM), 0) // c

    def spread(x):
        return jnp.concatenate([jnp.where(lane_head == u, x, 0.0) for u in range(grp)], axis=0)

    def stack(x):
        return jnp.concatenate([x[:, hsl[u]] for u in range(grp)], axis=0)

    def spread_lanes(x):
        return jnp.concatenate([jnp.where(row_head == u, x, 0.0) for u in range(grp)], axis=1)

    def split(x):
        hi = x.astype(BF16)
        return hi, (x - hi.astype(F32)).astype(BF16)

    def dot_split(a, b):
        return _dot(a[0], b[0]) + (_dot(a[0], b[1]) + _dot(a[1], b[0]))

    items = [(j, gi) for j in range(GDN_T // c) for gi in range(N_HEADS // grp)]
    pre = []
    for j, gi in items:
        rs = slice(j * c, (j + 1) * c)
        last = slice((j + 1) * c - 1, (j + 1) * c)
        gh = [gi * grp + u for u in range(grp)]
        gsl = slice(gi * gw, (gi + 1) * gw)
        qj, kj, vj = qf[rs, gsl], kf[rs, gsl], vc[rs, gsl]
        b_col = jnp.concatenate([tok[rs, L_GB + h:L_GB + h + 1] for h in gh], axis=0)
        g_col = jnp.concatenate([tok[rs, L_GA + h:L_GA + h + 1] for h in gh], axis=0)
        g_row = jnp.concatenate([tr_ref[0, L_GA + h:L_GA + h + 1, rs] for h in gh], axis=1)
        g_last = [tok[last, L_GA + h:L_GA + h + 1] for h in gh]
        g_last_col = jnp.concatenate([jnp.broadcast_to(g, (c, 1)) for g in g_last], axis=0)
        k_sp = spread(kj)
        q_sp = spread(qj)
        k_sp16 = k_sp.astype(BF16)
        decay = jnp.exp(jnp.where(tril, g_col - g_row, -jnp.inf))
        eg = jnp.exp(g_col)
        pre.append(dict(
            gh=gh, gsl=gsl,
            l_mat=b_col * _dot_nt(k_sp16, k_sp16) * decay,
            rhs=jnp.concatenate([stack(vj) * b_col, stack(kj) * (b_col * eg)], axis=1),
            qk=_dot_nt(q_sp.astype(BF16), k_sp16) * decay,
            q_dec=q_sp * eg,
            k_dec=k_sp * jnp.exp(g_last_col - g_col),
            e_last=jnp.concatenate([jnp.broadcast_to(jnp.exp(g), (HEAD_DIM, 1)) for g in g_last], axis=0)))

    t_inv = [eye - jnp.where(pair_masks[0], p["l_mat"], 0.0) for p in pre]
    for pm in pair_masks[1:]:
        t_s = [split(t) for t in t_inv]
        m_t = [dot_split(split(jnp.where(pm, p["l_mat"], 0.0)), ts) for p, ts in zip(pre, t_s)]
        t_inv = [t - dot_split(ts, split(m)) for t, ts, m in zip(t_inv, t_s, m_t)]
    sols = [dot_split(split(t), split(p["rhs"])) for t, p in zip(t_inv, pre)]

    outs = [[] for _ in heads]
    for p, sol in zip(pre, sols):
        u0 = sol[:, :HEAD_DIM]
        kcum = sol[:, HEAD_DIM:]
        st = state_scr[p["gsl"], :]
        stb = st.astype(BF16)
        v_new = u0 - _dot(spread_lanes(kcum).astype(BF16), stb)
        v_new_b = v_new.astype(BF16)
        o_st = _dot(p["q_dec"].astype(BF16), stb) + _dot(p["qk"].astype(BF16), v_new_b)
        state_scr[p["gsl"], :] = st * p["e_last"] + _dot(p["k_dec"].T.astype(BF16), v_new_b)
        for u, h in enumerate(p["gh"]):
            outs[h].append(o_st[u * c:(u + 1) * c, :])

    for h in heads:
        o = jnp.concatenate(outs[h], axis=0)
        ms = jnp.mean(o * o, axis=-1, keepdims=True)
        on = o * lax.rsqrt(ms + EPS) * ng_ref[...]
        o_ref[:, hsl[h]] = (on * _silu(z_ref[:, hsl[h]])).astype(BF16)


def gated_deltanet(proj, prep_tok, prep_tr, conv_w, norm_g, batch, seq):
    n = proj.shape[0]
    t = GDN_T
    nt = seq // t
    col = lambda off: pl.BlockSpec((t, WIDTH), lambda b, i: (b * nt + i, off // WIDTH))
    return pl.pallas_call(
        _gdn_kernel,
        grid=(batch, nt),
        in_specs=[col(C_GQ), col(C_GK), col(C_GV), col(C_GZ),
                  pl.BlockSpec((GDN_CONV, 3 * WIDTH), lambda b, i: (0, 0)),
                  pl.BlockSpec((1, HEAD_DIM), lambda b, i: (0, 0)),
                  pl.BlockSpec((t, LANE), lambda b, i: (b * nt + i, 0)),
                  pl.BlockSpec((1, 32, t), lambda b, i: (b, 0, i))],
        out_specs=pl.BlockSpec((t, WIDTH), lambda b, i: (b * nt + i, 0)),
        out_shape=jax.ShapeDtypeStruct((n, WIDTH), BF16),
        scratch_shapes=[pltpu.VMEM((t + GDN_HALO, WIDTH), F32),
                        pltpu.VMEM((t + GDN_HALO, WIDTH), F32),
                        pltpu.VMEM((t + GDN_HALO, WIDTH), F32),
                        pltpu.VMEM((N_HEADS * HEAD_DIM, HEAD_DIM), F32)],
        compiler_params=_cparams(("arbitrary", "arbitrary")),
        name="gated_deltanet",
    )(proj, proj, proj, proj, conv_w, norm_g.reshape(1, HEAD_DIM), prep_tok, prep_tr)


def _merge_kernel(h_ref, b0_ref, b1_ref, b2_ref, b3_ref, g0_ref, g1_ref, g2_ref, g3_ref,
                  wb_ref, o_ref):
    h = h_ref[...]
    acc = None
    for n, (b_ref, g_ref) in enumerate(zip((b0_ref, b1_ref, b2_ref, b3_ref),
                                           (g0_ref, g1_ref, g2_ref, g3_ref))):
        gate = jax.nn.sigmoid(_dot(h, g_ref[...]))
        term = gate * _dot(b_ref[...], wb_ref[n])
        acc = term if acc is None else acc + term
    o_ref[...] = acc.astype(BF16)


def merge_branches(h, branches, w_gate, w_branch, *, tm=512, tn=512):
    n, d = h.shape
    nj = d // tn
    bspec = pl.BlockSpec((tm, WIDTH), lambda j, i: (i, 0))
    gspec = lambda k: pl.BlockSpec((d, tn), lambda j, i: (0, k * nj + j))
    return pl.pallas_call(
        _merge_kernel,
        grid=(nj, n // tm),
        in_specs=[pl.BlockSpec((tm, d), lambda j, i: (i, 0)),
                  bspec, bspec, bspec, bspec,
                  gspec(0), gspec(1), gspec(2), gspec(3),
                  pl.BlockSpec((N_BRANCH, WIDTH, tn), lambda j, i: (0, 0, j))],
        out_specs=pl.BlockSpec((tm, tn), lambda j, i: (i, j)),
        out_shape=jax.ShapeDtypeStruct((n, d), BF16),
        compiler_params=_cparams(("arbitrary", "arbitrary")),
        name="merge_branches",
    )(h, *branches, w_gate, w_gate, w_gate, w_gate, w_branch)


def _resid_mm_kernel(a_ref, w_ref, x_ref, o_ref):
    o_ref[...] = x_ref[...] + _dot(a_ref[...], w_ref[...])


def resid_matmul(a, w, x, *, tm=512, tn=1024, name="resid_matmul"):
    n, k = a.shape
    d = w.shape[1]
    return pl.pallas_call(
        _resid_mm_kernel,
        grid=(d // tn, n // tm),
        in_specs=[pl.BlockSpec((tm, k), lambda j, i: (i, 0)),
                  pl.BlockSpec((k, tn), lambda j, i: (0, j)),
                  pl.BlockSpec((tm, tn), lambda j, i: (i, j))],
        out_specs=pl.BlockSpec((tm, tn), lambda j, i: (i, j)),
        out_shape=jax.ShapeDtypeStruct((n, d), F32),
        compiler_params=_cparams(("arbitrary", "arbitrary")),
        name=name,
    )(a, w, x)


FFN_HALO = 8
FFN_SUB = 512


def _ffn1_kernel(x_ref, g_ref, wg_ref, wu_ref, cw_ref, cb_ref, o_ref, h_scr, gt_scr, halo_scr,
                 *, tm, tiles_per_seq):
    i = pl.program_id(0)
    j = pl.program_id(1)

    @pl.when((i == 0) & (j == 0))
    def _():
        halo_scr[...] = jnp.zeros_like(halo_scr)

    @pl.when(j == 0)
    def _():
        def body(r, carry):
            rows = pl.ds(pl.multiple_of(r * NORM_ROWS, NORM_ROWS), NORM_ROWS)
            h_scr[rows, :] = _rmsnorm_rows(x_ref, g_ref, rows).astype(BF16)
            return carry
        lax.fori_loop(0, tm // NORM_ROWS, body, 0)

    h = h_scr[...]
    seq_start = (i % tiles_per_seq) == 0
    tn = o_ref.shape[1]
    for off in range(0, tn, FFN_SUB):
        cs = slice(off, min(off + FFN_SUB, tn))
        g = _dot(h, wg_ref[:, cs])
        gt_scr[FFN_HALO:, cs] = g
        gt_scr[0:FFN_HALO, cs] = jnp.where(seq_start, 0.0, halo_scr[j, :, cs])
        halo_scr[j, :, cs] = g[tm - FFN_HALO:, :]
        y = cb_ref[:, cs] + g * cw_ref[FFN_CONV - 1:FFN_CONV, cs]
        for t in range(FFN_CONV - 1):
            st = FFN_HALO - (FFN_CONV - 1) + t
            y = y + gt_scr[st:st + tm, cs] * cw_ref[t:t + 1, cs]
        o_ref[:, cs] = (_silu(y) * _dot(h, wu_ref[:, cs])).astype(BF16)


def conv_ffn_up(x, gain, w_gate, w_up, conv_w, conv_b, seq, *, tm=512, tn=1408):
    n, d = x.shape
    f = w_gate.shape[1]
    nj = f // tn
    return pl.pallas_call(
        functools.partial(_ffn1_kernel, tm=tm, tiles_per_seq=seq // tm),
        grid=(n // tm, nj),
        in_specs=[pl.BlockSpec((tm, d), lambda i, j: (i, 0)),
                  pl.BlockSpec((1, d), lambda i, j: (0, 0)),
                  pl.BlockSpec((d, tn), lambda i, j: (0, j)),
                  pl.BlockSpec((d, tn), lambda i, j: (0, j)),
                  pl.BlockSpec((FFN_CONV, tn), lambda i, j: (0, j)),
                  pl.BlockSpec((1, tn), lambda i, j: (0, j))],
        out_specs=pl.BlockSpec((tm, tn), lambda i, j: (i, j)),
        out_shape=jax.ShapeDtypeStruct((n, f), BF16),
        scratch_shapes=[pltpu.VMEM((tm, d), BF16),
                        pltpu.VMEM((tm + FFN_HALO, tn), F32),
                        pltpu.VMEM((nj, FFN_HALO, tn), F32)],
        compiler_params=_cparams(("arbitrary", "arbitrary")),
        name="conv_ffn_up",
    )(x, gain.reshape(1, d), w_gate, w_up, conv_w, conv_b.reshape(1, f))


IN_SIZES = (WIDTH, WIDTH, WIDTH, WIDTH,
            DSA_Q_RANK, HEAD_DIM, HEAD_DIM, IDX_DIM, IDX_HEADS,
            WIDTH, WIDTH, WIDTH, N_HEADS,
            WIDTH, WIDTH, WIDTH, WIDTH, N_HEADS, N_HEADS)
IN_NAMES = ("r_q", "r_k", "r_v", "r_g", "d_cq", "d_k", "d_v", "i_k", "i_w",
            "f_q", "f_k", "f_v", "f_f", "g_q", "g_k", "g_v", "g_z", "g_b", "g_a")
IN_PLAN = (("r_q", C_RQ), ("r_k", C_RK), ("r_v", C_RV), ("r_g", C_RG),
           ("f_q", C_FQ), ("f_k", C_FK), ("f_v", C_FV),
           ("g_q", C_GQ), ("g_k", C_GK), ("g_v", C_GV), ("g_z", C_GZ),
           ("d_k", C_DK), ("d_cq", C_DCQ), ("d_v", C_DV),
           ("i_k", C_KA), ("i_k", C_KB + IDX_DIM),
           ("i_w", C_SM + L_IW), ("f_f", C_SM + L_FF), ("g_b", C_SM + L_GB), ("g_a", C_SM + L_GA))


def _prep_w_in_kernel(w_ref, m_ref, g_ref):
    src = {}
    off = 0
    for name, size in zip(IN_NAMES, IN_SIZES):
        src[name] = (off, size)
        off += size
    m_ref[...] = jnp.zeros_like(m_ref)
    for name, dst in IN_PLAN:
        so, w = src[name]
        m_ref[:, dst:dst + w] = w_ref[:, so:so + w].astype(BF16)
    g_ref[...] = w_ref[:, off:off + g_ref.shape[1]].astype(BF16)


def prep_w_in(w_in, layer, *, tr=128):
    _, d, c = w_in.shape
    return pl.pallas_call(
        _prep_w_in_kernel,
        grid=(d // tr,),
        in_specs=[pl.BlockSpec((None, tr, c), lambda i: (layer, i, 0))],
        out_specs=[pl.BlockSpec((tr, C_TOT), lambda i: (i, 0)),
                   pl.BlockSpec((tr, N_BRANCH * d), lambda i: (i, 0))],
        out_shape=[jax.ShapeDtypeStruct((d, C_TOT), BF16),
                   jax.ShapeDtypeStruct((d, N_BRANCH * d), BF16)],
        compiler_params=_cparams(("arbitrary",)),
        name="prep_w_in",
    )(w_in)


def _cast_kernel(w_ref, o_ref):
    o_ref[...] = w_ref[...].astype(BF16)


def cast_layer(w, layer, *, tr=256):
    _, r, c = w.shape
    if r % tr:
        tr = r
    return pl.pallas_call(
        _cast_kernel,
        grid=(r // tr,),
        in_specs=[pl.BlockSpec((None, tr, c), lambda i: (layer, i, 0))],
        out_specs=pl.BlockSpec((tr, c), lambda i: (i, 0)),
        out_shape=jax.ShapeDtypeStruct((r, c), BF16),
        compiler_params=_cparams(("arbitrary",)),
        name="cast_bf16",
    )(w)


def cast_branch(w_branch, layer):
    _, nbr, r, c = w_branch.shape
    return pl.pallas_call(
        _cast_kernel,
        grid=(nbr,),
        in_specs=[pl.BlockSpec((None, None, r, c), lambda i: (layer, i, 0, 0))],
        out_specs=pl.BlockSpec((None, r, c), lambda i: (i, 0, 0)),
        out_shape=jax.ShapeDtypeStruct((nbr, r, c), BF16),
        compiler_params=_cparams(("arbitrary",)),
        name="cast_branch",
    )(w_branch)


def kernel(x, norm_mix, w_in, dsa_cq_norm, dsa_w_uq, dsa_w_qidx, fox_f_bias, gdn_conv, gdn_a_log,
           gdn_dt_bias, gdn_norm, w_branch, w_out, rel_bias, norm_ffn, ffn_w_gate, ffn_w_up,
           ffn_conv, ffn_conv_b, ffn_w_down, final_norm):
    batch, seq, d = x.shape
    depth = w_in.shape[0]
    xf = x.reshape(batch * seq, d)
    ret_tables = _retention_tables(seq)
    for l in range(depth):
        w_main, w_gate = prep_w_in(w_in, l)
        proj, h = norm_proj(xf, norm_mix[l], w_main)
        par = jnp.zeros((8, LANE), F32)
        par = par.at[0, L_FF:L_FF + N_HEADS].set(fox_f_bias[l])
        par = par.at[0, L_GA:L_GA + N_HEADS].set(gdn_dt_bias[l])
        par = par.at[1, L_GA:L_GA + N_HEADS].set(gdn_a_log[l])
        prep_tok, prep_tr = prep_small(proj, par, batch, seq)
        o_ret = retention(proj, ret_tables, batch, seq)
        q_dsa, q_idx = dsa_proj(proj, dsa_cq_norm[l], cast_layer(dsa_w_uq, l), cast_layer(dsa_w_qidx, l))
        o_dsa = dsa_attention(proj, q_dsa, q_idx, prep_tok, rel_bias, batch, seq)
        o_fox = fox_attention(proj, prep_tr, batch, seq)
        o_gdn = gated_deltanet(proj, prep_tok, prep_tr, gdn_conv[l], gdn_norm[l], batch, seq)
        merged = merge_branches(h, (o_ret, o_dsa, o_fox, o_gdn), w_gate, cast_branch(w_branch, l))
        xf = resid_matmul(merged, cast_layer(w_out, l), xf, name="out_proj")
        act = conv_ffn_up(xf, norm_ffn[l], cast_layer(ffn_w_gate, l), cast_layer(ffn_w_up, l),
                          ffn_conv[l], ffn_conv_b[l], seq)
        xf = resid_matmul(act, cast_layer(ffn_w_down, l), xf, name="ffn_down")
    return rmsnorm(xf, final_norm).reshape(batch, seq, d)
```

```python
import functools
import math

import jax
import jax.numpy as jnp
from jax import lax
from jax.experimental import pallas as pl
from jax.experimental.pallas import tpu as pltpu

F32 = jnp.float32
BF16 = jnp.bfloat16
I32 = jnp.int32

HEAD_DIM = 128
N_HEADS = 4
WIDTH = N_HEADS * HEAD_DIM
N_BRANCH = 4
RET_CHUNK = 128
ROPE_BASE = 10000.0
DSA_Q_RANK = 384
IDX_HEADS = 16
IDX_DIM = 64
TOPK_MAX = 256
GDN_CONV = 4
GDN_CHUNK = 64
REL_BUCKETS = 32
REL_MAX_DIST = 128
FFN_CONV = 3
EPS = 1e-6

LANE = 128
SUBLANE = 8
VMEM_LIMIT = 56 * 1024 * 1024

C_RQ, C_RK, C_RV, C_RG = 0, 512, 1024, 1536
C_FQ, C_FK, C_FV = 2048, 2560, 3072
C_GQ, C_GK, C_GV, C_GZ = 3584, 4096, 4608, 5120
C_DK, C_DCQ, C_DV, C_KA, C_KB, C_SM = 5632, 5760, 6144, 6272, 6400, 6528
C_TOT = 6656
L_IW, L_FF, L_GB, L_GA = 0, 16, 20, 24

LOG2E = 1.4426950408889634
INT_MIN = -(2 ** 31)
INT_MAX = 2 ** 31 - 1
HIGHEST = lax.Precision.HIGHEST


def _cparams(sem, vmem=VMEM_LIMIT):
    return pltpu.CompilerParams(dimension_semantics=sem, vmem_limit_bytes=vmem)


def _dot(a, b):
    return jnp.dot(a, b, preferred_element_type=F32)


def _dot_nt(a, b):
    return lax.dot_general(a, b, (((1,), (1,)), ((), ())), preferred_element_type=F32)


def _silu(x):
    return x * jax.nn.sigmoid(x)


NORM_ROWS = 32


def _rmsnorm_rows(x_ref, g_ref, rows):
    x = x_ref[rows, :]
    ms = jnp.mean(x * x, axis=-1, keepdims=True)
    return x * lax.rsqrt(ms + EPS) * g_ref[...]


PROJ_SUB = 512


def _norm_proj_kernel(x_ref, g_ref, w_ref, o_ref, h_ref, h_scr, *, tm):
    def body(r, carry):
        rows = pl.ds(pl.multiple_of(r * NORM_ROWS, NORM_ROWS), NORM_ROWS)
        hb = _rmsnorm_rows(x_ref, g_ref, rows).astype(BF16)
        h_scr[rows, :] = hb
        h_ref[rows, :] = hb
        return carry
    lax.fori_loop(0, tm // NORM_ROWS, body, 0)

    h = h_scr[...]
    tn = o_ref.shape[1]
    for off in range(0, tn, PROJ_SUB):
        cs = slice(off, min(off + PROJ_SUB, tn))
        o_ref[:, cs] = _dot(h, w_ref[:, cs])


def norm_proj(x, gain, w, *, tm=512, col_parts=2):
    n, d = x.shape
    c = w.shape[1]
    tn = c // col_parts
    return pl.pallas_call(
        functools.partial(_norm_proj_kernel, tm=tm),
        grid=(col_parts, n // tm),
        in_specs=[pl.BlockSpec((tm, d), lambda j, i: (i, 0)),
                  pl.BlockSpec((1, d), lambda j, i: (0, 0)),
                  pl.BlockSpec((d, tn), lambda j, i: (0, j), pipeline_mode=pl.Buffered(1))],
        out_specs=[pl.BlockSpec((tm, tn), lambda j, i: (i, j)),
                   pl.BlockSpec((None, tm, d), lambda j, i: (j, i, 0))],
        out_shape=[jax.ShapeDtypeStruct((n, c), F32),
                   jax.ShapeDtypeStruct((col_parts, n, d), BF16)],
        scratch_shapes=[pltpu.VMEM((tm, d), BF16)],
        compiler_params=_cparams(("arbitrary", "arbitrary")),
        name="norm_proj",
    )(x, gain.reshape(1, d), w)


def _rmsnorm_kernel(x_ref, g_ref, o_ref, *, tm):
    def body(r, carry):
        rows = pl.ds(pl.multiple_of(r * NORM_ROWS, NORM_ROWS), NORM_ROWS)
        o_ref[rows, :] = _rmsnorm_rows(x_ref, g_ref, rows)
        return carry
    lax.fori_loop(0, tm // NORM_ROWS, body, 0)


def rmsnorm(x, gain, *, tm=512):
    n, d = x.shape
    return pl.pallas_call(
        functools.partial(_rmsnorm_kernel, tm=tm),
        grid=(n // tm,),
        in_specs=[pl.BlockSpec((tm, d), lambda i: (i, 0)),
                  pl.BlockSpec((1, d), lambda i: (0, 0))],
        out_specs=pl.BlockSpec((tm, d), lambda i: (i, 0)),
        out_shape=jax.ShapeDtypeStruct((n, d), F32),
        compiler_params=_cparams(("arbitrary",)),
        name="final_rmsnorm",
    )(x, gain.reshape(1, d))


def _prep_kernel(s_ref, par_ref, tok_ref, tr_ref, carry_scr):
    @pl.when(pl.program_id(1) == 0)
    def _():
        carry_scr[...] = jnp.zeros_like(carry_scr)

    s = s_ref[...]
    lane = lax.broadcasted_iota(I32, (LANE, LANE), 1)
    row = lax.broadcasted_iota(I32, (LANE, LANE), 0)
    z = s + par_ref[0:1, :]
    soft = jnp.maximum(z, 0.0) + jnp.log1p(jnp.exp(-jnp.abs(z)))
    log_sig = z - soft
    sig = jax.nn.sigmoid(z)
    g_val = -jnp.exp(par_ref[1:2, :]) * soft
    is_f = (lane >= L_FF) & (lane < L_FF + N_HEADS)
    is_b = (lane >= L_GB) & (lane < L_GB + N_HEADS)
    is_a = (lane >= L_GA) & (lane < L_GA + N_HEADS)
    pre = jnp.where(is_f, log_sig, jnp.where(is_a, g_val, 0.0))
    tri = (row >= lane).astype(F32)
    tri_blk = ((row >= lane) & ((row // GDN_CHUNK) == (lane // GDN_CHUNK))).astype(F32)
    cum_full = jnp.dot(tri, pre, precision=HIGHEST, preferred_element_type=F32)
    cum_blk = jnp.dot(tri_blk, pre, precision=HIGHEST, preferred_element_type=F32)
    c_fox = cum_full + carry_scr[0:1, :]
    carry_scr[0:1, :] = c_fox[LANE - 1:LANE, :]
    scale_iw = IDX_HEADS ** -0.5 * IDX_DIM ** -0.5
    out = jnp.where(is_f, c_fox,
                    jnp.where(is_a, cum_blk,
                              jnp.where(is_b, sig,
                                        jnp.where(lane < IDX_HEADS, s * scale_iw, 0.0))))
    tok_ref[...] = out
    tr_ref[0] = out.T[0:32, :]


def prep_small(proj, par, batch, seq):
    n = proj.shape[0]
    nc = seq // LANE
    return pl.pallas_call(
        _prep_kernel,
        grid=(batch, nc),
        in_specs=[pl.BlockSpec((LANE, LANE), lambda b, c: (b * nc + c, C_SM // LANE)),
                  pl.BlockSpec((8, LANE), lambda b, c: (0, 0))],
        out_specs=[pl.BlockSpec((LANE, LANE), lambda b, c: (b * nc + c, 0)),
                   pl.BlockSpec((1, 32, LANE), lambda b, c: (b, 0, c))],
        out_shape=[jax.ShapeDtypeStruct((n, LANE), F32),
                   jax.ShapeDtypeStruct((batch, 32, seq), F32)],
        scratch_shapes=[pltpu.VMEM((8, LANE), F32)],
        compiler_params=_cparams(("arbitrary", "arbitrary")),
        name="prep_small",
    )(proj, par)


def _ret_gamma():
    return [math.log1p(-(2.0 ** (-5.0 - h))) for h in range(N_HEADS)]


def _retention_kernel(q_ref, k_ref, v_ref, g_ref, cos_ref, sin_ref, dec_ref, zeta_ref, xi_ref,
                      o_ref, state_scr):
    @pl.when(pl.program_id(1) == 0)
    def _():
        state_scr[...] = jnp.zeros_like(state_scr)

    cos_t = cos_ref[...]
    sin_t = sin_ref[...]
    log_gamma = _ret_gamma()
    heads = range(N_HEADS)
    hsl = [slice(h * HEAD_DIM, (h + 1) * HEAD_DIM) for h in heads]

    def rope(x):
        return x * cos_t + pltpu.roll(x, HEAD_DIM // 2, 1) * sin_t

    qb = [rope(q_ref[:, s]).astype(BF16) for s in hsl]
    kr = [rope(k_ref[:, s]) * (HEAD_DIM ** -0.5) for s in hsl]
    kb = [x.astype(BF16) for x in kr]
    vb = [v_ref[:, s].astype(BF16) for s in hsl]
    st = [state_scr[h] for h in heads]
    inner = [(_dot_nt(qb[h], kb[h]) * dec_ref[h]).astype(BF16) for h in heads]
    cross = [_dot(qb[h], st[h].astype(BF16)) * xi_ref[h] for h in heads]
    kv = [_dot((kr[h] * zeta_ref[h]).T.astype(BF16), vb[h]) for h in heads]
    o = [_dot(inner[h], vb[h]) + cross[h] for h in heads]
    for h in heads:
        state_scr[h] = st[h] * math.exp(log_gamma[h] * RET_CHUNK) + kv[h]
        mu = jnp.mean(o[h], axis=-1, keepdims=True)
        oc = o[h] - mu
        var = jnp.mean(oc * oc, axis=-1, keepdims=True)
        o_ref[:, hsl[h]] = (_silu(g_ref[:, hsl[h]]) * (oc * lax.rsqrt(var + EPS))).astype(BF16)


def _retention_tables(seq):
    half = HEAD_DIM // 2
    inv = 1.0 / (ROPE_BASE ** (jnp.arange(half, dtype=F32) / half))
    ang = jnp.arange(seq).astype(F32)[:, None] * inv[None, :]
    cos, sin = jnp.cos(ang), jnp.sin(ang)
    cos_t = jnp.concatenate([cos, cos], axis=-1)
    sin_t = jnp.concatenate([-sin, sin], axis=-1)
    c = RET_CHUNK
    log_gamma = jnp.log1p(-jnp.exp2(-5.0 - jnp.arange(N_HEADS, dtype=F32)))
    n = jnp.arange(c, dtype=F32)
    diff = n[:, None] - n[None, :]
    decay = jnp.where(diff >= 0, jnp.exp(log_gamma[:, None, None] * jnp.maximum(diff, 0.0)), 0.0)
    zeta = jnp.exp(log_gamma[:, None] * (c - 1 - n)[None, :])
    xi = jnp.exp(log_gamma[:, None] * (n + 1)[None, :])
    ones = jnp.ones((1, 1, HEAD_DIM), F32)
    return cos_t, sin_t, decay, zeta[:, :, None] * ones, xi[:, :, None] * ones


def retention(proj, tables, batch, seq):
    n = proj.shape[0]
    c = RET_CHUNK
    nc = seq // c
    cos_t, sin_t, decay, zeta, xi = tables
    col = lambda off: pl.BlockSpec((c, WIDTH), lambda b, i: (b * nc + i, off // WIDTH))
    full3 = pl.BlockSpec((N_HEADS, c, HEAD_DIM), lambda b, i: (0, 0, 0))
    return pl.pallas_call(
        _retention_kernel,
        grid=(batch, nc),
        in_specs=[col(C_RQ), col(C_RK), col(C_RV), col(C_RG),
                  pl.BlockSpec((c, HEAD_DIM), lambda b, i: (i, 0)),
                  pl.BlockSpec((c, HEAD_DIM), lambda b, i: (i, 0)),
                  full3, full3, full3],
        out_specs=pl.BlockSpec((c, WIDTH), lambda b, i: (b * nc + i, 0)),
        out_shape=jax.ShapeDtypeStruct((n, WIDTH), BF16),
        scratch_shapes=[pltpu.VMEM((N_HEADS, HEAD_DIM, HEAD_DIM), F32)],
        compiler_params=_cparams(("arbitrary", "arbitrary")),
        name="retention",
    )(proj, proj, proj, proj, cos_t, sin_t, decay, zeta, xi)


def _fox_kernel(qi_ref, ki_ref, q_ref, k_ref, v_ref, ctr_ref, o_ref, m_scr, acc_scr, *, t):
    qi = qi_ref[pl.program_id(1)]
    ki = ki_ref[pl.program_id(1)]

    @pl.when(ki == 0)
    def _():
        m_scr[...] = jnp.full_like(m_scr, -jnp.inf)
        acc_scr[...] = jnp.zeros_like(acc_scr)

    def step(masked):
        if masked:
            row = lax.broadcasted_iota(I32, (t, t), 0)
            colm = lax.broadcasted_iota(I32, (t, t), 1)
            keep = row >= colm
        ones = jnp.ones((t, HEAD_DIM), BF16)
        for h in range(N_HEADS):
            sl = slice(h * HEAD_DIM, (h + 1) * HEAD_DIM)
            qb = q_ref[:, sl].astype(BF16)
            kb = k_ref[:, sl].astype(BF16)
            c_k = ctr_ref[0, L_FF + h:L_FF + h + 1, :] * LOG2E
            s = _dot_nt(qb, kb) * (HEAD_DIM ** -0.5 * LOG2E) - c_k
            if masked:
                s = jnp.where(keep, s, -jnp.inf)
            m_old = m_scr[h]
            m_new = jnp.maximum(m_old, jnp.max(s, axis=-1, keepdims=True))
            alpha = jnp.exp2(m_old - m_new)
            p = jnp.exp2(s - m_new)
            v_aug = jnp.concatenate([v_ref[:, sl].astype(BF16), ones], axis=1)
            acc_scr[h] = alpha * acc_scr[h] + _dot(p.astype(BF16), v_aug)
            m_scr[h] = m_new

    @pl.when(ki < qi)
    def _():
        step(False)

    @pl.when(ki == qi)
    def _():
        step(True)
        for h in range(N_HEADS):
            sl = slice(h * HEAD_DIM, (h + 1) * HEAD_DIM)
            acc = acc_scr[h]
            o_ref[:, sl] = (acc[:, :HEAD_DIM] / acc[:, HEAD_DIM:]).astype(BF16)


def fox_attention(proj, prep_tr, batch, seq, *, t=512):
    n = proj.shape[0]
    nt = seq // t
    pairs = [(qi, ki) for qi in range(nt) for ki in range(qi + 1)]
    qi_arr = jnp.asarray([p[0] for p in pairs], I32)
    ki_arr = jnp.asarray([p[1] for p in pairs], I32)
    qspec = pl.BlockSpec((t, WIDTH), lambda b, s, qi, ki: (b * nt + qi[s], C_FQ // WIDTH))
    kspec = lambda off: pl.BlockSpec(
        (t, WIDTH), lambda b, s, qi, ki: (b * nt + ki[s], off // WIDTH))
    return pl.pallas_call(
        functools.partial(_fox_kernel, t=t),
        grid_spec=pltpu.PrefetchScalarGridSpec(
            num_scalar_prefetch=2,
            grid=(batch, len(pairs)),
            in_specs=[qspec, kspec(C_FK), kspec(C_FV),
                      pl.BlockSpec((1, 32, t), lambda b, s, qi, ki: (b, 0, ki[s]))],
            out_specs=pl.BlockSpec((t, WIDTH), lambda b, s, qi, ki: (b * nt + qi[s], 0)),
            scratch_shapes=[pltpu.VMEM((N_HEADS, t, 1), F32),
                            pltpu.VMEM((N_HEADS, t, 2 * HEAD_DIM), F32)]),
        out_shape=jax.ShapeDtypeStruct((n, WIDTH), BF16),
        compiler_params=_cparams(("arbitrary", "arbitrary")),
        name="fox_attention",
    )(qi_arr, ki_arr, proj, proj, proj, prep_tr)


def _dsa_proj_kernel(cq_ref, g_ref, wq_ref, wi_ref, q_ref, qi_ref):
    x = cq_ref[...]
    ms = jnp.mean(x * x, axis=-1, keepdims=True)
    cb = (x * lax.rsqrt(ms + EPS) * g_ref[...]).astype(BF16)
    q_ref[...] = _dot(cb, wq_ref[...]).astype(BF16)
    qi_ref[...] = _dot(cb, wi_ref[...]).astype(BF16)


def dsa_proj(proj, cq_norm, w_uq, w_qidx, *, tm=512):
    n = proj.shape[0]
    r = DSA_Q_RANK
    wi = IDX_HEADS * IDX_DIM
    return pl.pallas_call(
        _dsa_proj_kernel,
        grid=(n // tm,),
        in_specs=[pl.BlockSpec((tm, r), lambda i: (i, C_DCQ // r)),
                  pl.BlockSpec((1, r), lambda i: (0, 0)),
                  pl.BlockSpec((r, WIDTH), lambda i: (0, 0)),
                  pl.BlockSpec((r, wi), lambda i: (0, 0))],
        out_specs=[pl.BlockSpec((tm, WIDTH), lambda i: (i, 0)),
                   pl.BlockSpec((tm, wi), lambda i: (i, 0))],
        out_shape=[jax.ShapeDtypeStruct((n, WIDTH), BF16),
                   jax.ShapeDtypeStruct((n, wi), BF16)],
        compiler_params=_cparams(("arbitrary",)),
        name="dsa_proj",
    )(proj, cq_norm.reshape(1, r), w_uq, w_qidx)


DSA_QB = 256
DSA_KC = 512
DSA_SCORE_MID_STEPS = 20
DSA_FEW_KEYS = 4
DSA_HALVE_FIXED = 12
DSA_WALK_FIXED = 3


def _t5_bucket(rel):
    max_exact = REL_BUCKETS // 2
    relf = jnp.maximum(rel, max_exact).astype(F32)
    large = max_exact + (jnp.log(relf / max_exact) / math.log(REL_MAX_DIST / max_exact)
                         * (REL_BUCKETS - max_exact)).astype(I32)
    large = jnp.minimum(large, REL_BUCKETS - 1)
    return jnp.where(rel < max_exact, rel, large)


def _dsa_kernel(rb_ref, q_ref, qi_ref, tok_ref, k_ref, v_ref, ka_ref, kb_ref, o_ref,
                key_scr, lg_scr, band_scr, kb16_scr, vt_scr, ka16_scr, kb16i_scr, *, seq, topk):
    qb_idx = pl.program_id(1)
    t0 = qb_idx * DSA_QB
    n_kc = (t0 + DSA_QB - 1) // DSA_KC + 1
    row_vec = (1, DSA_QB)

    @pl.when(qb_idx == 0)
    def _():
        kb16_scr[...] = k_ref[...].astype(BF16)
        ka16_scr[...] = ka_ref[...].astype(BF16)
        kb16i_scr[...] = kb_ref[...].astype(BF16)
        for c in range(seq // DSA_KC):
            cs = slice(c * DSA_KC, (c + 1) * DSA_KC)
            vt_scr[0:HEAD_DIM, cs] = v_ref[cs, :].T.astype(BF16)
        vt_scr[HEAD_DIM:, :] = jnp.ones((HEAD_DIM, seq), BF16)

    @pl.when((pl.program_id(0) == 0) & (qb_idx == 0))
    def _():
        j_ = lax.broadcasted_iota(I32, (2 * DSA_QB, DSA_QB), 0)
        i_ = lax.broadcasted_iota(I32, (2 * DSA_QB, DSA_QB), 1)
        rel = i_ + DSA_QB - j_
        bucket = _t5_bucket(rel)
        for h in range(N_HEADS):
            far = rb_ref[REL_BUCKETS - 1, h]
            band = jnp.zeros((2 * DSA_QB, DSA_QB), F32)
            for bk in range(REL_BUCKETS - 1):
                band = jnp.where(bucket == bk, (rb_ref[bk, h] - far) * LOG2E, band)
            band_scr[h] = jnp.where(rel >= 0, band, 0.0)

    w_t = tok_ref[...].T
    key_s = lax.broadcasted_iota(I32, (DSA_KC, DSA_QB), 0)
    row_t = t0 + lax.broadcasted_iota(I32, (DSA_KC, DSA_QB), 1)

    def score_chunk(c, carry):
        kmax, kmin = carry
        ks = pl.ds(pl.multiple_of(c * DSA_KC, DSA_KC), DSA_KC)
        ka = ka16_scr[ks, :]
        kb = kb16i_scr[ks, :]
        acc = jnp.zeros((DSA_KC, DSA_QB), F32)
        for p in range(IDX_HEADS // 2):
            qp = qi_ref[:, p * LANE:(p + 1) * LANE]
            acc = acc + jnp.maximum(_dot_nt(ka, qp), 0.0) * w_t[2 * p:2 * p + 1, :]
            acc = acc + jnp.maximum(_dot_nt(kb, qp), 0.0) * w_t[2 * p + 1:2 * p + 2, :]
        bits = pltpu.bitcast(acc, I32)
        key = bits ^ ((bits >> 31) & 0x7FFFFFFF)
        valid = (c * DSA_KC + key_s) <= row_t
        key_scr[ks, :] = jnp.where(valid, key, INT_MIN)
        kmax = jnp.maximum(kmax, jnp.max(jnp.where(valid, key, INT_MIN), axis=0, keepdims=True))
        kmin = jnp.minimum(kmin, jnp.min(jnp.where(valid, key, INT_MAX), axis=0, keepdims=True))
        return kmax, kmin

    kmax, kmin = lax.fori_loop(0, n_kc, score_chunk, (jnp.full(row_vec, INT_MIN, I32),
                                                     jnp.full(row_vec, INT_MAX, I32)))

    def scan_keys(cand, with_below):
        def body(c, carry):
            cnt, below = carry
            ks = pl.ds(pl.multiple_of(c * DSA_KC, DSA_KC), DSA_KC)
            keys = key_scr[ks, :]
            ge = keys >= cand
            ones = ge.astype(I32)
            low = jnp.where(ge, INT_MIN, keys)
            for u in range(DSA_KC // SUBLANE):
                us = slice(u * SUBLANE, (u + 1) * SUBLANE)
                cnt = cnt + ones[us, :]
                if with_below:
                    below = jnp.maximum(below, low[us, :])
            return cnt, below
        cnt, below = lax.fori_loop(0, n_kc, body, (jnp.zeros((SUBLANE, DSA_QB), I32),
                                                   jnp.full((SUBLANE, DSA_QB), INT_MIN, I32)))
        cnt = jnp.sum(cnt, axis=0, keepdims=True)
        if with_below:
            return cnt, jnp.max(below, axis=0, keepdims=True)
        return cnt

    def key_to_score(k):
        return pltpu.bitcast(k ^ ((k >> 31) & 0x7FFFFFFF), F32)

    def score_to_key(s):
        b = pltpu.bitcast(s, I32)
        return b ^ ((b >> 31) & 0x7FFFFFFF)

    def open_rows(lo, hi, c_lo):
        return (c_lo > topk) & (hi - 1 > lo)

    def any_row(flag):
        return jnp.max(jnp.where(flag, 1, 0))

    def update(cand, cnt, lo, hi, c_lo, c_hi):
        ge = cnt >= topk
        return (jnp.where(ge, cand, lo), jnp.where(ge, hi, cand),
                jnp.where(ge, cnt, c_lo), jnp.where(ge, c_hi, cnt))

    def crowded(lo, hi, c_lo, c_hi):
        return any_row(open_rows(lo, hi, c_lo) & (c_lo - c_hi > DSA_FEW_KEYS))

    def halve_step(it, lo, hi, c_lo, c_hi):
        key_mid = (lo >> 1) + (hi >> 1) + (lo & hi & 1)
        score_mid = score_to_key(0.5 * key_to_score(lo) + 0.5 * key_to_score(hi - 1))
        cand = jnp.where(it < DSA_SCORE_MID_STEPS, score_mid, key_mid)
        cand = jnp.minimum(jnp.maximum(cand, lo + 1), hi - 1)
        cand = jnp.where(hi - 1 > lo, cand, lo)
        return update(cand, scan_keys(cand, False), lo, hi, c_lo, c_hi)

    def halve_body(st):
        it, _, lo, hi, c_lo, c_hi = st
        go = crowded(lo, hi, c_lo, c_hi)
        return (it + 1, go) + halve_step(it, lo, hi, c_lo, c_hi)

    def walk_step(lo, hi, c_lo, c_hi, nxt):
        is_open = open_rows(lo, hi, c_lo)
        cand = jnp.where(is_open, nxt, lo)
        cnt, below = scan_keys(cand, True)
        ge = cnt >= topk
        hi = jnp.where(is_open, jnp.where(ge, cand + 1, cand), hi)
        c_hi = jnp.where(is_open & jnp.logical_not(ge), cnt, c_hi)
        lo = jnp.where(is_open & ge, cand, lo)
        c_lo = jnp.where(is_open & ge, cnt, c_lo)
        nxt = jnp.where(ge, nxt, below)
        return lo, hi, c_lo, c_hi, nxt

    def walk_body(st):
        go = any_row(open_rows(st[1], st[2], st[3]))
        return (go,) + walk_step(*st[1:])

    n_valid = jnp.minimum(t0 + lax.broadcasted_iota(I32, row_vec, 1) + 1, seq)
    st = (kmin, kmax + 1, n_valid, jnp.zeros(row_vec, I32))
    st = lax.fori_loop(0, DSA_HALVE_FIXED, lambda it, s: halve_step(it, *s), st)
    st = lax.while_loop(lambda s: s[1] > 0, halve_body,
                        (jnp.int32(DSA_HALVE_FIXED), crowded(*st)) + st)[2:]
    _, nxt0 = scan_keys(st[1], True)
    st = lax.fori_loop(0, DSA_WALK_FIXED, lambda it, s: walk_step(*s), st + (nxt0,))
    _, thr, hi, n_ge, n_gt, _ = lax.while_loop(
        lambda s: s[0] > 0, walk_body, (any_row(open_rows(st[0], st[1], st[2])),) + st)

    tied = n_ge > topk
    has_tie = jnp.max(jnp.where(tied, 1, 0)) > 0

    @pl.when(has_tie)
    def _():
        room = (topk - n_gt).astype(F32)
        ii = lax.broadcasted_iota(I32, (LANE, LANE), 0)
        jj = lax.broadcasted_iota(I32, (LANE, LANE), 1)
        lower = (ii >= jj).astype(BF16)

        def body(c, seen):
            ks = pl.ds(pl.multiple_of(c * LANE, LANE), LANE)
            kk = key_scr[ks, :]
            eq = kk == thr
            rank = seen + _dot(lower, eq.astype(BF16))
            drop = eq & (rank > room) & tied
            key_scr[ks, :] = jnp.where(drop, INT_MIN, kk)
            return seen + jnp.sum(eq.astype(F32), axis=0, keepdims=True)
        lax.fori_loop(0, n_kc * (DSA_KC // LANE), body, jnp.zeros(row_vec, F32))

    def mask_chunk(c, carry):
        ks = pl.ds(pl.multiple_of(c * DSA_KC, DSA_KC), DSA_KC)
        sel = jnp.where(key_scr[ks, :] >= thr, 0.0, -jnp.inf).astype(F32)
        key_scr[ks, :] = pltpu.bitcast(sel, I32)
        return carry
    lax.fori_loop(0, n_kc, mask_chunk, 0)

    heads = range(N_HEADS)
    hsl = [slice(h * HEAD_DIM, (h + 1) * HEAD_DIM) for h in heads]

    def logit_chunk(c, ms):
        ks = pl.ds(pl.multiple_of(c * DSA_KC, DSA_KC), DSA_KC)
        k_c = kb16_scr[ks, :]
        sel = pltpu.bitcast(key_scr[ks, :], F32)
        out = []
        for h in heads:
            s = _dot_nt(k_c, q_ref[:, hsl[h]]) * (HEAD_DIM ** -0.5 * LOG2E) + sel
            lg_scr[h, ks, :] = s
            out.append(jnp.maximum(ms[h], jnp.max(s, axis=0, keepdims=True)))
        return tuple(out)
    ms = lax.fori_loop(0, n_kc, logit_chunk,
                       tuple(jnp.full(row_vec, -jnp.inf, F32) for _ in heads))

    band_off = pl.multiple_of(jnp.maximum(qb_idx - 1, 0) * DSA_QB, DSA_QB)
    ws = pl.ds(band_off, 2 * DSA_QB)
    ms = list(ms)
    for h in heads:
        band_h = band_scr[h]
        band_first = jnp.concatenate([band_h[DSA_QB:, :], jnp.zeros((DSA_QB, DSA_QB), F32)], axis=0)
        win = lg_scr[h, ws, :] + jnp.where(qb_idx == 0, band_first, band_h)
        lg_scr[h, ws, :] = win
        ms[h] = jnp.maximum(ms[h], jnp.max(win, axis=0, keepdims=True))

    def pv_chunk(c, accs):
        ks = pl.ds(pl.multiple_of(c * DSA_KC, DSA_KC), DSA_KC)
        vt_c = vt_scr[:, ks]
        return tuple(accs[h] + _dot(vt_c, jnp.exp2(lg_scr[h, ks, :] - ms[h]).astype(BF16))
                     for h in heads)
    accs = lax.fori_loop(0, n_kc, pv_chunk,
                         tuple(jnp.zeros((2 * HEAD_DIM, DSA_QB), F32) for _ in heads))
    for h in heads:
        o_ref[:, hsl[h]] = (accs[h][:HEAD_DIM, :] / accs[h][HEAD_DIM:, :]).T.astype(BF16)


def dsa_attention(proj, q, q_idx, prep_tok, rel_bias, batch, seq):
    n = proj.shape[0]
    nq = seq // DSA_QB
    topk = min(TOPK_MAX, seq // 4)
    wi = IDX_HEADS * IDX_DIM
    rowblk = lambda w, cb: pl.BlockSpec((DSA_QB, w), lambda b, i: (b * nq + i, cb))
    seqblk = lambda off: pl.BlockSpec((seq, LANE), lambda b, i: (b, off // LANE))
    return pl.pallas_call(
        functools.partial(_dsa_kernel, seq=seq, topk=topk),
        grid=(batch, nq),
        in_specs=[pl.BlockSpec(memory_space=pltpu.SMEM),
                  rowblk(WIDTH, 0), rowblk(wi, 0), rowblk(LANE, 0),
                  seqblk(C_DK), seqblk(C_DV), seqblk(C_KA), seqblk(C_KB)],
        out_specs=pl.BlockSpec((DSA_QB, WIDTH), lambda b, i: (b * nq + i, 0)),
        out_shape=jax.ShapeDtypeStruct((n, WIDTH), BF16),
        scratch_shapes=[pltpu.VMEM((seq, DSA_QB), I32),
                        pltpu.VMEM((N_HEADS, seq, DSA_QB), F32),
                        pltpu.VMEM((N_HEADS, 2 * DSA_QB, DSA_QB), F32),
                        pltpu.VMEM((seq, LANE), BF16),
                        pltpu.VMEM((2 * HEAD_DIM, seq), BF16),
                        pltpu.VMEM((seq, LANE), BF16),
                        pltpu.VMEM((seq, LANE), BF16)],
        compiler_params=_cparams(("arbitrary", "arbitrary")),
        name="dsa_attention",
    )(rel_bias, q, q_idx, prep_tok, proj, proj, proj, proj)


GDN_T = 256
GDN_GROUP = 2
GDN_HALO = 8


def _gdn_kernel(q_ref, k_ref, v_ref, z_ref, cw_ref, ng_ref, tok_ref, tr_ref, o_ref,
                xq_scr, xk_scr, xv_scr, state_scr):
    first = pl.program_id(1) == 0

    @pl.when(first)
    def _():
        state_scr[...] = jnp.zeros_like(state_scr)
        for scr in (xq_scr, xk_scr, xv_scr):
            scr[0:GDN_HALO, :] = jnp.zeros((GDN_HALO, WIDTH), F32)

    def conv(x_ref, scr, w_off):
        scr[GDN_HALO:, :] = x_ref[...]
        y = jnp.zeros((GDN_T, WIDTH), F32)
        for i in range(GDN_CONV):
            st = GDN_HALO - (GDN_CONV - 1) + i
            y = y + scr[st:st + GDN_T, :] * cw_ref[i:i + 1, w_off:w_off + WIDTH]
        scr[0:GDN_HALO, :] = scr[GDN_T:GDN_T + GDN_HALO, :]
        return _silu(y)

    qc = conv(q_ref, xq_scr, 0)
    kc = conv(k_ref, xk_scr, WIDTH)
    vc = conv(v_ref, xv_scr, 2 * WIDTH)
    tok = tok_ref[...]
    c = GDN_CHUNK
    heads = range(N_HEADS)
    hsl = [slice(h * HEAD_DIM, (h + 1) * HEAD_DIM) for h in heads]

    def l2norm_heads(x, scale):
        return jnp.concatenate(
            [x[:, s] * (lax.rsqrt(jnp.sum(x[:, s] * x[:, s], axis=-1, keepdims=True) + EPS) * scale)
             for s in hsl], axis=1)

    qf = l2norm_heads(qc, HEAD_DIM ** -0.5)
    kf = l2norm_heads(kc, 1.0)

    grp = GDN_GROUP
    gw = grp * HEAD_DIM
    nb = grp * c
    ri = lax.broadcasted_iota(I32, (nb, nb), 0)
    ci = lax.broadcasted_iota(I32, (nb, nb), 1)
    tril = ((ri // c) == (ci // c)) & (ri >= ci)
    eye = (ri == ci).astype(F32)
    pair_masks = []
    for lg in range(c.bit_length() - 1):
        pair_masks.append(((ri >> (lg + 1)) == (ci >> (lg + 1)))
                          & (((ri >> lg) & 1) == 1) & (((ci >> lg) & 1) == 0))
    lane_head = lax.broadcasted_iota(I32, (c, gw), 1) // HEAD_DIM
    row_head = lax.broadcasted_iota(I32, (nb, HEAD_DIM), 0) // c

    def spread(x):
        return jnp.concatenate([jnp.where(lane_head == u, x, 0.0) for u in range(grp)], axis=0)

    def stack(x):
        return jnp.concatenate([x[:, hsl[u]] for u in range(grp)], axis=0)

    def spread_lanes(x):
        return jnp.concatenate([jnp.where(row_head == u, x, 0.0) for u in range(grp)], axis=1)

    def split(x):
        hi = x.astype(BF16)
        return hi, (x - hi.astype(F32)).astype(BF16)

    def dot_split(a, b):
        return _dot(a[0], b[0]) + (_dot(a[0], b[1]) + _dot(a[1], b[0]))

    items = [(j, gi) for j in range(GDN_T // c) for gi in range(N_HEADS // grp)]
    pre = []
    for j, gi in items:
        rs = slice(j * c, (j + 1) * c)
        last = slice((j + 1) * c - 1, (j + 1) * c)
        gh = [gi * grp + u for u in range(grp)]
        gsl = slice(gi * gw, (gi + 1) * gw)
        qj, kj, vj = qf[rs, gsl], kf[rs, gsl], vc[rs, gsl]
        b_col = jnp.concatenate([tok[rs, L_GB + h:L_GB + h + 1] for h in gh], axis=0)
        g_col = jnp.concatenate([tok[rs, L_GA + h:L_GA + h + 1] for h in gh], axis=0)
        g_row = jnp.concatenate([tr_ref[0, L_GA + h:L_GA + h + 1, rs] for h in gh], axis=1)
        g_last = [tok[last, L_GA + h:L_GA + h + 1] for h in gh]
        g_last_col = jnp.concatenate([jnp.broadcast_to(g, (c, 1)) for g in g_last], axis=0)
        k_sp = spread(kj)
        q_sp = spread(qj)
        k_sp16 = k_sp.astype(BF16)
        decay = jnp.exp(jnp.where(tril, g_col - g_row, -jnp.inf))
        eg = jnp.exp(g_col)
        pre.append(dict(
            gh=gh, gsl=gsl,
            l_mat=b_col * _dot_nt(k_sp16, k_sp16) * decay,
            rhs=jnp.concatenate([stack(vj) * b_col, stack(kj) * (b_col * eg)], axis=1),
            qk=_dot_nt(q_sp.astype(BF16), k_sp16) * decay,
            q_dec=q_sp * eg,
            k_dec=k_sp * jnp.exp(g_last_col - g_col),
            e_last=jnp.concatenate([jnp.broadcast_to(jnp.exp(g), (HEAD_DIM, 1)) for g in g_last], axis=0)))

    t_inv = [eye - jnp.where(pair_masks[0], p["l_mat"], 0.0) for p in pre]
    for pm in pair_masks[1:]:
        t_s = [split(t) for t in t_inv]
        m_t = [dot_split(split(jnp.where(pm, p["l_mat"], 0.0)), ts) for p, ts in zip(pre, t_s)]
        t_inv = [t - dot_split(ts, split(m)) for t, ts, m in zip(t_inv, t_s, m_t)]
    sols = [dot_split(split(t), split(p["rhs"])) for t, p in zip(t_inv, pre)]

    outs = [[] for _ in heads]
    for p, sol in zip(pre, sols):
        u0 = sol[:, :HEAD_DIM]
        kcum = sol[:, HEAD_DIM:]
        st = state_scr[p["gsl"], :]
        stb = st.astype(BF16)
        v_new = u0 - _dot(spread_lanes(kcum).astype(BF16), stb)
        v_new_b = v_new.astype(BF16)
        o_st = _dot(p["q_dec"].astype(BF16), stb) + _dot(p["qk"].astype(BF16), v_new_b)
        state_scr[p["gsl"], :] = st * p["e_last"] + _dot(p["k_dec"].T.astype(BF16), v_new_b)
        for u, h in enumerate(p["gh"]):
            outs[h].append(o_st[u * c:(u + 1) * c, :])

    for h in heads:
        o = jnp.concatenate(outs[h], axis=0)
        ms = jnp.mean(o * o, axis=-1, keepdims=True)
        on = o * lax.rsqrt(ms + EPS) * ng_ref[...]
        o_ref[:, hsl[h]] = (on * _silu(z_ref[:, hsl[h]])).astype(BF16)


def gated_deltanet(proj, prep_tok, prep_tr, conv_w, norm_g, batch, seq):
    n = proj.shape[0]
    t = GDN_T
    nt = seq // t
    col = lambda off: pl.BlockSpec((t, WIDTH), lambda b, i: (b * nt + i, off // WIDTH))
    return pl.pallas_call(
        _gdn_kernel,
        grid=(batch, nt),
        in_specs=[col(C_GQ), col(C_GK), col(C_GV), col(C_GZ),
                  pl.BlockSpec((GDN_CONV, 3 * WIDTH), lambda b, i: (0, 0)),
                  pl.BlockSpec((1, HEAD_DIM), lambda b, i: (0, 0)),
                  pl.BlockSpec((t, LANE), lambda b, i: (b * nt + i, 0)),
                  pl.BlockSpec((1, 32, t), lambda b, i: (b, 0, i))],
        out_specs=pl.BlockSpec((t, WIDTH), lambda b, i: (b * nt + i, 0)),
        out_shape=jax.ShapeDtypeStruct((n, WIDTH), BF16),
        scratch_shapes=[pltpu.VMEM((t + GDN_HALO, WIDTH), F32),
                        pltpu.VMEM((t + GDN_HALO, WIDTH), F32),
                        pltpu.VMEM((t + GDN_HALO, WIDTH), F32),
                        pltpu.VMEM((N_HEADS * HEAD_DIM, HEAD_DIM), F32)],
        compiler_params=_cparams(("arbitrary", "arbitrary")),
        name="gated_deltanet",
    )(proj, proj, proj, proj, conv_w, norm_g.reshape(1, HEAD_DIM), prep_tok, prep_tr)


def _merge_kernel(h_ref, b0_ref, b1_ref, b2_ref, b3_ref, g0_ref, g1_ref, g2_ref, g3_ref,
                  wb_ref, o_ref):
    h = h_ref[...]
    acc = None
    for n, (b_ref, g_ref) in enumerate(zip((b0_ref, b1_ref, b2_ref, b3_ref),
                                           (g0_ref, g1_ref, g2_ref, g3_ref))):
        gate = jax.nn.sigmoid(_dot(h, g_ref[...]))
        term = gate * _dot(b_ref[...], wb_ref[n])
        acc = term if acc is None else acc + term
    o_ref[...] = acc.astype(BF16)


def merge_branches(h, branches, w_gate, w_branch, *, tm=512, tn=512):
    _, n, d = h.shape
    nj = d // tn
    bspec = pl.BlockSpec((tm, WIDTH), lambda j, i: (i, 0))
    gspec = lambda k: pl.BlockSpec((d, tn), lambda j, i: (0, k * nj + j))
    return pl.pallas_call(
        _merge_kernel,
        grid=(nj, n // tm),
        in_specs=[pl.BlockSpec((None, tm, d), lambda j, i: (0, i, 0)),
                  bspec, bspec, bspec, bspec,
                  gspec(0), gspec(1), gspec(2), gspec(3),
                  pl.BlockSpec((N_BRANCH, WIDTH, tn), lambda j, i: (0, 0, j))],
        out_specs=pl.BlockSpec((tm, tn), lambda j, i: (i, j)),
        out_shape=jax.ShapeDtypeStruct((n, d), BF16),
        compiler_params=_cparams(("arbitrary", "arbitrary")),
        name="merge_branches",
    )(h, *branches, w_gate, w_gate, w_gate, w_gate, w_branch)


def _resid_mm_kernel(a_ref, w_ref, x_ref, o_ref):
    o_ref[...] = x_ref[...] + _dot(a_ref[...], w_ref[...])


def resid_matmul(a, w, x, *, tm=512, tn=1024, name="resid_matmul"):
    n, k = a.shape
    d = w.shape[1]
    return pl.pallas_call(
        _resid_mm_kernel,
        grid=(d // tn, n // tm),
        in_specs=[pl.BlockSpec((tm, k), lambda j, i: (i, 0)),
                  pl.BlockSpec((k, tn), lambda j, i: (0, j)),
                  pl.BlockSpec((tm, tn), lambda j, i: (i, j))],
        out_specs=pl.BlockSpec((tm, tn), lambda j, i: (i, j)),
        out_shape=jax.ShapeDtypeStruct((n, d), F32),
        compiler_params=_cparams(("arbitrary", "arbitrary")),
        name=name,
    )(a, w, x)


FFN_HALO = 8
FFN_SUB = 512


def _ffn1_kernel(x_ref, g_ref, wg_ref, wu_ref, cw_ref, cb_ref, o_ref, h_scr, gt_scr, halo_scr,
                 *, tm, tiles_per_seq):
    i = pl.program_id(0)
    j = pl.program_id(1)

    @pl.when((i == 0) & (j == 0))
    def _():
        halo_scr[...] = jnp.zeros_like(halo_scr)

    @pl.when(j == 0)
    def _():
        def body(r, carry):
            rows = pl.ds(pl.multiple_of(r * NORM_ROWS, NORM_ROWS), NORM_ROWS)
            h_scr[rows, :] = _rmsnorm_rows(x_ref, g_ref, rows).astype(BF16)
            return carry
        lax.fori_loop(0, tm // NORM_ROWS, body, 0)

    h = h_scr[...]
    seq_start = (i % tiles_per_seq) == 0
    tn = o_ref.shape[1]
    for off in range(0, tn, FFN_SUB):
        cs = slice(off, min(off + FFN_SUB, tn))
        g = _dot(h, wg_ref[:, cs])
        gt_scr[FFN_HALO:, cs] = g
        gt_scr[0:FFN_HALO, cs] = jnp.where(seq_start, 0.0, halo_scr[j, :, cs])
        halo_scr[j, :, cs] = g[tm - FFN_HALO:, :]
        y = cb_ref[:, cs] + g * cw_ref[FFN_CONV - 1:FFN_CONV, cs]
        for t in range(FFN_CONV - 1):
            st = FFN_HALO - (FFN_CONV - 1) + t
            y = y + gt_scr[st:st + tm, cs] * cw_ref[t:t + 1, cs]
        o_ref[:, cs] = (_silu(y) * _dot(h, wu_ref[:, cs])).astype(BF16)


def conv_ffn_up(x, gain, w_gate, w_up, conv_w, conv_b, seq, *, tm=512, tn=1408):
    n, d = x.shape
    f = w_gate.shape[1]
    nj = f // tn
    return pl.pallas_call(
        functools.partial(_ffn1_kernel, tm=tm, tiles_per_seq=seq // tm),
        grid=(n // tm, nj),
        in_specs=[pl.BlockSpec((tm, d), lambda i, j: (i, 0)),
                  pl.BlockSpec((1, d), lambda i, j: (0, 0)),
                  pl.BlockSpec((d, tn), lambda i, j: (0, j)),
                  pl.BlockSpec((d, tn), lambda i, j: (0, j)),
                  pl.BlockSpec((FFN_CONV, tn), lambda i, j: (0, j)),
                  pl.BlockSpec((1, tn), lambda i, j: (0, j))],
        out_specs=pl.BlockSpec((tm, tn), lambda i, j: (i, j)),
        out_shape=jax.ShapeDtypeStruct((n, f), BF16),
        scratch_shapes=[pltpu.VMEM((tm, d), BF16),
                        pltpu.VMEM((tm + FFN_HALO, tn), F32),
                        pltpu.VMEM((nj, FFN_HALO, tn), F32)],
        compiler_params=_cparams(("arbitrary", "arbitrary")),
        name="conv_ffn_up",
    )(x, gain.reshape(1, d), w_gate, w_up, conv_w, conv_b.reshape(1, f))


IN_SIZES = (WIDTH, WIDTH, WIDTH, WIDTH,
            DSA_Q_RANK, HEAD_DIM, HEAD_DIM, IDX_DIM, IDX_HEADS,
            WIDTH, WIDTH, WIDTH, N_HEADS,
            WIDTH, WIDTH, WIDTH, WIDTH, N_HEADS, N_HEADS)
IN_NAMES = ("r_q", "r_k", "r_v", "r_g", "d_cq", "d_k", "d_v", "i_k", "i_w",
            "f_q", "f_k", "f_v", "f_f", "g_q", "g_k", "g_v", "g_z", "g_b", "g_a")
IN_PLAN = (("r_q", C_RQ), ("r_k", C_RK), ("r_v", C_RV), ("r_g", C_RG),
           ("f_q", C_FQ), ("f_k", C_FK), ("f_v", C_FV),
           ("g_q", C_GQ), ("g_k", C_GK), ("g_v", C_GV), ("g_z", C_GZ),
           ("d_k", C_DK), ("d_cq", C_DCQ), ("d_v", C_DV),
           ("i_k", C_KA), ("i_k", C_KB + IDX_DIM),
           ("i_w", C_SM + L_IW), ("f_f", C_SM + L_FF), ("g_b", C_SM + L_GB), ("g_a", C_SM + L_GA))


def _prep_w_in_kernel(w_ref, m_ref, g_ref):
    src = {}
    off = 0
    for name, size in zip(IN_NAMES, IN_SIZES):
        src[name] = (off, size)
        off += size
    m_ref[...] = jnp.zeros_like(m_ref)
    for name, dst in IN_PLAN:
        so, w = src[name]
        m_ref[:, dst:dst + w] = w_ref[:, so:so + w].astype(BF16)
    g_ref[...] = w_ref[:, off:off + g_ref.shape[1]].astype(BF16)


def prep_w_in(w_in, layer, *, tr=128):
    _, d, c = w_in.shape
    return pl.pallas_call(
        _prep_w_in_kernel,
        grid=(d // tr,),
        in_specs=[pl.BlockSpec((None, tr, c), lambda i: (layer, i, 0))],
        out_specs=[pl.BlockSpec((tr, C_TOT), lambda i: (i, 0)),
                   pl.BlockSpec((tr, N_BRANCH * d), lambda i: (i, 0))],
        out_shape=[jax.ShapeDtypeStruct((d, C_TOT), BF16),
                   jax.ShapeDtypeStruct((d, N_BRANCH * d), BF16)],
        compiler_params=_cparams(("arbitrary",)),
        name="prep_w_in",
    )(w_in)


def _cast_kernel(w_ref, o_ref):
    o_ref[...] = w_ref[...].astype(BF16)


def cast_layer(w, layer, *, tr=256):
    _, r, c = w.shape
    if r % tr:
        tr = r
    return pl.pallas_call(
        _cast_kernel,
        grid=(r // tr,),
        in_specs=[pl.BlockSpec((None, tr, c), lambda i: (layer, i, 0))],
        out_specs=pl.BlockSpec((tr, c), lambda i: (i, 0)),
        out_shape=jax.ShapeDtypeStruct((r, c), BF16),
        compiler_params=_cparams(("arbitrary",)),
        name="cast_bf16",
    )(w)


def cast_branch(w_branch, layer):
    _, nbr, r, c = w_branch.shape
    return pl.pallas_call(
        _cast_kernel,
        grid=(nbr,),
        in_specs=[pl.BlockSpec((None, None, r, c), lambda i: (layer, i, 0, 0))],
        out_specs=pl.BlockSpec((None, r, c), lambda i: (i, 0, 0)),
        out_shape=jax.ShapeDtypeStruct((nbr, r, c), BF16),
        compiler_params=_cparams(("arbitrary",)),
        name="cast_branch",
    )(w_branch)


def kernel(x, norm_mix, w_in, dsa_cq_norm, dsa_w_uq, dsa_w_qidx, fox_f_bias, gdn_conv, gdn_a_log,
           gdn_dt_bias, gdn_norm, w_branch, w_out, rel_bias, norm_ffn, ffn_w_gate, ffn_w_up,
           ffn_conv, ffn_conv_b, ffn_w_down, final_norm):
    batch, seq, d = x.shape
    depth = w_in.shape[0]
    xf = x.reshape(batch * seq, d)
    ret_tables = _retention_tables(seq)
    for l in range(depth):
        w_main, w_gate = prep_w_in(w_in, l)
        proj, h = norm_proj(xf, norm_mix[l], w_main)
        par = jnp.zeros((8, LANE), F32)
        par = par.at[0, L_FF:L_FF + N_HEADS].set(fox_f_bias[l])
        par = par.at[0, L_GA:L_GA + N_HEADS].set(gdn_dt_bias[l])
        par = par.at[1, L_GA:L_GA + N_HEADS].set(gdn_a_log[l])
        prep_tok, prep_tr = prep_small(proj, par, batch, seq)
        o_ret = retention(proj, ret_tables, batch, seq)
        q_dsa, q_idx = dsa_proj(proj, dsa_cq_norm[l], cast_layer(dsa_w_uq, l), cast_layer(dsa_w_qidx, l))
        o_dsa = dsa_attention(proj, q_dsa, q_idx, prep_tok, rel_bias, batch, seq)
        o_fox = fox_attention(proj, prep_tr, batch, seq)
        o_gdn = gated_deltanet(proj, prep_tok, prep_tr, gdn_conv[l], gdn_norm[l], batch, seq)
        merged = merge_branches(h, (o_ret, o_dsa, o_fox, o_gdn), w_gate, cast_branch(w_branch, l))
        xf = resid_matmul(merged, cast_layer(w_out, l), xf, tn=d, name="out_proj")
        act = conv_ffn_up(xf, norm_ffn[l], cast_layer(ffn_w_gate, l), cast_layer(ffn_w_up, l),
                          ffn_conv[l], ffn_conv_b[l], seq)
        xf = resid_matmul(act, cast_layer(ffn_w_down, l), xf, name="ffn_down")
    return rmsnorm(xf, final_norm).reshape(batch, seq, d)
```

```python
import functools
import math

import jax
import jax.numpy as jnp
from jax import lax
from jax.experimental import pallas as pl
from jax.experimental.pallas import tpu as pltpu

F32 = jnp.float32
BF16 = jnp.bfloat16
I32 = jnp.int32

HEAD_DIM = 128
N_HEADS = 4
WIDTH = N_HEADS * HEAD_DIM
N_BRANCH = 4
RET_CHUNK = 128
ROPE_BASE = 10000.0
DSA_Q_RANK = 384
IDX_HEADS = 16
IDX_DIM = 64
TOPK_MAX = 256
GDN_CONV = 4
GDN_CHUNK = 64
REL_BUCKETS = 32
REL_MAX_DIST = 128
FFN_CONV = 3
EPS = 1e-6

LANE = 128
SUBLANE = 8
VMEM_LIMIT = 56 * 1024 * 1024

C_RQ, C_RK, C_RV, C_RG = 0, 512, 1024, 1536
C_FQ, C_FK, C_FV = 2048, 2560, 3072
C_GQ, C_GK, C_GV, C_GZ = 3584, 4096, 4608, 5120
C_DK, C_DCQ, C_DV, C_KA, C_KB, C_SM = 5632, 5760, 6144, 6272, 6400, 6528
C_TOT = 6656
L_IW, L_FF, L_GB, L_GA = 0, 16, 20, 24

LOG2E = 1.4426950408889634
INT_MIN = -(2 ** 31)
INT_MAX = 2 ** 31 - 1
HIGHEST = lax.Precision.HIGHEST


def _cparams(sem, vmem=VMEM_LIMIT):
    return pltpu.CompilerParams(dimension_semantics=sem, vmem_limit_bytes=vmem)


def _dot(a, b):
    return jnp.dot(a, b, preferred_element_type=F32)


def _dot_nt(a, b):
    return lax.dot_general(a, b, (((1,), (1,)), ((), ())), preferred_element_type=F32)


def _silu(x):
    return x * jax.nn.sigmoid(x)


NORM_ROWS = 128


def _rmsnorm_rows(x_ref, g_ref, rows):
    x = x_ref[rows, :]
    ms = jnp.mean(x * x, axis=-1, keepdims=True)
    return x * lax.rsqrt(ms + EPS) * g_ref[...]


PROJ_SUB = 512


def _norm_proj_kernel(x_ref, g_ref, w_ref, o_ref, h_ref, h_scr, *, tm):
    def body(r, carry):
        rows = pl.ds(pl.multiple_of(r * NORM_ROWS, NORM_ROWS), NORM_ROWS)
        hb = _rmsnorm_rows(x_ref, g_ref, rows).astype(BF16)
        h_scr[rows, :] = hb
        h_ref[rows, :] = hb
        return carry
    lax.fori_loop(0, tm // NORM_ROWS, body, 0)

    h = h_scr[...]
    tn = o_ref.shape[1]
    for off in range(0, tn, PROJ_SUB):
        cs = slice(off, min(off + PROJ_SUB, tn))
        o_ref[:, cs] = _dot(h, w_ref[:, cs])


def norm_proj(x, gain, w, *, tm=512, col_parts=2):
    n, d = x.shape
    c = w.shape[1]
    tn = c // col_parts
    return pl.pallas_call(
        functools.partial(_norm_proj_kernel, tm=tm),
        grid=(col_parts, n // tm),
        in_specs=[pl.BlockSpec((tm, d), lambda j, i: (i, 0)),
                  pl.BlockSpec((1, d), lambda j, i: (0, 0)),
                  pl.BlockSpec((d, tn), lambda j, i: (0, j), pipeline_mode=pl.Buffered(1))],
        out_specs=[pl.BlockSpec((tm, tn), lambda j, i: (i, j)),
                   pl.BlockSpec((None, tm, d), lambda j, i: (j, i, 0))],
        out_shape=[jax.ShapeDtypeStruct((n, c), F32),
                   jax.ShapeDtypeStruct((col_parts, n, d), BF16)],
        scratch_shapes=[pltpu.VMEM((tm, d), BF16)],
        compiler_params=_cparams(("arbitrary", "arbitrary")),
        name="norm_proj",
    )(x, gain.reshape(1, d), w)


def _rmsnorm_kernel(x_ref, g_ref, o_ref, *, tm):
    def body(r, carry):
        rows = pl.ds(pl.multiple_of(r * NORM_ROWS, NORM_ROWS), NORM_ROWS)
        o_ref[rows, :] = _rmsnorm_rows(x_ref, g_ref, rows)
        return carry
    lax.fori_loop(0, tm // NORM_ROWS, body, 0)


def rmsnorm(x, gain, *, tm=512):
    n, d = x.shape
    return pl.pallas_call(
        functools.partial(_rmsnorm_kernel, tm=tm),
        grid=(n // tm,),
        in_specs=[pl.BlockSpec((tm, d), lambda i: (i, 0)),
                  pl.BlockSpec((1, d), lambda i: (0, 0))],
        out_specs=pl.BlockSpec((tm, d), lambda i: (i, 0)),
        out_shape=jax.ShapeDtypeStruct((n, d), F32),
        compiler_params=_cparams(("arbitrary",)),
        name="final_rmsnorm",
    )(x, gain.reshape(1, d))


def _prep_kernel(s_ref, par_ref, tok_ref, tr_ref, carry_scr):
    @pl.when(pl.program_id(1) == 0)
    def _():
        carry_scr[...] = jnp.zeros_like(carry_scr)

    s = s_ref[...]
    lane = lax.broadcasted_iota(I32, (LANE, LANE), 1)
    row = lax.broadcasted_iota(I32, (LANE, LANE), 0)
    z = s + par_ref[0:1, :]
    soft = jnp.maximum(z, 0.0) + jnp.log1p(jnp.exp(-jnp.abs(z)))
    log_sig = z - soft
    sig = jax.nn.sigmoid(z)
    g_val = -jnp.exp(par_ref[1:2, :]) * soft
    is_f = (lane >= L_FF) & (lane < L_FF + N_HEADS)
    is_b = (lane >= L_GB) & (lane < L_GB + N_HEADS)
    is_a = (lane >= L_GA) & (lane < L_GA + N_HEADS)
    pre = jnp.where(is_f, log_sig, jnp.where(is_a, g_val, 0.0))
    tri = (row >= lane).astype(F32)
    tri_blk = ((row >= lane) & ((row // GDN_CHUNK) == (lane // GDN_CHUNK))).astype(F32)
    cum_full = jnp.dot(tri, pre, precision=HIGHEST, preferred_element_type=F32)
    cum_blk = jnp.dot(tri_blk, pre, precision=HIGHEST, preferred_element_type=F32)
    c_fox = cum_full + carry_scr[0:1, :]
    carry_scr[0:1, :] = c_fox[LANE - 1:LANE, :]
    scale_iw = IDX_HEADS ** -0.5 * IDX_DIM ** -0.5
    out = jnp.where(is_f, c_fox,
                    jnp.where(is_a, cum_blk,
                              jnp.where(is_b, sig,
                                        jnp.where(lane < IDX_HEADS, s * scale_iw, 0.0))))
    tok_ref[...] = out
    tr_ref[0] = out.T[0:32, :]


def prep_small(proj, par, batch, seq):
    n = proj.shape[0]
    nc = seq // LANE
    return pl.pallas_call(
        _prep_kernel,
        grid=(batch, nc),
        in_specs=[pl.BlockSpec((LANE, LANE), lambda b, c: (b * nc + c, C_SM // LANE)),
                  pl.BlockSpec((8, LANE), lambda b, c: (0, 0))],
        out_specs=[pl.BlockSpec((LANE, LANE), lambda b, c: (b * nc + c, 0)),
                   pl.BlockSpec((1, 32, LANE), lambda b, c: (b, 0, c))],
        out_shape=[jax.ShapeDtypeStruct((n, LANE), F32),
                   jax.ShapeDtypeStruct((batch, 32, seq), F32)],
        scratch_shapes=[pltpu.VMEM((8, LANE), F32)],
        compiler_params=_cparams(("arbitrary", "arbitrary")),
        name="prep_small",
    )(proj, par)


def _ret_gamma():
    return [math.log1p(-(2.0 ** (-5.0 - h))) for h in range(N_HEADS)]


def _retention_kernel(q_ref, k_ref, v_ref, g_ref, cos_ref, sin_ref, dec_ref, zeta_ref, xi_ref,
                      o_ref, state_scr):
    @pl.when(pl.program_id(1) == 0)
    def _():
        state_scr[...] = jnp.zeros_like(state_scr)

    cos_t = cos_ref[...]
    sin_t = sin_ref[...]
    log_gamma = _ret_gamma()
    heads = range(N_HEADS)
    hsl = [slice(h * HEAD_DIM, (h + 1) * HEAD_DIM) for h in heads]

    def rope(x):
        return x * cos_t + pltpu.roll(x, HEAD_DIM // 2, 1) * sin_t

    qb = [rope(q_ref[:, s]).astype(BF16) for s in hsl]
    kr = [rope(k_ref[:, s]) * (HEAD_DIM ** -0.5) for s in hsl]
    kb = [x.astype(BF16) for x in kr]
    vb = [v_ref[:, s].astype(BF16) for s in hsl]
    st = [state_scr[h] for h in heads]
    inner = [(_dot_nt(qb[h], kb[h]) * dec_ref[h]).astype(BF16) for h in heads]
    cross = [_dot(qb[h], st[h].astype(BF16)) * xi_ref[h] for h in heads]
    kv = [_dot((kr[h] * zeta_ref[h]).T.astype(BF16), vb[h]) for h in heads]
    o = [_dot(inner[h], vb[h]) + cross[h] for h in heads]
    for h in heads:
        state_scr[h] = st[h] * math.exp(log_gamma[h] * RET_CHUNK) + kv[h]
        mu = jnp.mean(o[h], axis=-1, keepdims=True)
        oc = o[h] - mu
        var = jnp.mean(oc * oc, axis=-1, keepdims=True)
        o_ref[:, hsl[h]] = (_silu(g_ref[:, hsl[h]]) * (oc * lax.rsqrt(var + EPS))).astype(BF16)


def _retention_tables(seq):
    half = HEAD_DIM // 2
    inv = 1.0 / (ROPE_BASE ** (jnp.arange(half, dtype=F32) / half))
    ang = jnp.arange(seq).astype(F32)[:, None] * inv[None, :]
    cos, sin = jnp.cos(ang), jnp.sin(ang)
    cos_t = jnp.concatenate([cos, cos], axis=-1)
    sin_t = jnp.concatenate([-sin, sin], axis=-1)
    c = RET_CHUNK
    log_gamma = jnp.log1p(-jnp.exp2(-5.0 - jnp.arange(N_HEADS, dtype=F32)))
    n = jnp.arange(c, dtype=F32)
    diff = n[:, None] - n[None, :]
    decay = jnp.where(diff >= 0, jnp.exp(log_gamma[:, None, None] * jnp.maximum(diff, 0.0)), 0.0)
    zeta = jnp.exp(log_gamma[:, None] * (c - 1 - n)[None, :])
    xi = jnp.exp(log_gamma[:, None] * (n + 1)[None, :])
    ones = jnp.ones((1, 1, HEAD_DIM), F32)
    return cos_t, sin_t, decay, zeta[:, :, None] * ones, xi[:, :, None] * ones


def retention(proj, tables, batch, seq):
    n = proj.shape[0]
    c = RET_CHUNK
    nc = seq // c
    cos_t, sin_t, decay, zeta, xi = tables
    col = lambda off: pl.BlockSpec((c, WIDTH), lambda b, i: (b * nc + i, off // WIDTH))
    full3 = pl.BlockSpec((N_HEADS, c, HEAD_DIM), lambda b, i: (0, 0, 0))
    return pl.pallas_call(
        _retention_kernel,
        grid=(batch, nc),
        in_specs=[col(C_RQ), col(C_RK), col(C_RV), col(C_RG),
                  pl.BlockSpec((c, HEAD_DIM), lambda b, i: (i, 0)),
                  pl.BlockSpec((c, HEAD_DIM), lambda b, i: (i, 0)),
                  full3, full3, full3],
        out_specs=pl.BlockSpec((c, WIDTH), lambda b, i: (b * nc + i, 0)),
        out_shape=jax.ShapeDtypeStruct((n, WIDTH), BF16),
        scratch_shapes=[pltpu.VMEM((N_HEADS, HEAD_DIM, HEAD_DIM), F32)],
        compiler_params=_cparams(("arbitrary", "arbitrary")),
        name="retention",
    )(proj, proj, proj, proj, cos_t, sin_t, decay, zeta, xi)


def _fox_kernel(qi_ref, ki_ref, q_ref, k_ref, v_ref, ctr_ref, o_ref, m_scr, acc_scr, *, t):
    qi = qi_ref[pl.program_id(1)]
    ki = ki_ref[pl.program_id(1)]

    @pl.when(ki == 0)
    def _():
        m_scr[...] = jnp.full_like(m_scr, -jnp.inf)
        acc_scr[...] = jnp.zeros_like(acc_scr)

    def step(masked):
        if masked:
            row = lax.broadcasted_iota(I32, (t, t), 0)
            colm = lax.broadcasted_iota(I32, (t, t), 1)
            keep = row >= colm
        ones = jnp.ones((t, HEAD_DIM), BF16)
        for h in range(N_HEADS):
            sl = slice(h * HEAD_DIM, (h + 1) * HEAD_DIM)
            qb = q_ref[:, sl].astype(BF16)
            kb = k_ref[:, sl].astype(BF16)
            c_k = ctr_ref[0, L_FF + h:L_FF + h + 1, :] * LOG2E
            s = _dot_nt(qb, kb) * (HEAD_DIM ** -0.5 * LOG2E) - c_k
            if masked:
                s = jnp.where(keep, s, -jnp.inf)
            m_old = m_scr[h]
            m_new = jnp.maximum(m_old, jnp.max(s, axis=-1, keepdims=True))
            alpha = jnp.exp2(m_old - m_new)
            p = jnp.exp2(s - m_new)
            v_aug = jnp.concatenate([v_ref[:, sl].astype(BF16), ones], axis=1)
            acc_scr[h] = alpha * acc_scr[h] + _dot(p.astype(BF16), v_aug)
            m_scr[h] = m_new

    @pl.when(ki < qi)
    def _():
        step(False)

    @pl.when(ki == qi)
    def _():
        step(True)
        for h in range(N_HEADS):
            sl = slice(h * HEAD_DIM, (h + 1) * HEAD_DIM)
            acc = acc_scr[h]
            o_ref[:, sl] = (acc[:, :HEAD_DIM] / acc[:, HEAD_DIM:]).astype(BF16)


def fox_attention(proj, prep_tr, batch, seq, *, t=512):
    n = proj.shape[0]
    nt = seq // t
    pairs = [(qi, ki) for qi in range(nt) for ki in range(qi + 1)]
    qi_arr = jnp.asarray([p[0] for p in pairs], I32)
    ki_arr = jnp.asarray([p[1] for p in pairs], I32)
    qspec = pl.BlockSpec((t, WIDTH), lambda b, s, qi, ki: (b * nt + qi[s], C_FQ // WIDTH))
    kspec = lambda off: pl.BlockSpec(
        (t, WIDTH), lambda b, s, qi, ki: (b * nt + ki[s], off // WIDTH))
    return pl.pallas_call(
        functools.partial(_fox_kernel, t=t),
        grid_spec=pltpu.PrefetchScalarGridSpec(
            num_scalar_prefetch=2,
            grid=(batch, len(pairs)),
            in_specs=[qspec, kspec(C_FK), kspec(C_FV),
                      pl.BlockSpec((1, 32, t), lambda b, s, qi, ki: (b, 0, ki[s]))],
            out_specs=pl.BlockSpec((t, WIDTH), lambda b, s, qi, ki: (b * nt + qi[s], 0)),
            scratch_shapes=[pltpu.VMEM((N_HEADS, t, 1), F32),
                            pltpu.VMEM((N_HEADS, t, 2 * HEAD_DIM), F32)]),
        out_shape=jax.ShapeDtypeStruct((n, WIDTH), BF16),
        compiler_params=_cparams(("arbitrary", "arbitrary")),
        name="fox_attention",
    )(qi_arr, ki_arr, proj, proj, proj, prep_tr)


def _dsa_proj_kernel(cq_ref, g_ref, wq_ref, wi_ref, q_ref, qi_ref):
    x = cq_ref[...]
    ms = jnp.mean(x * x, axis=-1, keepdims=True)
    cb = (x * lax.rsqrt(ms + EPS) * g_ref[...]).astype(BF16)
    q_ref[...] = _dot(cb, wq_ref[...]).astype(BF16)
    qi_ref[...] = _dot(cb, wi_ref[...]).astype(BF16)


def dsa_proj(proj, cq_norm, w_uq, w_qidx, *, tm=512):
    n = proj.shape[0]
    r = DSA_Q_RANK
    wi = IDX_HEADS * IDX_DIM
    return pl.pallas_call(
        _dsa_proj_kernel,
        grid=(n // tm,),
        in_specs=[pl.BlockSpec((tm, r), lambda i: (i, C_DCQ // r)),
                  pl.BlockSpec((1, r), lambda i: (0, 0)),
                  pl.BlockSpec((r, WIDTH), lambda i: (0, 0)),
                  pl.BlockSpec((r, wi), lambda i: (0, 0))],
        out_specs=[pl.BlockSpec((tm, WIDTH), lambda i: (i, 0)),
                   pl.BlockSpec((tm, wi), lambda i: (i, 0))],
        out_shape=[jax.ShapeDtypeStruct((n, WIDTH), BF16),
                   jax.ShapeDtypeStruct((n, wi), BF16)],
        compiler_params=_cparams(("arbitrary",)),
        name="dsa_proj",
    )(proj, cq_norm.reshape(1, r), w_uq, w_qidx)


DSA_QB = 256
DSA_KC = 512
DSA_SCORE_MID_STEPS = 20
DSA_FEW_KEYS = 4
DSA_HALVE_FIXED = 12
DSA_WALK_FIXED = 3


def _t5_bucket(rel):
    max_exact = REL_BUCKETS // 2
    relf = jnp.maximum(rel, max_exact).astype(F32)
    large = max_exact + (jnp.log(relf / max_exact) / math.log(REL_MAX_DIST / max_exact)
                         * (REL_BUCKETS - max_exact)).astype(I32)
    large = jnp.minimum(large, REL_BUCKETS - 1)
    return jnp.where(rel < max_exact, rel, large)


def _dsa_kernel(rb_ref, q_ref, qi_ref, tok_ref, k_ref, v_ref, ka_ref, kb_ref, o_ref,
                key_scr, lg_scr, band_scr, kb16_scr, vt_scr, ka16_scr, kb16i_scr, *, seq, topk):
    qb_idx = pl.program_id(1)
    t0 = qb_idx * DSA_QB
    n_kc = (t0 + DSA_QB - 1) // DSA_KC + 1
    row_vec = (1, DSA_QB)

    @pl.when(qb_idx == 0)
    def _():
        kb16_scr[...] = k_ref[...].astype(BF16)
        ka16_scr[...] = ka_ref[...].astype(BF16)
        kb16i_scr[...] = kb_ref[...].astype(BF16)
        for c in range(seq // DSA_KC):
            cs = slice(c * DSA_KC, (c + 1) * DSA_KC)
            vt_scr[0:HEAD_DIM, cs] = v_ref[cs, :].T.astype(BF16)
        vt_scr[HEAD_DIM:, :] = jnp.ones((HEAD_DIM, seq), BF16)

    @pl.when((pl.program_id(0) == 0) & (qb_idx == 0))
    def _():
        j_ = lax.broadcasted_iota(I32, (2 * DSA_QB, DSA_QB), 0)
        i_ = lax.broadcasted_iota(I32, (2 * DSA_QB, DSA_QB), 1)
        rel = i_ + DSA_QB - j_
        bucket = _t5_bucket(rel)
        for h in range(N_HEADS):
            far = rb_ref[REL_BUCKETS - 1, h]
            band = jnp.zeros((2 * DSA_QB, DSA_QB), F32)
            for bk in range(REL_BUCKETS - 1):
                band = jnp.where(bucket == bk, (rb_ref[bk, h] - far) * LOG2E, band)
            band_scr[h] = jnp.where(rel >= 0, band, 0.0)

    w_t = tok_ref[...].T
    key_s = lax.broadcasted_iota(I32, (DSA_KC, DSA_QB), 0)
    row_t = t0 + lax.broadcasted_iota(I32, (DSA_KC, DSA_QB), 1)

    def score_chunk(c, carry):
        kmax, kmin = carry
        ks = pl.ds(pl.multiple_of(c * DSA_KC, DSA_KC), DSA_KC)
        ka = ka16_scr[ks, :]
        kb = kb16i_scr[ks, :]
        acc = jnp.zeros((DSA_KC, DSA_QB), F32)
        for p in range(IDX_HEADS // 2):
            qp = qi_ref[:, p * LANE:(p + 1) * LANE]
            acc = acc + jnp.maximum(_dot_nt(ka, qp), 0.0) * w_t[2 * p:2 * p + 1, :]
            acc = acc + jnp.maximum(_dot_nt(kb, qp), 0.0) * w_t[2 * p + 1:2 * p + 2, :]
        bits = pltpu.bitcast(acc, I32)
        key = bits ^ ((bits >> 31) & 0x7FFFFFFF)
        valid = (c * DSA_KC + key_s) <= row_t
        key_scr[ks, :] = jnp.where(valid, key, INT_MIN)
        kmax = jnp.maximum(kmax, jnp.max(jnp.where(valid, key, INT_MIN), axis=0, keepdims=True))
        kmin = jnp.minimum(kmin, jnp.min(jnp.where(valid, key, INT_MAX), axis=0, keepdims=True))
        return kmax, kmin

    kmax, kmin = lax.fori_loop(0, n_kc, score_chunk, (jnp.full(row_vec, INT_MIN, I32),
                                                     jnp.full(row_vec, INT_MAX, I32)))

    def scan_keys(cand, with_below):
        def body(c, carry):
            cnt, below = carry
            ks = pl.ds(pl.multiple_of(c * DSA_KC, DSA_KC), DSA_KC)
            keys = key_scr[ks, :]
            ge = keys >= cand
            ones = ge.astype(I32)
            low = jnp.where(ge, INT_MIN, keys)
            for u in range(DSA_KC // SUBLANE):
                us = slice(u * SUBLANE, (u + 1) * SUBLANE)
                cnt = cnt + ones[us, :]
                if with_below:
                    below = jnp.maximum(below, low[us, :])
            return cnt, below
        cnt, below = lax.fori_loop(0, n_kc, body, (jnp.zeros((SUBLANE, DSA_QB), I32),
                                                   jnp.full((SUBLANE, DSA_QB), INT_MIN, I32)))
        cnt = jnp.sum(cnt, axis=0, keepdims=True)
        if with_below:
            return cnt, jnp.max(below, axis=0, keepdims=True)
        return cnt

    def key_to_score(k):
        return pltpu.bitcast(k ^ ((k >> 31) & 0x7FFFFFFF), F32)

    def score_to_key(s):
        b = pltpu.bitcast(s, I32)
        return b ^ ((b >> 31) & 0x7FFFFFFF)

    def open_rows(lo, hi, c_lo):
        return (c_lo > topk) & (hi - 1 > lo)

    def any_row(flag):
        return jnp.max(jnp.where(flag, 1, 0))

    def update(cand, cnt, lo, hi, c_lo, c_hi):
        ge = cnt >= topk
        return (jnp.where(ge, cand, lo), jnp.where(ge, hi, cand),
                jnp.where(ge, cnt, c_lo), jnp.where(ge, c_hi, cnt))

    def crowded(lo, hi, c_lo, c_hi):
        return any_row(open_rows(lo, hi, c_lo) & (c_lo - c_hi > DSA_FEW_KEYS))

    def halve_step(it, lo, hi, c_lo, c_hi):
        key_mid = (lo >> 1) + (hi >> 1) + (lo & hi & 1)
        score_mid = score_to_key(0.5 * key_to_score(lo) + 0.5 * key_to_score(hi - 1))
        cand = jnp.where(it < DSA_SCORE_MID_STEPS, score_mid, key_mid)
        cand = jnp.minimum(jnp.maximum(cand, lo + 1), hi - 1)
        cand = jnp.where(hi - 1 > lo, cand, lo)
        return update(cand, scan_keys(cand, False), lo, hi, c_lo, c_hi)

    def halve_body(st):
        it, _, lo, hi, c_lo, c_hi = st
        go = crowded(lo, hi, c_lo, c_hi)
        return (it + 1, go) + halve_step(it, lo, hi, c_lo, c_hi)

    def walk_step(lo, hi, c_lo, c_hi, nxt):
        is_open = open_rows(lo, hi, c_lo)
        cand = jnp.where(is_open, nxt, lo)
        cnt, below = scan_keys(cand, True)
        ge = cnt >= topk
        hi = jnp.where(is_open, jnp.where(ge, cand + 1, cand), hi)
        c_hi = jnp.where(is_open & jnp.logical_not(ge), cnt, c_hi)
        lo = jnp.where(is_open & ge, cand, lo)
        c_lo = jnp.where(is_open & ge, cnt, c_lo)
        nxt = jnp.where(ge, nxt, below)
        return lo, hi, c_lo, c_hi, nxt

    def walk_body(st):
        go = any_row(open_rows(st[1], st[2], st[3]))
        return (go,) + walk_step(*st[1:])

    n_valid = jnp.minimum(t0 + lax.broadcasted_iota(I32, row_vec, 1) + 1, seq)
    st = (kmin, kmax + 1, n_valid, jnp.zeros(row_vec, I32))
    st = lax.fori_loop(0, DSA_HALVE_FIXED, lambda it, s: halve_step(it, *s), st)
    st = lax.while_loop(lambda s: s[1] > 0, halve_body,
                        (jnp.int32(DSA_HALVE_FIXED), crowded(*st)) + st)[2:]
    _, nxt0 = scan_keys(st[1], True)
    st = lax.fori_loop(0, DSA_WALK_FIXED, lambda it, s: walk_step(*s), st + (nxt0,))
    _, thr, hi, n_ge, n_gt, _ = lax.while_loop(
        lambda s: s[0] > 0, walk_body, (any_row(open_rows(st[0], st[1], st[2])),) + st)

    tied = n_ge > topk
    has_tie = jnp.max(jnp.where(tied, 1, 0)) > 0

    @pl.when(has_tie)
    def _():
        room = (topk - n_gt).astype(F32)
        ii = lax.broadcasted_iota(I32, (LANE, LANE), 0)
        jj = lax.broadcasted_iota(I32, (LANE, LANE), 1)
        lower = (ii >= jj).astype(BF16)

        def body(c, seen):
            ks = pl.ds(pl.multiple_of(c * LANE, LANE), LANE)
            kk = key_scr[ks, :]
            eq = kk == thr
            rank = seen + _dot(lower, eq.astype(BF16))
            drop = eq & (rank > room) & tied
            key_scr[ks, :] = jnp.where(drop, INT_MIN, kk)
            return seen + jnp.sum(eq.astype(F32), axis=0, keepdims=True)
        lax.fori_loop(0, n_kc * (DSA_KC // LANE), body, jnp.zeros(row_vec, F32))

    def mask_chunk(c, carry):
        ks = pl.ds(pl.multiple_of(c * DSA_KC, DSA_KC), DSA_KC)
        sel = jnp.where(key_scr[ks, :] >= thr, 0.0, -jnp.inf).astype(F32)
        key_scr[ks, :] = pltpu.bitcast(sel, I32)
        return carry
    lax.fori_loop(0, n_kc, mask_chunk, 0)

    heads = range(N_HEADS)
    hsl = [slice(h * HEAD_DIM, (h + 1) * HEAD_DIM) for h in heads]

    def logit_chunk(c, ms):
        ks = pl.ds(pl.multiple_of(c * DSA_KC, DSA_KC), DSA_KC)
        k_c = kb16_scr[ks, :]
        sel = pltpu.bitcast(key_scr[ks, :], F32)
        out = []
        for h in heads:
            s = _dot_nt(k_c, q_ref[:, hsl[h]]) * (HEAD_DIM ** -0.5 * LOG2E) + sel
            lg_scr[h, ks, :] = s
            out.append(jnp.maximum(ms[h], jnp.max(s, axis=0, keepdims=True)))
        return tuple(out)
    ms = lax.fori_loop(0, n_kc, logit_chunk,
                       tuple(jnp.full(row_vec, -jnp.inf, F32) for _ in heads))

    band_off = pl.multiple_of(jnp.maximum(qb_idx - 1, 0) * DSA_QB, DSA_QB)
    ws = pl.ds(band_off, 2 * DSA_QB)
    ms = list(ms)
    for h in heads:
        band_h = band_scr[h]
        band_first = jnp.concatenate([band_h[DSA_QB:, :], jnp.zeros((DSA_QB, DSA_QB), F32)], axis=0)
        win = lg_scr[h, ws, :] + jnp.where(qb_idx == 0, band_first, band_h)
        lg_scr[h, ws, :] = win
        ms[h] = jnp.maximum(ms[h], jnp.max(win, axis=0, keepdims=True))

    def pv_chunk(c, accs):
        ks = pl.ds(pl.multiple_of(c * DSA_KC, DSA_KC), DSA_KC)
        vt_c = vt_scr[:, ks]
        return tuple(accs[h] + _dot(vt_c, jnp.exp2(lg_scr[h, ks, :] - ms[h]).astype(BF16))
                     for h in heads)
    accs = lax.fori_loop(0, n_kc, pv_chunk,
                         tuple(jnp.zeros((2 * HEAD_DIM, DSA_QB), F32) for _ in heads))
    for h in heads:
        o_ref[:, hsl[h]] = (accs[h][:HEAD_DIM, :] / accs[h][HEAD_DIM:, :]).T.astype(BF16)


def dsa_attention(proj, q, q_idx, prep_tok, rel_bias, batch, seq):
    n = proj.shape[0]
    nq = seq // DSA_QB
    topk = min(TOPK_MAX, seq // 4)
    wi = IDX_HEADS * IDX_DIM
    rowblk = lambda w, cb: pl.BlockSpec((DSA_QB, w), lambda b, i: (b * nq + i, cb))
    seqblk = lambda off: pl.BlockSpec((seq, LANE), lambda b, i: (b, off // LANE))
    return pl.pallas_call(
        functools.partial(_dsa_kernel, seq=seq, topk=topk),
        grid=(batch, nq),
        in_specs=[pl.BlockSpec(memory_space=pltpu.SMEM),
                  rowblk(WIDTH, 0), rowblk(wi, 0), rowblk(LANE, 0),
                  seqblk(C_DK), seqblk(C_DV), seqblk(C_KA), seqblk(C_KB)],
        out_specs=pl.BlockSpec((DSA_QB, WIDTH), lambda b, i: (b * nq + i, 0)),
        out_shape=jax.ShapeDtypeStruct((n, WIDTH), BF16),
        scratch_shapes=[pltpu.VMEM((seq, DSA_QB), I32),
                        pltpu.VMEM((N_HEADS, seq, DSA_QB), F32),
                        pltpu.VMEM((N_HEADS, 2 * DSA_QB, DSA_QB), F32),
                        pltpu.VMEM((seq, LANE), BF16),
                        pltpu.VMEM((2 * HEAD_DIM, seq), BF16),
                        pltpu.VMEM((seq, LANE), BF16),
                        pltpu.VMEM((seq, LANE), BF16)],
        compiler_params=_cparams(("arbitrary", "arbitrary")),
        name="dsa_attention",
    )(rel_bias, q, q_idx, prep_tok, proj, proj, proj, proj)


GDN_T = 256
GDN_GROUP = 2
GDN_HALO = 8


def _gdn_kernel(q_ref, k_ref, v_ref, z_ref, cw_ref, ng_ref, tok_ref, tr_ref, o_ref,
                xq_scr, xk_scr, xv_scr, state_scr):
    first = pl.program_id(1) == 0

    @pl.when(first)
    def _():
        state_scr[...] = jnp.zeros_like(state_scr)
        for scr in (xq_scr, xk_scr, xv_scr):
            scr[0:GDN_HALO, :] = jnp.zeros((GDN_HALO, WIDTH), F32)

    def conv(x_ref, scr, w_off):
        scr[GDN_HALO:, :] = x_ref[...]
        y = jnp.zeros((GDN_T, WIDTH), F32)
        for i in range(GDN_CONV):
            st = GDN_HALO - (GDN_CONV - 1) + i
            y = y + scr[st:st + GDN_T, :] * cw_ref[i:i + 1, w_off:w_off + WIDTH]
        scr[0:GDN_HALO, :] = scr[GDN_T:GDN_T + GDN_HALO, :]
        return _silu(y)

    qc = conv(q_ref, xq_scr, 0)
    kc = conv(k_ref, xk_scr, WIDTH)
    vc = conv(v_ref, xv_scr, 2 * WIDTH)
    tok = tok_ref[...]
    c = GDN_CHUNK
    heads = range(N_HEADS)
    hsl = [slice(h * HEAD_DIM, (h + 1) * HEAD_DIM) for h in heads]

    def l2norm_heads(x, scale):
        return jnp.concatenate(
            [x[:, s] * (lax.rsqrt(jnp.sum(x[:, s] * x[:, s], axis=-1, keepdims=True) + EPS) * scale)
             for s in hsl], axis=1)

    qf = l2norm_heads(qc, HEAD_DIM ** -0.5)
    kf = l2norm_heads(kc, 1.0)

    grp = GDN_GROUP
    gw = grp * HEAD_DIM
    nb = grp * c
    ri = lax.broadcasted_iota(I32, (nb, nb), 0)
    ci = lax.broadcasted_iota(I32, (nb, nb), 1)
    tril = ((ri // c) == (ci // c)) & (ri >= ci)
    eye = (ri == ci).astype(F32)
    pair_masks = []
    for lg in range(c.bit_length() - 1):
        pair_masks.append(((ri >> (lg + 1)) == (ci >> (lg + 1)))
                          & (((ri >> lg) & 1) == 1) & (((ci >> lg) & 1) == 0))
    lane_head = lax.broadcasted_iota(I32, (c, gw), 1) // HEAD_DIM
    row_head = lax.broadcasted_iota(I32, (nb, HEAD_DIM), 0) // c

    def spread(x):
        return jnp.concatenate([jnp.where(lane_head == u, x, 0.0) for u in range(grp)], axis=0)

    def stack(x):
        return jnp.concatenate([x[:, hsl[u]] for u in range(grp)], axis=0)

    def spread_lanes(x):
        return jnp.concatenate([jnp.where(row_head == u, x, 0.0) for u in range(grp)], axis=1)

    def split(x):
        hi = x.astype(BF16)
        return hi, (x - hi.astype(F32)).astype(BF16)

    def dot_split(a, b):
        return _dot(a[0], b[0]) + (_dot(a[0], b[1]) + _dot(a[1], b[0]))

    items = [(j, gi) for j in range(GDN_T // c) for gi in range(N_HEADS // grp)]
    pre = []
    for j, gi in items:
        rs = slice(j * c, (j + 1) * c)
        last = slice((j + 1) * c - 1, (j + 1) * c)
        gh = [gi * grp + u for u in range(grp)]
        gsl = slice(gi * gw, (gi + 1) * gw)
        qj, kj, vj = qf[rs, gsl], kf[rs, gsl], vc[rs, gsl]
        b_col = jnp.concatenate([tok[rs, L_GB + h:L_GB + h + 1] for h in gh], axis=0)
        g_col = jnp.concatenate([tok[rs, L_GA + h:L_GA + h + 1] for h in gh], axis=0)
        g_row = jnp.concatenate([tr_ref[0, L_GA + h:L_GA + h + 1, rs] for h in gh], axis=1)
        g_last = [tok[last, L_GA + h:L_GA + h + 1] for h in gh]
        g_last_col = jnp.concatenate([jnp.broadcast_to(g, (c, 1)) for g in g_last], axis=0)
        k_sp = spread(kj)
        q_sp = spread(qj)
        k_sp16 = k_sp.astype(BF16)
        decay = jnp.exp(jnp.where(tril, g_col - g_row, -jnp.inf))
        eg = jnp.exp(g_col)
        pre.append(dict(
            gh=gh, gsl=gsl,
            l_mat=b_col * _dot_nt(k_sp16, k_sp16) * decay,
            rhs=jnp.concatenate([stack(vj) * b_col, stack(kj) * (b_col * eg)], axis=1),
            qk=_dot_nt(q_sp.astype(BF16), k_sp16) * decay,
            q_dec=q_sp * eg,
            k_dec=k_sp * jnp.exp(g_last_col - g_col),
            e_last=jnp.concatenate([jnp.broadcast_to(jnp.exp(g), (HEAD_DIM, 1)) for g in g_last], axis=0)))

    t_inv = [eye - jnp.where(pair_masks[0], p["l_mat"], 0.0) for p in pre]
    for pm in pair_masks[1:]:
        t_s = [split(t) for t in t_inv]
        m_t = [dot_split(split(jnp.where(pm, p["l_mat"], 0.0)), ts) for p, ts in zip(pre, t_s)]
        t_inv = [t - dot_split(ts, split(m)) for t, ts, m in zip(t_inv, t_s, m_t)]
    sols = [dot_split(split(t), split(p["rhs"])) for t, p in zip(t_inv, pre)]

    outs = [[] for _ in heads]
    for p, sol in zip(pre, sols):
        u0 = sol[:, :HEAD_DIM]
        kcum = sol[:, HEAD_DIM:]
        st = state_scr[p["gsl"], :]
        stb = st.astype(BF16)
        v_new = u0 - _dot(spread_lanes(kcum).astype(BF16), stb)
        v_new_b = v_new.astype(BF16)
        o_st = _dot(p["q_dec"].astype(BF16), stb) + _dot(p["qk"].astype(BF16), v_new_b)
        state_scr[p["gsl"], :] = st * p["e_last"] + _dot(p["k_dec"].T.astype(BF16), v_new_b)
        for u, h in enumerate(p["gh"]):
            outs[h].append(o_st[u * c:(u + 1) * c, :])

    for h in heads:
        o = jnp.concatenate(outs[h], axis=0)
        ms = jnp.mean(o * o, axis=-1, keepdims=True)
        on = o * lax.rsqrt(ms + EPS) * ng_ref[...]
        o_ref[:, hsl[h]] = (on * _silu(z_ref[:, hsl[h]])).astype(BF16)


def gated_deltanet(proj, prep_tok, prep_tr, conv_w, norm_g, batch, seq):
    n = proj.shape[0]
    t = GDN_T
    nt = seq // t
    col = lambda off: pl.BlockSpec((t, WIDTH), lambda b, i: (b * nt + i, off // WIDTH))
    return pl.pallas_call(
        _gdn_kernel,
        grid=(batch, nt),
        in_specs=[col(C_GQ), col(C_GK), col(C_GV), col(C_GZ),
                  pl.BlockSpec((GDN_CONV, 3 * WIDTH), lambda b, i: (0, 0)),
                  pl.BlockSpec((1, HEAD_DIM), lambda b, i: (0, 0)),
                  pl.BlockSpec((t, LANE), lambda b, i: (b * nt + i, 0)),
                  pl.BlockSpec((1, 32, t), lambda b, i: (b, 0, i))],
        out_specs=pl.BlockSpec((t, WIDTH), lambda b, i: (b * nt + i, 0)),
        out_shape=jax.ShapeDtypeStruct((n, WIDTH), BF16),
        scratch_shapes=[pltpu.VMEM((t + GDN_HALO, WIDTH), F32),
                        pltpu.VMEM((t + GDN_HALO, WIDTH), F32),
                        pltpu.VMEM((t + GDN_HALO, WIDTH), F32),
                        pltpu.VMEM((N_HEADS * HEAD_DIM, HEAD_DIM), F32)],
        compiler_params=_cparams(("arbitrary", "arbitrary")),
        name="gated_deltanet",
    )(proj, proj, proj, proj, conv_w, norm_g.reshape(1, HEAD_DIM), prep_tok, prep_tr)


def _merge_kernel(h_ref, b0_ref, b1_ref, b2_ref, b3_ref, g0_ref, g1_ref, g2_ref, g3_ref,
                  wb_ref, o_ref):
    h = h_ref[...]
    acc = None
    for n, (b_ref, g_ref) in enumerate(zip((b0_ref, b1_ref, b2_ref, b3_ref),
                                           (g0_ref, g1_ref, g2_ref, g3_ref))):
        gate = jax.nn.sigmoid(_dot(h, g_ref[...]))
        term = gate * _dot(b_ref[...], wb_ref[n])
        acc = term if acc is None else acc + term
    o_ref[...] = acc.astype(BF16)


def merge_branches(h, branches, w_gate, w_branch, *, tm=512, tn=512):
    _, n, d = h.shape
    nj = d // tn
    bspec = pl.BlockSpec((tm, WIDTH), lambda j, i: (i, 0))
    gspec = lambda k: pl.BlockSpec((d, tn), lambda j, i: (0, k * nj + j))
    return pl.pallas_call(
        _merge_kernel,
        grid=(nj, n // tm),
        in_specs=[pl.BlockSpec((None, tm, d), lambda j, i: (0, i, 0)),
                  bspec, bspec, bspec, bspec,
                  gspec(0), gspec(1), gspec(2), gspec(3),
                  pl.BlockSpec((N_BRANCH, WIDTH, tn), lambda j, i: (0, 0, j))],
        out_specs=pl.BlockSpec((tm, tn), lambda j, i: (i, j)),
        out_shape=jax.ShapeDtypeStruct((n, d), BF16),
        compiler_params=_cparams(("arbitrary", "arbitrary")),
        name="merge_branches",
    )(h, *branches, w_gate, w_gate, w_gate, w_gate, w_branch)


def _resid_mm_kernel(a_ref, w_ref, x_ref, o_ref):
    o_ref[...] = x_ref[...] + _dot(a_ref[...], w_ref[...])


def resid_matmul(a, w, x, *, tm=512, tn=1024, name="resid_matmul"):
    n, k = a.shape
    d = w.shape[1]
    return pl.pallas_call(
        _resid_mm_kernel,
        grid=(d // tn, n // tm),
        in_specs=[pl.BlockSpec((tm, k), lambda j, i: (i, 0)),
                  pl.BlockSpec((k, tn), lambda j, i: (0, j)),
                  pl.BlockSpec((tm, tn), lambda j, i: (i, j))],
        out_specs=pl.BlockSpec((tm, tn), lambda j, i: (i, j)),
        out_shape=jax.ShapeDtypeStruct((n, d), F32),
        compiler_params=_cparams(("arbitrary", "arbitrary")),
        name=name,
    )(a, w, x)


FFN_HALO = 8
FFN_SUB = 512


def _ffn1_kernel(x_ref, g_ref, wg_ref, wu_ref, cw_ref, cb_ref, o_ref, h_scr, gt_scr, halo_scr,
                 *, tm, tiles_per_seq):
    i = pl.program_id(0)
    j = pl.program_id(1)

    @pl.when((i == 0) & (j == 0))
    def _():
        halo_scr[...] = jnp.zeros_like(halo_scr)

    @pl.when(j == 0)
    def _():
        def body(r, carry):
            rows = pl.ds(pl.multiple_of(r * NORM_ROWS, NORM_ROWS), NORM_ROWS)
            h_scr[rows, :] = _rmsnorm_rows(x_ref, g_ref, rows).astype(BF16)
            return carry
        lax.fori_loop(0, tm // NORM_ROWS, body, 0)

    h = h_scr[...]
    seq_start = (i % tiles_per_seq) == 0
    tn = o_ref.shape[1]
    for off in range(0, tn, FFN_SUB):
        cs = slice(off, min(off + FFN_SUB, tn))
        g = _dot(h, wg_ref[:, cs])
        gt_scr[FFN_HALO:, cs] = g
        gt_scr[0:FFN_HALO, cs] = jnp.where(seq_start, 0.0, halo_scr[j, :, cs])
        halo_scr[j, :, cs] = g[tm - FFN_HALO:, :]
        y = cb_ref[:, cs] + g * cw_ref[FFN_CONV - 1:FFN_CONV, cs]
        for t in range(FFN_CONV - 1):
            st = FFN_HALO - (FFN_CONV - 1) + t
            y = y + gt_scr[st:st + tm, cs] * cw_ref[t:t + 1, cs]
        o_ref[:, cs] = (_silu(y) * _dot(h, wu_ref[:, cs])).astype(BF16)


def conv_ffn_up(x, gain, w_gate, w_up, conv_w, conv_b, seq, *, tm=512, tn=1408):
    n, d = x.shape
    f = w_gate.shape[1]
    nj = f // tn
    return pl.pallas_call(
        functools.partial(_ffn1_kernel, tm=tm, tiles_per_seq=seq // tm),
        grid=(n // tm, nj),
        in_specs=[pl.BlockSpec((tm, d), lambda i, j: (i, 0)),
                  pl.BlockSpec((1, d), lambda i, j: (0, 0)),
                  pl.BlockSpec((d, tn), lambda i, j: (0, j)),
                  pl.BlockSpec((d, tn), lambda i, j: (0, j)),
                  pl.BlockSpec((FFN_CONV, tn), lambda i, j: (0, j)),
                  pl.BlockSpec((1, tn), lambda i, j: (0, j))],
        out_specs=pl.BlockSpec((tm, tn), lambda i, j: (i, j)),
        out_shape=jax.ShapeDtypeStruct((n, f), BF16),
        scratch_shapes=[pltpu.VMEM((tm, d), BF16),
                        pltpu.VMEM((tm + FFN_HALO, tn), F32),
                        pltpu.VMEM((nj, FFN_HALO, tn), F32)],
        compiler_params=_cparams(("arbitrary", "arbitrary")),
        name="conv_ffn_up",
    )(x, gain.reshape(1, d), w_gate, w_up, conv_w, conv_b.reshape(1, f))


IN_SIZES = (WIDTH, WIDTH, WIDTH, WIDTH,
            DSA_Q_RANK, HEAD_DIM, HEAD_DIM, IDX_DIM, IDX_HEADS,
            WIDTH, WIDTH, WIDTH, N_HEADS,
            WIDTH, WIDTH, WIDTH, WIDTH, N_HEADS, N_HEADS)
IN_NAMES = ("r_q", "r_k", "r_v", "r_g", "d_cq", "d_k", "d_v", "i_k", "i_w",
            "f_q", "f_k", "f_v", "f_f", "g_q", "g_k", "g_v", "g_z", "g_b", "g_a")
IN_PLAN = (("r_q", C_RQ), ("r_k", C_RK), ("r_v", C_RV), ("r_g", C_RG),
           ("f_q", C_FQ), ("f_k", C_FK), ("f_v", C_FV),
           ("g_q", C_GQ), ("g_k", C_GK), ("g_v", C_GV), ("g_z", C_GZ),
           ("d_k", C_DK), ("d_cq", C_DCQ), ("d_v", C_DV),
           ("i_k", C_KA), ("i_k", C_KB + IDX_DIM),
           ("i_w", C_SM + L_IW), ("f_f", C_SM + L_FF), ("g_b", C_SM + L_GB), ("g_a", C_SM + L_GA))


def _prep_w_in_kernel(w_ref, m_ref, g_ref):
    src = {}
    off = 0
    for name, size in zip(IN_NAMES, IN_SIZES):
        src[name] = (off, size)
        off += size
    m_ref[...] = jnp.zeros_like(m_ref)
    for name, dst in IN_PLAN:
        so, w = src[name]
        m_ref[:, dst:dst + w] = w_ref[:, so:so + w].astype(BF16)
    g_ref[...] = w_ref[:, off:off + g_ref.shape[1]].astype(BF16)


def prep_w_in(w_in, layer, *, tr=128):
    _, d, c = w_in.shape
    return pl.pallas_call(
        _prep_w_in_kernel,
        grid=(d // tr,),
        in_specs=[pl.BlockSpec((None, tr, c), lambda i: (layer, i, 0))],
        out_specs=[pl.BlockSpec((tr, C_TOT), lambda i: (i, 0)),
                   pl.BlockSpec((tr, N_BRANCH * d), lambda i: (i, 0))],
        out_shape=[jax.ShapeDtypeStruct((d, C_TOT), BF16),
                   jax.ShapeDtypeStruct((d, N_BRANCH * d), BF16)],
        compiler_params=_cparams(("arbitrary",)),
        name="prep_w_in",
    )(w_in)


def _cast_kernel(w_ref, o_ref):
    o_ref[...] = w_ref[...].astype(BF16)


def cast_layer(w, layer, *, tr=256):
    _, r, c = w.shape
    if r % tr:
        tr = r
    return pl.pallas_call(
        _cast_kernel,
        grid=(r // tr,),
        in_specs=[pl.BlockSpec((None, tr, c), lambda i: (layer, i, 0))],
        out_specs=pl.BlockSpec((tr, c), lambda i: (i, 0)),
        out_shape=jax.ShapeDtypeStruct((r, c), BF16),
        compiler_params=_cparams(("arbitrary",)),
        name="cast_bf16",
    )(w)


def cast_branch(w_branch, layer):
    _, nbr, r, c = w_branch.shape
    return pl.pallas_call(
        _cast_kernel,
        grid=(nbr,),
        in_specs=[pl.BlockSpec((None, None, r, c), lambda i: (layer, i, 0, 0))],
        out_specs=pl.BlockSpec((None, r, c), lambda i: (i, 0, 0)),
        out_shape=jax.ShapeDtypeStruct((nbr, r, c), BF16),
        compiler_params=_cparams(("arbitrary",)),
        name="cast_branch",
    )(w_branch)


def kernel(x, norm_mix, w_in, dsa_cq_norm, dsa_w_uq, dsa_w_qidx, fox_f_bias, gdn_conv, gdn_a_log,
           gdn_dt_bias, gdn_norm, w_branch, w_out, rel_bias, norm_ffn, ffn_w_gate, ffn_w_up,
           ffn_conv, ffn_conv_b, ffn_w_down, final_norm):
    batch, seq, d = x.shape
    depth = w_in.shape[0]
    xf = x.reshape(batch * seq, d)
    ret_tables = _retention_tables(seq)
    for l in range(depth):
        w_main, w_gate = prep_w_in(w_in, l)
        proj, h = norm_proj(xf, norm_mix[l], w_main)
        par = jnp.zeros((8, LANE), F32)
        par = par.at[0, L_FF:L_FF + N_HEADS].set(fox_f_bias[l])
        par = par.at[0, L_GA:L_GA + N_HEADS].set(gdn_dt_bias[l])
        par = par.at[1, L_GA:L_GA + N_HEADS].set(gdn_a_log[l])
        prep_tok, prep_tr = prep_small(proj, par, batch, seq)
        o_ret = retention(proj, ret_tables, batch, seq)
        q_dsa, q_idx = dsa_proj(proj, dsa_cq_norm[l], cast_layer(dsa_w_uq, l), cast_layer(dsa_w_qidx, l))
        o_dsa = dsa_attention(proj, q_dsa, q_idx, prep_tok, rel_bias, batch, seq)
        o_fox = fox_attention(proj, prep_tr, batch, seq)
        o_gdn = gated_deltanet(proj, prep_tok, prep_tr, gdn_conv[l], gdn_norm[l], batch, seq)
        merged = merge_branches(h, (o_ret, o_dsa, o_fox, o_gdn), w_gate, cast_branch(w_branch, l))
        xf = resid_matmul(merged, cast_layer(w_out, l), xf, tn=d, name="out_proj")
        act = conv_ffn_up(xf, norm_ffn[l], cast_layer(ffn_w_gate, l), cast_layer(ffn_w_up, l),
                          ffn_conv[l], ffn_conv_b[l], seq)
        xf = resid_matmul(act, cast_layer(ffn_w_down, l), xf, name="ffn_down")
    return rmsnorm(xf, final_norm).reshape(batch, seq, d)
```

```python
import functools
import math

import jax
import jax.numpy as jnp
from jax import lax
from jax.experimental import pallas as pl
from jax.experimental.pallas import tpu as pltpu

F32 = jnp.float32
BF16 = jnp.bfloat16
I32 = jnp.int32

HEAD_DIM = 128
N_HEADS = 4
WIDTH = N_HEADS * HEAD_DIM
N_BRANCH = 4
RET_CHUNK = 128
ROPE_BASE = 10000.0
DSA_Q_RANK = 384
IDX_HEADS = 16
IDX_DIM = 64
TOPK_MAX = 256
GDN_CONV = 4
GDN_CHUNK = 64
REL_BUCKETS = 32
REL_MAX_DIST = 128
FFN_CONV = 3
EPS = 1e-6

LANE = 128
SUBLANE = 8
VMEM_LIMIT = 56 * 1024 * 1024

C_RQ, C_RK, C_RV, C_RG = 0, 512, 1024, 1536
C_FQ, C_FK, C_FV = 2048, 2560, 3072
C_GQ, C_GK, C_GV, C_GZ = 3584, 4096, 4608, 5120
C_DK, C_DCQ, C_DV, C_KA, C_KB, C_SM = 5632, 5760, 6144, 6272, 6400, 6528
C_TOT = 6656
L_IW, L_FF, L_GB, L_GA = 0, 16, 20, 24

LOG2E = 1.4426950408889634
INT_MIN = -(2 ** 31)
INT_MAX = 2 ** 31 - 1
HIGHEST = lax.Precision.HIGHEST


def _cparams(sem, vmem=VMEM_LIMIT):
    return pltpu.CompilerParams(dimension_semantics=sem, vmem_limit_bytes=vmem)


def _dot(a, b):
    return jnp.dot(a, b, preferred_element_type=F32)


def _dot_nt(a, b):
    return lax.dot_general(a, b, (((1,), (1,)), ((), ())), preferred_element_type=F32)


def _silu(x):
    return x * jax.nn.sigmoid(x)


NORM_ROWS = 128


def _rmsnorm_rows(x_ref, g_ref, rows):
    x = x_ref[rows, :]
    ms = jnp.mean(x * x, axis=-1, keepdims=True)
    return x * lax.rsqrt(ms + EPS) * g_ref[...]


PROJ_SUB = 512


def _norm_proj_kernel(x_ref, g_ref, w_ref, o_ref, h_ref, h_scr, *, tm):
    def body(r, carry):
        rows = pl.ds(pl.multiple_of(r * NORM_ROWS, NORM_ROWS), NORM_ROWS)
        hb = _rmsnorm_rows(x_ref, g_ref, rows).astype(BF16)
        h_scr[rows, :] = hb
        h_ref[rows, :] = hb
        return carry
    lax.fori_loop(0, tm // NORM_ROWS, body, 0)

    h = h_scr[...]
    tn = o_ref.shape[1]
    for off in range(0, tn, PROJ_SUB):
        cs = slice(off, min(off + PROJ_SUB, tn))
        o_ref[:, cs] = _dot(h, w_ref[:, cs])


def norm_proj(x, gain, w, *, tm=512, col_parts=2):
    n, d = x.shape
    c = w.shape[1]
    tn = c // col_parts
    return pl.pallas_call(
        functools.partial(_norm_proj_kernel, tm=tm),
        grid=(col_parts, n // tm),
        in_specs=[pl.BlockSpec((tm, d), lambda j, i: (i, 0)),
                  pl.BlockSpec((1, d), lambda j, i: (0, 0)),
                  pl.BlockSpec((d, tn), lambda j, i: (0, j), pipeline_mode=pl.Buffered(1))],
        out_specs=[pl.BlockSpec((tm, tn), lambda j, i: (i, j)),
                   pl.BlockSpec((None, tm, d), lambda j, i: (j, i, 0))],
        out_shape=[jax.ShapeDtypeStruct((n, c), F32),
                   jax.ShapeDtypeStruct((col_parts, n, d), BF16)],
        scratch_shapes=[pltpu.VMEM((tm, d), BF16)],
        compiler_params=_cparams(("arbitrary", "arbitrary")),
        name="norm_proj",
    )(x, gain.reshape(1, d), w)


def _rmsnorm_kernel(x_ref, g_ref, o_ref, *, tm):
    def body(r, carry):
        rows = pl.ds(pl.multiple_of(r * NORM_ROWS, NORM_ROWS), NORM_ROWS)
        o_ref[rows, :] = _rmsnorm_rows(x_ref, g_ref, rows)
        return carry
    lax.fori_loop(0, tm // NORM_ROWS, body, 0)


def rmsnorm(x, gain, *, tm=512):
    n, d = x.shape
    return pl.pallas_call(
        functools.partial(_rmsnorm_kernel, tm=tm),
        grid=(n // tm,),
        in_specs=[pl.BlockSpec((tm, d), lambda i: (i, 0)),
                  pl.BlockSpec((1, d), lambda i: (0, 0))],
        out_specs=pl.BlockSpec((tm, d), lambda i: (i, 0)),
        out_shape=jax.ShapeDtypeStruct((n, d), F32),
        compiler_params=_cparams(("arbitrary",)),
        name="final_rmsnorm",
    )(x, gain.reshape(1, d))


def _prep_kernel(s_ref, par_ref, tok_ref, tr_ref, carry_scr):
    @pl.when(pl.program_id(1) == 0)
    def _():
        carry_scr[...] = jnp.zeros_like(carry_scr)

    s = s_ref[...]
    lane = lax.broadcasted_iota(I32, (LANE, LANE), 1)
    row = lax.broadcasted_iota(I32, (LANE, LANE), 0)
    z = s + par_ref[0:1, :]
    soft = jnp.maximum(z, 0.0) + jnp.log1p(jnp.exp(-jnp.abs(z)))
    log_sig = z - soft
    sig = jax.nn.sigmoid(z)
    g_val = -jnp.exp(par_ref[1:2, :]) * soft
    is_f = (lane[0:1] >= L_FF) & (lane[0:1] < L_FF + N_HEADS)
    is_b = (lane[0:1] >= L_GB) & (lane[0:1] < L_GB + N_HEADS)
    is_a = (lane[0:1] >= L_GA) & (lane[0:1] < L_GA + N_HEADS)
    pre = jnp.where(is_f, log_sig, jnp.where(is_a, g_val, 0.0))
    tri = (row >= lane).astype(F32)
    tri_blk = ((row >= lane) & ((row // GDN_CHUNK) == (lane // GDN_CHUNK))).astype(F32)
    subs = [slice(u * LANE, (u + 1) * LANE) for u in range(PREP_T // LANE)]
    cum_full = [jnp.dot(tri, pre[u], precision=HIGHEST, preferred_element_type=F32) for u in subs]
    cum_blk = [jnp.dot(tri_blk, pre[u], precision=HIGHEST, preferred_element_type=F32) for u in subs]
    scale_iw = IDX_HEADS ** -0.5 * IDX_DIM ** -0.5
    carry = carry_scr[0:1, :]
    for u, cf, cb in zip(subs, cum_full, cum_blk):
        c_fox = cf + carry
        carry = c_fox[LANE - 1:LANE, :]
        out = jnp.where(is_f, c_fox,
                        jnp.where(is_a, cb,
                                  jnp.where(is_b, sig[u],
                                            jnp.where(lane[0:1] < IDX_HEADS, s[u] * scale_iw, 0.0))))
        tok_ref[u, :] = out
        tr_ref[0, :, u] = out.T[0:PREP_ROWS, :]
    carry_scr[0:1, :] = carry


PREP_T = 512
PREP_ROWS = 32


def prep_small(proj, par, batch, seq):
    n = proj.shape[0]
    nc = seq // PREP_T
    return pl.pallas_call(
        _prep_kernel,
        grid=(batch, nc),
        in_specs=[pl.BlockSpec((PREP_T, LANE), lambda b, c: (b * nc + c, C_SM // LANE)),
                  pl.BlockSpec((SUBLANE, LANE), lambda b, c: (0, 0))],
        out_specs=[pl.BlockSpec((PREP_T, LANE), lambda b, c: (b * nc + c, 0)),
                   pl.BlockSpec((1, PREP_ROWS, PREP_T), lambda b, c: (b, 0, c))],
        out_shape=[jax.ShapeDtypeStruct((n, LANE), F32),
                   jax.ShapeDtypeStruct((batch, PREP_ROWS, seq), F32)],
        scratch_shapes=[pltpu.VMEM((SUBLANE, LANE), F32)],
        compiler_params=_cparams(("arbitrary", "arbitrary")),
        name="prep_small",
    )(proj, par)


def _ret_gamma():
    return [math.log1p(-(2.0 ** (-5.0 - h))) for h in range(N_HEADS)]


def _retention_kernel(q_ref, k_ref, v_ref, g_ref, cos_ref, sin_ref, dec_ref, zeta_ref, xi_ref,
                      o_ref, state_scr):
    @pl.when(pl.program_id(1) == 0)
    def _():
        state_scr[...] = jnp.zeros_like(state_scr)

    cos_t = cos_ref[...]
    sin_t = sin_ref[...]
    log_gamma = _ret_gamma()
    heads = range(N_HEADS)
    hsl = [slice(h * HEAD_DIM, (h + 1) * HEAD_DIM) for h in heads]

    def rope(x):
        return x * cos_t + pltpu.roll(x, HEAD_DIM // 2, 1) * sin_t

    qb = [rope(q_ref[:, s]).astype(BF16) for s in hsl]
    kr = [rope(k_ref[:, s]) * (HEAD_DIM ** -0.5) for s in hsl]
    kb = [x.astype(BF16) for x in kr]
    vb = [v_ref[:, s].astype(BF16) for s in hsl]
    st = [state_scr[h] for h in heads]
    inner = [(_dot_nt(qb[h], kb[h]) * dec_ref[h]).astype(BF16) for h in heads]
    cross = [_dot(qb[h], st[h].astype(BF16)) * xi_ref[h] for h in heads]
    kv = [_dot((kr[h] * zeta_ref[h]).T.astype(BF16), vb[h]) for h in heads]
    o = [_dot(inner[h], vb[h]) + cross[h] for h in heads]
    for h in heads:
        state_scr[h] = st[h] * math.exp(log_gamma[h] * RET_CHUNK) + kv[h]
        mu = jnp.mean(o[h], axis=-1, keepdims=True)
        oc = o[h] - mu
        var = jnp.mean(oc * oc, axis=-1, keepdims=True)
        o_ref[:, hsl[h]] = (_silu(g_ref[:, hsl[h]]) * (oc * lax.rsqrt(var + EPS))).astype(BF16)


def _retention_tables(seq):
    half = HEAD_DIM // 2
    inv = 1.0 / (ROPE_BASE ** (jnp.arange(half, dtype=F32) / half))
    ang = jnp.arange(seq).astype(F32)[:, None] * inv[None, :]
    cos, sin = jnp.cos(ang), jnp.sin(ang)
    cos_t = jnp.concatenate([cos, cos], axis=-1)
    sin_t = jnp.concatenate([-sin, sin], axis=-1)
    c = RET_CHUNK
    log_gamma = jnp.log1p(-jnp.exp2(-5.0 - jnp.arange(N_HEADS, dtype=F32)))
    n = jnp.arange(c, dtype=F32)
    diff = n[:, None] - n[None, :]
    decay = jnp.where(diff >= 0, jnp.exp(log_gamma[:, None, None] * jnp.maximum(diff, 0.0)), 0.0)
    zeta = jnp.exp(log_gamma[:, None] * (c - 1 - n)[None, :])
    xi = jnp.exp(log_gamma[:, None] * (n + 1)[None, :])
    ones = jnp.ones((1, 1, HEAD_DIM), F32)
    return cos_t, sin_t, decay, zeta[:, :, None] * ones, xi[:, :, None] * ones


def retention(proj, tables, batch, seq):
    n = proj.shape[0]
    c = RET_CHUNK
    nc = seq // c
    cos_t, sin_t, decay, zeta, xi = tables
    col = lambda off: pl.BlockSpec((c, WIDTH), lambda b, i: (b * nc + i, off // WIDTH))
    full3 = pl.BlockSpec((N_HEADS, c, HEAD_DIM), lambda b, i: (0, 0, 0))
    return pl.pallas_call(
        _retention_kernel,
        grid=(batch, nc),
        in_specs=[col(C_RQ), col(C_RK), col(C_RV), col(C_RG),
                  pl.BlockSpec((c, HEAD_DIM), lambda b, i: (i, 0)),
                  pl.BlockSpec((c, HEAD_DIM), lambda b, i: (i, 0)),
                  full3, full3, full3],
        out_specs=pl.BlockSpec((c, WIDTH), lambda b, i: (b * nc + i, 0)),
        out_shape=jax.ShapeDtypeStruct((n, WIDTH), BF16),
        scratch_shapes=[pltpu.VMEM((N_HEADS, HEAD_DIM, HEAD_DIM), F32)],
        compiler_params=_cparams(("arbitrary", "arbitrary")),
        name="retention",
    )(proj, proj, proj, proj, cos_t, sin_t, decay, zeta, xi)


def _fox_kernel(qi_ref, ki_ref, q_ref, k_ref, v_ref, ctr_ref, o_ref, m_scr, acc_scr, *, t):
    qi = qi_ref[pl.program_id(1)]
    ki = ki_ref[pl.program_id(1)]

    @pl.when(ki == 0)
    def _():
        m_scr[...] = jnp.full_like(m_scr, -jnp.inf)
        acc_scr[...] = jnp.zeros_like(acc_scr)

    def step(masked):
        if masked:
            row = lax.broadcasted_iota(I32, (t, t), 0)
            colm = lax.broadcasted_iota(I32, (t, t), 1)
            keep = row >= colm
        ones = jnp.ones((t, HEAD_DIM), BF16)
        for h in range(N_HEADS):
            sl = slice(h * HEAD_DIM, (h + 1) * HEAD_DIM)
            qb = q_ref[:, sl].astype(BF16)
            kb = k_ref[:, sl].astype(BF16)
            c_k = ctr_ref[0, L_FF + h:L_FF + h + 1, :] * LOG2E
            s = _dot_nt(qb, kb) * (HEAD_DIM ** -0.5 * LOG2E) - c_k
            if masked:
                s = jnp.where(keep, s, -jnp.inf)
            m_old = m_scr[h]
            m_new = jnp.maximum(m_old, jnp.max(s, axis=-1, keepdims=True))
            alpha = jnp.exp2(m_old - m_new)
            p = jnp.exp2(s - m_new)
            v_aug = jnp.concatenate([v_ref[:, sl].astype(BF16), ones], axis=1)
            acc_scr[h] = alpha * acc_scr[h] + _dot(p.astype(BF16), v_aug)
            m_scr[h] = m_new

    @pl.when(ki < qi)
    def _():
        step(False)

    @pl.when(ki == qi)
    def _():
        step(True)
        for h in range(N_HEADS):
            sl = slice(h * HEAD_DIM, (h + 1) * HEAD_DIM)
            acc = acc_scr[h]
            o_ref[:, sl] = (acc[:, :HEAD_DIM] / acc[:, HEAD_DIM:]).astype(BF16)


def fox_attention(proj, prep_tr, batch, seq, *, t=512):
    n = proj.shape[0]
    nt = seq // t
    pairs = [(qi, ki) for qi in range(nt) for ki in range(qi + 1)]
    qi_arr = jnp.asarray([p[0] for p in pairs], I32)
    ki_arr = jnp.asarray([p[1] for p in pairs], I32)
    qspec = pl.BlockSpec((t, WIDTH), lambda b, s, qi, ki: (b * nt + qi[s], C_FQ // WIDTH))
    kspec = lambda off: pl.BlockSpec(
        (t, WIDTH), lambda b, s, qi, ki: (b * nt + ki[s], off // WIDTH))
    return pl.pallas_call(
        functools.partial(_fox_kernel, t=t),
        grid_spec=pltpu.PrefetchScalarGridSpec(
            num_scalar_prefetch=2,
            grid=(batch, len(pairs)),
            in_specs=[qspec, kspec(C_FK), kspec(C_FV),
                      pl.BlockSpec((1, PREP_ROWS, t), lambda b, s, qi, ki: (b, 0, ki[s]))],
            out_specs=pl.BlockSpec((t, WIDTH), lambda b, s, qi, ki: (b * nt + qi[s], 0)),
            scratch_shapes=[pltpu.VMEM((N_HEADS, t, 1), F32),
                            pltpu.VMEM((N_HEADS, t, 2 * HEAD_DIM), F32)]),
        out_shape=jax.ShapeDtypeStruct((n, WIDTH), BF16),
        compiler_params=_cparams(("arbitrary", "arbitrary")),
        name="fox_attention",
    )(qi_arr, ki_arr, proj, proj, proj, prep_tr)


def _dsa_proj_kernel(cq_ref, g_ref, wq_ref, wi_ref, q_ref, qi_ref):
    x = cq_ref[...]
    ms = jnp.mean(x * x, axis=-1, keepdims=True)
    cb = (x * lax.rsqrt(ms + EPS) * g_ref[...]).astype(BF16)
    q_ref[...] = _dot(cb, wq_ref[...]).astype(BF16)
    qi_ref[...] = _dot(cb, wi_ref[...]).astype(BF16)


def dsa_proj(proj, cq_norm, w_uq, w_qidx, *, tm=512):
    n = proj.shape[0]
    r = DSA_Q_RANK
    wi = IDX_HEADS * IDX_DIM
    return pl.pallas_call(
        _dsa_proj_kernel,
        grid=(n // tm,),
        in_specs=[pl.BlockSpec((tm, r), lambda i: (i, C_DCQ // r)),
                  pl.BlockSpec((1, r), lambda i: (0, 0)),
                  pl.BlockSpec((r, WIDTH), lambda i: (0, 0)),
                  pl.BlockSpec((r, wi), lambda i: (0, 0))],
        out_specs=[pl.BlockSpec((tm, WIDTH), lambda i: (i, 0)),
                   pl.BlockSpec((tm, wi), lambda i: (i, 0))],
        out_shape=[jax.ShapeDtypeStruct((n, WIDTH), BF16),
                   jax.ShapeDtypeStruct((n, wi), BF16)],
        compiler_params=_cparams(("arbitrary",)),
        name="dsa_proj",
    )(proj, cq_norm.reshape(1, r), w_uq, w_qidx)


DSA_QB = 256
DSA_KC = 512
DSA_SCORE_MID_STEPS = 20
DSA_FEW_KEYS = 4
DSA_HALVE_FIXED = 12
DSA_WALK_FIXED = 3


def _score_to_key(s):
    b = pltpu.bitcast(s, I32)
    return b ^ ((b >> 31) & INT_MAX)


def _key_to_score(k):
    return pltpu.bitcast(k ^ ((k >> 31) & INT_MAX), F32)


def _t5_bucket(rel):
    max_exact = REL_BUCKETS // 2
    relf = jnp.maximum(rel, max_exact).astype(F32)
    large = max_exact + (jnp.log(relf / max_exact) / math.log(REL_MAX_DIST / max_exact)
                         * (REL_BUCKETS - max_exact)).astype(I32)
    large = jnp.minimum(large, REL_BUCKETS - 1)
    return jnp.where(rel < max_exact, rel, large)


def _dsa_kernel(rb_ref, q_ref, qi_ref, tok_ref, k_ref, v_ref, ka_ref, kb_ref, o_ref,
                key_scr, lg_scr, band_scr, kb16_scr, vt_scr, ka16_scr, kb16i_scr, *, seq, topk):
    qb_idx = pl.program_id(1)
    t0 = qb_idx * DSA_QB
    n_kc = (t0 + DSA_QB - 1) // DSA_KC + 1
    row_vec = (1, DSA_QB)

    @pl.when(qb_idx == 0)
    def _():
        kb16_scr[...] = k_ref[...].astype(BF16)
        ka16_scr[...] = ka_ref[...].astype(BF16)
        kb16i_scr[...] = kb_ref[...].astype(BF16)
        for c in range(seq // DSA_KC):
            cs = slice(c * DSA_KC, (c + 1) * DSA_KC)
            vt_scr[0:HEAD_DIM, cs] = v_ref[cs, :].T.astype(BF16)
        vt_scr[HEAD_DIM:, :] = jnp.ones((HEAD_DIM, seq), BF16)

    @pl.when((pl.program_id(0) == 0) & (qb_idx == 0))
    def _():
        j_ = lax.broadcasted_iota(I32, (2 * DSA_QB, DSA_QB), 0)
        i_ = lax.broadcasted_iota(I32, (2 * DSA_QB, DSA_QB), 1)
        rel = i_ + DSA_QB - j_
        bucket = _t5_bucket(rel)
        for h in range(N_HEADS):
            far = rb_ref[REL_BUCKETS - 1, h]
            band = jnp.zeros((2 * DSA_QB, DSA_QB), F32)
            for bk in range(REL_BUCKETS - 1):
                band = jnp.where(bucket == bk, (rb_ref[bk, h] - far) * LOG2E, band)
            band_scr[h] = jnp.where(rel >= 0, band, 0.0)

    w_t = tok_ref[...].T
    key_s = lax.broadcasted_iota(I32, (DSA_KC, DSA_QB), 0)
    row_t = t0 + lax.broadcasted_iota(I32, (DSA_KC, DSA_QB), 1)

    def score_chunk(c, carry):
        kmax, kmin = carry
        ks = pl.ds(pl.multiple_of(c * DSA_KC, DSA_KC), DSA_KC)
        ka = ka16_scr[ks, :]
        kb = kb16i_scr[ks, :]
        acc = jnp.zeros((DSA_KC, DSA_QB), F32)
        for p in range(IDX_HEADS // 2):
            qp = qi_ref[:, p * LANE:(p + 1) * LANE]
            acc = acc + jnp.maximum(_dot_nt(ka, qp), 0.0) * w_t[2 * p:2 * p + 1, :]
            acc = acc + jnp.maximum(_dot_nt(kb, qp), 0.0) * w_t[2 * p + 1:2 * p + 2, :]
        key = _score_to_key(acc)
        valid = (c * DSA_KC + key_s) <= row_t
        key_scr[ks, :] = jnp.where(valid, key, INT_MIN)
        kmax = jnp.maximum(kmax, jnp.max(jnp.where(valid, key, INT_MIN), axis=0, keepdims=True))
        kmin = jnp.minimum(kmin, jnp.min(jnp.where(valid, key, INT_MAX), axis=0, keepdims=True))
        return kmax, kmin

    kmax, kmin = lax.fori_loop(0, n_kc, score_chunk, (jnp.full(row_vec, INT_MIN, I32),
                                                     jnp.full(row_vec, INT_MAX, I32)))

    def scan_keys(cand, with_below):
        def body(c, carry):
            cnt, below = carry
            ks = pl.ds(pl.multiple_of(c * DSA_KC, DSA_KC), DSA_KC)
            keys = key_scr[ks, :]
            ge = keys >= cand
            ones = ge.astype(I32)
            low = jnp.where(ge, INT_MIN, keys)
            for u in range(DSA_KC // SUBLANE):
                us = slice(u * SUBLANE, (u + 1) * SUBLANE)
                cnt = cnt + ones[us, :]
                if with_below:
                    below = jnp.maximum(below, low[us, :])
            return cnt, below
        cnt, below = lax.fori_loop(0, n_kc, body, (jnp.zeros((SUBLANE, DSA_QB), I32),
                                                   jnp.full((SUBLANE, DSA_QB), INT_MIN, I32)))
        cnt = jnp.sum(cnt, axis=0, keepdims=True)
        if with_below:
            return cnt, jnp.max(below, axis=0, keepdims=True)
        return cnt

    def open_rows(lo, hi, c_lo):
        return (c_lo > topk) & (hi - 1 > lo)

    def any_row(flag):
        return jnp.max(jnp.where(flag, 1, 0))

    def update(cand, cnt, lo, hi, c_lo, c_hi):
        ge = cnt >= topk
        return (jnp.where(ge, cand, lo), jnp.where(ge, hi, cand),
                jnp.where(ge, cnt, c_lo), jnp.where(ge, c_hi, cnt))

    def crowded(lo, hi, c_lo, c_hi):
        return any_row(open_rows(lo, hi, c_lo) & (c_lo - c_hi > DSA_FEW_KEYS))

    def halve_step(it, lo, hi, c_lo, c_hi):
        key_mid = (lo >> 1) + (hi >> 1) + (lo & hi & 1)
        score_mid = _score_to_key(0.5 * _key_to_score(lo) + 0.5 * _key_to_score(hi - 1))
        cand = jnp.where(it < DSA_SCORE_MID_STEPS, score_mid, key_mid)
        cand = jnp.minimum(jnp.maximum(cand, lo + 1), hi - 1)
        cand = jnp.where(hi - 1 > lo, cand, lo)
        return update(cand, scan_keys(cand, False), lo, hi, c_lo, c_hi)

    def halve_body(st):
        it, _, lo, hi, c_lo, c_hi = st
        go = crowded(lo, hi, c_lo, c_hi)
        return (it + 1, go) + halve_step(it, lo, hi, c_lo, c_hi)

    def walk_step(lo, hi, c_lo, c_hi, nxt):
        is_open = open_rows(lo, hi, c_lo)
        cand = jnp.where(is_open, nxt, lo)
        cnt, below = scan_keys(cand, True)
        ge = cnt >= topk
        hi = jnp.where(is_open, jnp.where(ge, cand + 1, cand), hi)
        c_hi = jnp.where(is_open & jnp.logical_not(ge), cnt, c_hi)
        lo = jnp.where(is_open & ge, cand, lo)
        c_lo = jnp.where(is_open & ge, cnt, c_lo)
        nxt = jnp.where(ge, nxt, below)
        return lo, hi, c_lo, c_hi, nxt

    def walk_body(st):
        go = any_row(open_rows(st[1], st[2], st[3]))
        return (go,) + walk_step(*st[1:])

    n_valid = jnp.minimum(t0 + lax.broadcasted_iota(I32, row_vec, 1) + 1, seq)
    st = (kmin, kmax + 1, n_valid, jnp.zeros(row_vec, I32))
    st = lax.fori_loop(0, DSA_HALVE_FIXED, lambda it, s: halve_step(it, *s), st)
    st = lax.while_loop(lambda s: s[1] > 0, halve_body,
                        (jnp.int32(DSA_HALVE_FIXED), crowded(*st)) + st)[2:]
    _, nxt0 = scan_keys(st[1], True)
    st = lax.fori_loop(0, DSA_WALK_FIXED, lambda it, s: walk_step(*s), st + (nxt0,))
    _, thr, hi, n_ge, n_gt, _ = lax.while_loop(
        lambda s: s[0] > 0, walk_body, (any_row(open_rows(st[0], st[1], st[2])),) + st)

    tied = n_ge > topk
    has_tie = jnp.max(jnp.where(tied, 1, 0)) > 0

    @pl.when(has_tie)
    def _():
        room = (topk - n_gt).astype(F32)
        ii = lax.broadcasted_iota(I32, (LANE, LANE), 0)
        jj = lax.broadcasted_iota(I32, (LANE, LANE), 1)
        lower = (ii >= jj).astype(BF16)

        def body(c, seen):
            ks = pl.ds(pl.multiple_of(c * LANE, LANE), LANE)
            kk = key_scr[ks, :]
            eq = kk == thr
            rank = seen + _dot(lower, eq.astype(BF16))
            drop = eq & (rank > room) & tied
            key_scr[ks, :] = jnp.where(drop, INT_MIN, kk)
            return seen + jnp.sum(eq.astype(F32), axis=0, keepdims=True)
        lax.fori_loop(0, n_kc * (DSA_KC // LANE), body, jnp.zeros(row_vec, F32))

    def mask_chunk(c, carry):
        ks = pl.ds(pl.multiple_of(c * DSA_KC, DSA_KC), DSA_KC)
        sel = jnp.where(key_scr[ks, :] >= thr, 0.0, -jnp.inf).astype(F32)
        key_scr[ks, :] = pltpu.bitcast(sel, I32)
        return carry
    lax.fori_loop(0, n_kc, mask_chunk, 0)

    heads = range(N_HEADS)
    hsl = [slice(h * HEAD_DIM, (h + 1) * HEAD_DIM) for h in heads]

    def logit_chunk(c, ms):
        ks = pl.ds(pl.multiple_of(c * DSA_KC, DSA_KC), DSA_KC)
        k_c = kb16_scr[ks, :]
        sel = pltpu.bitcast(key_scr[ks, :], F32)
        out = []
        for h in heads:
            s = _dot_nt(k_c, q_ref[:, hsl[h]]) * (HEAD_DIM ** -0.5 * LOG2E) + sel
            lg_scr[h, ks, :] = s
            out.append(jnp.maximum(ms[h], jnp.max(s, axis=0, keepdims=True)))
        return tuple(out)
    ms = lax.fori_loop(0, n_kc, logit_chunk,
                       tuple(jnp.full(row_vec, -jnp.inf, F32) for _ in heads))

    band_off = pl.multiple_of(jnp.maximum(qb_idx - 1, 0) * DSA_QB, DSA_QB)
    ws = pl.ds(band_off, 2 * DSA_QB)
    ms = list(ms)
    for h in heads:
        band_h = band_scr[h]
        band_first = jnp.concatenate([band_h[DSA_QB:, :], jnp.zeros((DSA_QB, DSA_QB), F32)], axis=0)
        win = lg_scr[h, ws, :] + jnp.where(qb_idx == 0, band_first, band_h)
        lg_scr[h, ws, :] = win
        ms[h] = jnp.maximum(ms[h], jnp.max(win, axis=0, keepdims=True))

    def pv_chunk(c, accs):
        ks = pl.ds(pl.multiple_of(c * DSA_KC, DSA_KC), DSA_KC)
        vt_c = vt_scr[:, ks]
        return tuple(accs[h] + _dot(vt_c, jnp.exp2(lg_scr[h, ks, :] - ms[h]).astype(BF16))
                     for h in heads)
    accs = lax.fori_loop(0, n_kc, pv_chunk,
                         tuple(jnp.zeros((2 * HEAD_DIM, DSA_QB), F32) for _ in heads))
    for h in heads:
        o_ref[:, hsl[h]] = (accs[h][:HEAD_DIM, :] / accs[h][HEAD_DIM:, :]).T.astype(BF16)


def dsa_attention(proj, q, q_idx, prep_tok, rel_bias, batch, seq):
    n = proj.shape[0]
    nq = seq // DSA_QB
    topk = min(TOPK_MAX, seq // 4)
    wi = IDX_HEADS * IDX_DIM
    rowblk = lambda w, cb: pl.BlockSpec((DSA_QB, w), lambda b, i: (b * nq + i, cb))
    seqblk = lambda off: pl.BlockSpec((seq, LANE), lambda b, i: (b, off // LANE))
    return pl.pallas_call(
        functools.partial(_dsa_kernel, seq=seq, topk=topk),
        grid=(batch, nq),
        in_specs=[pl.BlockSpec(memory_space=pltpu.SMEM),
                  rowblk(WIDTH, 0), rowblk(wi, 0), rowblk(LANE, 0),
                  seqblk(C_DK), seqblk(C_DV), seqblk(C_KA), seqblk(C_KB)],
        out_specs=pl.BlockSpec((DSA_QB, WIDTH), lambda b, i: (b * nq + i, 0)),
        out_shape=jax.ShapeDtypeStruct((n, WIDTH), BF16),
        scratch_shapes=[pltpu.VMEM((seq, DSA_QB), I32),
                        pltpu.VMEM((N_HEADS, seq, DSA_QB), F32),
                        pltpu.VMEM((N_HEADS, 2 * DSA_QB, DSA_QB), F32),
                        pltpu.VMEM((seq, LANE), BF16),
                        pltpu.VMEM((2 * HEAD_DIM, seq), BF16),
                        pltpu.VMEM((seq, LANE), BF16),
                        pltpu.VMEM((seq, LANE), BF16)],
        compiler_params=_cparams(("arbitrary", "arbitrary")),
        name="dsa_attention",
    )(rel_bias, q, q_idx, prep_tok, proj, proj, proj, proj)


GDN_T = 256
GDN_GROUP = 2
GDN_HALO = 8


def _gdn_kernel(q_ref, k_ref, v_ref, z_ref, cw_ref, ng_ref, tok_ref, tr_ref, o_ref,
                xq_scr, xk_scr, xv_scr, state_scr):
    first = pl.program_id(1) == 0

    @pl.when(first)
    def _():
        state_scr[...] = jnp.zeros_like(state_scr)
        for scr in (xq_scr, xk_scr, xv_scr):
            scr[0:GDN_HALO, :] = jnp.zeros((GDN_HALO, WIDTH), F32)

    def conv(x_ref, scr, w_off):
        scr[GDN_HALO:, :] = x_ref[...]
        y = jnp.zeros((GDN_T, WIDTH), F32)
        for i in range(GDN_CONV):
            st = GDN_HALO - (GDN_CONV - 1) + i
            y = y + scr[st:st + GDN_T, :] * cw_ref[i:i + 1, w_off:w_off + WIDTH]
        scr[0:GDN_HALO, :] = scr[GDN_T:GDN_T + GDN_HALO, :]
        return _silu(y)

    qc = conv(q_ref, xq_scr, 0)
    kc = conv(k_ref, xk_scr, WIDTH)
    vc = conv(v_ref, xv_scr, 2 * WIDTH)
    tok = tok_ref[...]
    c = GDN_CHUNK
    heads = range(N_HEADS)
    hsl = [slice(h * HEAD_DIM, (h + 1) * HEAD_DIM) for h in heads]

    def l2norm_heads(x, scale):
        return jnp.concatenate(
            [x[:, s] * (lax.rsqrt(jnp.sum(x[:, s] * x[:, s], axis=-1, keepdims=True) + EPS) * scale)
             for s in hsl], axis=1)

    qf = l2norm_heads(qc, HEAD_DIM ** -0.5)
    kf = l2norm_heads(kc, 1.0)

    grp = GDN_GROUP
    gw = grp * HEAD_DIM
    nb = grp * c
    ri = lax.broadcasted_iota(I32, (nb, nb), 0)
    ci = lax.broadcasted_iota(I32, (nb, nb), 1)
    tril = ((ri // c) == (ci // c)) & (ri >= ci)
    eye = (ri == ci).astype(F32)
    pair_masks = []
    for lg in range(c.bit_length() - 1):
        pair_masks.append(((ri >> (lg + 1)) == (ci >> (lg + 1)))
                          & (((ri >> lg) & 1) == 1) & (((ci >> lg) & 1) == 0))
    lane_head = lax.broadcasted_iota(I32, (c, gw), 1) // HEAD_DIM
    row_head = lax.broadcasted_iota(I32, (nb, HEAD_DIM), 0) // c

    def spread(x):
        return jnp.concatenate([jnp.where(lane_head == u, x, 0.0) for u in range(grp)], axis=0)

    def stack(x):
        return jnp.concatenate([x[:, hsl[u]] for u in range(grp)], axis=0)

    def spread_lanes(x):
        return jnp.concatenate([jnp.where(row_head == u, x, 0.0) for u in range(grp)], axis=1)

    def split(x):
        hi = x.astype(BF16)
        return hi, (x - hi.astype(F32)).astype(BF16)

    def dot_split(a, b):
        return _dot(a[0], b[0]) + (_dot(a[0], b[1]) + _dot(a[1], b[0]))

    items = [(j, gi) for j in range(GDN_T // c) for gi in range(N_HEADS // grp)]
    pre = []
    for j, gi in items:
        rs = slice(j * c, (j + 1) * c)
        last = slice((j + 1) * c - 1, (j + 1) * c)
        gh = [gi * grp + u for u in range(grp)]
        gsl = slice(gi * gw, (gi + 1) * gw)
        qj, kj, vj = qf[rs, gsl], kf[rs, gsl], vc[rs, gsl]
        b_col = jnp.concatenate([tok[rs, L_GB + h:L_GB + h + 1] for h in gh], axis=0)
        g_col = jnp.concatenate([tok[rs, L_GA + h:L_GA + h + 1] for h in gh], axis=0)
        g_row = jnp.concatenate([tr_ref[0, L_GA + h:L_GA + h + 1, rs] for h in gh], axis=1)
        g_last = [tok[last, L_GA + h:L_GA + h + 1] for h in gh]
        g_last_col = jnp.concatenate([jnp.broadcast_to(g, (c, 1)) for g in g_last], axis=0)
        k_sp = spread(kj)
        q_sp = spread(qj)
        k_sp16 = k_sp.astype(BF16)
        decay = jnp.exp(jnp.where(tril, g_col - g_row, -jnp.inf))
        eg = jnp.exp(g_col)
        pre.append(dict(
            gh=gh, gsl=gsl,
            l_mat=b_col * _dot_nt(k_sp16, k_sp16) * decay,
            rhs=jnp.concatenate([stack(vj) * b_col, stack(kj) * (b_col * eg)], axis=1),
            qk=_dot_nt(q_sp.astype(BF16), k_sp16) * decay,
            q_dec=q_sp * eg,
            k_dec=k_sp * jnp.exp(g_last_col - g_col),
            e_last=jnp.concatenate([jnp.broadcast_to(jnp.exp(g), (HEAD_DIM, 1)) for g in g_last], axis=0)))

    t_inv = [eye - jnp.where(pair_masks[0], p["l_mat"], 0.0) for p in pre]
    for pm in pair_masks[1:]:
        t_s = [split(t) for t in t_inv]
        m_t = [dot_split(split(jnp.where(pm, p["l_mat"], 0.0)), ts) for p, ts in zip(pre, t_s)]
        t_inv = [t - dot_split(ts, split(m)) for t, ts, m in zip(t_inv, t_s, m_t)]
    sols = [dot_split(split(t), split(p["rhs"])) for t, p in zip(t_inv, pre)]

    outs = [[] for _ in heads]
    for p, sol in zip(pre, sols):
        u0 = sol[:, :HEAD_DIM]
        kcum = sol[:, HEAD_DIM:]
        st = state_scr[p["gsl"], :]
        stb = st.astype(BF16)
        v_new = u0 - _dot(spread_lanes(kcum).astype(BF16), stb)
        v_new_b = v_new.astype(BF16)
        o_st = _dot(p["q_dec"].astype(BF16), stb) + _dot(p["qk"].astype(BF16), v_new_b)
        state_scr[p["gsl"], :] = st * p["e_last"] + _dot(p["k_dec"].T.astype(BF16), v_new_b)
        for u, h in enumerate(p["gh"]):
            outs[h].append(o_st[u * c:(u + 1) * c, :])

    for h in heads:
        o = jnp.concatenate(outs[h], axis=0)
        ms = jnp.mean(o * o, axis=-1, keepdims=True)
        on = o * lax.rsqrt(ms + EPS) * ng_ref[...]
        o_ref[:, hsl[h]] = (on * _silu(z_ref[:, hsl[h]])).astype(BF16)


def gated_deltanet(proj, prep_tok, prep_tr, conv_w, norm_g, batch, seq):
    n = proj.shape[0]
    t = GDN_T
    nt = seq // t
    col = lambda off: pl.BlockSpec((t, WIDTH), lambda b, i: (b * nt + i, off // WIDTH))
    return pl.pallas_call(
        _gdn_kernel,
        grid=(batch, nt),
        in_specs=[col(C_GQ), col(C_GK), col(C_GV), col(C_GZ),
                  pl.BlockSpec((GDN_CONV, 3 * WIDTH), lambda b, i: (0, 0)),
                  pl.BlockSpec((1, HEAD_DIM), lambda b, i: (0, 0)),
                  pl.BlockSpec((t, LANE), lambda b, i: (b * nt + i, 0)),
                  pl.BlockSpec((1, PREP_ROWS, t), lambda b, i: (b, 0, i))],
        out_specs=pl.BlockSpec((t, WIDTH), lambda b, i: (b * nt + i, 0)),
        out_shape=jax.ShapeDtypeStruct((n, WIDTH), BF16),
        scratch_shapes=[pltpu.VMEM((t + GDN_HALO, WIDTH), F32),
                        pltpu.VMEM((t + GDN_HALO, WIDTH), F32),
                        pltpu.VMEM((t + GDN_HALO, WIDTH), F32),
                        pltpu.VMEM((N_HEADS * HEAD_DIM, HEAD_DIM), F32)],
        compiler_params=_cparams(("arbitrary", "arbitrary")),
        name="gated_deltanet",
    )(proj, proj, proj, proj, conv_w, norm_g.reshape(1, HEAD_DIM), prep_tok, prep_tr)


def _merge_kernel(h_ref, b0_ref, b1_ref, b2_ref, b3_ref, g0_ref, g1_ref, g2_ref, g3_ref,
                  wb_ref, o_ref):
    h = h_ref[...]
    acc = None
    for n, (b_ref, g_ref) in enumerate(zip((b0_ref, b1_ref, b2_ref, b3_ref),
                                           (g0_ref, g1_ref, g2_ref, g3_ref))):
        gate = jax.nn.sigmoid(_dot(h, g_ref[...]))
        term = gate * _dot(b_ref[...], wb_ref[n])
        acc = term if acc is None else acc + term
    o_ref[...] = acc.astype(BF16)


def merge_branches(h, branches, w_gate, w_branch, *, tm=512, tn=512):
    _, n, d = h.shape
    nj = d // tn
    bspec = pl.BlockSpec((tm, WIDTH), lambda j, i: (i, 0))
    gspec = lambda k: pl.BlockSpec((d, tn), lambda j, i: (0, k * nj + j))
    return pl.pallas_call(
        _merge_kernel,
        grid=(nj, n // tm),
        in_specs=[pl.BlockSpec((None, tm, d), lambda j, i: (0, i, 0)),
                  bspec, bspec, bspec, bspec,
                  gspec(0), gspec(1), gspec(2), gspec(3),
                  pl.BlockSpec((N_BRANCH, WIDTH, tn), lambda j, i: (0, 0, j))],
        out_specs=pl.BlockSpec((tm, tn), lambda j, i: (i, j)),
        out_shape=jax.ShapeDtypeStruct((n, d), BF16),
        compiler_params=_cparams(("arbitrary", "arbitrary")),
        name="merge_branches",
    )(h, *branches, w_gate, w_gate, w_gate, w_gate, w_branch)


def _resid_mm_kernel(a_ref, w_ref, x_ref, o_ref):
    o_ref[...] = x_ref[...] + _dot(a_ref[...], w_ref[...])


def resid_matmul(a, w, x, *, tm=512, tn=1024, name="resid_matmul"):
    n, k = a.shape
    d = w.shape[1]
    return pl.pallas_call(
        _resid_mm_kernel,
        grid=(d // tn, n // tm),
        in_specs=[pl.BlockSpec((tm, k), lambda j, i: (i, 0)),
                  pl.BlockSpec((k, tn), lambda j, i: (0, j)),
                  pl.BlockSpec((tm, tn), lambda j, i: (i, j))],
        out_specs=pl.BlockSpec((tm, tn), lambda j, i: (i, j)),
        out_shape=jax.ShapeDtypeStruct((n, d), F32),
        compiler_params=_cparams(("arbitrary", "arbitrary")),
        name=name,
    )(a, w, x)


FFN_HALO = 8
FFN_SUB = 512


def _ffn1_kernel(x_ref, g_ref, wg_ref, wu_ref, cw_ref, cb_ref, o_ref, h_scr, gt_scr, halo_scr,
                 *, tm, tiles_per_seq):
    i = pl.program_id(0)
    j = pl.program_id(1)

    @pl.when((i == 0) & (j == 0))
    def _():
        halo_scr[...] = jnp.zeros_like(halo_scr)

    @pl.when(j == 0)
    def _():
        def body(r, carry):
            rows = pl.ds(pl.multiple_of(r * NORM_ROWS, NORM_ROWS), NORM_ROWS)
            h_scr[rows, :] = _rmsnorm_rows(x_ref, g_ref, rows).astype(BF16)
            return carry
        lax.fori_loop(0, tm // NORM_ROWS, body, 0)

    h = h_scr[...]
    seq_start = (i % tiles_per_seq) == 0
    tn = o_ref.shape[1]
    for off in range(0, tn, FFN_SUB):
        cs = slice(off, min(off + FFN_SUB, tn))
        g = _dot(h, wg_ref[:, cs])
        gt_scr[FFN_HALO:, cs] = g
        gt_scr[0:FFN_HALO, cs] = jnp.where(seq_start, 0.0, halo_scr[j, :, cs])
        halo_scr[j, :, cs] = g[tm - FFN_HALO:, :]
        y = cb_ref[:, cs] + g * cw_ref[FFN_CONV - 1:FFN_CONV, cs]
        for t in range(FFN_CONV - 1):
            st = FFN_HALO - (FFN_CONV - 1) + t
            y = y + gt_scr[st:st + tm, cs] * cw_ref[t:t + 1, cs]
        o_ref[:, cs] = (_silu(y) * _dot(h, wu_ref[:, cs])).astype(BF16)


def conv_ffn_up(x, gain, w_gate, w_up, conv_w, conv_b, seq, *, tm=512, tn=1408):
    n, d = x.shape
    f = w_gate.shape[1]
    nj = f // tn
    return pl.pallas_call(
        functools.partial(_ffn1_kernel, tm=tm, tiles_per_seq=seq // tm),
        grid=(n // tm, nj),
        in_specs=[pl.BlockSpec((tm, d), lambda i, j: (i, 0)),
                  pl.BlockSpec((1, d), lambda i, j: (0, 0)),
                  pl.BlockSpec((d, tn), lambda i, j: (0, j)),
                  pl.BlockSpec((d, tn), lambda i, j: (0, j)),
                  pl.BlockSpec((FFN_CONV, tn), lambda i, j: (0, j)),
                  pl.BlockSpec((1, tn), lambda i, j: (0, j))],
        out_specs=pl.BlockSpec((tm, tn), lambda i, j: (i, j)),
        out_shape=jax.ShapeDtypeStruct((n, f), BF16),
        scratch_shapes=[pltpu.VMEM((tm, d), BF16),
                        pltpu.VMEM((tm + FFN_HALO, tn), F32),
                        pltpu.VMEM((nj, FFN_HALO, tn), F32)],
        compiler_params=_cparams(("arbitrary", "arbitrary")),
        name="conv_ffn_up",
    )(x, gain.reshape(1, d), w_gate, w_up, conv_w, conv_b.reshape(1, f))


IN_SIZES = (WIDTH, WIDTH, WIDTH, WIDTH,
            DSA_Q_RANK, HEAD_DIM, HEAD_DIM, IDX_DIM, IDX_HEADS,
            WIDTH, WIDTH, WIDTH, N_HEADS,
            WIDTH, WIDTH, WIDTH, WIDTH, N_HEADS, N_HEADS)
IN_NAMES = ("r_q", "r_k", "r_v", "r_g", "d_cq", "d_k", "d_v", "i_k", "i_w",
            "f_q", "f_k", "f_v", "f_f", "g_q", "g_k", "g_v", "g_z", "g_b", "g_a")
IN_PLAN = (("r_q", C_RQ), ("r_k", C_RK), ("r_v", C_RV), ("r_g", C_RG),
           ("f_q", C_FQ), ("f_k", C_FK), ("f_v", C_FV),
           ("g_q", C_GQ), ("g_k", C_GK), ("g_v", C_GV), ("g_z", C_GZ),
           ("d_k", C_DK), ("d_cq", C_DCQ), ("d_v", C_DV),
           ("i_k", C_KA), ("i_k", C_KB + IDX_DIM),
           ("i_w", C_SM + L_IW), ("f_f", C_SM + L_FF), ("g_b", C_SM + L_GB), ("g_a", C_SM + L_GA))


def _prep_w_in_kernel(w_ref, m_ref, g_ref):
    src = {}
    off = 0
    for name, size in zip(IN_NAMES, IN_SIZES):
        src[name] = (off, size)
        off += size
    m_ref[...] = jnp.zeros_like(m_ref)
    for name, dst in IN_PLAN:
        so, w = src[name]
        m_ref[:, dst:dst + w] = w_ref[:, so:so + w].astype(BF16)
    g_ref[...] = w_ref[:, off:off + g_ref.shape[1]].astype(BF16)


def prep_w_in(w_in, layer, *, tr=128):
    _, d, c = w_in.shape
    return pl.pallas_call(
        _prep_w_in_kernel,
        grid=(d // tr,),
        in_specs=[pl.BlockSpec((None, tr, c), lambda i: (layer, i, 0))],
        out_specs=[pl.BlockSpec((tr, C_TOT), lambda i: (i, 0)),
                   pl.BlockSpec((tr, N_BRANCH * d), lambda i: (i, 0))],
        out_shape=[jax.ShapeDtypeStruct((d, C_TOT), BF16),
                   jax.ShapeDtypeStruct((d, N_BRANCH * d), BF16)],
        compiler_params=_cparams(("arbitrary",)),
        name="prep_w_in",
    )(w_in)


def _cast_kernel(w_ref, o_ref):
    o_ref[...] = w_ref[...].astype(BF16)


def cast_layer(w, layer, *, tr=256):
    _, r, c = w.shape
    if r % tr:
        tr = r
    return pl.pallas_call(
        _cast_kernel,
        grid=(r // tr,),
        in_specs=[pl.BlockSpec((None, tr, c), lambda i: (layer, i, 0))],
        out_specs=pl.BlockSpec((tr, c), lambda i: (i, 0)),
        out_shape=jax.ShapeDtypeStruct((r, c), BF16),
        compiler_params=_cparams(("arbitrary",)),
        name="cast_bf16",
    )(w)


def cast_branch(w_branch, layer):
    _, nbr, r, c = w_branch.shape
    return pl.pallas_call(
        _cast_kernel,
        grid=(nbr,),
        in_specs=[pl.BlockSpec((None, None, r, c), lambda i: (layer, i, 0, 0))],
        out_specs=pl.BlockSpec((None, r, c), lambda i: (i, 0, 0)),
        out_shape=jax.ShapeDtypeStruct((nbr, r, c), BF16),
        compiler_params=_cparams(("arbitrary",)),
        name="cast_branch",
    )(w_branch)


def kernel(x, norm_mix, w_in, dsa_cq_norm, dsa_w_uq, dsa_w_qidx, fox_f_bias, gdn_conv, gdn_a_log,
           gdn_dt_bias, gdn_norm, w_branch, w_out, rel_bias, norm_ffn, ffn_w_gate, ffn_w_up,
           ffn_conv, ffn_conv_b, ffn_w_down, final_norm):
    batch, seq, d = x.shape
    depth = w_in.shape[0]
    xf = x.reshape(batch * seq, d)
    ret_tables = _retention_tables(seq)
    for l in range(depth):
        w_main, w_gate = prep_w_in(w_in, l)
        proj, h = norm_proj(xf, norm_mix[l], w_main)
        par = jnp.zeros((SUBLANE, LANE), F32)
        par = par.at[0, L_FF:L_FF + N_HEADS].set(fox_f_bias[l])
        par = par.at[0, L_GA:L_GA + N_HEADS].set(gdn_dt_bias[l])
        par = par.at[1, L_GA:L_GA + N_HEADS].set(gdn_a_log[l])
        prep_tok, prep_tr = prep_small(proj, par, batch, seq)
        o_ret = retention(proj, ret_tables, batch, seq)
        q_dsa, q_idx = dsa_proj(proj, dsa_cq_norm[l], cast_layer(dsa_w_uq, l), cast_layer(dsa_w_qidx, l))
        o_dsa = dsa_attention(proj, q_dsa, q_idx, prep_tok, rel_bias, batch, seq)
        o_fox = fox_attention(proj, prep_tr, batch, seq)
        o_gdn = gated_deltanet(proj, prep_tok, prep_tr, gdn_conv[l], gdn_norm[l], batch, seq)
        merged = merge_branches(h, (o_ret, o_dsa, o_fox, o_gdn), w_gate, cast_branch(w_branch, l))
        xf = resid_matmul(merged, cast_layer(w_out, l), xf, tn=d, name="out_proj")
        act = conv_ffn_up(xf, norm_ffn[l], cast_layer(ffn_w_gate, l), cast_layer(ffn_w_up, l),
                          ffn_conv[l], ffn_conv_b[l], seq)
        xf = resid_matmul(act, cast_layer(ffn_w_down, l), xf, name="ffn_down")
    return rmsnorm(xf, final_norm).reshape(batch, seq, d)
```

```python
import functools
import math

import jax
import jax.numpy as jnp
from jax import lax
from jax.experimental import pallas as pl
from jax.experimental.pallas import tpu as pltpu

F32 = jnp.float32
BF16 = jnp.bfloat16
I32 = jnp.int32

HEAD_DIM = 128
N_HEADS = 4
WIDTH = N_HEADS * HEAD_DIM
N_BRANCH = 4
RET_CHUNK = 128
ROPE_BASE = 10000.0
DSA_Q_RANK = 384
IDX_HEADS = 16
IDX_DIM = 64
TOPK_MAX = 256
GDN_CONV = 4
GDN_CHUNK = 64
REL_BUCKETS = 32
REL_MAX_DIST = 128
FFN_CONV = 3
EPS = 1e-6

LANE = 128
SUBLANE = 8
VMEM_LIMIT = 56 * 1024 * 1024

C_RQ, C_RK, C_RV, C_RG = 0, 512, 1024, 1536
C_FQ, C_FK, C_FV = 2048, 2560, 3072
C_GQ, C_GK, C_GV, C_GZ = 3584, 4096, 4608, 5120
C_DK, C_DCQ, C_DV, C_KA, C_KB, C_SM = 5632, 5760, 6144, 6272, 6400, 6528
C_TOT = 6656
L_IW, L_FF, L_GB, L_GA = 0, 16, 20, 24

LOG2E = 1.4426950408889634
INT_MIN = -(2 ** 31)
INT_MAX = 2 ** 31 - 1
HIGHEST = lax.Precision.HIGHEST


def _cparams(sem, vmem=VMEM_LIMIT):
    return pltpu.CompilerParams(dimension_semantics=sem, vmem_limit_bytes=vmem)


def _dot(a, b):
    return jnp.dot(a, b, preferred_element_type=F32)


def _dot_nt(a, b):
    return lax.dot_general(a, b, (((1,), (1,)), ((), ())), preferred_element_type=F32)


def _silu(x):
    return x * jax.nn.sigmoid(x)


NORM_ROWS = 128


def _rmsnorm_rows(x_ref, g_ref, rows):
    x = x_ref[rows, :]
    ms = jnp.mean(x * x, axis=-1, keepdims=True)
    return x * lax.rsqrt(ms + EPS) * g_ref[...]


PROJ_SUB = 512


def _norm_proj_kernel(x_ref, g_ref, w_ref, o_ref, h_ref, h_scr, *, tm):
    def body(r, carry):
        rows = pl.ds(pl.multiple_of(r * NORM_ROWS, NORM_ROWS), NORM_ROWS)
        hb = _rmsnorm_rows(x_ref, g_ref, rows).astype(BF16)
        h_scr[rows, :] = hb
        h_ref[rows, :] = hb
        return carry
    lax.fori_loop(0, tm // NORM_ROWS, body, 0)

    h = h_scr[...]
    tn = o_ref.shape[1]
    for off in range(0, tn, PROJ_SUB):
        cs = slice(off, min(off + PROJ_SUB, tn))
        o_ref[:, cs] = _dot(h, w_ref[:, cs])


def norm_proj(x, gain, w, *, tm=512, col_parts=2):
    n, d = x.shape
    c = w.shape[1]
    tn = c // col_parts
    return pl.pallas_call(
        functools.partial(_norm_proj_kernel, tm=tm),
        grid=(col_parts, n // tm),
        in_specs=[pl.BlockSpec((tm, d), lambda j, i: (i, 0)),
                  pl.BlockSpec((1, d), lambda j, i: (0, 0)),
                  pl.BlockSpec((d, tn), lambda j, i: (0, j), pipeline_mode=pl.Buffered(1))],
        out_specs=[pl.BlockSpec((tm, tn), lambda j, i: (i, j)),
                   pl.BlockSpec((None, tm, d), lambda j, i: (j, i, 0))],
        out_shape=[jax.ShapeDtypeStruct((n, c), F32),
                   jax.ShapeDtypeStruct((col_parts, n, d), BF16)],
        scratch_shapes=[pltpu.VMEM((tm, d), BF16)],
        compiler_params=_cparams(("arbitrary", "arbitrary")),
        name="norm_proj",
    )(x, gain.reshape(1, d), w)


def _rmsnorm_kernel(x_ref, g_ref, o_ref, *, tm):
    def body(r, carry):
        rows = pl.ds(pl.multiple_of(r * NORM_ROWS, NORM_ROWS), NORM_ROWS)
        o_ref[rows, :] = _rmsnorm_rows(x_ref, g_ref, rows)
        return carry
    lax.fori_loop(0, tm // NORM_ROWS, body, 0)


def rmsnorm(x, gain, *, tm=512):
    n, d = x.shape
    return pl.pallas_call(
        functools.partial(_rmsnorm_kernel, tm=tm),
        grid=(n // tm,),
        in_specs=[pl.BlockSpec((tm, d), lambda i: (i, 0)),
                  pl.BlockSpec((1, d), lambda i: (0, 0))],
        out_specs=pl.BlockSpec((tm, d), lambda i: (i, 0)),
        out_shape=jax.ShapeDtypeStruct((n, d), F32),
        compiler_params=_cparams(("arbitrary",)),
        name="final_rmsnorm",
    )(x, gain.reshape(1, d))


def _prep_kernel(s_ref, par_ref, tok_ref, tr_ref, carry_scr):
    @pl.when(pl.program_id(1) == 0)
    def _():
        carry_scr[...] = jnp.zeros_like(carry_scr)

    s = s_ref[...]
    lane = lax.broadcasted_iota(I32, (LANE, LANE), 1)
    row = lax.broadcasted_iota(I32, (LANE, LANE), 0)
    z = s + par_ref[0:1, :]
    soft = jnp.maximum(z, 0.0) + jnp.log1p(jnp.exp(-jnp.abs(z)))
    log_sig = z - soft
    sig = jax.nn.sigmoid(z)
    g_val = -jnp.exp(par_ref[1:2, :]) * soft
    is_f = (lane[0:1] >= L_FF) & (lane[0:1] < L_FF + N_HEADS)
    is_b = (lane[0:1] >= L_GB) & (lane[0:1] < L_GB + N_HEADS)
    is_a = (lane[0:1] >= L_GA) & (lane[0:1] < L_GA + N_HEADS)
    pre = jnp.where(is_f, log_sig, jnp.where(is_a, g_val, 0.0))
    tri = (row >= lane).astype(F32)
    tri_blk = ((row >= lane) & ((row // GDN_CHUNK) == (lane // GDN_CHUNK))).astype(F32)
    subs = [slice(u * LANE, (u + 1) * LANE) for u in range(PREP_T // LANE)]
    cum_full = [jnp.dot(tri, pre[u], precision=HIGHEST, preferred_element_type=F32) for u in subs]
    cum_blk = [jnp.dot(tri_blk, pre[u], precision=HIGHEST, preferred_element_type=F32) for u in subs]
    scale_iw = IDX_HEADS ** -0.5 * IDX_DIM ** -0.5
    carry = carry_scr[0:1, :]
    for u, cf, cb in zip(subs, cum_full, cum_blk):
        c_fox = cf + carry
        carry = c_fox[LANE - 1:LANE, :]
        out = jnp.where(is_f, c_fox,
                        jnp.where(is_a, cb,
                                  jnp.where(is_b, sig[u],
                                            jnp.where(lane[0:1] < IDX_HEADS, s[u] * scale_iw, 0.0))))
        tok_ref[u, :] = out
        tr_ref[0, :, u] = out.T[0:PREP_ROWS, :]
    carry_scr[0:1, :] = carry


PREP_T = 512
PREP_ROWS = 32


def prep_small(proj, par, batch, seq):
    n = proj.shape[0]
    nc = seq // PREP_T
    return pl.pallas_call(
        _prep_kernel,
        grid=(batch, nc),
        in_specs=[pl.BlockSpec((PREP_T, LANE), lambda b, c: (b * nc + c, C_SM // LANE)),
                  pl.BlockSpec((SUBLANE, LANE), lambda b, c: (0, 0))],
        out_specs=[pl.BlockSpec((PREP_T, LANE), lambda b, c: (b * nc + c, 0)),
                   pl.BlockSpec((1, PREP_ROWS, PREP_T), lambda b, c: (b, 0, c))],
        out_shape=[jax.ShapeDtypeStruct((n, LANE), F32),
                   jax.ShapeDtypeStruct((batch, PREP_ROWS, seq), F32)],
        scratch_shapes=[pltpu.VMEM((SUBLANE, LANE), F32)],
        compiler_params=_cparams(("arbitrary", "arbitrary")),
        name="prep_small",
    )(proj, par)


def _ret_gamma():
    return [math.log1p(-(2.0 ** (-5.0 - h))) for h in range(N_HEADS)]


def _retention_kernel(q_ref, k_ref, v_ref, g_ref, cos_ref, sin_ref, dec_ref, zeta_ref, xi_ref,
                      o_ref, state_scr):
    @pl.when(pl.program_id(1) == 0)
    def _():
        state_scr[...] = jnp.zeros_like(state_scr)

    cos_t = cos_ref[...]
    sin_t = sin_ref[...]
    log_gamma = _ret_gamma()
    heads = range(N_HEADS)
    hsl = [slice(h * HEAD_DIM, (h + 1) * HEAD_DIM) for h in heads]

    def rope(x):
        return x * cos_t + pltpu.roll(x, HEAD_DIM // 2, 1) * sin_t

    qb = [rope(q_ref[:, s]).astype(BF16) for s in hsl]
    kr = [rope(k_ref[:, s]) * (HEAD_DIM ** -0.5) for s in hsl]
    kb = [x.astype(BF16) for x in kr]
    vb = [v_ref[:, s].astype(BF16) for s in hsl]
    st = [state_scr[h] for h in heads]
    inner = [(_dot_nt(qb[h], kb[h]) * dec_ref[h]).astype(BF16) for h in heads]
    cross = [_dot(qb[h], st[h].astype(BF16)) * xi_ref[h] for h in heads]
    kv = [_dot((kr[h] * zeta_ref[h]).T.astype(BF16), vb[h]) for h in heads]
    o = [_dot(inner[h], vb[h]) + cross[h] for h in heads]
    for h in heads:
        state_scr[h] = st[h] * math.exp(log_gamma[h] * RET_CHUNK) + kv[h]
        mu = jnp.mean(o[h], axis=-1, keepdims=True)
        oc = o[h] - mu
        var = jnp.mean(oc * oc, axis=-1, keepdims=True)
        o_ref[:, hsl[h]] = (_silu(g_ref[:, hsl[h]]) * (oc * lax.rsqrt(var + EPS))).astype(BF16)


def _retention_tables(seq):
    half = HEAD_DIM // 2
    inv = 1.0 / (ROPE_BASE ** (jnp.arange(half, dtype=F32) / half))
    ang = jnp.arange(seq).astype(F32)[:, None] * inv[None, :]
    cos, sin = jnp.cos(ang), jnp.sin(ang)
    cos_t = jnp.concatenate([cos, cos], axis=-1)
    sin_t = jnp.concatenate([-sin, sin], axis=-1)
    c = RET_CHUNK
    log_gamma = jnp.log1p(-jnp.exp2(-5.0 - jnp.arange(N_HEADS, dtype=F32)))
    n = jnp.arange(c, dtype=F32)
    diff = n[:, None] - n[None, :]
    decay = jnp.where(diff >= 0, jnp.exp(log_gamma[:, None, None] * jnp.maximum(diff, 0.0)), 0.0)
    zeta = jnp.exp(log_gamma[:, None] * (c - 1 - n)[None, :])
    xi = jnp.exp(log_gamma[:, None] * (n + 1)[None, :])
    ones = jnp.ones((1, 1, HEAD_DIM), F32)
    return cos_t, sin_t, decay, zeta[:, :, None] * ones, xi[:, :, None] * ones


def retention(proj, tables, batch, seq):
    n = proj.shape[0]
    c = RET_CHUNK
    nc = seq // c
    cos_t, sin_t, decay, zeta, xi = tables
    col = lambda off: pl.BlockSpec((c, WIDTH), lambda b, i: (b * nc + i, off // WIDTH))
    full3 = pl.BlockSpec((N_HEADS, c, HEAD_DIM), lambda b, i: (0, 0, 0))
    return pl.pallas_call(
        _retention_kernel,
        grid=(batch, nc),
        in_specs=[col(C_RQ), col(C_RK), col(C_RV), col(C_RG),
                  pl.BlockSpec((c, HEAD_DIM), lambda b, i: (i, 0)),
                  pl.BlockSpec((c, HEAD_DIM), lambda b, i: (i, 0)),
                  full3, full3, full3],
        out_specs=pl.BlockSpec((c, WIDTH), lambda b, i: (b * nc + i, 0)),
        out_shape=jax.ShapeDtypeStruct((n, WIDTH), BF16),
        scratch_shapes=[pltpu.VMEM((N_HEADS, HEAD_DIM, HEAD_DIM), F32)],
        compiler_params=_cparams(("arbitrary", "arbitrary")),
        name="retention",
    )(proj, proj, proj, proj, cos_t, sin_t, decay, zeta, xi)


def _fox_kernel(qi_ref, ki_ref, q_ref, k_ref, v_ref, ctr_ref, o_ref, m_scr, acc_scr, *, t):
    qi = qi_ref[pl.program_id(1)]
    ki = ki_ref[pl.program_id(1)]

    @pl.when(ki == 0)
    def _():
        m_scr[...] = jnp.full_like(m_scr, -jnp.inf)
        acc_scr[...] = jnp.zeros_like(acc_scr)

    def step(masked):
        if masked:
            row = lax.broadcasted_iota(I32, (t, t), 0)
            colm = lax.broadcasted_iota(I32, (t, t), 1)
            keep = row >= colm
        ones = jnp.ones((t, HEAD_DIM), BF16)
        for h in range(N_HEADS):
            sl = slice(h * HEAD_DIM, (h + 1) * HEAD_DIM)
            qb = q_ref[:, sl].astype(BF16)
            kb = k_ref[:, sl].astype(BF16)
            c_k = ctr_ref[0, L_FF + h:L_FF + h + 1, :] * LOG2E
            s = _dot_nt(qb, kb) * (HEAD_DIM ** -0.5 * LOG2E) - c_k
            if masked:
                s = jnp.where(keep, s, -jnp.inf)
            m_old = m_scr[h]
            m_new = jnp.maximum(m_old, jnp.max(s, axis=-1, keepdims=True))
            alpha = jnp.exp2(m_old - m_new)
            p = jnp.exp2(s - m_new)
            v_aug = jnp.concatenate([v_ref[:, sl].astype(BF16), ones], axis=1)
            acc_scr[h] = alpha * acc_scr[h] + _dot(p.astype(BF16), v_aug)
            m_scr[h] = m_new

    @pl.when(ki < qi)
    def _():
        step(False)

    @pl.when(ki == qi)
    def _():
        step(True)
        for h in range(N_HEADS):
            sl = slice(h * HEAD_DIM, (h + 1) * HEAD_DIM)
            acc = acc_scr[h]
            o_ref[:, sl] = (acc[:, :HEAD_DIM] / acc[:, HEAD_DIM:]).astype(BF16)


def fox_attention(proj, prep_tr, batch, seq, *, t=512):
    n = proj.shape[0]
    nt = seq // t
    pairs = [(qi, ki) for qi in range(nt) for ki in range(qi + 1)]
    qi_arr = jnp.asarray([p[0] for p in pairs], I32)
    ki_arr = jnp.asarray([p[1] for p in pairs], I32)
    qspec = pl.BlockSpec((t, WIDTH), lambda b, s, qi, ki: (b * nt + qi[s], C_FQ // WIDTH))
    kspec = lambda off: pl.BlockSpec(
        (t, WIDTH), lambda b, s, qi, ki: (b * nt + ki[s], off // WIDTH))
    return pl.pallas_call(
        functools.partial(_fox_kernel, t=t),
        grid_spec=pltpu.PrefetchScalarGridSpec(
            num_scalar_prefetch=2,
            grid=(batch, len(pairs)),
            in_specs=[qspec, kspec(C_FK), kspec(C_FV),
                      pl.BlockSpec((1, PREP_ROWS, t), lambda b, s, qi, ki: (b, 0, ki[s]))],
            out_specs=pl.BlockSpec((t, WIDTH), lambda b, s, qi, ki: (b * nt + qi[s], 0)),
            scratch_shapes=[pltpu.VMEM((N_HEADS, t, 1), F32),
                            pltpu.VMEM((N_HEADS, t, 2 * HEAD_DIM), F32)]),
        out_shape=jax.ShapeDtypeStruct((n, WIDTH), BF16),
        compiler_params=_cparams(("arbitrary", "arbitrary")),
        name="fox_attention",
    )(qi_arr, ki_arr, proj, proj, proj, prep_tr)


def _dsa_proj_kernel(cq_ref, g_ref, wq_ref, wi_ref, q_ref, qi_ref):
    x = cq_ref[...]
    ms = jnp.mean(x * x, axis=-1, keepdims=True)
    cb = (x * lax.rsqrt(ms + EPS) * g_ref[...]).astype(BF16)
    q_ref[...] = _dot(cb, wq_ref[...]).astype(BF16)
    qi_ref[...] = _dot(cb, wi_ref[...]).astype(BF16)


def dsa_proj(proj, cq_norm, w_uq, w_qidx, *, tm=512):
    n = proj.shape[0]
    r = DSA_Q_RANK
    wi = IDX_HEADS * IDX_DIM
    return pl.pallas_call(
        _dsa_proj_kernel,
        grid=(n // tm,),
        in_specs=[pl.BlockSpec((tm, r), lambda i: (i, C_DCQ // r)),
                  pl.BlockSpec((1, r), lambda i: (0, 0)),
                  pl.BlockSpec((r, WIDTH), lambda i: (0, 0)),
                  pl.BlockSpec((r, wi), lambda i: (0, 0))],
        out_specs=[pl.BlockSpec((tm, WIDTH), lambda i: (i, 0)),
                   pl.BlockSpec((tm, wi), lambda i: (i, 0))],
        out_shape=[jax.ShapeDtypeStruct((n, WIDTH), BF16),
                   jax.ShapeDtypeStruct((n, wi), BF16)],
        compiler_params=_cparams(("arbitrary",)),
        name="dsa_proj",
    )(proj, cq_norm.reshape(1, r), w_uq, w_qidx)


DSA_QB = 256
DSA_KC = 512
DSA_SCORE_MID_STEPS = 20
DSA_FEW_KEYS = 4
DSA_HALVE_FIXED = 12
DSA_WALK_FIXED = 3


def _score_to_key(s):
    b = pltpu.bitcast(s, I32)
    return b ^ ((b >> 31) & INT_MAX)


def _key_to_score(k):
    return pltpu.bitcast(k ^ ((k >> 31) & INT_MAX), F32)


def _t5_bucket(rel):
    max_exact = REL_BUCKETS // 2
    relf = jnp.maximum(rel, max_exact).astype(F32)
    large = max_exact + (jnp.log(relf / max_exact) / math.log(REL_MAX_DIST / max_exact)
                         * (REL_BUCKETS - max_exact)).astype(I32)
    large = jnp.minimum(large, REL_BUCKETS - 1)
    return jnp.where(rel < max_exact, rel, large)


def _dsa_kernel(rb_ref, q_ref, qi_ref, tok_ref, k_ref, v_ref, ka_ref, kb_ref, o_ref,
                key_scr, lg_scr, band_scr, kb16_scr, vt_scr, ka16_scr, kb16i_scr, *, seq, topk):
    qb_idx = pl.program_id(1)
    t0 = qb_idx * DSA_QB
    n_kc = (t0 + DSA_QB - 1) // DSA_KC + 1
    row_vec = (1, DSA_QB)

    @pl.when(qb_idx == 0)
    def _():
        kb16_scr[...] = k_ref[...].astype(BF16)
        ka16_scr[...] = ka_ref[...].astype(BF16)
        kb16i_scr[...] = kb_ref[...].astype(BF16)
        for c in range(seq // DSA_KC):
            cs = slice(c * DSA_KC, (c + 1) * DSA_KC)
            vt_scr[0:HEAD_DIM, cs] = v_ref[cs, :].T.astype(BF16)
        vt_scr[HEAD_DIM:, :] = jnp.ones((HEAD_DIM, seq), BF16)

    @pl.when((pl.program_id(0) == 0) & (qb_idx == 0))
    def _():
        j_ = lax.broadcasted_iota(I32, (2 * DSA_QB, DSA_QB), 0)
        i_ = lax.broadcasted_iota(I32, (2 * DSA_QB, DSA_QB), 1)
        rel = i_ + DSA_QB - j_
        bucket = _t5_bucket(rel)
        for h in range(N_HEADS):
            far = rb_ref[REL_BUCKETS - 1, h]
            band = jnp.zeros((2 * DSA_QB, DSA_QB), F32)
            for bk in range(REL_BUCKETS - 1):
                band = jnp.where(bucket == bk, (rb_ref[bk, h] - far) * LOG2E, band)
            band_scr[h] = jnp.where(rel >= 0, band, 0.0)

    w_t = tok_ref[...].T
    key_s = lax.broadcasted_iota(I32, (DSA_KC, DSA_QB), 0)
    row_t = t0 + lax.broadcasted_iota(I32, (DSA_KC, DSA_QB), 1)

    def score_chunk(c, carry):
        kmax, kmin = carry
        ks = pl.ds(pl.multiple_of(c * DSA_KC, DSA_KC), DSA_KC)
        ka = ka16_scr[ks, :]
        kb = kb16i_scr[ks, :]
        acc = jnp.zeros((DSA_KC, DSA_QB), F32)
        for p in range(IDX_HEADS // 2):
            qp = qi_ref[:, p * LANE:(p + 1) * LANE]
            acc = acc + jnp.maximum(_dot_nt(ka, qp), 0.0) * w_t[2 * p:2 * p + 1, :]
            acc = acc + jnp.maximum(_dot_nt(kb, qp), 0.0) * w_t[2 * p + 1:2 * p + 2, :]
        key = _score_to_key(acc)
        valid = (c * DSA_KC + key_s) <= row_t
        key_scr[ks, :] = jnp.where(valid, key, INT_MIN)
        kmax = jnp.maximum(kmax, jnp.max(jnp.where(valid, key, INT_MIN), axis=0, keepdims=True))
        kmin = jnp.minimum(kmin, jnp.min(jnp.where(valid, key, INT_MAX), axis=0, keepdims=True))
        return kmax, kmin

    kmax, kmin = lax.fori_loop(0, n_kc, score_chunk, (jnp.full(row_vec, INT_MIN, I32),
                                                     jnp.full(row_vec, INT_MAX, I32)))

    def scan_keys(cand, with_below):
        def body(c, carry):
            cnt, below = carry
            ks = pl.ds(pl.multiple_of(c * DSA_KC, DSA_KC), DSA_KC)
            keys = key_scr[ks, :]
            ge = keys >= cand
            ones = ge.astype(I32)
            low = jnp.where(ge, INT_MIN, keys)
            for u in range(DSA_KC // SUBLANE):
                us = slice(u * SUBLANE, (u + 1) * SUBLANE)
                cnt = cnt + ones[us, :]
                if with_below:
                    below = jnp.maximum(below, low[us, :])
            return cnt, below
        cnt, below = lax.fori_loop(0, n_kc, body, (jnp.zeros((SUBLANE, DSA_QB), I32),
                                                   jnp.full((SUBLANE, DSA_QB), INT_MIN, I32)))
        cnt = jnp.sum(cnt, axis=0, keepdims=True)
        if with_below:
            return cnt, jnp.max(below, axis=0, keepdims=True)
        return cnt

    def open_rows(lo, hi, c_lo):
        return (c_lo > topk) & (hi - 1 > lo)

    def any_row(flag):
        return jnp.max(jnp.where(flag, 1, 0))

    def update(cand, cnt, lo, hi, c_lo, c_hi):
        ge = cnt >= topk
        return (jnp.where(ge, cand, lo), jnp.where(ge, hi, cand),
                jnp.where(ge, cnt, c_lo), jnp.where(ge, c_hi, cnt))

    def crowded(lo, hi, c_lo, c_hi):
        return any_row(open_rows(lo, hi, c_lo) & (c_lo - c_hi > DSA_FEW_KEYS))

    def halve_step(it, lo, hi, c_lo, c_hi):
        key_mid = (lo >> 1) + (hi >> 1) + (lo & hi & 1)
        score_mid = _score_to_key(0.5 * _key_to_score(lo) + 0.5 * _key_to_score(hi - 1))
        cand = jnp.where(it < DSA_SCORE_MID_STEPS, score_mid, key_mid)
        cand = jnp.minimum(jnp.maximum(cand, lo + 1), hi - 1)
        cand = jnp.where(hi - 1 > lo, cand, lo)
        return update(cand, scan_keys(cand, False), lo, hi, c_lo, c_hi)

    def halve_body(st):
        it, _, lo, hi, c_lo, c_hi = st
        go = crowded(lo, hi, c_lo, c_hi)
        return (it + 1, go) + halve_step(it, lo, hi, c_lo, c_hi)

    def walk_step(lo, hi, c_lo, c_hi, nxt):
        is_open = open_rows(lo, hi, c_lo)
        cand = jnp.where(is_open, nxt, lo)
        cnt, below = scan_keys(cand, True)
        ge = cnt >= topk
        hi = jnp.where(is_open, jnp.where(ge, cand + 1, cand), hi)
        c_hi = jnp.where(is_open & jnp.logical_not(ge), cnt, c_hi)
        lo = jnp.where(is_open & ge, cand, lo)
        c_lo = jnp.where(is_open & ge, cnt, c_lo)
        nxt = jnp.where(ge, nxt, below)
        return lo, hi, c_lo, c_hi, nxt

    def walk_body(st):
        go = any_row(open_rows(st[1], st[2], st[3]))
        return (go,) + walk_step(*st[1:])

    n_valid = jnp.minimum(t0 + lax.broadcasted_iota(I32, row_vec, 1) + 1, seq)
    st = (kmin, kmax + 1, n_valid, jnp.zeros(row_vec, I32))
    st = lax.fori_loop(0, DSA_HALVE_FIXED, lambda it, s: halve_step(it, *s), st)
    st = lax.while_loop(lambda s: s[1] > 0, halve_body,
                        (jnp.int32(DSA_HALVE_FIXED), crowded(*st)) + st)[2:]
    _, nxt0 = scan_keys(st[1], True)
    st = lax.fori_loop(0, DSA_WALK_FIXED, lambda it, s: walk_step(*s), st + (nxt0,))
    _, thr, hi, n_ge, n_gt, _ = lax.while_loop(
        lambda s: s[0] > 0, walk_body, (any_row(open_rows(st[0], st[1], st[2])),) + st)

    tied = n_ge > topk
    has_tie = jnp.max(jnp.where(tied, 1, 0)) > 0

    @pl.when(has_tie)
    def _():
        room = (topk - n_gt).astype(F32)
        ii = lax.broadcasted_iota(I32, (LANE, LANE), 0)
        jj = lax.broadcasted_iota(I32, (LANE, LANE), 1)
        lower = (ii >= jj).astype(BF16)

        def body(c, seen):
            ks = pl.ds(pl.multiple_of(c * LANE, LANE), LANE)
            kk = key_scr[ks, :]
            eq = kk == thr
            rank = seen + _dot(lower, eq.astype(BF16))
            drop = eq & (rank > room) & tied
            key_scr[ks, :] = jnp.where(drop, INT_MIN, kk)
            return seen + jnp.sum(eq.astype(F32), axis=0, keepdims=True)
        lax.fori_loop(0, n_kc * (DSA_KC // LANE), body, jnp.zeros(row_vec, F32))

    def mask_chunk(c, carry):
        ks = pl.ds(pl.multiple_of(c * DSA_KC, DSA_KC), DSA_KC)
        sel = jnp.where(key_scr[ks, :] >= thr, 0.0, -jnp.inf).astype(F32)
        key_scr[ks, :] = pltpu.bitcast(sel, I32)
        return carry
    lax.fori_loop(0, n_kc, mask_chunk, 0)

    heads = range(N_HEADS)
    hsl = [slice(h * HEAD_DIM, (h + 1) * HEAD_DIM) for h in heads]

    def logit_chunk(c, ms):
        ks = pl.ds(pl.multiple_of(c * DSA_KC, DSA_KC), DSA_KC)
        k_c = kb16_scr[ks, :]
        sel = pltpu.bitcast(key_scr[ks, :], F32)
        out = []
        for h in heads:
            s = _dot_nt(k_c, q_ref[:, hsl[h]]) * (HEAD_DIM ** -0.5 * LOG2E) + sel
            lg_scr[h, ks, :] = s
            out.append(jnp.maximum(ms[h], jnp.max(s, axis=0, keepdims=True)))
        return tuple(out)
    ms = lax.fori_loop(0, n_kc, logit_chunk,
                       tuple(jnp.full(row_vec, -jnp.inf, F32) for _ in heads))

    band_off = pl.multiple_of(jnp.maximum(qb_idx - 1, 0) * DSA_QB, DSA_QB)
    ws = pl.ds(band_off, 2 * DSA_QB)
    ms = list(ms)
    for h in heads:
        band_h = band_scr[h]
        band_first = jnp.concatenate([band_h[DSA_QB:, :], jnp.zeros((DSA_QB, DSA_QB), F32)], axis=0)
        win = lg_scr[h, ws, :] + jnp.where(qb_idx == 0, band_first, band_h)
        lg_scr[h, ws, :] = win
        ms[h] = jnp.maximum(ms[h], jnp.max(win, axis=0, keepdims=True))

    def pv_chunk(c, accs):
        ks = pl.ds(pl.multiple_of(c * DSA_KC, DSA_KC), DSA_KC)
        vt_c = vt_scr[:, ks]
        return tuple(accs[h] + _dot(vt_c, jnp.exp2(lg_scr[h, ks, :] - ms[h]).astype(BF16))
                     for h in heads)
    accs = lax.fori_loop(0, n_kc, pv_chunk,
                         tuple(jnp.zeros((2 * HEAD_DIM, DSA_QB), F32) for _ in heads))
    for h in heads:
        o_ref[:, hsl[h]] = (accs[h][:HEAD_DIM, :] / accs[h][HEAD_DIM:, :]).T.astype(BF16)


def dsa_attention(proj, q, q_idx, prep_tok, rel_bias, batch, seq):
    n = proj.shape[0]
    nq = seq // DSA_QB
    topk = min(TOPK_MAX, seq // 4)
    wi = IDX_HEADS * IDX_DIM
    rowblk = lambda w, cb: pl.BlockSpec((DSA_QB, w), lambda b, i: (b * nq + i, cb))
    seqblk = lambda off: pl.BlockSpec((seq, LANE), lambda b, i: (b, off // LANE))
    return pl.pallas_call(
        functools.partial(_dsa_kernel, seq=seq, topk=topk),
        grid=(batch, nq),
        in_specs=[pl.BlockSpec(memory_space=pltpu.SMEM),
                  rowblk(WIDTH, 0), rowblk(wi, 0), rowblk(LANE, 0),
                  seqblk(C_DK), seqblk(C_DV), seqblk(C_KA), seqblk(C_KB)],
        out_specs=pl.BlockSpec((DSA_QB, WIDTH), lambda b, i: (b * nq + i, 0)),
        out_shape=jax.ShapeDtypeStruct((n, WIDTH), BF16),
        scratch_shapes=[pltpu.VMEM((seq, DSA_QB), I32),
                        pltpu.VMEM((N_HEADS, seq, DSA_QB), F32),
                        pltpu.VMEM((N_HEADS, 2 * DSA_QB, DSA_QB), F32),
                        pltpu.VMEM((seq, LANE), BF16),
                        pltpu.VMEM((2 * HEAD_DIM, seq), BF16),
                        pltpu.VMEM((seq, LANE), BF16),
                        pltpu.VMEM((seq, LANE), BF16)],
        compiler_params=_cparams(("arbitrary", "arbitrary")),
        name="dsa_attention",
    )(rel_bias, q, q_idx, prep_tok, proj, proj, proj, proj)


GDN_T = 256
GDN_GROUP = 2
GDN_HALO = 8


def _gdn_kernel(q_ref, k_ref, v_ref, z_ref, cw_ref, ng_ref, tok_ref, tr_ref, o_ref,
                xq_scr, xk_scr, xv_scr, state_scr):
    first = pl.program_id(1) == 0

    @pl.when(first)
    def _():
        state_scr[...] = jnp.zeros_like(state_scr)
        for scr in (xq_scr, xk_scr, xv_scr):
            scr[0:GDN_HALO, :] = jnp.zeros((GDN_HALO, WIDTH), F32)

    def conv(x_ref, scr, w_off):
        scr[GDN_HALO:, :] = x_ref[...]
        y = jnp.zeros((GDN_T, WIDTH), F32)
        for i in range(GDN_CONV):
            st = GDN_HALO - (GDN_CONV - 1) + i
            y = y + scr[st:st + GDN_T, :] * cw_ref[i:i + 1, w_off:w_off + WIDTH]
        scr[0:GDN_HALO, :] = scr[GDN_T:GDN_T + GDN_HALO, :]
        return _silu(y)

    qc = conv(q_ref, xq_scr, 0)
    kc = conv(k_ref, xk_scr, WIDTH)
    vc = conv(v_ref, xv_scr, 2 * WIDTH)
    tok = tok_ref[...]
    c = GDN_CHUNK
    heads = range(N_HEADS)
    hsl = [slice(h * HEAD_DIM, (h + 1) * HEAD_DIM) for h in heads]

    def l2norm_heads(x, scale):
        return jnp.concatenate(
            [x[:, s] * (lax.rsqrt(jnp.sum(x[:, s] * x[:, s], axis=-1, keepdims=True) + EPS) * scale)
             for s in hsl], axis=1)

    qf = l2norm_heads(qc, HEAD_DIM ** -0.5)
    kf = l2norm_heads(kc, 1.0)

    grp = GDN_GROUP
    gw = grp * HEAD_DIM
    nb = grp * c
    ri = lax.broadcasted_iota(I32, (nb, nb), 0)
    ci = lax.broadcasted_iota(I32, (nb, nb), 1)
    tril = ((ri // c) == (ci // c)) & (ri >= ci)
    eye = (ri == ci).astype(F32)
    pair_masks = []
    for lg in range(c.bit_length() - 1):
        pair_masks.append(((ri >> (lg + 1)) == (ci >> (lg + 1)))
                          & (((ri >> lg) & 1) == 1) & (((ci >> lg) & 1) == 0))
    lane_head = lax.broadcasted_iota(I32, (c, gw), 1) // HEAD_DIM
    row_head = lax.broadcasted_iota(I32, (nb, HEAD_DIM), 0) // c

    def spread(x):
        return jnp.concatenate([jnp.where(lane_head == u, x, 0.0) for u in range(grp)], axis=0)

    def stack(x):
        return jnp.concatenate([x[:, hsl[u]] for u in range(grp)], axis=0)

    def spread_lanes(x):
        return jnp.concatenate([jnp.where(row_head == u, x, 0.0) for u in range(grp)], axis=1)

    def split(x):
        hi = x.astype(BF16)
        return hi, (x - hi.astype(F32)).astype(BF16)

    def dot_split(a, b):
        return _dot(a[0], b[0]) + (_dot(a[0], b[1]) + _dot(a[1], b[0]))

    items = [(j, gi) for j in range(GDN_T // c) for gi in range(N_HEADS // grp)]
    pre = []
    for j, gi in items:
        rs = slice(j * c, (j + 1) * c)
        last = slice((j + 1) * c - 1, (j + 1) * c)
        gh = [gi * grp + u for u in range(grp)]
        gsl = slice(gi * gw, (gi + 1) * gw)
        qj, kj, vj = qf[rs, gsl], kf[rs, gsl], vc[rs, gsl]
        b_col = jnp.concatenate([tok[rs, L_GB + h:L_GB + h + 1] for h in gh], axis=0)
        g_col = jnp.concatenate([tok[rs, L_GA + h:L_GA + h + 1] for h in gh], axis=0)
        g_row = jnp.concatenate([tr_ref[0, L_GA + h:L_GA + h + 1, rs] for h in gh], axis=1)
        g_last = [tok[last, L_GA + h:L_GA + h + 1] for h in gh]
        g_last_col = jnp.concatenate([jnp.broadcast_to(g, (c, 1)) for g in g_last], axis=0)
        k_sp = spread(kj)
        q_sp = spread(qj)
        k_sp16 = k_sp.astype(BF16)
        decay = jnp.exp(jnp.where(tril, g_col - g_row, -jnp.inf))
        eg = jnp.exp(g_col)
        pre.append(dict(
            gh=gh, gsl=gsl,
            l_mat=b_col * _dot_nt(k_sp16, k_sp16) * decay,
            rhs=jnp.concatenate([stack(vj) * b_col, stack(kj) * (b_col * eg)], axis=1),
            qk=_dot_nt(q_sp.astype(BF16), k_sp16) * decay,
            q_dec=q_sp * eg,
            k_dec=k_sp * jnp.exp(g_last_col - g_col),
            e_last=jnp.concatenate([jnp.broadcast_to(jnp.exp(g), (HEAD_DIM, 1)) for g in g_last], axis=0)))

    t_inv = [eye - jnp.where(pair_masks[0], p["l_mat"], 0.0) for p in pre]
    for pm in pair_masks[1:]:
        t_s = [split(t) for t in t_inv]
        m_t = [dot_split(split(jnp.where(pm, p["l_mat"], 0.0)), ts) for p, ts in zip(pre, t_s)]
        t_inv = [t - dot_split(ts, split(m)) for t, ts, m in zip(t_inv, t_s, m_t)]
    sols = [dot_split(split(t), split(p["rhs"])) for t, p in zip(t_inv, pre)]

    outs = [[] for _ in heads]
    for p, sol in zip(pre, sols):
        u0 = sol[:, :HEAD_DIM]
        kcum = sol[:, HEAD_DIM:]
        st = state_scr[p["gsl"], :]
        stb = st.astype(BF16)
        v_new = u0 - _dot(spread_lanes(kcum).astype(BF16), stb)
        v_new_b = v_new.astype(BF16)
        o_st = _dot(p["q_dec"].astype(BF16), stb) + _dot(p["qk"].astype(BF16), v_new_b)
        state_scr[p["gsl"], :] = st * p["e_last"] + _dot(p["k_dec"].T.astype(BF16), v_new_b)
        for u, h in enumerate(p["gh"]):
            outs[h].append(o_st[u * c:(u + 1) * c, :])

    for h in heads:
        o = jnp.concatenate(outs[h], axis=0)
        ms = jnp.mean(o * o, axis=-1, keepdims=True)
        on = o * lax.rsqrt(ms + EPS) * ng_ref[...]
        o_ref[:, hsl[h]] = (on * _silu(z_ref[:, hsl[h]])).astype(BF16)


def gated_deltanet(proj, prep_tok, prep_tr, conv_w, norm_g, batch, seq):
    n = proj.shape[0]
    t = GDN_T
    nt = seq // t
    col = lambda off: pl.BlockSpec((t, WIDTH), lambda b, i: (b * nt + i, off // WIDTH))
    return pl.pallas_call(
        _gdn_kernel,
        grid=(batch, nt),
        in_specs=[col(C_GQ), col(C_GK), col(C_GV), col(C_GZ),
                  pl.BlockSpec((GDN_CONV, 3 * WIDTH), lambda b, i: (0, 0)),
                  pl.BlockSpec((1, HEAD_DIM), lambda b, i: (0, 0)),
                  pl.BlockSpec((t, LANE), lambda b, i: (b * nt + i, 0)),
                  pl.BlockSpec((1, PREP_ROWS, t), lambda b, i: (b, 0, i))],
        out_specs=pl.BlockSpec((t, WIDTH), lambda b, i: (b * nt + i, 0)),
        out_shape=jax.ShapeDtypeStruct((n, WIDTH), BF16),
        scratch_shapes=[pltpu.VMEM((t + GDN_HALO, WIDTH), F32),
                        pltpu.VMEM((t + GDN_HALO, WIDTH), F32),
                        pltpu.VMEM((t + GDN_HALO, WIDTH), F32),
                        pltpu.VMEM((N_HEADS * HEAD_DIM, HEAD_DIM), F32)],
        compiler_params=_cparams(("arbitrary", "arbitrary")),
        name="gated_deltanet",
    )(proj, proj, proj, proj, conv_w, norm_g.reshape(1, HEAD_DIM), prep_tok, prep_tr)


def _merge_kernel(h_ref, b0_ref, b1_ref, b2_ref, b3_ref, g0_ref, g1_ref, g2_ref, g3_ref,
                  wb_ref, o_ref):
    h = h_ref[...]
    acc = None
    for n, (b_ref, g_ref) in enumerate(zip((b0_ref, b1_ref, b2_ref, b3_ref),
                                           (g0_ref, g1_ref, g2_ref, g3_ref))):
        gate = jax.nn.sigmoid(_dot(h, g_ref[...]))
        term = gate * _dot(b_ref[...], wb_ref[n])
        acc = term if acc is None else acc + term
    o_ref[...] = acc.astype(BF16)


def merge_branches(h, branches, w_gate, w_branch, *, tm=512, tn=512):
    _, n, d = h.shape
    nj = d // tn
    bspec = pl.BlockSpec((tm, WIDTH), lambda j, i: (i, 0))
    gspec = lambda k: pl.BlockSpec((d, tn), lambda j, i: (0, k * nj + j))
    return pl.pallas_call(
        _merge_kernel,
        grid=(nj, n // tm),
        in_specs=[pl.BlockSpec((None, tm, d), lambda j, i: (0, i, 0)),
                  bspec, bspec, bspec, bspec,
                  gspec(0), gspec(1), gspec(2), gspec(3),
                  pl.BlockSpec((N_BRANCH, WIDTH, tn), lambda j, i: (0, 0, j))],
        out_specs=pl.BlockSpec((tm, tn), lambda j, i: (i, j)),
        out_shape=jax.ShapeDtypeStruct((n, d), BF16),
        compiler_params=_cparams(("arbitrary", "arbitrary")),
        name="merge_branches",
    )(h, *branches, w_gate, w_gate, w_gate, w_gate, w_branch)


def _resid_mm_kernel(a_ref, w_ref, x_ref, o_ref):
    o_ref[...] = x_ref[...] + _dot(a_ref[...], w_ref[...])


def resid_matmul(a, w, x, *, tm=512, tn=1024, name="resid_matmul"):
    n, k = a.shape
    d = w.shape[1]
    return pl.pallas_call(
        _resid_mm_kernel,
        grid=(d // tn, n // tm),
        in_specs=[pl.BlockSpec((tm, k), lambda j, i: (i, 0)),
                  pl.BlockSpec((k, tn), lambda j, i: (0, j)),
                  pl.BlockSpec((tm, tn), lambda j, i: (i, j))],
        out_specs=pl.BlockSpec((tm, tn), lambda j, i: (i, j)),
        out_shape=jax.ShapeDtypeStruct((n, d), F32),
        compiler_params=_cparams(("arbitrary", "arbitrary")),
        name=name,
    )(a, w, x)


FFN_HALO = 8
FFN_SUB = 512


def _ffn1_kernel(x_ref, g_ref, wg_ref, wu_ref, cw_ref, cb_ref, o_ref, h_scr, gt_scr, halo_scr,
                 *, tm, tiles_per_seq):
    i = pl.program_id(1)

    @pl.when(i == 0)
    def _():
        halo_scr[...] = jnp.zeros_like(halo_scr)

    def body(r, carry):
        rows = pl.ds(pl.multiple_of(r * NORM_ROWS, NORM_ROWS), NORM_ROWS)
        h_scr[rows, :] = _rmsnorm_rows(x_ref, g_ref, rows).astype(BF16)
        return carry
    lax.fori_loop(0, tm // NORM_ROWS, body, 0)

    h = h_scr[...]
    seq_start = (i % tiles_per_seq) == 0
    tn = o_ref.shape[1]
    for off in range(0, tn, FFN_SUB):
        cs = slice(off, min(off + FFN_SUB, tn))
        g = _dot(h, wg_ref[:, cs])
        gt_scr[FFN_HALO:, cs] = g
        gt_scr[0:FFN_HALO, cs] = jnp.where(seq_start, 0.0, halo_scr[:, cs])
        halo_scr[:, cs] = g[tm - FFN_HALO:, :]
        y = cb_ref[:, cs] + g * cw_ref[FFN_CONV - 1:FFN_CONV, cs]
        for t in range(FFN_CONV - 1):
            st = FFN_HALO - (FFN_CONV - 1) + t
            y = y + gt_scr[st:st + tm, cs] * cw_ref[t:t + 1, cs]
        o_ref[:, cs] = (_silu(y) * _dot(h, wu_ref[:, cs])).astype(BF16)


def conv_ffn_up(x, gain, w_gate, w_up, conv_w, conv_b, seq, *, tm=512, col_parts=2):
    n, d = x.shape
    f = w_gate.shape[1]
    tn = f // col_parts
    wspec = pl.BlockSpec((d, tn), lambda j, i: (0, j), pipeline_mode=pl.Buffered(1))
    return pl.pallas_call(
        functools.partial(_ffn1_kernel, tm=tm, tiles_per_seq=seq // tm),
        grid=(col_parts, n // tm),
        in_specs=[pl.BlockSpec((tm, d), lambda j, i: (i, 0)),
                  pl.BlockSpec((1, d), lambda j, i: (0, 0)),
                  wspec, wspec,
                  pl.BlockSpec((FFN_CONV, tn), lambda j, i: (0, j)),
                  pl.BlockSpec((1, tn), lambda j, i: (0, j))],
        out_specs=pl.BlockSpec((tm, tn), lambda j, i: (i, j)),
        out_shape=jax.ShapeDtypeStruct((n, f), BF16),
        scratch_shapes=[pltpu.VMEM((tm, d), BF16),
                        pltpu.VMEM((tm + FFN_HALO, tn), F32),
                        pltpu.VMEM((FFN_HALO, tn), F32)],
        compiler_params=_cparams(("arbitrary", "arbitrary")),
        name="conv_ffn_up",
    )(x, gain.reshape(1, d), w_gate, w_up, conv_w, conv_b.reshape(1, f))


IN_SIZES = (WIDTH, WIDTH, WIDTH, WIDTH,
            DSA_Q_RANK, HEAD_DIM, HEAD_DIM, IDX_DIM, IDX_HEADS,
            WIDTH, WIDTH, WIDTH, N_HEADS,
            WIDTH, WIDTH, WIDTH, WIDTH, N_HEADS, N_HEADS)
IN_NAMES = ("r_q", "r_k", "r_v", "r_g", "d_cq", "d_k", "d_v", "i_k", "i_w",
            "f_q", "f_k", "f_v", "f_f", "g_q", "g_k", "g_v", "g_z", "g_b", "g_a")
IN_PLAN = (("r_q", C_RQ), ("r_k", C_RK), ("r_v", C_RV), ("r_g", C_RG),
           ("f_q", C_FQ), ("f_k", C_FK), ("f_v", C_FV),
           ("g_q", C_GQ), ("g_k", C_GK), ("g_v", C_GV), ("g_z", C_GZ),
           ("d_k", C_DK), ("d_cq", C_DCQ), ("d_v", C_DV),
           ("i_k", C_KA), ("i_k", C_KB + IDX_DIM),
           ("i_w", C_SM + L_IW), ("f_f", C_SM + L_FF), ("g_b", C_SM + L_GB), ("g_a", C_SM + L_GA))


def _prep_w_in_kernel(w_ref, m_ref, g_ref):
    src = {}
    off = 0
    for name, size in zip(IN_NAMES, IN_SIZES):
        src[name] = (off, size)
        off += size
    m_ref[...] = jnp.zeros_like(m_ref)
    for name, dst in IN_PLAN:
        so, w = src[name]
        m_ref[:, dst:dst + w] = w_ref[:, so:so + w].astype(BF16)
    g_ref[...] = w_ref[:, off:off + g_ref.shape[1]].astype(BF16)


def prep_w_in(w_in, layer, *, tr=128):
    _, d, c = w_in.shape
    return pl.pallas_call(
        _prep_w_in_kernel,
        grid=(d // tr,),
        in_specs=[pl.BlockSpec((None, tr, c), lambda i: (layer, i, 0))],
        out_specs=[pl.BlockSpec((tr, C_TOT), lambda i: (i, 0)),
                   pl.BlockSpec((tr, N_BRANCH * d), lambda i: (i, 0))],
        out_shape=[jax.ShapeDtypeStruct((d, C_TOT), BF16),
                   jax.ShapeDtypeStruct((d, N_BRANCH * d), BF16)],
        compiler_params=_cparams(("arbitrary",)),
        name="prep_w_in",
    )(w_in)


def _cast_kernel(w_ref, o_ref):
    o_ref[...] = w_ref[...].astype(BF16)


def cast_layer(w, layer, *, tr=256):
    _, r, c = w.shape
    if r % tr:
        tr = r
    return pl.pallas_call(
        _cast_kernel,
        grid=(r // tr,),
        in_specs=[pl.BlockSpec((None, tr, c), lambda i: (layer, i, 0))],
        out_specs=pl.BlockSpec((tr, c), lambda i: (i, 0)),
        out_shape=jax.ShapeDtypeStruct((r, c), BF16),
        compiler_params=_cparams(("arbitrary",)),
        name="cast_bf16",
    )(w)


def cast_branch(w_branch, layer):
    _, nbr, r, c = w_branch.shape
    return pl.pallas_call(
        _cast_kernel,
        grid=(nbr,),
        in_specs=[pl.BlockSpec((None, None, r, c), lambda i: (layer, i, 0, 0))],
        out_specs=pl.BlockSpec((None, r, c), lambda i: (i, 0, 0)),
        out_shape=jax.ShapeDtypeStruct((nbr, r, c), BF16),
        compiler_params=_cparams(("arbitrary",)),
        name="cast_branch",
    )(w_branch)


def kernel(x, norm_mix, w_in, dsa_cq_norm, dsa_w_uq, dsa_w_qidx, fox_f_bias, gdn_conv, gdn_a_log,
           gdn_dt_bias, gdn_norm, w_branch, w_out, rel_bias, norm_ffn, ffn_w_gate, ffn_w_up,
           ffn_conv, ffn_conv_b, ffn_w_down, final_norm):
    batch, seq, d = x.shape
    depth = w_in.shape[0]
    xf = x.reshape(batch * seq, d)
    ret_tables = _retention_tables(seq)
    for l in range(depth):
        w_main, w_gate = prep_w_in(w_in, l)
        proj, h = norm_proj(xf, norm_mix[l], w_main)
        par = jnp.zeros((SUBLANE, LANE), F32)
        par = par.at[0, L_FF:L_FF + N_HEADS].set(fox_f_bias[l])
        par = par.at[0, L_GA:L_GA + N_HEADS].set(gdn_dt_bias[l])
        par = par.at[1, L_GA:L_GA + N_HEADS].set(gdn_a_log[l])
        prep_tok, prep_tr = prep_small(proj, par, batch, seq)
        o_ret = retention(proj, ret_tables, batch, seq)
        q_dsa, q_idx = dsa_proj(proj, dsa_cq_norm[l], cast_layer(dsa_w_uq, l), cast_layer(dsa_w_qidx, l))
        o_dsa = dsa_attention(proj, q_dsa, q_idx, prep_tok, rel_bias, batch, seq)
        o_fox = fox_attention(proj, prep_tr, batch, seq)
        o_gdn = gated_deltanet(proj, prep_tok, prep_tr, gdn_conv[l], gdn_norm[l], batch, seq)
        merged = merge_branches(h, (o_ret, o_dsa, o_fox, o_gdn), w_gate, cast_branch(w_branch, l))
        xf = resid_matmul(merged, cast_layer(w_out, l), xf, tn=d, name="out_proj")
        act = conv_ffn_up(xf, norm_ffn[l], cast_layer(ffn_w_gate, l), cast_layer(ffn_w_up, l),
                          ffn_conv[l], ffn_conv_b[l], seq)
        xf = resid_matmul(act, cast_layer(ffn_w_down, l), xf, name="ffn_down")
    return rmsnorm(xf, final_norm).reshape(batch, seq, d)
```

```python
import functools
import math

import jax
import jax.numpy as jnp
from jax import lax
from jax.experimental import pallas as pl
from jax.experimental.pallas import tpu as pltpu

F32 = jnp.float32
BF16 = jnp.bfloat16
I32 = jnp.int32

HEAD_DIM = 128
N_HEADS = 4
WIDTH = N_HEADS * HEAD_DIM
N_BRANCH = 4
RET_CHUNK = 128
ROPE_BASE = 10000.0
DSA_Q_RANK = 384
IDX_HEADS = 16
IDX_DIM = 64
TOPK_MAX = 256
GDN_CONV = 4
GDN_CHUNK = 64
REL_BUCKETS = 32
REL_MAX_DIST = 128
FFN_CONV = 3
EPS = 1e-6

LANE = 128
SUBLANE = 8
VMEM_LIMIT = 56 * 1024 * 1024

C_RQ, C_RK, C_RV, C_RG = 0, 512, 1024, 1536
C_GQ, C_GK, C_GV, C_GZ = 2048, 2560, 3072, 3584
C_SM, C_DCQ = 4096, 4224
C_TOT32 = 4608
C_FQ, C_FK, C_FV = 0, 512, 1024
C_DK, C_DV, C_KA, C_KB = 1536, 1664, 1792, 1920
C_TOT16 = 2048
L_IW, L_FF, L_GB, L_GA = 0, 16, 20, 24

LOG2E = 1.4426950408889634
INT_MIN = -(2 ** 31)
INT_MAX = 2 ** 31 - 1
HIGHEST = lax.Precision.HIGHEST


def _cparams(sem, vmem=VMEM_LIMIT):
    return pltpu.CompilerParams(dimension_semantics=sem, vmem_limit_bytes=vmem)


def _dot(a, b):
    return jnp.dot(a, b, preferred_element_type=F32)


def _dot_nt(a, b):
    return lax.dot_general(a, b, (((1,), (1,)), ((), ())), preferred_element_type=F32)


def _silu(x):
    return x * jax.nn.sigmoid(x)


NORM_ROWS = 128


def _rmsnorm_rows(x_ref, g_ref, rows):
    x = x_ref[rows, :]
    ms = jnp.mean(x * x, axis=-1, keepdims=True)
    return x * lax.rsqrt(ms + EPS) * g_ref[...]


PROJ_SUB = 512


def _norm_proj_kernel(x_ref, g_ref, w_ref, o_ref, *rest, tm):
    h_scr = rest[-1]
    h_ref = rest[0] if len(rest) == 2 else None

    def body(r, carry):
        rows = pl.ds(pl.multiple_of(r * NORM_ROWS, NORM_ROWS), NORM_ROWS)
        hb = _rmsnorm_rows(x_ref, g_ref, rows).astype(BF16)
        h_scr[rows, :] = hb
        if h_ref is not None:
            h_ref[rows, :] = hb
        return carry
    lax.fori_loop(0, tm // NORM_ROWS, body, 0)

    h = h_scr[...]
    tn = o_ref.shape[1]
    for off in range(0, tn, PROJ_SUB):
        cs = slice(off, min(off + PROJ_SUB, tn))
        o_ref[:, cs] = _dot(h, w_ref[:, cs]).astype(o_ref.dtype)


def norm_proj(x, gain, w, *, out_dtype, emit_h, col_parts, tm=512):
    n, d = x.shape
    c = w.shape[1]
    tn = c // col_parts
    out_specs = [pl.BlockSpec((tm, tn), lambda j, i: (i, j))]
    out_shape = [jax.ShapeDtypeStruct((n, c), out_dtype)]
    if emit_h:
        out_specs.append(pl.BlockSpec((None, tm, d), lambda j, i: (j, i, 0)))
        out_shape.append(jax.ShapeDtypeStruct((col_parts, n, d), BF16))
    return pl.pallas_call(
        functools.partial(_norm_proj_kernel, tm=tm),
        grid=(col_parts, n // tm),
        in_specs=[pl.BlockSpec((tm, d), lambda j, i: (i, 0)),
                  pl.BlockSpec((1, d), lambda j, i: (0, 0)),
                  pl.BlockSpec((d, tn), lambda j, i: (0, j), pipeline_mode=pl.Buffered(1))],
        out_specs=out_specs,
        out_shape=out_shape,
        scratch_shapes=[pltpu.VMEM((tm, d), BF16)],
        compiler_params=_cparams(("arbitrary", "arbitrary")),
        name="norm_proj",
    )(x, gain.reshape(1, d), w)


def _rmsnorm_kernel(x_ref, g_ref, o_ref, *, tm):
    def body(r, carry):
        rows = pl.ds(pl.multiple_of(r * NORM_ROWS, NORM_ROWS), NORM_ROWS)
        o_ref[rows, :] = _rmsnorm_rows(x_ref, g_ref, rows)
        return carry
    lax.fori_loop(0, tm // NORM_ROWS, body, 0)


def rmsnorm(x, gain, *, tm=512):
    n, d = x.shape
    return pl.pallas_call(
        functools.partial(_rmsnorm_kernel, tm=tm),
        grid=(n // tm,),
        in_specs=[pl.BlockSpec((tm, d), lambda i: (i, 0)),
                  pl.BlockSpec((1, d), lambda i: (0, 0))],
        out_specs=pl.BlockSpec((tm, d), lambda i: (i, 0)),
        out_shape=jax.ShapeDtypeStruct((n, d), F32),
        compiler_params=_cparams(("arbitrary",)),
        name="final_rmsnorm",
    )(x, gain.reshape(1, d))


def _prep_kernel(s_ref, par_ref, tok_ref, tr_ref, carry_scr):
    @pl.when(pl.program_id(1) == 0)
    def _():
        carry_scr[...] = jnp.zeros_like(carry_scr)

    s = s_ref[...]
    lane = lax.broadcasted_iota(I32, (LANE, LANE), 1)
    row = lax.broadcasted_iota(I32, (LANE, LANE), 0)
    z = s + par_ref[0:1, :]
    soft = jnp.maximum(z, 0.0) + jnp.log1p(jnp.exp(-jnp.abs(z)))
    log_sig = z - soft
    sig = jax.nn.sigmoid(z)
    g_val = -jnp.exp(par_ref[1:2, :]) * soft
    is_f = (lane[0:1] >= L_FF) & (lane[0:1] < L_FF + N_HEADS)
    is_b = (lane[0:1] >= L_GB) & (lane[0:1] < L_GB + N_HEADS)
    is_a = (lane[0:1] >= L_GA) & (lane[0:1] < L_GA + N_HEADS)
    pre = jnp.where(is_f, log_sig, jnp.where(is_a, g_val, 0.0))
    tri = (row >= lane).astype(F32)
    tri_blk = ((row >= lane) & ((row // GDN_CHUNK) == (lane // GDN_CHUNK))).astype(F32)
    subs = [slice(u * LANE, (u + 1) * LANE) for u in range(PREP_T // LANE)]
    cum_full = [jnp.dot(tri, pre[u], precision=HIGHEST, preferred_element_type=F32) for u in subs]
    cum_blk = [jnp.dot(tri_blk, pre[u], precision=HIGHEST, preferred_element_type=F32) for u in subs]
    scale_iw = IDX_HEADS ** -0.5 * IDX_DIM ** -0.5
    carry = carry_scr[0:1, :]
    for u, cf, cb in zip(subs, cum_full, cum_blk):
        c_fox = cf + carry
        carry = c_fox[LANE - 1:LANE, :]
        out = jnp.where(is_f, c_fox,
                        jnp.where(is_a, cb,
                                  jnp.where(is_b, sig[u],
                                            jnp.where(lane[0:1] < IDX_HEADS, s[u] * scale_iw, 0.0))))
        tok_ref[u, :] = out
        tr_ref[0, :, u] = out.T[0:PREP_ROWS, :]
    carry_scr[0:1, :] = carry


PREP_T = 512
PREP_ROWS = 32


def prep_small(proj, par, batch, seq):
    n = proj.shape[0]
    nc = seq // PREP_T
    return pl.pallas_call(
        _prep_kernel,
        grid=(batch, nc),
        in_specs=[pl.BlockSpec((PREP_T, LANE), lambda b, c: (b * nc + c, C_SM // LANE)),
                  pl.BlockSpec((SUBLANE, LANE), lambda b, c: (0, 0))],
        out_specs=[pl.BlockSpec((PREP_T, LANE), lambda b, c: (b * nc + c, 0)),
                   pl.BlockSpec((1, PREP_ROWS, PREP_T), lambda b, c: (b, 0, c))],
        out_shape=[jax.ShapeDtypeStruct((n, LANE), F32),
                   jax.ShapeDtypeStruct((batch, PREP_ROWS, seq), F32)],
        scratch_shapes=[pltpu.VMEM((SUBLANE, LANE), F32)],
        compiler_params=_cparams(("arbitrary", "arbitrary")),
        name="prep_small",
    )(proj, par)


def _ret_gamma():
    return [math.log1p(-(2.0 ** (-5.0 - h))) for h in range(N_HEADS)]


def _retention_kernel(q_ref, k_ref, v_ref, g_ref, cos_ref, sin_ref, dec_ref, zeta_ref, xi_ref,
                      o_ref, state_scr):
    @pl.when(pl.program_id(1) == 0)
    def _():
        state_scr[...] = jnp.zeros_like(state_scr)

    cos_t = cos_ref[...]
    sin_t = sin_ref[...]
    log_gamma = _ret_gamma()
    heads = range(N_HEADS)
    hsl = [slice(h * HEAD_DIM, (h + 1) * HEAD_DIM) for h in heads]

    def rope(x):
        return x * cos_t + pltpu.roll(x, HEAD_DIM // 2, 1) * sin_t

    qb = [rope(q_ref[:, s]).astype(BF16) for s in hsl]
    kr = [rope(k_ref[:, s]) * (HEAD_DIM ** -0.5) for s in hsl]
    kb = [x.astype(BF16) for x in kr]
    vb = [v_ref[:, s].astype(BF16) for s in hsl]
    st = [state_scr[h] for h in heads]
    inner = [(_dot_nt(qb[h], kb[h]) * dec_ref[h]).astype(BF16) for h in heads]
    cross = [_dot(qb[h], st[h].astype(BF16)) * xi_ref[h] for h in heads]
    kv = [_dot((kr[h] * zeta_ref[h]).T.astype(BF16), vb[h]) for h in heads]
    o = [_dot(inner[h], vb[h]) + cross[h] for h in heads]
    for h in heads:
        state_scr[h] = st[h] * math.exp(log_gamma[h] * RET_CHUNK) + kv[h]
        mu = jnp.mean(o[h], axis=-1, keepdims=True)
        oc = o[h] - mu
        var = jnp.mean(oc * oc, axis=-1, keepdims=True)
        o_ref[:, hsl[h]] = (_silu(g_ref[:, hsl[h]]) * (oc * lax.rsqrt(var + EPS))).astype(BF16)


def _retention_tables(seq):
    half = HEAD_DIM // 2
    inv = 1.0 / (ROPE_BASE ** (jnp.arange(half, dtype=F32) / half))
    ang = jnp.arange(seq).astype(F32)[:, None] * inv[None, :]
    cos, sin = jnp.cos(ang), jnp.sin(ang)
    cos_t = jnp.concatenate([cos, cos], axis=-1)
    sin_t = jnp.concatenate([-sin, sin], axis=-1)
    c = RET_CHUNK
    log_gamma = jnp.log1p(-jnp.exp2(-5.0 - jnp.arange(N_HEADS, dtype=F32)))
    n = jnp.arange(c, dtype=F32)
    diff = n[:, None] - n[None, :]
    decay = jnp.where(diff >= 0, jnp.exp(log_gamma[:, None, None] * jnp.maximum(diff, 0.0)), 0.0)
    zeta = jnp.exp(log_gamma[:, None] * (c - 1 - n)[None, :])
    xi = jnp.exp(log_gamma[:, None] * (n + 1)[None, :])
    ones = jnp.ones((1, 1, HEAD_DIM), F32)
    return cos_t, sin_t, decay, zeta[:, :, None] * ones, xi[:, :, None] * ones


def retention(proj, tables, batch, seq):
    n = proj.shape[0]
    c = RET_CHUNK
    nc = seq // c
    cos_t, sin_t, decay, zeta, xi = tables
    col = lambda off: pl.BlockSpec((c, WIDTH), lambda b, i: (b * nc + i, off // WIDTH))
    full3 = pl.BlockSpec((N_HEADS, c, HEAD_DIM), lambda b, i: (0, 0, 0))
    return pl.pallas_call(
        _retention_kernel,
        grid=(batch, nc),
        in_specs=[col(C_RQ), col(C_RK), col(C_RV), col(C_RG),
                  pl.BlockSpec((c, HEAD_DIM), lambda b, i: (i, 0)),
                  pl.BlockSpec((c, HEAD_DIM), lambda b, i: (i, 0)),
                  full3, full3, full3],
        out_specs=pl.BlockSpec((c, WIDTH), lambda b, i: (b * nc + i, 0)),
        out_shape=jax.ShapeDtypeStruct((n, WIDTH), BF16),
        scratch_shapes=[pltpu.VMEM((N_HEADS, HEAD_DIM, HEAD_DIM), F32)],
        compiler_params=_cparams(("arbitrary", "arbitrary")),
        name="retention",
    )(proj, proj, proj, proj, cos_t, sin_t, decay, zeta, xi)


def _fox_kernel(qi_ref, ki_ref, q_ref, k_ref, v_ref, ctr_ref, o_ref, m_scr, acc_scr, *, t):
    qi = qi_ref[pl.program_id(1)]
    ki = ki_ref[pl.program_id(1)]

    @pl.when(ki == 0)
    def _():
        m_scr[...] = jnp.full_like(m_scr, -jnp.inf)
        acc_scr[...] = jnp.zeros_like(acc_scr)

    def step(masked):
        if masked:
            row = lax.broadcasted_iota(I32, (t, t), 0)
            colm = lax.broadcasted_iota(I32, (t, t), 1)
            keep = row >= colm
        ones = jnp.ones((t, HEAD_DIM), BF16)
        for h in range(N_HEADS):
            sl = slice(h * HEAD_DIM, (h + 1) * HEAD_DIM)
            qb = q_ref[:, sl].astype(BF16)
            kb = k_ref[:, sl].astype(BF16)
            c_k = ctr_ref[0, L_FF + h:L_FF + h + 1, :] * LOG2E
            s = _dot_nt(qb, kb) * (HEAD_DIM ** -0.5 * LOG2E) - c_k
            if masked:
                s = jnp.where(keep, s, -jnp.inf)
            m_old = m_scr[h]
            m_new = jnp.maximum(m_old, jnp.max(s, axis=-1, keepdims=True))
            alpha = jnp.exp2(m_old - m_new)
            p = jnp.exp2(s - m_new)
            v_aug = jnp.concatenate([v_ref[:, sl].astype(BF16), ones], axis=1)
            acc_scr[h] = alpha * acc_scr[h] + _dot(p.astype(BF16), v_aug)
            m_scr[h] = m_new

    @pl.when(ki < qi)
    def _():
        step(False)

    @pl.when(ki == qi)
    def _():
        step(True)
        for h in range(N_HEADS):
            sl = slice(h * HEAD_DIM, (h + 1) * HEAD_DIM)
            acc = acc_scr[h]
            o_ref[:, sl] = (acc[:, :HEAD_DIM] / acc[:, HEAD_DIM:]).astype(BF16)


def fox_attention(proj, prep_tr, batch, seq, *, t=512):
    n = proj.shape[0]
    nt = seq // t
    pairs = [(qi, ki) for qi in range(nt) for ki in range(qi + 1)]
    qi_arr = jnp.asarray([p[0] for p in pairs], I32)
    ki_arr = jnp.asarray([p[1] for p in pairs], I32)
    qspec = pl.BlockSpec((t, WIDTH), lambda b, s, qi, ki: (b * nt + qi[s], C_FQ // WIDTH))
    kspec = lambda off: pl.BlockSpec(
        (t, WIDTH), lambda b, s, qi, ki: (b * nt + ki[s], off // WIDTH))
    return pl.pallas_call(
        functools.partial(_fox_kernel, t=t),
        grid_spec=pltpu.PrefetchScalarGridSpec(
            num_scalar_prefetch=2,
            grid=(batch, len(pairs)),
            in_specs=[qspec, kspec(C_FK), kspec(C_FV),
                      pl.BlockSpec((1, PREP_ROWS, t), lambda b, s, qi, ki: (b, 0, ki[s]))],
            out_specs=pl.BlockSpec((t, WIDTH), lambda b, s, qi, ki: (b * nt + qi[s], 0)),
            scratch_shapes=[pltpu.VMEM((N_HEADS, t, 1), F32),
                            pltpu.VMEM((N_HEADS, t, 2 * HEAD_DIM), F32)]),
        out_shape=jax.ShapeDtypeStruct((n, WIDTH), BF16),
        compiler_params=_cparams(("arbitrary", "arbitrary")),
        name="fox_attention",
    )(qi_arr, ki_arr, proj, proj, proj, prep_tr)


def _dsa_proj_kernel(cq_ref, g_ref, wq_ref, wi_ref, q_ref, qi_ref):
    x = cq_ref[...]
    ms = jnp.mean(x * x, axis=-1, keepdims=True)
    cb = (x * lax.rsqrt(ms + EPS) * g_ref[...]).astype(BF16)
    q_ref[...] = _dot(cb, wq_ref[...]).astype(BF16)
    qi_ref[...] = _dot(cb, wi_ref[...]).astype(BF16)


def dsa_proj(proj, cq_norm, w_uq, w_qidx, *, tm=512):
    n = proj.shape[0]
    r = DSA_Q_RANK
    wi = IDX_HEADS * IDX_DIM
    return pl.pallas_call(
        _dsa_proj_kernel,
        grid=(n // tm,),
        in_specs=[pl.BlockSpec((tm, r), lambda i: (i, C_DCQ // r)),
                  pl.BlockSpec((1, r), lambda i: (0, 0)),
                  pl.BlockSpec((r, WIDTH), lambda i: (0, 0)),
                  pl.BlockSpec((r, wi), lambda i: (0, 0))],
        out_specs=[pl.BlockSpec((tm, WIDTH), lambda i: (i, 0)),
                   pl.BlockSpec((tm, wi), lambda i: (i, 0))],
        out_shape=[jax.ShapeDtypeStruct((n, WIDTH), BF16),
                   jax.ShapeDtypeStruct((n, wi), BF16)],
        compiler_params=_cparams(("arbitrary",)),
        name="dsa_proj",
    )(proj, cq_norm.reshape(1, r), w_uq, w_qidx)


DSA_QB = 256
DSA_KC = 512
DSA_SCORE_MID_STEPS = 20
DSA_FEW_KEYS = 4
DSA_HALVE_FIXED = 12
DSA_WALK_FIXED = 3


def _score_to_key(s):
    b = pltpu.bitcast(s, I32)
    return b ^ ((b >> 31) & INT_MAX)


def _key_to_score(k):
    return pltpu.bitcast(k ^ ((k >> 31) & INT_MAX), F32)


def _t5_bucket(rel):
    max_exact = REL_BUCKETS // 2
    relf = jnp.maximum(rel, max_exact).astype(F32)
    large = max_exact + (jnp.log(relf / max_exact) / math.log(REL_MAX_DIST / max_exact)
                         * (REL_BUCKETS - max_exact)).astype(I32)
    large = jnp.minimum(large, REL_BUCKETS - 1)
    return jnp.where(rel < max_exact, rel, large)


def _dsa_kernel(rb_ref, q_ref, qi_ref, tok_ref, k_ref, v_ref, ka_ref, kb_ref, o_ref,
                key_scr, lg_scr, band_scr, vt_scr, *, seq, topk):
    qb_idx = pl.program_id(1)
    t0 = qb_idx * DSA_QB
    n_kc = (t0 + DSA_QB - 1) // DSA_KC + 1
    row_vec = (1, DSA_QB)

    @pl.when(qb_idx == 0)
    def _():
        for c in range(seq // DSA_KC):
            cs = slice(c * DSA_KC, (c + 1) * DSA_KC)
            vt_scr[0:HEAD_DIM, cs] = v_ref[cs, :].astype(F32).T.astype(BF16)
        vt_scr[HEAD_DIM:, :] = jnp.ones((HEAD_DIM, seq), BF16)

    @pl.when((pl.program_id(0) == 0) & (qb_idx == 0))
    def _():
        j_ = lax.broadcasted_iota(I32, (2 * DSA_QB, DSA_QB), 0)
        i_ = lax.broadcasted_iota(I32, (2 * DSA_QB, DSA_QB), 1)
        rel = i_ + DSA_QB - j_
        bucket = _t5_bucket(rel)
        for h in range(N_HEADS):
            far = rb_ref[REL_BUCKETS - 1, h]
            band = jnp.zeros((2 * DSA_QB, DSA_QB), F32)
            for bk in range(REL_BUCKETS - 1):
                band = jnp.where(bucket == bk, (rb_ref[bk, h] - far) * LOG2E, band)
            band_scr[h] = jnp.where(rel >= 0, band, 0.0)

    w_t = tok_ref[...].T
    key_s = lax.broadcasted_iota(I32, (DSA_KC, DSA_QB), 0)
    row_t = t0 + lax.broadcasted_iota(I32, (DSA_KC, DSA_QB), 1)

    def score_chunk(c, carry):
        kmax, kmin = carry
        ks = pl.ds(pl.multiple_of(c * DSA_KC, DSA_KC), DSA_KC)
        ka = ka_ref[ks, :]
        kb = kb_ref[ks, :]
        acc = jnp.zeros((DSA_KC, DSA_QB), F32)
        for p in range(IDX_HEADS // 2):
            qp = qi_ref[:, p * LANE:(p + 1) * LANE]
            acc = acc + jnp.maximum(_dot_nt(ka, qp), 0.0) * w_t[2 * p:2 * p + 1, :]
            acc = acc + jnp.maximum(_dot_nt(kb, qp), 0.0) * w_t[2 * p + 1:2 * p + 2, :]
        key = _score_to_key(acc)
        valid = (c * DSA_KC + key_s) <= row_t
        key_scr[ks, :] = jnp.where(valid, key, INT_MIN)
        kmax = jnp.maximum(kmax, jnp.max(jnp.where(valid, key, INT_MIN), axis=0, keepdims=True))
        kmin = jnp.minimum(kmin, jnp.min(jnp.where(valid, key, INT_MAX), axis=0, keepdims=True))
        return kmax, kmin

    kmax, kmin = lax.fori_loop(0, n_kc, score_chunk, (jnp.full(row_vec, INT_MIN, I32),
                                                     jnp.full(row_vec, INT_MAX, I32)))

    def scan_keys(cand, with_below):
        def body(c, carry):
            cnt, below = carry
            ks = pl.ds(pl.multiple_of(c * DSA_KC, DSA_KC), DSA_KC)
            keys = key_scr[ks, :]
            ge = keys >= cand
            ones = ge.astype(I32)
            low = jnp.where(ge, INT_MIN, keys)
            for u in range(DSA_KC // SUBLANE):
                us = slice(u * SUBLANE, (u + 1) * SUBLANE)
                cnt = cnt + ones[us, :]
                if with_below:
                    below = jnp.maximum(below, low[us, :])
            return cnt, below
        cnt, below = lax.fori_loop(0, n_kc, body, (jnp.zeros((SUBLANE, DSA_QB), I32),
                                                   jnp.full((SUBLANE, DSA_QB), INT_MIN, I32)))
        cnt = jnp.sum(cnt, axis=0, keepdims=True)
        if with_below:
            return cnt, jnp.max(below, axis=0, keepdims=True)
        return cnt

    def open_rows(lo, hi, c_lo):
        return (c_lo > topk) & (hi - 1 > lo)

    def any_row(flag):
        return jnp.max(jnp.where(flag, 1, 0))

    def update(cand, cnt, lo, hi, c_lo, c_hi):
        ge = cnt >= topk
        return (jnp.where(ge, cand, lo), jnp.where(ge, hi, cand),
                jnp.where(ge, cnt, c_lo), jnp.where(ge, c_hi, cnt))

    def crowded(lo, hi, c_lo, c_hi):
        return any_row(open_rows(lo, hi, c_lo) & (c_lo - c_hi > DSA_FEW_KEYS))

    def halve_step(it, lo, hi, c_lo, c_hi):
        key_mid = (lo >> 1) + (hi >> 1) + (lo & hi & 1)
        score_mid = _score_to_key(0.5 * _key_to_score(lo) + 0.5 * _key_to_score(hi - 1))
        cand = jnp.where(it < DSA_SCORE_MID_STEPS, score_mid, key_mid)
        cand = jnp.minimum(jnp.maximum(cand, lo + 1), hi - 1)
        cand = jnp.where(hi - 1 > lo, cand, lo)
        return update(cand, scan_keys(cand, False), lo, hi, c_lo, c_hi)

    def halve_body(st):
        it, _, lo, hi, c_lo, c_hi = st
        go = crowded(lo, hi, c_lo, c_hi)
        return (it + 1, go) + halve_step(it, lo, hi, c_lo, c_hi)

    def walk_step(lo, hi, c_lo, c_hi, nxt):
        is_open = open_rows(lo, hi, c_lo)
        cand = jnp.where(is_open, nxt, lo)
        cnt, below = scan_keys(cand, True)
        ge = cnt >= topk
        hi = jnp.where(is_open, jnp.where(ge, cand + 1, cand), hi)
        c_hi = jnp.where(is_open & jnp.logical_not(ge), cnt, c_hi)
        lo = jnp.where(is_open & ge, cand, lo)
        c_lo = jnp.where(is_open & ge, cnt, c_lo)
        nxt = jnp.where(ge, nxt, below)
        return lo, hi, c_lo, c_hi, nxt

    def walk_body(st):
        go = any_row(open_rows(st[1], st[2], st[3]))
        return (go,) + walk_step(*st[1:])

    n_valid = jnp.minimum(t0 + lax.broadcasted_iota(I32, row_vec, 1) + 1, seq)
    st = (kmin, kmax + 1, n_valid, jnp.zeros(row_vec, I32))
    st = lax.fori_loop(0, DSA_HALVE_FIXED, lambda it, s: halve_step(it, *s), st)
    st = lax.while_loop(lambda s: s[1] > 0, halve_body,
                        (jnp.int32(DSA_HALVE_FIXED), crowded(*st)) + st)[2:]
    _, nxt0 = scan_keys(st[1], True)
    st = lax.fori_loop(0, DSA_WALK_FIXED, lambda it, s: walk_step(*s), st + (nxt0,))
    _, thr, hi, n_ge, n_gt, _ = lax.while_loop(
        lambda s: s[0] > 0, walk_body, (any_row(open_rows(st[0], st[1], st[2])),) + st)

    tied = n_ge > topk
    has_tie = jnp.max(jnp.where(tied, 1, 0)) > 0

    @pl.when(has_tie)
    def _():
        room = (topk - n_gt).astype(F32)
        ii = lax.broadcasted_iota(I32, (LANE, LANE), 0)
        jj = lax.broadcasted_iota(I32, (LANE, LANE), 1)
        lower = (ii >= jj).astype(BF16)

        def body(c, seen):
            ks = pl.ds(pl.multiple_of(c * LANE, LANE), LANE)
            kk = key_scr[ks, :]
            eq = kk == thr
            rank = seen + _dot(lower, eq.astype(BF16))
            drop = eq & (rank > room) & tied
            key_scr[ks, :] = jnp.where(drop, INT_MIN, kk)
            return seen + jnp.sum(eq.astype(F32), axis=0, keepdims=True)
        lax.fori_loop(0, n_kc * (DSA_KC // LANE), body, jnp.zeros(row_vec, F32))

    def mask_chunk(c, carry):
        ks = pl.ds(pl.multiple_of(c * DSA_KC, DSA_KC), DSA_KC)
        sel = jnp.where(key_scr[ks, :] >= thr, 0.0, -jnp.inf).astype(F32)
        key_scr[ks, :] = pltpu.bitcast(sel, I32)
        return carry
    lax.fori_loop(0, n_kc, mask_chunk, 0)

    heads = range(N_HEADS)
    hsl = [slice(h * HEAD_DIM, (h + 1) * HEAD_DIM) for h in heads]

    def logit_chunk(c, ms):
        ks = pl.ds(pl.multiple_of(c * DSA_KC, DSA_KC), DSA_KC)
        k_c = k_ref[ks, :]
        sel = pltpu.bitcast(key_scr[ks, :], F32)
        out = []
        for h in heads:
            s = _dot_nt(k_c, q_ref[:, hsl[h]]) * (HEAD_DIM ** -0.5 * LOG2E) + sel
            lg_scr[h, ks, :] = s
            out.append(jnp.maximum(ms[h], jnp.max(s, axis=0, keepdims=True)))
        return tuple(out)
    ms = lax.fori_loop(0, n_kc, logit_chunk,
                       tuple(jnp.full(row_vec, -jnp.inf, F32) for _ in heads))

    band_off = pl.multiple_of(jnp.maximum(qb_idx - 1, 0) * DSA_QB, DSA_QB)
    ws = pl.ds(band_off, 2 * DSA_QB)
    ms = list(ms)
    for h in heads:
        band_h = band_scr[h]
        band_first = jnp.concatenate([band_h[DSA_QB:, :], jnp.zeros((DSA_QB, DSA_QB), F32)], axis=0)
        win = lg_scr[h, ws, :] + jnp.where(qb_idx == 0, band_first, band_h)
        lg_scr[h, ws, :] = win
        ms[h] = jnp.maximum(ms[h], jnp.max(win, axis=0, keepdims=True))

    def pv_chunk(c, accs):
        ks = pl.ds(pl.multiple_of(c * DSA_KC, DSA_KC), DSA_KC)
        vt_c = vt_scr[:, ks]
        return tuple(accs[h] + _dot(vt_c, jnp.exp2(lg_scr[h, ks, :] - ms[h]).astype(BF16))
                     for h in heads)
    accs = lax.fori_loop(0, n_kc, pv_chunk,
                         tuple(jnp.zeros((2 * HEAD_DIM, DSA_QB), F32) for _ in heads))
    for h in heads:
        o_ref[:, hsl[h]] = (accs[h][:HEAD_DIM, :] / accs[h][HEAD_DIM:, :]).T.astype(BF16)


def dsa_attention(proj, q, q_idx, prep_tok, rel_bias, batch, seq):
    n = proj.shape[0]
    nq = seq // DSA_QB
    topk = min(TOPK_MAX, seq // 4)
    wi = IDX_HEADS * IDX_DIM
    rowblk = lambda w, cb: pl.BlockSpec((DSA_QB, w), lambda b, i: (b * nq + i, cb))
    seqblk = lambda off: pl.BlockSpec((seq, LANE), lambda b, i: (b, off // LANE))
    return pl.pallas_call(
        functools.partial(_dsa_kernel, seq=seq, topk=topk),
        grid=(batch, nq),
        in_specs=[pl.BlockSpec(memory_space=pltpu.SMEM),
                  rowblk(WIDTH, 0), rowblk(wi, 0), rowblk(LANE, 0),
                  seqblk(C_DK), seqblk(C_DV), seqblk(C_KA), seqblk(C_KB)],
        out_specs=pl.BlockSpec((DSA_QB, WIDTH), lambda b, i: (b * nq + i, 0)),
        out_shape=jax.ShapeDtypeStruct((n, WIDTH), BF16),
        scratch_shapes=[pltpu.VMEM((seq, DSA_QB), I32),
                        pltpu.VMEM((N_HEADS, seq, DSA_QB), F32),
                        pltpu.VMEM((N_HEADS, 2 * DSA_QB, DSA_QB), F32),
                        pltpu.VMEM((2 * HEAD_DIM, seq), BF16)],
        compiler_params=_cparams(("arbitrary", "arbitrary")),
        name="dsa_attention",
    )(rel_bias, q, q_idx, prep_tok, proj, proj, proj, proj)


GDN_T = 256
GDN_GROUP = 2
GDN_HALO = 8


def _gdn_kernel(q_ref, k_ref, v_ref, z_ref, cw_ref, ng_ref, tok_ref, tr_ref, o_ref,
                xq_scr, xk_scr, xv_scr, state_scr):
    first = pl.program_id(1) == 0

    @pl.when(first)
    def _():
        state_scr[...] = jnp.zeros_like(state_scr)
        for scr in (xq_scr, xk_scr, xv_scr):
            scr[0:GDN_HALO, :] = jnp.zeros((GDN_HALO, WIDTH), F32)

    def conv(x_ref, scr, w_off):
        scr[GDN_HALO:, :] = x_ref[...]
        y = jnp.zeros((GDN_T, WIDTH), F32)
        for i in range(GDN_CONV):
            st = GDN_HALO - (GDN_CONV - 1) + i
            y = y + scr[st:st + GDN_T, :] * cw_ref[i:i + 1, w_off:w_off + WIDTH]
        scr[0:GDN_HALO, :] = scr[GDN_T:GDN_T + GDN_HALO, :]
        return _silu(y)

    qc = conv(q_ref, xq_scr, 0)
    kc = conv(k_ref, xk_scr, WIDTH)
    vc = conv(v_ref, xv_scr, 2 * WIDTH)
    tok = tok_ref[...]
    c = GDN_CHUNK
    heads = range(N_HEADS)
    hsl = [slice(h * HEAD_DIM, (h + 1) * HEAD_DIM) for h in heads]

    def l2norm_heads(x, scale):
        return jnp.concatenate(
            [x[:, s] * (lax.rsqrt(jnp.sum(x[:, s] * x[:, s], axis=-1, keepdims=True) + EPS) * scale)
             for s in hsl], axis=1)

    qf = l2norm_heads(qc, HEAD_DIM ** -0.5)
    kf = l2norm_heads(kc, 1.0)

    grp = GDN_GROUP
    gw = grp * HEAD_DIM
    nb = grp * c
    ri = lax.broadcasted_iota(I32, (nb, nb), 0)
    ci = lax.broadcasted_iota(I32, (nb, nb), 1)
    tril = ((ri // c) == (ci // c)) & (ri >= ci)
    eye = (ri == ci).astype(F32)
    pair_masks = []
    for lg in range(c.bit_length() - 1):
        pair_masks.append(((ri >> (lg + 1)) == (ci >> (lg + 1)))
                          & (((ri >> lg) & 1) == 1) & (((ci >> lg) & 1) == 0))
    lane_head = lax.broadcasted_iota(I32, (c, gw), 1) // HEAD_DIM
    row_head = lax.broadcasted_iota(I32, (nb, HEAD_DIM), 0) // c

    def spread(x):
        return jnp.concatenate([jnp.where(lane_head == u, x, 0.0) for u in range(grp)], axis=0)

    def stack(x):
        return jnp.concatenate([x[:, hsl[u]] for u in range(grp)], axis=0)

    def spread_lanes(x):
        return jnp.concatenate([jnp.where(row_head == u, x, 0.0) for u in range(grp)], axis=1)

    def split(x):
        hi = x.astype(BF16)
        return hi, (x - hi.astype(F32)).astype(BF16)

    def dot_split(a, b):
        return _dot(a[0], b[0]) + (_dot(a[0], b[1]) + _dot(a[1], b[0]))

    items = [(j, gi) for j in range(GDN_T // c) for gi in range(N_HEADS // grp)]
    pre = []
    for j, gi in items:
        rs = slice(j * c, (j + 1) * c)
        last = slice((j + 1) * c - 1, (j + 1) * c)
        gh = [gi * grp + u for u in range(grp)]
        gsl = slice(gi * gw, (gi + 1) * gw)
        qj, kj, vj = qf[rs, gsl], kf[rs, gsl], vc[rs, gsl]
        b_col = jnp.concatenate([tok[rs, L_GB + h:L_GB + h + 1] for h in gh], axis=0)
        g_col = jnp.concatenate([tok[rs, L_GA + h:L_GA + h + 1] for h in gh], axis=0)
        g_row = jnp.concatenate([tr_ref[0, L_GA + h:L_GA + h + 1, rs] for h in gh], axis=1)
        g_last = [tok[last, L_GA + h:L_GA + h + 1] for h in gh]
        g_last_col = jnp.concatenate([jnp.broadcast_to(g, (c, 1)) for g in g_last], axis=0)
        k_sp = spread(kj)
        q_sp = spread(qj)
        k_sp16 = k_sp.astype(BF16)
        decay = jnp.exp(jnp.where(tril, g_col - g_row, -jnp.inf))
        eg = jnp.exp(g_col)
        pre.append(dict(
            gh=gh, gsl=gsl,
            l_mat=b_col * _dot_nt(k_sp16, k_sp16) * decay,
            rhs=jnp.concatenate([stack(vj) * b_col, stack(kj) * (b_col * eg)], axis=1),
            qk=_dot_nt(q_sp.astype(BF16), k_sp16) * decay,
            q_dec=q_sp * eg,
            k_dec=k_sp * jnp.exp(g_last_col - g_col),
            e_last=jnp.concatenate([jnp.broadcast_to(jnp.exp(g), (HEAD_DIM, 1)) for g in g_last], axis=0)))

    t_inv = [eye - jnp.where(pair_masks[0], p["l_mat"], 0.0) for p in pre]
    for pm in pair_masks[1:]:
        t_s = [split(t) for t in t_inv]
        m_t = [dot_split(split(jnp.where(pm, p["l_mat"], 0.0)), ts) for p, ts in zip(pre, t_s)]
        t_inv = [t - dot_split(ts, split(m)) for t, ts, m in zip(t_inv, t_s, m_t)]
    sols = [dot_split(split(t), split(p["rhs"])) for t, p in zip(t_inv, pre)]

    outs = [[] for _ in heads]
    for p, sol in zip(pre, sols):
        u0 = sol[:, :HEAD_DIM]
        kcum = sol[:, HEAD_DIM:]
        st = state_scr[p["gsl"], :]
        stb = st.astype(BF16)
        v_new = u0 - _dot(spread_lanes(kcum).astype(BF16), stb)
        v_new_b = v_new.astype(BF16)
        o_st = _dot(p["q_dec"].astype(BF16), stb) + _dot(p["qk"].astype(BF16), v_new_b)
        state_scr[p["gsl"], :] = st * p["e_last"] + _dot(p["k_dec"].T.astype(BF16), v_new_b)
        for u, h in enumerate(p["gh"]):
            outs[h].append(o_st[u * c:(u + 1) * c, :])

    for h in heads:
        o = jnp.concatenate(outs[h], axis=0)
        ms = jnp.mean(o * o, axis=-1, keepdims=True)
        on = o * lax.rsqrt(ms + EPS) * ng_ref[...]
        o_ref[:, hsl[h]] = (on * _silu(z_ref[:, hsl[h]])).astype(BF16)


def gated_deltanet(proj, prep_tok, prep_tr, conv_w, norm_g, batch, seq):
    n = proj.shape[0]
    t = GDN_T
    nt = seq // t
    col = lambda off: pl.BlockSpec((t, WIDTH), lambda b, i: (b * nt + i, off // WIDTH))
    return pl.pallas_call(
        _gdn_kernel,
        grid=(batch, nt),
        in_specs=[col(C_GQ), col(C_GK), col(C_GV), col(C_GZ),
                  pl.BlockSpec((GDN_CONV, 3 * WIDTH), lambda b, i: (0, 0)),
                  pl.BlockSpec((1, HEAD_DIM), lambda b, i: (0, 0)),
                  pl.BlockSpec((t, LANE), lambda b, i: (b * nt + i, 0)),
                  pl.BlockSpec((1, PREP_ROWS, t), lambda b, i: (b, 0, i))],
        out_specs=pl.BlockSpec((t, WIDTH), lambda b, i: (b * nt + i, 0)),
        out_shape=jax.ShapeDtypeStruct((n, WIDTH), BF16),
        scratch_shapes=[pltpu.VMEM((t + GDN_HALO, WIDTH), F32),
                        pltpu.VMEM((t + GDN_HALO, WIDTH), F32),
                        pltpu.VMEM((t + GDN_HALO, WIDTH), F32),
                        pltpu.VMEM((N_HEADS * HEAD_DIM, HEAD_DIM), F32)],
        compiler_params=_cparams(("arbitrary", "arbitrary")),
        name="gated_deltanet",
    )(proj, proj, proj, proj, conv_w, norm_g.reshape(1, HEAD_DIM), prep_tok, prep_tr)


def _merge_kernel(h_ref, b0_ref, b1_ref, b2_ref, b3_ref, g0_ref, g1_ref, g2_ref, g3_ref,
                  wb_ref, o_ref):
    h = h_ref[...]
    acc = None
    for n, (b_ref, g_ref) in enumerate(zip((b0_ref, b1_ref, b2_ref, b3_ref),
                                           (g0_ref, g1_ref, g2_ref, g3_ref))):
        gate = jax.nn.sigmoid(_dot(h, g_ref[...]))
        term = gate * _dot(b_ref[...], wb_ref[n])
        acc = term if acc is None else acc + term
    o_ref[...] = acc.astype(BF16)


def merge_branches(h, branches, w_gate, w_branch, *, tm=512, tn=512):
    _, n, d = h.shape
    nj = d // tn
    bspec = pl.BlockSpec((tm, WIDTH), lambda j, i: (i, 0))
    gspec = lambda k: pl.BlockSpec((d, tn), lambda j, i: (0, k * nj + j))
    return pl.pallas_call(
        _merge_kernel,
        grid=(nj, n // tm),
        in_specs=[pl.BlockSpec((None, tm, d), lambda j, i: (0, i, 0)),
                  bspec, bspec, bspec, bspec,
                  gspec(0), gspec(1), gspec(2), gspec(3),
                  pl.BlockSpec((N_BRANCH, WIDTH, tn), lambda j, i: (0, 0, j))],
        out_specs=pl.BlockSpec((tm, tn), lambda j, i: (i, j)),
        out_shape=jax.ShapeDtypeStruct((n, d), BF16),
        compiler_params=_cparams(("arbitrary", "arbitrary")),
        name="merge_branches",
    )(h, *branches, w_gate, w_gate, w_gate, w_gate, w_branch)


def _resid_mm_kernel(a_ref, w_ref, x_ref, o_ref):
    o_ref[...] = x_ref[...] + _dot(a_ref[...], w_ref[...])


def resid_matmul(a, w, x, *, tm=512, tn=1024, name="resid_matmul"):
    n, k = a.shape
    d = w.shape[1]
    return pl.pallas_call(
        _resid_mm_kernel,
        grid=(d // tn, n // tm),
        in_specs=[pl.BlockSpec((tm, k), lambda j, i: (i, 0)),
                  pl.BlockSpec((k, tn), lambda j, i: (0, j)),
                  pl.BlockSpec((tm, tn), lambda j, i: (i, j))],
        out_specs=pl.BlockSpec((tm, tn), lambda j, i: (i, j)),
        out_shape=jax.ShapeDtypeStruct((n, d), F32),
        compiler_params=_cparams(("arbitrary", "arbitrary")),
        name=name,
    )(a, w, x)


FFN_HALO = 8
FFN_SUB = 512


def _ffn1_kernel(x_ref, g_ref, wg_ref, wu_ref, cw_ref, cb_ref, o_ref, h_scr, gt_scr, halo_scr,
                 *, tm, tiles_per_seq):
    i = pl.program_id(1)

    @pl.when(i == 0)
    def _():
        halo_scr[...] = jnp.zeros_like(halo_scr)

    def body(r, carry):
        rows = pl.ds(pl.multiple_of(r * NORM_ROWS, NORM_ROWS), NORM_ROWS)
        h_scr[rows, :] = _rmsnorm_rows(x_ref, g_ref, rows).astype(BF16)
        return carry
    lax.fori_loop(0, tm // NORM_ROWS, body, 0)

    h = h_scr[...]
    seq_start = (i % tiles_per_seq) == 0
    tn = o_ref.shape[1]
    for off in range(0, tn, FFN_SUB):
        cs = slice(off, min(off + FFN_SUB, tn))
        g = _dot(h, wg_ref[:, cs])
        gt_scr[FFN_HALO:, cs] = g
        gt_scr[0:FFN_HALO, cs] = jnp.where(seq_start, 0.0, halo_scr[:, cs])
        halo_scr[:, cs] = g[tm - FFN_HALO:, :]
        y = cb_ref[:, cs] + g * cw_ref[FFN_CONV - 1:FFN_CONV, cs]
        for t in range(FFN_CONV - 1):
            st = FFN_HALO - (FFN_CONV - 1) + t
            y = y + gt_scr[st:st + tm, cs] * cw_ref[t:t + 1, cs]
        o_ref[:, cs] = (_silu(y) * _dot(h, wu_ref[:, cs])).astype(BF16)


def conv_ffn_up(x, gain, w_gate, w_up, conv_w, conv_b, seq, *, tm=512, col_parts=2):
    n, d = x.shape
    f = w_gate.shape[1]
    tn = f // col_parts
    wspec = pl.BlockSpec((d, tn), lambda j, i: (0, j), pipeline_mode=pl.Buffered(1))
    return pl.pallas_call(
        functools.partial(_ffn1_kernel, tm=tm, tiles_per_seq=seq // tm),
        grid=(col_parts, n // tm),
        in_specs=[pl.BlockSpec((tm, d), lambda j, i: (i, 0)),
                  pl.BlockSpec((1, d), lambda j, i: (0, 0)),
                  wspec, wspec,
                  pl.BlockSpec((FFN_CONV, tn), lambda j, i: (0, j)),
                  pl.BlockSpec((1, tn), lambda j, i: (0, j))],
        out_specs=pl.BlockSpec((tm, tn), lambda j, i: (i, j)),
        out_shape=jax.ShapeDtypeStruct((n, f), BF16),
        scratch_shapes=[pltpu.VMEM((tm, d), BF16),
                        pltpu.VMEM((tm + FFN_HALO, tn), F32),
                        pltpu.VMEM((FFN_HALO, tn), F32)],
        compiler_params=_cparams(("arbitrary", "arbitrary")),
        name="conv_ffn_up",
    )(x, gain.reshape(1, d), w_gate, w_up, conv_w, conv_b.reshape(1, f))


IN_SIZES = (WIDTH, WIDTH, WIDTH, WIDTH,
            DSA_Q_RANK, HEAD_DIM, HEAD_DIM, IDX_DIM, IDX_HEADS,
            WIDTH, WIDTH, WIDTH, N_HEADS,
            WIDTH, WIDTH, WIDTH, WIDTH, N_HEADS, N_HEADS)
IN_NAMES = ("r_q", "r_k", "r_v", "r_g", "d_cq", "d_k", "d_v", "i_k", "i_w",
            "f_q", "f_k", "f_v", "f_f", "g_q", "g_k", "g_v", "g_z", "g_b", "g_a")
IN_PLAN32 = (("r_q", C_RQ), ("r_k", C_RK), ("r_v", C_RV), ("r_g", C_RG),
             ("g_q", C_GQ), ("g_k", C_GK), ("g_v", C_GV), ("g_z", C_GZ), ("d_cq", C_DCQ),
             ("i_w", C_SM + L_IW), ("f_f", C_SM + L_FF), ("g_b", C_SM + L_GB), ("g_a", C_SM + L_GA))
IN_PLAN16 = (("f_q", C_FQ), ("f_k", C_FK), ("f_v", C_FV), ("d_k", C_DK), ("d_v", C_DV),
             ("i_k", C_KA), ("i_k", C_KB + IDX_DIM))


def _prep_w_in_kernel(w_ref, m32_ref, m16_ref, g_ref):
    src = {}
    off = 0
    for name, size in zip(IN_NAMES, IN_SIZES):
        src[name] = (off, size)
        off += size
    for m_ref, plan in ((m32_ref, IN_PLAN32), (m16_ref, IN_PLAN16)):
        m_ref[...] = jnp.zeros_like(m_ref)
        for name, dst in plan:
            so, w = src[name]
            m_ref[:, dst:dst + w] = w_ref[:, so:so + w].astype(BF16)
    g_ref[...] = w_ref[:, off:off + g_ref.shape[1]].astype(BF16)


def prep_w_in(w_in, layer, *, tr=128):
    _, d, c = w_in.shape
    widths = (C_TOT32, C_TOT16, N_BRANCH * d)
    return pl.pallas_call(
        _prep_w_in_kernel,
        grid=(d // tr,),
        in_specs=[pl.BlockSpec((None, tr, c), lambda i: (layer, i, 0))],
        out_specs=[pl.BlockSpec((tr, w), lambda i: (i, 0)) for w in widths],
        out_shape=[jax.ShapeDtypeStruct((d, w), BF16) for w in widths],
        compiler_params=_cparams(("arbitrary",)),
        name="prep_w_in",
    )(w_in)


def _cast_kernel(w_ref, o_ref):
    o_ref[...] = w_ref[...].astype(BF16)


def cast_layer(w, layer, *, tr=256):
    _, r, c = w.shape
    if r % tr:
        tr = r
    return pl.pallas_call(
        _cast_kernel,
        grid=(r // tr,),
        in_specs=[pl.BlockSpec((None, tr, c), lambda i: (layer, i, 0))],
        out_specs=pl.BlockSpec((tr, c), lambda i: (i, 0)),
        out_shape=jax.ShapeDtypeStruct((r, c), BF16),
        compiler_params=_cparams(("arbitrary",)),
        name="cast_bf16",
    )(w)


def cast_branch(w_branch, layer):
    _, nbr, r, c = w_branch.shape
    return pl.pallas_call(
        _cast_kernel,
        grid=(nbr,),
        in_specs=[pl.BlockSpec((None, None, r, c), lambda i: (layer, i, 0, 0))],
        out_specs=pl.BlockSpec((None, r, c), lambda i: (i, 0, 0)),
        out_shape=jax.ShapeDtypeStruct((nbr, r, c), BF16),
        compiler_params=_cparams(("arbitrary",)),
        name="cast_branch",
    )(w_branch)


def kernel(x, norm_mix, w_in, dsa_cq_norm, dsa_w_uq, dsa_w_qidx, fox_f_bias, gdn_conv, gdn_a_log,
           gdn_dt_bias, gdn_norm, w_branch, w_out, rel_bias, norm_ffn, ffn_w_gate, ffn_w_up,
           ffn_conv, ffn_conv_b, ffn_w_down, final_norm):
    batch, seq, d = x.shape
    depth = w_in.shape[0]
    xf = x.reshape(batch * seq, d)
    ret_tables = _retention_tables(seq)
    for l in range(depth):
        w_main32, w_main16, w_gate = prep_w_in(w_in, l)
        proj32, h = norm_proj(xf, norm_mix[l], w_main32, out_dtype=F32, emit_h=True, col_parts=2)
        proj16, = norm_proj(xf, norm_mix[l], w_main16, out_dtype=BF16, emit_h=False, col_parts=1)
        par = jnp.zeros((SUBLANE, LANE), F32)
        par = par.at[0, L_FF:L_FF + N_HEADS].set(fox_f_bias[l])
        par = par.at[0, L_GA:L_GA + N_HEADS].set(gdn_dt_bias[l])
        par = par.at[1, L_GA:L_GA + N_HEADS].set(gdn_a_log[l])
        prep_tok, prep_tr = prep_small(proj32, par, batch, seq)
        o_ret = retention(proj32, ret_tables, batch, seq)
        q_dsa, q_idx = dsa_proj(proj32, dsa_cq_norm[l], cast_layer(dsa_w_uq, l), cast_layer(dsa_w_qidx, l))
        o_dsa = dsa_attention(proj16, q_dsa, q_idx, prep_tok, rel_bias, batch, seq)
        o_fox = fox_attention(proj16, prep_tr, batch, seq)
        o_gdn = gated_deltanet(proj32, prep_tok, prep_tr, gdn_conv[l], gdn_norm[l], batch, seq)
        merged = merge_branches(h, (o_ret, o_dsa, o_fox, o_gdn), w_gate, cast_branch(w_branch, l))
        xf = resid_matmul(merged, cast_layer(w_out, l), xf, tn=d, name="out_proj")
        act = conv_ffn_up(xf, norm_ffn[l], cast_layer(ffn_w_gate, l), cast_layer(ffn_w_up, l),
                          ffn_conv[l], ffn_conv_b[l], seq)
        xf = resid_matmul(act, cast_layer(ffn_w_down, l), xf, name="ffn_down")
    return rmsnorm(xf, final_norm).reshape(batch, seq, d)
```

```python
import functools
import math

import jax
import jax.numpy as jnp
from jax import lax
from jax.experimental import pallas as pl
from jax.experimental.pallas import tpu as pltpu

F32 = jnp.float32
BF16 = jnp.bfloat16
I32 = jnp.int32

HEAD_DIM = 128
N_HEADS = 4
WIDTH = N_HEADS * HEAD_DIM
N_BRANCH = 4
RET_CHUNK = 128
ROPE_BASE = 10000.0
DSA_Q_RANK = 384
IDX_HEADS = 16
IDX_DIM = 64
TOPK_MAX = 256
GDN_CONV = 4
GDN_CHUNK = 64
REL_BUCKETS = 32
REL_MAX_DIST = 128
FFN_CONV = 3
EPS = 1e-6

LANE = 128
SUBLANE = 8
VMEM_LIMIT = 56 * 1024 * 1024

C_RQ, C_RK, C_RV, C_RG = 0, 512, 1024, 1536
C_FQ, C_DK, C_DCQ = 2048, 2560, 2688
C_FK, C_FV = 3072, 3584
C_GQ, C_GK, C_GV, C_GZ = 4096, 4608, 5120, 5632
C_DV, C_KA, C_KB, C_SM = 6144, 6272, 6400, 6528
C_TOT = 6656
L_IW, L_FF, L_GB, L_GA = 0, 16, 20, 24

LOG2E = 1.4426950408889634
INT_MIN = -(2 ** 31)
INT_MAX = 2 ** 31 - 1
HIGHEST = lax.Precision.HIGHEST


def _cparams(sem, vmem=VMEM_LIMIT):
    return pltpu.CompilerParams(dimension_semantics=sem, vmem_limit_bytes=vmem)


def _dot(a, b):
    return jnp.dot(a, b, preferred_element_type=F32)


def _dot_nt(a, b):
    return lax.dot_general(a, b, (((1,), (1,)), ((), ())), preferred_element_type=F32)


def _silu(x):
    return x * jax.nn.sigmoid(x)


NORM_ROWS = 128


def _rmsnorm_rows(x_ref, g_ref, rows):
    x = x_ref[rows, :]
    ms = jnp.mean(x * x, axis=-1, keepdims=True)
    return x * lax.rsqrt(ms + EPS) * g_ref[...]


PROJ_SUB = 512


def _norm_proj_kernel(x_ref, g_ref, w_ref, o_ref, h_ref, h_scr, *, tm):
    def body(r, carry):
        rows = pl.ds(pl.multiple_of(r * NORM_ROWS, NORM_ROWS), NORM_ROWS)
        hb = _rmsnorm_rows(x_ref, g_ref, rows).astype(BF16)
        h_scr[rows, :] = hb
        h_ref[rows, :] = hb
        return carry
    lax.fori_loop(0, tm // NORM_ROWS, body, 0)

    h = h_scr[...]
    tn = o_ref.shape[1]
    for off in range(0, tn, PROJ_SUB):
        cs = slice(off, min(off + PROJ_SUB, tn))
        o_ref[:, cs] = _dot(h, w_ref[:, cs])


def norm_proj(x, gain, w, *, tm=512, col_parts=2):
    n, d = x.shape
    c = w.shape[1]
    tn = c // col_parts
    return pl.pallas_call(
        functools.partial(_norm_proj_kernel, tm=tm),
        grid=(col_parts, n // tm),
        in_specs=[pl.BlockSpec((tm, d), lambda j, i: (i, 0)),
                  pl.BlockSpec((1, d), lambda j, i: (0, 0)),
                  pl.BlockSpec((d, tn), lambda j, i: (0, j), pipeline_mode=pl.Buffered(1))],
        out_specs=[pl.BlockSpec((tm, tn), lambda j, i: (i, j)),
                   pl.BlockSpec((None, tm, d), lambda j, i: (j, i, 0))],
        out_shape=[jax.ShapeDtypeStruct((n, c), F32),
                   jax.ShapeDtypeStruct((col_parts, n, d), BF16)],
        scratch_shapes=[pltpu.VMEM((tm, d), BF16)],
        compiler_params=_cparams(("arbitrary", "arbitrary")),
        name="norm_proj",
    )(x, gain.reshape(1, d), w)


def _rmsnorm_kernel(x_ref, g_ref, o_ref, *, tm):
    def body(r, carry):
        rows = pl.ds(pl.multiple_of(r * NORM_ROWS, NORM_ROWS), NORM_ROWS)
        o_ref[rows, :] = _rmsnorm_rows(x_ref, g_ref, rows)
        return carry
    lax.fori_loop(0, tm // NORM_ROWS, body, 0)


def rmsnorm(x, gain, *, tm=512):
    n, d = x.shape
    return pl.pallas_call(
        functools.partial(_rmsnorm_kernel, tm=tm),
        grid=(n // tm,),
        in_specs=[pl.BlockSpec((tm, d), lambda i: (i, 0)),
                  pl.BlockSpec((1, d), lambda i: (0, 0))],
        out_specs=pl.BlockSpec((tm, d), lambda i: (i, 0)),
        out_shape=jax.ShapeDtypeStruct((n, d), F32),
        compiler_params=_cparams(("arbitrary",)),
        name="final_rmsnorm",
    )(x, gain.reshape(1, d))


def _prep_kernel(s_ref, par_ref, tok_ref, tr_ref, carry_scr):
    @pl.when(pl.program_id(1) == 0)
    def _():
        carry_scr[...] = jnp.zeros_like(carry_scr)

    s = s_ref[...]
    lane = lax.broadcasted_iota(I32, (LANE, LANE), 1)
    row = lax.broadcasted_iota(I32, (LANE, LANE), 0)
    z = s + par_ref[0:1, :]
    soft = jnp.maximum(z, 0.0) + jnp.log1p(jnp.exp(-jnp.abs(z)))
    log_sig = z - soft
    sig = jax.nn.sigmoid(z)
    g_val = -jnp.exp(par_ref[1:2, :]) * soft
    is_f = (lane[0:1] >= L_FF) & (lane[0:1] < L_FF + N_HEADS)
    is_b = (lane[0:1] >= L_GB) & (lane[0:1] < L_GB + N_HEADS)
    is_a = (lane[0:1] >= L_GA) & (lane[0:1] < L_GA + N_HEADS)
    pre = jnp.where(is_f, log_sig, jnp.where(is_a, g_val, 0.0))
    tri = (row >= lane).astype(F32)
    tri_blk = ((row >= lane) & ((row // GDN_CHUNK) == (lane // GDN_CHUNK))).astype(F32)
    subs = [slice(u * LANE, (u + 1) * LANE) for u in range(PREP_T // LANE)]
    cum_full = [jnp.dot(tri, pre[u], precision=HIGHEST, preferred_element_type=F32) for u in subs]
    cum_blk = [jnp.dot(tri_blk, pre[u], precision=HIGHEST, preferred_element_type=F32) for u in subs]
    scale_iw = IDX_HEADS ** -0.5 * IDX_DIM ** -0.5
    carry = carry_scr[0:1, :]
    for u, cf, cb in zip(subs, cum_full, cum_blk):
        c_fox = cf + carry
        carry = c_fox[LANE - 1:LANE, :]
        out = jnp.where(is_f, c_fox,
                        jnp.where(is_a, cb,
                                  jnp.where(is_b, sig[u],
                                            jnp.where(lane[0:1] < IDX_HEADS, s[u] * scale_iw, 0.0))))
        tok_ref[u, :] = out
        tr_ref[0, :, u] = out.T[0:PREP_ROWS, :]
    carry_scr[0:1, :] = carry


PREP_T = 512
PREP_ROWS = 32


def prep_small(proj, par, batch, seq):
    n = proj.shape[0]
    nc = seq // PREP_T
    return pl.pallas_call(
        _prep_kernel,
        grid=(batch, nc),
        in_specs=[pl.BlockSpec((PREP_T, LANE), lambda b, c: (b * nc + c, C_SM // LANE)),
                  pl.BlockSpec((SUBLANE, LANE), lambda b, c: (0, 0))],
        out_specs=[pl.BlockSpec((PREP_T, LANE), lambda b, c: (b * nc + c, 0)),
                   pl.BlockSpec((1, PREP_ROWS, PREP_T), lambda b, c: (b, 0, c))],
        out_shape=[jax.ShapeDtypeStruct((n, LANE), F32),
                   jax.ShapeDtypeStruct((batch, PREP_ROWS, seq), F32)],
        scratch_shapes=[pltpu.VMEM((SUBLANE, LANE), F32)],
        compiler_params=_cparams(("arbitrary", "arbitrary")),
        name="prep_small",
    )(proj, par)


def _ret_gamma():
    return [math.log1p(-(2.0 ** (-5.0 - h))) for h in range(N_HEADS)]


def _retention_kernel(x_ref, cos_ref, sin_ref, dec_ref, zeta_ref, xi_ref, o_ref, state_scr):
    @pl.when(pl.program_id(1) == 0)
    def _():
        state_scr[...] = jnp.zeros_like(state_scr)

    cos_t = cos_ref[...]
    sin_t = sin_ref[...]
    log_gamma = _ret_gamma()
    heads = range(N_HEADS)
    hsl = [slice(h * HEAD_DIM, (h + 1) * HEAD_DIM) for h in heads]

    def part(col):
        return [slice(col - C_RQ + s.start, col - C_RQ + s.stop) for s in hsl]

    def rope(x):
        return x * cos_t + pltpu.roll(x, HEAD_DIM // 2, 1) * sin_t

    qb = [rope(x_ref[:, s]).astype(BF16) for s in part(C_RQ)]
    kr = [rope(x_ref[:, s]) * (HEAD_DIM ** -0.5) for s in part(C_RK)]
    kb = [x.astype(BF16) for x in kr]
    vb = [x_ref[:, s].astype(BF16) for s in part(C_RV)]
    gsl = part(C_RG)
    st = [state_scr[h] for h in heads]
    inner = [(_dot_nt(qb[h], kb[h]) * dec_ref[h]).astype(BF16) for h in heads]
    cross = [_dot(qb[h], st[h].astype(BF16)) * xi_ref[h] for h in heads]
    kv = [_dot((kr[h] * zeta_ref[h]).T.astype(BF16), vb[h]) for h in heads]
    o = [_dot(inner[h], vb[h]) + cross[h] for h in heads]
    for h in heads:
        state_scr[h] = st[h] * math.exp(log_gamma[h] * RET_CHUNK) + kv[h]
        mu = jnp.mean(o[h], axis=-1, keepdims=True)
        oc = o[h] - mu
        var = jnp.mean(oc * oc, axis=-1, keepdims=True)
        o_ref[:, hsl[h]] = (_silu(x_ref[:, gsl[h]]) * (oc * lax.rsqrt(var + EPS))).astype(BF16)


def _retention_tables(seq):
    half = HEAD_DIM // 2
    inv = 1.0 / (ROPE_BASE ** (jnp.arange(half, dtype=F32) / half))
    ang = jnp.arange(seq).astype(F32)[:, None] * inv[None, :]
    cos, sin = jnp.cos(ang), jnp.sin(ang)
    cos_t = jnp.concatenate([cos, cos], axis=-1)
    sin_t = jnp.concatenate([-sin, sin], axis=-1)
    c = RET_CHUNK
    log_gamma = jnp.log1p(-jnp.exp2(-5.0 - jnp.arange(N_HEADS, dtype=F32)))
    n = jnp.arange(c, dtype=F32)
    diff = n[:, None] - n[None, :]
    decay = jnp.where(diff >= 0, jnp.exp(log_gamma[:, None, None] * jnp.maximum(diff, 0.0)), 0.0)
    zeta = jnp.exp(log_gamma[:, None] * (c - 1 - n)[None, :])
    xi = jnp.exp(log_gamma[:, None] * (n + 1)[None, :])
    ones = jnp.ones((1, 1, HEAD_DIM), F32)
    return cos_t, sin_t, decay, zeta[:, :, None] * ones, xi[:, :, None] * ones


def retention(proj, tables, batch, seq):
    n = proj.shape[0]
    c = RET_CHUNK
    nc = seq // c
    cos_t, sin_t, decay, zeta, xi = tables
    full3 = pl.BlockSpec((N_HEADS, c, HEAD_DIM), lambda b, i: (0, 0, 0))
    return pl.pallas_call(
        _retention_kernel,
        grid=(batch, nc),
        in_specs=[pl.BlockSpec((c, 4 * WIDTH), lambda b, i: (b * nc + i, C_RQ // (4 * WIDTH))),
                  pl.BlockSpec((c, HEAD_DIM), lambda b, i: (i, 0)),
                  pl.BlockSpec((c, HEAD_DIM), lambda b, i: (i, 0)),
                  full3, full3, full3],
        out_specs=pl.BlockSpec((c, WIDTH), lambda b, i: (b * nc + i, 0)),
        out_shape=jax.ShapeDtypeStruct((n, WIDTH), BF16),
        scratch_shapes=[pltpu.VMEM((N_HEADS, HEAD_DIM, HEAD_DIM), F32)],
        compiler_params=_cparams(("arbitrary", "arbitrary")),
        name="retention",
    )(proj, cos_t, sin_t, decay, zeta, xi)


def _fox_kernel(qi_ref, ki_ref, q_ref, kv_ref, ctr_ref, o_ref, m_scr, acc_scr, *, t):
    qi = qi_ref[pl.program_id(1)]
    ki = ki_ref[pl.program_id(1)]

    @pl.when(ki == 0)
    def _():
        m_scr[...] = jnp.full_like(m_scr, -jnp.inf)
        acc_scr[...] = jnp.zeros_like(acc_scr)

    def step(masked):
        if masked:
            row = lax.broadcasted_iota(I32, (t, t), 0)
            colm = lax.broadcasted_iota(I32, (t, t), 1)
            keep = row >= colm
        ones = jnp.ones((t, HEAD_DIM), BF16)
        for h in range(N_HEADS):
            sl = slice(h * HEAD_DIM, (h + 1) * HEAD_DIM)
            vsl = slice(WIDTH + h * HEAD_DIM, WIDTH + (h + 1) * HEAD_DIM)
            qb = q_ref[:, sl].astype(BF16)
            kb = kv_ref[:, sl].astype(BF16)
            c_k = ctr_ref[0, L_FF + h:L_FF + h + 1, :] * LOG2E
            s = _dot_nt(qb, kb) * (HEAD_DIM ** -0.5 * LOG2E) - c_k
            if masked:
                s = jnp.where(keep, s, -jnp.inf)
            m_old = m_scr[h]
            m_new = jnp.maximum(m_old, jnp.max(s, axis=-1, keepdims=True))
            alpha = jnp.exp2(m_old - m_new)
            p = jnp.exp2(s - m_new)
            v_aug = jnp.concatenate([kv_ref[:, vsl].astype(BF16), ones], axis=1)
            acc_scr[h] = alpha * acc_scr[h] + _dot(p.astype(BF16), v_aug)
            m_scr[h] = m_new

    @pl.when(ki < qi)
    def _():
        step(False)

    @pl.when(ki == qi)
    def _():
        step(True)
        for h in range(N_HEADS):
            sl = slice(h * HEAD_DIM, (h + 1) * HEAD_DIM)
            acc = acc_scr[h]
            o_ref[:, sl] = (acc[:, :HEAD_DIM] / acc[:, HEAD_DIM:]).astype(BF16)


def fox_attention(proj, prep_tr, batch, seq, *, t=512):
    n = proj.shape[0]
    nt = seq // t
    pairs = [(qi, ki) for qi in range(nt) for ki in range(qi + 1)]
    qi_arr = jnp.asarray([p[0] for p in pairs], I32)
    ki_arr = jnp.asarray([p[1] for p in pairs], I32)
    qspec = pl.BlockSpec((t, WIDTH), lambda b, s, qi, ki: (b * nt + qi[s], C_FQ // WIDTH))
    kvspec = pl.BlockSpec((t, 2 * WIDTH), lambda b, s, qi, ki: (b * nt + ki[s], C_FK // (2 * WIDTH)))
    return pl.pallas_call(
        functools.partial(_fox_kernel, t=t),
        grid_spec=pltpu.PrefetchScalarGridSpec(
            num_scalar_prefetch=2,
            grid=(batch, len(pairs)),
            in_specs=[qspec, kvspec,
                      pl.BlockSpec((1, PREP_ROWS, t), lambda b, s, qi, ki: (b, 0, ki[s]))],
            out_specs=pl.BlockSpec((t, WIDTH), lambda b, s, qi, ki: (b * nt + qi[s], 0)),
            scratch_shapes=[pltpu.VMEM((N_HEADS, t, 1), F32),
                            pltpu.VMEM((N_HEADS, t, 2 * HEAD_DIM), F32)]),
        out_shape=jax.ShapeDtypeStruct((n, WIDTH), BF16),
        compiler_params=_cparams(("arbitrary", "arbitrary")),
        name="fox_attention",
    )(qi_arr, ki_arr, proj, proj, prep_tr)


def _dsa_proj_kernel(cq_ref, g_ref, wq_ref, wi_ref, q_ref, qi_ref):
    x = cq_ref[...]
    ms = jnp.mean(x * x, axis=-1, keepdims=True)
    cb = (x * lax.rsqrt(ms + EPS) * g_ref[...]).astype(BF16)
    q_ref[...] = _dot(cb, wq_ref[...]).astype(BF16)
    qi_ref[...] = _dot(cb, wi_ref[...]).astype(BF16)


def dsa_proj(proj, cq_norm, w_uq, w_qidx, *, tm=512):
    n = proj.shape[0]
    r = DSA_Q_RANK
    wi = IDX_HEADS * IDX_DIM
    return pl.pallas_call(
        _dsa_proj_kernel,
        grid=(n // tm,),
        in_specs=[pl.BlockSpec((tm, r), lambda i: (i, C_DCQ // r)),
                  pl.BlockSpec((1, r), lambda i: (0, 0)),
                  pl.BlockSpec((r, WIDTH), lambda i: (0, 0)),
                  pl.BlockSpec((r, wi), lambda i: (0, 0))],
        out_specs=[pl.BlockSpec((tm, WIDTH), lambda i: (i, 0)),
                   pl.BlockSpec((tm, wi), lambda i: (i, 0))],
        out_shape=[jax.ShapeDtypeStruct((n, WIDTH), BF16),
                   jax.ShapeDtypeStruct((n, wi), BF16)],
        compiler_params=_cparams(("arbitrary",)),
        name="dsa_proj",
    )(proj, cq_norm.reshape(1, r), w_uq, w_qidx)


DSA_QB = 256
DSA_KC = 512
DSA_SCORE_MID_STEPS = 20
DSA_FEW_KEYS = 4
DSA_HALVE_FIXED = 12
DSA_WALK_FIXED = 3


def _score_to_key(s):
    b = pltpu.bitcast(s, I32)
    return b ^ ((b >> 31) & INT_MAX)


def _key_to_score(k):
    return pltpu.bitcast(k ^ ((k >> 31) & INT_MAX), F32)


def _t5_bucket(rel):
    max_exact = REL_BUCKETS // 2
    relf = jnp.maximum(rel, max_exact).astype(F32)
    large = max_exact + (jnp.log(relf / max_exact) / math.log(REL_MAX_DIST / max_exact)
                         * (REL_BUCKETS - max_exact)).astype(I32)
    large = jnp.minimum(large, REL_BUCKETS - 1)
    return jnp.where(rel < max_exact, rel, large)


def _dsa_kernel(rb_ref, q_ref, qi_ref, tok_ref, k_ref, v_ref, ka_ref, kb_ref, o_ref,
                key_scr, lg_scr, band_scr, kb16_scr, vt_scr, ka16_scr, kb16i_scr, *, seq, topk):
    qb_idx = pl.program_id(1)
    t0 = qb_idx * DSA_QB
    n_kc = (t0 + DSA_QB - 1) // DSA_KC + 1
    row_vec = (1, DSA_QB)

    @pl.when(qb_idx == 0)
    def _():
        kb16_scr[...] = k_ref[...].astype(BF16)
        ka16_scr[...] = ka_ref[...].astype(BF16)
        kb16i_scr[...] = kb_ref[...].astype(BF16)
        for c in range(seq // DSA_KC):
            cs = slice(c * DSA_KC, (c + 1) * DSA_KC)
            vt_scr[0:HEAD_DIM, cs] = v_ref[cs, :].T.astype(BF16)
        vt_scr[HEAD_DIM:, :] = jnp.ones((HEAD_DIM, seq), BF16)

    @pl.when((pl.program_id(0) == 0) & (qb_idx == 0))
    def _():
        j_ = lax.broadcasted_iota(I32, (2 * DSA_QB, DSA_QB), 0)
        i_ = lax.broadcasted_iota(I32, (2 * DSA_QB, DSA_QB), 1)
        rel = i_ + DSA_QB - j_
        bucket = _t5_bucket(rel)
        for h in range(N_HEADS):
            far = rb_ref[REL_BUCKETS - 1, h]
            band = jnp.zeros((2 * DSA_QB, DSA_QB), F32)
            for bk in range(REL_BUCKETS - 1):
                band = jnp.where(bucket == bk, (rb_ref[bk, h] - far) * LOG2E, band)
            band_scr[h] = jnp.where(rel >= 0, band, 0.0)

    w_t = tok_ref[...].T
    key_s = lax.broadcasted_iota(I32, (DSA_KC, DSA_QB), 0)
    row_t = t0 + lax.broadcasted_iota(I32, (DSA_KC, DSA_QB), 1)

    def score_chunk(c, carry):
        kmax, kmin = carry
        ks = pl.ds(pl.multiple_of(c * DSA_KC, DSA_KC), DSA_KC)
        ka = ka16_scr[ks, :]
        kb = kb16i_scr[ks, :]
        acc = jnp.zeros((DSA_KC, DSA_QB), F32)
        for p in range(IDX_HEADS // 2):
            qp = qi_ref[:, p * LANE:(p + 1) * LANE]
            acc = acc + jnp.maximum(_dot_nt(ka, qp), 0.0) * w_t[2 * p:2 * p + 1, :]
            acc = acc + jnp.maximum(_dot_nt(kb, qp), 0.0) * w_t[2 * p + 1:2 * p + 2, :]
        key = _score_to_key(acc)
        valid = (c * DSA_KC + key_s) <= row_t
        key_scr[ks, :] = jnp.where(valid, key, INT_MIN)
        kmax = jnp.maximum(kmax, jnp.max(jnp.where(valid, key, INT_MIN), axis=0, keepdims=True))
        kmin = jnp.minimum(kmin, jnp.min(jnp.where(valid, key, INT_MAX), axis=0, keepdims=True))
        return kmax, kmin

    kmax, kmin = lax.fori_loop(0, n_kc, score_chunk, (jnp.full(row_vec, INT_MIN, I32),
                                                     jnp.full(row_vec, INT_MAX, I32)))

    def scan_keys(cand, with_below):
        def body(c, carry):
            cnt, below = carry
            ks = pl.ds(pl.multiple_of(c * DSA_KC, DSA_KC), DSA_KC)
            keys = key_scr[ks, :]
            ge = keys >= cand
            ones = ge.astype(I32)
            low = jnp.where(ge, INT_MIN, keys)
            for u in range(DSA_KC // SUBLANE):
                us = slice(u * SUBLANE, (u + 1) * SUBLANE)
                cnt = cnt + ones[us, :]
                if with_below:
                    below = jnp.maximum(below, low[us, :])
            return cnt, below
        cnt, below = lax.fori_loop(0, n_kc, body, (jnp.zeros((SUBLANE, DSA_QB), I32),
                                                   jnp.full((SUBLANE, DSA_QB), INT_MIN, I32)))
        cnt = jnp.sum(cnt, axis=0, keepdims=True)
        if with_below:
            return cnt, jnp.max(below, axis=0, keepdims=True)
        return cnt

    def open_rows(lo, hi, c_lo):
        return (c_lo > topk) & (hi - 1 > lo)

    def any_row(flag):
        return jnp.max(jnp.where(flag, 1, 0))

    def update(cand, cnt, lo, hi, c_lo, c_hi):
        ge = cnt >= topk
        return (jnp.where(ge, cand, lo), jnp.where(ge, hi, cand),
                jnp.where(ge, cnt, c_lo), jnp.where(ge, c_hi, cnt))

    def crowded(lo, hi, c_lo, c_hi):
        return any_row(open_rows(lo, hi, c_lo) & (c_lo - c_hi > DSA_FEW_KEYS))

    def halve_step(it, lo, hi, c_lo, c_hi):
        key_mid = (lo >> 1) + (hi >> 1) + (lo & hi & 1)
        score_mid = _score_to_key(0.5 * _key_to_score(lo) + 0.5 * _key_to_score(hi - 1))
        cand = jnp.where(it < DSA_SCORE_MID_STEPS, score_mid, key_mid)
        cand = jnp.minimum(jnp.maximum(cand, lo + 1), hi - 1)
        cand = jnp.where(hi - 1 > lo, cand, lo)
        return update(cand, scan_keys(cand, False), lo, hi, c_lo, c_hi)

    def halve_body(st):
        it, _, lo, hi, c_lo, c_hi = st
        go = crowded(lo, hi, c_lo, c_hi)
        return (it + 1, go) + halve_step(it, lo, hi, c_lo, c_hi)

    def walk_step(lo, hi, c_lo, c_hi, nxt):
        is_open = open_rows(lo, hi, c_lo)
        cand = jnp.where(is_open, nxt, lo)
        cnt, below = scan_keys(cand, True)
        ge = cnt >= topk
        hi = jnp.where(is_open, jnp.where(ge, cand + 1, cand), hi)
        c_hi = jnp.where(is_open & jnp.logical_not(ge), cnt, c_hi)
        lo = jnp.where(is_open & ge, cand, lo)
        c_lo = jnp.where(is_open & ge, cnt, c_lo)
        nxt = jnp.where(ge, nxt, below)
        return lo, hi, c_lo, c_hi, nxt

    def walk_body(st):
        go = any_row(open_rows(st[1], st[2], st[3]))
        return (go,) + walk_step(*st[1:])

    n_valid = jnp.minimum(t0 + lax.broadcasted_iota(I32, row_vec, 1) + 1, seq)
    st = (kmin, kmax + 1, n_valid, jnp.zeros(row_vec, I32))
    st = lax.fori_loop(0, DSA_HALVE_FIXED, lambda it, s: halve_step(it, *s), st)
    st = lax.while_loop(lambda s: s[1] > 0, halve_body,
                        (jnp.int32(DSA_HALVE_FIXED), crowded(*st)) + st)[2:]
    _, nxt0 = scan_keys(st[1], True)
    st = lax.fori_loop(0, DSA_WALK_FIXED, lambda it, s: walk_step(*s), st + (nxt0,))
    _, thr, hi, n_ge, n_gt, _ = lax.while_loop(
        lambda s: s[0] > 0, walk_body, (any_row(open_rows(st[0], st[1], st[2])),) + st)

    tied = n_ge > topk
    has_tie = jnp.max(jnp.where(tied, 1, 0)) > 0

    @pl.when(has_tie)
    def _():
        room = (topk - n_gt).astype(F32)
        ii = lax.broadcasted_iota(I32, (LANE, LANE), 0)
        jj = lax.broadcasted_iota(I32, (LANE, LANE), 1)
        lower = (ii >= jj).astype(BF16)

        def body(c, seen):
            ks = pl.ds(pl.multiple_of(c * LANE, LANE), LANE)
            kk = key_scr[ks, :]
            eq = kk == thr
            rank = seen + _dot(lower, eq.astype(BF16))
            drop = eq & (rank > room) & tied
            key_scr[ks, :] = jnp.where(drop, INT_MIN, kk)
            return seen + jnp.sum(eq.astype(F32), axis=0, keepdims=True)
        lax.fori_loop(0, n_kc * (DSA_KC // LANE), body, jnp.zeros(row_vec, F32))

    def mask_chunk(c, carry):
        ks = pl.ds(pl.multiple_of(c * DSA_KC, DSA_KC), DSA_KC)
        sel = jnp.where(key_scr[ks, :] >= thr, 0.0, -jnp.inf).astype(F32)
        key_scr[ks, :] = pltpu.bitcast(sel, I32)
        return carry
    lax.fori_loop(0, n_kc, mask_chunk, 0)

    heads = range(N_HEADS)
    hsl = [slice(h * HEAD_DIM, (h + 1) * HEAD_DIM) for h in heads]

    def logit_chunk(c, ms):
        ks = pl.ds(pl.multiple_of(c * DSA_KC, DSA_KC), DSA_KC)
        k_c = kb16_scr[ks, :]
        sel = pltpu.bitcast(key_scr[ks, :], F32)
        out = []
        for h in heads:
            s = _dot_nt(k_c, q_ref[:, hsl[h]]) * (HEAD_DIM ** -0.5 * LOG2E) + sel
            lg_scr[h, ks, :] = s
            out.append(jnp.maximum(ms[h], jnp.max(s, axis=0, keepdims=True)))
        return tuple(out)
    ms = lax.fori_loop(0, n_kc, logit_chunk,
                       tuple(jnp.full(row_vec, -jnp.inf, F32) for _ in heads))

    band_off = pl.multiple_of(jnp.maximum(qb_idx - 1, 0) * DSA_QB, DSA_QB)
    ws = pl.ds(band_off, 2 * DSA_QB)
    ms = list(ms)
    for h in heads:
        band_h = band_scr[h]
        band_first = jnp.concatenate([band_h[DSA_QB:, :], jnp.zeros((DSA_QB, DSA_QB), F32)], axis=0)
        win = lg_scr[h, ws, :] + jnp.where(qb_idx == 0, band_first, band_h)
        lg_scr[h, ws, :] = win
        ms[h] = jnp.maximum(ms[h], jnp.max(win, axis=0, keepdims=True))

    def pv_chunk(c, accs):
        ks = pl.ds(pl.multiple_of(c * DSA_KC, DSA_KC), DSA_KC)
        vt_c = vt_scr[:, ks]
        return tuple(accs[h] + _dot(vt_c, jnp.exp2(lg_scr[h, ks, :] - ms[h]).astype(BF16))
                     for h in heads)
    accs = lax.fori_loop(0, n_kc, pv_chunk,
                         tuple(jnp.zeros((2 * HEAD_DIM, DSA_QB), F32) for _ in heads))
    for h in heads:
        o_ref[:, hsl[h]] = (accs[h][:HEAD_DIM, :] / accs[h][HEAD_DIM:, :]).T.astype(BF16)


def dsa_attention(proj, q, q_idx, prep_tok, rel_bias, batch, seq):
    n = proj.shape[0]
    nq = seq // DSA_QB
    topk = min(TOPK_MAX, seq // 4)
    wi = IDX_HEADS * IDX_DIM
    rowblk = lambda w, cb: pl.BlockSpec((DSA_QB, w), lambda b, i: (b * nq + i, cb))
    seqblk = lambda off: pl.BlockSpec((seq, LANE), lambda b, i: (b, off // LANE))
    return pl.pallas_call(
        functools.partial(_dsa_kernel, seq=seq, topk=topk),
        grid=(batch, nq),
        in_specs=[pl.BlockSpec(memory_space=pltpu.SMEM),
                  rowblk(WIDTH, 0), rowblk(wi, 0), rowblk(LANE, 0),
                  seqblk(C_DK), seqblk(C_DV), seqblk(C_KA), seqblk(C_KB)],
        out_specs=pl.BlockSpec((DSA_QB, WIDTH), lambda b, i: (b * nq + i, 0)),
        out_shape=jax.ShapeDtypeStruct((n, WIDTH), BF16),
        scratch_shapes=[pltpu.VMEM((seq, DSA_QB), I32),
                        pltpu.VMEM((N_HEADS, seq, DSA_QB), F32),
                        pltpu.VMEM((N_HEADS, 2 * DSA_QB, DSA_QB), F32),
                        pltpu.VMEM((seq, LANE), BF16),
                        pltpu.VMEM((2 * HEAD_DIM, seq), BF16),
                        pltpu.VMEM((seq, LANE), BF16),
                        pltpu.VMEM((seq, LANE), BF16)],
        compiler_params=_cparams(("arbitrary", "arbitrary")),
        name="dsa_attention",
    )(rel_bias, q, q_idx, prep_tok, proj, proj, proj, proj)


GDN_T = 256
GDN_GROUP = 2
GDN_HALO = 8


def _gdn_kernel(x_ref, cw_ref, ng_ref, tok_ref, tr_ref, o_ref,
                xq_scr, xk_scr, xv_scr, state_scr):
    first = pl.program_id(1) == 0

    @pl.when(first)
    def _():
        state_scr[...] = jnp.zeros_like(state_scr)
        for scr in (xq_scr, xk_scr, xv_scr):
            scr[0:GDN_HALO, :] = jnp.zeros((GDN_HALO, WIDTH), F32)

    def conv(col, scr, w_off):
        scr[GDN_HALO:, :] = x_ref[:, col - C_GQ:col - C_GQ + WIDTH]
        y = jnp.zeros((GDN_T, WIDTH), F32)
        for i in range(GDN_CONV):
            st = GDN_HALO - (GDN_CONV - 1) + i
            y = y + scr[st:st + GDN_T, :] * cw_ref[i:i + 1, w_off:w_off + WIDTH]
        scr[0:GDN_HALO, :] = scr[GDN_T:GDN_T + GDN_HALO, :]
        return _silu(y)

    qc = conv(C_GQ, xq_scr, 0)
    kc = conv(C_GK, xk_scr, WIDTH)
    vc = conv(C_GV, xv_scr, 2 * WIDTH)
    z_off = C_GZ - C_GQ
    tok = tok_ref[...]
    c = GDN_CHUNK
    heads = range(N_HEADS)
    hsl = [slice(h * HEAD_DIM, (h + 1) * HEAD_DIM) for h in heads]

    def l2norm_heads(x, scale):
        return jnp.concatenate(
            [x[:, s] * (lax.rsqrt(jnp.sum(x[:, s] * x[:, s], axis=-1, keepdims=True) + EPS) * scale)
             for s in hsl], axis=1)

    qf = l2norm_heads(qc, HEAD_DIM ** -0.5)
    kf = l2norm_heads(kc, 1.0)

    grp = GDN_GROUP
    gw = grp * HEAD_DIM
    nb = grp * c
    ri = lax.broadcasted_iota(I32, (nb, nb), 0)
    ci = lax.broadcasted_iota(I32, (nb, nb), 1)
    tril = ((ri // c) == (ci // c)) & (ri >= ci)
    eye = (ri == ci).astype(F32)
    pair_masks = []
    for lg in range(c.bit_length() - 1):
        pair_masks.append(((ri >> (lg + 1)) == (ci >> (lg + 1)))
                          & (((ri >> lg) & 1) == 1) & (((ci >> lg) & 1) == 0))
    lane_head = lax.broadcasted_iota(I32, (c, gw), 1) // HEAD_DIM
    row_head = lax.broadcasted_iota(I32, (nb, HEAD_DIM), 0) // c

    def spread(x):
        return jnp.concatenate([jnp.where(lane_head == u, x, 0.0) for u in range(grp)], axis=0)

    def stack(x):
        return jnp.concatenate([x[:, hsl[u]] for u in range(grp)], axis=0)

    def spread_lanes(x):
        return jnp.concatenate([jnp.where(row_head == u, x, 0.0) for u in range(grp)], axis=1)

    def split(x):
        hi = x.astype(BF16)
        return hi, (x - hi.astype(F32)).astype(BF16)

    def dot_split(a, b):
        return _dot(a[0], b[0]) + (_dot(a[0], b[1]) + _dot(a[1], b[0]))

    items = [(j, gi) for j in range(GDN_T // c) for gi in range(N_HEADS // grp)]
    pre = []
    for j, gi in items:
        rs = slice(j * c, (j + 1) * c)
        last = slice((j + 1) * c - 1, (j + 1) * c)
        gh = [gi * grp + u for u in range(grp)]
        gsl = slice(gi * gw, (gi + 1) * gw)
        qj, kj, vj = qf[rs, gsl], kf[rs, gsl], vc[rs, gsl]
        b_col = jnp.concatenate([tok[rs, L_GB + h:L_GB + h + 1] for h in gh], axis=0)
        g_col = jnp.concatenate([tok[rs, L_GA + h:L_GA + h + 1] for h in gh], axis=0)
        g_row = jnp.concatenate([tr_ref[0, L_GA + h:L_GA + h + 1, rs] for h in gh], axis=1)
        g_last = [tok[last, L_GA + h:L_GA + h + 1] for h in gh]
        g_last_col = jnp.concatenate([jnp.broadcast_to(g, (c, 1)) for g in g_last], axis=0)
        k_sp = spread(kj)
        q_sp = spread(qj)
        k_sp16 = k_sp.astype(BF16)
        decay = jnp.exp(jnp.where(tril, g_col - g_row, -jnp.inf))
        eg = jnp.exp(g_col)
        pre.append(dict(
            gh=gh, gsl=gsl,
            l_mat=b_col * _dot_nt(k_sp16, k_sp16) * decay,
            rhs=jnp.concatenate([stack(vj) * b_col, stack(kj) * (b_col * eg)], axis=1),
            qk=_dot_nt(q_sp.astype(BF16), k_sp16) * decay,
            q_dec=q_sp * eg,
            k_dec=k_sp * jnp.exp(g_last_col - g_col),
            e_last=jnp.concatenate([jnp.broadcast_to(jnp.exp(g), (HEAD_DIM, 1)) for g in g_last], axis=0)))

    t_inv = [eye - jnp.where(pair_masks[0], p["l_mat"], 0.0) for p in pre]
    for pm in pair_masks[1:]:
        t_s = [split(t) for t in t_inv]
        m_t = [dot_split(split(jnp.where(pm, p["l_mat"], 0.0)), ts) for p, ts in zip(pre, t_s)]
        t_inv = [t - dot_split(ts, split(m)) for t, ts, m in zip(t_inv, t_s, m_t)]
    sols = [dot_split(split(t), split(p["rhs"])) for t, p in zip(t_inv, pre)]

    outs = [[] for _ in heads]
    for p, sol in zip(pre, sols):
        u0 = sol[:, :HEAD_DIM]
        kcum = sol[:, HEAD_DIM:]
        st = state_scr[p["gsl"], :]
        stb = st.astype(BF16)
        v_new = u0 - _dot(spread_lanes(kcum).astype(BF16), stb)
        v_new_b = v_new.astype(BF16)
        o_st = _dot(p["q_dec"].astype(BF16), stb) + _dot(p["qk"].astype(BF16), v_new_b)
        state_scr[p["gsl"], :] = st * p["e_last"] + _dot(p["k_dec"].T.astype(BF16), v_new_b)
        for u, h in enumerate(p["gh"]):
            outs[h].append(o_st[u * c:(u + 1) * c, :])

    for h in heads:
        o = jnp.concatenate(outs[h], axis=0)
        ms = jnp.mean(o * o, axis=-1, keepdims=True)
        on = o * lax.rsqrt(ms + EPS) * ng_ref[...]
        z = x_ref[:, z_off + hsl[h].start:z_off + hsl[h].stop]
        o_ref[:, hsl[h]] = (on * _silu(z)).astype(BF16)


def gated_deltanet(proj, prep_tok, prep_tr, conv_w, norm_g, batch, seq):
    n = proj.shape[0]
    t = GDN_T
    nt = seq // t
    return pl.pallas_call(
        _gdn_kernel,
        grid=(batch, nt),
        in_specs=[pl.BlockSpec((t, 4 * WIDTH), lambda b, i: (b * nt + i, C_GQ // (4 * WIDTH))),
                  pl.BlockSpec((GDN_CONV, 3 * WIDTH), lambda b, i: (0, 0)),
                  pl.BlockSpec((1, HEAD_DIM), lambda b, i: (0, 0)),
                  pl.BlockSpec((t, LANE), lambda b, i: (b * nt + i, 0)),
                  pl.BlockSpec((1, PREP_ROWS, t), lambda b, i: (b, 0, i))],
        out_specs=pl.BlockSpec((t, WIDTH), lambda b, i: (b * nt + i, 0)),
        out_shape=jax.ShapeDtypeStruct((n, WIDTH), BF16),
        scratch_shapes=[pltpu.VMEM((t + GDN_HALO, WIDTH), F32),
                        pltpu.VMEM((t + GDN_HALO, WIDTH), F32),
                        pltpu.VMEM((t + GDN_HALO, WIDTH), F32),
                        pltpu.VMEM((N_HEADS * HEAD_DIM, HEAD_DIM), F32)],
        compiler_params=_cparams(("arbitrary", "arbitrary")),
        name="gated_deltanet",
    )(proj, conv_w, norm_g.reshape(1, HEAD_DIM), prep_tok, prep_tr)


def _merge_kernel(h_ref, b0_ref, b1_ref, b2_ref, b3_ref, g0_ref, g1_ref, g2_ref, g3_ref,
                  wb_ref, o_ref):
    h = h_ref[...]
    acc = None
    for n, (b_ref, g_ref) in enumerate(zip((b0_ref, b1_ref, b2_ref, b3_ref),
                                           (g0_ref, g1_ref, g2_ref, g3_ref))):
        gate = jax.nn.sigmoid(_dot(h, g_ref[...]))
        term = gate * _dot(b_ref[...], wb_ref[n])
        acc = term if acc is None else acc + term
    o_ref[...] = acc.astype(BF16)


def merge_branches(h, branches, w_gate, w_branch, *, tm=512, tn=512):
    _, n, d = h.shape
    nj = d // tn
    bspec = pl.BlockSpec((tm, WIDTH), lambda j, i: (i, 0))
    gspec = lambda k: pl.BlockSpec((d, tn), lambda j, i: (0, k * nj + j))
    return pl.pallas_call(
        _merge_kernel,
        grid=(nj, n // tm),
        in_specs=[pl.BlockSpec((None, tm, d), lambda j, i: (0, i, 0)),
                  bspec, bspec, bspec, bspec,
                  gspec(0), gspec(1), gspec(2), gspec(3),
                  pl.BlockSpec((N_BRANCH, WIDTH, tn), lambda j, i: (0, 0, j))],
        out_specs=pl.BlockSpec((tm, tn), lambda j, i: (i, j)),
        out_shape=jax.ShapeDtypeStruct((n, d), BF16),
        compiler_params=_cparams(("arbitrary", "arbitrary")),
        name="merge_branches",
    )(h, *branches, w_gate, w_gate, w_gate, w_gate, w_branch)


def _resid_mm_kernel(a_ref, w_ref, x_ref, o_ref):
    o_ref[...] = x_ref[...] + _dot(a_ref[...], w_ref[...])


def resid_matmul(a, w, x, *, tm=512, tn=1024, name="resid_matmul"):
    n, k = a.shape
    d = w.shape[1]
    return pl.pallas_call(
        _resid_mm_kernel,
        grid=(d // tn, n // tm),
        in_specs=[pl.BlockSpec((tm, k), lambda j, i: (i, 0)),
                  pl.BlockSpec((k, tn), lambda j, i: (0, j)),
                  pl.BlockSpec((tm, tn), lambda j, i: (i, j))],
        out_specs=pl.BlockSpec((tm, tn), lambda j, i: (i, j)),
        out_shape=jax.ShapeDtypeStruct((n, d), F32),
        compiler_params=_cparams(("arbitrary", "arbitrary")),
        name=name,
    )(a, w, x)


FFN_HALO = 8
FFN_SUB = 512


def _ffn1_kernel(x_ref, g_ref, wg_ref, wu_ref, cw_ref, cb_ref, o_ref, h_scr, gt_scr, halo_scr,
                 *, tm, tiles_per_seq):
    i = pl.program_id(1)

    @pl.when(i == 0)
    def _():
        halo_scr[...] = jnp.zeros_like(halo_scr)

    def body(r, carry):
        rows = pl.ds(pl.multiple_of(r * NORM_ROWS, NORM_ROWS), NORM_ROWS)
        h_scr[rows, :] = _rmsnorm_rows(x_ref, g_ref, rows).astype(BF16)
        return carry
    lax.fori_loop(0, tm // NORM_ROWS, body, 0)

    h = h_scr[...]
    seq_start = (i % tiles_per_seq) == 0
    tn = o_ref.shape[1]
    for off in range(0, tn, FFN_SUB):
        cs = slice(off, min(off + FFN_SUB, tn))
        g = _dot(h, wg_ref[:, cs])
        gt_scr[FFN_HALO:, cs] = g
        gt_scr[0:FFN_HALO, cs] = jnp.where(seq_start, 0.0, halo_scr[:, cs])
        halo_scr[:, cs] = g[tm - FFN_HALO:, :]
        y = cb_ref[:, cs] + g * cw_ref[FFN_CONV - 1:FFN_CONV, cs]
        for t in range(FFN_CONV - 1):
            st = FFN_HALO - (FFN_CONV - 1) + t
            y = y + gt_scr[st:st + tm, cs] * cw_ref[t:t + 1, cs]
        o_ref[:, cs] = (_silu(y) * _dot(h, wu_ref[:, cs])).astype(BF16)


def conv_ffn_up(x, gain, w_gate, w_up, conv_w, conv_b, seq, *, tm=512, col_parts=2):
    n, d = x.shape
    f = w_gate.shape[1]
    tn = f // col_parts
    wspec = pl.BlockSpec((d, tn), lambda j, i: (0, j), pipeline_mode=pl.Buffered(1))
    return pl.pallas_call(
        functools.partial(_ffn1_kernel, tm=tm, tiles_per_seq=seq // tm),
        grid=(col_parts, n // tm),
        in_specs=[pl.BlockSpec((tm, d), lambda j, i: (i, 0)),
                  pl.BlockSpec((1, d), lambda j, i: (0, 0)),
                  wspec, wspec,
                  pl.BlockSpec((FFN_CONV, tn), lambda j, i: (0, j)),
                  pl.BlockSpec((1, tn), lambda j, i: (0, j))],
        out_specs=pl.BlockSpec((tm, tn), lambda j, i: (i, j)),
        out_shape=jax.ShapeDtypeStruct((n, f), BF16),
        scratch_shapes=[pltpu.VMEM((tm, d), BF16),
                        pltpu.VMEM((tm + FFN_HALO, tn), F32),
                        pltpu.VMEM((FFN_HALO, tn), F32)],
        compiler_params=_cparams(("arbitrary", "arbitrary")),
        name="conv_ffn_up",
    )(x, gain.reshape(1, d), w_gate, w_up, conv_w, conv_b.reshape(1, f))


IN_SIZES = (WIDTH, WIDTH, WIDTH, WIDTH,
            DSA_Q_RANK, HEAD_DIM, HEAD_DIM, IDX_DIM, IDX_HEADS,
            WIDTH, WIDTH, WIDTH, N_HEADS,
            WIDTH, WIDTH, WIDTH, WIDTH, N_HEADS, N_HEADS)
IN_NAMES = ("r_q", "r_k", "r_v", "r_g", "d_cq", "d_k", "d_v", "i_k", "i_w",
            "f_q", "f_k", "f_v", "f_f", "g_q", "g_k", "g_v", "g_z", "g_b", "g_a")
IN_PLAN = (("r_q", C_RQ), ("r_k", C_RK), ("r_v", C_RV), ("r_g", C_RG),
           ("f_q", C_FQ), ("f_k", C_FK), ("f_v", C_FV),
           ("g_q", C_GQ), ("g_k", C_GK), ("g_v", C_GV), ("g_z", C_GZ),
           ("d_k", C_DK), ("d_cq", C_DCQ), ("d_v", C_DV),
           ("i_k", C_KA), ("i_k", C_KB + IDX_DIM),
           ("i_w", C_SM + L_IW), ("f_f", C_SM + L_FF), ("g_b", C_SM + L_GB), ("g_a", C_SM + L_GA))


def _prep_w_in_kernel(w_ref, m_ref, g_ref):
    src = {}
    off = 0
    for name, size in zip(IN_NAMES, IN_SIZES):
        src[name] = (off, size)
        off += size
    m_ref[...] = jnp.zeros_like(m_ref)
    for name, dst in IN_PLAN:
        so, w = src[name]
        m_ref[:, dst:dst + w] = w_ref[:, so:so + w].astype(BF16)
    g_ref[...] = w_ref[:, off:off + g_ref.shape[1]].astype(BF16)


def prep_w_in(w_in, layer, *, tr=128):
    _, d, c = w_in.shape
    return pl.pallas_call(
        _prep_w_in_kernel,
        grid=(d // tr,),
        in_specs=[pl.BlockSpec((None, tr, c), lambda i: (layer, i, 0))],
        out_specs=[pl.BlockSpec((tr, C_TOT), lambda i: (i, 0)),
                   pl.BlockSpec((tr, N_BRANCH * d), lambda i: (i, 0))],
        out_shape=[jax.ShapeDtypeStruct((d, C_TOT), BF16),
                   jax.ShapeDtypeStruct((d, N_BRANCH * d), BF16)],
        compiler_params=_cparams(("arbitrary",)),
        name="prep_w_in",
    )(w_in)


def _cast_kernel(w_ref, o_ref):
    o_ref[...] = w_ref[...].astype(BF16)


def cast_layer(w, layer, *, tr=256):
    _, r, c = w.shape
    if r % tr:
        tr = r
    return pl.pallas_call(
        _cast_kernel,
        grid=(r // tr,),
        in_specs=[pl.BlockSpec((None, tr, c), lambda i: (layer, i, 0))],
        out_specs=pl.BlockSpec((tr, c), lambda i: (i, 0)),
        out_shape=jax.ShapeDtypeStruct((r, c), BF16),
        compiler_params=_cparams(("arbitrary",)),
        name="cast_bf16",
    )(w)


def cast_branch(w_branch, layer):
    _, nbr, r, c = w_branch.shape
    return pl.pallas_call(
        _cast_kernel,
        grid=(nbr,),
        in_specs=[pl.BlockSpec((None, None, r, c), lambda i: (layer, i, 0, 0))],
        out_specs=pl.BlockSpec((None, r, c), lambda i: (i, 0, 0)),
        out_shape=jax.ShapeDtypeStruct((nbr, r, c), BF16),
        compiler_params=_cparams(("arbitrary",)),
        name="cast_branch",
    )(w_branch)


def kernel(x, norm_mix, w_in, dsa_cq_norm, dsa_w_uq, dsa_w_qidx, fox_f_bias, gdn_conv, gdn_a_log,
           gdn_dt_bias, gdn_norm, w_branch, w_out, rel_bias, norm_ffn, ffn_w_gate, ffn_w_up,
           ffn_conv, ffn_conv_b, ffn_w_down, final_norm):
    batch, seq, d = x.shape
    depth = w_in.shape[0]
    xf = x.reshape(batch * seq, d)
    ret_tables = _retention_tables(seq)
    for l in range(depth):
        w_main, w_gate = prep_w_in(w_in, l)
        proj, h = norm_proj(xf, norm_mix[l], w_main)
        par = jnp.zeros((SUBLANE, LANE), F32)
        par = par.at[0, L_FF:L_FF + N_HEADS].set(fox_f_bias[l])
        par = par.at[0, L_GA:L_GA + N_HEADS].set(gdn_dt_bias[l])
        par = par.at[1, L_GA:L_GA + N_HEADS].set(gdn_a_log[l])
        prep_tok, prep_tr = prep_small(proj, par, batch, seq)
        o_ret = retention(proj, ret_tables, batch, seq)
        q_dsa, q_idx = dsa_proj(proj, dsa_cq_norm[l], cast_layer(dsa_w_uq, l), cast_layer(dsa_w_qidx, l))
        o_dsa = dsa_attention(proj, q_dsa, q_idx, prep_tok, rel_bias, batch, seq)
        o_fox = fox_attention(proj, prep_tr, batch, seq)
        o_gdn = gated_deltanet(proj, prep_tok, prep_tr, gdn_conv[l], gdn_norm[l], batch, seq)
        merged = merge_branches(h, (o_ret, o_dsa, o_fox, o_gdn), w_gate, cast_branch(w_branch, l))
        xf = resid_matmul(merged, cast_layer(w_out, l), xf, tn=d, name="out_proj")
        act = conv_ffn_up(xf, norm_ffn[l], cast_layer(ffn_w_gate, l), cast_layer(ffn_w_up, l),
                          ffn_conv[l], ffn_conv_b[l], seq)
        xf = resid_matmul(act, cast_layer(ffn_w_down, l), xf, name="ffn_down")
    return rmsnorm(xf, final_norm).reshape(batch, seq, d)
```

```python
import functools
import math

import jax
import jax.numpy as jnp
from jax import lax
from jax.experimental import pallas as pl
from jax.experimental.pallas import tpu as pltpu

F32 = jnp.float32
BF16 = jnp.bfloat16
I32 = jnp.int32

HEAD_DIM = 128
N_HEADS = 4
WIDTH = N_HEADS * HEAD_DIM
N_BRANCH = 4
RET_CHUNK = 128
ROPE_BASE = 10000.0
DSA_Q_RANK = 384
IDX_HEADS = 16
IDX_DIM = 64
TOPK_MAX = 256
GDN_CONV = 4
GDN_CHUNK = 64
REL_BUCKETS = 32
REL_MAX_DIST = 128
FFN_CONV = 3
EPS = 1e-6

LANE = 128
SUBLANE = 8
VMEM_LIMIT = 56 * 1024 * 1024

C_RQ, C_RK, C_RV, C_RG = 0, 512, 1024, 1536
C_FQ, C_DK, C_DCQ = 2048, 2560, 2688
C_FK, C_FV = 3072, 3584
C_GQ, C_GK, C_GV, C_GZ = 4096, 4608, 5120, 5632
C_DV, C_KA, C_KB, C_SM = 6144, 6272, 6400, 6528
C_TOT = 6656
L_IW, L_FF, L_GB, L_GA = 0, 16, 20, 24

LOG2E = 1.4426950408889634
INT_MIN = -(2 ** 31)
INT_MAX = 2 ** 31 - 1
HIGHEST = lax.Precision.HIGHEST


def _cparams(sem, vmem=VMEM_LIMIT):
    return pltpu.CompilerParams(dimension_semantics=sem, vmem_limit_bytes=vmem)


def _dot(a, b):
    return jnp.dot(a, b, preferred_element_type=F32)


def _dot_nt(a, b):
    return lax.dot_general(a, b, (((1,), (1,)), ((), ())), preferred_element_type=F32)


def _silu(x):
    return x * jax.nn.sigmoid(x)


NORM_ROWS = 128


def _rmsnorm_rows(x_ref, g_ref, rows):
    x = x_ref[rows, :]
    ms = jnp.mean(x * x, axis=-1, keepdims=True)
    return x * lax.rsqrt(ms + EPS) * g_ref[...]


PROJ_SUB = 512


def _norm_proj_kernel(x_ref, g_ref, w_ref, o_ref, h_ref, h_scr, *, tm):
    def body(r, carry):
        rows = pl.ds(pl.multiple_of(r * NORM_ROWS, NORM_ROWS), NORM_ROWS)
        hb = _rmsnorm_rows(x_ref, g_ref, rows).astype(BF16)
        h_scr[rows, :] = hb
        h_ref[rows, :] = hb
        return carry
    lax.fori_loop(0, tm // NORM_ROWS, body, 0)

    h = h_scr[...]
    tn = o_ref.shape[1]
    for off in range(0, tn, PROJ_SUB):
        cs = slice(off, min(off + PROJ_SUB, tn))
        o_ref[:, cs] = _dot(h, w_ref[:, cs])


def norm_proj(x, gain, w, *, tm=512, col_parts=2):
    n, d = x.shape
    c = w.shape[1]
    tn = c // col_parts
    return pl.pallas_call(
        functools.partial(_norm_proj_kernel, tm=tm),
        grid=(col_parts, n // tm),
        in_specs=[pl.BlockSpec((tm, d), lambda j, i: (i, 0)),
                  pl.BlockSpec((1, d), lambda j, i: (0, 0)),
                  pl.BlockSpec((d, tn), lambda j, i: (0, j), pipeline_mode=pl.Buffered(1))],
        out_specs=[pl.BlockSpec((tm, tn), lambda j, i: (i, j)),
                   pl.BlockSpec((None, tm, d), lambda j, i: (j, i, 0))],
        out_shape=[jax.ShapeDtypeStruct((n, c), F32),
                   jax.ShapeDtypeStruct((col_parts, n, d), BF16)],
        scratch_shapes=[pltpu.VMEM((tm, d), BF16)],
        compiler_params=_cparams(("arbitrary", "arbitrary")),
        name="norm_proj",
    )(x, gain.reshape(1, d), w)


def _rmsnorm_kernel(x_ref, g_ref, o_ref, *, tm):
    def body(r, carry):
        rows = pl.ds(pl.multiple_of(r * NORM_ROWS, NORM_ROWS), NORM_ROWS)
        o_ref[rows, :] = _rmsnorm_rows(x_ref, g_ref, rows)
        return carry
    lax.fori_loop(0, tm // NORM_ROWS, body, 0)


def rmsnorm(x, gain, *, tm=512):
    n, d = x.shape
    return pl.pallas_call(
        functools.partial(_rmsnorm_kernel, tm=tm),
        grid=(n // tm,),
        in_specs=[pl.BlockSpec((tm, d), lambda i: (i, 0)),
                  pl.BlockSpec((1, d), lambda i: (0, 0))],
        out_specs=pl.BlockSpec((tm, d), lambda i: (i, 0)),
        out_shape=jax.ShapeDtypeStruct((n, d), F32),
        compiler_params=_cparams(("arbitrary",)),
        name="final_rmsnorm",
    )(x, gain.reshape(1, d))


def _prep_kernel(s_ref, par_ref, tok_ref, tr_ref, carry_scr):
    @pl.when(pl.program_id(1) == 0)
    def _():
        carry_scr[...] = jnp.zeros_like(carry_scr)

    s = s_ref[...]
    lane = lax.broadcasted_iota(I32, (LANE, LANE), 1)
    row = lax.broadcasted_iota(I32, (LANE, LANE), 0)
    z = s + par_ref[0:1, :]
    soft = jnp.maximum(z, 0.0) + jnp.log1p(jnp.exp(-jnp.abs(z)))
    log_sig = z - soft
    sig = jax.nn.sigmoid(z)
    g_val = -jnp.exp(par_ref[1:2, :]) * soft
    is_f = (lane[0:1] >= L_FF) & (lane[0:1] < L_FF + N_HEADS)
    is_b = (lane[0:1] >= L_GB) & (lane[0:1] < L_GB + N_HEADS)
    is_a = (lane[0:1] >= L_GA) & (lane[0:1] < L_GA + N_HEADS)
    pre = jnp.where(is_f, log_sig, jnp.where(is_a, g_val, 0.0))
    tri = (row >= lane).astype(F32)
    tri_blk = ((row >= lane) & ((row // GDN_CHUNK) == (lane // GDN_CHUNK))).astype(F32)
    subs = [slice(u * LANE, (u + 1) * LANE) for u in range(PREP_T // LANE)]
    cum_full = [jnp.dot(tri, pre[u], precision=HIGHEST, preferred_element_type=F32) for u in subs]
    cum_blk = [jnp.dot(tri_blk, pre[u], precision=HIGHEST, preferred_element_type=F32) for u in subs]
    scale_iw = IDX_HEADS ** -0.5 * IDX_DIM ** -0.5
    carry = carry_scr[0:1, :]
    for u, cf, cb in zip(subs, cum_full, cum_blk):
        c_fox = cf + carry
        carry = c_fox[LANE - 1:LANE, :]
        out = jnp.where(is_f, c_fox,
                        jnp.where(is_a, cb,
                                  jnp.where(is_b, sig[u],
                                            jnp.where(lane[0:1] < IDX_HEADS, s[u] * scale_iw, 0.0))))
        tok_ref[u, :] = out
        tr_ref[0, :, u] = out.T[0:PREP_ROWS, :]
    carry_scr[0:1, :] = carry


PREP_T = 512
PREP_ROWS = 32


def prep_small(proj, par, batch, seq):
    n = proj.shape[0]
    nc = seq // PREP_T
    return pl.pallas_call(
        _prep_kernel,
        grid=(batch, nc),
        in_specs=[pl.BlockSpec((PREP_T, LANE), lambda b, c: (b * nc + c, C_SM // LANE)),
                  pl.BlockSpec((SUBLANE, LANE), lambda b, c: (0, 0))],
        out_specs=[pl.BlockSpec((PREP_T, LANE), lambda b, c: (b * nc + c, 0)),
                   pl.BlockSpec((1, PREP_ROWS, PREP_T), lambda b, c: (b, 0, c))],
        out_shape=[jax.ShapeDtypeStruct((n, LANE), F32),
                   jax.ShapeDtypeStruct((batch, PREP_ROWS, seq), F32)],
        scratch_shapes=[pltpu.VMEM((SUBLANE, LANE), F32)],
        compiler_params=_cparams(("arbitrary", "arbitrary")),
        name="prep_small",
    )(proj, par)


def _ret_gamma():
    return [math.log1p(-(2.0 ** (-5.0 - h))) for h in range(N_HEADS)]


def _retention_kernel(x_ref, cos_ref, sin_ref, dec_ref, zeta_ref, xi_ref, o_ref, state_scr):
    @pl.when(pl.program_id(1) == 0)
    def _():
        state_scr[...] = jnp.zeros_like(state_scr)

    cos_t = cos_ref[...]
    sin_t = sin_ref[...]
    log_gamma = _ret_gamma()
    heads = range(N_HEADS)
    hsl = [slice(h * HEAD_DIM, (h + 1) * HEAD_DIM) for h in heads]

    def part(col):
        return [slice(col - C_RQ + s.start, col - C_RQ + s.stop) for s in hsl]

    def rope(x):
        return x * cos_t + pltpu.roll(x, HEAD_DIM // 2, 1) * sin_t

    qb = [rope(x_ref[:, s]).astype(BF16) for s in part(C_RQ)]
    kr = [rope(x_ref[:, s]) * (HEAD_DIM ** -0.5) for s in part(C_RK)]
    kb = [x.astype(BF16) for x in kr]
    vb = [x_ref[:, s].astype(BF16) for s in part(C_RV)]
    gsl = part(C_RG)
    st = [state_scr[h] for h in heads]
    inner = [(_dot_nt(qb[h], kb[h]) * dec_ref[h]).astype(BF16) for h in heads]
    cross = [_dot(qb[h], st[h].astype(BF16)) * xi_ref[h] for h in heads]
    kv = [_dot((kr[h] * zeta_ref[h]).T.astype(BF16), vb[h]) for h in heads]
    o = [_dot(inner[h], vb[h]) + cross[h] for h in heads]
    for h in heads:
        state_scr[h] = st[h] * math.exp(log_gamma[h] * RET_CHUNK) + kv[h]
        mu = jnp.mean(o[h], axis=-1, keepdims=True)
        oc = o[h] - mu
        var = jnp.mean(oc * oc, axis=-1, keepdims=True)
        o_ref[:, hsl[h]] = (_silu(x_ref[:, gsl[h]]) * (oc * lax.rsqrt(var + EPS))).astype(BF16)


def _retention_tables(seq):
    half = HEAD_DIM // 2
    inv = 1.0 / (ROPE_BASE ** (jnp.arange(half, dtype=F32) / half))
    ang = jnp.arange(seq).astype(F32)[:, None] * inv[None, :]
    cos, sin = jnp.cos(ang), jnp.sin(ang)
    cos_t = jnp.concatenate([cos, cos], axis=-1)
    sin_t = jnp.concatenate([-sin, sin], axis=-1)
    c = RET_CHUNK
    log_gamma = jnp.log1p(-jnp.exp2(-5.0 - jnp.arange(N_HEADS, dtype=F32)))
    n = jnp.arange(c, dtype=F32)
    diff = n[:, None] - n[None, :]
    decay = jnp.where(diff >= 0, jnp.exp(log_gamma[:, None, None] * jnp.maximum(diff, 0.0)), 0.0)
    zeta = jnp.exp(log_gamma[:, None] * (c - 1 - n)[None, :])
    xi = jnp.exp(log_gamma[:, None] * (n + 1)[None, :])
    ones = jnp.ones((1, 1, HEAD_DIM), F32)
    return cos_t, sin_t, decay, zeta[:, :, None] * ones, xi[:, :, None] * ones


def retention(proj, tables, batch, seq):
    n = proj.shape[0]
    c = RET_CHUNK
    nc = seq // c
    cos_t, sin_t, decay, zeta, xi = tables
    full3 = pl.BlockSpec((N_HEADS, c, HEAD_DIM), lambda b, i: (0, 0, 0))
    return pl.pallas_call(
        _retention_kernel,
        grid=(batch, nc),
        in_specs=[pl.BlockSpec((c, 4 * WIDTH), lambda b, i: (b * nc + i, C_RQ // (4 * WIDTH))),
                  pl.BlockSpec((c, HEAD_DIM), lambda b, i: (i, 0)),
                  pl.BlockSpec((c, HEAD_DIM), lambda b, i: (i, 0)),
                  full3, full3, full3],
        out_specs=pl.BlockSpec((c, WIDTH), lambda b, i: (b * nc + i, 0)),
        out_shape=jax.ShapeDtypeStruct((n, WIDTH), BF16),
        scratch_shapes=[pltpu.VMEM((N_HEADS, HEAD_DIM, HEAD_DIM), F32)],
        compiler_params=_cparams(("arbitrary", "arbitrary")),
        name="retention",
    )(proj, cos_t, sin_t, decay, zeta, xi)


def _fox_kernel(qi_ref, ki_ref, q_ref, kv_ref, ctr_ref, o_ref, m_scr, acc_scr, *, t):
    qi = qi_ref[pl.program_id(1)]
    ki = ki_ref[pl.program_id(1)]

    @pl.when(ki == 0)
    def _():
        m_scr[...] = jnp.full_like(m_scr, -jnp.inf)
        acc_scr[...] = jnp.zeros_like(acc_scr)

    def step(masked):
        if masked:
            row = lax.broadcasted_iota(I32, (t, t), 0)
            colm = lax.broadcasted_iota(I32, (t, t), 1)
            keep = row >= colm
        ones = jnp.ones((t, HEAD_DIM), BF16)
        for h in range(N_HEADS):
            sl = slice(h * HEAD_DIM, (h + 1) * HEAD_DIM)
            vsl = slice(WIDTH + h * HEAD_DIM, WIDTH + (h + 1) * HEAD_DIM)
            qb = q_ref[:, sl].astype(BF16)
            kb = kv_ref[:, sl].astype(BF16)
            c_k = ctr_ref[0, L_FF + h:L_FF + h + 1, :] * LOG2E
            s = _dot_nt(qb, kb) * (HEAD_DIM ** -0.5 * LOG2E) - c_k
            if masked:
                s = jnp.where(keep, s, -jnp.inf)
            m_old = m_scr[h]
            m_new = jnp.maximum(m_old, jnp.max(s, axis=-1, keepdims=True))
            alpha = jnp.exp2(m_old - m_new)
            p = jnp.exp2(s - m_new)
            v_aug = jnp.concatenate([kv_ref[:, vsl].astype(BF16), ones], axis=1)
            acc_scr[h] = alpha * acc_scr[h] + _dot(p.astype(BF16), v_aug)
            m_scr[h] = m_new

    @pl.when(ki < qi)
    def _():
        step(False)

    @pl.when(ki == qi)
    def _():
        step(True)
        for h in range(N_HEADS):
            sl = slice(h * HEAD_DIM, (h + 1) * HEAD_DIM)
            acc = acc_scr[h]
            o_ref[:, sl] = (acc[:, :HEAD_DIM] / acc[:, HEAD_DIM:]).astype(BF16)


def fox_attention(proj, prep_tr, batch, seq, *, t=512):
    n = proj.shape[0]
    nt = seq // t
    pairs = [(qi, ki) for qi in range(nt) for ki in range(qi + 1)]
    qi_arr = jnp.asarray([p[0] for p in pairs], I32)
    ki_arr = jnp.asarray([p[1] for p in pairs], I32)
    qspec = pl.BlockSpec((t, WIDTH), lambda b, s, qi, ki: (b * nt + qi[s], C_FQ // WIDTH))
    kvspec = pl.BlockSpec((t, 2 * WIDTH), lambda b, s, qi, ki: (b * nt + ki[s], C_FK // (2 * WIDTH)))
    return pl.pallas_call(
        functools.partial(_fox_kernel, t=t),
        grid_spec=pltpu.PrefetchScalarGridSpec(
            num_scalar_prefetch=2,
            grid=(batch, len(pairs)),
            in_specs=[qspec, kvspec,
                      pl.BlockSpec((1, PREP_ROWS, t), lambda b, s, qi, ki: (b, 0, ki[s]))],
            out_specs=pl.BlockSpec((t, WIDTH), lambda b, s, qi, ki: (b * nt + qi[s], 0)),
            scratch_shapes=[pltpu.VMEM((N_HEADS, t, 1), F32),
                            pltpu.VMEM((N_HEADS, t, 2 * HEAD_DIM), F32)]),
        out_shape=jax.ShapeDtypeStruct((n, WIDTH), BF16),
        compiler_params=_cparams(("arbitrary", "arbitrary")),
        name="fox_attention",
    )(qi_arr, ki_arr, proj, proj, prep_tr)


def _dsa_proj_kernel(cq_ref, g_ref, wq_ref, wi_ref, q_ref, qi_ref):
    x = cq_ref[...]
    ms = jnp.mean(x * x, axis=-1, keepdims=True)
    cb = (x * lax.rsqrt(ms + EPS) * g_ref[...]).astype(BF16)
    q_ref[...] = _dot(cb, wq_ref[...]).astype(BF16)
    qi_ref[...] = _dot(cb, wi_ref[...]).astype(BF16)


def dsa_proj(proj, cq_norm, w_uq, w_qidx, *, tm=512):
    n = proj.shape[0]
    r = DSA_Q_RANK
    wi = IDX_HEADS * IDX_DIM
    return pl.pallas_call(
        _dsa_proj_kernel,
        grid=(n // tm,),
        in_specs=[pl.BlockSpec((tm, r), lambda i: (i, C_DCQ // r)),
                  pl.BlockSpec((1, r), lambda i: (0, 0)),
                  pl.BlockSpec((r, WIDTH), lambda i: (0, 0)),
                  pl.BlockSpec((r, wi), lambda i: (0, 0))],
        out_specs=[pl.BlockSpec((tm, WIDTH), lambda i: (i, 0)),
                   pl.BlockSpec((tm, wi), lambda i: (i, 0))],
        out_shape=[jax.ShapeDtypeStruct((n, WIDTH), BF16),
                   jax.ShapeDtypeStruct((n, wi), BF16)],
        compiler_params=_cparams(("arbitrary",)),
        name="dsa_proj",
    )(proj, cq_norm.reshape(1, r), w_uq, w_qidx)


DSA_QB = 256
DSA_KC = 512
DSA_SCORE_MID_STEPS = 20
DSA_FEW_KEYS = 4
DSA_HALVE_FIXED = 13
DSA_WALK_FIXED = 3


def _score_to_key(s):
    b = pltpu.bitcast(s, I32)
    return b ^ ((b >> 31) & INT_MAX)


def _key_to_score(k):
    return pltpu.bitcast(k ^ ((k >> 31) & INT_MAX), F32)


def _t5_bucket(rel):
    max_exact = REL_BUCKETS // 2
    relf = jnp.maximum(rel, max_exact).astype(F32)
    large = max_exact + (jnp.log(relf / max_exact) / math.log(REL_MAX_DIST / max_exact)
                         * (REL_BUCKETS - max_exact)).astype(I32)
    large = jnp.minimum(large, REL_BUCKETS - 1)
    return jnp.where(rel < max_exact, rel, large)


def _dsa_kernel(rb_ref, q_ref, qi_ref, tok_ref, k_ref, v_ref, ka_ref, kb_ref, o_ref,
                key_scr, lg_scr, band_scr, kb16_scr, vt_scr, ka16_scr, kb16i_scr, *, seq, topk):
    qb_idx = pl.program_id(1)
    t0 = qb_idx * DSA_QB
    n_kc = (t0 + DSA_QB - 1) // DSA_KC + 1
    row_vec = (1, DSA_QB)

    @pl.when(qb_idx == 0)
    def _():
        kb16_scr[...] = k_ref[...].astype(BF16)
        ka16_scr[...] = ka_ref[...].astype(BF16)
        kb16i_scr[...] = kb_ref[...].astype(BF16)
        for c in range(seq // DSA_KC):
            cs = slice(c * DSA_KC, (c + 1) * DSA_KC)
            vt_scr[0:HEAD_DIM, cs] = v_ref[cs, :].T.astype(BF16)
        vt_scr[HEAD_DIM:, :] = jnp.ones((HEAD_DIM, seq), BF16)

    @pl.when((pl.program_id(0) == 0) & (qb_idx == 0))
    def _():
        j_ = lax.broadcasted_iota(I32, (2 * DSA_QB, DSA_QB), 0)
        i_ = lax.broadcasted_iota(I32, (2 * DSA_QB, DSA_QB), 1)
        rel = i_ + DSA_QB - j_
        bucket = _t5_bucket(rel)
        for h in range(N_HEADS):
            far = rb_ref[REL_BUCKETS - 1, h]
            band = jnp.zeros((2 * DSA_QB, DSA_QB), F32)
            for bk in range(REL_BUCKETS - 1):
                band = jnp.where(bucket == bk, (rb_ref[bk, h] - far) * LOG2E, band)
            band_scr[h] = jnp.where(rel >= 0, band, 0.0)

    w_t = tok_ref[...].T
    key_s = lax.broadcasted_iota(I32, (DSA_KC, DSA_QB), 0)
    row_t = t0 + lax.broadcasted_iota(I32, (DSA_KC, DSA_QB), 1)

    def score_chunk(c, carry):
        kmax, kmin = carry
        ks = pl.ds(pl.multiple_of(c * DSA_KC, DSA_KC), DSA_KC)
        ka = ka16_scr[ks, :]
        kb = kb16i_scr[ks, :]
        acc = jnp.zeros((DSA_KC, DSA_QB), F32)
        for p in range(IDX_HEADS // 2):
            qp = qi_ref[:, p * LANE:(p + 1) * LANE]
            acc = acc + jnp.maximum(_dot_nt(ka, qp), 0.0) * w_t[2 * p:2 * p + 1, :]
            acc = acc + jnp.maximum(_dot_nt(kb, qp), 0.0) * w_t[2 * p + 1:2 * p + 2, :]
        key = _score_to_key(acc)
        valid = (c * DSA_KC + key_s) <= row_t
        key_scr[ks, :] = jnp.where(valid, key, INT_MIN)
        kmax = jnp.maximum(kmax, jnp.max(jnp.where(valid, key, INT_MIN), axis=0, keepdims=True))
        kmin = jnp.minimum(kmin, jnp.min(jnp.where(valid, key, INT_MAX), axis=0, keepdims=True))
        return kmax, kmin

    kmax, kmin = lax.fori_loop(0, n_kc, score_chunk, (jnp.full(row_vec, INT_MIN, I32),
                                                     jnp.full(row_vec, INT_MAX, I32)))

    def scan_keys(cand, with_below):
        def body(c, carry):
            cnt, below = carry
            ks = pl.ds(pl.multiple_of(c * DSA_KC, DSA_KC), DSA_KC)
            keys = key_scr[ks, :]
            ge = keys >= cand
            ones = ge.astype(I32)
            low = jnp.where(ge, INT_MIN, keys)
            for u in range(DSA_KC // SUBLANE):
                us = slice(u * SUBLANE, (u + 1) * SUBLANE)
                cnt = cnt + ones[us, :]
                if with_below:
                    below = jnp.maximum(below, low[us, :])
            return cnt, below
        cnt, below = lax.fori_loop(0, n_kc, body, (jnp.zeros((SUBLANE, DSA_QB), I32),
                                                   jnp.full((SUBLANE, DSA_QB), INT_MIN, I32)))
        cnt = jnp.sum(cnt, axis=0, keepdims=True)
        if with_below:
            return cnt, jnp.max(below, axis=0, keepdims=True)
        return cnt

    def open_rows(lo, hi, c_lo):
        return (c_lo > topk) & (hi - 1 > lo)

    def any_row(flag):
        return jnp.max(jnp.where(flag, 1, 0))

    def update(cand, cnt, lo, hi, c_lo, c_hi):
        ge = cnt >= topk
        return (jnp.where(ge, cand, lo), jnp.where(ge, hi, cand),
                jnp.where(ge, cnt, c_lo), jnp.where(ge, c_hi, cnt))

    def crowded(lo, hi, c_lo, c_hi):
        return any_row(open_rows(lo, hi, c_lo) & (c_lo - c_hi > DSA_FEW_KEYS))

    def halve_step(it, lo, hi, c_lo, c_hi):
        key_mid = (lo >> 1) + (hi >> 1) + (lo & hi & 1)
        score_mid = _score_to_key(0.5 * _key_to_score(lo) + 0.5 * _key_to_score(hi - 1))
        cand = jnp.where(it < DSA_SCORE_MID_STEPS, score_mid, key_mid)
        cand = jnp.minimum(jnp.maximum(cand, lo + 1), hi - 1)
        cand = jnp.where(hi - 1 > lo, cand, lo)
        return update(cand, scan_keys(cand, False), lo, hi, c_lo, c_hi)

    def halve_body(st):
        it, _, lo, hi, c_lo, c_hi = st
        go = crowded(lo, hi, c_lo, c_hi)
        return (it + 1, go) + halve_step(it, lo, hi, c_lo, c_hi)

    def walk_step(lo, hi, c_lo, c_hi, nxt):
        is_open = open_rows(lo, hi, c_lo)
        cand = jnp.where(is_open, nxt, lo)
        cnt, below = scan_keys(cand, True)
        ge = cnt >= topk
        hi = jnp.where(is_open, jnp.where(ge, cand + 1, cand), hi)
        c_hi = jnp.where(is_open & jnp.logical_not(ge), cnt, c_hi)
        lo = jnp.where(is_open & ge, cand, lo)
        c_lo = jnp.where(is_open & ge, cnt, c_lo)
        nxt = jnp.where(ge, nxt, below)
        return lo, hi, c_lo, c_hi, nxt

    def walk_body(st):
        go = any_row(open_rows(st[1], st[2], st[3]))
        return (go,) + walk_step(*st[1:])

    n_valid = jnp.minimum(t0 + lax.broadcasted_iota(I32, row_vec, 1) + 1, seq)
    st = (kmin, kmax + 1, n_valid, jnp.zeros(row_vec, I32))
    st = lax.fori_loop(0, DSA_HALVE_FIXED, lambda it, s: halve_step(it, *s), st)
    st = lax.while_loop(lambda s: s[1] > 0, halve_body,
                        (jnp.int32(DSA_HALVE_FIXED), crowded(*st)) + st)[2:]
    _, nxt0 = scan_keys(st[1], True)
    st = lax.fori_loop(0, DSA_WALK_FIXED, lambda it, s: walk_step(*s), st + (nxt0,))
    _, thr, hi, n_ge, n_gt, _ = lax.while_loop(
        lambda s: s[0] > 0, walk_body, (any_row(open_rows(st[0], st[1], st[2])),) + st)

    tied = n_ge > topk
    has_tie = jnp.max(jnp.where(tied, 1, 0)) > 0

    @pl.when(has_tie)
    def _():
        room = (topk - n_gt).astype(F32)
        ii = lax.broadcasted_iota(I32, (LANE, LANE), 0)
        jj = lax.broadcasted_iota(I32, (LANE, LANE), 1)
        lower = (ii >= jj).astype(BF16)

        def body(c, seen):
            ks = pl.ds(pl.multiple_of(c * LANE, LANE), LANE)
            kk = key_scr[ks, :]
            eq = kk == thr
            rank = seen + _dot(lower, eq.astype(BF16))
            drop = eq & (rank > room) & tied
            key_scr[ks, :] = jnp.where(drop, INT_MIN, kk)
            return seen + jnp.sum(eq.astype(F32), axis=0, keepdims=True)
        lax.fori_loop(0, n_kc * (DSA_KC // LANE), body, jnp.zeros(row_vec, F32))

    def mask_chunk(c, carry):
        ks = pl.ds(pl.multiple_of(c * DSA_KC, DSA_KC), DSA_KC)
        sel = jnp.where(key_scr[ks, :] >= thr, 0.0, -jnp.inf).astype(F32)
        key_scr[ks, :] = pltpu.bitcast(sel, I32)
        return carry
    lax.fori_loop(0, n_kc, mask_chunk, 0)

    heads = range(N_HEADS)
    hsl = [slice(h * HEAD_DIM, (h + 1) * HEAD_DIM) for h in heads]

    def logit_chunk(c, ms):
        ks = pl.ds(pl.multiple_of(c * DSA_KC, DSA_KC), DSA_KC)
        k_c = kb16_scr[ks, :]
        sel = pltpu.bitcast(key_scr[ks, :], F32)
        out = []
        for h in heads:
            s = _dot_nt(k_c, q_ref[:, hsl[h]]) * (HEAD_DIM ** -0.5 * LOG2E) + sel
            lg_scr[h, ks, :] = s
            out.append(jnp.maximum(ms[h], jnp.max(s, axis=0, keepdims=True)))
        return tuple(out)
    ms = lax.fori_loop(0, n_kc, logit_chunk,
                       tuple(jnp.full(row_vec, -jnp.inf, F32) for _ in heads))

    band_off = pl.multiple_of(jnp.maximum(qb_idx - 1, 0) * DSA_QB, DSA_QB)
    ws = pl.ds(band_off, 2 * DSA_QB)
    ms = list(ms)
    for h in heads:
        band_h = band_scr[h]
        band_first = jnp.concatenate([band_h[DSA_QB:, :], jnp.zeros((DSA_QB, DSA_QB), F32)], axis=0)
        win = lg_scr[h, ws, :] + jnp.where(qb_idx == 0, band_first, band_h)
        lg_scr[h, ws, :] = win
        ms[h] = jnp.maximum(ms[h], jnp.max(win, axis=0, keepdims=True))

    def pv_chunk(c, accs):
        ks = pl.ds(pl.multiple_of(c * DSA_KC, DSA_KC), DSA_KC)
        vt_c = vt_scr[:, ks]
        return tuple(accs[h] + _dot(vt_c, jnp.exp2(lg_scr[h, ks, :] - ms[h]).astype(BF16))
                     for h in heads)
    accs = lax.fori_loop(0, n_kc, pv_chunk,
                         tuple(jnp.zeros((2 * HEAD_DIM, DSA_QB), F32) for _ in heads))
    for h in heads:
        o_ref[:, hsl[h]] = (accs[h][:HEAD_DIM, :] / accs[h][HEAD_DIM:, :]).T.astype(BF16)


def dsa_attention(proj, q, q_idx, prep_tok, rel_bias, batch, seq):
    n = proj.shape[0]
    nq = seq // DSA_QB
    topk = min(TOPK_MAX, seq // 4)
    wi = IDX_HEADS * IDX_DIM
    rowblk = lambda w, cb: pl.BlockSpec((DSA_QB, w), lambda b, i: (b * nq + i, cb))
    seqblk = lambda off: pl.BlockSpec((seq, LANE), lambda b, i: (b, off // LANE))
    return pl.pallas_call(
        functools.partial(_dsa_kernel, seq=seq, topk=topk),
        grid=(batch, nq),
        in_specs=[pl.BlockSpec(memory_space=pltpu.SMEM),
                  rowblk(WIDTH, 0), rowblk(wi, 0), rowblk(LANE, 0),
                  seqblk(C_DK), seqblk(C_DV), seqblk(C_KA), seqblk(C_KB)],
        out_specs=pl.BlockSpec((DSA_QB, WIDTH), lambda b, i: (b * nq + i, 0)),
        out_shape=jax.ShapeDtypeStruct((n, WIDTH), BF16),
        scratch_shapes=[pltpu.VMEM((seq, DSA_QB), I32),
                        pltpu.VMEM((N_HEADS, seq, DSA_QB), F32),
                        pltpu.VMEM((N_HEADS, 2 * DSA_QB, DSA_QB), F32),
                        pltpu.VMEM((seq, LANE), BF16),
                        pltpu.VMEM((2 * HEAD_DIM, seq), BF16),
                        pltpu.VMEM((seq, LANE), BF16),
                        pltpu.VMEM((seq, LANE), BF16)],
        compiler_params=_cparams(("arbitrary", "arbitrary")),
        name="dsa_attention",
    )(rel_bias, q, q_idx, prep_tok, proj, proj, proj, proj)


GDN_T = 256
GDN_GROUP = 2
GDN_HALO = 8


def _gdn_kernel(x_ref, cw_ref, ng_ref, tok_ref, tr_ref, o_ref,
                xq_scr, xk_scr, xv_scr, state_scr):
    first = pl.program_id(1) == 0

    @pl.when(first)
    def _():
        state_scr[...] = jnp.zeros_like(state_scr)
        for scr in (xq_scr, xk_scr, xv_scr):
            scr[0:GDN_HALO, :] = jnp.zeros((GDN_HALO, WIDTH), F32)

    def conv(col, scr, w_off):
        scr[GDN_HALO:, :] = x_ref[:, col - C_GQ:col - C_GQ + WIDTH]
        y = jnp.zeros((GDN_T, WIDTH), F32)
        for i in range(GDN_CONV):
            st = GDN_HALO - (GDN_CONV - 1) + i
            y = y + scr[st:st + GDN_T, :] * cw_ref[i:i + 1, w_off:w_off + WIDTH]
        scr[0:GDN_HALO, :] = scr[GDN_T:GDN_T + GDN_HALO, :]
        return _silu(y)

    qc = conv(C_GQ, xq_scr, 0)
    kc = conv(C_GK, xk_scr, WIDTH)
    vc = conv(C_GV, xv_scr, 2 * WIDTH)
    z_off = C_GZ - C_GQ
    tok = tok_ref[...]
    c = GDN_CHUNK
    heads = range(N_HEADS)
    hsl = [slice(h * HEAD_DIM, (h + 1) * HEAD_DIM) for h in heads]

    def l2norm_heads(x, scale):
        return jnp.concatenate(
            [x[:, s] * (lax.rsqrt(jnp.sum(x[:, s] * x[:, s], axis=-1, keepdims=True) + EPS) * scale)
             for s in hsl], axis=1)

    qf = l2norm_heads(qc, HEAD_DIM ** -0.5)
    kf = l2norm_heads(kc, 1.0)

    grp = GDN_GROUP
    gw = grp * HEAD_DIM
    nb = grp * c
    ri = lax.broadcasted_iota(I32, (nb, nb), 0)
    ci = lax.broadcasted_iota(I32, (nb, nb), 1)
    tril = ((ri // c) == (ci // c)) & (ri >= ci)
    eye = (ri == ci).astype(F32)
    pair_masks = []
    for lg in range(c.bit_length() - 1):
        pair_masks.append(((ri >> (lg + 1)) == (ci >> (lg + 1)))
                          & (((ri >> lg) & 1) == 1) & (((ci >> lg) & 1) == 0))
    lane_head = lax.broadcasted_iota(I32, (c, gw), 1) // HEAD_DIM
    row_head = lax.broadcasted_iota(I32, (nb, HEAD_DIM), 0) // c

    def spread(x):
        return jnp.concatenate([jnp.where(lane_head == u, x, 0.0) for u in range(grp)], axis=0)

    def stack(x):
        return jnp.concatenate([x[:, hsl[u]] for u in range(grp)], axis=0)

    def spread_lanes(x):
        return jnp.concatenate([jnp.where(row_head == u, x, 0.0) for u in range(grp)], axis=1)

    def split(x):
        hi = x.astype(BF16)
        return hi, (x - hi.astype(F32)).astype(BF16)

    def dot_split(a, b):
        return _dot(a[0], b[0]) + (_dot(a[0], b[1]) + _dot(a[1], b[0]))

    items = [(j, gi) for j in range(GDN_T // c) for gi in range(N_HEADS // grp)]
    pre = []
    for j, gi in items:
        rs = slice(j * c, (j + 1) * c)
        last = slice((j + 1) * c - 1, (j + 1) * c)
        gh = [gi * grp + u for u in range(grp)]
        gsl = slice(gi * gw, (gi + 1) * gw)
        qj, kj, vj = qf[rs, gsl], kf[rs, gsl], vc[rs, gsl]
        b_col = jnp.concatenate([tok[rs, L_GB + h:L_GB + h + 1] for h in gh], axis=0)
        g_col = jnp.concatenate([tok[rs, L_GA + h:L_GA + h + 1] for h in gh], axis=0)
        g_row = jnp.concatenate([tr_ref[0, L_GA + h:L_GA + h + 1, rs] for h in gh], axis=1)
        g_last = [tok[last, L_GA + h:L_GA + h + 1] for h in gh]
        g_last_col = jnp.concatenate([jnp.broadcast_to(g, (c, 1)) for g in g_last], axis=0)
        k_sp = spread(kj)
        q_sp = spread(qj)
        k_sp16 = k_sp.astype(BF16)
        decay = jnp.exp(jnp.where(tril, g_col - g_row, -jnp.inf))
        eg = jnp.exp(g_col)
        pre.append(dict(
            gh=gh, gsl=gsl,
            l_mat=b_col * _dot_nt(k_sp16, k_sp16) * decay,
            rhs=jnp.concatenate([stack(vj) * b_col, stack(kj) * (b_col * eg)], axis=1),
            qk=_dot_nt(q_sp.astype(BF16), k_sp16) * decay,
            q_dec=q_sp * eg,
            k_dec=k_sp * jnp.exp(g_last_col - g_col),
            e_last=jnp.concatenate([jnp.broadcast_to(jnp.exp(g), (HEAD_DIM, 1)) for g in g_last], axis=0)))

    t_inv = [eye - jnp.where(pair_masks[0], p["l_mat"], 0.0) for p in pre]
    for pm in pair_masks[1:]:
        t_s = [split(t) for t in t_inv]
        m_t = [dot_split(split(jnp.where(pm, p["l_mat"], 0.0)), ts) for p, ts in zip(pre, t_s)]
        t_inv = [t - dot_split(ts, split(m)) for t, ts, m in zip(t_inv, t_s, m_t)]
    sols = [dot_split(split(t), split(p["rhs"])) for t, p in zip(t_inv, pre)]

    outs = [[] for _ in heads]
    for p, sol in zip(pre, sols):
        u0 = sol[:, :HEAD_DIM]
        kcum = sol[:, HEAD_DIM:]
        st = state_scr[p["gsl"], :]
        stb = st.astype(BF16)
        v_new = u0 - _dot(spread_lanes(kcum).astype(BF16), stb)
        v_new_b = v_new.astype(BF16)
        o_st = _dot(p["q_dec"].astype(BF16), stb) + _dot(p["qk"].astype(BF16), v_new_b)
        state_scr[p["gsl"], :] = st * p["e_last"] + _dot(p["k_dec"].T.astype(BF16), v_new_b)
        for u, h in enumerate(p["gh"]):
            outs[h].append(o_st[u * c:(u + 1) * c, :])

    for h in heads:
        o = jnp.concatenate(outs[h], axis=0)
        ms = jnp.mean(o * o, axis=-1, keepdims=True)
        on = o * lax.rsqrt(ms + EPS) * ng_ref[...]
        z = x_ref[:, z_off + hsl[h].start:z_off + hsl[h].stop]
        o_ref[:, hsl[h]] = (on * _silu(z)).astype(BF16)


def gated_deltanet(proj, prep_tok, prep_tr, conv_w, norm_g, batch, seq):
    n = proj.shape[0]
    t = GDN_T
    nt = seq // t
    return pl.pallas_call(
        _gdn_kernel,
        grid=(batch, nt),
        in_specs=[pl.BlockSpec((t, 4 * WIDTH), lambda b, i: (b * nt + i, C_GQ // (4 * WIDTH))),
                  pl.BlockSpec((GDN_CONV, 3 * WIDTH), lambda b, i: (0, 0)),
                  pl.BlockSpec((1, HEAD_DIM), lambda b, i: (0, 0)),
                  pl.BlockSpec((t, LANE), lambda b, i: (b * nt + i, 0)),
                  pl.BlockSpec((1, PREP_ROWS, t), lambda b, i: (b, 0, i))],
        out_specs=pl.BlockSpec((t, WIDTH), lambda b, i: (b * nt + i, 0)),
        out_shape=jax.ShapeDtypeStruct((n, WIDTH), BF16),
        scratch_shapes=[pltpu.VMEM((t + GDN_HALO, WIDTH), F32),
                        pltpu.VMEM((t + GDN_HALO, WIDTH), F32),
                        pltpu.VMEM((t + GDN_HALO, WIDTH), F32),
                        pltpu.VMEM((N_HEADS * HEAD_DIM, HEAD_DIM), F32)],
        compiler_params=_cparams(("arbitrary", "arbitrary")),
        name="gated_deltanet",
    )(proj, conv_w, norm_g.reshape(1, HEAD_DIM), prep_tok, prep_tr)


def _merge_kernel(h_ref, b0_ref, b1_ref, b2_ref, b3_ref, g0_ref, g1_ref, g2_ref, g3_ref,
                  wb_ref, o_ref):
    h = h_ref[...]
    acc = None
    for n, (b_ref, g_ref) in enumerate(zip((b0_ref, b1_ref, b2_ref, b3_ref),
                                           (g0_ref, g1_ref, g2_ref, g3_ref))):
        gate = jax.nn.sigmoid(_dot(h, g_ref[...]))
        term = gate * _dot(b_ref[...], wb_ref[n])
        acc = term if acc is None else acc + term
    o_ref[...] = acc.astype(BF16)


def merge_branches(h, branches, w_gate, w_branch, *, tm=512, tn=512):
    _, n, d = h.shape
    nj = d // tn
    bspec = pl.BlockSpec((tm, WIDTH), lambda j, i: (i, 0))
    gspec = lambda k: pl.BlockSpec((d, tn), lambda j, i: (0, k * nj + j))
    return pl.pallas_call(
        _merge_kernel,
        grid=(nj, n // tm),
        in_specs=[pl.BlockSpec((None, tm, d), lambda j, i: (0, i, 0)),
                  bspec, bspec, bspec, bspec,
                  gspec(0), gspec(1), gspec(2), gspec(3),
                  pl.BlockSpec((N_BRANCH, WIDTH, tn), lambda j, i: (0, 0, j))],
        out_specs=pl.BlockSpec((tm, tn), lambda j, i: (i, j)),
        out_shape=jax.ShapeDtypeStruct((n, d), BF16),
        compiler_params=_cparams(("arbitrary", "arbitrary")),
        name="merge_branches",
    )(h, *branches, w_gate, w_gate, w_gate, w_gate, w_branch)


def _resid_mm_kernel(a_ref, w_ref, x_ref, o_ref):
    o_ref[...] = x_ref[...] + _dot(a_ref[...], w_ref[...])


def resid_matmul(a, w, x, *, tm=512, tn=1024, name="resid_matmul"):
    n, k = a.shape
    d = w.shape[1]
    return pl.pallas_call(
        _resid_mm_kernel,
        grid=(d // tn, n // tm),
        in_specs=[pl.BlockSpec((tm, k), lambda j, i: (i, 0)),
                  pl.BlockSpec((k, tn), lambda j, i: (0, j)),
                  pl.BlockSpec((tm, tn), lambda j, i: (i, j))],
        out_specs=pl.BlockSpec((tm, tn), lambda j, i: (i, j)),
        out_shape=jax.ShapeDtypeStruct((n, d), F32),
        compiler_params=_cparams(("arbitrary", "arbitrary")),
        name=name,
    )(a, w, x)


FFN_HALO = 8
FFN_SUB = 512


def _ffn1_kernel(x_ref, g_ref, wg_ref, wu_ref, cw_ref, cb_ref, o_ref, h_scr, gt_scr, halo_scr,
                 *, tm, tiles_per_seq):
    i = pl.program_id(1)

    @pl.when(i == 0)
    def _():
        halo_scr[...] = jnp.zeros_like(halo_scr)

    def body(r, carry):
        rows = pl.ds(pl.multiple_of(r * NORM_ROWS, NORM_ROWS), NORM_ROWS)
        h_scr[rows, :] = _rmsnorm_rows(x_ref, g_ref, rows).astype(BF16)
        return carry
    lax.fori_loop(0, tm // NORM_ROWS, body, 0)

    h = h_scr[...]
    seq_start = (i % tiles_per_seq) == 0
    tn = o_ref.shape[1]
    for off in range(0, tn, FFN_SUB):
        cs = slice(off, min(off + FFN_SUB, tn))
        g = _dot(h, wg_ref[:, cs])
        gt_scr[FFN_HALO:, cs] = g
        gt_scr[0:FFN_HALO, cs] = jnp.where(seq_start, 0.0, halo_scr[:, cs])
        halo_scr[:, cs] = g[tm - FFN_HALO:, :]
        y = cb_ref[:, cs] + g * cw_ref[FFN_CONV - 1:FFN_CONV, cs]
        for t in range(FFN_CONV - 1):
            st = FFN_HALO - (FFN_CONV - 1) + t
            y = y + gt_scr[st:st + tm, cs] * cw_ref[t:t + 1, cs]
        o_ref[:, cs] = (_silu(y) * _dot(h, wu_ref[:, cs])).astype(BF16)


def conv_ffn_up(x, gain, w_gate, w_up, conv_w, conv_b, seq, *, tm=512, col_parts=2):
    n, d = x.shape
    f = w_gate.shape[1]
    tn = f // col_parts
    wspec = pl.BlockSpec((d, tn), lambda j, i: (0, j), pipeline_mode=pl.Buffered(1))
    return pl.pallas_call(
        functools.partial(_ffn1_kernel, tm=tm, tiles_per_seq=seq // tm),
        grid=(col_parts, n // tm),
        in_specs=[pl.BlockSpec((tm, d), lambda j, i: (i, 0)),
                  pl.BlockSpec((1, d), lambda j, i: (0, 0)),
                  wspec, wspec,
                  pl.BlockSpec((FFN_CONV, tn), lambda j, i: (0, j)),
                  pl.BlockSpec((1, tn), lambda j, i: (0, j))],
        out_specs=pl.BlockSpec((tm, tn), lambda j, i: (i, j)),
        out_shape=jax.ShapeDtypeStruct((n, f), BF16),
        scratch_shapes=[pltpu.VMEM((tm, d), BF16),
                        pltpu.VMEM((tm + FFN_HALO, tn), F32),
                        pltpu.VMEM((FFN_HALO, tn), F32)],
        compiler_params=_cparams(("arbitrary", "arbitrary")),
        name="conv_ffn_up",
    )(x, gain.reshape(1, d), w_gate, w_up, conv_w, conv_b.reshape(1, f))


IN_SIZES = (WIDTH, WIDTH, WIDTH, WIDTH,
            DSA_Q_RANK, HEAD_DIM, HEAD_DIM, IDX_DIM, IDX_HEADS,
            WIDTH, WIDTH, WIDTH, N_HEADS,
            WIDTH, WIDTH, WIDTH, WIDTH, N_HEADS, N_HEADS)
IN_NAMES = ("r_q", "r_k", "r_v", "r_g", "d_cq", "d_k", "d_v", "i_k", "i_w",
            "f_q", "f_k", "f_v", "f_f", "g_q", "g_k", "g_v", "g_z", "g_b", "g_a")
IN_PLAN = (("r_q", C_RQ), ("r_k", C_RK), ("r_v", C_RV), ("r_g", C_RG),
           ("f_q", C_FQ), ("f_k", C_FK), ("f_v", C_FV),
           ("g_q", C_GQ), ("g_k", C_GK), ("g_v", C_GV), ("g_z", C_GZ),
           ("d_k", C_DK), ("d_cq", C_DCQ), ("d_v", C_DV),
           ("i_k", C_KA), ("i_k", C_KB + IDX_DIM),
           ("i_w", C_SM + L_IW), ("f_f", C_SM + L_FF), ("g_b", C_SM + L_GB), ("g_a", C_SM + L_GA))


def _prep_w_in_kernel(w_ref, m_ref, g_ref):
    src = {}
    off = 0
    for name, size in zip(IN_NAMES, IN_SIZES):
        src[name] = (off, size)
        off += size
    m_ref[...] = jnp.zeros_like(m_ref)
    for name, dst in IN_PLAN:
        so, w = src[name]
        m_ref[:, dst:dst + w] = w_ref[:, so:so + w].astype(BF16)
    g_ref[...] = w_ref[:, off:off + g_ref.shape[1]].astype(BF16)


def prep_w_in(w_in, layer, *, tr=128):
    _, d, c = w_in.shape
    return pl.pallas_call(
        _prep_w_in_kernel,
        grid=(d // tr,),
        in_specs=[pl.BlockSpec((None, tr, c), lambda i: (layer, i, 0))],
        out_specs=[pl.BlockSpec((tr, C_TOT), lambda i: (i, 0)),
                   pl.BlockSpec((tr, N_BRANCH * d), lambda i: (i, 0))],
        out_shape=[jax.ShapeDtypeStruct((d, C_TOT), BF16),
                   jax.ShapeDtypeStruct((d, N_BRANCH * d), BF16)],
        compiler_params=_cparams(("arbitrary",)),
        name="prep_w_in",
    )(w_in)


def _cast_kernel(w_ref, o_ref):
    o_ref[...] = w_ref[...].astype(BF16)


def cast_layer(w, layer, *, tr=256):
    _, r, c = w.shape
    if r % tr:
        tr = r
    return pl.pallas_call(
        _cast_kernel,
        grid=(r // tr,),
        in_specs=[pl.BlockSpec((None, tr, c), lambda i: (layer, i, 0))],
        out_specs=pl.BlockSpec((tr, c), lambda i: (i, 0)),
        out_shape=jax.ShapeDtypeStruct((r, c), BF16),
        compiler_params=_cparams(("arbitrary",)),
        name="cast_bf16",
    )(w)


def cast_branch(w_branch, layer):
    _, nbr, r, c = w_branch.shape
    return pl.pallas_call(
        _cast_kernel,
        grid=(nbr,),
        in_specs=[pl.BlockSpec((None, None, r, c), lambda i: (layer, i, 0, 0))],
        out_specs=pl.BlockSpec((None, r, c), lambda i: (i, 0, 0)),
        out_shape=jax.ShapeDtypeStruct((nbr, r, c), BF16),
        compiler_params=_cparams(("arbitrary",)),
        name="cast_branch",
    )(w_branch)


def kernel(x, norm_mix, w_in, dsa_cq_norm, dsa_w_uq, dsa_w_qidx, fox_f_bias, gdn_conv, gdn_a_log,
           gdn_dt_bias, gdn_norm, w_branch, w_out, rel_bias, norm_ffn, ffn_w_gate, ffn_w_up,
           ffn_conv, ffn_conv_b, ffn_w_down, final_norm):
    batch, seq, d = x.shape
    depth = w_in.shape[0]
    xf = x.reshape(batch * seq, d)
    ret_tables = _retention_tables(seq)
    for l in range(depth):
        w_main, w_gate = prep_w_in(w_in, l)
        proj, h = norm_proj(xf, norm_mix[l], w_main)
        par = jnp.zeros((SUBLANE, LANE), F32)
        par = par.at[0, L_FF:L_FF + N_HEADS].set(fox_f_bias[l])
        par = par.at[0, L_GA:L_GA + N_HEADS].set(gdn_dt_bias[l])
        par = par.at[1, L_GA:L_GA + N_HEADS].set(gdn_a_log[l])
        prep_tok, prep_tr = prep_small(proj, par, batch, seq)
        o_ret = retention(proj, ret_tables, batch, seq)
        q_dsa, q_idx = dsa_proj(proj, dsa_cq_norm[l], cast_layer(dsa_w_uq, l), cast_layer(dsa_w_qidx, l))
        o_dsa = dsa_attention(proj, q_dsa, q_idx, prep_tok, rel_bias, batch, seq)
        o_fox = fox_attention(proj, prep_tr, batch, seq)
        o_gdn = gated_deltanet(proj, prep_tok, prep_tr, gdn_conv[l], gdn_norm[l], batch, seq)
        merged = merge_branches(h, (o_ret, o_dsa, o_fox, o_gdn), w_gate, cast_branch(w_branch, l))
        xf = resid_matmul(merged, cast_layer(w_out, l), xf, tn=d, name="out_proj")
        act = conv_ffn_up(xf, norm_ffn[l], cast_layer(ffn_w_gate, l), cast_layer(ffn_w_up, l),
                          ffn_conv[l], ffn_conv_b[l], seq)
        xf = resid_matmul(act, cast_layer(ffn_w_down, l), xf, name="ffn_down")
    return rmsnorm(xf, final_norm).reshape(batch, seq, d)
```

```python
import functools
import math

import jax
import jax.numpy as jnp
from jax import lax
from jax.experimental import pallas as pl
from jax.experimental.pallas import tpu as pltpu

F32 = jnp.float32
BF16 = jnp.bfloat16
I32 = jnp.int32

HEAD_DIM = 128
N_HEADS = 4
WIDTH = N_HEADS * HEAD_DIM
N_BRANCH = 4
RET_CHUNK = 128
ROPE_BASE = 10000.0
DSA_Q_RANK = 384
IDX_HEADS = 16
IDX_DIM = 64
TOPK_MAX = 256
GDN_CONV = 4
GDN_CHUNK = 64
REL_BUCKETS = 32
REL_MAX_DIST = 128
FFN_CONV = 3
EPS = 1e-6

LANE = 128
SUBLANE = 8
VMEM_LIMIT = 56 * 1024 * 1024

C_RQ, C_RK, C_RV, C_RG = 0, 512, 1024, 1536
C_FQ, C_DK, C_DCQ = 2048, 2560, 2688
C_FK, C_FV = 3072, 3584
C_GQ, C_GK, C_GV, C_GZ = 4096, 4608, 5120, 5632
C_DV, C_KA, C_KB, C_SM = 6144, 6272, 6400, 6528
C_TOT = 6656
L_IW, L_FF, L_GB, L_GA = 0, 16, 20, 24

LOG2E = 1.4426950408889634
INT_MIN = -(2 ** 31)
INT_MAX = 2 ** 31 - 1
HIGHEST = lax.Precision.HIGHEST


def _cparams(sem, vmem=VMEM_LIMIT):
    return pltpu.CompilerParams(dimension_semantics=sem, vmem_limit_bytes=vmem)


def _dot(a, b):
    return jnp.dot(a, b, preferred_element_type=F32)


def _dot_nt(a, b):
    return lax.dot_general(a, b, (((1,), (1,)), ((), ())), preferred_element_type=F32)


def _silu(x):
    return x * jax.nn.sigmoid(x)


NORM_ROWS = 128


def _rmsnorm_rows(x_ref, g_ref, rows):
    x = x_ref[rows, :]
    ms = jnp.mean(x * x, axis=-1, keepdims=True)
    return x * lax.rsqrt(ms + EPS) * g_ref[...]


PROJ_SUB = 512


def _norm_proj_kernel(x_ref, g_ref, w_ref, o_ref, h_ref, h_scr, *, tm):
    def body(r, carry):
        rows = pl.ds(pl.multiple_of(r * NORM_ROWS, NORM_ROWS), NORM_ROWS)
        hb = _rmsnorm_rows(x_ref, g_ref, rows).astype(BF16)
        h_scr[rows, :] = hb
        h_ref[rows, :] = hb
        return carry
    lax.fori_loop(0, tm // NORM_ROWS, body, 0)

    h = h_scr[...]
    tn = o_ref.shape[1]
    for off in range(0, tn, PROJ_SUB):
        cs = slice(off, min(off + PROJ_SUB, tn))
        o_ref[:, cs] = _dot(h, w_ref[:, cs])


def norm_proj(x, gain, w, *, tm=512, col_parts=2):
    n, d = x.shape
    c = w.shape[1]
    tn = c // col_parts
    return pl.pallas_call(
        functools.partial(_norm_proj_kernel, tm=tm),
        grid=(col_parts, n // tm),
        in_specs=[pl.BlockSpec((tm, d), lambda j, i: (i, 0)),
                  pl.BlockSpec((1, d), lambda j, i: (0, 0)),
                  pl.BlockSpec((d, tn), lambda j, i: (0, j), pipeline_mode=pl.Buffered(1))],
        out_specs=[pl.BlockSpec((tm, tn), lambda j, i: (i, j)),
                   pl.BlockSpec((None, tm, d), lambda j, i: (j, i, 0))],
        out_shape=[jax.ShapeDtypeStruct((n, c), F32),
                   jax.ShapeDtypeStruct((col_parts, n, d), BF16)],
        scratch_shapes=[pltpu.VMEM((tm, d), BF16)],
        compiler_params=_cparams(("arbitrary", "arbitrary")),
        name="norm_proj",
    )(x, gain.reshape(1, d), w)


def _rmsnorm_kernel(x_ref, g_ref, o_ref, *, tm):
    def body(r, carry):
        rows = pl.ds(pl.multiple_of(r * NORM_ROWS, NORM_ROWS), NORM_ROWS)
        o_ref[rows, :] = _rmsnorm_rows(x_ref, g_ref, rows)
        return carry
    lax.fori_loop(0, tm // NORM_ROWS, body, 0)


def rmsnorm(x, gain, *, tm=512):
    n, d = x.shape
    return pl.pallas_call(
        functools.partial(_rmsnorm_kernel, tm=tm),
        grid=(n // tm,),
        in_specs=[pl.BlockSpec((tm, d), lambda i: (i, 0)),
                  pl.BlockSpec((1, d), lambda i: (0, 0))],
        out_specs=pl.BlockSpec((tm, d), lambda i: (i, 0)),
        out_shape=jax.ShapeDtypeStruct((n, d), F32),
        compiler_params=_cparams(("arbitrary",)),
        name="final_rmsnorm",
    )(x, gain.reshape(1, d))


def _prep_kernel(s_ref, par_ref, tok_ref, tr_ref, carry_scr):
    @pl.when(pl.program_id(1) == 0)
    def _():
        carry_scr[...] = jnp.zeros_like(carry_scr)

    s = s_ref[...]
    lane = lax.broadcasted_iota(I32, (LANE, LANE), 1)
    row = lax.broadcasted_iota(I32, (LANE, LANE), 0)
    z = s + par_ref[0:1, :]
    soft = jnp.maximum(z, 0.0) + jnp.log1p(jnp.exp(-jnp.abs(z)))
    log_sig = z - soft
    sig = jax.nn.sigmoid(z)
    g_val = -jnp.exp(par_ref[1:2, :]) * soft
    is_f = (lane[0:1] >= L_FF) & (lane[0:1] < L_FF + N_HEADS)
    is_b = (lane[0:1] >= L_GB) & (lane[0:1] < L_GB + N_HEADS)
    is_a = (lane[0:1] >= L_GA) & (lane[0:1] < L_GA + N_HEADS)
    pre = jnp.where(is_f, log_sig, jnp.where(is_a, g_val, 0.0))
    tri = (row >= lane).astype(F32)
    tri_blk = ((row >= lane) & ((row // GDN_CHUNK) == (lane // GDN_CHUNK))).astype(F32)
    subs = [slice(u * LANE, (u + 1) * LANE) for u in range(PREP_T // LANE)]
    cum_full = [jnp.dot(tri, pre[u], precision=HIGHEST, preferred_element_type=F32) for u in subs]
    cum_blk = [jnp.dot(tri_blk, pre[u], precision=HIGHEST, preferred_element_type=F32) for u in subs]
    scale_iw = IDX_HEADS ** -0.5 * IDX_DIM ** -0.5
    carry = carry_scr[0:1, :]
    for u, cf, cb in zip(subs, cum_full, cum_blk):
        c_fox = cf + carry
        carry = c_fox[LANE - 1:LANE, :]
        out = jnp.where(is_f, c_fox,
                        jnp.where(is_a, cb,
                                  jnp.where(is_b, sig[u],
                                            jnp.where(lane[0:1] < IDX_HEADS, s[u] * scale_iw, 0.0))))
        tok_ref[u, :] = out
        tr_ref[0, :, u] = out.T[0:PREP_ROWS, :]
    carry_scr[0:1, :] = carry


PREP_T = 512
PREP_ROWS = 32


def prep_small(proj, par, batch, seq):
    n = proj.shape[0]
    nc = seq // PREP_T
    return pl.pallas_call(
        _prep_kernel,
        grid=(batch, nc),
        in_specs=[pl.BlockSpec((PREP_T, LANE), lambda b, c: (b * nc + c, C_SM // LANE)),
                  pl.BlockSpec((SUBLANE, LANE), lambda b, c: (0, 0))],
        out_specs=[pl.BlockSpec((PREP_T, LANE), lambda b, c: (b * nc + c, 0)),
                   pl.BlockSpec((1, PREP_ROWS, PREP_T), lambda b, c: (b, 0, c))],
        out_shape=[jax.ShapeDtypeStruct((n, LANE), F32),
                   jax.ShapeDtypeStruct((batch, PREP_ROWS, seq), F32)],
        scratch_shapes=[pltpu.VMEM((SUBLANE, LANE), F32)],
        compiler_params=_cparams(("arbitrary", "arbitrary")),
        name="prep_small",
    )(proj, par)


def _ret_gamma():
    return [math.log1p(-(2.0 ** (-5.0 - h))) for h in range(N_HEADS)]


def _retention_kernel(x_ref, cos_ref, sin_ref, dec_ref, zeta_ref, xi_ref, o_ref, state_scr):
    @pl.when(pl.program_id(1) == 0)
    def _():
        state_scr[...] = jnp.zeros_like(state_scr)

    cos_t = cos_ref[...]
    sin_t = sin_ref[...]
    log_gamma = _ret_gamma()
    heads = range(N_HEADS)
    hsl = [slice(h * HEAD_DIM, (h + 1) * HEAD_DIM) for h in heads]

    def part(col):
        return [slice(col - C_RQ + s.start, col - C_RQ + s.stop) for s in hsl]

    def rope(x):
        return x * cos_t + pltpu.roll(x, HEAD_DIM // 2, 1) * sin_t

    qb = [rope(x_ref[:, s]).astype(BF16) for s in part(C_RQ)]
    kr = [rope(x_ref[:, s]) * (HEAD_DIM ** -0.5) for s in part(C_RK)]
    kb = [x.astype(BF16) for x in kr]
    vb = [x_ref[:, s].astype(BF16) for s in part(C_RV)]
    gsl = part(C_RG)
    st = [state_scr[h] for h in heads]
    inner = [(_dot_nt(qb[h], kb[h]) * dec_ref[h]).astype(BF16) for h in heads]
    cross = [_dot(qb[h], st[h].astype(BF16)) * xi_ref[h] for h in heads]
    kv = [_dot((kr[h] * zeta_ref[h]).T.astype(BF16), vb[h]) for h in heads]
    o = [_dot(inner[h], vb[h]) + cross[h] for h in heads]
    for h in heads:
        state_scr[h] = st[h] * math.exp(log_gamma[h] * RET_CHUNK) + kv[h]
        mu = jnp.mean(o[h], axis=-1, keepdims=True)
        oc = o[h] - mu
        var = jnp.mean(oc * oc, axis=-1, keepdims=True)
        o_ref[:, hsl[h]] = (_silu(x_ref[:, gsl[h]]) * (oc * lax.rsqrt(var + EPS))).astype(BF16)


def _retention_tables(seq):
    half = HEAD_DIM // 2
    inv = 1.0 / (ROPE_BASE ** (jnp.arange(half, dtype=F32) / half))
    ang = jnp.arange(seq).astype(F32)[:, None] * inv[None, :]
    cos, sin = jnp.cos(ang), jnp.sin(ang)
    cos_t = jnp.concatenate([cos, cos], axis=-1)
    sin_t = jnp.concatenate([-sin, sin], axis=-1)
    c = RET_CHUNK
    log_gamma = jnp.log1p(-jnp.exp2(-5.0 - jnp.arange(N_HEADS, dtype=F32)))
    n = jnp.arange(c, dtype=F32)
    diff = n[:, None] - n[None, :]
    decay = jnp.where(diff >= 0, jnp.exp(log_gamma[:, None, None] * jnp.maximum(diff, 0.0)), 0.0)
    zeta = jnp.exp(log_gamma[:, None] * (c - 1 - n)[None, :])
    xi = jnp.exp(log_gamma[:, None] * (n + 1)[None, :])
    ones = jnp.ones((1, 1, HEAD_DIM), F32)
    return cos_t, sin_t, decay, zeta[:, :, None] * ones, xi[:, :, None] * ones


def retention(proj, tables, batch, seq):
    n = proj.shape[0]
    c = RET_CHUNK
    nc = seq // c
    cos_t, sin_t, decay, zeta, xi = tables
    full3 = pl.BlockSpec((N_HEADS, c, HEAD_DIM), lambda b, i: (0, 0, 0))
    return pl.pallas_call(
        _retention_kernel,
        grid=(batch, nc),
        in_specs=[pl.BlockSpec((c, 4 * WIDTH), lambda b, i: (b * nc + i, C_RQ // (4 * WIDTH))),
                  pl.BlockSpec((c, HEAD_DIM), lambda b, i: (i, 0)),
                  pl.BlockSpec((c, HEAD_DIM), lambda b, i: (i, 0)),
                  full3, full3, full3],
        out_specs=pl.BlockSpec((c, WIDTH), lambda b, i: (b * nc + i, 0)),
        out_shape=jax.ShapeDtypeStruct((n, WIDTH), BF16),
        scratch_shapes=[pltpu.VMEM((N_HEADS, HEAD_DIM, HEAD_DIM), F32)],
        compiler_params=_cparams(("arbitrary", "arbitrary")),
        name="retention",
    )(proj, cos_t, sin_t, decay, zeta, xi)


def _fox_kernel(qi_ref, ki_ref, q_ref, kv_ref, ctr_ref, o_ref, m_scr, acc_scr, *, t):
    qi = qi_ref[pl.program_id(1)]
    ki = ki_ref[pl.program_id(1)]

    @pl.when(ki == 0)
    def _():
        m_scr[...] = jnp.full_like(m_scr, -jnp.inf)
        acc_scr[...] = jnp.zeros_like(acc_scr)

    def step(masked):
        if masked:
            row = lax.broadcasted_iota(I32, (t, t), 0)
            colm = lax.broadcasted_iota(I32, (t, t), 1)
            keep = row >= colm
        ones = jnp.ones((t, HEAD_DIM), BF16)
        for h in range(N_HEADS):
            sl = slice(h * HEAD_DIM, (h + 1) * HEAD_DIM)
            vsl = slice(WIDTH + h * HEAD_DIM, WIDTH + (h + 1) * HEAD_DIM)
            qb = q_ref[:, sl].astype(BF16)
            kb = kv_ref[:, sl].astype(BF16)
            c_k = ctr_ref[0, L_FF + h:L_FF + h + 1, :] * LOG2E
            s = _dot_nt(qb, kb) * (HEAD_DIM ** -0.5 * LOG2E) - c_k
            if masked:
                s = jnp.where(keep, s, -jnp.inf)
            m_old = m_scr[h]
            m_new = jnp.maximum(m_old, jnp.max(s, axis=-1, keepdims=True))
            alpha = jnp.exp2(m_old - m_new)
            p = jnp.exp2(s - m_new)
            v_aug = jnp.concatenate([kv_ref[:, vsl].astype(BF16), ones], axis=1)
            acc_scr[h] = alpha * acc_scr[h] + _dot(p.astype(BF16), v_aug)
            m_scr[h] = m_new

    @pl.when(ki < qi)
    def _():
        step(False)

    @pl.when(ki == qi)
    def _():
        step(True)
        for h in range(N_HEADS):
            sl = slice(h * HEAD_DIM, (h + 1) * HEAD_DIM)
            acc = acc_scr[h]
            o_ref[:, sl] = (acc[:, :HEAD_DIM] / acc[:, HEAD_DIM:]).astype(BF16)


def fox_attention(proj, prep_tr, batch, seq, *, t=1024):
    n = proj.shape[0]
    nt = seq // t
    pairs = [(qi, ki) for qi in range(nt) for ki in range(qi + 1)]
    qi_arr = jnp.asarray([p[0] for p in pairs], I32)
    ki_arr = jnp.asarray([p[1] for p in pairs], I32)
    qspec = pl.BlockSpec((t, WIDTH), lambda b, s, qi, ki: (b * nt + qi[s], C_FQ // WIDTH))
    kvspec = pl.BlockSpec((t, 2 * WIDTH), lambda b, s, qi, ki: (b * nt + ki[s], C_FK // (2 * WIDTH)))
    return pl.pallas_call(
        functools.partial(_fox_kernel, t=t),
        grid_spec=pltpu.PrefetchScalarGridSpec(
            num_scalar_prefetch=2,
            grid=(batch, len(pairs)),
            in_specs=[qspec, kvspec,
                      pl.BlockSpec((1, PREP_ROWS, t), lambda b, s, qi, ki: (b, 0, ki[s]))],
            out_specs=pl.BlockSpec((t, WIDTH), lambda b, s, qi, ki: (b * nt + qi[s], 0)),
            scratch_shapes=[pltpu.VMEM((N_HEADS, t, 1), F32),
                            pltpu.VMEM((N_HEADS, t, 2 * HEAD_DIM), F32)]),
        out_shape=jax.ShapeDtypeStruct((n, WIDTH), BF16),
        compiler_params=_cparams(("arbitrary", "arbitrary")),
        name="fox_attention",
    )(qi_arr, ki_arr, proj, proj, prep_tr)


def _dsa_proj_kernel(cq_ref, g_ref, wq_ref, wi_ref, q_ref, qi_ref):
    x = cq_ref[...]
    ms = jnp.mean(x * x, axis=-1, keepdims=True)
    cb = (x * lax.rsqrt(ms + EPS) * g_ref[...]).astype(BF16)
    q_ref[...] = _dot(cb, wq_ref[...]).astype(BF16)
    qi_ref[...] = _dot(cb, wi_ref[...]).astype(BF16)


def dsa_proj(proj, cq_norm, w_uq, w_qidx, *, tm=512):
    n = proj.shape[0]
    r = DSA_Q_RANK
    wi = IDX_HEADS * IDX_DIM
    return pl.pallas_call(
        _dsa_proj_kernel,
        grid=(n // tm,),
        in_specs=[pl.BlockSpec((tm, r), lambda i: (i, C_DCQ // r)),
                  pl.BlockSpec((1, r), lambda i: (0, 0)),
                  pl.BlockSpec((r, WIDTH), lambda i: (0, 0)),
                  pl.BlockSpec((r, wi), lambda i: (0, 0))],
        out_specs=[pl.BlockSpec((tm, WIDTH), lambda i: (i, 0)),
                   pl.BlockSpec((tm, wi), lambda i: (i, 0))],
        out_shape=[jax.ShapeDtypeStruct((n, WIDTH), BF16),
                   jax.ShapeDtypeStruct((n, wi), BF16)],
        compiler_params=_cparams(("arbitrary",)),
        name="dsa_proj",
    )(proj, cq_norm.reshape(1, r), w_uq, w_qidx)


DSA_QB = 256
DSA_KC = 512
DSA_SCORE_MID_STEPS = 20
DSA_FEW_KEYS = 4
DSA_HALVE_FIXED = 12
DSA_WALK_FIXED = 3


def _score_to_key(s):
    b = pltpu.bitcast(s, I32)
    return b ^ ((b >> 31) & INT_MAX)


def _key_to_score(k):
    return pltpu.bitcast(k ^ ((k >> 31) & INT_MAX), F32)


def _t5_bucket(rel):
    max_exact = REL_BUCKETS // 2
    relf = jnp.maximum(rel, max_exact).astype(F32)
    large = max_exact + (jnp.log(relf / max_exact) / math.log(REL_MAX_DIST / max_exact)
                         * (REL_BUCKETS - max_exact)).astype(I32)
    large = jnp.minimum(large, REL_BUCKETS - 1)
    return jnp.where(rel < max_exact, rel, large)


def _dsa_kernel(rb_ref, q_ref, qi_ref, tok_ref, k_ref, v_ref, ka_ref, kb_ref, o_ref,
                key_scr, lg_scr, band_scr, kb16_scr, vt_scr, ka16_scr, kb16i_scr, *, seq, topk):
    qb_idx = pl.program_id(1)
    t0 = qb_idx * DSA_QB
    n_kc = (t0 + DSA_QB - 1) // DSA_KC + 1
    row_vec = (1, DSA_QB)

    @pl.when(qb_idx == 0)
    def _():
        kb16_scr[...] = k_ref[...].astype(BF16)
        ka16_scr[...] = ka_ref[...].astype(BF16)
        kb16i_scr[...] = kb_ref[...].astype(BF16)
        for c in range(seq // DSA_KC):
            cs = slice(c * DSA_KC, (c + 1) * DSA_KC)
            vt_scr[0:HEAD_DIM, cs] = v_ref[cs, :].T.astype(BF16)
        vt_scr[HEAD_DIM:, :] = jnp.ones((HEAD_DIM, seq), BF16)

    @pl.when((pl.program_id(0) == 0) & (qb_idx == 0))
    def _():
        j_ = lax.broadcasted_iota(I32, (2 * DSA_QB, DSA_QB), 0)
        i_ = lax.broadcasted_iota(I32, (2 * DSA_QB, DSA_QB), 1)
        rel = i_ + DSA_QB - j_
        bucket = _t5_bucket(rel)
        for h in range(N_HEADS):
            far = rb_ref[REL_BUCKETS - 1, h]
            band = jnp.zeros((2 * DSA_QB, DSA_QB), F32)
            for bk in range(REL_BUCKETS - 1):
                band = jnp.where(bucket == bk, (rb_ref[bk, h] - far) * LOG2E, band)
            band_scr[h] = jnp.where(rel >= 0, band, 0.0)

    w_t = tok_ref[...].T
    key_s = lax.broadcasted_iota(I32, (DSA_KC, DSA_QB), 0)
    row_t = t0 + lax.broadcasted_iota(I32, (DSA_KC, DSA_QB), 1)

    def score_chunk(c, carry):
        kmax, kmin = carry
        ks = pl.ds(pl.multiple_of(c * DSA_KC, DSA_KC), DSA_KC)
        ka = ka16_scr[ks, :]
        kb = kb16i_scr[ks, :]
        acc = jnp.zeros((DSA_KC, DSA_QB), F32)
        for p in range(IDX_HEADS // 2):
            qp = qi_ref[:, p * LANE:(p + 1) * LANE]
            acc = acc + jnp.maximum(_dot_nt(ka, qp), 0.0) * w_t[2 * p:2 * p + 1, :]
            acc = acc + jnp.maximum(_dot_nt(kb, qp), 0.0) * w_t[2 * p + 1:2 * p + 2, :]
        key = _score_to_key(acc)
        valid = (c * DSA_KC + key_s) <= row_t
        key_scr[ks, :] = jnp.where(valid, key, INT_MIN)
        kmax = jnp.maximum(kmax, jnp.max(jnp.where(valid, key, INT_MIN), axis=0, keepdims=True))
        kmin = jnp.minimum(kmin, jnp.min(jnp.where(valid, key, INT_MAX), axis=0, keepdims=True))
        return kmax, kmin

    kmax, kmin = lax.fori_loop(0, n_kc, score_chunk, (jnp.full(row_vec, INT_MIN, I32),
                                                     jnp.full(row_vec, INT_MAX, I32)))

    def scan_keys(cand, with_below):
        def body(c, carry):
            cnt, below = carry
            ks = pl.ds(pl.multiple_of(c * DSA_KC, DSA_KC), DSA_KC)
            keys = key_scr[ks, :]
            ge = keys >= cand
            ones = ge.astype(I32)
            low = jnp.where(ge, INT_MIN, keys)
            for u in range(DSA_KC // SUBLANE):
                us = slice(u * SUBLANE, (u + 1) * SUBLANE)
                cnt = cnt + ones[us, :]
                if with_below:
                    below = jnp.maximum(below, low[us, :])
            return cnt, below
        cnt, below = lax.fori_loop(0, n_kc, body, (jnp.zeros((SUBLANE, DSA_QB), I32),
                                                   jnp.full((SUBLANE, DSA_QB), INT_MIN, I32)))
        cnt = jnp.sum(cnt, axis=0, keepdims=True)
        if with_below:
            return cnt, jnp.max(below, axis=0, keepdims=True)
        return cnt

    def open_rows(lo, hi, c_lo):
        return (c_lo > topk) & (hi - 1 > lo)

    def any_row(flag):
        return jnp.max(jnp.where(flag, 1, 0))

    def update(cand, cnt, lo, hi, c_lo, c_hi):
        ge = cnt >= topk
        return (jnp.where(ge, cand, lo), jnp.where(ge, hi, cand),
                jnp.where(ge, cnt, c_lo), jnp.where(ge, c_hi, cnt))

    def crowded(lo, hi, c_lo, c_hi):
        return any_row(open_rows(lo, hi, c_lo) & (c_lo - c_hi > DSA_FEW_KEYS))

    def halve_step(it, lo, hi, c_lo, c_hi):
        key_mid = (lo >> 1) + (hi >> 1) + (lo & hi & 1)
        score_mid = _score_to_key(0.5 * _key_to_score(lo) + 0.5 * _key_to_score(hi - 1))
        cand = jnp.where(it < DSA_SCORE_MID_STEPS, score_mid, key_mid)
        cand = jnp.minimum(jnp.maximum(cand, lo + 1), hi - 1)
        cand = jnp.where(hi - 1 > lo, cand, lo)
        return update(cand, scan_keys(cand, False), lo, hi, c_lo, c_hi)

    def halve_body(st):
        it, _, lo, hi, c_lo, c_hi = st
        go = crowded(lo, hi, c_lo, c_hi)
        return (it + 1, go) + halve_step(it, lo, hi, c_lo, c_hi)

    def walk_step(lo, hi, c_lo, c_hi, nxt):
        is_open = open_rows(lo, hi, c_lo)
        cand = jnp.where(is_open, nxt, lo)
        cnt, below = scan_keys(cand, True)
        ge = cnt >= topk
        hi = jnp.where(is_open, jnp.where(ge, cand + 1, cand), hi)
        c_hi = jnp.where(is_open & jnp.logical_not(ge), cnt, c_hi)
        lo = jnp.where(is_open & ge, cand, lo)
        c_lo = jnp.where(is_open & ge, cnt, c_lo)
        nxt = jnp.where(ge, nxt, below)
        return lo, hi, c_lo, c_hi, nxt

    def walk_body(st):
        go = any_row(open_rows(st[1], st[2], st[3]))
        return (go,) + walk_step(*st[1:])

    n_valid = jnp.minimum(t0 + lax.broadcasted_iota(I32, row_vec, 1) + 1, seq)
    st = (kmin, kmax + 1, n_valid, jnp.zeros(row_vec, I32))
    st = lax.fori_loop(0, DSA_HALVE_FIXED, lambda it, s: halve_step(it, *s), st)
    st = lax.while_loop(lambda s: s[1] > 0, halve_body,
                        (jnp.int32(DSA_HALVE_FIXED), crowded(*st)) + st)[2:]
    _, nxt0 = scan_keys(st[1], True)
    st = lax.fori_loop(0, DSA_WALK_FIXED, lambda it, s: walk_step(*s), st + (nxt0,))
    _, thr, hi, n_ge, n_gt, _ = lax.while_loop(
        lambda s: s[0] > 0, walk_body, (any_row(open_rows(st[0], st[1], st[2])),) + st)

    tied = n_ge > topk
    has_tie = jnp.max(jnp.where(tied, 1, 0)) > 0

    @pl.when(has_tie)
    def _():
        room = (topk - n_gt).astype(F32)
        ii = lax.broadcasted_iota(I32, (LANE, LANE), 0)
        jj = lax.broadcasted_iota(I32, (LANE, LANE), 1)
        lower = (ii >= jj).astype(BF16)

        def body(c, seen):
            ks = pl.ds(pl.multiple_of(c * LANE, LANE), LANE)
            kk = key_scr[ks, :]
            eq = kk == thr
            rank = seen + _dot(lower, eq.astype(BF16))
            drop = eq & (rank > room) & tied
            key_scr[ks, :] = jnp.where(drop, INT_MIN, kk)
            return seen + jnp.sum(eq.astype(F32), axis=0, keepdims=True)
        lax.fori_loop(0, n_kc * (DSA_KC // LANE), body, jnp.zeros(row_vec, F32))

    def mask_chunk(c, carry):
        ks = pl.ds(pl.multiple_of(c * DSA_KC, DSA_KC), DSA_KC)
        sel = jnp.where(key_scr[ks, :] >= thr, 0.0, -jnp.inf).astype(F32)
        key_scr[ks, :] = pltpu.bitcast(sel, I32)
        return carry
    lax.fori_loop(0, n_kc, mask_chunk, 0)

    heads = range(N_HEADS)
    hsl = [slice(h * HEAD_DIM, (h + 1) * HEAD_DIM) for h in heads]

    def logit_chunk(c, ms):
        ks = pl.ds(pl.multiple_of(c * DSA_KC, DSA_KC), DSA_KC)
        k_c = kb16_scr[ks, :]
        sel = pltpu.bitcast(key_scr[ks, :], F32)
        out = []
        for h in heads:
            s = _dot_nt(k_c, q_ref[:, hsl[h]]) * (HEAD_DIM ** -0.5 * LOG2E) + sel
            lg_scr[h, ks, :] = s
            out.append(jnp.maximum(ms[h], jnp.max(s, axis=0, keepdims=True)))
        return tuple(out)
    ms = lax.fori_loop(0, n_kc, logit_chunk,
                       tuple(jnp.full(row_vec, -jnp.inf, F32) for _ in heads))

    band_off = pl.multiple_of(jnp.maximum(qb_idx - 1, 0) * DSA_QB, DSA_QB)
    ws = pl.ds(band_off, 2 * DSA_QB)
    ms = list(ms)
    for h in heads:
        band_h = band_scr[h]
        band_first = jnp.concatenate([band_h[DSA_QB:, :], jnp.zeros((DSA_QB, DSA_QB), F32)], axis=0)
        win = lg_scr[h, ws, :] + jnp.where(qb_idx == 0, band_first, band_h)
        lg_scr[h, ws, :] = win
        ms[h] = jnp.maximum(ms[h], jnp.max(win, axis=0, keepdims=True))

    def pv_chunk(c, accs):
        ks = pl.ds(pl.multiple_of(c * DSA_KC, DSA_KC), DSA_KC)
        vt_c = vt_scr[:, ks]
        return tuple(accs[h] + _dot(vt_c, jnp.exp2(lg_scr[h, ks, :] - ms[h]).astype(BF16))
                     for h in heads)
    accs = lax.fori_loop(0, n_kc, pv_chunk,
                         tuple(jnp.zeros((2 * HEAD_DIM, DSA_QB), F32) for _ in heads))
    for h in heads:
        o_ref[:, hsl[h]] = (accs[h][:HEAD_DIM, :] / accs[h][HEAD_DIM:, :]).T.astype(BF16)


def dsa_attention(proj, q, q_idx, prep_tok, rel_bias, batch, seq):
    n = proj.shape[0]
    nq = seq // DSA_QB
    topk = min(TOPK_MAX, seq // 4)
    wi = IDX_HEADS * IDX_DIM
    rowblk = lambda w, cb: pl.BlockSpec((DSA_QB, w), lambda b, i: (b * nq + i, cb))
    seqblk = lambda off: pl.BlockSpec((seq, LANE), lambda b, i: (b, off // LANE))
    return pl.pallas_call(
        functools.partial(_dsa_kernel, seq=seq, topk=topk),
        grid=(batch, nq),
        in_specs=[pl.BlockSpec(memory_space=pltpu.SMEM),
                  rowblk(WIDTH, 0), rowblk(wi, 0), rowblk(LANE, 0),
                  seqblk(C_DK), seqblk(C_DV), seqblk(C_KA), seqblk(C_KB)],
        out_specs=pl.BlockSpec((DSA_QB, WIDTH), lambda b, i: (b * nq + i, 0)),
        out_shape=jax.ShapeDtypeStruct((n, WIDTH), BF16),
        scratch_shapes=[pltpu.VMEM((seq, DSA_QB), I32),
                        pltpu.VMEM((N_HEADS, seq, DSA_QB), F32),
                        pltpu.VMEM((N_HEADS, 2 * DSA_QB, DSA_QB), F32),
                        pltpu.VMEM((seq, LANE), BF16),
                        pltpu.VMEM((2 * HEAD_DIM, seq), BF16),
                        pltpu.VMEM((seq, LANE), BF16),
                        pltpu.VMEM((seq, LANE), BF16)],
        compiler_params=_cparams(("arbitrary", "arbitrary")),
        name="dsa_attention",
    )(rel_bias, q, q_idx, prep_tok, proj, proj, proj, proj)


GDN_T = 256
GDN_GROUP = 2
GDN_HALO = 8


def _gdn_kernel(x_ref, cw_ref, ng_ref, tok_ref, tr_ref, o_ref,
                xq_scr, xk_scr, xv_scr, state_scr):
    first = pl.program_id(1) == 0

    @pl.when(first)
    def _():
        state_scr[...] = jnp.zeros_like(state_scr)
        for scr in (xq_scr, xk_scr, xv_scr):
            scr[0:GDN_HALO, :] = jnp.zeros((GDN_HALO, WIDTH), F32)

    def conv(col, scr, w_off):
        scr[GDN_HALO:, :] = x_ref[:, col - C_GQ:col - C_GQ + WIDTH]
        y = jnp.zeros((GDN_T, WIDTH), F32)
        for i in range(GDN_CONV):
            st = GDN_HALO - (GDN_CONV - 1) + i
            y = y + scr[st:st + GDN_T, :] * cw_ref[i:i + 1, w_off:w_off + WIDTH]
        scr[0:GDN_HALO, :] = scr[GDN_T:GDN_T + GDN_HALO, :]
        return _silu(y)

    qc = conv(C_GQ, xq_scr, 0)
    kc = conv(C_GK, xk_scr, WIDTH)
    vc = conv(C_GV, xv_scr, 2 * WIDTH)
    z_off = C_GZ - C_GQ
    tok = tok_ref[...]
    c = GDN_CHUNK
    heads = range(N_HEADS)
    hsl = [slice(h * HEAD_DIM, (h + 1) * HEAD_DIM) for h in heads]

    def l2norm_heads(x, scale):
        return jnp.concatenate(
            [x[:, s] * (lax.rsqrt(jnp.sum(x[:, s] * x[:, s], axis=-1, keepdims=True) + EPS) * scale)
             for s in hsl], axis=1)

    qf = l2norm_heads(qc, HEAD_DIM ** -0.5)
    kf = l2norm_heads(kc, 1.0)

    grp = GDN_GROUP
    gw = grp * HEAD_DIM
    nb = grp * c
    ri = lax.broadcasted_iota(I32, (nb, nb), 0)
    ci = lax.broadcasted_iota(I32, (nb, nb), 1)
    tril = ((ri // c) == (ci // c)) & (ri >= ci)
    eye = (ri == ci).astype(F32)
    pair_masks = []
    for lg in range(c.bit_length() - 1):
        pair_masks.append(((ri >> (lg + 1)) == (ci >> (lg + 1)))
                          & (((ri >> lg) & 1) == 1) & (((ci >> lg) & 1) == 0))
    lane_head = lax.broadcasted_iota(I32, (c, gw), 1) // HEAD_DIM
    row_head = lax.broadcasted_iota(I32, (nb, HEAD_DIM), 0) // c

    def spread(x):
        return jnp.concatenate([jnp.where(lane_head == u, x, 0.0) for u in range(grp)], axis=0)

    def stack(x):
        return jnp.concatenate([x[:, hsl[u]] for u in range(grp)], axis=0)

    def spread_lanes(x):
        return jnp.concatenate([jnp.where(row_head == u, x, 0.0) for u in range(grp)], axis=1)

    def split(x):
        hi = x.astype(BF16)
        return hi, (x - hi.astype(F32)).astype(BF16)

    def dot_split(a, b):
        return _dot(a[0], b[0]) + (_dot(a[0], b[1]) + _dot(a[1], b[0]))

    items = [(j, gi) for j in range(GDN_T // c) for gi in range(N_HEADS // grp)]
    pre = []
    for j, gi in items:
        rs = slice(j * c, (j + 1) * c)
        last = slice((j + 1) * c - 1, (j + 1) * c)
        gh = [gi * grp + u for u in range(grp)]
        gsl = slice(gi * gw, (gi + 1) * gw)
        qj, kj, vj = qf[rs, gsl], kf[rs, gsl], vc[rs, gsl]
        b_col = jnp.concatenate([tok[rs, L_GB + h:L_GB + h + 1] for h in gh], axis=0)
        g_col = jnp.concatenate([tok[rs, L_GA + h:L_GA + h + 1] for h in gh], axis=0)
        g_row = jnp.concatenate([tr_ref[0, L_GA + h:L_GA + h + 1, rs] for h in gh], axis=1)
        g_last = [tok[last, L_GA + h:L_GA + h + 1] for h in gh]
        g_last_col = jnp.concatenate([jnp.broadcast_to(g, (c, 1)) for g in g_last], axis=0)
        k_sp = spread(kj)
        q_sp = spread(qj)
        k_sp16 = k_sp.astype(BF16)
        decay = jnp.exp(jnp.where(tril, g_col - g_row, -jnp.inf))
        eg = jnp.exp(g_col)
        pre.append(dict(
            gh=gh, gsl=gsl,
            l_mat=b_col * _dot_nt(k_sp16, k_sp16) * decay,
            rhs=jnp.concatenate([stack(vj) * b_col, stack(kj) * (b_col * eg)], axis=1),
            qk=_dot_nt(q_sp.astype(BF16), k_sp16) * decay,
            q_dec=q_sp * eg,
            k_dec=k_sp * jnp.exp(g_last_col - g_col),
            e_last=jnp.concatenate([jnp.broadcast_to(jnp.exp(g), (HEAD_DIM, 1)) for g in g_last], axis=0)))

    t_inv = [eye - jnp.where(pair_masks[0], p["l_mat"], 0.0) for p in pre]
    for pm in pair_masks[1:]:
        t_s = [split(t) for t in t_inv]
        m_t = [dot_split(split(jnp.where(pm, p["l_mat"], 0.0)), ts) for p, ts in zip(pre, t_s)]
        t_inv = [t - dot_split(ts, split(m)) for t, ts, m in zip(t_inv, t_s, m_t)]
    sols = [dot_split(split(t), split(p["rhs"])) for t, p in zip(t_inv, pre)]

    outs = [[] for _ in heads]
    for p, sol in zip(pre, sols):
        u0 = sol[:, :HEAD_DIM]
        kcum = sol[:, HEAD_DIM:]
        st = state_scr[p["gsl"], :]
        stb = st.astype(BF16)
        v_new = u0 - _dot(spread_lanes(kcum).astype(BF16), stb)
        v_new_b = v_new.astype(BF16)
        o_st = _dot(p["q_dec"].astype(BF16), stb) + _dot(p["qk"].astype(BF16), v_new_b)
        state_scr[p["gsl"], :] = st * p["e_last"] + _dot(p["k_dec"].T.astype(BF16), v_new_b)
        for u, h in enumerate(p["gh"]):
            outs[h].append(o_st[u * c:(u + 1) * c, :])

    for h in heads:
        o = jnp.concatenate(outs[h], axis=0)
        ms = jnp.mean(o * o, axis=-1, keepdims=True)
        on = o * lax.rsqrt(ms + EPS) * ng_ref[...]
        z = x_ref[:, z_off + hsl[h].start:z_off + hsl[h].stop]
        o_ref[:, hsl[h]] = (on * _silu(z)).astype(BF16)


def gated_deltanet(proj, prep_tok, prep_tr, conv_w, norm_g, batch, seq):
    n = proj.shape[0]
    t = GDN_T
    nt = seq // t
    return pl.pallas_call(
        _gdn_kernel,
        grid=(batch, nt),
        in_specs=[pl.BlockSpec((t, 4 * WIDTH), lambda b, i: (b * nt + i, C_GQ // (4 * WIDTH))),
                  pl.BlockSpec((GDN_CONV, 3 * WIDTH), lambda b, i: (0, 0)),
                  pl.BlockSpec((1, HEAD_DIM), lambda b, i: (0, 0)),
                  pl.BlockSpec((t, LANE), lambda b, i: (b * nt + i, 0)),
                  pl.BlockSpec((1, PREP_ROWS, t), lambda b, i: (b, 0, i))],
        out_specs=pl.BlockSpec((t, WIDTH), lambda b, i: (b * nt + i, 0)),
        out_shape=jax.ShapeDtypeStruct((n, WIDTH), BF16),
        scratch_shapes=[pltpu.VMEM((t + GDN_HALO, WIDTH), F32),
                        pltpu.VMEM((t + GDN_HALO, WIDTH), F32),
                        pltpu.VMEM((t + GDN_HALO, WIDTH), F32),
                        pltpu.VMEM((N_HEADS * HEAD_DIM, HEAD_DIM), F32)],
        compiler_params=_cparams(("arbitrary", "arbitrary")),
        name="gated_deltanet",
    )(proj, conv_w, norm_g.reshape(1, HEAD_DIM), prep_tok, prep_tr)


def _merge_kernel(h_ref, b0_ref, b1_ref, b2_ref, b3_ref, g0_ref, g1_ref, g2_ref, g3_ref,
                  wb_ref, o_ref):
    h = h_ref[...]
    acc = None
    for n, (b_ref, g_ref) in enumerate(zip((b0_ref, b1_ref, b2_ref, b3_ref),
                                           (g0_ref, g1_ref, g2_ref, g3_ref))):
        gate = jax.nn.sigmoid(_dot(h, g_ref[...]))
        term = gate * _dot(b_ref[...], wb_ref[n])
        acc = term if acc is None else acc + term
    o_ref[...] = acc.astype(BF16)


def merge_branches(h, branches, w_gate, w_branch, *, tm=512, tn=512):
    _, n, d = h.shape
    nj = d // tn
    bspec = pl.BlockSpec((tm, WIDTH), lambda j, i: (i, 0))
    gspec = lambda k: pl.BlockSpec((d, tn), lambda j, i: (0, k * nj + j))
    return pl.pallas_call(
        _merge_kernel,
        grid=(nj, n // tm),
        in_specs=[pl.BlockSpec((None, tm, d), lambda j, i: (0, i, 0)),
                  bspec, bspec, bspec, bspec,
                  gspec(0), gspec(1), gspec(2), gspec(3),
                  pl.BlockSpec((N_BRANCH, WIDTH, tn), lambda j, i: (0, 0, j))],
        out_specs=pl.BlockSpec((tm, tn), lambda j, i: (i, j)),
        out_shape=jax.ShapeDtypeStruct((n, d), BF16),
        compiler_params=_cparams(("arbitrary", "arbitrary")),
        name="merge_branches",
    )(h, *branches, w_gate, w_gate, w_gate, w_gate, w_branch)


def _resid_mm_kernel(a_ref, w_ref, x_ref, o_ref):
    o_ref[...] = x_ref[...] + _dot(a_ref[...], w_ref[...])


def resid_matmul(a, w, x, *, tm=512, tn=1024, name="resid_matmul"):
    n, k = a.shape
    d = w.shape[1]
    return pl.pallas_call(
        _resid_mm_kernel,
        grid=(d // tn, n // tm),
        in_specs=[pl.BlockSpec((tm, k), lambda j, i: (i, 0)),
                  pl.BlockSpec((k, tn), lambda j, i: (0, j)),
                  pl.BlockSpec((tm, tn), lambda j, i: (i, j))],
        out_specs=pl.BlockSpec((tm, tn), lambda j, i: (i, j)),
        out_shape=jax.ShapeDtypeStruct((n, d), F32),
        compiler_params=_cparams(("arbitrary", "arbitrary")),
        name=name,
    )(a, w, x)


FFN_HALO = 8
FFN_SUB = 512


def _ffn1_kernel(x_ref, g_ref, wg_ref, wu_ref, cw_ref, cb_ref, o_ref, h_scr, gt_scr, halo_scr,
                 *, tm, tiles_per_seq):
    i = pl.program_id(1)

    @pl.when(i == 0)
    def _():
        halo_scr[...] = jnp.zeros_like(halo_scr)

    def body(r, carry):
        rows = pl.ds(pl.multiple_of(r * NORM_ROWS, NORM_ROWS), NORM_ROWS)
        h_scr[rows, :] = _rmsnorm_rows(x_ref, g_ref, rows).astype(BF16)
        return carry
    lax.fori_loop(0, tm // NORM_ROWS, body, 0)

    h = h_scr[...]
    seq_start = (i % tiles_per_seq) == 0
    tn = o_ref.shape[1]
    for off in range(0, tn, FFN_SUB):
        cs = slice(off, min(off + FFN_SUB, tn))
        g = _dot(h, wg_ref[:, cs])
        gt_scr[FFN_HALO:, cs] = g
        gt_scr[0:FFN_HALO, cs] = jnp.where(seq_start, 0.0, halo_scr[:, cs])
        halo_scr[:, cs] = g[tm - FFN_HALO:, :]
        y = cb_ref[:, cs] + g * cw_ref[FFN_CONV - 1:FFN_CONV, cs]
        for t in range(FFN_CONV - 1):
            st = FFN_HALO - (FFN_CONV - 1) + t
            y = y + gt_scr[st:st + tm, cs] * cw_ref[t:t + 1, cs]
        o_ref[:, cs] = (_silu(y) * _dot(h, wu_ref[:, cs])).astype(BF16)


def conv_ffn_up(x, gain, w_gate, w_up, conv_w, conv_b, seq, *, tm=512, col_parts=2):
    n, d = x.shape
    f = w_gate.shape[1]
    tn = f // col_parts
    wspec = pl.BlockSpec((d, tn), lambda j, i: (0, j), pipeline_mode=pl.Buffered(1))
    return pl.pallas_call(
        functools.partial(_ffn1_kernel, tm=tm, tiles_per_seq=seq // tm),
        grid=(col_parts, n // tm),
        in_specs=[pl.BlockSpec((tm, d), lambda j, i: (i, 0)),
                  pl.BlockSpec((1, d), lambda j, i: (0, 0)),
                  wspec, wspec,
                  pl.BlockSpec((FFN_CONV, tn), lambda j, i: (0, j)),
                  pl.BlockSpec((1, tn), lambda j, i: (0, j))],
        out_specs=pl.BlockSpec((tm, tn), lambda j, i: (i, j)),
        out_shape=jax.ShapeDtypeStruct((n, f), BF16),
        scratch_shapes=[pltpu.VMEM((tm, d), BF16),
                        pltpu.VMEM((tm + FFN_HALO, tn), F32),
                        pltpu.VMEM((FFN_HALO, tn), F32)],
        compiler_params=_cparams(("arbitrary", "arbitrary")),
        name="conv_ffn_up",
    )(x, gain.reshape(1, d), w_gate, w_up, conv_w, conv_b.reshape(1, f))


IN_SIZES = (WIDTH, WIDTH, WIDTH, WIDTH,
            DSA_Q_RANK, HEAD_DIM, HEAD_DIM, IDX_DIM, IDX_HEADS,
            WIDTH, WIDTH, WIDTH, N_HEADS,
            WIDTH, WIDTH, WIDTH, WIDTH, N_HEADS, N_HEADS)
IN_NAMES = ("r_q", "r_k", "r_v", "r_g", "d_cq", "d_k", "d_v", "i_k", "i_w",
            "f_q", "f_k", "f_v", "f_f", "g_q", "g_k", "g_v", "g_z", "g_b", "g_a")
IN_PLAN = (("r_q", C_RQ), ("r_k", C_RK), ("r_v", C_RV), ("r_g", C_RG),
           ("f_q", C_FQ), ("f_k", C_FK), ("f_v", C_FV),
           ("g_q", C_GQ), ("g_k", C_GK), ("g_v", C_GV), ("g_z", C_GZ),
           ("d_k", C_DK), ("d_cq", C_DCQ), ("d_v", C_DV),
           ("i_k", C_KA), ("i_k", C_KB + IDX_DIM),
           ("i_w", C_SM + L_IW), ("f_f", C_SM + L_FF), ("g_b", C_SM + L_GB), ("g_a", C_SM + L_GA))


def _prep_w_in_kernel(w_ref, m_ref, g_ref):
    src = {}
    off = 0
    for name, size in zip(IN_NAMES, IN_SIZES):
        src[name] = (off, size)
        off += size
    m_ref[...] = jnp.zeros_like(m_ref)
    for name, dst in IN_PLAN:
        so, w = src[name]
        m_ref[:, dst:dst + w] = w_ref[:, so:so + w].astype(BF16)
    g_ref[...] = w_ref[:, off:off + g_ref.shape[1]].astype(BF16)


def prep_w_in(w_in, layer, *, tr=128):
    _, d, c = w_in.shape
    return pl.pallas_call(
        _prep_w_in_kernel,
        grid=(d // tr,),
        in_specs=[pl.BlockSpec((None, tr, c), lambda i: (layer, i, 0))],
        out_specs=[pl.BlockSpec((tr, C_TOT), lambda i: (i, 0)),
                   pl.BlockSpec((tr, N_BRANCH * d), lambda i: (i, 0))],
        out_shape=[jax.ShapeDtypeStruct((d, C_TOT), BF16),
                   jax.ShapeDtypeStruct((d, N_BRANCH * d), BF16)],
        compiler_params=_cparams(("arbitrary",)),
        name="prep_w_in",
    )(w_in)


def _cast_kernel(w_ref, o_ref):
    o_ref[...] = w_ref[...].astype(BF16)


def cast_layer(w, layer, *, tr=256):
    _, r, c = w.shape
    if r % tr:
        tr = r
    return pl.pallas_call(
        _cast_kernel,
        grid=(r // tr,),
        in_specs=[pl.BlockSpec((None, tr, c), lambda i: (layer, i, 0))],
        out_specs=pl.BlockSpec((tr, c), lambda i: (i, 0)),
        out_shape=jax.ShapeDtypeStruct((r, c), BF16),
        compiler_params=_cparams(("arbitrary",)),
        name="cast_bf16",
    )(w)


def cast_branch(w_branch, layer):
    _, nbr, r, c = w_branch.shape
    return pl.pallas_call(
        _cast_kernel,
        grid=(nbr,),
        in_specs=[pl.BlockSpec((None, None, r, c), lambda i: (layer, i, 0, 0))],
        out_specs=pl.BlockSpec((None, r, c), lambda i: (i, 0, 0)),
        out_shape=jax.ShapeDtypeStruct((nbr, r, c), BF16),
        compiler_params=_cparams(("arbitrary",)),
        name="cast_branch",
    )(w_branch)


def kernel(x, norm_mix, w_in, dsa_cq_norm, dsa_w_uq, dsa_w_qidx, fox_f_bias, gdn_conv, gdn_a_log,
           gdn_dt_bias, gdn_norm, w_branch, w_out, rel_bias, norm_ffn, ffn_w_gate, ffn_w_up,
           ffn_conv, ffn_conv_b, ffn_w_down, final_norm):
    batch, seq, d = x.shape
    depth = w_in.shape[0]
    xf = x.reshape(batch * seq, d)
    ret_tables = _retention_tables(seq)
    for l in range(depth):
        w_main, w_gate = prep_w_in(w_in, l)
        proj, h = norm_proj(xf, norm_mix[l], w_main)
        par = jnp.zeros((SUBLANE, LANE), F32)
        par = par.at[0, L_FF:L_FF + N_HEADS].set(fox_f_bias[l])
        par = par.at[0, L_GA:L_GA + N_HEADS].set(gdn_dt_bias[l])
        par = par.at[1, L_GA:L_GA + N_HEADS].set(gdn_a_log[l])
        prep_tok, prep_tr = prep_small(proj, par, batch, seq)
        o_ret = retention(proj, ret_tables, batch, seq)
        q_dsa, q_idx = dsa_proj(proj, dsa_cq_norm[l], cast_layer(dsa_w_uq, l), cast_layer(dsa_w_qidx, l))
        o_dsa = dsa_attention(proj, q_dsa, q_idx, prep_tok, rel_bias, batch, seq)
        o_fox = fox_attention(proj, prep_tr, batch, seq)
        o_gdn = gated_deltanet(proj, prep_tok, prep_tr, gdn_conv[l], gdn_norm[l], batch, seq)
        merged = merge_branches(h, (o_ret, o_dsa, o_fox, o_gdn), w_gate, cast_branch(w_branch, l))
        xf = resid_matmul(merged, cast_layer(w_out, l), xf, tn=d, name="out_proj")
        act = conv_ffn_up(xf, norm_ffn[l], cast_layer(ffn_w_gate, l), cast_layer(ffn_w_up, l),
                          ffn_conv[l], ffn_conv_b[l], seq)
        xf = resid_matmul(act, cast_layer(ffn_w_down, l), xf, name="ffn_down")
    return rmsnorm(xf, final_norm).reshape(batch, seq, d)
```

```python
import functools
import math

import jax
import jax.numpy as jnp
from jax import lax
from jax.experimental import pallas as pl
from jax.experimental.pallas import tpu as pltpu

F32 = jnp.float32
BF16 = jnp.bfloat16
I32 = jnp.int32

HEAD_DIM = 128
N_HEADS = 4
WIDTH = N_HEADS * HEAD_DIM
N_BRANCH = 4
RET_CHUNK = 128
ROPE_BASE = 10000.0
DSA_Q_RANK = 384
IDX_HEADS = 16
IDX_DIM = 64
TOPK_MAX = 256
GDN_CONV = 4
GDN_CHUNK = 64
REL_BUCKETS = 32
REL_MAX_DIST = 128
FFN_CONV = 3
EPS = 1e-6

LANE = 128
SUBLANE = 8
VMEM_LIMIT = 56 * 1024 * 1024

C_RQ, C_RK, C_RV, C_RG = 0, 512, 1024, 1536
C_FQ, C_DK, C_DCQ = 2048, 2560, 2688
C_FK, C_FV = 3072, 3584
C_GQ, C_GK, C_GV, C_GZ = 4096, 4608, 5120, 5632
C_DV, C_KA, C_KB, C_SM = 6144, 6272, 6400, 6528
C_TOT = 6656
L_IW, L_FF, L_GB, L_GA = 0, 16, 20, 24

LOG2E = 1.4426950408889634
INT_MIN = -(2 ** 31)
INT_MAX = 2 ** 31 - 1
HIGHEST = lax.Precision.HIGHEST


def _cparams(sem, vmem=VMEM_LIMIT):
    return pltpu.CompilerParams(dimension_semantics=sem, vmem_limit_bytes=vmem)


def _dot(a, b):
    return jnp.dot(a, b, preferred_element_type=F32)


def _dot_nt(a, b):
    return lax.dot_general(a, b, (((1,), (1,)), ((), ())), preferred_element_type=F32)


def _silu(x):
    return x * jax.nn.sigmoid(x)


NORM_ROWS = 128


def _rmsnorm_rows(x_ref, g_ref, rows):
    x = x_ref[rows, :]
    ms = jnp.mean(x * x, axis=-1, keepdims=True)
    return x * lax.rsqrt(ms + EPS) * g_ref[...]


PROJ_SUB = 512


def _norm_proj_kernel(x_ref, g_ref, w_ref, o_ref, h_ref, h_scr, *, tm):
    def body(r, carry):
        rows = pl.ds(pl.multiple_of(r * NORM_ROWS, NORM_ROWS), NORM_ROWS)
        hb = _rmsnorm_rows(x_ref, g_ref, rows).astype(BF16)
        h_scr[rows, :] = hb
        h_ref[rows, :] = hb
        return carry
    lax.fori_loop(0, tm // NORM_ROWS, body, 0)

    h = h_scr[...]
    tn = o_ref.shape[1]
    for off in range(0, tn, PROJ_SUB):
        cs = slice(off, min(off + PROJ_SUB, tn))
        o_ref[:, cs] = _dot(h, w_ref[:, cs])


def norm_proj(x, gain, w, *, tm=512, col_parts=2):
    n, d = x.shape
    c = w.shape[1]
    tn = c // col_parts
    return pl.pallas_call(
        functools.partial(_norm_proj_kernel, tm=tm),
        grid=(col_parts, n // tm),
        in_specs=[pl.BlockSpec((tm, d), lambda j, i: (i, 0)),
                  pl.BlockSpec((1, d), lambda j, i: (0, 0)),
                  pl.BlockSpec((d, tn), lambda j, i: (0, j), pipeline_mode=pl.Buffered(1))],
        out_specs=[pl.BlockSpec((tm, tn), lambda j, i: (i, j)),
                   pl.BlockSpec((None, tm, d), lambda j, i: (j, i, 0))],
        out_shape=[jax.ShapeDtypeStruct((n, c), F32),
                   jax.ShapeDtypeStruct((col_parts, n, d), BF16)],
        scratch_shapes=[pltpu.VMEM((tm, d), BF16)],
        compiler_params=_cparams(("arbitrary", "arbitrary")),
        name="norm_proj",
    )(x, gain.reshape(1, d), w)


def _rmsnorm_kernel(x_ref, g_ref, o_ref, *, tm):
    def body(r, carry):
        rows = pl.ds(pl.multiple_of(r * NORM_ROWS, NORM_ROWS), NORM_ROWS)
        o_ref[rows, :] = _rmsnorm_rows(x_ref, g_ref, rows)
        return carry
    lax.fori_loop(0, tm // NORM_ROWS, body, 0)


def rmsnorm(x, gain, *, tm=512):
    n, d = x.shape
    return pl.pallas_call(
        functools.partial(_rmsnorm_kernel, tm=tm),
        grid=(n // tm,),
        in_specs=[pl.BlockSpec((tm, d), lambda i: (i, 0)),
                  pl.BlockSpec((1, d), lambda i: (0, 0))],
        out_specs=pl.BlockSpec((tm, d), lambda i: (i, 0)),
        out_shape=jax.ShapeDtypeStruct((n, d), F32),
        compiler_params=_cparams(("arbitrary",)),
        name="final_rmsnorm",
    )(x, gain.reshape(1, d))


def _prep_kernel(s_ref, par_ref, tok_ref, tr_ref, carry_scr):
    @pl.when(pl.program_id(1) == 0)
    def _():
        carry_scr[...] = jnp.zeros_like(carry_scr)

    s = s_ref[...]
    lane = lax.broadcasted_iota(I32, (LANE, LANE), 1)
    row = lax.broadcasted_iota(I32, (LANE, LANE), 0)
    z = s + par_ref[0:1, :]
    soft = jnp.maximum(z, 0.0) + jnp.log1p(jnp.exp(-jnp.abs(z)))
    log_sig = z - soft
    sig = jax.nn.sigmoid(z)
    g_val = -jnp.exp(par_ref[1:2, :]) * soft
    is_f = (lane[0:1] >= L_FF) & (lane[0:1] < L_FF + N_HEADS)
    is_b = (lane[0:1] >= L_GB) & (lane[0:1] < L_GB + N_HEADS)
    is_a = (lane[0:1] >= L_GA) & (lane[0:1] < L_GA + N_HEADS)
    pre = jnp.where(is_f, log_sig, jnp.where(is_a, g_val, 0.0))
    tri = (row >= lane).astype(F32)
    tri_blk = ((row >= lane) & ((row // GDN_CHUNK) == (lane // GDN_CHUNK))).astype(F32)
    subs = [slice(u * LANE, (u + 1) * LANE) for u in range(PREP_T // LANE)]
    cum_full = [jnp.dot(tri, pre[u], precision=HIGHEST, preferred_element_type=F32) for u in subs]
    cum_blk = [jnp.dot(tri_blk, pre[u], precision=HIGHEST, preferred_element_type=F32) for u in subs]
    scale_iw = IDX_HEADS ** -0.5 * IDX_DIM ** -0.5
    carry = carry_scr[0:1, :]
    for u, cf, cb in zip(subs, cum_full, cum_blk):
        c_fox = cf + carry
        carry = c_fox[LANE - 1:LANE, :]
        out = jnp.where(is_f, c_fox,
                        jnp.where(is_a, cb,
                                  jnp.where(is_b, sig[u],
                                            jnp.where(lane[0:1] < IDX_HEADS, s[u] * scale_iw, 0.0))))
        tok_ref[u, :] = out
        tr_ref[0, :, u] = out.T[0:PREP_ROWS, :]
    carry_scr[0:1, :] = carry


PREP_T = 512
PREP_ROWS = 32


def prep_small(proj, par, batch, seq):
    n = proj.shape[0]
    nc = seq // PREP_T
    return pl.pallas_call(
        _prep_kernel,
        grid=(batch, nc),
        in_specs=[pl.BlockSpec((PREP_T, LANE), lambda b, c: (b * nc + c, C_SM // LANE)),
                  pl.BlockSpec((SUBLANE, LANE), lambda b, c: (0, 0))],
        out_specs=[pl.BlockSpec((PREP_T, LANE), lambda b, c: (b * nc + c, 0)),
                   pl.BlockSpec((1, PREP_ROWS, PREP_T), lambda b, c: (b, 0, c))],
        out_shape=[jax.ShapeDtypeStruct((n, LANE), F32),
                   jax.ShapeDtypeStruct((batch, PREP_ROWS, seq), F32)],
        scratch_shapes=[pltpu.VMEM((SUBLANE, LANE), F32)],
        compiler_params=_cparams(("arbitrary", "arbitrary")),
        name="prep_small",
    )(proj, par)


RET_T = 512


def _ret_gamma():
    return [math.log1p(-(2.0 ** (-5.0 - h))) for h in range(N_HEADS)]


def _retention_kernel(x_ref, cos_ref, sin_ref, dec_ref, zeta_ref, xi_ref, o_ref, state_scr):
    @pl.when(pl.program_id(1) == 0)
    def _():
        state_scr[...] = jnp.zeros_like(state_scr)

    log_gamma = _ret_gamma()
    heads = range(N_HEADS)
    hsl = [slice(h * HEAD_DIM, (h + 1) * HEAD_DIM) for h in heads]
    chunks = [slice(j * RET_CHUNK, (j + 1) * RET_CHUNK) for j in range(RET_T // RET_CHUNK)]
    items = [(rs, h) for rs in chunks for h in heads]

    def part(col, h):
        return slice(col - C_RQ + hsl[h].start, col - C_RQ + hsl[h].stop)

    def rope(x, rs):
        return x * cos_ref[rs, :] + pltpu.roll(x, HEAD_DIM // 2, 1) * sin_ref[rs, :]

    qb = [rope(x_ref[rs, part(C_RQ, h)], rs).astype(BF16) for rs, h in items]
    kr = [rope(x_ref[rs, part(C_RK, h)], rs) * (HEAD_DIM ** -0.5) for rs, h in items]
    kb = [x.astype(BF16) for x in kr]
    vb = [x_ref[rs, part(C_RV, h)].astype(BF16) for rs, h in items]
    inner = [(_dot_nt(q, k) * dec_ref[h]).astype(BF16) for q, k, (_, h) in zip(qb, kb, items)]
    kv = [_dot((k * zeta_ref[h]).T.astype(BF16), v) for k, v, (_, h) in zip(kr, vb, items)]
    o_in = [_dot(a, v) for a, v in zip(inner, vb)]
    st = [state_scr[h] for h in heads]
    for n, (rs, h) in enumerate(items):
        o = o_in[n] + _dot(qb[n], st[h].astype(BF16)) * xi_ref[h]
        st[h] = st[h] * math.exp(log_gamma[h] * RET_CHUNK) + kv[n]
        mu = jnp.mean(o, axis=-1, keepdims=True)
        oc = o - mu
        var = jnp.mean(oc * oc, axis=-1, keepdims=True)
        gate = x_ref[rs, part(C_RG, h)]
        o_ref[rs, hsl[h]] = (_silu(gate) * (oc * lax.rsqrt(var + EPS))).astype(BF16)
    for h in heads:
        state_scr[h] = st[h]


def _retention_tables(seq):
    half = HEAD_DIM // 2
    inv = 1.0 / (ROPE_BASE ** (jnp.arange(half, dtype=F32) / half))
    ang = jnp.arange(seq).astype(F32)[:, None] * inv[None, :]
    cos, sin = jnp.cos(ang), jnp.sin(ang)
    cos_t = jnp.concatenate([cos, cos], axis=-1)
    sin_t = jnp.concatenate([-sin, sin], axis=-1)
    c = RET_CHUNK
    log_gamma = jnp.log1p(-jnp.exp2(-5.0 - jnp.arange(N_HEADS, dtype=F32)))
    n = jnp.arange(c, dtype=F32)
    diff = n[:, None] - n[None, :]
    decay = jnp.where(diff >= 0, jnp.exp(log_gamma[:, None, None] * jnp.maximum(diff, 0.0)), 0.0)
    zeta = jnp.exp(log_gamma[:, None] * (c - 1 - n)[None, :])
    xi = jnp.exp(log_gamma[:, None] * (n + 1)[None, :])
    ones = jnp.ones((1, 1, HEAD_DIM), F32)
    return cos_t, sin_t, decay, zeta[:, :, None] * ones, xi[:, :, None] * ones


def retention(proj, tables, batch, seq):
    n = proj.shape[0]
    t = RET_T
    nc = seq // t
    cos_t, sin_t, decay, zeta, xi = tables
    full3 = pl.BlockSpec((N_HEADS, RET_CHUNK, HEAD_DIM), lambda b, i: (0, 0, 0))
    return pl.pallas_call(
        _retention_kernel,
        grid=(batch, nc),
        in_specs=[pl.BlockSpec((t, 4 * WIDTH), lambda b, i: (b * nc + i, C_RQ // (4 * WIDTH))),
                  pl.BlockSpec((t, HEAD_DIM), lambda b, i: (i, 0)),
                  pl.BlockSpec((t, HEAD_DIM), lambda b, i: (i, 0)),
                  full3, full3, full3],
        out_specs=pl.BlockSpec((t, WIDTH), lambda b, i: (b * nc + i, 0)),
        out_shape=jax.ShapeDtypeStruct((n, WIDTH), BF16),
        scratch_shapes=[pltpu.VMEM((N_HEADS, HEAD_DIM, HEAD_DIM), F32)],
        compiler_params=_cparams(("arbitrary", "arbitrary")),
        name="retention",
    )(proj, cos_t, sin_t, decay, zeta, xi)


def _fox_kernel(qi_ref, ki_ref, q_ref, kv_ref, ctr_ref, o_ref, m_scr, acc_scr, *, t):
    qi = qi_ref[pl.program_id(1)]
    ki = ki_ref[pl.program_id(1)]

    @pl.when(ki == 0)
    def _():
        m_scr[...] = jnp.full_like(m_scr, -jnp.inf)
        acc_scr[...] = jnp.zeros_like(acc_scr)

    def step(masked):
        if masked:
            row = lax.broadcasted_iota(I32, (t, t), 0)
            colm = lax.broadcasted_iota(I32, (t, t), 1)
            keep = row >= colm
        ones = jnp.ones((t, HEAD_DIM), BF16)
        for h in range(N_HEADS):
            sl = slice(h * HEAD_DIM, (h + 1) * HEAD_DIM)
            vsl = slice(WIDTH + h * HEAD_DIM, WIDTH + (h + 1) * HEAD_DIM)
            qb = q_ref[:, sl].astype(BF16)
            kb = kv_ref[:, sl].astype(BF16)
            c_k = ctr_ref[0, L_FF + h:L_FF + h + 1, :] * LOG2E
            s = _dot_nt(qb, kb) * (HEAD_DIM ** -0.5 * LOG2E) - c_k
            if masked:
                s = jnp.where(keep, s, -jnp.inf)
            m_old = m_scr[h]
            m_new = jnp.maximum(m_old, jnp.max(s, axis=-1, keepdims=True))
            alpha = jnp.exp2(m_old - m_new)
            p = jnp.exp2(s - m_new)
            v_aug = jnp.concatenate([kv_ref[:, vsl].astype(BF16), ones], axis=1)
            acc_scr[h] = alpha * acc_scr[h] + _dot(p.astype(BF16), v_aug)
            m_scr[h] = m_new

    @pl.when(ki < qi)
    def _():
        step(False)

    @pl.when(ki == qi)
    def _():
        step(True)
        for h in range(N_HEADS):
            sl = slice(h * HEAD_DIM, (h + 1) * HEAD_DIM)
            acc = acc_scr[h]
            o_ref[:, sl] = (acc[:, :HEAD_DIM] / acc[:, HEAD_DIM:]).astype(BF16)


def fox_attention(proj, prep_tr, batch, seq, *, t=1024):
    n = proj.shape[0]
    nt = seq // t
    pairs = [(qi, ki) for qi in range(nt) for ki in range(qi + 1)]
    qi_arr = jnp.asarray([p[0] for p in pairs], I32)
    ki_arr = jnp.asarray([p[1] for p in pairs], I32)
    qspec = pl.BlockSpec((t, WIDTH), lambda b, s, qi, ki: (b * nt + qi[s], C_FQ // WIDTH))
    kvspec = pl.BlockSpec((t, 2 * WIDTH), lambda b, s, qi, ki: (b * nt + ki[s], C_FK // (2 * WIDTH)))
    return pl.pallas_call(
        functools.partial(_fox_kernel, t=t),
        grid_spec=pltpu.PrefetchScalarGridSpec(
            num_scalar_prefetch=2,
            grid=(batch, len(pairs)),
            in_specs=[qspec, kvspec,
                      pl.BlockSpec((1, PREP_ROWS, t), lambda b, s, qi, ki: (b, 0, ki[s]))],
            out_specs=pl.BlockSpec((t, WIDTH), lambda b, s, qi, ki: (b * nt + qi[s], 0)),
            scratch_shapes=[pltpu.VMEM((N_HEADS, t, 1), F32),
                            pltpu.VMEM((N_HEADS, t, 2 * HEAD_DIM), F32)]),
        out_shape=jax.ShapeDtypeStruct((n, WIDTH), BF16),
        compiler_params=_cparams(("arbitrary", "arbitrary")),
        name="fox_attention",
    )(qi_arr, ki_arr, proj, proj, prep_tr)


def _dsa_proj_kernel(cq_ref, g_ref, wq_ref, wi_ref, q_ref, qi_ref):
    x = cq_ref[...]
    ms = jnp.mean(x * x, axis=-1, keepdims=True)
    cb = (x * lax.rsqrt(ms + EPS) * g_ref[...]).astype(BF16)
    q_ref[...] = _dot(cb, wq_ref[...]).astype(BF16)
    qi_ref[...] = _dot(cb, wi_ref[...]).astype(BF16)


def dsa_proj(proj, cq_norm, w_uq, w_qidx, *, tm=1024):
    n = proj.shape[0]
    r = DSA_Q_RANK
    wi = IDX_HEADS * IDX_DIM
    return pl.pallas_call(
        _dsa_proj_kernel,
        grid=(n // tm,),
        in_specs=[pl.BlockSpec((tm, r), lambda i: (i, C_DCQ // r)),
                  pl.BlockSpec((1, r), lambda i: (0, 0)),
                  pl.BlockSpec((r, WIDTH), lambda i: (0, 0)),
                  pl.BlockSpec((r, wi), lambda i: (0, 0))],
        out_specs=[pl.BlockSpec((tm, WIDTH), lambda i: (i, 0)),
                   pl.BlockSpec((tm, wi), lambda i: (i, 0))],
        out_shape=[jax.ShapeDtypeStruct((n, WIDTH), BF16),
                   jax.ShapeDtypeStruct((n, wi), BF16)],
        compiler_params=_cparams(("arbitrary",)),
        name="dsa_proj",
    )(proj, cq_norm.reshape(1, r), w_uq, w_qidx)


DSA_QB = 256
DSA_KC = 512
DSA_SCORE_MID_STEPS = 20
DSA_FEW_KEYS = 4
DSA_HALVE_FIXED = 12
DSA_WALK_FIXED = 3


def _score_to_key(s):
    b = pltpu.bitcast(s, I32)
    return b ^ ((b >> 31) & INT_MAX)


def _key_to_score(k):
    return pltpu.bitcast(k ^ ((k >> 31) & INT_MAX), F32)


def _t5_bucket(rel):
    max_exact = REL_BUCKETS // 2
    relf = jnp.maximum(rel, max_exact).astype(F32)
    large = max_exact + (jnp.log(relf / max_exact) / math.log(REL_MAX_DIST / max_exact)
                         * (REL_BUCKETS - max_exact)).astype(I32)
    large = jnp.minimum(large, REL_BUCKETS - 1)
    return jnp.where(rel < max_exact, rel, large)


def _dsa_kernel(rb_ref, q_ref, qi_ref, tok_ref, k_ref, v_ref, ka_ref, kb_ref, o_ref,
                key_scr, lg_scr, band_scr, kb16_scr, vt_scr, ka16_scr, kb16i_scr, *, seq, topk):
    qb_idx = pl.program_id(1)
    t0 = qb_idx * DSA_QB
    n_kc = (t0 + DSA_QB - 1) // DSA_KC + 1
    row_vec = (1, DSA_QB)

    @pl.when(qb_idx == 0)
    def _():
        kb16_scr[...] = k_ref[...].astype(BF16)
        ka16_scr[...] = ka_ref[...].astype(BF16)
        kb16i_scr[...] = kb_ref[...].astype(BF16)
        for c in range(seq // DSA_KC):
            cs = slice(c * DSA_KC, (c + 1) * DSA_KC)
            vt_scr[0:HEAD_DIM, cs] = v_ref[cs, :].T.astype(BF16)
        vt_scr[HEAD_DIM:, :] = jnp.ones((HEAD_DIM, seq), BF16)

    @pl.when((pl.program_id(0) == 0) & (qb_idx == 0))
    def _():
        j_ = lax.broadcasted_iota(I32, (2 * DSA_QB, DSA_QB), 0)
        i_ = lax.broadcasted_iota(I32, (2 * DSA_QB, DSA_QB), 1)
        rel = i_ + DSA_QB - j_
        bucket = _t5_bucket(rel)
        for h in range(N_HEADS):
            far = rb_ref[REL_BUCKETS - 1, h]
            band = jnp.zeros((2 * DSA_QB, DSA_QB), F32)
            for bk in range(REL_BUCKETS - 1):
                band = jnp.where(bucket == bk, (rb_ref[bk, h] - far) * LOG2E, band)
            band_scr[h] = jnp.where(rel >= 0, band, 0.0)

    w_t = tok_ref[...].T
    key_s = lax.broadcasted_iota(I32, (DSA_KC, DSA_QB), 0)
    row_t = t0 + lax.broadcasted_iota(I32, (DSA_KC, DSA_QB), 1)

    def score_chunk(c, carry):
        kmax, kmin = carry
        ks = pl.ds(pl.multiple_of(c * DSA_KC, DSA_KC), DSA_KC)
        ka = ka16_scr[ks, :]
        kb = kb16i_scr[ks, :]
        acc = jnp.zeros((DSA_KC, DSA_QB), F32)
        for p in range(IDX_HEADS // 2):
            qp = qi_ref[:, p * LANE:(p + 1) * LANE]
            acc = acc + jnp.maximum(_dot_nt(ka, qp), 0.0) * w_t[2 * p:2 * p + 1, :]
            acc = acc + jnp.maximum(_dot_nt(kb, qp), 0.0) * w_t[2 * p + 1:2 * p + 2, :]
        key = _score_to_key(acc)
        valid = (c * DSA_KC + key_s) <= row_t
        key_scr[ks, :] = jnp.where(valid, key, INT_MIN)
        kmax = jnp.maximum(kmax, jnp.max(jnp.where(valid, key, INT_MIN), axis=0, keepdims=True))
        kmin = jnp.minimum(kmin, jnp.min(jnp.where(valid, key, INT_MAX), axis=0, keepdims=True))
        return kmax, kmin

    kmax, kmin = lax.fori_loop(0, n_kc, score_chunk, (jnp.full(row_vec, INT_MIN, I32),
                                                     jnp.full(row_vec, INT_MAX, I32)))

    def scan_keys(cand, with_below):
        def body(c, carry):
            cnt, below = carry
            ks = pl.ds(pl.multiple_of(c * DSA_KC, DSA_KC), DSA_KC)
            keys = key_scr[ks, :]
            ge = keys >= cand
            ones = ge.astype(I32)
            low = jnp.where(ge, INT_MIN, keys)
            for u in range(DSA_KC // SUBLANE):
                us = slice(u * SUBLANE, (u + 1) * SUBLANE)
                cnt = cnt + ones[us, :]
                if with_below:
                    below = jnp.maximum(below, low[us, :])
            return cnt, below
        cnt, below = lax.fori_loop(0, n_kc, body, (jnp.zeros((SUBLANE, DSA_QB), I32),
                                                   jnp.full((SUBLANE, DSA_QB), INT_MIN, I32)))
        cnt = jnp.sum(cnt, axis=0, keepdims=True)
        if with_below:
            return cnt, jnp.max(below, axis=0, keepdims=True)
        return cnt

    def open_rows(lo, hi, c_lo):
        return (c_lo > topk) & (hi - 1 > lo)

    def any_row(flag):
        return jnp.max(jnp.where(flag, 1, 0))

    def update(cand, cnt, lo, hi, c_lo, c_hi):
        ge = cnt >= topk
        return (jnp.where(ge, cand, lo), jnp.where(ge, hi, cand),
                jnp.where(ge, cnt, c_lo), jnp.where(ge, c_hi, cnt))

    def crowded(lo, hi, c_lo, c_hi):
        return any_row(open_rows(lo, hi, c_lo) & (c_lo - c_hi > DSA_FEW_KEYS))

    def halve_step(it, lo, hi, c_lo, c_hi):
        key_mid = (lo >> 1) + (hi >> 1) + (lo & hi & 1)
        score_mid = _score_to_key(0.5 * _key_to_score(lo) + 0.5 * _key_to_score(hi - 1))
        cand = jnp.where(it < DSA_SCORE_MID_STEPS, score_mid, key_mid)
        cand = jnp.minimum(jnp.maximum(cand, lo + 1), hi - 1)
        cand = jnp.where(hi - 1 > lo, cand, lo)
        return update(cand, scan_keys(cand, False), lo, hi, c_lo, c_hi)

    def halve_body(st):
        it, _, lo, hi, c_lo, c_hi = st
        go = crowded(lo, hi, c_lo, c_hi)
        return (it + 1, go) + halve_step(it, lo, hi, c_lo, c_hi)

    def walk_step(lo, hi, c_lo, c_hi, nxt):
        is_open = open_rows(lo, hi, c_lo)
        cand = jnp.where(is_open, nxt, lo)
        cnt, below = scan_keys(cand, True)
        ge = cnt >= topk
        hi = jnp.where(is_open, jnp.where(ge, cand + 1, cand), hi)
        c_hi = jnp.where(is_open & jnp.logical_not(ge), cnt, c_hi)
        lo = jnp.where(is_open & ge, cand, lo)
        c_lo = jnp.where(is_open & ge, cnt, c_lo)
        nxt = jnp.where(ge, nxt, below)
        return lo, hi, c_lo, c_hi, nxt

    def walk_body(st):
        go = any_row(open_rows(st[1], st[2], st[3]))
        return (go,) + walk_step(*st[1:])

    n_valid = jnp.minimum(t0 + lax.broadcasted_iota(I32, row_vec, 1) + 1, seq)
    st = (kmin, kmax + 1, n_valid, jnp.zeros(row_vec, I32))
    st = lax.fori_loop(0, DSA_HALVE_FIXED, lambda it, s: halve_step(it, *s), st)
    st = lax.while_loop(lambda s: s[1] > 0, halve_body,
                        (jnp.int32(DSA_HALVE_FIXED), crowded(*st)) + st)[2:]
    _, nxt0 = scan_keys(st[1], True)
    st = lax.fori_loop(0, DSA_WALK_FIXED, lambda it, s: walk_step(*s), st + (nxt0,))
    _, thr, hi, n_ge, n_gt, _ = lax.while_loop(
        lambda s: s[0] > 0, walk_body, (any_row(open_rows(st[0], st[1], st[2])),) + st)

    tied = n_ge > topk
    has_tie = jnp.max(jnp.where(tied, 1, 0)) > 0

    @pl.when(has_tie)
    def _():
        room = (topk - n_gt).astype(F32)
        ii = lax.broadcasted_iota(I32, (LANE, LANE), 0)
        jj = lax.broadcasted_iota(I32, (LANE, LANE), 1)
        lower = (ii >= jj).astype(BF16)

        def body(c, seen):
            ks = pl.ds(pl.multiple_of(c * LANE, LANE), LANE)
            kk = key_scr[ks, :]
            eq = kk == thr
            rank = seen + _dot(lower, eq.astype(BF16))
            drop = eq & (rank > room) & tied
            key_scr[ks, :] = jnp.where(drop, INT_MIN, kk)
            return seen + jnp.sum(eq.astype(F32), axis=0, keepdims=True)
        lax.fori_loop(0, n_kc * (DSA_KC // LANE), body, jnp.zeros(row_vec, F32))

    def mask_chunk(c, carry):
        ks = pl.ds(pl.multiple_of(c * DSA_KC, DSA_KC), DSA_KC)
        sel = jnp.where(key_scr[ks, :] >= thr, 0.0, -jnp.inf).astype(F32)
        key_scr[ks, :] = pltpu.bitcast(sel, I32)
        return carry
    lax.fori_loop(0, n_kc, mask_chunk, 0)

    heads = range(N_HEADS)
    hsl = [slice(h * HEAD_DIM, (h + 1) * HEAD_DIM) for h in heads]

    def logit_chunk(c, ms):
        ks = pl.ds(pl.multiple_of(c * DSA_KC, DSA_KC), DSA_KC)
        k_c = kb16_scr[ks, :]
        sel = pltpu.bitcast(key_scr[ks, :], F32)
        out = []
        for h in heads:
            s = _dot_nt(k_c, q_ref[:, hsl[h]]) * (HEAD_DIM ** -0.5 * LOG2E) + sel
            lg_scr[h, ks, :] = s
            out.append(jnp.maximum(ms[h], jnp.max(s, axis=0, keepdims=True)))
        return tuple(out)
    ms = lax.fori_loop(0, n_kc, logit_chunk,
                       tuple(jnp.full(row_vec, -jnp.inf, F32) for _ in heads))

    band_off = pl.multiple_of(jnp.maximum(qb_idx - 1, 0) * DSA_QB, DSA_QB)
    ws = pl.ds(band_off, 2 * DSA_QB)
    ms = list(ms)
    for h in heads:
        band_h = band_scr[h]
        band_first = jnp.concatenate([band_h[DSA_QB:, :], jnp.zeros((DSA_QB, DSA_QB), F32)], axis=0)
        win = lg_scr[h, ws, :] + jnp.where(qb_idx == 0, band_first, band_h)
        lg_scr[h, ws, :] = win
        ms[h] = jnp.maximum(ms[h], jnp.max(win, axis=0, keepdims=True))

    def pv_chunk(c, accs):
        ks = pl.ds(pl.multiple_of(c * DSA_KC, DSA_KC), DSA_KC)
        vt_c = vt_scr[:, ks]
        return tuple(accs[h] + _dot(vt_c, jnp.exp2(lg_scr[h, ks, :] - ms[h]).astype(BF16))
                     for h in heads)
    accs = lax.fori_loop(0, n_kc, pv_chunk,
                         tuple(jnp.zeros((2 * HEAD_DIM, DSA_QB), F32) for _ in heads))
    for h in heads:
        o_ref[:, hsl[h]] = (accs[h][:HEAD_DIM, :] / accs[h][HEAD_DIM:, :]).T.astype(BF16)


def dsa_attention(proj, q, q_idx, prep_tok, rel_bias, batch, seq):
    n = proj.shape[0]
    nq = seq // DSA_QB
    topk = min(TOPK_MAX, seq // 4)
    wi = IDX_HEADS * IDX_DIM
    rowblk = lambda w, cb: pl.BlockSpec((DSA_QB, w), lambda b, i: (b * nq + i, cb))
    seqblk = lambda off: pl.BlockSpec((seq, LANE), lambda b, i: (b, off // LANE))
    return pl.pallas_call(
        functools.partial(_dsa_kernel, seq=seq, topk=topk),
        grid=(batch, nq),
        in_specs=[pl.BlockSpec(memory_space=pltpu.SMEM),
                  rowblk(WIDTH, 0), rowblk(wi, 0), rowblk(LANE, 0),
                  seqblk(C_DK), seqblk(C_DV), seqblk(C_KA), seqblk(C_KB)],
        out_specs=pl.BlockSpec((DSA_QB, WIDTH), lambda b, i: (b * nq + i, 0)),
        out_shape=jax.ShapeDtypeStruct((n, WIDTH), BF16),
        scratch_shapes=[pltpu.VMEM((seq, DSA_QB), I32),
                        pltpu.VMEM((N_HEADS, seq, DSA_QB), F32),
                        pltpu.VMEM((N_HEADS, 2 * DSA_QB, DSA_QB), F32),
                        pltpu.VMEM((seq, LANE), BF16),
                        pltpu.VMEM((2 * HEAD_DIM, seq), BF16),
                        pltpu.VMEM((seq, LANE), BF16),
                        pltpu.VMEM((seq, LANE), BF16)],
        compiler_params=_cparams(("arbitrary", "arbitrary")),
        name="dsa_attention",
    )(rel_bias, q, q_idx, prep_tok, proj, proj, proj, proj)


GDN_T = 256
GDN_GROUP = 2
GDN_HALO = 8


def _gdn_kernel(x_ref, cw_ref, ng_ref, tok_ref, tr_ref, o_ref,
                xq_scr, xk_scr, xv_scr, state_scr):
    first = pl.program_id(1) == 0

    @pl.when(first)
    def _():
        state_scr[...] = jnp.zeros_like(state_scr)
        for scr in (xq_scr, xk_scr, xv_scr):
            scr[0:GDN_HALO, :] = jnp.zeros((GDN_HALO, WIDTH), F32)

    def conv(col, scr, w_off):
        scr[GDN_HALO:, :] = x_ref[:, col - C_GQ:col - C_GQ + WIDTH]
        y = jnp.zeros((GDN_T, WIDTH), F32)
        for i in range(GDN_CONV):
            st = GDN_HALO - (GDN_CONV - 1) + i
            y = y + scr[st:st + GDN_T, :] * cw_ref[i:i + 1, w_off:w_off + WIDTH]
        scr[0:GDN_HALO, :] = scr[GDN_T:GDN_T + GDN_HALO, :]
        return _silu(y)

    qc = conv(C_GQ, xq_scr, 0)
    kc = conv(C_GK, xk_scr, WIDTH)
    vc = conv(C_GV, xv_scr, 2 * WIDTH)
    z_off = C_GZ - C_GQ
    tok = tok_ref[...]
    c = GDN_CHUNK
    heads = range(N_HEADS)
    hsl = [slice(h * HEAD_DIM, (h + 1) * HEAD_DIM) for h in heads]

    def l2norm_heads(x, scale):
        return jnp.concatenate(
            [x[:, s] * (lax.rsqrt(jnp.sum(x[:, s] * x[:, s], axis=-1, keepdims=True) + EPS) * scale)
             for s in hsl], axis=1)

    qf = l2norm_heads(qc, HEAD_DIM ** -0.5)
    kf = l2norm_heads(kc, 1.0)

    grp = GDN_GROUP
    gw = grp * HEAD_DIM
    nb = grp * c
    ri = lax.broadcasted_iota(I32, (nb, nb), 0)
    ci = lax.broadcasted_iota(I32, (nb, nb), 1)
    tril = ((ri // c) == (ci // c)) & (ri >= ci)
    eye = (ri == ci).astype(F32)
    pair_masks = []
    for lg in range(c.bit_length() - 1):
        pair_masks.append(((ri >> (lg + 1)) == (ci >> (lg + 1)))
                          & (((ri >> lg) & 1) == 1) & (((ci >> lg) & 1) == 0))
    lane_head = lax.broadcasted_iota(I32, (c, gw), 1) // HEAD_DIM
    row_head = lax.broadcasted_iota(I32, (nb, HEAD_DIM), 0) // c

    def spread(x):
        return jnp.concatenate([jnp.where(lane_head == u, x, 0.0) for u in range(grp)], axis=0)

    def stack(x):
        return jnp.concatenate([x[:, hsl[u]] for u in range(grp)], axis=0)

    def spread_lanes(x):
        return jnp.concatenate([jnp.where(row_head == u, x, 0.0) for u in range(grp)], axis=1)

    def split(x):
        hi = x.astype(BF16)
        return hi, (x - hi.astype(F32)).astype(BF16)

    def dot_split(a, b):
        return _dot(a[0], b[0]) + (_dot(a[0], b[1]) + _dot(a[1], b[0]))

    items = [(j, gi) for j in range(GDN_T // c) for gi in range(N_HEADS // grp)]
    pre = []
    for j, gi in items:
        rs = slice(j * c, (j + 1) * c)
        last = slice((j + 1) * c - 1, (j + 1) * c)
        gh = [gi * grp + u for u in range(grp)]
        gsl = slice(gi * gw, (gi + 1) * gw)
        qj, kj, vj = qf[rs, gsl], kf[rs, gsl], vc[rs, gsl]
        b_col = jnp.concatenate([tok[rs, L_GB + h:L_GB + h + 1] for h in gh], axis=0)
        g_col = jnp.concatenate([tok[rs, L_GA + h:L_GA + h + 1] for h in gh], axis=0)
        g_row = jnp.concatenate([tr_ref[0, L_GA + h:L_GA + h + 1, rs] for h in gh], axis=1)
        g_last = [tok[last, L_GA + h:L_GA + h + 1] for h in gh]
        g_last_col = jnp.concatenate([jnp.broadcast_to(g, (c, 1)) for g in g_last], axis=0)
        k_sp = spread(kj)
        q_sp = spread(qj)
        k_sp16 = k_sp.astype(BF16)
        decay = jnp.exp(jnp.where(tril, g_col - g_row, -jnp.inf))
        eg = jnp.exp(g_col)
        pre.append(dict(
            gh=gh, gsl=gsl,
            l_mat=b_col * _dot_nt(k_sp16, k_sp16) * decay,
            rhs=jnp.concatenate([stack(vj) * b_col, stack(kj) * (b_col * eg)], axis=1),
            qk=_dot_nt(q_sp.astype(BF16), k_sp16) * decay,
            q_dec=q_sp * eg,
            k_dec=k_sp * jnp.exp(g_last_col - g_col),
            e_last=jnp.concatenate([jnp.broadcast_to(jnp.exp(g), (HEAD_DIM, 1)) for g in g_last], axis=0)))

    t_inv = [eye - jnp.where(pair_masks[0], p["l_mat"], 0.0) for p in pre]
    for pm in pair_masks[1:]:
        t_s = [split(t) for t in t_inv]
        m_t = [dot_split(split(jnp.where(pm, p["l_mat"], 0.0)), ts) for p, ts in zip(pre, t_s)]
        t_inv = [t - dot_split(ts, split(m)) for t, ts, m in zip(t_inv, t_s, m_t)]
    sols = [dot_split(split(t), split(p["rhs"])) for t, p in zip(t_inv, pre)]

    outs = [[] for _ in heads]
    for p, sol in zip(pre, sols):
        u0 = sol[:, :HEAD_DIM]
        kcum = sol[:, HEAD_DIM:]
        st = state_scr[p["gsl"], :]
        stb = st.astype(BF16)
        v_new = u0 - _dot(spread_lanes(kcum).astype(BF16), stb)
        v_new_b = v_new.astype(BF16)
        o_st = _dot(p["q_dec"].astype(BF16), stb) + _dot(p["qk"].astype(BF16), v_new_b)
        state_scr[p["gsl"], :] = st * p["e_last"] + _dot(p["k_dec"].T.astype(BF16), v_new_b)
        for u, h in enumerate(p["gh"]):
            outs[h].append(o_st[u * c:(u + 1) * c, :])

    for h in heads:
        o = jnp.concatenate(outs[h], axis=0)
        ms = jnp.mean(o * o, axis=-1, keepdims=True)
        on = o * lax.rsqrt(ms + EPS) * ng_ref[...]
        z = x_ref[:, z_off + hsl[h].start:z_off + hsl[h].stop]
        o_ref[:, hsl[h]] = (on * _silu(z)).astype(BF16)


def gated_deltanet(proj, prep_tok, prep_tr, conv_w, norm_g, batch, seq):
    n = proj.shape[0]
    t = GDN_T
    nt = seq // t
    return pl.pallas_call(
        _gdn_kernel,
        grid=(batch, nt),
        in_specs=[pl.BlockSpec((t, 4 * WIDTH), lambda b, i: (b * nt + i, C_GQ // (4 * WIDTH))),
                  pl.BlockSpec((GDN_CONV, 3 * WIDTH), lambda b, i: (0, 0)),
                  pl.BlockSpec((1, HEAD_DIM), lambda b, i: (0, 0)),
                  pl.BlockSpec((t, LANE), lambda b, i: (b * nt + i, 0)),
                  pl.BlockSpec((1, PREP_ROWS, t), lambda b, i: (b, 0, i))],
        out_specs=pl.BlockSpec((t, WIDTH), lambda b, i: (b * nt + i, 0)),
        out_shape=jax.ShapeDtypeStruct((n, WIDTH), BF16),
        scratch_shapes=[pltpu.VMEM((t + GDN_HALO, WIDTH), F32),
                        pltpu.VMEM((t + GDN_HALO, WIDTH), F32),
                        pltpu.VMEM((t + GDN_HALO, WIDTH), F32),
                        pltpu.VMEM((N_HEADS * HEAD_DIM, HEAD_DIM), F32)],
        compiler_params=_cparams(("arbitrary", "arbitrary")),
        name="gated_deltanet",
    )(proj, conv_w, norm_g.reshape(1, HEAD_DIM), prep_tok, prep_tr)


def _merge_kernel(h_ref, b0_ref, b1_ref, b2_ref, b3_ref, g0_ref, g1_ref, g2_ref, g3_ref,
                  wb_ref, o_ref):
    h = h_ref[...]
    acc = None
    for n, (b_ref, g_ref) in enumerate(zip((b0_ref, b1_ref, b2_ref, b3_ref),
                                           (g0_ref, g1_ref, g2_ref, g3_ref))):
        gate = jax.nn.sigmoid(_dot(h, g_ref[...]))
        term = gate * _dot(b_ref[...], wb_ref[n])
        acc = term if acc is None else acc + term
    o_ref[...] = acc.astype(BF16)


def merge_branches(h, branches, w_gate, w_branch, *, tm=512, tn=512):
    _, n, d = h.shape
    nj = d // tn
    bspec = pl.BlockSpec((tm, WIDTH), lambda j, i: (i, 0))
    gspec = lambda k: pl.BlockSpec((d, tn), lambda j, i: (0, k * nj + j))
    return pl.pallas_call(
        _merge_kernel,
        grid=(nj, n // tm),
        in_specs=[pl.BlockSpec((None, tm, d), lambda j, i: (0, i, 0)),
                  bspec, bspec, bspec, bspec,
                  gspec(0), gspec(1), gspec(2), gspec(3),
                  pl.BlockSpec((N_BRANCH, WIDTH, tn), lambda j, i: (0, 0, j))],
        out_specs=pl.BlockSpec((tm, tn), lambda j, i: (i, j)),
        out_shape=jax.ShapeDtypeStruct((n, d), BF16),
        compiler_params=_cparams(("arbitrary", "arbitrary")),
        name="merge_branches",
    )(h, *branches, w_gate, w_gate, w_gate, w_gate, w_branch)


def _resid_mm_kernel(a_ref, w_ref, x_ref, o_ref):
    o_ref[...] = x_ref[...] + _dot(a_ref[...], w_ref[...])


def resid_matmul(a, w, x, *, tm=512, tn=1024, name="resid_matmul"):
    n, k = a.shape
    d = w.shape[1]
    return pl.pallas_call(
        _resid_mm_kernel,
        grid=(d // tn, n // tm),
        in_specs=[pl.BlockSpec((tm, k), lambda j, i: (i, 0)),
                  pl.BlockSpec((k, tn), lambda j, i: (0, j)),
                  pl.BlockSpec((tm, tn), lambda j, i: (i, j))],
        out_specs=pl.BlockSpec((tm, tn), lambda j, i: (i, j)),
        out_shape=jax.ShapeDtypeStruct((n, d), F32),
        compiler_params=_cparams(("arbitrary", "arbitrary")),
        name=name,
    )(a, w, x)


FFN_HALO = 8
FFN_SUB = 512


def _ffn1_kernel(x_ref, g_ref, wg_ref, wu_ref, cw_ref, cb_ref, o_ref, h_scr, gt_scr, halo_scr,
                 *, tm, tiles_per_seq):
    i = pl.program_id(1)

    @pl.when(i == 0)
    def _():
        halo_scr[...] = jnp.zeros_like(halo_scr)

    def body(r, carry):
        rows = pl.ds(pl.multiple_of(r * NORM_ROWS, NORM_ROWS), NORM_ROWS)
        h_scr[rows, :] = _rmsnorm_rows(x_ref, g_ref, rows).astype(BF16)
        return carry
    lax.fori_loop(0, tm // NORM_ROWS, body, 0)

    h = h_scr[...]
    seq_start = (i % tiles_per_seq) == 0
    tn = o_ref.shape[1]
    for off in range(0, tn, FFN_SUB):
        cs = slice(off, min(off + FFN_SUB, tn))
        g = _dot(h, wg_ref[:, cs])
        gt_scr[FFN_HALO:, cs] = g
        gt_scr[0:FFN_HALO, cs] = jnp.where(seq_start, 0.0, halo_scr[:, cs])
        halo_scr[:, cs] = g[tm - FFN_HALO:, :]
        y = cb_ref[:, cs] + g * cw_ref[FFN_CONV - 1:FFN_CONV, cs]
        for t in range(FFN_CONV - 1):
            st = FFN_HALO - (FFN_CONV - 1) + t
            y = y + gt_scr[st:st + tm, cs] * cw_ref[t:t + 1, cs]
        o_ref[:, cs] = (_silu(y) * _dot(h, wu_ref[:, cs])).astype(BF16)


def conv_ffn_up(x, gain, w_gate, w_up, conv_w, conv_b, seq, *, tm=512, col_parts=2):
    n, d = x.shape
    f = w_gate.shape[1]
    tn = f // col_parts
    wspec = pl.BlockSpec((d, tn), lambda j, i: (0, j), pipeline_mode=pl.Buffered(1))
    return pl.pallas_call(
        functools.partial(_ffn1_kernel, tm=tm, tiles_per_seq=seq // tm),
        grid=(col_parts, n // tm),
        in_specs=[pl.BlockSpec((tm, d), lambda j, i: (i, 0)),
                  pl.BlockSpec((1, d), lambda j, i: (0, 0)),
                  wspec, wspec,
                  pl.BlockSpec((FFN_CONV, tn), lambda j, i: (0, j)),
                  pl.BlockSpec((1, tn), lambda j, i: (0, j))],
        out_specs=pl.BlockSpec((tm, tn), lambda j, i: (i, j)),
        out_shape=jax.ShapeDtypeStruct((n, f), BF16),
        scratch_shapes=[pltpu.VMEM((tm, d), BF16),
                        pltpu.VMEM((tm + FFN_HALO, tn), F32),
                        pltpu.VMEM((FFN_HALO, tn), F32)],
        compiler_params=_cparams(("arbitrary", "arbitrary")),
        name="conv_ffn_up",
    )(x, gain.reshape(1, d), w_gate, w_up, conv_w, conv_b.reshape(1, f))


IN_SIZES = (WIDTH, WIDTH, WIDTH, WIDTH,
            DSA_Q_RANK, HEAD_DIM, HEAD_DIM, IDX_DIM, IDX_HEADS,
            WIDTH, WIDTH, WIDTH, N_HEADS,
            WIDTH, WIDTH, WIDTH, WIDTH, N_HEADS, N_HEADS)
IN_NAMES = ("r_q", "r_k", "r_v", "r_g", "d_cq", "d_k", "d_v", "i_k", "i_w",
            "f_q", "f_k", "f_v", "f_f", "g_q", "g_k", "g_v", "g_z", "g_b", "g_a")
IN_PLAN = (("r_q", C_RQ), ("r_k", C_RK), ("r_v", C_RV), ("r_g", C_RG),
           ("f_q", C_FQ), ("f_k", C_FK), ("f_v", C_FV),
           ("g_q", C_GQ), ("g_k", C_GK), ("g_v", C_GV), ("g_z", C_GZ),
           ("d_k", C_DK), ("d_cq", C_DCQ), ("d_v", C_DV),
           ("i_k", C_KA), ("i_k", C_KB + IDX_DIM),
           ("i_w", C_SM + L_IW), ("f_f", C_SM + L_FF), ("g_b", C_SM + L_GB), ("g_a", C_SM + L_GA))


def _prep_w_in_kernel(w_ref, m_ref, g_ref):
    src = {}
    off = 0
    for name, size in zip(IN_NAMES, IN_SIZES):
        src[name] = (off, size)
        off += size
    m_ref[...] = jnp.zeros_like(m_ref)
    for name, dst in IN_PLAN:
        so, w = src[name]
        m_ref[:, dst:dst + w] = w_ref[:, so:so + w].astype(BF16)
    g_ref[...] = w_ref[:, off:off + g_ref.shape[1]].astype(BF16)


def prep_w_in(w_in, layer, *, tr=128):
    _, d, c = w_in.shape
    return pl.pallas_call(
        _prep_w_in_kernel,
        grid=(d // tr,),
        in_specs=[pl.BlockSpec((None, tr, c), lambda i: (layer, i, 0))],
        out_specs=[pl.BlockSpec((tr, C_TOT), lambda i: (i, 0)),
                   pl.BlockSpec((tr, N_BRANCH * d), lambda i: (i, 0))],
        out_shape=[jax.ShapeDtypeStruct((d, C_TOT), BF16),
                   jax.ShapeDtypeStruct((d, N_BRANCH * d), BF16)],
        compiler_params=_cparams(("arbitrary",)),
        name="prep_w_in",
    )(w_in)


def _cast_kernel(w_ref, o_ref):
    o_ref[...] = w_ref[...].astype(BF16)


def cast_layer(w, layer, *, tr=256):
    _, r, c = w.shape
    if r % tr:
        tr = r
    return pl.pallas_call(
        _cast_kernel,
        grid=(r // tr,),
        in_specs=[pl.BlockSpec((None, tr, c), lambda i: (layer, i, 0))],
        out_specs=pl.BlockSpec((tr, c), lambda i: (i, 0)),
        out_shape=jax.ShapeDtypeStruct((r, c), BF16),
        compiler_params=_cparams(("arbitrary",)),
        name="cast_bf16",
    )(w)


def cast_branch(w_branch, layer):
    _, nbr, r, c = w_branch.shape
    return pl.pallas_call(
        _cast_kernel,
        grid=(nbr,),
        in_specs=[pl.BlockSpec((None, None, r, c), lambda i: (layer, i, 0, 0))],
        out_specs=pl.BlockSpec((None, r, c), lambda i: (i, 0, 0)),
        out_shape=jax.ShapeDtypeStruct((nbr, r, c), BF16),
        compiler_params=_cparams(("arbitrary",)),
        name="cast_branch",
    )(w_branch)


def kernel(x, norm_mix, w_in, dsa_cq_norm, dsa_w_uq, dsa_w_qidx, fox_f_bias, gdn_conv, gdn_a_log,
           gdn_dt_bias, gdn_norm, w_branch, w_out, rel_bias, norm_ffn, ffn_w_gate, ffn_w_up,
           ffn_conv, ffn_conv_b, ffn_w_down, final_norm):
    batch, seq, d = x.shape
    depth = w_in.shape[0]
    xf = x.reshape(batch * seq, d)
    ret_tables = _retention_tables(seq)
    for l in range(depth):
        w_main, w_gate = prep_w_in(w_in, l)
        proj, h = norm_proj(xf, norm_mix[l], w_main)
        par = jnp.zeros((SUBLANE, LANE), F32)
        par = par.at[0, L_FF:L_FF + N_HEADS].set(fox_f_bias[l])
        par = par.at[0, L_GA:L_GA + N_HEADS].set(gdn_dt_bias[l])
        par = par.at[1, L_GA:L_GA + N_HEADS].set(gdn_a_log[l])
        prep_tok, prep_tr = prep_small(proj, par, batch, seq)
        o_ret = retention(proj, ret_tables, batch, seq)
        q_dsa, q_idx = dsa_proj(proj, dsa_cq_norm[l], cast_layer(dsa_w_uq, l), cast_layer(dsa_w_qidx, l))
        o_dsa = dsa_attention(proj, q_dsa, q_idx, prep_tok, rel_bias, batch, seq)
        o_fox = fox_attention(proj, prep_tr, batch, seq)
        o_gdn = gated_deltanet(proj, prep_tok, prep_tr, gdn_conv[l], gdn_norm[l], batch, seq)
        merged = merge_branches(h, (o_ret, o_dsa, o_fox, o_gdn), w_gate, cast_branch(w_branch, l))
        xf = resid_matmul(merged, cast_layer(w_out, l), xf, tn=d, name="out_proj")
        act = conv_ffn_up(xf, norm_ffn[l], cast_layer(ffn_w_gate, l), cast_layer(ffn_w_up, l),
                          ffn_conv[l], ffn_conv_b[l], seq)
        xf = resid_matmul(act, cast_layer(ffn_w_down, l), xf, name="ffn_down")
    return rmsnorm(xf, final_norm).reshape(batch, seq, d)
```

```python
import functools
import math

import jax
import jax.numpy as jnp
from jax import lax
from jax.experimental import pallas as pl
from jax.experimental.pallas import tpu as pltpu

F32 = jnp.float32
BF16 = jnp.bfloat16
I32 = jnp.int32

HEAD_DIM = 128
N_HEADS = 4
WIDTH = N_HEADS * HEAD_DIM
N_BRANCH = 4
RET_CHUNK = 128
ROPE_BASE = 10000.0
DSA_Q_RANK = 384
IDX_HEADS = 16
IDX_DIM = 64
TOPK_MAX = 256
GDN_CONV = 4
GDN_CHUNK = 64
REL_BUCKETS = 32
REL_MAX_DIST = 128
FFN_CONV = 3
EPS = 1e-6

LANE = 128
SUBLANE = 8
VMEM_LIMIT = 56 * 1024 * 1024

C_RQ, C_RK, C_RV, C_RG = 0, 512, 1024, 1536
C_FQ, C_DK, C_DCQ = 2048, 2560, 2688
C_FK, C_FV = 3072, 3584
C_GQ, C_GK, C_GV, C_GZ = 4096, 4608, 5120, 5632
C_DV, C_KA, C_KB, C_SM = 6144, 6272, 6400, 6528
C_TOT = 6656
L_IW, L_FF, L_GB, L_GA = 0, 16, 20, 24

LOG2E = 1.4426950408889634
INT_MIN = -(2 ** 31)
INT_MAX = 2 ** 31 - 1
HIGHEST = lax.Precision.HIGHEST


def _cparams(sem, vmem=VMEM_LIMIT):
    return pltpu.CompilerParams(dimension_semantics=sem, vmem_limit_bytes=vmem)


def _dot(a, b):
    return jnp.dot(a, b, preferred_element_type=F32)


def _dot_nt(a, b):
    return lax.dot_general(a, b, (((1,), (1,)), ((), ())), preferred_element_type=F32)


def _silu(x):
    return x * jax.nn.sigmoid(x)


NORM_ROWS = 128


def _rmsnorm_rows(x_ref, g_ref, rows):
    x = x_ref[rows, :]
    ms = jnp.mean(x * x, axis=-1, keepdims=True)
    return x * lax.rsqrt(ms + EPS) * g_ref[...]


PROJ_SUB = 512


def _norm_proj_kernel(x_ref, g_ref, w_ref, o_ref, h_ref, h_scr, *, tm):
    def body(r, carry):
        rows = pl.ds(pl.multiple_of(r * NORM_ROWS, NORM_ROWS), NORM_ROWS)
        hb = _rmsnorm_rows(x_ref, g_ref, rows).astype(BF16)
        h_scr[rows, :] = hb
        h_ref[rows, :] = hb
        return carry
    lax.fori_loop(0, tm // NORM_ROWS, body, 0)

    h = h_scr[...]
    tn = o_ref.shape[1]
    for off in range(0, tn, PROJ_SUB):
        cs = slice(off, min(off + PROJ_SUB, tn))
        o_ref[:, cs] = _dot(h, w_ref[:, cs])


def norm_proj(x, gain, w, *, tm=512, col_parts=2):
    n, d = x.shape
    c = w.shape[1]
    tn = c // col_parts
    return pl.pallas_call(
        functools.partial(_norm_proj_kernel, tm=tm),
        grid=(col_parts, n // tm),
        in_specs=[pl.BlockSpec((tm, d), lambda j, i: (i, 0)),
                  pl.BlockSpec((1, d), lambda j, i: (0, 0)),
                  pl.BlockSpec((d, tn), lambda j, i: (0, j), pipeline_mode=pl.Buffered(1))],
        out_specs=[pl.BlockSpec((tm, tn), lambda j, i: (i, j)),
                   pl.BlockSpec((None, tm, d), lambda j, i: (j, i, 0))],
        out_shape=[jax.ShapeDtypeStruct((n, c), F32),
                   jax.ShapeDtypeStruct((col_parts, n, d), BF16)],
        scratch_shapes=[pltpu.VMEM((tm, d), BF16)],
        compiler_params=_cparams(("arbitrary", "arbitrary")),
        name="norm_proj",
    )(x, gain.reshape(1, d), w)


def _rmsnorm_kernel(x_ref, g_ref, o_ref, *, tm):
    def body(r, carry):
        rows = pl.ds(pl.multiple_of(r * NORM_ROWS, NORM_ROWS), NORM_ROWS)
        o_ref[rows, :] = _rmsnorm_rows(x_ref, g_ref, rows)
        return carry
    lax.fori_loop(0, tm // NORM_ROWS, body, 0)


def rmsnorm(x, gain, *, tm=512):
    n, d = x.shape
    return pl.pallas_call(
        functools.partial(_rmsnorm_kernel, tm=tm),
        grid=(n // tm,),
        in_specs=[pl.BlockSpec((tm, d), lambda i: (i, 0)),
                  pl.BlockSpec((1, d), lambda i: (0, 0))],
        out_specs=pl.BlockSpec((tm, d), lambda i: (i, 0)),
        out_shape=jax.ShapeDtypeStruct((n, d), F32),
        compiler_params=_cparams(("arbitrary",)),
        name="final_rmsnorm",
    )(x, gain.reshape(1, d))


def _prep_kernel(s_ref, par_ref, tok_ref, tr_ref, carry_scr):
    @pl.when(pl.program_id(1) == 0)
    def _():
        carry_scr[...] = jnp.zeros_like(carry_scr)

    s = s_ref[...]
    lane = lax.broadcasted_iota(I32, (LANE, LANE), 1)
    row = lax.broadcasted_iota(I32, (LANE, LANE), 0)
    z = s + par_ref[0:1, :]
    soft = jnp.maximum(z, 0.0) + jnp.log1p(jnp.exp(-jnp.abs(z)))
    log_sig = z - soft
    sig = jax.nn.sigmoid(z)
    g_val = -jnp.exp(par_ref[1:2, :]) * soft
    is_f = (lane[0:1] >= L_FF) & (lane[0:1] < L_FF + N_HEADS)
    is_b = (lane[0:1] >= L_GB) & (lane[0:1] < L_GB + N_HEADS)
    is_a = (lane[0:1] >= L_GA) & (lane[0:1] < L_GA + N_HEADS)
    pre = jnp.where(is_f, log_sig, jnp.where(is_a, g_val, 0.0))
    tri = (row >= lane).astype(F32)
    tri_blk = ((row >= lane) & ((row // GDN_CHUNK) == (lane // GDN_CHUNK))).astype(F32)
    subs = [slice(u * LANE, (u + 1) * LANE) for u in range(PREP_T // LANE)]
    cum_full = [jnp.dot(tri, pre[u], precision=HIGHEST, preferred_element_type=F32) for u in subs]
    cum_blk = [jnp.dot(tri_blk, pre[u], precision=HIGHEST, preferred_element_type=F32) for u in subs]
    scale_iw = IDX_HEADS ** -0.5 * IDX_DIM ** -0.5
    carry = carry_scr[0:1, :]
    for u, cf, cb in zip(subs, cum_full, cum_blk):
        c_fox = cf + carry
        carry = c_fox[LANE - 1:LANE, :]
        out = jnp.where(is_f, c_fox,
                        jnp.where(is_a, cb,
                                  jnp.where(is_b, sig[u],
                                            jnp.where(lane[0:1] < IDX_HEADS, s[u] * scale_iw, 0.0))))
        tok_ref[u, :] = out
        tr_ref[0, :, u] = out.T[0:PREP_ROWS, :]
    carry_scr[0:1, :] = carry


PREP_T = 512
PREP_ROWS = 32


def prep_small(proj, par, batch, seq):
    n = proj.shape[0]
    nc = seq // PREP_T
    return pl.pallas_call(
        _prep_kernel,
        grid=(batch, nc),
        in_specs=[pl.BlockSpec((PREP_T, LANE), lambda b, c: (b * nc + c, C_SM // LANE)),
                  pl.BlockSpec((SUBLANE, LANE), lambda b, c: (0, 0))],
        out_specs=[pl.BlockSpec((PREP_T, LANE), lambda b, c: (b * nc + c, 0)),
                   pl.BlockSpec((1, PREP_ROWS, PREP_T), lambda b, c: (b, 0, c))],
        out_shape=[jax.ShapeDtypeStruct((n, LANE), F32),
                   jax.ShapeDtypeStruct((batch, PREP_ROWS, seq), F32)],
        scratch_shapes=[pltpu.VMEM((SUBLANE, LANE), F32)],
        compiler_params=_cparams(("arbitrary", "arbitrary")),
        name="prep_small",
    )(proj, par)


RET_T = 512


def _ret_gamma():
    return [math.log1p(-(2.0 ** (-5.0 - h))) for h in range(N_HEADS)]


def _retention_kernel(x_ref, cos_ref, sin_ref, dec_ref, zeta_ref, xi_ref, o_ref, state_scr):
    @pl.when(pl.program_id(1) == 0)
    def _():
        state_scr[...] = jnp.zeros_like(state_scr)

    log_gamma = _ret_gamma()
    heads = range(N_HEADS)
    hsl = [slice(h * HEAD_DIM, (h + 1) * HEAD_DIM) for h in heads]
    chunks = [slice(j * RET_CHUNK, (j + 1) * RET_CHUNK) for j in range(RET_T // RET_CHUNK)]
    items = [(rs, h) for rs in chunks for h in heads]

    def part(col, h):
        return slice(col - C_RQ + hsl[h].start, col - C_RQ + hsl[h].stop)

    def rope(x, rs):
        return x * cos_ref[rs, :] + pltpu.roll(x, HEAD_DIM // 2, 1) * sin_ref[rs, :]

    qb = [rope(x_ref[rs, part(C_RQ, h)], rs).astype(BF16) for rs, h in items]
    kr = [rope(x_ref[rs, part(C_RK, h)], rs) * (HEAD_DIM ** -0.5) for rs, h in items]
    kb = [x.astype(BF16) for x in kr]
    vb = [x_ref[rs, part(C_RV, h)].astype(BF16) for rs, h in items]
    inner = [(_dot_nt(q, k) * dec_ref[h]).astype(BF16) for q, k, (_, h) in zip(qb, kb, items)]
    kv = [_dot((k * zeta_ref[h]).T.astype(BF16), v) for k, v, (_, h) in zip(kr, vb, items)]
    o_in = [_dot(a, v) for a, v in zip(inner, vb)]
    st = [state_scr[h] for h in heads]
    for n, (rs, h) in enumerate(items):
        o = o_in[n] + _dot(qb[n], st[h].astype(BF16)) * xi_ref[h]
        st[h] = st[h] * math.exp(log_gamma[h] * RET_CHUNK) + kv[n]
        mu = jnp.mean(o, axis=-1, keepdims=True)
        oc = o - mu
        var = jnp.mean(oc * oc, axis=-1, keepdims=True)
        gate = x_ref[rs, part(C_RG, h)]
        o_ref[rs, hsl[h]] = (_silu(gate) * (oc * lax.rsqrt(var + EPS))).astype(BF16)
    for h in heads:
        state_scr[h] = st[h]


def _retention_tables(seq):
    half = HEAD_DIM // 2
    inv = 1.0 / (ROPE_BASE ** (jnp.arange(half, dtype=F32) / half))
    ang = jnp.arange(seq).astype(F32)[:, None] * inv[None, :]
    cos, sin = jnp.cos(ang), jnp.sin(ang)
    cos_t = jnp.concatenate([cos, cos], axis=-1)
    sin_t = jnp.concatenate([-sin, sin], axis=-1)
    c = RET_CHUNK
    log_gamma = jnp.log1p(-jnp.exp2(-5.0 - jnp.arange(N_HEADS, dtype=F32)))
    n = jnp.arange(c, dtype=F32)
    diff = n[:, None] - n[None, :]
    decay = jnp.where(diff >= 0, jnp.exp(log_gamma[:, None, None] * jnp.maximum(diff, 0.0)), 0.0)
    zeta = jnp.exp(log_gamma[:, None] * (c - 1 - n)[None, :])
    xi = jnp.exp(log_gamma[:, None] * (n + 1)[None, :])
    ones = jnp.ones((1, 1, HEAD_DIM), F32)
    return cos_t, sin_t, decay, zeta[:, :, None] * ones, xi[:, :, None] * ones


def retention(proj, tables, batch, seq):
    n = proj.shape[0]
    t = RET_T
    nc = seq // t
    cos_t, sin_t, decay, zeta, xi = tables
    full3 = pl.BlockSpec((N_HEADS, RET_CHUNK, HEAD_DIM), lambda b, i: (0, 0, 0))
    return pl.pallas_call(
        _retention_kernel,
        grid=(batch, nc),
        in_specs=[pl.BlockSpec((t, 4 * WIDTH), lambda b, i: (b * nc + i, C_RQ // (4 * WIDTH))),
                  pl.BlockSpec((t, HEAD_DIM), lambda b, i: (i, 0)),
                  pl.BlockSpec((t, HEAD_DIM), lambda b, i: (i, 0)),
                  full3, full3, full3],
        out_specs=pl.BlockSpec((t, WIDTH), lambda b, i: (b * nc + i, 0)),
        out_shape=jax.ShapeDtypeStruct((n, WIDTH), BF16),
        scratch_shapes=[pltpu.VMEM((N_HEADS, HEAD_DIM, HEAD_DIM), F32)],
        compiler_params=_cparams(("arbitrary", "arbitrary")),
        name="retention",
    )(proj, cos_t, sin_t, decay, zeta, xi)


def _fox_kernel(qi_ref, ki_ref, q_ref, kv_ref, ctr_ref, o_ref, m_scr, acc_scr, *, t):
    qi = qi_ref[pl.program_id(1)]
    ki = ki_ref[pl.program_id(1)]

    @pl.when(ki == 0)
    def _():
        m_scr[...] = jnp.full_like(m_scr, -jnp.inf)
        acc_scr[...] = jnp.zeros_like(acc_scr)

    def step(masked):
        if masked:
            row = lax.broadcasted_iota(I32, (t, t), 0)
            colm = lax.broadcasted_iota(I32, (t, t), 1)
            keep = row >= colm
        ones = jnp.ones((t, HEAD_DIM), BF16)
        for h in range(N_HEADS):
            sl = slice(h * HEAD_DIM, (h + 1) * HEAD_DIM)
            vsl = slice(WIDTH + h * HEAD_DIM, WIDTH + (h + 1) * HEAD_DIM)
            qb = q_ref[:, sl].astype(BF16)
            kb = kv_ref[:, sl].astype(BF16)
            c_k = ctr_ref[0, L_FF + h:L_FF + h + 1, :] * LOG2E
            s = _dot_nt(qb, kb) * (HEAD_DIM ** -0.5 * LOG2E) - c_k
            if masked:
                s = jnp.where(keep, s, -jnp.inf)
            m_old = m_scr[h]
            m_new = jnp.maximum(m_old, jnp.max(s, axis=-1, keepdims=True))
            alpha = jnp.exp2(m_old - m_new)
            p = jnp.exp2(s - m_new)
            v_aug = jnp.concatenate([kv_ref[:, vsl].astype(BF16), ones], axis=1)
            acc_scr[h] = alpha * acc_scr[h] + _dot(p.astype(BF16), v_aug)
            m_scr[h] = m_new

    @pl.when(ki < qi)
    def _():
        step(False)

    @pl.when(ki == qi)
    def _():
        step(True)
        for h in range(N_HEADS):
            sl = slice(h * HEAD_DIM, (h + 1) * HEAD_DIM)
            acc = acc_scr[h]
            o_ref[:, sl] = (acc[:, :HEAD_DIM] / acc[:, HEAD_DIM:]).astype(BF16)


def fox_attention(proj, prep_tr, batch, seq, *, t=1024):
    n = proj.shape[0]
    nt = seq // t
    pairs = [(qi, ki) for qi in range(nt) for ki in range(qi + 1)]
    qi_arr = jnp.asarray([p[0] for p in pairs], I32)
    ki_arr = jnp.asarray([p[1] for p in pairs], I32)
    qspec = pl.BlockSpec((t, WIDTH), lambda b, s, qi, ki: (b * nt + qi[s], C_FQ // WIDTH))
    kvspec = pl.BlockSpec((t, 2 * WIDTH), lambda b, s, qi, ki: (b * nt + ki[s], C_FK // (2 * WIDTH)))
    return pl.pallas_call(
        functools.partial(_fox_kernel, t=t),
        grid_spec=pltpu.PrefetchScalarGridSpec(
            num_scalar_prefetch=2,
            grid=(batch, len(pairs)),
            in_specs=[qspec, kvspec,
                      pl.BlockSpec((1, PREP_ROWS, t), lambda b, s, qi, ki: (b, 0, ki[s]))],
            out_specs=pl.BlockSpec((t, WIDTH), lambda b, s, qi, ki: (b * nt + qi[s], 0)),
            scratch_shapes=[pltpu.VMEM((N_HEADS, t, 1), F32),
                            pltpu.VMEM((N_HEADS, t, 2 * HEAD_DIM), F32)]),
        out_shape=jax.ShapeDtypeStruct((n, WIDTH), BF16),
        compiler_params=_cparams(("arbitrary", "arbitrary")),
        name="fox_attention",
    )(qi_arr, ki_arr, proj, proj, prep_tr)


def _dsa_proj_kernel(cq_ref, g_ref, wq_ref, wi_ref, q_ref, qi_ref):
    x = cq_ref[...]
    ms = jnp.mean(x * x, axis=-1, keepdims=True)
    cb = (x * lax.rsqrt(ms + EPS) * g_ref[...]).astype(BF16)
    q_ref[...] = _dot(cb, wq_ref[...]).astype(BF16)
    qi_ref[...] = _dot(cb, wi_ref[...]).astype(BF16)


def dsa_proj(proj, cq_norm, w_uq, w_qidx, *, tm=1024):
    n = proj.shape[0]
    r = DSA_Q_RANK
    wi = IDX_HEADS * IDX_DIM
    return pl.pallas_call(
        _dsa_proj_kernel,
        grid=(n // tm,),
        in_specs=[pl.BlockSpec((tm, r), lambda i: (i, C_DCQ // r)),
                  pl.BlockSpec((1, r), lambda i: (0, 0)),
                  pl.BlockSpec((r, WIDTH), lambda i: (0, 0)),
                  pl.BlockSpec((r, wi), lambda i: (0, 0))],
        out_specs=[pl.BlockSpec((tm, WIDTH), lambda i: (i, 0)),
                   pl.BlockSpec((tm, wi), lambda i: (i, 0))],
        out_shape=[jax.ShapeDtypeStruct((n, WIDTH), BF16),
                   jax.ShapeDtypeStruct((n, wi), BF16)],
        compiler_params=_cparams(("arbitrary",)),
        name="dsa_proj",
    )(proj, cq_norm.reshape(1, r), w_uq, w_qidx)


DSA_QB = 256
DSA_KC = 512
DSA_SCORE_MID_STEPS = 20
DSA_FEW_KEYS = 4
DSA_HALVE_FIXED = 12
DSA_WALK_FIXED = 3


def _score_to_key(s):
    b = pltpu.bitcast(s, I32)
    return b ^ ((b >> 31) & INT_MAX)


def _key_to_score(k):
    return pltpu.bitcast(k ^ ((k >> 31) & INT_MAX), F32)


def _t5_bucket(rel):
    max_exact = REL_BUCKETS // 2
    relf = jnp.maximum(rel, max_exact).astype(F32)
    large = max_exact + (jnp.log(relf / max_exact) / math.log(REL_MAX_DIST / max_exact)
                         * (REL_BUCKETS - max_exact)).astype(I32)
    large = jnp.minimum(large, REL_BUCKETS - 1)
    return jnp.where(rel < max_exact, rel, large)


def _dsa_kernel(rb_ref, q_ref, qi_ref, tok_ref, k_ref, v_ref, ka_ref, kb_ref, o_ref,
                key_scr, lg_scr, band_scr, kb16_scr, vt_scr, ka16_scr, kb16i_scr, *, seq, topk):
    qb_idx = pl.program_id(1)
    t0 = qb_idx * DSA_QB
    n_kc = (t0 + DSA_QB - 1) // DSA_KC + 1
    row_vec = (1, DSA_QB)

    @pl.when(qb_idx == 0)
    def _():
        kb16_scr[...] = k_ref[...].astype(BF16)
        ka16_scr[...] = ka_ref[...].astype(BF16)
        kb16i_scr[...] = kb_ref[...].astype(BF16)
        for c in range(seq // DSA_KC):
            cs = slice(c * DSA_KC, (c + 1) * DSA_KC)
            vt_scr[0:HEAD_DIM, cs] = v_ref[cs, :].T.astype(BF16)
        vt_scr[HEAD_DIM:, :] = jnp.ones((HEAD_DIM, seq), BF16)

    @pl.when((pl.program_id(0) == 0) & (qb_idx == 0))
    def _():
        j_ = lax.broadcasted_iota(I32, (2 * DSA_QB, DSA_QB), 0)
        i_ = lax.broadcasted_iota(I32, (2 * DSA_QB, DSA_QB), 1)
        rel = i_ + DSA_QB - j_
        bucket = _t5_bucket(rel)
        for h in range(N_HEADS):
            far = rb_ref[REL_BUCKETS - 1, h]
            band = jnp.zeros((2 * DSA_QB, DSA_QB), F32)
            for bk in range(REL_BUCKETS - 1):
                band = jnp.where(bucket == bk, (rb_ref[bk, h] - far) * LOG2E, band)
            band_scr[h] = jnp.where(rel >= 0, band, 0.0)

    w_t = tok_ref[...].T
    key_s = lax.broadcasted_iota(I32, (DSA_KC, DSA_QB), 0)
    row_t = t0 + lax.broadcasted_iota(I32, (DSA_KC, DSA_QB), 1)

    def score_chunk(c, carry):
        kmax, kmin = carry
        ks = pl.ds(pl.multiple_of(c * DSA_KC, DSA_KC), DSA_KC)
        ka = ka16_scr[ks, :]
        kb = kb16i_scr[ks, :]
        acc = jnp.zeros((DSA_KC, DSA_QB), F32)
        for p in range(IDX_HEADS // 2):
            qp = qi_ref[:, p * LANE:(p + 1) * LANE]
            acc = acc + jnp.maximum(_dot_nt(ka, qp), 0.0) * w_t[2 * p:2 * p + 1, :]
            acc = acc + jnp.maximum(_dot_nt(kb, qp), 0.0) * w_t[2 * p + 1:2 * p + 2, :]
        key = _score_to_key(acc)
        valid = (c * DSA_KC + key_s) <= row_t
        key_scr[ks, :] = jnp.where(valid, key, INT_MIN)
        kmax = jnp.maximum(kmax, jnp.max(jnp.where(valid, key, INT_MIN), axis=0, keepdims=True))
        kmin = jnp.minimum(kmin, jnp.min(jnp.where(valid, key, INT_MAX), axis=0, keepdims=True))
        return kmax, kmin

    kmax, kmin = lax.fori_loop(0, n_kc, score_chunk, (jnp.full(row_vec, INT_MIN, I32),
                                                     jnp.full(row_vec, INT_MAX, I32)))

    def scan_keys(cand, with_below):
        def body(c, carry):
            cnt, below = carry
            ks = pl.ds(pl.multiple_of(c * DSA_KC, DSA_KC), DSA_KC)
            keys = key_scr[ks, :]
            ge = keys >= cand
            ones = ge.astype(I32)
            low = jnp.where(ge, INT_MIN, keys)
            for u in range(DSA_KC // SUBLANE):
                us = slice(u * SUBLANE, (u + 1) * SUBLANE)
                cnt = cnt + ones[us, :]
                if with_below:
                    below = jnp.maximum(below, low[us, :])
            return cnt, below
        cnt, below = lax.fori_loop(0, n_kc, body, (jnp.zeros((SUBLANE, DSA_QB), I32),
                                                   jnp.full((SUBLANE, DSA_QB), INT_MIN, I32)))
        cnt = jnp.sum(cnt, axis=0, keepdims=True)
        if with_below:
            return cnt, jnp.max(below, axis=0, keepdims=True)
        return cnt

    def open_rows(lo, hi, c_lo):
        return (c_lo > topk) & (hi - 1 > lo)

    def any_row(flag):
        return jnp.max(jnp.where(flag, 1, 0))

    def update(cand, cnt, lo, hi, c_lo, c_hi):
        ge = cnt >= topk
        return (jnp.where(ge, cand, lo), jnp.where(ge, hi, cand),
                jnp.where(ge, cnt, c_lo), jnp.where(ge, c_hi, cnt))

    def crowded(lo, hi, c_lo, c_hi):
        return any_row(open_rows(lo, hi, c_lo) & (c_lo - c_hi > DSA_FEW_KEYS))

    def halve_step(it, lo, hi, c_lo, c_hi):
        key_mid = (lo >> 1) + (hi >> 1) + (lo & hi & 1)
        score_mid = _score_to_key(0.5 * _key_to_score(lo) + 0.5 * _key_to_score(hi - 1))
        cand = jnp.where(it < DSA_SCORE_MID_STEPS, score_mid, key_mid)
        cand = jnp.minimum(jnp.maximum(cand, lo + 1), hi - 1)
        cand = jnp.where(hi - 1 > lo, cand, lo)
        return update(cand, scan_keys(cand, False), lo, hi, c_lo, c_hi)

    def halve_body(st):
        it, _, lo, hi, c_lo, c_hi = st
        go = crowded(lo, hi, c_lo, c_hi)
        return (it + 1, go) + halve_step(it, lo, hi, c_lo, c_hi)

    def walk_step(lo, hi, c_lo, c_hi, nxt):
        is_open = open_rows(lo, hi, c_lo)
        cand = jnp.where(is_open, nxt, lo)
        cnt, below = scan_keys(cand, True)
        ge = cnt >= topk
        hi = jnp.where(is_open, jnp.where(ge, cand + 1, cand), hi)
        c_hi = jnp.where(is_open & jnp.logical_not(ge), cnt, c_hi)
        lo = jnp.where(is_open & ge, cand, lo)
        c_lo = jnp.where(is_open & ge, cnt, c_lo)
        nxt = jnp.where(ge, nxt, below)
        return lo, hi, c_lo, c_hi, nxt

    def walk_body(st):
        go = any_row(open_rows(st[1], st[2], st[3]))
        return (go,) + walk_step(*st[1:])

    n_valid = jnp.minimum(t0 + lax.broadcasted_iota(I32, row_vec, 1) + 1, seq)
    st = (kmin, kmax + 1, n_valid, jnp.zeros(row_vec, I32))
    st = lax.fori_loop(0, DSA_HALVE_FIXED, lambda it, s: halve_step(it, *s), st)
    st = lax.while_loop(lambda s: s[1] > 0, halve_body,
                        (jnp.int32(DSA_HALVE_FIXED), crowded(*st)) + st)[2:]
    _, nxt0 = scan_keys(st[1], True)
    st = lax.fori_loop(0, DSA_WALK_FIXED, lambda it, s: walk_step(*s), st + (nxt0,))
    _, thr, hi, n_ge, n_gt, _ = lax.while_loop(
        lambda s: s[0] > 0, walk_body, (any_row(open_rows(st[0], st[1], st[2])),) + st)

    tied = n_ge > topk
    has_tie = jnp.max(jnp.where(tied, 1, 0)) > 0

    @pl.when(has_tie)
    def _():
        room = (topk - n_gt).astype(F32)
        ii = lax.broadcasted_iota(I32, (LANE, LANE), 0)
        jj = lax.broadcasted_iota(I32, (LANE, LANE), 1)
        lower = (ii >= jj).astype(BF16)

        def body(c, seen):
            ks = pl.ds(pl.multiple_of(c * LANE, LANE), LANE)
            kk = key_scr[ks, :]
            eq = kk == thr
            rank = seen + _dot(lower, eq.astype(BF16))
            drop = eq & (rank > room) & tied
            key_scr[ks, :] = jnp.where(drop, INT_MIN, kk)
            return seen + jnp.sum(eq.astype(F32), axis=0, keepdims=True)
        lax.fori_loop(0, n_kc * (DSA_KC // LANE), body, jnp.zeros(row_vec, F32))

    def mask_chunk(c, carry):
        ks = pl.ds(pl.multiple_of(c * DSA_KC, DSA_KC), DSA_KC)
        sel = jnp.where(key_scr[ks, :] >= thr, 0.0, -jnp.inf).astype(F32)
        key_scr[ks, :] = pltpu.bitcast(sel, I32)
        return carry
    lax.fori_loop(0, n_kc, mask_chunk, 0)

    heads = range(N_HEADS)
    hsl = [slice(h * HEAD_DIM, (h + 1) * HEAD_DIM) for h in heads]

    def logit_chunk(c, ms):
        ks = pl.ds(pl.multiple_of(c * DSA_KC, DSA_KC), DSA_KC)
        k_c = kb16_scr[ks, :]
        sel = pltpu.bitcast(key_scr[ks, :], F32)
        out = []
        for h in heads:
            s = _dot_nt(k_c, q_ref[:, hsl[h]]) * (HEAD_DIM ** -0.5 * LOG2E) + sel
            lg_scr[h, ks, :] = s
            out.append(jnp.maximum(ms[h], jnp.max(s, axis=0, keepdims=True)))
        return tuple(out)
    ms = lax.fori_loop(0, n_kc, logit_chunk,
                       tuple(jnp.full(row_vec, -jnp.inf, F32) for _ in heads))

    band_off = pl.multiple_of(jnp.maximum(qb_idx - 1, 0) * DSA_QB, DSA_QB)
    ws = pl.ds(band_off, 2 * DSA_QB)
    ms = list(ms)
    for h in heads:
        band_h = band_scr[h]
        band_first = jnp.concatenate([band_h[DSA_QB:, :], jnp.zeros((DSA_QB, DSA_QB), F32)], axis=0)
        win = lg_scr[h, ws, :] + jnp.where(qb_idx == 0, band_first, band_h)
        lg_scr[h, ws, :] = win
        ms[h] = jnp.maximum(ms[h], jnp.max(win, axis=0, keepdims=True))

    def pv_chunk(c, accs):
        ks = pl.ds(pl.multiple_of(c * DSA_KC, DSA_KC), DSA_KC)
        vt_c = vt_scr[:, ks]
        return tuple(accs[h] + _dot(vt_c, jnp.exp2(lg_scr[h, ks, :] - ms[h]).astype(BF16))
                     for h in heads)
    accs = lax.fori_loop(0, n_kc, pv_chunk,
                         tuple(jnp.zeros((2 * HEAD_DIM, DSA_QB), F32) for _ in heads))
    for h in heads:
        o_ref[:, hsl[h]] = (accs[h][:HEAD_DIM, :] / accs[h][HEAD_DIM:, :]).T.astype(BF16)


def dsa_attention(proj, q, q_idx, prep_tok, rel_bias, batch, seq):
    n = proj.shape[0]
    nq = seq // DSA_QB
    topk = min(TOPK_MAX, seq // 4)
    wi = IDX_HEADS * IDX_DIM
    rowblk = lambda w, cb: pl.BlockSpec((DSA_QB, w), lambda b, i: (b * nq + i, cb))
    seqblk = lambda off: pl.BlockSpec((seq, LANE), lambda b, i: (b, off // LANE))
    return pl.pallas_call(
        functools.partial(_dsa_kernel, seq=seq, topk=topk),
        grid=(batch, nq),
        in_specs=[pl.BlockSpec(memory_space=pltpu.SMEM),
                  rowblk(WIDTH, 0), rowblk(wi, 0), rowblk(LANE, 0),
                  seqblk(C_DK), seqblk(C_DV), seqblk(C_KA), seqblk(C_KB)],
        out_specs=pl.BlockSpec((DSA_QB, WIDTH), lambda b, i: (b * nq + i, 0)),
        out_shape=jax.ShapeDtypeStruct((n, WIDTH), BF16),
        scratch_shapes=[pltpu.VMEM((seq, DSA_QB), I32),
                        pltpu.VMEM((N_HEADS, seq, DSA_QB), F32),
                        pltpu.VMEM((N_HEADS, 2 * DSA_QB, DSA_QB), F32),
                        pltpu.VMEM((seq, LANE), BF16),
                        pltpu.VMEM((2 * HEAD_DIM, seq), BF16),
                        pltpu.VMEM((seq, LANE), BF16),
                        pltpu.VMEM((seq, LANE), BF16)],
        compiler_params=_cparams(("arbitrary", "arbitrary")),
        name="dsa_attention",
    )(rel_bias, q, q_idx, prep_tok, proj, proj, proj, proj)


GDN_T = 256
GDN_GROUP = 2
GDN_HALO = 8


def _gdn_kernel(x_ref, cw_ref, ng_ref, tok_ref, tr_ref, o_ref,
                xq_scr, xk_scr, xv_scr, state_scr):
    first = pl.program_id(1) == 0

    @pl.when(first)
    def _():
        state_scr[...] = jnp.zeros_like(state_scr)
        for scr in (xq_scr, xk_scr, xv_scr):
            scr[0:GDN_HALO, :] = jnp.zeros((GDN_HALO, WIDTH), F32)

    def conv(col, scr, w_off):
        scr[GDN_HALO:, :] = x_ref[:, col - C_GQ:col - C_GQ + WIDTH]
        y = jnp.zeros((GDN_T, WIDTH), F32)
        for i in range(GDN_CONV):
            st = GDN_HALO - (GDN_CONV - 1) + i
            y = y + scr[st:st + GDN_T, :] * cw_ref[i:i + 1, w_off:w_off + WIDTH]
        scr[0:GDN_HALO, :] = scr[GDN_T:GDN_T + GDN_HALO, :]
        return _silu(y)

    qc = conv(C_GQ, xq_scr, 0)
    kc = conv(C_GK, xk_scr, WIDTH)
    vc = conv(C_GV, xv_scr, 2 * WIDTH)
    z_off = C_GZ - C_GQ
    tok = tok_ref[...]
    c = GDN_CHUNK
    heads = range(N_HEADS)
    hsl = [slice(h * HEAD_DIM, (h + 1) * HEAD_DIM) for h in heads]

    def l2norm_heads(x, scale):
        return jnp.concatenate(
            [x[:, s] * (lax.rsqrt(jnp.sum(x[:, s] * x[:, s], axis=-1, keepdims=True) + EPS) * scale)
             for s in hsl], axis=1)

    qf = l2norm_heads(qc, HEAD_DIM ** -0.5)
    kf = l2norm_heads(kc, 1.0)

    grp = GDN_GROUP
    gw = grp * HEAD_DIM
    nb = grp * c
    ri = lax.broadcasted_iota(I32, (nb, nb), 0)
    ci = lax.broadcasted_iota(I32, (nb, nb), 1)
    tril = ((ri // c) == (ci // c)) & (ri >= ci)
    eye = (ri == ci).astype(F32)
    pair_masks = []
    for lg in range(c.bit_length() - 1):
        pair_masks.append(((ri >> (lg + 1)) == (ci >> (lg + 1)))
                          & (((ri >> lg) & 1) == 1) & (((ci >> lg) & 1) == 0))
    lane_head = lax.broadcasted_iota(I32, (c, gw), 1) // HEAD_DIM
    row_head = lax.broadcasted_iota(I32, (nb, HEAD_DIM), 0) // c

    def spread(x):
        return jnp.concatenate([jnp.where(lane_head == u, x, 0.0) for u in range(grp)], axis=0)

    def stack(x):
        return jnp.concatenate([x[:, hsl[u]] for u in range(grp)], axis=0)

    def spread_lanes(x):
        return jnp.concatenate([jnp.where(row_head == u, x, 0.0) for u in range(grp)], axis=1)

    def split(x):
        hi = x.astype(BF16)
        return hi, (x - hi.astype(F32)).astype(BF16)

    def dot_split(a, b):
        return _dot(a[0], b[0]) + (_dot(a[0], b[1]) + _dot(a[1], b[0]))

    items = [(j, gi) for j in range(GDN_T // c) for gi in range(N_HEADS // grp)]
    pre = []
    for j, gi in items:
        rs = slice(j * c, (j + 1) * c)
        last = slice((j + 1) * c - 1, (j + 1) * c)
        gh = [gi * grp + u for u in range(grp)]
        gsl = slice(gi * gw, (gi + 1) * gw)
        qj, kj, vj = qf[rs, gsl], kf[rs, gsl], vc[rs, gsl]
        b_col = jnp.concatenate([tok[rs, L_GB + h:L_GB + h + 1] for h in gh], axis=0)
        g_col = jnp.concatenate([tok[rs, L_GA + h:L_GA + h + 1] for h in gh], axis=0)
        g_row = jnp.concatenate([tr_ref[0, L_GA + h:L_GA + h + 1, rs] for h in gh], axis=1)
        g_last = [tok[last, L_GA + h:L_GA + h + 1] for h in gh]
        g_last_col = jnp.concatenate([jnp.broadcast_to(g, (c, 1)) for g in g_last], axis=0)
        k_sp = spread(kj)
        q_sp = spread(qj)
        k_sp16 = k_sp.astype(BF16)
        decay = jnp.exp(jnp.where(tril, g_col - g_row, -jnp.inf))
        eg = jnp.exp(g_col)
        pre.append(dict(
            gh=gh, gsl=gsl,
            l_mat=b_col * _dot_nt(k_sp16, k_sp16) * decay,
            rhs=jnp.concatenate([stack(vj) * b_col, stack(kj) * (b_col * eg)], axis=1),
            qk=_dot_nt(q_sp.astype(BF16), k_sp16) * decay,
            q_dec=q_sp * eg,
            k_dec=k_sp * jnp.exp(g_last_col - g_col),
            e_last=jnp.concatenate([jnp.broadcast_to(jnp.exp(g), (HEAD_DIM, 1)) for g in g_last], axis=0)))

    t_inv = [eye - jnp.where(pair_masks[0], p["l_mat"], 0.0) for p in pre]
    for pm in pair_masks[1:]:
        t_s = [split(t) for t in t_inv]
        m_t = [dot_split(split(jnp.where(pm, p["l_mat"], 0.0)), ts) for p, ts in zip(pre, t_s)]
        t_inv = [t - dot_split(ts, split(m)) for t, ts, m in zip(t_inv, t_s, m_t)]
    sols = [dot_split(split(t), split(p["rhs"])) for t, p in zip(t_inv, pre)]

    outs = [[] for _ in heads]
    for p, sol in zip(pre, sols):
        u0 = sol[:, :HEAD_DIM]
        kcum = sol[:, HEAD_DIM:]
        st = state_scr[p["gsl"], :]
        stb = st.astype(BF16)
        v_new = u0 - _dot(spread_lanes(kcum).astype(BF16), stb)
        v_new_b = v_new.astype(BF16)
        o_st = _dot(p["q_dec"].astype(BF16), stb) + _dot(p["qk"].astype(BF16), v_new_b)
        state_scr[p["gsl"], :] = st * p["e_last"] + _dot(p["k_dec"].T.astype(BF16), v_new_b)
        for u, h in enumerate(p["gh"]):
            outs[h].append(o_st[u * c:(u + 1) * c, :])

    for h in heads:
        o = jnp.concatenate(outs[h], axis=0)
        ms = jnp.mean(o * o, axis=-1, keepdims=True)
        on = o * lax.rsqrt(ms + EPS) * ng_ref[...]
        z = x_ref[:, z_off + hsl[h].start:z_off + hsl[h].stop]
        o_ref[:, hsl[h]] = (on * _silu(z)).astype(BF16)


def gated_deltanet(proj, prep_tok, prep_tr, conv_w, norm_g, batch, seq):
    n = proj.shape[0]
    t = GDN_T
    nt = seq // t
    return pl.pallas_call(
        _gdn_kernel,
        grid=(batch, nt),
        in_specs=[pl.BlockSpec((t, 4 * WIDTH), lambda b, i: (b * nt + i, C_GQ // (4 * WIDTH))),
                  pl.BlockSpec((GDN_CONV, 3 * WIDTH), lambda b, i: (0, 0)),
                  pl.BlockSpec((1, HEAD_DIM), lambda b, i: (0, 0)),
                  pl.BlockSpec((t, LANE), lambda b, i: (b * nt + i, 0)),
                  pl.BlockSpec((1, PREP_ROWS, t), lambda b, i: (b, 0, i))],
        out_specs=pl.BlockSpec((t, WIDTH), lambda b, i: (b * nt + i, 0)),
        out_shape=jax.ShapeDtypeStruct((n, WIDTH), BF16),
        scratch_shapes=[pltpu.VMEM((t + GDN_HALO, WIDTH), F32),
                        pltpu.VMEM((t + GDN_HALO, WIDTH), F32),
                        pltpu.VMEM((t + GDN_HALO, WIDTH), F32),
                        pltpu.VMEM((N_HEADS * HEAD_DIM, HEAD_DIM), F32)],
        compiler_params=_cparams(("arbitrary", "arbitrary")),
        name="gated_deltanet",
    )(proj, conv_w, norm_g.reshape(1, HEAD_DIM), prep_tok, prep_tr)


def _merge_kernel(h_ref, b0_ref, b1_ref, b2_ref, b3_ref, g0_ref, g1_ref, g2_ref, g3_ref,
                  wb_ref, o_ref):
    h = h_ref[...]
    acc = None
    for n, (b_ref, g_ref) in enumerate(zip((b0_ref, b1_ref, b2_ref, b3_ref),
                                           (g0_ref, g1_ref, g2_ref, g3_ref))):
        gate = jax.nn.sigmoid(_dot(h, g_ref[...]))
        term = gate * _dot(b_ref[...], wb_ref[n])
        acc = term if acc is None else acc + term
    o_ref[...] = acc.astype(BF16)


def merge_branches(h, branches, w_gate, w_branch, *, tm=512, tn=512):
    _, n, d = h.shape
    nj = d // tn
    bspec = pl.BlockSpec((tm, WIDTH), lambda j, i: (i, 0))
    gspec = lambda k: pl.BlockSpec((d, tn), lambda j, i: (0, k * nj + j))
    return pl.pallas_call(
        _merge_kernel,
        grid=(nj, n // tm),
        in_specs=[pl.BlockSpec((None, tm, d), lambda j, i: (0, i, 0)),
                  bspec, bspec, bspec, bspec,
                  gspec(0), gspec(1), gspec(2), gspec(3),
                  pl.BlockSpec((N_BRANCH, WIDTH, tn), lambda j, i: (0, 0, j))],
        out_specs=pl.BlockSpec((tm, tn), lambda j, i: (i, j)),
        out_shape=jax.ShapeDtypeStruct((n, d), BF16),
        compiler_params=_cparams(("arbitrary", "arbitrary")),
        name="merge_branches",
    )(h, *branches, w_gate, w_gate, w_gate, w_gate, w_branch)


def _resid_mm_kernel(a_ref, w_ref, x_ref, o_ref):
    o_ref[...] = x_ref[...] + _dot(a_ref[...], w_ref[...])


def resid_matmul(a, w, x, *, tm=512, tn=1024, name="resid_matmul"):
    n, k = a.shape
    d = w.shape[1]
    return pl.pallas_call(
        _resid_mm_kernel,
        grid=(d // tn, n // tm),
        in_specs=[pl.BlockSpec((tm, k), lambda j, i: (i, 0)),
                  pl.BlockSpec((k, tn), lambda j, i: (0, j)),
                  pl.BlockSpec((tm, tn), lambda j, i: (i, j))],
        out_specs=pl.BlockSpec((tm, tn), lambda j, i: (i, j)),
        out_shape=jax.ShapeDtypeStruct((n, d), F32),
        compiler_params=_cparams(("arbitrary", "arbitrary")),
        name=name,
    )(a, w, x)


FFN_HALO = 8
FFN_SUB = 512


def _ffn1_kernel(x_ref, g_ref, wg_ref, wu_ref, cw_ref, cb_ref, o_ref, h_scr, gt_scr, halo_scr,
                 *, tm, tiles_per_seq):
    i = pl.program_id(1)

    @pl.when(i == 0)
    def _():
        halo_scr[...] = jnp.zeros_like(halo_scr)

    def body(r, carry):
        rows = pl.ds(pl.multiple_of(r * NORM_ROWS, NORM_ROWS), NORM_ROWS)
        h_scr[rows, :] = _rmsnorm_rows(x_ref, g_ref, rows).astype(BF16)
        return carry
    lax.fori_loop(0, tm // NORM_ROWS, body, 0)

    h = h_scr[...]
    seq_start = (i % tiles_per_seq) == 0
    tn = o_ref.shape[1]
    for off in range(0, tn, FFN_SUB):
        cs = slice(off, min(off + FFN_SUB, tn))
        g = _dot(h, wg_ref[:, cs])
        gt_scr[FFN_HALO:, cs] = g
        gt_scr[0:FFN_HALO, cs] = jnp.where(seq_start, 0.0, halo_scr[:, cs])
        halo_scr[:, cs] = g[tm - FFN_HALO:, :]
        y = cb_ref[:, cs] + g * cw_ref[FFN_CONV - 1:FFN_CONV, cs]
        for t in range(FFN_CONV - 1):
            st = FFN_HALO - (FFN_CONV - 1) + t
            y = y + gt_scr[st:st + tm, cs] * cw_ref[t:t + 1, cs]
        o_ref[:, cs] = (_silu(y) * _dot(h, wu_ref[:, cs])).astype(BF16)


def conv_ffn_up(x, gain, w_gate, w_up, conv_w, conv_b, seq, *, tm=512, col_parts=2):
    n, d = x.shape
    f = w_gate.shape[1]
    tn = f // col_parts
    wspec = pl.BlockSpec((d, tn), lambda j, i: (0, j), pipeline_mode=pl.Buffered(1))
    return pl.pallas_call(
        functools.partial(_ffn1_kernel, tm=tm, tiles_per_seq=seq // tm),
        grid=(col_parts, n // tm),
        in_specs=[pl.BlockSpec((tm, d), lambda j, i: (i, 0)),
                  pl.BlockSpec((1, d), lambda j, i: (0, 0)),
                  wspec, wspec,
                  pl.BlockSpec((FFN_CONV, tn), lambda j, i: (0, j)),
                  pl.BlockSpec((1, tn), lambda j, i: (0, j))],
        out_specs=pl.BlockSpec((tm, tn), lambda j, i: (i, j)),
        out_shape=jax.ShapeDtypeStruct((n, f), BF16),
        scratch_shapes=[pltpu.VMEM((tm, d), BF16),
                        pltpu.VMEM((tm + FFN_HALO, tn), F32),
                        pltpu.VMEM((FFN_HALO, tn), F32)],
        compiler_params=_cparams(("arbitrary", "arbitrary")),
        name="conv_ffn_up",
    )(x, gain.reshape(1, d), w_gate, w_up, conv_w, conv_b.reshape(1, f))


IN_SIZES = (WIDTH, WIDTH, WIDTH, WIDTH,
            DSA_Q_RANK, HEAD_DIM, HEAD_DIM, IDX_DIM, IDX_HEADS,
            WIDTH, WIDTH, WIDTH, N_HEADS,
            WIDTH, WIDTH, WIDTH, WIDTH, N_HEADS, N_HEADS)
IN_NAMES = ("r_q", "r_k", "r_v", "r_g", "d_cq", "d_k", "d_v", "i_k", "i_w",
            "f_q", "f_k", "f_v", "f_f", "g_q", "g_k", "g_v", "g_z", "g_b", "g_a")
IN_PLAN = (("r_q", C_RQ), ("r_k", C_RK), ("r_v", C_RV), ("r_g", C_RG),
           ("f_q", C_FQ), ("f_k", C_FK), ("f_v", C_FV),
           ("g_q", C_GQ), ("g_k", C_GK), ("g_v", C_GV), ("g_z", C_GZ),
           ("d_k", C_DK), ("d_cq", C_DCQ), ("d_v", C_DV),
           ("i_k", C_KA), ("i_k", C_KB + IDX_DIM),
           ("i_w", C_SM + L_IW), ("f_f", C_SM + L_FF), ("g_b", C_SM + L_GB), ("g_a", C_SM + L_GA))


def _prep_w_in_kernel(w_ref, m_ref, g_ref):
    src = {}
    off = 0
    for name, size in zip(IN_NAMES, IN_SIZES):
        src[name] = (off, size)
        off += size
    m_ref[...] = jnp.zeros_like(m_ref)
    for name, dst in IN_PLAN:
        so, w = src[name]
        m_ref[:, dst:dst + w] = w_ref[:, so:so + w].astype(BF16)
    g_ref[...] = w_ref[:, off:off + g_ref.shape[1]].astype(BF16)


def prep_w_in(w_in, layer, *, tr=256):
    _, d, c = w_in.shape
    return pl.pallas_call(
        _prep_w_in_kernel,
        grid=(d // tr,),
        in_specs=[pl.BlockSpec((None, tr, c), lambda i: (layer, i, 0))],
        out_specs=[pl.BlockSpec((tr, C_TOT), lambda i: (i, 0)),
                   pl.BlockSpec((tr, N_BRANCH * d), lambda i: (i, 0))],
        out_shape=[jax.ShapeDtypeStruct((d, C_TOT), BF16),
                   jax.ShapeDtypeStruct((d, N_BRANCH * d), BF16)],
        compiler_params=_cparams(("arbitrary",)),
        name="prep_w_in",
    )(w_in)


def _cast_kernel(w_ref, o_ref):
    o_ref[...] = w_ref[...].astype(BF16)


CAST_BLOCK_BYTES = 12 * 1024 * 1024


def cast_layer(w, layer):
    _, r, c = w.shape
    tr = r
    while tr * c * 4 > CAST_BLOCK_BYTES and tr % 16 == 0:
        tr //= 2
    return pl.pallas_call(
        _cast_kernel,
        grid=(r // tr,),
        in_specs=[pl.BlockSpec((None, tr, c), lambda i: (layer, i, 0))],
        out_specs=pl.BlockSpec((tr, c), lambda i: (i, 0)),
        out_shape=jax.ShapeDtypeStruct((r, c), BF16),
        compiler_params=_cparams(("arbitrary",)),
        name="cast_bf16",
    )(w)


def cast_branch(w_branch, layer):
    _, nbr, r, c = w_branch.shape
    return pl.pallas_call(
        _cast_kernel,
        grid=(nbr,),
        in_specs=[pl.BlockSpec((None, None, r, c), lambda i: (layer, i, 0, 0))],
        out_specs=pl.BlockSpec((None, r, c), lambda i: (i, 0, 0)),
        out_shape=jax.ShapeDtypeStruct((nbr, r, c), BF16),
        compiler_params=_cparams(("arbitrary",)),
        name="cast_branch",
    )(w_branch)


def kernel(x, norm_mix, w_in, dsa_cq_norm, dsa_w_uq, dsa_w_qidx, fox_f_bias, gdn_conv, gdn_a_log,
           gdn_dt_bias, gdn_norm, w_branch, w_out, rel_bias, norm_ffn, ffn_w_gate, ffn_w_up,
           ffn_conv, ffn_conv_b, ffn_w_down, final_norm):
    batch, seq, d = x.shape
    depth = w_in.shape[0]
    xf = x.reshape(batch * seq, d)
    ret_tables = _retention_tables(seq)
    for l in range(depth):
        w_main, w_gate = prep_w_in(w_in, l)
        proj, h = norm_proj(xf, norm_mix[l], w_main)
        par = jnp.zeros((SUBLANE, LANE), F32)
        par = par.at[0, L_FF:L_FF + N_HEADS].set(fox_f_bias[l])
        par = par.at[0, L_GA:L_GA + N_HEADS].set(gdn_dt_bias[l])
        par = par.at[1, L_GA:L_GA + N_HEADS].set(gdn_a_log[l])
        prep_tok, prep_tr = prep_small(proj, par, batch, seq)
        o_ret = retention(proj, ret_tables, batch, seq)
        q_dsa, q_idx = dsa_proj(proj, dsa_cq_norm[l], cast_layer(dsa_w_uq, l), cast_layer(dsa_w_qidx, l))
        o_dsa = dsa_attention(proj, q_dsa, q_idx, prep_tok, rel_bias, batch, seq)
        o_fox = fox_attention(proj, prep_tr, batch, seq)
        o_gdn = gated_deltanet(proj, prep_tok, prep_tr, gdn_conv[l], gdn_norm[l], batch, seq)
        merged = merge_branches(h, (o_ret, o_dsa, o_fox, o_gdn), w_gate, cast_branch(w_branch, l))
        xf = resid_matmul(merged, cast_layer(w_out, l), xf, tn=d, name="out_proj")
        act = conv_ffn_up(xf, norm_ffn[l], cast_layer(ffn_w_gate, l), cast_layer(ffn_w_up, l),
                          ffn_conv[l], ffn_conv_b[l], seq)
        xf = resid_matmul(act, cast_layer(ffn_w_down, l), xf, name="ffn_down")
    return rmsnorm(xf, final_norm).reshape(batch, seq, d)
```

```python
import functools
import math

import jax
import jax.numpy as jnp
from jax import lax
from jax.experimental import pallas as pl
from jax.experimental.pallas import tpu as pltpu

F32 = jnp.float32
BF16 = jnp.bfloat16
I32 = jnp.int32

HEAD_DIM = 128
N_HEADS = 4
WIDTH = N_HEADS * HEAD_DIM
N_BRANCH = 4
RET_CHUNK = 128
ROPE_BASE = 10000.0
DSA_Q_RANK = 384
IDX_HEADS = 16
IDX_DIM = 64
TOPK_MAX = 256
GDN_CONV = 4
GDN_CHUNK = 64
REL_BUCKETS = 32
REL_MAX_DIST = 128
FFN_CONV = 3
EPS = 1e-6

LANE = 128
SUBLANE = 8
VMEM_LIMIT = 56 * 1024 * 1024

C_RQ, C_RK, C_RV, C_RG = 0, 512, 1024, 1536
C_FQ, C_DK, C_DCQ = 2048, 2560, 2688
C_FK, C_FV = 3072, 3584
C_GQ, C_GK, C_GV, C_GZ = 4096, 4608, 5120, 5632
C_DV, C_KA, C_KB, C_SM = 6144, 6272, 6400, 6528
C_TOT = 6656
L_IW, L_FF, L_GB, L_GA = 0, 16, 20, 24

LOG2E = 1.4426950408889634
INT_MIN = -(2 ** 31)
INT_MAX = 2 ** 31 - 1
HIGHEST = lax.Precision.HIGHEST


def _cparams(sem, vmem=VMEM_LIMIT):
    return pltpu.CompilerParams(dimension_semantics=sem, vmem_limit_bytes=vmem)


def _dot(a, b):
    return jnp.dot(a, b, preferred_element_type=F32)


def _dot_nt(a, b):
    return lax.dot_general(a, b, (((1,), (1,)), ((), ())), preferred_element_type=F32)


def _silu(x):
    return x * jax.nn.sigmoid(x)


NORM_ROWS = 128


def _rmsnorm_rows(x_ref, g_ref, rows):
    x = x_ref[rows, :]
    ms = jnp.mean(x * x, axis=-1, keepdims=True)
    return x * lax.rsqrt(ms + EPS) * g_ref[...]


PROJ_SUB = 512


def _norm_proj_kernel(x_ref, g_ref, w_ref, o_ref, h_ref, h_scr, *, tm):
    def body(r, carry):
        rows = pl.ds(pl.multiple_of(r * NORM_ROWS, NORM_ROWS), NORM_ROWS)
        hb = _rmsnorm_rows(x_ref, g_ref, rows).astype(BF16)
        h_scr[rows, :] = hb
        h_ref[rows, :] = hb
        return carry
    lax.fori_loop(0, tm // NORM_ROWS, body, 0)

    h = h_scr[...]
    tn = o_ref.shape[1]
    for off in range(0, tn, PROJ_SUB):
        cs = slice(off, min(off + PROJ_SUB, tn))
        o_ref[:, cs] = _dot(h, w_ref[:, cs])


def norm_proj(x, gain, w, *, tm=512, col_parts=2):
    n, d = x.shape
    c = w.shape[1]
    tn = c // col_parts
    return pl.pallas_call(
        functools.partial(_norm_proj_kernel, tm=tm),
        grid=(col_parts, n // tm),
        in_specs=[pl.BlockSpec((tm, d), lambda j, i: (i, 0)),
                  pl.BlockSpec((1, d), lambda j, i: (0, 0)),
                  pl.BlockSpec((d, tn), lambda j, i: (0, j), pipeline_mode=pl.Buffered(1))],
        out_specs=[pl.BlockSpec((tm, tn), lambda j, i: (i, j)),
                   pl.BlockSpec((None, tm, d), lambda j, i: (j, i, 0))],
        out_shape=[jax.ShapeDtypeStruct((n, c), F32),
                   jax.ShapeDtypeStruct((col_parts, n, d), BF16)],
        scratch_shapes=[pltpu.VMEM((tm, d), BF16)],
        compiler_params=_cparams(("arbitrary", "arbitrary")),
        name="norm_proj",
    )(x, gain.reshape(1, d), w)


def _rmsnorm_kernel(x_ref, g_ref, o_ref, *, tm):
    def body(r, carry):
        rows = pl.ds(pl.multiple_of(r * NORM_ROWS, NORM_ROWS), NORM_ROWS)
        o_ref[rows, :] = _rmsnorm_rows(x_ref, g_ref, rows)
        return carry
    lax.fori_loop(0, tm // NORM_ROWS, body, 0)


def rmsnorm(x, gain, *, tm=512):
    n, d = x.shape
    return pl.pallas_call(
        functools.partial(_rmsnorm_kernel, tm=tm),
        grid=(n // tm,),
        in_specs=[pl.BlockSpec((tm, d), lambda i: (i, 0)),
                  pl.BlockSpec((1, d), lambda i: (0, 0))],
        out_specs=pl.BlockSpec((tm, d), lambda i: (i, 0)),
        out_shape=jax.ShapeDtypeStruct((n, d), F32),
        compiler_params=_cparams(("arbitrary",)),
        name="final_rmsnorm",
    )(x, gain.reshape(1, d))


def _prep_kernel(s_ref, par_ref, tok_ref, tr_ref, carry_scr):
    @pl.when(pl.program_id(1) == 0)
    def _():
        carry_scr[...] = jnp.zeros_like(carry_scr)

    s = s_ref[...]
    lane = lax.broadcasted_iota(I32, (LANE, LANE), 1)
    row = lax.broadcasted_iota(I32, (LANE, LANE), 0)
    z = s + par_ref[0:1, :]
    soft = jnp.maximum(z, 0.0) + jnp.log1p(jnp.exp(-jnp.abs(z)))
    log_sig = z - soft
    sig = jax.nn.sigmoid(z)
    g_val = -jnp.exp(par_ref[1:2, :]) * soft
    is_f = (lane[0:1] >= L_FF) & (lane[0:1] < L_FF + N_HEADS)
    is_b = (lane[0:1] >= L_GB) & (lane[0:1] < L_GB + N_HEADS)
    is_a = (lane[0:1] >= L_GA) & (lane[0:1] < L_GA + N_HEADS)
    pre = jnp.where(is_f, log_sig, jnp.where(is_a, g_val, 0.0))
    tri = (row >= lane).astype(F32)
    tri_blk = ((row >= lane) & ((row // GDN_CHUNK) == (lane // GDN_CHUNK))).astype(F32)
    subs = [slice(u * LANE, (u + 1) * LANE) for u in range(PREP_T // LANE)]
    cum_full = [jnp.dot(tri, pre[u], precision=HIGHEST, preferred_element_type=F32) for u in subs]
    cum_blk = [jnp.dot(tri_blk, pre[u], precision=HIGHEST, preferred_element_type=F32) for u in subs]
    scale_iw = IDX_HEADS ** -0.5 * IDX_DIM ** -0.5
    carry = carry_scr[0:1, :]
    for u, cf, cb in zip(subs, cum_full, cum_blk):
        c_fox = cf + carry
        carry = c_fox[LANE - 1:LANE, :]
        out = jnp.where(is_f, c_fox,
                        jnp.where(is_a, cb,
                                  jnp.where(is_b, sig[u],
                                            jnp.where(lane[0:1] < IDX_HEADS, s[u] * scale_iw, 0.0))))
        tok_ref[u, :] = out
        tr_ref[0, :, u] = out.T[0:PREP_ROWS, :]
    carry_scr[0:1, :] = carry


PREP_T = 512
PREP_ROWS = 32


def prep_small(proj, par, batch, seq):
    n = proj.shape[0]
    nc = seq // PREP_T
    return pl.pallas_call(
        _prep_kernel,
        grid=(batch, nc),
        in_specs=[pl.BlockSpec((PREP_T, LANE), lambda b, c: (b * nc + c, C_SM // LANE)),
                  pl.BlockSpec((SUBLANE, LANE), lambda b, c: (0, 0))],
        out_specs=[pl.BlockSpec((PREP_T, LANE), lambda b, c: (b * nc + c, 0)),
                   pl.BlockSpec((1, PREP_ROWS, PREP_T), lambda b, c: (b, 0, c))],
        out_shape=[jax.ShapeDtypeStruct((n, LANE), F32),
                   jax.ShapeDtypeStruct((batch, PREP_ROWS, seq), F32)],
        scratch_shapes=[pltpu.VMEM((SUBLANE, LANE), F32)],
        compiler_params=_cparams(("arbitrary", "arbitrary")),
        name="prep_small",
    )(proj, par)


RET_T = 512


def _ret_gamma():
    return [math.log1p(-(2.0 ** (-5.0 - h))) for h in range(N_HEADS)]


def _retention_kernel(x_ref, cos_ref, sin_ref, dec_ref, zeta_ref, xi_ref, o_ref, state_scr):
    @pl.when(pl.program_id(1) == 0)
    def _():
        state_scr[...] = jnp.zeros_like(state_scr)

    log_gamma = _ret_gamma()
    heads = range(N_HEADS)
    hsl = [slice(h * HEAD_DIM, (h + 1) * HEAD_DIM) for h in heads]
    chunks = [slice(j * RET_CHUNK, (j + 1) * RET_CHUNK) for j in range(RET_T // RET_CHUNK)]
    items = [(rs, h) for rs in chunks for h in heads]

    def part(col, h):
        return slice(col - C_RQ + hsl[h].start, col - C_RQ + hsl[h].stop)

    def rope(x, rs):
        return x * cos_ref[rs, :] + pltpu.roll(x, HEAD_DIM // 2, 1) * sin_ref[rs, :]

    qb = [rope(x_ref[rs, part(C_RQ, h)], rs).astype(BF16) for rs, h in items]
    kr = [rope(x_ref[rs, part(C_RK, h)], rs) * (HEAD_DIM ** -0.5) for rs, h in items]
    kb = [x.astype(BF16) for x in kr]
    vb = [x_ref[rs, part(C_RV, h)].astype(BF16) for rs, h in items]
    inner = [(_dot_nt(q, k) * dec_ref[h]).astype(BF16) for q, k, (_, h) in zip(qb, kb, items)]
    kv = [_dot((k * zeta_ref[h]).T.astype(BF16), v) for k, v, (_, h) in zip(kr, vb, items)]
    o_in = [_dot(a, v) for a, v in zip(inner, vb)]
    st = [state_scr[h] for h in heads]
    for n, (rs, h) in enumerate(items):
        o = o_in[n] + _dot(qb[n], st[h].astype(BF16)) * xi_ref[h]
        st[h] = st[h] * math.exp(log_gamma[h] * RET_CHUNK) + kv[n]
        mu = jnp.mean(o, axis=-1, keepdims=True)
        oc = o - mu
        var = jnp.mean(oc * oc, axis=-1, keepdims=True)
        gate = x_ref[rs, part(C_RG, h)]
        o_ref[rs, hsl[h]] = (_silu(gate) * (oc * lax.rsqrt(var + EPS))).astype(BF16)
    for h in heads:
        state_scr[h] = st[h]


def _retention_tables(seq):
    half = HEAD_DIM // 2
    inv = 1.0 / (ROPE_BASE ** (jnp.arange(half, dtype=F32) / half))
    ang = jnp.arange(seq).astype(F32)[:, None] * inv[None, :]
    cos, sin = jnp.cos(ang), jnp.sin(ang)
    cos_t = jnp.concatenate([cos, cos], axis=-1)
    sin_t = jnp.concatenate([-sin, sin], axis=-1)
    c = RET_CHUNK
    log_gamma = jnp.log1p(-jnp.exp2(-5.0 - jnp.arange(N_HEADS, dtype=F32)))
    n = jnp.arange(c, dtype=F32)
    diff = n[:, None] - n[None, :]
    decay = jnp.where(diff >= 0, jnp.exp(log_gamma[:, None, None] * jnp.maximum(diff, 0.0)), 0.0)
    zeta = jnp.exp(log_gamma[:, None] * (c - 1 - n)[None, :])
    xi = jnp.exp(log_gamma[:, None] * (n + 1)[None, :])
    ones = jnp.ones((1, 1, HEAD_DIM), F32)
    return cos_t, sin_t, decay, zeta[:, :, None] * ones, xi[:, :, None] * ones


def retention(proj, tables, batch, seq):
    n = proj.shape[0]
    t = RET_T
    nc = seq // t
    cos_t, sin_t, decay, zeta, xi = tables
    full3 = pl.BlockSpec((N_HEADS, RET_CHUNK, HEAD_DIM), lambda b, i: (0, 0, 0))
    return pl.pallas_call(
        _retention_kernel,
        grid=(batch, nc),
        in_specs=[pl.BlockSpec((t, 4 * WIDTH), lambda b, i: (b * nc + i, C_RQ // (4 * WIDTH))),
                  pl.BlockSpec((t, HEAD_DIM), lambda b, i: (i, 0)),
                  pl.BlockSpec((t, HEAD_DIM), lambda b, i: (i, 0)),
                  full3, full3, full3],
        out_specs=pl.BlockSpec((t, WIDTH), lambda b, i: (b * nc + i, 0)),
        out_shape=jax.ShapeDtypeStruct((n, WIDTH), BF16),
        scratch_shapes=[pltpu.VMEM((N_HEADS, HEAD_DIM, HEAD_DIM), F32)],
        compiler_params=_cparams(("arbitrary", "arbitrary")),
        name="retention",
    )(proj, cos_t, sin_t, decay, zeta, xi)


def _fox_kernel(qi_ref, ki_ref, q_ref, kv_ref, ctr_ref, o_ref, m_scr, acc_scr, *, t):
    qi = qi_ref[pl.program_id(1)]
    ki = ki_ref[pl.program_id(1)]

    @pl.when(ki == 0)
    def _():
        m_scr[...] = jnp.full_like(m_scr, -jnp.inf)
        acc_scr[...] = jnp.zeros_like(acc_scr)

    def step(masked):
        if masked:
            row = lax.broadcasted_iota(I32, (t, t), 0)
            colm = lax.broadcasted_iota(I32, (t, t), 1)
            keep = row >= colm
        ones = jnp.ones((t, HEAD_DIM), BF16)
        for h in range(N_HEADS):
            sl = slice(h * HEAD_DIM, (h + 1) * HEAD_DIM)
            vsl = slice(WIDTH + h * HEAD_DIM, WIDTH + (h + 1) * HEAD_DIM)
            qb = q_ref[:, sl].astype(BF16)
            kb = kv_ref[:, sl].astype(BF16)
            c_k = ctr_ref[0, L_FF + h:L_FF + h + 1, :] * LOG2E
            s = _dot_nt(qb, kb) * (HEAD_DIM ** -0.5 * LOG2E) - c_k
            if masked:
                s = jnp.where(keep, s, -jnp.inf)
            m_old = m_scr[h]
            m_new = jnp.maximum(m_old, jnp.max(s, axis=-1, keepdims=True))
            alpha = jnp.exp2(m_old - m_new)
            p = jnp.exp2(s - m_new)
            v_aug = jnp.concatenate([kv_ref[:, vsl].astype(BF16), ones], axis=1)
            acc_scr[h] = alpha * acc_scr[h] + _dot(p.astype(BF16), v_aug)
            m_scr[h] = m_new

    @pl.when(ki < qi)
    def _():
        step(False)

    @pl.when(ki == qi)
    def _():
        step(True)
        for h in range(N_HEADS):
            sl = slice(h * HEAD_DIM, (h + 1) * HEAD_DIM)
            acc = acc_scr[h]
            o_ref[:, sl] = (acc[:, :HEAD_DIM] / acc[:, HEAD_DIM:]).astype(BF16)


def fox_attention(proj, prep_tr, batch, seq, *, t=1024):
    n = proj.shape[0]
    nt = seq // t
    pairs = [(qi, ki) for qi in range(nt) for ki in range(qi + 1)]
    qi_arr = jnp.asarray([p[0] for p in pairs], I32)
    ki_arr = jnp.asarray([p[1] for p in pairs], I32)
    qspec = pl.BlockSpec((t, WIDTH), lambda b, s, qi, ki: (b * nt + qi[s], C_FQ // WIDTH))
    kvspec = pl.BlockSpec((t, 2 * WIDTH), lambda b, s, qi, ki: (b * nt + ki[s], C_FK // (2 * WIDTH)))
    return pl.pallas_call(
        functools.partial(_fox_kernel, t=t),
        grid_spec=pltpu.PrefetchScalarGridSpec(
            num_scalar_prefetch=2,
            grid=(batch, len(pairs)),
            in_specs=[qspec, kvspec,
                      pl.BlockSpec((1, PREP_ROWS, t), lambda b, s, qi, ki: (b, 0, ki[s]))],
            out_specs=pl.BlockSpec((t, WIDTH), lambda b, s, qi, ki: (b * nt + qi[s], 0)),
            scratch_shapes=[pltpu.VMEM((N_HEADS, t, 1), F32),
                            pltpu.VMEM((N_HEADS, t, 2 * HEAD_DIM), F32)]),
        out_shape=jax.ShapeDtypeStruct((n, WIDTH), BF16),
        compiler_params=_cparams(("arbitrary", "arbitrary")),
        name="fox_attention",
    )(qi_arr, ki_arr, proj, proj, prep_tr)


def _dsa_proj_kernel(cq_ref, g_ref, wq_ref, wi_ref, q_ref, qi_ref):
    x = cq_ref[...]
    ms = jnp.mean(x * x, axis=-1, keepdims=True)
    cb = (x * lax.rsqrt(ms + EPS) * g_ref[...]).astype(BF16)
    q_ref[...] = _dot(cb, wq_ref[...]).astype(BF16)
    qi_ref[...] = _dot(cb, wi_ref[...]).astype(BF16)


def dsa_proj(proj, cq_norm, w_uq, w_qidx, *, tm=1024):
    n = proj.shape[0]
    r = DSA_Q_RANK
    wi = IDX_HEADS * IDX_DIM
    return pl.pallas_call(
        _dsa_proj_kernel,
        grid=(n // tm,),
        in_specs=[pl.BlockSpec((tm, r), lambda i: (i, C_DCQ // r)),
                  pl.BlockSpec((1, r), lambda i: (0, 0)),
                  pl.BlockSpec((r, WIDTH), lambda i: (0, 0)),
                  pl.BlockSpec((r, wi), lambda i: (0, 0))],
        out_specs=[pl.BlockSpec((tm, WIDTH), lambda i: (i, 0)),
                   pl.BlockSpec((tm, wi), lambda i: (i, 0))],
        out_shape=[jax.ShapeDtypeStruct((n, WIDTH), BF16),
                   jax.ShapeDtypeStruct((n, wi), BF16)],
        compiler_params=_cparams(("arbitrary",)),
        name="dsa_proj",
    )(proj, cq_norm.reshape(1, r), w_uq, w_qidx)


DSA_QB = 256
DSA_KC = 512
DSA_SCORE_MID_STEPS = 20
DSA_FEW_KEYS = 4
DSA_HALVE_FIXED = 12
DSA_WALK_FIXED = 3


def _score_to_key(s):
    b = pltpu.bitcast(s, I32)
    return b ^ ((b >> 31) & INT_MAX)


def _key_to_score(k):
    return pltpu.bitcast(k ^ ((k >> 31) & INT_MAX), F32)


def _t5_bucket(rel):
    max_exact = REL_BUCKETS // 2
    relf = jnp.maximum(rel, max_exact).astype(F32)
    large = max_exact + (jnp.log(relf / max_exact) / math.log(REL_MAX_DIST / max_exact)
                         * (REL_BUCKETS - max_exact)).astype(I32)
    large = jnp.minimum(large, REL_BUCKETS - 1)
    return jnp.where(rel < max_exact, rel, large)


def _dsa_kernel(rb_ref, q_ref, qi_ref, tok_ref, k_ref, v_ref, ka_ref, kb_ref, o_ref,
                key_scr, lg_scr, band_scr, kb16_scr, vt_scr, ka16_scr, kb16i_scr, *, seq, topk):
    qb_idx = pl.program_id(1)
    t0 = qb_idx * DSA_QB
    n_kc = (t0 + DSA_QB - 1) // DSA_KC + 1
    row_vec = (1, DSA_QB)

    @pl.when(qb_idx == 0)
    def _():
        kb16_scr[...] = k_ref[...].astype(BF16)
        ka16_scr[...] = ka_ref[...].astype(BF16)
        kb16i_scr[...] = kb_ref[...].astype(BF16)
        for c in range(seq // DSA_KC):
            cs = slice(c * DSA_KC, (c + 1) * DSA_KC)
            vt_scr[0:HEAD_DIM, cs] = v_ref[cs, :].T.astype(BF16)
        vt_scr[HEAD_DIM:, :] = jnp.ones((HEAD_DIM, seq), BF16)

    @pl.when((pl.program_id(0) == 0) & (qb_idx == 0))
    def _():
        j_ = lax.broadcasted_iota(I32, (2 * DSA_QB, DSA_QB), 0)
        i_ = lax.broadcasted_iota(I32, (2 * DSA_QB, DSA_QB), 1)
        rel = i_ + DSA_QB - j_
        bucket = _t5_bucket(rel)
        for h in range(N_HEADS):
            far = rb_ref[REL_BUCKETS - 1, h]
            band = jnp.zeros((2 * DSA_QB, DSA_QB), F32)
            for bk in range(REL_BUCKETS - 1):
                band = jnp.where(bucket == bk, (rb_ref[bk, h] - far) * LOG2E, band)
            band_scr[h] = jnp.where(rel >= 0, band, 0.0)

    w_t = tok_ref[...].T
    key_s = lax.broadcasted_iota(I32, (DSA_KC, DSA_QB), 0)
    row_t = t0 + lax.broadcasted_iota(I32, (DSA_KC, DSA_QB), 1)

    def score_chunk(c, carry):
        kmax, kmin = carry
        ks = pl.ds(pl.multiple_of(c * DSA_KC, DSA_KC), DSA_KC)
        ka = ka16_scr[ks, :]
        kb = kb16i_scr[ks, :]
        acc = jnp.zeros((DSA_KC, DSA_QB), F32)
        for p in range(IDX_HEADS // 2):
            qp = qi_ref[:, p * LANE:(p + 1) * LANE]
            acc = acc + jnp.maximum(_dot_nt(ka, qp), 0.0) * w_t[2 * p:2 * p + 1, :]
            acc = acc + jnp.maximum(_dot_nt(kb, qp), 0.0) * w_t[2 * p + 1:2 * p + 2, :]
        key = _score_to_key(acc)
        valid = (c * DSA_KC + key_s) <= row_t
        key_scr[ks, :] = jnp.where(valid, key, INT_MIN)
        kmax = jnp.maximum(kmax, jnp.max(jnp.where(valid, key, INT_MIN), axis=0, keepdims=True))
        kmin = jnp.minimum(kmin, jnp.min(jnp.where(valid, key, INT_MAX), axis=0, keepdims=True))
        return kmax, kmin

    kmax, kmin = lax.fori_loop(0, n_kc, score_chunk, (jnp.full(row_vec, INT_MIN, I32),
                                                     jnp.full(row_vec, INT_MAX, I32)))

    def scan_keys(cand, with_below):
        def body(c, carry):
            cnt, below = carry
            ks = pl.ds(pl.multiple_of(c * DSA_KC, DSA_KC), DSA_KC)
            keys = key_scr[ks, :]
            ge = keys >= cand
            ones = ge.astype(I32)
            low = jnp.where(ge, INT_MIN, keys)
            for u in range(DSA_KC // SUBLANE):
                us = slice(u * SUBLANE, (u + 1) * SUBLANE)
                cnt = cnt + ones[us, :]
                if with_below:
                    below = jnp.maximum(below, low[us, :])
            return cnt, below
        cnt, below = lax.fori_loop(0, n_kc, body, (jnp.zeros((SUBLANE, DSA_QB), I32),
                                                   jnp.full((SUBLANE, DSA_QB), INT_MIN, I32)))
        cnt = jnp.sum(cnt, axis=0, keepdims=True)
        if with_below:
            return cnt, jnp.max(below, axis=0, keepdims=True)
        return cnt

    def open_rows(lo, hi, c_lo):
        return (c_lo > topk) & (hi - 1 > lo)

    def any_row(flag):
        return jnp.max(jnp.where(flag, 1, 0))

    def update(cand, cnt, lo, hi, c_lo, c_hi):
        ge = cnt >= topk
        return (jnp.where(ge, cand, lo), jnp.where(ge, hi, cand),
                jnp.where(ge, cnt, c_lo), jnp.where(ge, c_hi, cnt))

    def crowded(lo, hi, c_lo, c_hi):
        return any_row(open_rows(lo, hi, c_lo) & (c_lo - c_hi > DSA_FEW_KEYS))

    def halve_step(it, lo, hi, c_lo, c_hi):
        key_mid = (lo >> 1) + (hi >> 1) + (lo & hi & 1)
        score_mid = _score_to_key(0.5 * _key_to_score(lo) + 0.5 * _key_to_score(hi - 1))
        cand = jnp.where(it < DSA_SCORE_MID_STEPS, score_mid, key_mid)
        cand = jnp.minimum(jnp.maximum(cand, lo + 1), hi - 1)
        cand = jnp.where(hi - 1 > lo, cand, lo)
        return update(cand, scan_keys(cand, False), lo, hi, c_lo, c_hi)

    def halve_body(st):
        it, _, lo, hi, c_lo, c_hi = st
        go = crowded(lo, hi, c_lo, c_hi)
        return (it + 1, go) + halve_step(it, lo, hi, c_lo, c_hi)

    def walk_step(lo, hi, c_lo, c_hi, nxt):
        is_open = open_rows(lo, hi, c_lo)
        cand = jnp.where(is_open, nxt, lo)
        cnt, below = scan_keys(cand, True)
        ge = cnt >= topk
        hi = jnp.where(is_open, jnp.where(ge, cand + 1, cand), hi)
        c_hi = jnp.where(is_open & jnp.logical_not(ge), cnt, c_hi)
        lo = jnp.where(is_open & ge, cand, lo)
        c_lo = jnp.where(is_open & ge, cnt, c_lo)
        nxt = jnp.where(ge, nxt, below)
        return lo, hi, c_lo, c_hi, nxt

    def walk_body(st):
        go = any_row(open_rows(st[1], st[2], st[3]))
        return (go,) + walk_step(*st[1:])

    n_valid = jnp.minimum(t0 + lax.broadcasted_iota(I32, row_vec, 1) + 1, seq)
    st = (kmin, kmax + 1, n_valid, jnp.zeros(row_vec, I32))
    st = lax.fori_loop(0, DSA_HALVE_FIXED, lambda it, s: halve_step(it, *s), st)
    st = lax.while_loop(lambda s: s[1] > 0, halve_body,
                        (jnp.int32(DSA_HALVE_FIXED), crowded(*st)) + st)[2:]
    _, nxt0 = scan_keys(st[1], True)
    st = lax.fori_loop(0, DSA_WALK_FIXED, lambda it, s: walk_step(*s), st + (nxt0,))
    _, thr, hi, n_ge, n_gt, _ = lax.while_loop(
        lambda s: s[0] > 0, walk_body, (any_row(open_rows(st[0], st[1], st[2])),) + st)

    tied = n_ge > topk
    has_tie = jnp.max(jnp.where(tied, 1, 0)) > 0

    @pl.when(has_tie)
    def _():
        room = (topk - n_gt).astype(F32)
        ii = lax.broadcasted_iota(I32, (LANE, LANE), 0)
        jj = lax.broadcasted_iota(I32, (LANE, LANE), 1)
        lower = (ii >= jj).astype(BF16)

        def body(c, seen):
            ks = pl.ds(pl.multiple_of(c * LANE, LANE), LANE)
            kk = key_scr[ks, :]
            eq = kk == thr
            rank = seen + _dot(lower, eq.astype(BF16))
            drop = eq & (rank > room) & tied
            key_scr[ks, :] = jnp.where(drop, INT_MIN, kk)
            return seen + jnp.sum(eq.astype(F32), axis=0, keepdims=True)
        lax.fori_loop(0, n_kc * (DSA_KC // LANE), body, jnp.zeros(row_vec, F32))

    def mask_chunk(c, carry):
        ks = pl.ds(pl.multiple_of(c * DSA_KC, DSA_KC), DSA_KC)
        sel = jnp.where(key_scr[ks, :] >= thr, 0.0, -jnp.inf).astype(F32)
        key_scr[ks, :] = pltpu.bitcast(sel, I32)
        return carry
    lax.fori_loop(0, n_kc, mask_chunk, 0)

    heads = range(N_HEADS)
    hsl = [slice(h * HEAD_DIM, (h + 1) * HEAD_DIM) for h in heads]

    def logit_chunk(c, ms):
        ks = pl.ds(pl.multiple_of(c * DSA_KC, DSA_KC), DSA_KC)
        k_c = kb16_scr[ks, :]
        sel = pltpu.bitcast(key_scr[ks, :], F32)
        out = []
        for h in heads:
            s = _dot_nt(k_c, q_ref[:, hsl[h]]) * (HEAD_DIM ** -0.5 * LOG2E) + sel
            lg_scr[h, ks, :] = s
            out.append(jnp.maximum(ms[h], jnp.max(s, axis=0, keepdims=True)))
        return tuple(out)
    ms = lax.fori_loop(0, n_kc, logit_chunk,
                       tuple(jnp.full(row_vec, -jnp.inf, F32) for _ in heads))

    band_off = pl.multiple_of(jnp.maximum(qb_idx - 1, 0) * DSA_QB, DSA_QB)
    ws = pl.ds(band_off, 2 * DSA_QB)
    ms = list(ms)
    for h in heads:
        band_h = band_scr[h]
        band_first = jnp.concatenate([band_h[DSA_QB:, :], jnp.zeros((DSA_QB, DSA_QB), F32)], axis=0)
        win = lg_scr[h, ws, :] + jnp.where(qb_idx == 0, band_first, band_h)
        lg_scr[h, ws, :] = win
        ms[h] = jnp.maximum(ms[h], jnp.max(win, axis=0, keepdims=True))

    def pv_chunk(c, accs):
        ks = pl.ds(pl.multiple_of(c * DSA_KC, DSA_KC), DSA_KC)
        vt_c = vt_scr[:, ks]
        return tuple(accs[h] + _dot(vt_c, jnp.exp2(lg_scr[h, ks, :] - ms[h]).astype(BF16))
                     for h in heads)
    accs = lax.fori_loop(0, n_kc, pv_chunk,
                         tuple(jnp.zeros((2 * HEAD_DIM, DSA_QB), F32) for _ in heads))
    for h in heads:
        o_ref[:, hsl[h]] = (accs[h][:HEAD_DIM, :] / accs[h][HEAD_DIM:, :]).T.astype(BF16)


def dsa_attention(proj, q, q_idx, prep_tok, rel_bias, batch, seq):
    n = proj.shape[0]
    nq = seq // DSA_QB
    topk = min(TOPK_MAX, seq // 4)
    wi = IDX_HEADS * IDX_DIM
    rowblk = lambda w, cb: pl.BlockSpec((DSA_QB, w), lambda b, i: (b * nq + i, cb))
    seqblk = lambda off: pl.BlockSpec((seq, LANE), lambda b, i: (b, off // LANE))
    return pl.pallas_call(
        functools.partial(_dsa_kernel, seq=seq, topk=topk),
        grid=(batch, nq),
        in_specs=[pl.BlockSpec(memory_space=pltpu.SMEM),
                  rowblk(WIDTH, 0), rowblk(wi, 0), rowblk(LANE, 0),
                  seqblk(C_DK), seqblk(C_DV), seqblk(C_KA), seqblk(C_KB)],
        out_specs=pl.BlockSpec((DSA_QB, WIDTH), lambda b, i: (b * nq + i, 0)),
        out_shape=jax.ShapeDtypeStruct((n, WIDTH), BF16),
        scratch_shapes=[pltpu.VMEM((seq, DSA_QB), I32),
                        pltpu.VMEM((N_HEADS, seq, DSA_QB), F32),
                        pltpu.VMEM((N_HEADS, 2 * DSA_QB, DSA_QB), F32),
                        pltpu.VMEM((seq, LANE), BF16),
                        pltpu.VMEM((2 * HEAD_DIM, seq), BF16),
                        pltpu.VMEM((seq, LANE), BF16),
                        pltpu.VMEM((seq, LANE), BF16)],
        compiler_params=_cparams(("arbitrary", "arbitrary")),
        name="dsa_attention",
    )(rel_bias, q, q_idx, prep_tok, proj, proj, proj, proj)


GDN_T = 256
GDN_GROUP = 2
GDN_HALO = 8


def _gdn_kernel(x_ref, cw_ref, ng_ref, tok_ref, tr_ref, o_ref,
                xq_scr, xk_scr, xv_scr, state_scr):
    first = pl.program_id(1) == 0

    @pl.when(first)
    def _():
        state_scr[...] = jnp.zeros_like(state_scr)
        for scr in (xq_scr, xk_scr, xv_scr):
            scr[0:GDN_HALO, :] = jnp.zeros((GDN_HALO, WIDTH), F32)

    def conv(col, scr, w_off):
        scr[GDN_HALO:, :] = x_ref[:, col - C_GQ:col - C_GQ + WIDTH]
        y = jnp.zeros((GDN_T, WIDTH), F32)
        for i in range(GDN_CONV):
            st = GDN_HALO - (GDN_CONV - 1) + i
            y = y + scr[st:st + GDN_T, :] * cw_ref[i:i + 1, w_off:w_off + WIDTH]
        scr[0:GDN_HALO, :] = scr[GDN_T:GDN_T + GDN_HALO, :]
        return _silu(y)

    qc = conv(C_GQ, xq_scr, 0)
    kc = conv(C_GK, xk_scr, WIDTH)
    vc = conv(C_GV, xv_scr, 2 * WIDTH)
    z_off = C_GZ - C_GQ
    tok = tok_ref[...]
    c = GDN_CHUNK
    heads = range(N_HEADS)
    hsl = [slice(h * HEAD_DIM, (h + 1) * HEAD_DIM) for h in heads]

    def l2norm_heads(x, scale):
        return jnp.concatenate(
            [x[:, s] * (lax.rsqrt(jnp.sum(x[:, s] * x[:, s], axis=-1, keepdims=True) + EPS) * scale)
             for s in hsl], axis=1)

    qf = l2norm_heads(qc, HEAD_DIM ** -0.5)
    kf = l2norm_heads(kc, 1.0)

    grp = GDN_GROUP
    gw = grp * HEAD_DIM
    nb = grp * c
    ri = lax.broadcasted_iota(I32, (nb, nb), 0)
    ci = lax.broadcasted_iota(I32, (nb, nb), 1)
    tril = ((ri // c) == (ci // c)) & (ri >= ci)
    eye = (ri == ci).astype(F32)
    pair_masks = []
    for lg in range(c.bit_length() - 1):
        pair_masks.append(((ri >> (lg + 1)) == (ci >> (lg + 1)))
                          & (((ri >> lg) & 1) == 1) & (((ci >> lg) & 1) == 0))
    lane_head = lax.broadcasted_iota(I32, (c, gw), 1) // HEAD_DIM
    row_head = lax.broadcasted_iota(I32, (nb, HEAD_DIM), 0) // c

    def spread(x):
        return jnp.concatenate([jnp.where(lane_head == u, x, 0.0) for u in range(grp)], axis=0)

    def stack(x):
        return jnp.concatenate([x[:, hsl[u]] for u in range(grp)], axis=0)

    def spread_lanes(x):
        return jnp.concatenate([jnp.where(row_head == u, x, 0.0) for u in range(grp)], axis=1)

    def split(x):
        hi = x.astype(BF16)
        return hi, (x - hi.astype(F32)).astype(BF16)

    def dot_split(a, b):
        return _dot(a[0], b[0]) + (_dot(a[0], b[1]) + _dot(a[1], b[0]))

    items = [(j, gi) for j in range(GDN_T // c) for gi in range(N_HEADS // grp)]
    pre = []
    for j, gi in items:
        rs = slice(j * c, (j + 1) * c)
        last = slice((j + 1) * c - 1, (j + 1) * c)
        gh = [gi * grp + u for u in range(grp)]
        gsl = slice(gi * gw, (gi + 1) * gw)
        qj, kj, vj = qf[rs, gsl], kf[rs, gsl], vc[rs, gsl]
        b_col = jnp.concatenate([tok[rs, L_GB + h:L_GB + h + 1] for h in gh], axis=0)
        g_col = jnp.concatenate([tok[rs, L_GA + h:L_GA + h + 1] for h in gh], axis=0)
        g_row = jnp.concatenate([tr_ref[0, L_GA + h:L_GA + h + 1, rs] for h in gh], axis=1)
        g_last = [tok[last, L_GA + h:L_GA + h + 1] for h in gh]
        g_last_col = jnp.concatenate([jnp.broadcast_to(g, (c, 1)) for g in g_last], axis=0)
        k_sp = spread(kj)
        q_sp = spread(qj)
        k_sp16 = k_sp.astype(BF16)
        decay = jnp.exp(jnp.where(tril, g_col - g_row, -jnp.inf))
        eg = jnp.exp(g_col)
        pre.append(dict(
            gh=gh, gsl=gsl,
            l_mat=b_col * _dot_nt(k_sp16, k_sp16) * decay,
            rhs=jnp.concatenate([stack(vj) * b_col, stack(kj) * (b_col * eg)], axis=1),
            qk=_dot_nt(q_sp.astype(BF16), k_sp16) * decay,
            q_dec=q_sp * eg,
            k_dec=k_sp * jnp.exp(g_last_col - g_col),
            e_last=jnp.concatenate([jnp.broadcast_to(jnp.exp(g), (HEAD_DIM, 1)) for g in g_last], axis=0)))

    t_inv = [eye - jnp.where(pair_masks[0], p["l_mat"], 0.0) for p in pre]
    for pm in pair_masks[1:]:
        t_s = [split(t) for t in t_inv]
        m_t = [dot_split(split(jnp.where(pm, p["l_mat"], 0.0)), ts) for p, ts in zip(pre, t_s)]
        t_inv = [t - dot_split(ts, split(m)) for t, ts, m in zip(t_inv, t_s, m_t)]
    sols = [dot_split(split(t), split(p["rhs"])) for t, p in zip(t_inv, pre)]

    outs = [[] for _ in heads]
    for p, sol in zip(pre, sols):
        u0 = sol[:, :HEAD_DIM]
        kcum = sol[:, HEAD_DIM:]
        st = state_scr[p["gsl"], :]
        stb = st.astype(BF16)
        v_new = u0 - _dot(spread_lanes(kcum).astype(BF16), stb)
        v_new_b = v_new.astype(BF16)
        o_st = _dot(p["q_dec"].astype(BF16), stb) + _dot(p["qk"].astype(BF16), v_new_b)
        state_scr[p["gsl"], :] = st * p["e_last"] + _dot(p["k_dec"].T.astype(BF16), v_new_b)
        for u, h in enumerate(p["gh"]):
            outs[h].append(o_st[u * c:(u + 1) * c, :])

    for h in heads:
        o = jnp.concatenate(outs[h], axis=0)
        ms = jnp.mean(o * o, axis=-1, keepdims=True)
        on = o * lax.rsqrt(ms + EPS) * ng_ref[...]
        z = x_ref[:, z_off + hsl[h].start:z_off + hsl[h].stop]
        o_ref[:, hsl[h]] = (on * _silu(z)).astype(BF16)


def gated_deltanet(proj, prep_tok, prep_tr, conv_w, norm_g, batch, seq):
    n = proj.shape[0]
    t = GDN_T
    nt = seq // t
    return pl.pallas_call(
        _gdn_kernel,
        grid=(batch, nt),
        in_specs=[pl.BlockSpec((t, 4 * WIDTH), lambda b, i: (b * nt + i, C_GQ // (4 * WIDTH))),
                  pl.BlockSpec((GDN_CONV, 3 * WIDTH), lambda b, i: (0, 0)),
                  pl.BlockSpec((1, HEAD_DIM), lambda b, i: (0, 0)),
                  pl.BlockSpec((t, LANE), lambda b, i: (b * nt + i, 0)),
                  pl.BlockSpec((1, PREP_ROWS, t), lambda b, i: (b, 0, i))],
        out_specs=pl.BlockSpec((t, WIDTH), lambda b, i: (b * nt + i, 0)),
        out_shape=jax.ShapeDtypeStruct((n, WIDTH), BF16),
        scratch_shapes=[pltpu.VMEM((t + GDN_HALO, WIDTH), F32),
                        pltpu.VMEM((t + GDN_HALO, WIDTH), F32),
                        pltpu.VMEM((t + GDN_HALO, WIDTH), F32),
                        pltpu.VMEM((N_HEADS * HEAD_DIM, HEAD_DIM), F32)],
        compiler_params=_cparams(("arbitrary", "arbitrary")),
        name="gated_deltanet",
    )(proj, conv_w, norm_g.reshape(1, HEAD_DIM), prep_tok, prep_tr)


def _merge_kernel(h_ref, b0_ref, b1_ref, b2_ref, b3_ref, g0_ref, g1_ref, g2_ref, g3_ref,
                  wb_ref, o_ref):
    h = h_ref[...]
    acc = None
    for n, (b_ref, g_ref) in enumerate(zip((b0_ref, b1_ref, b2_ref, b3_ref),
                                           (g0_ref, g1_ref, g2_ref, g3_ref))):
        gate = jax.nn.sigmoid(_dot(h, g_ref[...]))
        term = gate * _dot(b_ref[...], wb_ref[n])
        acc = term if acc is None else acc + term
    o_ref[...] = acc.astype(BF16)


def merge_branches(h, branches, w_gate, w_branch, *, tm=1024, tn=512):
    _, n, d = h.shape
    nj = d // tn
    bspec = pl.BlockSpec((tm, WIDTH), lambda j, i: (i, 0))
    gspec = lambda k: pl.BlockSpec((d, tn), lambda j, i: (0, k * nj + j))
    return pl.pallas_call(
        _merge_kernel,
        grid=(nj, n // tm),
        in_specs=[pl.BlockSpec((None, tm, d), lambda j, i: (0, i, 0)),
                  bspec, bspec, bspec, bspec,
                  gspec(0), gspec(1), gspec(2), gspec(3),
                  pl.BlockSpec((N_BRANCH, WIDTH, tn), lambda j, i: (0, 0, j))],
        out_specs=pl.BlockSpec((tm, tn), lambda j, i: (i, j)),
        out_shape=jax.ShapeDtypeStruct((n, d), BF16),
        compiler_params=_cparams(("arbitrary", "arbitrary")),
        name="merge_branches",
    )(h, *branches, w_gate, w_gate, w_gate, w_gate, w_branch)


def _resid_mm_kernel(a_ref, w_ref, x_ref, o_ref):
    o_ref[...] = x_ref[...] + _dot(a_ref[...], w_ref[...])


def resid_matmul(a, w, x, *, tm=512, tn=1024, name="resid_matmul"):
    n, k = a.shape
    d = w.shape[1]
    return pl.pallas_call(
        _resid_mm_kernel,
        grid=(d // tn, n // tm),
        in_specs=[pl.BlockSpec((tm, k), lambda j, i: (i, 0)),
                  pl.BlockSpec((k, tn), lambda j, i: (0, j)),
                  pl.BlockSpec((tm, tn), lambda j, i: (i, j))],
        out_specs=pl.BlockSpec((tm, tn), lambda j, i: (i, j)),
        out_shape=jax.ShapeDtypeStruct((n, d), F32),
        compiler_params=_cparams(("arbitrary", "arbitrary")),
        name=name,
    )(a, w, x)


FFN_HALO = 8
FFN_SUB = 512


def _ffn1_kernel(x_ref, g_ref, wg_ref, wu_ref, cw_ref, cb_ref, o_ref, h_scr, gt_scr, halo_scr,
                 *, tm, tiles_per_seq):
    i = pl.program_id(1)

    @pl.when(i == 0)
    def _():
        halo_scr[...] = jnp.zeros_like(halo_scr)

    def body(r, carry):
        rows = pl.ds(pl.multiple_of(r * NORM_ROWS, NORM_ROWS), NORM_ROWS)
        h_scr[rows, :] = _rmsnorm_rows(x_ref, g_ref, rows).astype(BF16)
        return carry
    lax.fori_loop(0, tm // NORM_ROWS, body, 0)

    h = h_scr[...]
    seq_start = (i % tiles_per_seq) == 0
    tn = o_ref.shape[1]
    for off in range(0, tn, FFN_SUB):
        cs = slice(off, min(off + FFN_SUB, tn))
        g = _dot(h, wg_ref[:, cs])
        gt_scr[FFN_HALO:, cs] = g
        gt_scr[0:FFN_HALO, cs] = jnp.where(seq_start, 0.0, halo_scr[:, cs])
        halo_scr[:, cs] = g[tm - FFN_HALO:, :]
        y = cb_ref[:, cs] + g * cw_ref[FFN_CONV - 1:FFN_CONV, cs]
        for t in range(FFN_CONV - 1):
            st = FFN_HALO - (FFN_CONV - 1) + t
            y = y + gt_scr[st:st + tm, cs] * cw_ref[t:t + 1, cs]
        o_ref[:, cs] = (_silu(y) * _dot(h, wu_ref[:, cs])).astype(BF16)


def conv_ffn_up(x, gain, w_gate, w_up, conv_w, conv_b, seq, *, tm=512, col_parts=2):
    n, d = x.shape
    f = w_gate.shape[1]
    tn = f // col_parts
    wspec = pl.BlockSpec((d, tn), lambda j, i: (0, j), pipeline_mode=pl.Buffered(1))
    return pl.pallas_call(
        functools.partial(_ffn1_kernel, tm=tm, tiles_per_seq=seq // tm),
        grid=(col_parts, n // tm),
        in_specs=[pl.BlockSpec((tm, d), lambda j, i: (i, 0)),
                  pl.BlockSpec((1, d), lambda j, i: (0, 0)),
                  wspec, wspec,
                  pl.BlockSpec((FFN_CONV, tn), lambda j, i: (0, j)),
                  pl.BlockSpec((1, tn), lambda j, i: (0, j))],
        out_specs=pl.BlockSpec((tm, tn), lambda j, i: (i, j)),
        out_shape=jax.ShapeDtypeStruct((n, f), BF16),
        scratch_shapes=[pltpu.VMEM((tm, d), BF16),
                        pltpu.VMEM((tm + FFN_HALO, tn), F32),
                        pltpu.VMEM((FFN_HALO, tn), F32)],
        compiler_params=_cparams(("arbitrary", "arbitrary")),
        name="conv_ffn_up",
    )(x, gain.reshape(1, d), w_gate, w_up, conv_w, conv_b.reshape(1, f))


IN_SIZES = (WIDTH, WIDTH, WIDTH, WIDTH,
            DSA_Q_RANK, HEAD_DIM, HEAD_DIM, IDX_DIM, IDX_HEADS,
            WIDTH, WIDTH, WIDTH, N_HEADS,
            WIDTH, WIDTH, WIDTH, WIDTH, N_HEADS, N_HEADS)
IN_NAMES = ("r_q", "r_k", "r_v", "r_g", "d_cq", "d_k", "d_v", "i_k", "i_w",
            "f_q", "f_k", "f_v", "f_f", "g_q", "g_k", "g_v", "g_z", "g_b", "g_a")
IN_PLAN = (("r_q", C_RQ), ("r_k", C_RK), ("r_v", C_RV), ("r_g", C_RG),
           ("f_q", C_FQ), ("f_k", C_FK), ("f_v", C_FV),
           ("g_q", C_GQ), ("g_k", C_GK), ("g_v", C_GV), ("g_z", C_GZ),
           ("d_k", C_DK), ("d_cq", C_DCQ), ("d_v", C_DV),
           ("i_k", C_KA), ("i_k", C_KB + IDX_DIM),
           ("i_w", C_SM + L_IW), ("f_f", C_SM + L_FF), ("g_b", C_SM + L_GB), ("g_a", C_SM + L_GA))


def _prep_w_in_kernel(w_ref, m_ref, g_ref):
    src = {}
    off = 0
    for name, size in zip(IN_NAMES, IN_SIZES):
        src[name] = (off, size)
        off += size
    m_ref[...] = jnp.zeros_like(m_ref)
    for name, dst in IN_PLAN:
        so, w = src[name]
        m_ref[:, dst:dst + w] = w_ref[:, so:so + w].astype(BF16)
    g_ref[...] = w_ref[:, off:off + g_ref.shape[1]].astype(BF16)


def prep_w_in(w_in, layer, *, tr=256):
    _, d, c = w_in.shape
    return pl.pallas_call(
        _prep_w_in_kernel,
        grid=(d // tr,),
        in_specs=[pl.BlockSpec((None, tr, c), lambda i: (layer, i, 0))],
        out_specs=[pl.BlockSpec((tr, C_TOT), lambda i: (i, 0)),
                   pl.BlockSpec((tr, N_BRANCH * d), lambda i: (i, 0))],
        out_shape=[jax.ShapeDtypeStruct((d, C_TOT), BF16),
                   jax.ShapeDtypeStruct((d, N_BRANCH * d), BF16)],
        compiler_params=_cparams(("arbitrary",)),
        name="prep_w_in",
    )(w_in)


def _cast_kernel(w_ref, o_ref):
    o_ref[...] = w_ref[...].astype(BF16)


CAST_BLOCK_BYTES = 12 * 1024 * 1024


def cast_layer(w, layer):
    _, r, c = w.shape
    tr = r
    while tr * c * 4 > CAST_BLOCK_BYTES and tr % 16 == 0:
        tr //= 2
    return pl.pallas_call(
        _cast_kernel,
        grid=(r // tr,),
        in_specs=[pl.BlockSpec((None, tr, c), lambda i: (layer, i, 0))],
        out_specs=pl.BlockSpec((tr, c), lambda i: (i, 0)),
        out_shape=jax.ShapeDtypeStruct((r, c), BF16),
        compiler_params=_cparams(("arbitrary",)),
        name="cast_bf16",
    )(w)


def cast_branch(w_branch, layer):
    _, nbr, r, c = w_branch.shape
    return pl.pallas_call(
        _cast_kernel,
        grid=(nbr,),
        in_specs=[pl.BlockSpec((None, None, r, c), lambda i: (layer, i, 0, 0))],
        out_specs=pl.BlockSpec((None, r, c), lambda i: (i, 0, 0)),
        out_shape=jax.ShapeDtypeStruct((nbr, r, c), BF16),
        compiler_params=_cparams(("arbitrary",)),
        name="cast_branch",
    )(w_branch)


def kernel(x, norm_mix, w_in, dsa_cq_norm, dsa_w_uq, dsa_w_qidx, fox_f_bias, gdn_conv, gdn_a_log,
           gdn_dt_bias, gdn_norm, w_branch, w_out, rel_bias, norm_ffn, ffn_w_gate, ffn_w_up,
           ffn_conv, ffn_conv_b, ffn_w_down, final_norm):
    batch, seq, d = x.shape
    depth = w_in.shape[0]
    xf = x.reshape(batch * seq, d)
    ret_tables = _retention_tables(seq)
    for l in range(depth):
        w_main, w_gate = prep_w_in(w_in, l)
        proj, h = norm_proj(xf, norm_mix[l], w_main)
        par = jnp.zeros((SUBLANE, LANE), F32)
        par = par.at[0, L_FF:L_FF + N_HEADS].set(fox_f_bias[l])
        par = par.at[0, L_GA:L_GA + N_HEADS].set(gdn_dt_bias[l])
        par = par.at[1, L_GA:L_GA + N_HEADS].set(gdn_a_log[l])
        prep_tok, prep_tr = prep_small(proj, par, batch, seq)
        o_ret = retention(proj, ret_tables, batch, seq)
        q_dsa, q_idx = dsa_proj(proj, dsa_cq_norm[l], cast_layer(dsa_w_uq, l), cast_layer(dsa_w_qidx, l))
        o_dsa = dsa_attention(proj, q_dsa, q_idx, prep_tok, rel_bias, batch, seq)
        o_fox = fox_attention(proj, prep_tr, batch, seq)
        o_gdn = gated_deltanet(proj, prep_tok, prep_tr, gdn_conv[l], gdn_norm[l], batch, seq)
        merged = merge_branches(h, (o_ret, o_dsa, o_fox, o_gdn), w_gate, cast_branch(w_branch, l))
        xf = resid_matmul(merged, cast_layer(w_out, l), xf, tn=d, name="out_proj")
        act = conv_ffn_up(xf, norm_ffn[l], cast_layer(ffn_w_gate, l), cast_layer(ffn_w_up, l),
                          ffn_conv[l], ffn_conv_b[l], seq)
        xf = resid_matmul(act, cast_layer(ffn_w_down, l), xf, name="ffn_down")
    return rmsnorm(xf, final_norm).reshape(batch, seq, d)
```

```python
import functools
import math

import jax
import jax.numpy as jnp
from jax import lax
from jax.experimental import pallas as pl
from jax.experimental.pallas import tpu as pltpu

F32 = jnp.float32
BF16 = jnp.bfloat16
I32 = jnp.int32

HEAD_DIM = 128
N_HEADS = 4
WIDTH = N_HEADS * HEAD_DIM
N_BRANCH = 4
RET_CHUNK = 128
ROPE_BASE = 10000.0
DSA_Q_RANK = 384
IDX_HEADS = 16
IDX_DIM = 64
TOPK_MAX = 256
GDN_CONV = 4
GDN_CHUNK = 64
REL_BUCKETS = 32
REL_MAX_DIST = 128
FFN_CONV = 3
EPS = 1e-6

LANE = 128
SUBLANE = 8
VMEM_LIMIT = 56 * 1024 * 1024

C_RQ, C_RK, C_RV, C_RG = 0, 512, 1024, 1536
C_FQ, C_DK, C_DCQ = 2048, 2560, 2688
C_FK, C_FV = 3072, 3584
C_GQ, C_GK, C_GV, C_GZ = 4096, 4608, 5120, 5632
C_DV, C_KA, C_KB, C_SM = 6144, 6272, 6400, 6528
C_TOT = 6656
L_IW, L_FF, L_GB, L_GA = 0, 16, 20, 24

LOG2E = 1.4426950408889634
INT_MIN = -(2 ** 31)
INT_MAX = 2 ** 31 - 1
HIGHEST = lax.Precision.HIGHEST


def _cparams(sem, vmem=VMEM_LIMIT):
    return pltpu.CompilerParams(dimension_semantics=sem, vmem_limit_bytes=vmem)


def _dot(a, b):
    return jnp.dot(a, b, preferred_element_type=F32)


def _dot_nt(a, b):
    return lax.dot_general(a, b, (((1,), (1,)), ((), ())), preferred_element_type=F32)


def _silu(x):
    return x * jax.nn.sigmoid(x)


NORM_ROWS = 128


def _rmsnorm_rows(x_ref, g_ref, rows):
    x = x_ref[rows, :]
    ms = jnp.mean(x * x, axis=-1, keepdims=True)
    return x * lax.rsqrt(ms + EPS) * g_ref[...]


PROJ_SUB = 512


def _norm_proj_kernel(x_ref, g_ref, w_ref, o_ref, h_ref, h_scr, *, tm):
    def body(r, carry):
        rows = pl.ds(pl.multiple_of(r * NORM_ROWS, NORM_ROWS), NORM_ROWS)
        hb = _rmsnorm_rows(x_ref, g_ref, rows).astype(BF16)
        h_scr[rows, :] = hb
        h_ref[rows, :] = hb
        return carry
    lax.fori_loop(0, tm // NORM_ROWS, body, 0)

    h = h_scr[...]
    tn = o_ref.shape[1]
    for off in range(0, tn, PROJ_SUB):
        cs = slice(off, min(off + PROJ_SUB, tn))
        o_ref[:, cs] = _dot(h, w_ref[:, cs])


def norm_proj(x, gain, w, *, tm=512, col_parts=2):
    n, d = x.shape
    c = w.shape[1]
    tn = c // col_parts
    return pl.pallas_call(
        functools.partial(_norm_proj_kernel, tm=tm),
        grid=(col_parts, n // tm),
        in_specs=[pl.BlockSpec((tm, d), lambda j, i: (i, 0)),
                  pl.BlockSpec((1, d), lambda j, i: (0, 0)),
                  pl.BlockSpec((d, tn), lambda j, i: (0, j), pipeline_mode=pl.Buffered(1))],
        out_specs=[pl.BlockSpec((tm, tn), lambda j, i: (i, j)),
                   pl.BlockSpec((None, tm, d), lambda j, i: (j, i, 0))],
        out_shape=[jax.ShapeDtypeStruct((n, c), F32),
                   jax.ShapeDtypeStruct((col_parts, n, d), BF16)],
        scratch_shapes=[pltpu.VMEM((tm, d), BF16)],
        compiler_params=_cparams(("arbitrary", "arbitrary")),
        name="norm_proj",
    )(x, gain.reshape(1, d), w)


def _rmsnorm_kernel(x_ref, g_ref, o_ref, *, tm):
    def body(r, carry):
        rows = pl.ds(pl.multiple_of(r * NORM_ROWS, NORM_ROWS), NORM_ROWS)
        o_ref[rows, :] = _rmsnorm_rows(x_ref, g_ref, rows)
        return carry
    lax.fori_loop(0, tm // NORM_ROWS, body, 0)


def rmsnorm(x, gain, *, tm=512):
    n, d = x.shape
    return pl.pallas_call(
        functools.partial(_rmsnorm_kernel, tm=tm),
        grid=(n // tm,),
        in_specs=[pl.BlockSpec((tm, d), lambda i: (i, 0)),
                  pl.BlockSpec((1, d), lambda i: (0, 0))],
        out_specs=pl.BlockSpec((tm, d), lambda i: (i, 0)),
        out_shape=jax.ShapeDtypeStruct((n, d), F32),
        compiler_params=_cparams(("arbitrary",)),
        name="final_rmsnorm",
    )(x, gain.reshape(1, d))


def _prep_kernel(s_ref, par_ref, tok_ref, tr_ref, carry_scr):
    @pl.when(pl.program_id(1) == 0)
    def _():
        carry_scr[...] = jnp.zeros_like(carry_scr)

    s = s_ref[...]
    lane = lax.broadcasted_iota(I32, (LANE, LANE), 1)
    row = lax.broadcasted_iota(I32, (LANE, LANE), 0)
    z = s + par_ref[0:1, :]
    soft = jnp.maximum(z, 0.0) + jnp.log1p(jnp.exp(-jnp.abs(z)))
    log_sig = z - soft
    sig = jax.nn.sigmoid(z)
    g_val = -jnp.exp(par_ref[1:2, :]) * soft
    is_f = (lane[0:1] >= L_FF) & (lane[0:1] < L_FF + N_HEADS)
    is_b = (lane[0:1] >= L_GB) & (lane[0:1] < L_GB + N_HEADS)
    is_a = (lane[0:1] >= L_GA) & (lane[0:1] < L_GA + N_HEADS)
    pre = jnp.where(is_f, log_sig, jnp.where(is_a, g_val, 0.0))
    tri = (row >= lane).astype(F32)
    tri_blk = ((row >= lane) & ((row // GDN_CHUNK) == (lane // GDN_CHUNK))).astype(F32)
    subs = [slice(u * LANE, (u + 1) * LANE) for u in range(PREP_T // LANE)]
    cum_full = [jnp.dot(tri, pre[u], precision=HIGHEST, preferred_element_type=F32) for u in subs]
    cum_blk = [jnp.dot(tri_blk, pre[u], precision=HIGHEST, preferred_element_type=F32) for u in subs]
    scale_iw = IDX_HEADS ** -0.5 * IDX_DIM ** -0.5
    carry = carry_scr[0:1, :]
    for u, cf, cb in zip(subs, cum_full, cum_blk):
        c_fox = cf + carry
        carry = c_fox[LANE - 1:LANE, :]
        out = jnp.where(is_f, c_fox,
                        jnp.where(is_a, cb,
                                  jnp.where(is_b, sig[u],
                                            jnp.where(lane[0:1] < IDX_HEADS, s[u] * scale_iw, 0.0))))
        tok_ref[u, :] = out
        tr_ref[0, :, u] = out.T[0:PREP_ROWS, :]
    carry_scr[0:1, :] = carry


PREP_T = 512
PREP_ROWS = 32


def prep_small(proj, par, batch, seq):
    n = proj.shape[0]
    nc = seq // PREP_T
    return pl.pallas_call(
        _prep_kernel,
        grid=(batch, nc),
        in_specs=[pl.BlockSpec((PREP_T, LANE), lambda b, c: (b * nc + c, C_SM // LANE)),
                  pl.BlockSpec((SUBLANE, LANE), lambda b, c: (0, 0))],
        out_specs=[pl.BlockSpec((PREP_T, LANE), lambda b, c: (b * nc + c, 0)),
                   pl.BlockSpec((1, PREP_ROWS, PREP_T), lambda b, c: (b, 0, c))],
        out_shape=[jax.ShapeDtypeStruct((n, LANE), F32),
                   jax.ShapeDtypeStruct((batch, PREP_ROWS, seq), F32)],
        scratch_shapes=[pltpu.VMEM((SUBLANE, LANE), F32)],
        compiler_params=_cparams(("arbitrary", "arbitrary")),
        name="prep_small",
    )(proj, par)


RET_T = 512


def _ret_gamma():
    return [math.log1p(-(2.0 ** (-5.0 - h))) for h in range(N_HEADS)]


def _retention_kernel(x_ref, cos_ref, sin_ref, dec_ref, zeta_ref, xi_ref, o_ref, state_scr):
    @pl.when(pl.program_id(1) == 0)
    def _():
        state_scr[...] = jnp.zeros_like(state_scr)

    log_gamma = _ret_gamma()
    heads = range(N_HEADS)
    hsl = [slice(h * HEAD_DIM, (h + 1) * HEAD_DIM) for h in heads]
    chunks = [slice(j * RET_CHUNK, (j + 1) * RET_CHUNK) for j in range(RET_T // RET_CHUNK)]
    items = [(rs, h) for rs in chunks for h in heads]

    def part(col, h):
        return slice(col - C_RQ + hsl[h].start, col - C_RQ + hsl[h].stop)

    def rope(x, rs):
        return x * cos_ref[rs, :] + pltpu.roll(x, HEAD_DIM // 2, 1) * sin_ref[rs, :]

    qb = [rope(x_ref[rs, part(C_RQ, h)], rs).astype(BF16) for rs, h in items]
    kr = [rope(x_ref[rs, part(C_RK, h)], rs) * (HEAD_DIM ** -0.5) for rs, h in items]
    kb = [x.astype(BF16) for x in kr]
    vb = [x_ref[rs, part(C_RV, h)].astype(BF16) for rs, h in items]
    inner = [(_dot_nt(q, k) * dec_ref[h]).astype(BF16) for q, k, (_, h) in zip(qb, kb, items)]
    kv = [_dot((k * zeta_ref[h]).T.astype(BF16), v) for k, v, (_, h) in zip(kr, vb, items)]
    o_in = [_dot(a, v) for a, v in zip(inner, vb)]
    st = [state_scr[h] for h in heads]
    for n, (rs, h) in enumerate(items):
        o = o_in[n] + _dot(qb[n], st[h].astype(BF16)) * xi_ref[h]
        st[h] = st[h] * math.exp(log_gamma[h] * RET_CHUNK) + kv[n]
        mu = jnp.mean(o, axis=-1, keepdims=True)
        oc = o - mu
        var = jnp.mean(oc * oc, axis=-1, keepdims=True)
        gate = x_ref[rs, part(C_RG, h)]
        o_ref[rs, hsl[h]] = (_silu(gate) * (oc * lax.rsqrt(var + EPS))).astype(BF16)
    for h in heads:
        state_scr[h] = st[h]


def _retention_tables(seq):
    half = HEAD_DIM // 2
    inv = 1.0 / (ROPE_BASE ** (jnp.arange(half, dtype=F32) / half))
    ang = jnp.arange(seq).astype(F32)[:, None] * inv[None, :]
    cos, sin = jnp.cos(ang), jnp.sin(ang)
    cos_t = jnp.concatenate([cos, cos], axis=-1)
    sin_t = jnp.concatenate([-sin, sin], axis=-1)
    c = RET_CHUNK
    log_gamma = jnp.log1p(-jnp.exp2(-5.0 - jnp.arange(N_HEADS, dtype=F32)))
    n = jnp.arange(c, dtype=F32)
    diff = n[:, None] - n[None, :]
    decay = jnp.where(diff >= 0, jnp.exp(log_gamma[:, None, None] * jnp.maximum(diff, 0.0)), 0.0)
    zeta = jnp.exp(log_gamma[:, None] * (c - 1 - n)[None, :])
    xi = jnp.exp(log_gamma[:, None] * (n + 1)[None, :])
    ones = jnp.ones((1, 1, HEAD_DIM), F32)
    return cos_t, sin_t, decay, zeta[:, :, None] * ones, xi[:, :, None] * ones


def retention(proj, tables, batch, seq):
    n = proj.shape[0]
    t = RET_T
    nc = seq // t
    cos_t, sin_t, decay, zeta, xi = tables
    full3 = pl.BlockSpec((N_HEADS, RET_CHUNK, HEAD_DIM), lambda b, i: (0, 0, 0))
    return pl.pallas_call(
        _retention_kernel,
        grid=(batch, nc),
        in_specs=[pl.BlockSpec((t, 4 * WIDTH), lambda b, i: (b * nc + i, C_RQ // (4 * WIDTH))),
                  pl.BlockSpec((t, HEAD_DIM), lambda b, i: (i, 0)),
                  pl.BlockSpec((t, HEAD_DIM), lambda b, i: (i, 0)),
                  full3, full3, full3],
        out_specs=pl.BlockSpec((t, WIDTH), lambda b, i: (b * nc + i, 0)),
        out_shape=jax.ShapeDtypeStruct((n, WIDTH), BF16),
        scratch_shapes=[pltpu.VMEM((N_HEADS, HEAD_DIM, HEAD_DIM), F32)],
        compiler_params=_cparams(("arbitrary", "arbitrary")),
        name="retention",
    )(proj, cos_t, sin_t, decay, zeta, xi)


def _fox_kernel(qi_ref, ki_ref, q_ref, kv_ref, ctr_ref, o_ref, m_scr, acc_scr, *, t):
    qi = qi_ref[pl.program_id(1)]
    ki = ki_ref[pl.program_id(1)]

    @pl.when(ki == 0)
    def _():
        m_scr[...] = jnp.full_like(m_scr, -jnp.inf)
        acc_scr[...] = jnp.zeros_like(acc_scr)

    def step(masked):
        if masked:
            row = lax.broadcasted_iota(I32, (t, t), 0)
            colm = lax.broadcasted_iota(I32, (t, t), 1)
            keep = row >= colm
        ones = jnp.ones((t, HEAD_DIM), BF16)
        for h in range(N_HEADS):
            sl = slice(h * HEAD_DIM, (h + 1) * HEAD_DIM)
            vsl = slice(WIDTH + h * HEAD_DIM, WIDTH + (h + 1) * HEAD_DIM)
            qb = q_ref[:, sl].astype(BF16)
            kb = kv_ref[:, sl].astype(BF16)
            c_k = ctr_ref[0, L_FF + h:L_FF + h + 1, :] * LOG2E
            s = _dot_nt(qb, kb) * (HEAD_DIM ** -0.5 * LOG2E) - c_k
            if masked:
                s = jnp.where(keep, s, -jnp.inf)
            m_old = m_scr[h]
            m_new = jnp.maximum(m_old, jnp.max(s, axis=-1, keepdims=True))
            alpha = jnp.exp2(m_old - m_new)
            p = jnp.exp2(s - m_new)
            v_aug = jnp.concatenate([kv_ref[:, vsl].astype(BF16), ones], axis=1)
            acc_scr[h] = alpha * acc_scr[h] + _dot(p.astype(BF16), v_aug)
            m_scr[h] = m_new

    @pl.when(ki < qi)
    def _():
        step(False)

    @pl.when(ki == qi)
    def _():
        step(True)
        for h in range(N_HEADS):
            sl = slice(h * HEAD_DIM, (h + 1) * HEAD_DIM)
            acc = acc_scr[h]
            o_ref[:, sl] = (acc[:, :HEAD_DIM] / acc[:, HEAD_DIM:]).astype(BF16)


def fox_attention(proj, prep_tr, batch, seq, *, t=1024):
    n = proj.shape[0]
    nt = seq // t
    pairs = [(qi, ki) for qi in range(nt) for ki in range(qi + 1)]
    qi_arr = jnp.asarray([p[0] for p in pairs], I32)
    ki_arr = jnp.asarray([p[1] for p in pairs], I32)
    qspec = pl.BlockSpec((t, WIDTH), lambda b, s, qi, ki: (b * nt + qi[s], C_FQ // WIDTH))
    kvspec = pl.BlockSpec((t, 2 * WIDTH), lambda b, s, qi, ki: (b * nt + ki[s], C_FK // (2 * WIDTH)))
    return pl.pallas_call(
        functools.partial(_fox_kernel, t=t),
        grid_spec=pltpu.PrefetchScalarGridSpec(
            num_scalar_prefetch=2,
            grid=(batch, len(pairs)),
            in_specs=[qspec, kvspec,
                      pl.BlockSpec((1, PREP_ROWS, t), lambda b, s, qi, ki: (b, 0, ki[s]))],
            out_specs=pl.BlockSpec((t, WIDTH), lambda b, s, qi, ki: (b * nt + qi[s], 0)),
            scratch_shapes=[pltpu.VMEM((N_HEADS, t, 1), F32),
                            pltpu.VMEM((N_HEADS, t, 2 * HEAD_DIM), F32)]),
        out_shape=jax.ShapeDtypeStruct((n, WIDTH), BF16),
        compiler_params=_cparams(("arbitrary", "arbitrary")),
        name="fox_attention",
    )(qi_arr, ki_arr, proj, proj, prep_tr)


def _dsa_proj_kernel(cq_ref, g_ref, wq_ref, wi_ref, q_ref, qi_ref):
    x = cq_ref[...]
    ms = jnp.mean(x * x, axis=-1, keepdims=True)
    cb = (x * lax.rsqrt(ms + EPS) * g_ref[...]).astype(BF16)
    q_ref[...] = _dot(cb, wq_ref[...]).astype(BF16)
    qi_ref[...] = _dot(cb, wi_ref[...]).astype(BF16)


def dsa_proj(proj, cq_norm, w_uq, w_qidx, *, tm=1024):
    n = proj.shape[0]
    r = DSA_Q_RANK
    wi = IDX_HEADS * IDX_DIM
    return pl.pallas_call(
        _dsa_proj_kernel,
        grid=(n // tm,),
        in_specs=[pl.BlockSpec((tm, r), lambda i: (i, C_DCQ // r)),
                  pl.BlockSpec((1, r), lambda i: (0, 0)),
                  pl.BlockSpec((r, WIDTH), lambda i: (0, 0)),
                  pl.BlockSpec((r, wi), lambda i: (0, 0))],
        out_specs=[pl.BlockSpec((tm, WIDTH), lambda i: (i, 0)),
                   pl.BlockSpec((tm, wi), lambda i: (i, 0))],
        out_shape=[jax.ShapeDtypeStruct((n, WIDTH), BF16),
                   jax.ShapeDtypeStruct((n, wi), BF16)],
        compiler_params=_cparams(("arbitrary",)),
        name="dsa_proj",
    )(proj, cq_norm.reshape(1, r), w_uq, w_qidx)


DSA_QB = 256
DSA_KC = 512
DSA_SCORE_MID_STEPS = 20
DSA_FEW_KEYS = 4
DSA_HALVE_FIXED = 12
DSA_WALK_FIXED = 3


def _score_to_key(s):
    b = pltpu.bitcast(s, I32)
    return b ^ ((b >> 31) & INT_MAX)


def _key_to_score(k):
    return pltpu.bitcast(k ^ ((k >> 31) & INT_MAX), F32)


def _t5_bucket(rel):
    max_exact = REL_BUCKETS // 2
    relf = jnp.maximum(rel, max_exact).astype(F32)
    large = max_exact + (jnp.log(relf / max_exact) / math.log(REL_MAX_DIST / max_exact)
                         * (REL_BUCKETS - max_exact)).astype(I32)
    large = jnp.minimum(large, REL_BUCKETS - 1)
    return jnp.where(rel < max_exact, rel, large)


def _dsa_kernel(rb_ref, q_ref, qi_ref, tok_ref, k_ref, v_ref, ka_ref, kb_ref, o_ref,
                key_scr, lg_scr, band_scr, kb16_scr, vt_scr, ka16_scr, kb16i_scr, *, seq, topk):
    qb_idx = pl.program_id(1)
    t0 = qb_idx * DSA_QB
    n_kc = (t0 + DSA_QB - 1) // DSA_KC + 1
    row_vec = (1, DSA_QB)

    @pl.when(qb_idx == 0)
    def _():
        kb16_scr[...] = k_ref[...].astype(BF16)
        ka16_scr[...] = ka_ref[...].astype(BF16)
        kb16i_scr[...] = kb_ref[...].astype(BF16)
        for c in range(seq // DSA_KC):
            cs = slice(c * DSA_KC, (c + 1) * DSA_KC)
            vt_scr[0:HEAD_DIM, cs] = v_ref[cs, :].T.astype(BF16)
        vt_scr[HEAD_DIM:, :] = jnp.ones((HEAD_DIM, seq), BF16)

    @pl.when((pl.program_id(0) == 0) & (qb_idx == 0))
    def _():
        j_ = lax.broadcasted_iota(I32, (2 * DSA_QB, DSA_QB), 0)
        i_ = lax.broadcasted_iota(I32, (2 * DSA_QB, DSA_QB), 1)
        rel = i_ + DSA_QB - j_
        bucket = _t5_bucket(rel)
        for h in range(N_HEADS):
            far = rb_ref[REL_BUCKETS - 1, h]
            band = jnp.zeros((2 * DSA_QB, DSA_QB), F32)
            for bk in range(REL_BUCKETS - 1):
                band = jnp.where(bucket == bk, (rb_ref[bk, h] - far) * LOG2E, band)
            band_scr[h] = jnp.where(rel >= 0, band, 0.0)

    w_t = tok_ref[...].T
    key_s = lax.broadcasted_iota(I32, (DSA_KC, DSA_QB), 0)
    row_t = t0 + lax.broadcasted_iota(I32, (DSA_KC, DSA_QB), 1)

    def score_chunk(c, carry):
        kmax, kmin = carry
        ks = pl.ds(pl.multiple_of(c * DSA_KC, DSA_KC), DSA_KC)
        ka = ka16_scr[ks, :]
        kb = kb16i_scr[ks, :]
        acc = jnp.zeros((DSA_KC, DSA_QB), F32)
        for p in range(IDX_HEADS // 2):
            qp = qi_ref[:, p * LANE:(p + 1) * LANE]
            acc = acc + jnp.maximum(_dot_nt(ka, qp), 0.0) * w_t[2 * p:2 * p + 1, :]
            acc = acc + jnp.maximum(_dot_nt(kb, qp), 0.0) * w_t[2 * p + 1:2 * p + 2, :]
        key = _score_to_key(acc)
        valid = (c * DSA_KC + key_s) <= row_t
        key_scr[ks, :] = jnp.where(valid, key, INT_MIN)
        kmax = jnp.maximum(kmax, jnp.max(jnp.where(valid, key, INT_MIN), axis=0, keepdims=True))
        kmin = jnp.minimum(kmin, jnp.min(jnp.where(valid, key, INT_MAX), axis=0, keepdims=True))
        return kmax, kmin

    kmax, kmin = lax.fori_loop(0, n_kc, score_chunk, (jnp.full(row_vec, INT_MIN, I32),
                                                     jnp.full(row_vec, INT_MAX, I32)))

    def scan_keys(cand, with_below):
        def body(c, carry):
            cnt, below = carry
            ks = pl.ds(pl.multiple_of(c * DSA_KC, DSA_KC), DSA_KC)
            keys = key_scr[ks, :]
            ge = keys >= cand
            ones = ge.astype(I32)
            low = jnp.where(ge, INT_MIN, keys)
            for u in range(DSA_KC // SUBLANE):
                us = slice(u * SUBLANE, (u + 1) * SUBLANE)
                cnt = cnt + ones[us, :]
                if with_below:
                    below = jnp.maximum(below, low[us, :])
            return cnt, below
        cnt, below = lax.fori_loop(0, n_kc, body, (jnp.zeros((SUBLANE, DSA_QB), I32),
                                                   jnp.full((SUBLANE, DSA_QB), INT_MIN, I32)))
        cnt = jnp.sum(cnt, axis=0, keepdims=True)
        if with_below:
            return cnt, jnp.max(below, axis=0, keepdims=True)
        return cnt

    def open_rows(lo, hi, c_lo):
        return (c_lo > topk) & (hi - 1 > lo)

    def any_row(flag):
        return jnp.max(jnp.where(flag, 1, 0))

    def update(cand, cnt, lo, hi, c_lo, c_hi):
        ge = cnt >= topk
        return (jnp.where(ge, cand, lo), jnp.where(ge, hi, cand),
                jnp.where(ge, cnt, c_lo), jnp.where(ge, c_hi, cnt))

    def crowded(lo, hi, c_lo, c_hi):
        return any_row(open_rows(lo, hi, c_lo) & (c_lo - c_hi > DSA_FEW_KEYS))

    def halve_step(it, lo, hi, c_lo, c_hi):
        key_mid = (lo >> 1) + (hi >> 1) + (lo & hi & 1)
        score_mid = _score_to_key(0.5 * _key_to_score(lo) + 0.5 * _key_to_score(hi - 1))
        cand = jnp.where(it < DSA_SCORE_MID_STEPS, score_mid, key_mid)
        cand = jnp.minimum(jnp.maximum(cand, lo + 1), hi - 1)
        cand = jnp.where(hi - 1 > lo, cand, lo)
        return update(cand, scan_keys(cand, False), lo, hi, c_lo, c_hi)

    def halve_body(st):
        it, _, lo, hi, c_lo, c_hi = st
        go = crowded(lo, hi, c_lo, c_hi)
        return (it + 1, go) + halve_step(it, lo, hi, c_lo, c_hi)

    def walk_step(lo, hi, c_lo, c_hi, nxt):
        is_open = open_rows(lo, hi, c_lo)
        cand = jnp.where(is_open, nxt, lo)
        cnt, below = scan_keys(cand, True)
        ge = cnt >= topk
        hi = jnp.where(is_open, jnp.where(ge, cand + 1, cand), hi)
        c_hi = jnp.where(is_open & jnp.logical_not(ge), cnt, c_hi)
        lo = jnp.where(is_open & ge, cand, lo)
        c_lo = jnp.where(is_open & ge, cnt, c_lo)
        nxt = jnp.where(ge, nxt, below)
        return lo, hi, c_lo, c_hi, nxt

    def walk_body(st):
        go = any_row(open_rows(st[1], st[2], st[3]))
        return (go,) + walk_step(*st[1:])

    n_valid = jnp.minimum(t0 + lax.broadcasted_iota(I32, row_vec, 1) + 1, seq)
    st = (kmin, kmax + 1, n_valid, jnp.zeros(row_vec, I32))
    st = lax.fori_loop(0, DSA_HALVE_FIXED, lambda it, s: halve_step(it, *s), st)
    st = lax.while_loop(lambda s: s[1] > 0, halve_body,
                        (jnp.int32(DSA_HALVE_FIXED), crowded(*st)) + st)[2:]
    _, nxt0 = scan_keys(st[1], True)
    st = lax.fori_loop(0, DSA_WALK_FIXED, lambda it, s: walk_step(*s), st + (nxt0,))
    _, thr, hi, n_ge, n_gt, _ = lax.while_loop(
        lambda s: s[0] > 0, walk_body, (any_row(open_rows(st[0], st[1], st[2])),) + st)

    tied = n_ge > topk
    has_tie = jnp.max(jnp.where(tied, 1, 0)) > 0

    @pl.when(has_tie)
    def _():
        room = (topk - n_gt).astype(F32)
        ii = lax.broadcasted_iota(I32, (LANE, LANE), 0)
        jj = lax.broadcasted_iota(I32, (LANE, LANE), 1)
        lower = (ii >= jj).astype(BF16)

        def body(c, seen):
            ks = pl.ds(pl.multiple_of(c * LANE, LANE), LANE)
            kk = key_scr[ks, :]
            eq = kk == thr
            rank = seen + _dot(lower, eq.astype(BF16))
            drop = eq & (rank > room) & tied
            key_scr[ks, :] = jnp.where(drop, INT_MIN, kk)
            return seen + jnp.sum(eq.astype(F32), axis=0, keepdims=True)
        lax.fori_loop(0, n_kc * (DSA_KC // LANE), body, jnp.zeros(row_vec, F32))

    def mask_chunk(c, carry):
        ks = pl.ds(pl.multiple_of(c * DSA_KC, DSA_KC), DSA_KC)
        sel = jnp.where(key_scr[ks, :] >= thr, 0.0, -jnp.inf).astype(F32)
        key_scr[ks, :] = pltpu.bitcast(sel, I32)
        return carry
    lax.fori_loop(0, n_kc, mask_chunk, 0)

    heads = range(N_HEADS)
    hsl = [slice(h * HEAD_DIM, (h + 1) * HEAD_DIM) for h in heads]

    def logit_chunk(c, ms):
        ks = pl.ds(pl.multiple_of(c * DSA_KC, DSA_KC), DSA_KC)
        k_c = kb16_scr[ks, :]
        sel = pltpu.bitcast(key_scr[ks, :], F32)
        out = []
        for h in heads:
            s = _dot_nt(k_c, q_ref[:, hsl[h]]) * (HEAD_DIM ** -0.5 * LOG2E) + sel
            lg_scr[h, ks, :] = s
            out.append(jnp.maximum(ms[h], jnp.max(s, axis=0, keepdims=True)))
        return tuple(out)
    ms = lax.fori_loop(0, n_kc, logit_chunk,
                       tuple(jnp.full(row_vec, -jnp.inf, F32) for _ in heads))

    band_off = pl.multiple_of(jnp.maximum(qb_idx - 1, 0) * DSA_QB, DSA_QB)
    ws = pl.ds(band_off, 2 * DSA_QB)
    ms = list(ms)
    for h in heads:
        band_h = band_scr[h]
        band_first = jnp.concatenate([band_h[DSA_QB:, :], jnp.zeros((DSA_QB, DSA_QB), F32)], axis=0)
        win = lg_scr[h, ws, :] + jnp.where(qb_idx == 0, band_first, band_h)
        lg_scr[h, ws, :] = win
        ms[h] = jnp.maximum(ms[h], jnp.max(win, axis=0, keepdims=True))

    def pv_chunk(c, accs):
        ks = pl.ds(pl.multiple_of(c * DSA_KC, DSA_KC), DSA_KC)
        vt_c = vt_scr[:, ks]
        return tuple(accs[h] + _dot(vt_c, jnp.exp2(lg_scr[h, ks, :] - ms[h]).astype(BF16))
                     for h in heads)
    accs = lax.fori_loop(0, n_kc, pv_chunk,
                         tuple(jnp.zeros((2 * HEAD_DIM, DSA_QB), F32) for _ in heads))
    for h in heads:
        o_ref[:, hsl[h]] = (accs[h][:HEAD_DIM, :] / accs[h][HEAD_DIM:, :]).T.astype(BF16)


def dsa_attention(proj, q, q_idx, prep_tok, rel_bias, batch, seq):
    n = proj.shape[0]
    nq = seq // DSA_QB
    topk = min(TOPK_MAX, seq // 4)
    wi = IDX_HEADS * IDX_DIM
    rowblk = lambda w, cb: pl.BlockSpec((DSA_QB, w), lambda b, i: (b * nq + i, cb))
    seqblk = lambda off: pl.BlockSpec((seq, LANE), lambda b, i: (b, off // LANE))
    return pl.pallas_call(
        functools.partial(_dsa_kernel, seq=seq, topk=topk),
        grid=(batch, nq),
        in_specs=[pl.BlockSpec(memory_space=pltpu.SMEM),
                  rowblk(WIDTH, 0), rowblk(wi, 0), rowblk(LANE, 0),
                  seqblk(C_DK), seqblk(C_DV), seqblk(C_KA), seqblk(C_KB)],
        out_specs=pl.BlockSpec((DSA_QB, WIDTH), lambda b, i: (b * nq + i, 0)),
        out_shape=jax.ShapeDtypeStruct((n, WIDTH), BF16),
        scratch_shapes=[pltpu.VMEM((seq, DSA_QB), I32),
                        pltpu.VMEM((N_HEADS, seq, DSA_QB), F32),
                        pltpu.VMEM((N_HEADS, 2 * DSA_QB, DSA_QB), F32),
                        pltpu.VMEM((seq, LANE), BF16),
                        pltpu.VMEM((2 * HEAD_DIM, seq), BF16),
                        pltpu.VMEM((seq, LANE), BF16),
                        pltpu.VMEM((seq, LANE), BF16)],
        compiler_params=_cparams(("arbitrary", "arbitrary")),
        name="dsa_attention",
    )(rel_bias, q, q_idx, prep_tok, proj, proj, proj, proj)


GDN_T = 256
GDN_GROUP = 2
GDN_HALO = 8


def _gdn_kernel(x_ref, cw_ref, ng_ref, tok_ref, tr_ref, o_ref,
                xq_scr, xk_scr, xv_scr, state_scr):
    first = pl.program_id(1) == 0

    @pl.when(first)
    def _():
        state_scr[...] = jnp.zeros_like(state_scr)
        for scr in (xq_scr, xk_scr, xv_scr):
            scr[0:GDN_HALO, :] = jnp.zeros((GDN_HALO, WIDTH), F32)

    def conv(col, scr, w_off):
        scr[GDN_HALO:, :] = x_ref[:, col - C_GQ:col - C_GQ + WIDTH]
        y = jnp.zeros((GDN_T, WIDTH), F32)
        for i in range(GDN_CONV):
            st = GDN_HALO - (GDN_CONV - 1) + i
            y = y + scr[st:st + GDN_T, :] * cw_ref[i:i + 1, w_off:w_off + WIDTH]
        scr[0:GDN_HALO, :] = scr[GDN_T:GDN_T + GDN_HALO, :]
        return _silu(y)

    qc = conv(C_GQ, xq_scr, 0)
    kc = conv(C_GK, xk_scr, WIDTH)
    vc = conv(C_GV, xv_scr, 2 * WIDTH)
    z_off = C_GZ - C_GQ
    tok = tok_ref[...]
    c = GDN_CHUNK
    heads = range(N_HEADS)
    hsl = [slice(h * HEAD_DIM, (h + 1) * HEAD_DIM) for h in heads]

    def l2norm_heads(x, scale):
        return jnp.concatenate(
            [x[:, s] * (lax.rsqrt(jnp.sum(x[:, s] * x[:, s], axis=-1, keepdims=True) + EPS) * scale)
             for s in hsl], axis=1)

    qf = l2norm_heads(qc, HEAD_DIM ** -0.5)
    kf = l2norm_heads(kc, 1.0)

    grp = GDN_GROUP
    gw = grp * HEAD_DIM
    nb = grp * c
    ri = lax.broadcasted_iota(I32, (nb, nb), 0)
    ci = lax.broadcasted_iota(I32, (nb, nb), 1)
    tril = ((ri // c) == (ci // c)) & (ri >= ci)
    eye = (ri == ci).astype(F32)
    pair_masks = []
    for lg in range(c.bit_length() - 1):
        pair_masks.append(((ri >> (lg + 1)) == (ci >> (lg + 1)))
                          & (((ri >> lg) & 1) == 1) & (((ci >> lg) & 1) == 0))
    lane_head = lax.broadcasted_iota(I32, (c, gw), 1) // HEAD_DIM
    row_head = lax.broadcasted_iota(I32, (nb, HEAD_DIM), 0) // c

    def spread(x):
        return jnp.concatenate([jnp.where(lane_head == u, x, 0.0) for u in range(grp)], axis=0)

    def stack(x):
        return jnp.concatenate([x[:, hsl[u]] for u in range(grp)], axis=0)

    def spread_lanes(x):
        return jnp.concatenate([jnp.where(row_head == u, x, 0.0) for u in range(grp)], axis=1)

    def split(x):
        hi = x.astype(BF16)
        return hi, (x - hi.astype(F32)).astype(BF16)

    def dot_split(a, b):
        return _dot(a[0], b[0]) + (_dot(a[0], b[1]) + _dot(a[1], b[0]))

    items = [(j, gi) for j in range(GDN_T // c) for gi in range(N_HEADS // grp)]
    pre = []
    for j, gi in items:
        rs = slice(j * c, (j + 1) * c)
        last = slice((j + 1) * c - 1, (j + 1) * c)
        gh = [gi * grp + u for u in range(grp)]
        gsl = slice(gi * gw, (gi + 1) * gw)
        qj, kj, vj = qf[rs, gsl], kf[rs, gsl], vc[rs, gsl]
        b_col = jnp.concatenate([tok[rs, L_GB + h:L_GB + h + 1] for h in gh], axis=0)
        g_col = jnp.concatenate([tok[rs, L_GA + h:L_GA + h + 1] for h in gh], axis=0)
        g_row = jnp.concatenate([tr_ref[0, L_GA + h:L_GA + h + 1, rs] for h in gh], axis=1)
        g_last = [tok[last, L_GA + h:L_GA + h + 1] for h in gh]
        g_last_col = jnp.concatenate([jnp.broadcast_to(g, (c, 1)) for g in g_last], axis=0)
        k_sp = spread(kj)
        q_sp = spread(qj)
        k_sp16 = k_sp.astype(BF16)
        decay = jnp.exp(jnp.where(tril, g_col - g_row, -jnp.inf))
        eg = jnp.exp(g_col)
        pre.append(dict(
            gh=gh, gsl=gsl,
            l_mat=b_col * _dot_nt(k_sp16, k_sp16) * decay,
            rhs=jnp.concatenate([stack(vj) * b_col, stack(kj) * (b_col * eg)], axis=1),
            qk=_dot_nt(q_sp.astype(BF16), k_sp16) * decay,
            q_dec=q_sp * eg,
            k_dec=k_sp * jnp.exp(g_last_col - g_col),
            e_last=jnp.concatenate([jnp.broadcast_to(jnp.exp(g), (HEAD_DIM, 1)) for g in g_last], axis=0)))

    t_inv = [eye - jnp.where(pair_masks[0], p["l_mat"], 0.0) for p in pre]
    for pm in pair_masks[1:]:
        t_s = [split(t) for t in t_inv]
        m_t = [dot_split(split(jnp.where(pm, p["l_mat"], 0.0)), ts) for p, ts in zip(pre, t_s)]
        t_inv = [t - dot_split(ts, split(m)) for t, ts, m in zip(t_inv, t_s, m_t)]
    sols = [dot_split(split(t), split(p["rhs"])) for t, p in zip(t_inv, pre)]

    outs = [[] for _ in heads]
    for p, sol in zip(pre, sols):
        u0 = sol[:, :HEAD_DIM]
        kcum = sol[:, HEAD_DIM:]
        st = state_scr[p["gsl"], :]
        stb = st.astype(BF16)
        v_new = u0 - _dot(spread_lanes(kcum).astype(BF16), stb)
        v_new_b = v_new.astype(BF16)
        o_st = _dot(p["q_dec"].astype(BF16), stb) + _dot(p["qk"].astype(BF16), v_new_b)
        state_scr[p["gsl"], :] = st * p["e_last"] + _dot(p["k_dec"].T.astype(BF16), v_new_b)
        for u, h in enumerate(p["gh"]):
            outs[h].append(o_st[u * c:(u + 1) * c, :])

    for h in heads:
        o = jnp.concatenate(outs[h], axis=0)
        ms = jnp.mean(o * o, axis=-1, keepdims=True)
        on = o * lax.rsqrt(ms + EPS) * ng_ref[...]
        z = x_ref[:, z_off + hsl[h].start:z_off + hsl[h].stop]
        o_ref[:, hsl[h]] = (on * _silu(z)).astype(BF16)


def gated_deltanet(proj, prep_tok, prep_tr, conv_w, norm_g, batch, seq):
    n = proj.shape[0]
    t = GDN_T
    nt = seq // t
    return pl.pallas_call(
        _gdn_kernel,
        grid=(batch, nt),
        in_specs=[pl.BlockSpec((t, 4 * WIDTH), lambda b, i: (b * nt + i, C_GQ // (4 * WIDTH))),
                  pl.BlockSpec((GDN_CONV, 3 * WIDTH), lambda b, i: (0, 0)),
                  pl.BlockSpec((1, HEAD_DIM), lambda b, i: (0, 0)),
                  pl.BlockSpec((t, LANE), lambda b, i: (b * nt + i, 0)),
                  pl.BlockSpec((1, PREP_ROWS, t), lambda b, i: (b, 0, i))],
        out_specs=pl.BlockSpec((t, WIDTH), lambda b, i: (b * nt + i, 0)),
        out_shape=jax.ShapeDtypeStruct((n, WIDTH), BF16),
        scratch_shapes=[pltpu.VMEM((t + GDN_HALO, WIDTH), F32),
                        pltpu.VMEM((t + GDN_HALO, WIDTH), F32),
                        pltpu.VMEM((t + GDN_HALO, WIDTH), F32),
                        pltpu.VMEM((N_HEADS * HEAD_DIM, HEAD_DIM), F32)],
        compiler_params=_cparams(("arbitrary", "arbitrary")),
        name="gated_deltanet",
    )(proj, conv_w, norm_g.reshape(1, HEAD_DIM), prep_tok, prep_tr)


def _merge_kernel(h_ref, b0_ref, b1_ref, b2_ref, b3_ref, g0_ref, g1_ref, g2_ref, g3_ref,
                  wb_ref, o_ref):
    h = h_ref[...]
    acc = None
    for n, (b_ref, g_ref) in enumerate(zip((b0_ref, b1_ref, b2_ref, b3_ref),
                                           (g0_ref, g1_ref, g2_ref, g3_ref))):
        gate = jax.nn.sigmoid(_dot(h, g_ref[...]))
        term = gate * _dot(b_ref[...], wb_ref[n])
        acc = term if acc is None else acc + term
    o_ref[...] = acc.astype(BF16)


def merge_branches(h, branches, w_gate, w_branch, *, tm=1024, tn=512):
    _, n, d = h.shape
    nj = d // tn
    bspec = pl.BlockSpec((tm, WIDTH), lambda j, i: (i, 0))
    gspec = lambda k: pl.BlockSpec((d, tn), lambda j, i: (0, k * nj + j))
    return pl.pallas_call(
        _merge_kernel,
        grid=(nj, n // tm),
        in_specs=[pl.BlockSpec((None, tm, d), lambda j, i: (0, i, 0)),
                  bspec, bspec, bspec, bspec,
                  gspec(0), gspec(1), gspec(2), gspec(3),
                  pl.BlockSpec((N_BRANCH, WIDTH, tn), lambda j, i: (0, 0, j))],
        out_specs=pl.BlockSpec((tm, tn), lambda j, i: (i, j)),
        out_shape=jax.ShapeDtypeStruct((n, d), BF16),
        compiler_params=_cparams(("arbitrary", "arbitrary")),
        name="merge_branches",
    )(h, *branches, w_gate, w_gate, w_gate, w_gate, w_branch)


def _resid_mm_kernel(a_ref, w_ref, x_ref, o_ref):
    o_ref[...] = x_ref[...] + _dot(a_ref[...], w_ref[...])


def resid_matmul(a, w, x, *, tm=512, tn=1024, name="resid_matmul"):
    n, k = a.shape
    d = w.shape[1]
    return pl.pallas_call(
        _resid_mm_kernel,
        grid=(d // tn, n // tm),
        in_specs=[pl.BlockSpec((tm, k), lambda j, i: (i, 0)),
                  pl.BlockSpec((k, tn), lambda j, i: (0, j)),
                  pl.BlockSpec((tm, tn), lambda j, i: (i, j))],
        out_specs=pl.BlockSpec((tm, tn), lambda j, i: (i, j)),
        out_shape=jax.ShapeDtypeStruct((n, d), F32),
        compiler_params=_cparams(("arbitrary", "arbitrary")),
        name=name,
    )(a, w, x)


def _resid_mm_f32w_kernel(a_ref, w_ref, x_ref, o_ref, wb_scr):
    @pl.when(pl.program_id(0) == 0)
    def _():
        wb_scr[...] = w_ref[...].astype(BF16)

    o_ref[...] = x_ref[...] + _dot(a_ref[...], wb_scr[...])


def resid_matmul_f32w(a, w, layer, x, *, tm=512, name="resid_matmul_f32w"):
    n, k = a.shape
    d = w.shape[2]
    return pl.pallas_call(
        _resid_mm_f32w_kernel,
        grid=(n // tm,),
        in_specs=[pl.BlockSpec((tm, k), lambda i: (i, 0)),
                  pl.BlockSpec((None, k, d), lambda i: (layer, 0, 0), pipeline_mode=pl.Buffered(1)),
                  pl.BlockSpec((tm, d), lambda i: (i, 0))],
        out_specs=pl.BlockSpec((tm, d), lambda i: (i, 0)),
        out_shape=jax.ShapeDtypeStruct((n, d), F32),
        scratch_shapes=[pltpu.VMEM((k, d), BF16)],
        compiler_params=_cparams(("arbitrary",)),
        name=name,
    )(a, w, x)


FFN_HALO = 8
FFN_SUB = 512


def _ffn1_kernel(x_ref, g_ref, wg_ref, wu_ref, cw_ref, cb_ref, o_ref, h_scr, gt_scr, halo_scr,
                 *, tm, tiles_per_seq):
    i = pl.program_id(1)

    @pl.when(i == 0)
    def _():
        halo_scr[...] = jnp.zeros_like(halo_scr)

    def body(r, carry):
        rows = pl.ds(pl.multiple_of(r * NORM_ROWS, NORM_ROWS), NORM_ROWS)
        h_scr[rows, :] = _rmsnorm_rows(x_ref, g_ref, rows).astype(BF16)
        return carry
    lax.fori_loop(0, tm // NORM_ROWS, body, 0)

    h = h_scr[...]
    seq_start = (i % tiles_per_seq) == 0
    tn = o_ref.shape[1]
    for off in range(0, tn, FFN_SUB):
        cs = slice(off, min(off + FFN_SUB, tn))
        g = _dot(h, wg_ref[:, cs])
        gt_scr[FFN_HALO:, cs] = g
        gt_scr[0:FFN_HALO, cs] = jnp.where(seq_start, 0.0, halo_scr[:, cs])
        halo_scr[:, cs] = g[tm - FFN_HALO:, :]
        y = cb_ref[:, cs] + g * cw_ref[FFN_CONV - 1:FFN_CONV, cs]
        for t in range(FFN_CONV - 1):
            st = FFN_HALO - (FFN_CONV - 1) + t
            y = y + gt_scr[st:st + tm, cs] * cw_ref[t:t + 1, cs]
        o_ref[:, cs] = (_silu(y) * _dot(h, wu_ref[:, cs])).astype(BF16)


def conv_ffn_up(x, gain, w_gate, w_up, conv_w, conv_b, seq, *, tm=512, col_parts=2):
    n, d = x.shape
    f = w_gate.shape[1]
    tn = f // col_parts
    wspec = pl.BlockSpec((d, tn), lambda j, i: (0, j), pipeline_mode=pl.Buffered(1))
    return pl.pallas_call(
        functools.partial(_ffn1_kernel, tm=tm, tiles_per_seq=seq // tm),
        grid=(col_parts, n // tm),
        in_specs=[pl.BlockSpec((tm, d), lambda j, i: (i, 0)),
                  pl.BlockSpec((1, d), lambda j, i: (0, 0)),
                  wspec, wspec,
                  pl.BlockSpec((FFN_CONV, tn), lambda j, i: (0, j)),
                  pl.BlockSpec((1, tn), lambda j, i: (0, j))],
        out_specs=pl.BlockSpec((tm, tn), lambda j, i: (i, j)),
        out_shape=jax.ShapeDtypeStruct((n, f), BF16),
        scratch_shapes=[pltpu.VMEM((tm, d), BF16),
                        pltpu.VMEM((tm + FFN_HALO, tn), F32),
                        pltpu.VMEM((FFN_HALO, tn), F32)],
        compiler_params=_cparams(("arbitrary", "arbitrary")),
        name="conv_ffn_up",
    )(x, gain.reshape(1, d), w_gate, w_up, conv_w, conv_b.reshape(1, f))


IN_SIZES = (WIDTH, WIDTH, WIDTH, WIDTH,
            DSA_Q_RANK, HEAD_DIM, HEAD_DIM, IDX_DIM, IDX_HEADS,
            WIDTH, WIDTH, WIDTH, N_HEADS,
            WIDTH, WIDTH, WIDTH, WIDTH, N_HEADS, N_HEADS)
IN_NAMES = ("r_q", "r_k", "r_v", "r_g", "d_cq", "d_k", "d_v", "i_k", "i_w",
            "f_q", "f_k", "f_v", "f_f", "g_q", "g_k", "g_v", "g_z", "g_b", "g_a")
IN_PLAN = (("r_q", C_RQ), ("r_k", C_RK), ("r_v", C_RV), ("r_g", C_RG),
           ("f_q", C_FQ), ("f_k", C_FK), ("f_v", C_FV),
           ("g_q", C_GQ), ("g_k", C_GK), ("g_v", C_GV), ("g_z", C_GZ),
           ("d_k", C_DK), ("d_cq", C_DCQ), ("d_v", C_DV),
           ("i_k", C_KA), ("i_k", C_KB + IDX_DIM),
           ("i_w", C_SM + L_IW), ("f_f", C_SM + L_FF), ("g_b", C_SM + L_GB), ("g_a", C_SM + L_GA))


def _prep_w_in_kernel(w_ref, m_ref, g_ref):
    src = {}
    off = 0
    for name, size in zip(IN_NAMES, IN_SIZES):
        src[name] = (off, size)
        off += size
    m_ref[...] = jnp.zeros_like(m_ref)
    for name, dst in IN_PLAN:
        so, w = src[name]
        m_ref[:, dst:dst + w] = w_ref[:, so:so + w].astype(BF16)
    g_ref[...] = w_ref[:, off:off + g_ref.shape[1]].astype(BF16)


def prep_w_in(w_in, layer, *, tr=256):
    _, d, c = w_in.shape
    return pl.pallas_call(
        _prep_w_in_kernel,
        grid=(d // tr,),
        in_specs=[pl.BlockSpec((None, tr, c), lambda i: (layer, i, 0))],
        out_specs=[pl.BlockSpec((tr, C_TOT), lambda i: (i, 0)),
                   pl.BlockSpec((tr, N_BRANCH * d), lambda i: (i, 0))],
        out_shape=[jax.ShapeDtypeStruct((d, C_TOT), BF16),
                   jax.ShapeDtypeStruct((d, N_BRANCH * d), BF16)],
        compiler_params=_cparams(("arbitrary",)),
        name="prep_w_in",
    )(w_in)


def _cast_kernel(w_ref, o_ref):
    o_ref[...] = w_ref[...].astype(BF16)


CAST_BLOCK_BYTES = 12 * 1024 * 1024


def cast_layer(w, layer):
    _, r, c = w.shape
    tr = r
    while tr * c * 4 > CAST_BLOCK_BYTES and tr % 16 == 0:
        tr //= 2
    return pl.pallas_call(
        _cast_kernel,
        grid=(r // tr,),
        in_specs=[pl.BlockSpec((None, tr, c), lambda i: (layer, i, 0))],
        out_specs=pl.BlockSpec((tr, c), lambda i: (i, 0)),
        out_shape=jax.ShapeDtypeStruct((r, c), BF16),
        compiler_params=_cparams(("arbitrary",)),
        name="cast_bf16",
    )(w)


def cast_branch(w_branch, layer):
    _, nbr, r, c = w_branch.shape
    return pl.pallas_call(
        _cast_kernel,
        grid=(nbr,),
        in_specs=[pl.BlockSpec((None, None, r, c), lambda i: (layer, i, 0, 0))],
        out_specs=pl.BlockSpec((None, r, c), lambda i: (i, 0, 0)),
        out_shape=jax.ShapeDtypeStruct((nbr, r, c), BF16),
        compiler_params=_cparams(("arbitrary",)),
        name="cast_branch",
    )(w_branch)


def kernel(x, norm_mix, w_in, dsa_cq_norm, dsa_w_uq, dsa_w_qidx, fox_f_bias, gdn_conv, gdn_a_log,
           gdn_dt_bias, gdn_norm, w_branch, w_out, rel_bias, norm_ffn, ffn_w_gate, ffn_w_up,
           ffn_conv, ffn_conv_b, ffn_w_down, final_norm):
    batch, seq, d = x.shape
    depth = w_in.shape[0]
    xf = x.reshape(batch * seq, d)
    ret_tables = _retention_tables(seq)
    for l in range(depth):
        w_main, w_gate = prep_w_in(w_in, l)
        proj, h = norm_proj(xf, norm_mix[l], w_main)
        par = jnp.zeros((SUBLANE, LANE), F32)
        par = par.at[0, L_FF:L_FF + N_HEADS].set(fox_f_bias[l])
        par = par.at[0, L_GA:L_GA + N_HEADS].set(gdn_dt_bias[l])
        par = par.at[1, L_GA:L_GA + N_HEADS].set(gdn_a_log[l])
        prep_tok, prep_tr = prep_small(proj, par, batch, seq)
        o_ret = retention(proj, ret_tables, batch, seq)
        q_dsa, q_idx = dsa_proj(proj, dsa_cq_norm[l], cast_layer(dsa_w_uq, l), cast_layer(dsa_w_qidx, l))
        o_dsa = dsa_attention(proj, q_dsa, q_idx, prep_tok, rel_bias, batch, seq)
        o_fox = fox_attention(proj, prep_tr, batch, seq)
        o_gdn = gated_deltanet(proj, prep_tok, prep_tr, gdn_conv[l], gdn_norm[l], batch, seq)
        merged = merge_branches(h, (o_ret, o_dsa, o_fox, o_gdn), w_gate, cast_branch(w_branch, l))
        xf = resid_matmul_f32w(merged, w_out, l, xf, name="out_proj")
        act = conv_ffn_up(xf, norm_ffn[l], cast_layer(ffn_w_gate, l), cast_layer(ffn_w_up, l),
                          ffn_conv[l], ffn_conv_b[l], seq)
        xf = resid_matmul(act, cast_layer(ffn_w_down, l), xf, name="ffn_down")
    return rmsnorm(xf, final_norm).reshape(batch, seq, d)
```
